```python
import jax, jax.numpy as jnp
from jax import lax
import numpy as np

D_MODEL = 1024
BATCH = 8
SEQ = 4096
DEPTH = 4

CHUNK = 64
N_MEM = 256
EPS = 1e-6
HG_HEADS = 4
HG_DK = 128
HG_DV = 128
HG_W = HG_HEADS * HG_DK
CV_W = 512
CV_KERNEL = 31
POOL_WINDOWS = (2, 4, 8, 16)
POOL_GROUPS = 4
POOL_GW = 128
POOL_W = POOL_GROUPS * POOL_GW
LRU_W = 512
LRU_HEADS = 8
LRU_HD = LRU_W // LRU_HEADS
LRU_CONV = 4
LRU_C = 8.0
N_BRANCH = 4
BRANCH_W = 512
IN_SIZES = (HG_W, HG_W, HG_W, HG_W, 2 * CV_W, POOL_W, LRU_W, LRU_W, N_BRANCH * D_MODEL)
IN_W = HG_W * 4 + 2 * CV_W + POOL_W + 2 * LRU_W + N_BRANCH * D_MODEL
XA_HEADS = 4
XA_HD = D_MODEL // XA_HEADS
D_FF = 4 * D_MODEL

kernel_name = 'hybrid_chunk_causal_streaming_encoder'


def rmsnorm(x, w):
    xf = x.astype(jnp.float32)
    y = xf * lax.rsqrt(jnp.mean(xf * xf, axis=-1, keepdims=True) + EPS) * w.astype(jnp.float32)
    return y.astype(x.dtype)


def layernorm(x, w, b):
    xf = x.astype(jnp.float32)
    mu = jnp.mean(xf, axis=-1, keepdims=True)
    var = jnp.mean(jnp.square(xf - mu), axis=-1, keepdims=True)
    return ((xf - mu) * lax.rsqrt(var + EPS) * w.astype(jnp.float32) + b.astype(jnp.float32)).astype(x.dtype)


def causal_depthwise_conv(z, w, b):
    k = w.shape[0]
    out = lax.conv_general_dilated(z, w[:, None, :].astype(z.dtype), window_strides=(1,),
                                   padding=[(k - 1, 0)], dimension_numbers=('NWC', 'WIO', 'NWC'),
                                   feature_group_count=z.shape[-1])
    return out + b.astype(z.dtype)


def split_in(proj):
    parts = []
    off = 0
    for n in IN_SIZES:
        parts.append(proj[..., off:off + n])
        off += n
    return parts


def hgrn2_mixer(q, f_pre, v, g, lb, norm_w):
    bsz, seq, _ = q.shape
    n = seq // CHUNK
    f32 = jnp.float32
    fg = lb + (1.0 - lb) * jax.nn.sigmoid(f_pre.astype(f32))
    kk = 1.0 - fg
    logf = jnp.log(fg)
    qf = jax.nn.silu(q.astype(f32))

    def heads(t):
        return t.reshape(bsz, n, CHUNK, HG_HEADS, -1).transpose(1, 0, 3, 2, 4)

    qc, kc, vc, lc = heads(qf), heads(kk), heads(v.astype(f32)), heads(logf)
    bc = jnp.cumsum(lc, axis=3)
    causal = jnp.tril(jnp.ones((CHUNK, CHUNK), dtype=bool))

    def step(state, inp):
        q_, k_, v_, b_ = inp
        diff = b_[:, :, :, None, :] - b_[:, :, None, :, :]
        decay = jnp.exp(jnp.where(causal[:, :, None], diff, -jnp.inf))
        att = jnp.einsum('bhtd,bhsd,bhtsd->bhts', q_, k_, decay)
        o = (jnp.einsum('bhts,bhsv->bhtv', att, v_)
             + jnp.einsum('bhtd,bhdv->bhtv', q_ * jnp.exp(b_), state))
        b_last = b_[:, :, -1:, :]
        state = (jnp.exp(b_last[:, :, 0, :])[..., None] * state
                 + jnp.einsum('bhsd,bhsv->bhdv', k_ * jnp.exp(b_last - b_), v_))
        return state, o

    state0 = jnp.zeros((bsz, HG_HEADS, HG_DK, HG_DV), f32)
    _, o = lax.scan(step, state0, (qc, kc, vc, bc))
    o = o.transpose(1, 0, 3, 2, 4).reshape(bsz, seq, HG_HEADS, HG_DV)
    gh = g.astype(f32).reshape(bsz, seq, HG_HEADS, HG_DV)
    o = rmsnorm(o, norm_w) * jax.nn.silu(gh)
    return o.reshape(bsz, seq, HG_HEADS * HG_DV).astype(q.dtype)


def conformer_conv(u, dw_w, dw_b, ln_w, ln_b):
    a, b = u[..., :CV_W], u[..., CV_W:]
    z = a * jax.nn.sigmoid(b)
    z = causal_depthwise_conv(z, dw_w, dw_b)
    return jax.nn.silu(layernorm(z, ln_w, ln_b))


def pool_mixer(u, w_grp, scale):
    bsz, seq, _ = u.shape
    uf = u.astype(jnp.float32).reshape(bsz, seq, POOL_GROUPS, POOL_GW)
    cs = jnp.pad(jnp.cumsum(uf, axis=1), ((0, 0), (1, 0), (0, 0), (0, 0)))
    t = jnp.arange(seq)
    pooled = []
    for gi, w in enumerate(POOL_WINDOWS):
        lo = jnp.maximum(t + 1 - w, 0)
        csg = cs[:, :, gi]
        cnt = (t + 1 - lo).astype(jnp.float32)
        pooled.append((csg[:, 1:] - csg[:, lo]) / cnt[None, :, None])
    pooled = jnp.stack(pooled, axis=2) - uf
    y = jnp.einsum('bsgc,gcd->bsgd', pooled, w_grp.astype(jnp.float32))
    y = y.reshape(bsz, seq, POOL_W) * scale.astype(jnp.float32)
    return y.astype(u.dtype)


def rglru_mixer(xb, yb, conv_w, conv_b, w_a, b_a, w_x, b_x, lam):
    bsz, seq, _ = xb.shape
    f32 = jnp.float32
    xc = causal_depthwise_conv(xb, conv_w, conv_b)
    xh = xc.astype(f32).reshape(bsz, seq, LRU_HEADS, LRU_HD)
    r = jax.nn.sigmoid(jnp.einsum('bshi,hij->bshj', xh, w_a.astype(f32)) + b_a.astype(f32))
    ig = jax.nn.sigmoid(jnp.einsum('bshi,hij->bshj', xh, w_x.astype(f32)) + b_x.astype(f32))
    log_a = -LRU_C * r * jax.nn.softplus(-lam.astype(f32))
    a = jnp.exp(log_a)
    bterm = jnp.sqrt(-jnp.expm1(2.0 * log_a)) * (ig * xh)

    def combine(c1, c2):
        a1, b1 = c1
        a2, b2 = c2
        return a1 * a2, a2 * b1 + b2

    _, hstate = lax.associative_scan(combine, (a, bterm), axis=1)
    hstate = hstate.reshape(bsz, seq, LRU_W)
    return (hstate * jax.nn.gelu(yb.astype(f32))).astype(xb.dtype)


def mem_attention(h, mem_n, w_q, w_kv, w_o):
    bsz, seq, _ = h.shape
    q = (h @ w_q).reshape(bsz, seq, XA_HEADS, XA_HD)
    kv = mem_n @ w_kv
    k = kv[..., :D_MODEL].reshape(bsz, -1, XA_HEADS, XA_HD)
    v = kv[..., D_MODEL:].reshape(bsz, -1, XA_HEADS, XA_HD)
    s = jnp.einsum('bshd,bmhd->bhsm', q, k).astype(jnp.float32) * (XA_HD ** -0.5)
    p = jax.nn.softmax(s, axis=-1).astype(v.dtype)
    o = jnp.einsum('bhsm,bmhd->bshd', p, v).reshape(bsz, seq, D_MODEL)
    return o @ w_o


def _fwd_setup_inputs(seed: int = 0) -> dict:
    key = jax.random.key(seed)
    ks = list(jax.random.split(key, 40))
    f32 = jnp.float32

    def nrm(shape, fan_in, scale=1.0):
        return jax.random.normal(ks.pop(), shape, f32) * (scale * fan_in ** -0.5)

    def gain(shape):
        return 1.0 + 0.02 * jax.random.normal(ks.pop(), shape, f32)

    def small(shape, s=0.02):
        return s * jax.random.normal(ks.pop(), shape, f32)

    u = jax.random.uniform(ks.pop(), (DEPTH, LRU_HEADS, LRU_HD), f32,
                           minval=-np.log(0.999) / LRU_C, maxval=-np.log(0.9) / LRU_C)
    lru_lambda = -jnp.log(jnp.expm1(u))
    return {
        'x': jax.random.normal(ks.pop(), (BATCH, SEQ, D_MODEL), f32),
        'mem': jax.random.normal(ks.pop(), (BATCH, N_MEM, D_MODEL), f32),
        'norm_mix_w': gain((DEPTH, D_MODEL)),
        'w_in': nrm((DEPTH, D_MODEL, IN_W), D_MODEL),
        'hg_lb_raw': small((DEPTH, HG_W), 0.5),
        'hg_norm_w': gain((DEPTH, HG_DV)),
        'cv_dw_w': nrm((DEPTH, CV_KERNEL, CV_W), CV_KERNEL),
        'cv_dw_b': small((DEPTH, CV_W)),
        'cv_ln_w': gain((DEPTH, CV_W)),
        'cv_ln_b': small((DEPTH, CV_W)),
        'pl_w': nrm((DEPTH, POOL_GROUPS, POOL_GW, POOL_GW), POOL_GW),
        'pl_scale': gain((DEPTH, POOL_W)),
        'lru_conv_w': nrm((DEPTH, LRU_CONV, LRU_W), LRU_CONV),
        'lru_conv_b': small((DEPTH, LRU_W)),
        'lru_wa': nrm((DEPTH, LRU_HEADS, LRU_HD, LRU_HD), LRU_HD),
        'lru_ba': small((DEPTH, LRU_HEADS, LRU_HD)),
        'lru_wx': nrm((DEPTH, LRU_HEADS, LRU_HD, LRU_HD), LRU_HD),
        'lru_bx': small((DEPTH, LRU_HEADS, LRU_HD)),
        'lru_lambda': lru_lambda,
        'gate_b': small((DEPTH, N_BRANCH, D_MODEL)),
        'w_branch': nrm((DEPTH, N_BRANCH, BRANCH_W, D_MODEL), BRANCH_W),
        'w_out': nrm((DEPTH, D_MODEL, D_MODEL), D_MODEL, 0.5),
        'norm_mem_w': gain((DEPTH, D_MODEL)),
        'mem_norm_w': gain((DEPTH, D_MODEL)),
        'xa_wq': nrm((DEPTH, D_MODEL, D_MODEL), D_MODEL),
        'xa_wkv': nrm((DEPTH, D_MODEL, 2 * D_MODEL), D_MODEL),
        'xa_wo': nrm((DEPTH, D_MODEL, D_MODEL), D_MODEL, 0.5),
        'norm_ffn_w': gain((DEPTH, D_MODEL)),
        'ffn_w1': nrm((DEPTH, D_MODEL, D_FF), D_MODEL),
        'ffn_w2': nrm((DEPTH, D_FF, D_MODEL), D_FF, 0.5),
        'final_norm_w': gain((D_MODEL,)),
    }


def _fwd_reference(x, mem, norm_mix_w, w_in, hg_lb_raw, hg_norm_w, cv_dw_w, cv_dw_b, cv_ln_w, cv_ln_b,
              pl_w, pl_scale, lru_conv_w, lru_conv_b, lru_wa, lru_ba, lru_wx, lru_bx, lru_lambda,
              gate_b, w_branch, w_out, norm_mem_w, mem_norm_w, xa_wq, xa_wkv, xa_wo,
              norm_ffn_w, ffn_w1, ffn_w2, final_norm_w):
    bsz, seq, _ = x.shape
    lb = jnp.cumsum(jax.nn.softmax(hg_lb_raw.astype(jnp.float32), axis=0), axis=0)
    lb = lb - lb[0:1]
    for l in range(DEPTH):
        h = rmsnorm(x, norm_mix_w[l])
        proj = h @ w_in[l]
        hq, hf, hv, hg, cv_u, pl_u, lru_x, lru_y, gate_pre = split_in(proj)
        b_hg = hgrn2_mixer(hq, hf, hv, hg, lb[l], hg_norm_w[l])
        b_cv = conformer_conv(cv_u, cv_dw_w[l], cv_dw_b[l], cv_ln_w[l], cv_ln_b[l])
        b_pl = pool_mixer(pl_u, pl_w[l], pl_scale[l])
        b_lru = rglru_mixer(lru_x, lru_y, lru_conv_w[l], lru_conv_b[l], lru_wa[l], lru_ba[l],
                            lru_wx[l], lru_bx[l], lru_lambda[l])
        branches = jnp.stack([b_hg, b_cv, b_pl, b_lru], axis=2)
        up = jnp.einsum('bskm,kmd->bskd', branches, w_branch[l])
        gates = jax.nn.sigmoid(gate_pre.reshape(bsz, seq, N_BRANCH, D_MODEL) + gate_b[l])
        merged = jnp.sum(gates * up, axis=2)
        x = x + (merged @ w_out[l]).astype(x.dtype)
        h = rmsnorm(x, norm_mem_w[l])
        x = x + mem_attention(h, rmsnorm(mem, mem_norm_w[l]), xa_wq[l], xa_wkv[l], xa_wo[l]).astype(x.dtype)
        h = rmsnorm(x, norm_ffn_w[l])
        x = x + (jnp.square(jax.nn.relu(h @ ffn_w1[l])) @ ffn_w2[l]).astype(x.dtype)
    return rmsnorm(x, final_norm_w)


import jax as _jax
import jax.numpy as _jnp

TWIN_FORMAT = 'train_step'
FWD_PARAMS = ['x', 'mem', 'norm_mix_w', 'w_in', 'hg_lb_raw', 'hg_norm_w', 'cv_dw_w', 'cv_dw_b', 'cv_ln_w', 'cv_ln_b', 'pl_w', 'pl_scale', 'lru_conv_w', 'lru_conv_b', 'lru_wa', 'lru_ba', 'lru_wx', 'lru_bx', 'lru_lambda', 'gate_b', 'w_branch', 'w_out', 'norm_mem_w', 'mem_norm_w', 'xa_wq', 'xa_wkv', 'xa_wo', 'norm_ffn_w', 'ffn_w1', 'ffn_w2', 'final_norm_w']
TWIN_WEIGHTS = ['norm_mix_w', 'w_in', 'hg_lb_raw', 'hg_norm_w', 'cv_dw_w', 'cv_dw_b', 'cv_ln_w', 'cv_ln_b', 'pl_w', 'pl_scale', 'lru_conv_w', 'lru_conv_b', 'lru_wa', 'lru_ba', 'lru_wx', 'lru_bx', 'lru_lambda', 'gate_b', 'w_branch', 'w_out', 'norm_mem_w', 'mem_norm_w', 'xa_wq', 'xa_wkv', 'xa_wo', 'norm_ffn_w', 'ffn_w1', 'ffn_w2', 'final_norm_w']
TWIN_DIFF_INPUT = 'x'
TWIN_INPUTS = ['x', 'mem', 'norm_mix_w', 'w_in', 'hg_lb_raw', 'hg_norm_w', 'cv_dw_w', 'cv_dw_b', 'cv_ln_w', 'cv_ln_b', 'pl_w', 'pl_scale', 'lru_conv_w', 'lru_conv_b', 'lru_wa', 'lru_ba', 'lru_wx', 'lru_bx', 'lru_lambda', 'gate_b', 'w_branch', 'w_out', 'norm_mem_w', 'mem_norm_w', 'xa_wq', 'xa_wkv', 'xa_wo', 'norm_ffn_w', 'ffn_w1', 'ffn_w2', 'final_norm_w', 'loss_target', 'm_norm_mix_w', 'm_w_in', 'm_hg_lb_raw', 'm_hg_norm_w', 'm_cv_dw_w', 'm_cv_dw_b', 'm_cv_ln_w', 'm_cv_ln_b', 'm_pl_w', 'm_pl_scale', 'm_lru_conv_w', 'm_lru_conv_b', 'm_lru_wa', 'm_lru_ba', 'm_lru_wx', 'm_lru_bx', 'm_lru_lambda', 'm_gate_b', 'm_w_branch', 'm_w_out', 'm_norm_mem_w', 'm_mem_norm_w', 'm_xa_wq', 'm_xa_wkv', 'm_xa_wo', 'm_norm_ffn_w', 'm_ffn_w1', 'm_ffn_w2', 'm_final_norm_w', 'v_norm_mix_w', 'v_w_in', 'v_hg_lb_raw', 'v_hg_norm_w', 'v_cv_dw_w', 'v_cv_dw_b', 'v_cv_ln_w', 'v_cv_ln_b', 'v_pl_w', 'v_pl_scale', 'v_lru_conv_w', 'v_lru_conv_b', 'v_lru_wa', 'v_lru_ba', 'v_lru_wx', 'v_lru_bx', 'v_lru_lambda', 'v_gate_b', 'v_w_branch', 'v_w_out', 'v_norm_mem_w', 'v_mem_norm_w', 'v_xa_wq', 'v_xa_wkv', 'v_xa_wo', 'v_norm_ffn_w', 'v_ffn_w1', 'v_ffn_w2', 'v_final_norm_w']
TWIN_OUTPUTS = ['loss', 'grad_x', 'grad_norm_mix_w', 'grad_w_in', 'grad_hg_lb_raw', 'grad_hg_norm_w', 'grad_cv_dw_w', 'grad_cv_dw_b', 'grad_cv_ln_w', 'grad_cv_ln_b', 'grad_pl_w', 'grad_pl_scale', 'grad_lru_conv_w', 'grad_lru_conv_b', 'grad_lru_wa', 'grad_lru_ba', 'grad_lru_wx', 'grad_lru_bx', 'grad_lru_lambda', 'grad_gate_b', 'grad_w_branch', 'grad_w_out', 'grad_norm_mem_w', 'grad_mem_norm_w', 'grad_xa_wq', 'grad_xa_wkv', 'grad_xa_wo', 'grad_norm_ffn_w', 'grad_ffn_w1', 'grad_ffn_w2', 'grad_final_norm_w', 'delta_norm_mix_w', 'delta_w_in', 'delta_hg_lb_raw', 'delta_hg_norm_w', 'delta_cv_dw_w', 'delta_cv_dw_b', 'delta_cv_ln_w', 'delta_cv_ln_b', 'delta_pl_w', 'delta_pl_scale', 'delta_lru_conv_w', 'delta_lru_conv_b', 'delta_lru_wa', 'delta_lru_ba', 'delta_lru_wx', 'delta_lru_bx', 'delta_lru_lambda', 'delta_gate_b', 'delta_w_branch', 'delta_w_out', 'delta_norm_mem_w', 'delta_mem_norm_w', 'delta_xa_wq', 'delta_xa_wkv', 'delta_xa_wo', 'delta_norm_ffn_w', 'delta_ffn_w1', 'delta_ffn_w2', 'delta_final_norm_w', 'new_m_norm_mix_w', 'new_m_w_in', 'new_m_hg_lb_raw', 'new_m_hg_norm_w', 'new_m_cv_dw_w', 'new_m_cv_dw_b', 'new_m_cv_ln_w', 'new_m_cv_ln_b', 'new_m_pl_w', 'new_m_pl_scale', 'new_m_lru_conv_w', 'new_m_lru_conv_b', 'new_m_lru_wa', 'new_m_lru_ba', 'new_m_lru_wx', 'new_m_lru_bx', 'new_m_lru_lambda', 'new_m_gate_b', 'new_m_w_branch', 'new_m_w_out', 'new_m_norm_mem_w', 'new_m_mem_norm_w', 'new_m_xa_wq', 'new_m_xa_wkv', 'new_m_xa_wo', 'new_m_norm_ffn_w', 'new_m_ffn_w1', 'new_m_ffn_w2', 'new_m_final_norm_w', 'new_v_norm_mix_w', 'new_v_w_in', 'new_v_hg_lb_raw', 'new_v_hg_norm_w', 'new_v_cv_dw_w', 'new_v_cv_dw_b', 'new_v_cv_ln_w', 'new_v_cv_ln_b', 'new_v_pl_w', 'new_v_pl_scale', 'new_v_lru_conv_w', 'new_v_lru_conv_b', 'new_v_lru_wa', 'new_v_lru_ba', 'new_v_lru_wx', 'new_v_lru_bx', 'new_v_lru_lambda', 'new_v_gate_b', 'new_v_w_branch', 'new_v_w_out', 'new_v_norm_mem_w', 'new_v_mem_norm_w', 'new_v_xa_wq', 'new_v_xa_wkv', 'new_v_xa_wo', 'new_v_norm_ffn_w', 'new_v_ffn_w1', 'new_v_ffn_w2', 'new_v_final_norm_w']
TWIN_LEAF_KINDS = {'loss': 'loss', 'grad_x': 'grad_x', 'grad_norm_mix_w': 'grad_w', 'grad_w_in': 'grad_w', 'grad_hg_lb_raw': 'grad_w', 'grad_hg_norm_w': 'grad_w', 'grad_cv_dw_w': 'grad_w', 'grad_cv_dw_b': 'grad_w', 'grad_cv_ln_w': 'grad_w', 'grad_cv_ln_b': 'grad_w', 'grad_pl_w': 'grad_w', 'grad_pl_scale': 'grad_w', 'grad_lru_conv_w': 'grad_w', 'grad_lru_conv_b': 'grad_w', 'grad_lru_wa': 'grad_w', 'grad_lru_ba': 'grad_w', 'grad_lru_wx': 'grad_w', 'grad_lru_bx': 'grad_w', 'grad_lru_lambda': 'grad_w', 'grad_gate_b': 'grad_w', 'grad_w_branch': 'grad_w', 'grad_w_out': 'grad_w', 'grad_norm_mem_w': 'grad_w', 'grad_mem_norm_w': 'grad_w', 'grad_xa_wq': 'grad_w', 'grad_xa_wkv': 'grad_w', 'grad_xa_wo': 'grad_w', 'grad_norm_ffn_w': 'grad_w', 'grad_ffn_w1': 'grad_w', 'grad_ffn_w2': 'grad_w', 'grad_final_norm_w': 'grad_w', 'delta_norm_mix_w': 'delta_w', 'delta_w_in': 'delta_w', 'delta_hg_lb_raw': 'delta_w', 'delta_hg_norm_w': 'delta_w', 'delta_cv_dw_w': 'delta_w', 'delta_cv_dw_b': 'delta_w', 'delta_cv_ln_w': 'delta_w', 'delta_cv_ln_b': 'delta_w', 'delta_pl_w': 'delta_w', 'delta_pl_scale': 'delta_w', 'delta_lru_conv_w': 'delta_w', 'delta_lru_conv_b': 'delta_w', 'delta_lru_wa': 'delta_w', 'delta_lru_ba': 'delta_w', 'delta_lru_wx': 'delta_w', 'delta_lru_bx': 'delta_w', 'delta_lru_lambda': 'delta_w', 'delta_gate_b': 'delta_w', 'delta_w_branch': 'delta_w', 'delta_w_out': 'delta_w', 'delta_norm_mem_w': 'delta_w', 'delta_mem_norm_w': 'delta_w', 'delta_xa_wq': 'delta_w', 'delta_xa_wkv': 'delta_w', 'delta_xa_wo': 'delta_w', 'delta_norm_ffn_w': 'delta_w', 'delta_ffn_w1': 'delta_w', 'delta_ffn_w2': 'delta_w', 'delta_final_norm_w': 'delta_w', 'new_m_norm_mix_w': 'new_m', 'new_m_w_in': 'new_m', 'new_m_hg_lb_raw': 'new_m', 'new_m_hg_norm_w': 'new_m', 'new_m_cv_dw_w': 'new_m', 'new_m_cv_dw_b': 'new_m', 'new_m_cv_ln_w': 'new_m', 'new_m_cv_ln_b': 'new_m', 'new_m_pl_w': 'new_m', 'new_m_pl_scale': 'new_m', 'new_m_lru_conv_w': 'new_m', 'new_m_lru_conv_b': 'new_m', 'new_m_lru_wa': 'new_m', 'new_m_lru_ba': 'new_m', 'new_m_lru_wx': 'new_m', 'new_m_lru_bx': 'new_m', 'new_m_lru_lambda': 'new_m', 'new_m_gate_b': 'new_m', 'new_m_w_branch': 'new_m', 'new_m_w_out': 'new_m', 'new_m_norm_mem_w': 'new_m', 'new_m_mem_norm_w': 'new_m', 'new_m_xa_wq': 'new_m', 'new_m_xa_wkv': 'new_m', 'new_m_xa_wo': 'new_m', 'new_m_norm_ffn_w': 'new_m', 'new_m_ffn_w1': 'new_m', 'new_m_ffn_w2': 'new_m', 'new_m_final_norm_w': 'new_m', 'new_v_norm_mix_w': 'new_v', 'new_v_w_in': 'new_v', 'new_v_hg_lb_raw': 'new_v', 'new_v_hg_norm_w': 'new_v', 'new_v_cv_dw_w': 'new_v', 'new_v_cv_dw_b': 'new_v', 'new_v_cv_ln_w': 'new_v', 'new_v_cv_ln_b': 'new_v', 'new_v_pl_w': 'new_v', 'new_v_pl_scale': 'new_v', 'new_v_lru_conv_w': 'new_v', 'new_v_lru_conv_b': 'new_v', 'new_v_lru_wa': 'new_v', 'new_v_lru_ba': 'new_v', 'new_v_lru_wx': 'new_v', 'new_v_lru_bx': 'new_v', 'new_v_lru_lambda': 'new_v', 'new_v_gate_b': 'new_v', 'new_v_w_branch': 'new_v', 'new_v_w_out': 'new_v', 'new_v_norm_mem_w': 'new_v', 'new_v_mem_norm_w': 'new_v', 'new_v_xa_wq': 'new_v', 'new_v_xa_wkv': 'new_v', 'new_v_xa_wo': 'new_v', 'new_v_norm_ffn_w': 'new_v', 'new_v_ffn_w1': 'new_v', 'new_v_ffn_w2': 'new_v', 'new_v_final_norm_w': 'new_v'}


def _forward(args):
    return _fwd_reference(*[args[k] for k in FWD_PARAMS])


def _output_shape():
    def fwd():
        inp = _fwd_setup_inputs(0)
        return _fwd_reference(*[inp[k] for k in FWD_PARAMS])
    out = _jax.eval_shape(fwd)
    return out.shape, out.dtype

N_MICROBATCH = 1
ADAM_LR = 0.001
ADAM_B1 = 0.9
ADAM_B2 = 0.999
ADAM_EPS = 1e-08
ADAM_WD = 0.01
ADAM_STEP = 10
PER_EXAMPLE_BATCH_AXIS = {'x': 0, 'mem': 0, 'loss_target': 0}
SHARED_INPUTS = []
_WEIGHT_DTYPES = {'norm_mix_w': _jnp.float32, 'w_in': _jnp.float32, 'hg_lb_raw': _jnp.float32, 'hg_norm_w': _jnp.float32, 'cv_dw_w': _jnp.float32, 'cv_dw_b': _jnp.float32, 'cv_ln_w': _jnp.float32, 'cv_ln_b': _jnp.float32, 'pl_w': _jnp.float32, 'pl_scale': _jnp.float32, 'lru_conv_w': _jnp.float32, 'lru_conv_b': _jnp.float32, 'lru_wa': _jnp.float32, 'lru_ba': _jnp.float32, 'lru_wx': _jnp.float32, 'lru_bx': _jnp.float32, 'lru_lambda': _jnp.float32, 'gate_b': _jnp.float32, 'w_branch': _jnp.float32, 'w_out': _jnp.float32, 'norm_mem_w': _jnp.float32, 'mem_norm_w': _jnp.float32, 'xa_wq': _jnp.float32, 'xa_wkv': _jnp.float32, 'xa_wo': _jnp.float32, 'norm_ffn_w': _jnp.float32, 'ffn_w1': _jnp.float32, 'ffn_w2': _jnp.float32, 'final_norm_w': _jnp.float32}
MOMENT_SCALE = {'norm_mix_w': 9.014075e-02, 'w_in': 2.956072e-02, 'hg_lb_raw': 2.223616e-03, 'hg_norm_w': 8.237600e-02, 'cv_dw_w': 3.629097e-02, 'cv_dw_b': 9.667110e-02, 'cv_ln_w': 5.111176e-02, 'cv_ln_b': 5.613069e-02, 'pl_w': 4.940857e-02, 'pl_scale': 4.937356e-02, 'lru_conv_w': 5.976869e-02, 'lru_conv_b': 3.102597e-01, 'lru_wa': 1.233927e-02, 'lru_ba': 1.295601e-02, 'lru_wx': 2.414615e-02, 'lru_bx': 2.610417e-02, 'lru_lambda': 3.021232e-02, 'gate_b': 1.335700e-02, 'w_branch': 3.277441e-02, 'w_out': 1.322128e-01, 'norm_mem_w': 7.851606e-03, 'mem_norm_w': 1.337102e-02, 'xa_wq': 7.889109e-03, 'xa_wkv': 8.742960e-03, 'xa_wo': 1.899242e-02, 'norm_ffn_w': 9.323637e-02, 'ffn_w1': 4.687718e-02, 'ffn_w2': 2.047297e-01, 'final_norm_w': 3.238231e+01}


def _to_microbatches(a, axis):
    t = _jnp.moveaxis(a, axis, 0)
    t = t.reshape((N_MICROBATCH, t.shape[0] // N_MICROBATCH) + t.shape[1:])
    return _jnp.moveaxis(t, 1, axis + 1)


def setup_inputs(seed: int = 0) -> dict:
    inp = _fwd_setup_inputs(seed)
    key = _jax.random.fold_in(_jax.random.key(seed), 7919)
    shape, _ = _output_shape()
    out = dict(inp)
    out["loss_target"] = _jax.random.normal(_jax.random.fold_in(key, 0), shape, _jnp.float32)
    for i, name in enumerate(TWIN_WEIGHTS):
        w = inp[name].astype(_jnp.float32)
        if MOMENT_SCALE is None:
            s = _jnp.sqrt(_jnp.mean(_jnp.square(w)) + 1e-30)
        else:
            s = MOMENT_SCALE[name]
        km, kv = _jax.random.split(_jax.random.fold_in(key, i + 1))
        out[name] = w
        out["m_" + name] = s * _jax.random.normal(km, w.shape, _jnp.float32)
        out["v_" + name] = (s * s) * _jax.random.uniform(kv, w.shape, _jnp.float32, 0.5, 1.5)
    if N_MICROBATCH > 1:
        for name, axis in PER_EXAMPLE_BATCH_AXIS.items():
            out[name] = _to_microbatches(out[name], axis)
    return {'x': out['x'], 'mem': out['mem'], 'norm_mix_w': out['norm_mix_w'], 'w_in': out['w_in'], 'hg_lb_raw': out['hg_lb_raw'], 'hg_norm_w': out['hg_norm_w'], 'cv_dw_w': out['cv_dw_w'], 'cv_dw_b': out['cv_dw_b'], 'cv_ln_w': out['cv_ln_w'], 'cv_ln_b': out['cv_ln_b'], 'pl_w': out['pl_w'], 'pl_scale': out['pl_scale'], 'lru_conv_w': out['lru_conv_w'], 'lru_conv_b': out['lru_conv_b'], 'lru_wa': out['lru_wa'], 'lru_ba': out['lru_ba'], 'lru_wx': out['lru_wx'], 'lru_bx': out['lru_bx'], 'lru_lambda': out['lru_lambda'], 'gate_b': out['gate_b'], 'w_branch': out['w_branch'], 'w_out': out['w_out'], 'norm_mem_w': out['norm_mem_w'], 'mem_norm_w': out['mem_norm_w'], 'xa_wq': out['xa_wq'], 'xa_wkv': out['xa_wkv'], 'xa_wo': out['xa_wo'], 'norm_ffn_w': out['norm_ffn_w'], 'ffn_w1': out['ffn_w1'], 'ffn_w2': out['ffn_w2'], 'final_norm_w': out['final_norm_w'], 'loss_target': out['loss_target'], 'm_norm_mix_w': out['m_norm_mix_w'], 'm_w_in': out['m_w_in'], 'm_hg_lb_raw': out['m_hg_lb_raw'], 'm_hg_norm_w': out['m_hg_norm_w'], 'm_cv_dw_w': out['m_cv_dw_w'], 'm_cv_dw_b': out['m_cv_dw_b'], 'm_cv_ln_w': out['m_cv_ln_w'], 'm_cv_ln_b': out['m_cv_ln_b'], 'm_pl_w': out['m_pl_w'], 'm_pl_scale': out['m_pl_scale'], 'm_lru_conv_w': out['m_lru_conv_w'], 'm_lru_conv_b': out['m_lru_conv_b'], 'm_lru_wa': out['m_lru_wa'], 'm_lru_ba': out['m_lru_ba'], 'm_lru_wx': out['m_lru_wx'], 'm_lru_bx': out['m_lru_bx'], 'm_lru_lambda': out['m_lru_lambda'], 'm_gate_b': out['m_gate_b'], 'm_w_branch': out['m_w_branch'], 'm_w_out': out['m_w_out'], 'm_norm_mem_w': out['m_norm_mem_w'], 'm_mem_norm_w': out['m_mem_norm_w'], 'm_xa_wq': out['m_xa_wq'], 'm_xa_wkv': out['m_xa_wkv'], 'm_xa_wo': out['m_xa_wo'], 'm_norm_ffn_w': out['m_norm_ffn_w'], 'm_ffn_w1': out['m_ffn_w1'], 'm_ffn_w2': out['m_ffn_w2'], 'm_final_norm_w': out['m_final_norm_w'], 'v_norm_mix_w': out['v_norm_mix_w'], 'v_w_in': out['v_w_in'], 'v_hg_lb_raw': out['v_hg_lb_raw'], 'v_hg_norm_w': out['v_hg_norm_w'], 'v_cv_dw_w': out['v_cv_dw_w'], 'v_cv_dw_b': out['v_cv_dw_b'], 'v_cv_ln_w': out['v_cv_ln_w'], 'v_cv_ln_b': out['v_cv_ln_b'], 'v_pl_w': out['v_pl_w'], 'v_pl_scale': out['v_pl_scale'], 'v_lru_conv_w': out['v_lru_conv_w'], 'v_lru_conv_b': out['v_lru_conv_b'], 'v_lru_wa': out['v_lru_wa'], 'v_lru_ba': out['v_lru_ba'], 'v_lru_wx': out['v_lru_wx'], 'v_lru_bx': out['v_lru_bx'], 'v_lru_lambda': out['v_lru_lambda'], 'v_gate_b': out['v_gate_b'], 'v_w_branch': out['v_w_branch'], 'v_w_out': out['v_w_out'], 'v_norm_mem_w': out['v_norm_mem_w'], 'v_mem_norm_w': out['v_mem_norm_w'], 'v_xa_wq': out['v_xa_wq'], 'v_xa_wkv': out['v_xa_wkv'], 'v_xa_wo': out['v_xa_wo'], 'v_norm_ffn_w': out['v_norm_ffn_w'], 'v_ffn_w1': out['v_ffn_w1'], 'v_ffn_w2': out['v_ffn_w2'], 'v_final_norm_w': out['v_final_norm_w']}


def _loss(weights, diff, rest, loss_target):
    with _jax.named_scope("forward"):
        args = {**rest, TWIN_DIFF_INPUT: diff, **{k: w.astype(_WEIGHT_DTYPES[k]) for k, w in weights.items()}}
        y = _forward(args)
    with _jax.named_scope("loss_head"):
        err = _jnp.square(y.astype(_jnp.float32) - loss_target)
        return 0.5 * _jnp.sum(_jnp.mean(err, axis=-1)) if err.ndim else 0.5 * err


def _adamw(w, g, m, v):
    m = ADAM_B1 * m + (1.0 - ADAM_B1) * g
    v = ADAM_B2 * v + (1.0 - ADAM_B2) * _jnp.square(g)
    m_hat = m / (1.0 - ADAM_B1 ** ADAM_STEP)
    v_hat = v / (1.0 - ADAM_B2 ** ADAM_STEP)
    delta = -ADAM_LR * (m_hat / (_jnp.sqrt(v_hat) + ADAM_EPS) + ADAM_WD * w)
    return delta, m, v


def reference(x, mem, norm_mix_w, w_in, hg_lb_raw, hg_norm_w, cv_dw_w, cv_dw_b, cv_ln_w, cv_ln_b, pl_w, pl_scale, lru_conv_w, lru_conv_b, lru_wa, lru_ba, lru_wx, lru_bx, lru_lambda, gate_b, w_branch, w_out, norm_mem_w, mem_norm_w, xa_wq, xa_wkv, xa_wo, norm_ffn_w, ffn_w1, ffn_w2, final_norm_w, loss_target, m_norm_mix_w, m_w_in, m_hg_lb_raw, m_hg_norm_w, m_cv_dw_w, m_cv_dw_b, m_cv_ln_w, m_cv_ln_b, m_pl_w, m_pl_scale, m_lru_conv_w, m_lru_conv_b, m_lru_wa, m_lru_ba, m_lru_wx, m_lru_bx, m_lru_lambda, m_gate_b, m_w_branch, m_w_out, m_norm_mem_w, m_mem_norm_w, m_xa_wq, m_xa_wkv, m_xa_wo, m_norm_ffn_w, m_ffn_w1, m_ffn_w2, m_final_norm_w, v_norm_mix_w, v_w_in, v_hg_lb_raw, v_hg_norm_w, v_cv_dw_w, v_cv_dw_b, v_cv_ln_w, v_cv_ln_b, v_pl_w, v_pl_scale, v_lru_conv_w, v_lru_conv_b, v_lru_wa, v_lru_ba, v_lru_wx, v_lru_bx, v_lru_lambda, v_gate_b, v_w_branch, v_w_out, v_norm_mem_w, v_mem_norm_w, v_xa_wq, v_xa_wkv, v_xa_wo, v_norm_ffn_w, v_ffn_w1, v_ffn_w2, v_final_norm_w):
    given = dict(x=x, mem=mem, norm_mix_w=norm_mix_w, w_in=w_in, hg_lb_raw=hg_lb_raw, hg_norm_w=hg_norm_w, cv_dw_w=cv_dw_w, cv_dw_b=cv_dw_b, cv_ln_w=cv_ln_w, cv_ln_b=cv_ln_b, pl_w=pl_w, pl_scale=pl_scale, lru_conv_w=lru_conv_w, lru_conv_b=lru_conv_b, lru_wa=lru_wa, lru_ba=lru_ba, lru_wx=lru_wx, lru_bx=lru_bx, lru_lambda=lru_lambda, gate_b=gate_b, w_branch=w_branch, w_out=w_out, norm_mem_w=norm_mem_w, mem_norm_w=mem_norm_w, xa_wq=xa_wq, xa_wkv=xa_wkv, xa_wo=xa_wo, norm_ffn_w=norm_ffn_w, ffn_w1=ffn_w1, ffn_w2=ffn_w2, final_norm_w=final_norm_w, loss_target=loss_target, m_norm_mix_w=m_norm_mix_w, m_w_in=m_w_in, m_hg_lb_raw=m_hg_lb_raw, m_hg_norm_w=m_hg_norm_w, m_cv_dw_w=m_cv_dw_w, m_cv_dw_b=m_cv_dw_b, m_cv_ln_w=m_cv_ln_w, m_cv_ln_b=m_cv_ln_b, m_pl_w=m_pl_w, m_pl_scale=m_pl_scale, m_lru_conv_w=m_lru_conv_w, m_lru_conv_b=m_lru_conv_b, m_lru_wa=m_lru_wa, m_lru_ba=m_lru_ba, m_lru_wx=m_lru_wx, m_lru_bx=m_lru_bx, m_lru_lambda=m_lru_lambda, m_gate_b=m_gate_b, m_w_branch=m_w_branch, m_w_out=m_w_out, m_norm_mem_w=m_norm_mem_w, m_mem_norm_w=m_mem_norm_w, m_xa_wq=m_xa_wq, m_xa_wkv=m_xa_wkv, m_xa_wo=m_xa_wo, m_norm_ffn_w=m_norm_ffn_w, m_ffn_w1=m_ffn_w1, m_ffn_w2=m_ffn_w2, m_final_norm_w=m_final_norm_w, v_norm_mix_w=v_norm_mix_w, v_w_in=v_w_in, v_hg_lb_raw=v_hg_lb_raw, v_hg_norm_w=v_hg_norm_w, v_cv_dw_w=v_cv_dw_w, v_cv_dw_b=v_cv_dw_b, v_cv_ln_w=v_cv_ln_w, v_cv_ln_b=v_cv_ln_b, v_pl_w=v_pl_w, v_pl_scale=v_pl_scale, v_lru_conv_w=v_lru_conv_w, v_lru_conv_b=v_lru_conv_b, v_lru_wa=v_lru_wa, v_lru_ba=v_lru_ba, v_lru_wx=v_lru_wx, v_lru_bx=v_lru_bx, v_lru_lambda=v_lru_lambda, v_gate_b=v_gate_b, v_w_branch=v_w_branch, v_w_out=v_w_out, v_norm_mem_w=v_norm_mem_w, v_mem_norm_w=v_mem_norm_w, v_xa_wq=v_xa_wq, v_xa_wkv=v_xa_wkv, v_xa_wo=v_xa_wo, v_norm_ffn_w=v_norm_ffn_w, v_ffn_w1=v_ffn_w1, v_ffn_w2=v_ffn_w2, v_final_norm_w=v_final_norm_w)
    weights = {n: given[n] for n in TWIN_WEIGHTS}
    shared = {n: given[n] for n in SHARED_INPUTS}
    per_example = {n: given[n] for n in ['x', 'mem']}
    grad_fn = _jax.value_and_grad(_loss, argnums=(0, 1))

    def one_microbatch(ex, loss_target):
        ex = dict(ex)
        diff = ex.pop(TWIN_DIFF_INPUT)
        return grad_fn(weights, diff, {**shared, **ex}, loss_target)

    if N_MICROBATCH == 1:
        loss, (grad_w, grad_x) = one_microbatch(per_example, given["loss_target"])
    else:
        def body(carry, xs):
            loss_sum, grad_sum = carry
            l_k, (gw_k, gx_k) = one_microbatch(xs[0], xs[1])
            with _jax.named_scope("update"):
                return (loss_sum + l_k, _jax.tree.map(_jnp.add, grad_sum, gw_k)), gx_k

        init = (_jnp.zeros((), _jnp.float32), _jax.tree.map(_jnp.zeros_like, weights))
        (loss, grad_w), grad_x = _jax.lax.scan(body, init, (per_example, given["loss_target"]))
    with _jax.named_scope("update"):
        delta_w, new_m, new_v = {}, {}, {}
        for n in TWIN_WEIGHTS:
            delta_w[n], new_m[n], new_v[n] = _adamw(weights[n], grad_w[n], given["m_" + n], given["v_" + n])
    return (loss, grad_x, *[grad_w[n] for n in TWIN_WEIGHTS], *[delta_w[n] for n in TWIN_WEIGHTS],
            *[new_m[n] for n in TWIN_WEIGHTS], *[new_v[n] for n in TWIN_WEIGHTS])
```

```python
import functools
import math

import jax
import jax.numpy as jnp
from jax import lax
from jax.experimental import pallas as pl
from jax.experimental.pallas import tpu as pltpu

F32 = jnp.float32
BF16 = jnp.bfloat16
I32 = jnp.int32

N_DEV = 8
D_MODEL = 1024
DEPTH = 4
CHUNK = 64
EPS = 1e-6
HG_HEADS = 4
BRANCH_W = 512
CV_KERNEL = 31
POOL_WINDOWS = (2, 4, 8, 16)
LRU_CONV = 4
LRU_C = 8.0
XA_HEADS = 4
XA_HD = D_MODEL // XA_HEADS
D_FF = 4 * D_MODEL
IN_W = 8704
OFF_Q, OFF_F, OFF_V, OFF_G, OFF_CV, OFF_PL, OFF_LX, OFF_LY, OFF_GATE = 0, 512, 1024, 1536, 2048, 3072, 3584, 4096, 4608
LANE = 128
ADAM_LR, ADAM_B1, ADAM_B2, ADAM_EPS, ADAM_WD, ADAM_STEP = 0.001, 0.9, 0.999, 1e-08, 0.01, 10
VMEM_LIMIT = 56 * 1024 * 1024
MESH = pl.DeviceIdType.MESH
NEG = -1e30


def _cp(sem, **kw):
    return pltpu.CompilerParams(dimension_semantics=sem, vmem_limit_bytes=VMEM_LIMIT, **kw)


def _sigmoid(x):
    return 1.0 / (1.0 + jnp.exp(-x))


def _dsilu(x, s):
    return s * (1.0 + x * (1.0 - s))


def _dot(a, b, cdims, precision=None):
    return lax.dot_general(a, b, (cdims, ((), ())), preferred_element_type=F32, precision=precision)


NN = ((1,), (0,))
NT = ((1,), (1,))
TN = ((0,), (0,))


def _mm(name, a, b, *, grid, a_spec, b_spec, o_specs, out_shapes, acc_shape, cdims, epi=None, extras=(), extra_specs=()):
    nk = grid[2]
    n_e, n_o = len(extras), len(out_shapes)

    def body(*refs):
        a_ref, b_ref = refs[0], refs[1]
        e_refs = refs[2:2 + n_e]
        o_refs = refs[2 + n_e:2 + n_e + n_o]

        def finish(acc):
            vals = epi(acc, *[r[...] for r in e_refs]) if epi is not None else (acc,)
            for r, v in zip(o_refs, vals):
                r[...] = v.astype(r.dtype)

        part = _dot(a_ref[...].astype(BF16), b_ref[...].astype(BF16), cdims)
        if nk == 1:
            finish(part)
        else:
            acc_ref = refs[-1]
            k = pl.program_id(2)

            @pl.when(k == 0)
            def _():
                acc_ref[...] = part

            @pl.when(k > 0)
            def _():
                acc_ref[...] += part

            @pl.when(k == nk - 1)
            def _():
                finish(acc_ref[...])

    return pl.pallas_call(
        body, name=name, grid=grid,
        in_specs=[a_spec, b_spec, *extra_specs], out_specs=list(o_specs), out_shape=list(out_shapes),
        scratch_shapes=[] if nk == 1 else [pltpu.VMEM(acc_shape, F32)],
        compiler_params=_cp(("parallel", "parallel", "arbitrary")),
    )(a, b, *extras)


def _tile(n, pref):
    t = min(n, pref)
    while n % t:
        t //= 2
    return t


def mm_nt(name, a, b, out_dtype=F32, epi=None, extras=(), n_out=1, out_dtypes=None, tm=1024, tn=512, tk=1024):
    M, K = a.shape
    N = b.shape[0]
    tm, tn, tk = _tile(M, tm), _tile(N, tn), _tile(K, tk)
    odt = out_dtypes or [out_dtype] * n_out
    o_spec = pl.BlockSpec((tm, tn), lambda i, j, k: (i, j))
    return _mm(name, a, b, grid=(M // tm, N // tn, K // tk),
               a_spec=pl.BlockSpec((tm, tk), lambda i, j, k: (i, k)),
               b_spec=pl.BlockSpec((tn, tk), lambda i, j, k: (j, k)),
               o_specs=[o_spec] * len(odt), out_shapes=[jax.ShapeDtypeStruct((M, N), d) for d in odt],
               acc_shape=(tm, tn), cdims=NT, epi=epi, extras=extras, extra_specs=[o_spec] * len(extras))


def mm_nn(name, a, b, out_dtype=F32, epi=None, extras=(), n_out=1, out_dtypes=None, tm=1024, tn=512, tk=1024):
    M, K = a.shape
    N = b.shape[1]
    tm, tn, tk = _tile(M, tm), _tile(N, tn), _tile(K, tk)
    odt = out_dtypes or [out_dtype] * n_out
    o_spec = pl.BlockSpec((tm, tn), lambda i, j, k: (i, j))
    return _mm(name, a, b, grid=(M // tm, N // tn, K // tk),
               a_spec=pl.BlockSpec((tm, tk), lambda i, j, k: (i, k)),
               b_spec=pl.BlockSpec((tk, tn), lambda i, j, k: (k, j)),
               o_specs=[o_spec] * len(odt), out_shapes=[jax.ShapeDtypeStruct((M, N), d) for d in odt],
               acc_shape=(tm, tn), cdims=NN, epi=epi, extras=extras, extra_specs=[o_spec] * len(extras))


def mm_tn(name, a, b, out_dtype=BF16, tm=512, tn=512, tk=1024):
    K, M = a.shape
    N = b.shape[1]
    tm, tn, tk = _tile(M, tm), _tile(N, tn), _tile(K, tk)
    return _mm(name, a, b, grid=(M // tm, N // tn, K // tk),
               a_spec=pl.BlockSpec((tk, tm), lambda i, j, k: (k, i)),
               b_spec=pl.BlockSpec((tk, tn), lambda i, j, k: (k, j)),
               o_specs=[pl.BlockSpec((tm, tn), lambda i, j, k: (i, j))],
               out_shapes=[jax.ShapeDtypeStruct((M, N), out_dtype)], acc_shape=(tm, tn), cdims=TN)[0]


def mm_nn_cb(name, a, b, out_dtype=F32, epi=None, out_dtypes=None, tm=1024):
    M, K = a.shape
    nb, _, c = b.shape
    tm = _tile(M, tm)
    odt = out_dtypes or [out_dtype]
    return _mm(name, a, b, grid=(M // tm, nb, 1),
               a_spec=pl.BlockSpec((tm, K), lambda i, j, k: (i, 0)),
               b_spec=pl.BlockSpec((None, K, c), lambda i, j, k: (j, 0, 0)),
               o_specs=[pl.BlockSpec((tm, c), lambda i, j, k: (i, j))] * len(odt),
               out_shapes=[jax.ShapeDtypeStruct((M, nb * c), d) for d in odt], acc_shape=(tm, c), cdims=NN, epi=epi)


def mm_nt_cb(name, a, b, out_dtype=F32, epi=None, extras=(), tm=1024, tn=512):
    M = a.shape[0]
    nb, K, c = b.shape
    tm, tn = _tile(M, tm), _tile(K, tn)
    o_spec = pl.BlockSpec((tm, tn), lambda i, j, k: (i, j))
    return _mm(name, a, b, grid=(M // tm, K // tn, nb),
               a_spec=pl.BlockSpec((tm, c), lambda i, j, k: (i, k)),
               b_spec=pl.BlockSpec((None, tn, c), lambda i, j, k: (k, j, 0)),
               o_specs=[o_spec], out_shapes=[jax.ShapeDtypeStruct((M, K), out_dtype)],
               acc_shape=(tm, tn), cdims=NT, epi=epi, extras=extras, extra_specs=[o_spec] * len(extras))


def mm_tn_cb(name, a, b, nb, out_dtype=BF16, tm=512, tk=1024):
    K, M = a.shape
    N = b.shape[1]
    c = N // nb
    tm, tk = _tile(M, tm), _tile(K, tk)
    return _mm(name, a, b, grid=(M // tm, nb, K // tk),
               a_spec=pl.BlockSpec((tk, tm), lambda i, j, k: (k, i)),
               b_spec=pl.BlockSpec((tk, c), lambda i, j, k: (k, j)),
               o_specs=[pl.BlockSpec((None, tm, c), lambda i, j, k: (j, i, 0))],
               out_shapes=[jax.ShapeDtypeStruct((nb, M, c), out_dtype)], acc_shape=(tm, c), cdims=TN)[0]


def rms_fwd(name, x, w, out_dtype=BF16, tm=512):
    S, D = x.shape
    tm = _tile(S, tm)

    def body(x_ref, w_ref, o_ref):
        xv = x_ref[...]
        r = lax.rsqrt(jnp.mean(xv * xv, axis=-1, keepdims=True) + EPS)
        o_ref[...] = (xv * r * w_ref[...]).astype(o_ref.dtype)

    return pl.pallas_call(
        body, name=name, grid=(S // tm,),
        in_specs=[pl.BlockSpec((tm, D), lambda i: (i, 0)), pl.BlockSpec((1, D), lambda i: (0, 0))],
        out_specs=pl.BlockSpec((tm, D), lambda i: (i, 0)), out_shape=jax.ShapeDtypeStruct((S, D), out_dtype),
        compiler_params=_cp(("parallel",)),
    )(x, w.reshape(1, D))


def rms_bwd(name, x, w, dh, dres=None, tm=512):
    S, D = x.shape
    tm = _tile(S, tm)
    has_res = dres is not None

    def body(*refs):
        if has_res:
            x_ref, w_ref, dh_ref, dres_ref, dx_ref, dw_ref = refs
        else:
            x_ref, w_ref, dh_ref, dx_ref, dw_ref = refs
        xv = x_ref[...]
        dhv = dh_ref[...].astype(F32)
        r = lax.rsqrt(jnp.mean(xv * xv, axis=-1, keepdims=True) + EPS)
        g = dhv * w_ref[...]
        dx = r * g - xv * (r * r * r) * jnp.mean(xv * g, axis=-1, keepdims=True)
        if has_res:
            dx = dx + dres_ref[...]
        dx_ref[...] = dx

        @pl.when(pl.program_id(0) == 0)
        def _():
            dw_ref[...] = jnp.zeros_like(dw_ref)

        dw_ref[...] += jnp.sum(dhv * xv * r, axis=0, keepdims=True)

    row = pl.BlockSpec((tm, D), lambda i: (i, 0))
    vec = pl.BlockSpec((1, D), lambda i: (0, 0))
    args = [x, w.reshape(1, D), dh] + ([dres] if has_res else [])
    dx, dw = pl.pallas_call(
        body, name=name, grid=(S // tm,),
        in_specs=[row, vec, row] + ([row] if has_res else []),
        out_specs=[row, vec], out_shape=[jax.ShapeDtypeStruct((S, D), F32), jax.ShapeDtypeStruct((1, D), F32)],
        compiler_params=_cp(("arbitrary",)),
    )(*args)
    return dx, dw.reshape(D)


def loss_head(x, w, target, tm=512):
    S, D = x.shape
    tm = _tile(S, tm)

    def body(x_ref, w_ref, t_ref, loss_ref, dx_ref, dw_ref):
        xv = x_ref[...]
        wv = w_ref[...]
        r = lax.rsqrt(jnp.mean(xv * xv, axis=-1, keepdims=True) + EPS)
        y = xv * r * wv
        err = y - t_ref[...]
        dy = err * (1.0 / D)
        g = dy * wv
        dx_ref[...] = r * g - xv * (r * r * r) * jnp.mean(xv * g, axis=-1, keepdims=True)

        @pl.when(pl.program_id(0) == 0)
        def _():
            dw_ref[...] = jnp.zeros_like(dw_ref)
            loss_ref[...] = jnp.zeros_like(loss_ref)

        dw_ref[...] += jnp.sum(dy * xv * r, axis=0, keepdims=True)
        part = 0.5 * jnp.sum(jnp.mean(err * err, axis=-1, keepdims=True), axis=0, keepdims=True)
        loss_ref[...] += jnp.broadcast_to(part, loss_ref.shape)

    row = pl.BlockSpec((tm, D), lambda i: (i, 0))
    vec = pl.BlockSpec((1, D), lambda i: (0, 0))
    loss, dx, dw = pl.pallas_call(
        body, name="loss_head", grid=(S // tm,),
        in_specs=[row, vec, row],
        out_specs=[pl.BlockSpec((1, LANE), lambda i: (0, 0)), row, vec],
        out_shape=[jax.ShapeDtypeStruct((1, LANE), F32), jax.ShapeDtypeStruct((S, D), F32), jax.ShapeDtypeStruct((1, D), F32)],
        compiler_params=_cp(("arbitrary",)),
    )(x, w.reshape(1, D), target)
    return loss[0, 0], dx, dw.reshape(D)


def _hg_common(q, f, lbv, b_s, qf_s):
    sig = _sigmoid(f)
    fg = lbv + (1.0 - lbv) * sig
    kk = 1.0 - fg
    logf = jnp.log(fg)
    sq = _sigmoid(q)
    qf = q * sq
    ri = lax.broadcasted_iota(I32, (CHUNK, CHUNK), 0)
    ci = lax.broadcasted_iota(I32, (CHUNK, CHUNK), 1)
    tri = (ci <= ri).astype(F32)
    b = _dot(tri, logf, NN, precision=lax.Precision.HIGHEST)
    b_s[...] = b
    qf_s[...] = qf
    return sig, fg, kk, sq, qf, b


def _hg_intra(b, kk, b_s, qf_s, e_s):
    rows = lax.broadcasted_iota(I32, (CHUNK, LANE), 0)
    lanes = lax.broadcasted_iota(I32, (CHUNK, CHUNK), 1)

    def tbody(t, pt):
        bt = b_s[pl.ds(t, 1), :]
        qt = qf_s[pl.ds(t, 1), :]
        e = jnp.exp(jnp.where(rows <= t, bt - b, NEG))
        if e_s is not None:
            e_s[t] = e
        col = jnp.sum(e * kk * qt, axis=1, keepdims=True)
        return jnp.where(lanes == t, col, pt)

    return lax.fori_loop(0, CHUNK, tbody, jnp.zeros((CHUNK, CHUNK), F32))


def hgrn_fwd(proj, lb, nw):
    S = proj.shape[0]
    NC = S // CHUNK
    H = HG_HEADS

    def body(q_ref, f_ref, v_ref, g_ref, lb_ref, nw_ref, o_ref, st_out_ref, st, b_s, qf_s):
        c = pl.program_id(1)

        @pl.when(c == 0)
        def _():
            st[...] = jnp.zeros_like(st)

        st_in = st[...]
        st_out_ref[...] = st_in
        q, f, v, g = q_ref[...], f_ref[...], v_ref[...], g_ref[...]
        sig, fg, kk, sq, qf, b = _hg_common(q, f, lb_ref[...], b_s, qf_s)
        pt = _hg_intra(b, kk, b_s, qf_s, None)
        bl = b_s[pl.ds(CHUNK - 1, 1), :]
        vb = v.astype(BF16)
        o = _dot(pt.astype(BF16), vb, TN) + _dot((qf * jnp.exp(b)).astype(BF16), st_in.astype(BF16), NT)
        ke = kk * jnp.exp(bl - b)
        st[...] = st_in * jnp.exp(bl) + _dot(vb, ke.astype(BF16), TN)
        r = lax.rsqrt(jnp.mean(o * o, axis=-1, keepdims=True) + EPS)
        o_ref[...] = (o * r * nw_ref[...] * (g * _sigmoid(g))).astype(o_ref.dtype)

    def seg(off):
        return pl.BlockSpec((CHUNK, LANE), lambda h, c: (c, off // LANE + h))

    out, states = pl.pallas_call(
        body, name="hgrn_fwd", grid=(H, NC),
        in_specs=[seg(OFF_Q), seg(OFF_F), seg(OFF_V), seg(OFF_G),
                  pl.BlockSpec((1, LANE), lambda h, c: (0, h)), pl.BlockSpec((1, LANE), lambda h, c: (0, 0))],
        out_specs=[pl.BlockSpec((CHUNK, LANE), lambda h, c: (c, h)),
                   pl.BlockSpec((None, None, LANE, LANE), lambda h, c: (c, h, 0, 0))],
        out_shape=[jax.ShapeDtypeStruct((S, H * LANE), BF16), jax.ShapeDtypeStruct((NC, H, LANE, LANE), F32)],
        scratch_shapes=[pltpu.VMEM((LANE, LANE), F32), pltpu.VMEM((CHUNK, LANE), F32), pltpu.VMEM((CHUNK, LANE), F32)],
        compiler_params=_cp(("parallel", "arbitrary")),
    )(proj, proj, proj, proj, lb.reshape(1, H * LANE), nw.reshape(1, LANE))
    return out, states


def hgrn_bwd(proj, lb, nw, states, dout):
    S = proj.shape[0]
    NC = S // CHUNK
    H = HG_HEADS

    def body(q_ref, f_ref, v_ref, g_ref, lb_ref, nw_ref, st_ref, do_ref,
             dq_ref, df_ref, dv_ref, dg_ref, dlb_ref, dnw_ref, dst, b_s, qf_s, e_s, dqf_s):
        c = pl.program_id(1)

        @pl.when(c == 0)
        def _():
            dst[...] = jnp.zeros_like(dst)
            dlb_ref[...] = jnp.zeros_like(dlb_ref)
            dnw_ref[...] = jnp.zeros_like(dnw_ref)

        q, f, v, g = q_ref[...], f_ref[...], v_ref[...], g_ref[...]
        lbv = lb_ref[...]
        nwv = nw_ref[...]
        st_in = st_ref[...]
        dstv = dst[...]
        sig, fg, kk, sq, qf, b = _hg_common(q, f, lbv, b_s, qf_s)
        pt = _hg_intra(b, kk, b_s, qf_s, e_s)
        bl = b_s[pl.ds(CHUNK - 1, 1), :]
        eb = jnp.exp(b)
        ebl = jnp.exp(bl - b)
        el = jnp.exp(bl)
        qe = qf * eb
        ke = kk * ebl
        vb = v.astype(BF16)
        ptb = pt.astype(BF16)
        stb = st_in.astype(BF16)
        dstb = dstv.astype(BF16)
        o = _dot(ptb, vb, TN) + _dot(qe.astype(BF16), stb, NT)
        dov = do_ref[...].astype(F32)
        sg = _sigmoid(g)
        r = lax.rsqrt(jnp.mean(o * o, axis=-1, keepdims=True) + EPS)
        on = o * r * nwv
        dg_ref[...] = (dov * on * _dsilu(g, sg)).astype(dg_ref.dtype)
        don = dov * (g * sg)
        dnw_ref[...] += jnp.broadcast_to(jnp.sum(don * o * r, axis=0, keepdims=True), dnw_ref.shape)
        gno = don * nwv
        do = r * gno - o * (r * r * r) * jnp.mean(o * gno, axis=-1, keepdims=True)
        dob = do.astype(BF16)
        dpt = _dot(vb, dob, NT)
        dv_ref[...] = (_dot(ptb, dob, NN) + _dot(ke.astype(BF16), dstb, NT)).astype(dv_ref.dtype)
        w_ = _dot(vb, dstb, NN)
        dqe = _dot(dob, stb, NN)
        lanes = lax.broadcasted_iota(I32, (CHUNK, CHUNK), 1)

        def tbody(t, dkk):
            col = jnp.sum(jnp.where(lanes == t, dpt, 0.0), axis=1, keepdims=True)
            xe = col * e_s[t]
            dqf_s[pl.ds(t, 1), :] = jnp.sum(xe * kk, axis=0, keepdims=True)
            return dkk + xe * qf_s[pl.ds(t, 1), :]

        dkk = lax.fori_loop(0, CHUNK, tbody, jnp.zeros((CHUNK, LANE), F32))
        dqf = dqf_s[...] + eb * dqe
        dkk = dkk + ebl * w_
        db = qf * dqf - kk * dkk
        dbl = el * jnp.sum(st_in * dstv, axis=0, keepdims=True) + jnp.sum(ke * w_, axis=0, keepdims=True)
        rows = lax.broadcasted_iota(I32, (CHUNK, LANE), 0)
        db = db + jnp.where(rows == CHUNK - 1, dbl, 0.0)
        ri = lax.broadcasted_iota(I32, (CHUNK, CHUNK), 0)
        ci = lax.broadcasted_iota(I32, (CHUNK, CHUNK), 1)
        triu = (ci >= ri).astype(F32)
        dlogf = _dot(triu, db, NN, precision=lax.Precision.HIGHEST)
        dfg = dlogf / fg - dkk
        df_ref[...] = (dfg * (1.0 - lbv) * sig * (1.0 - sig)).astype(df_ref.dtype)
        dlb_ref[...] += jnp.broadcast_to(jnp.sum(dfg * (1.0 - sig), axis=0, keepdims=True), dlb_ref.shape)
        dq_ref[...] = (dqf * _dsilu(q, sq)).astype(dq_ref.dtype)
        dst[...] = dstv * el + _dot(dob, qe.astype(BF16), TN)

    def seg(off):
        return pl.BlockSpec((CHUNK, LANE), lambda h, c: (NC - 1 - c, off // LANE + h))

    blk = pl.BlockSpec((CHUNK, LANE), lambda h, c: (NC - 1 - c, h))
    osd = jax.ShapeDtypeStruct((S, H * LANE), BF16)
    dq, df, dv, dg, dlb, dnw = pl.pallas_call(
        body, name="hgrn_bwd", grid=(H, NC),
        in_specs=[seg(OFF_Q), seg(OFF_F), seg(OFF_V), seg(OFF_G),
                  pl.BlockSpec((1, LANE), lambda h, c: (0, h)), pl.BlockSpec((1, LANE), lambda h, c: (0, 0)),
                  pl.BlockSpec((None, None, LANE, LANE), lambda h, c: (NC - 1 - c, h, 0, 0)), blk],
        out_specs=[blk, blk, blk, blk,
                   pl.BlockSpec((8, LANE), lambda h, c: (0, h)), pl.BlockSpec((8, LANE), lambda h, c: (h, 0))],
        out_shape=[osd, osd, osd, osd, jax.ShapeDtypeStruct((8, H * LANE), F32), jax.ShapeDtypeStruct((8 * H, LANE), F32)],
        scratch_shapes=[pltpu.VMEM((LANE, LANE), F32), pltpu.VMEM((CHUNK, LANE), F32), pltpu.VMEM((CHUNK, LANE), F32),
                        pltpu.VMEM((CHUNK, CHUNK, LANE), F32), pltpu.VMEM((CHUNK, LANE), F32)],
        compiler_params=_cp(("parallel", "arbitrary")),
    )(proj, proj, proj, proj, lb.reshape(1, H * LANE), nw.reshape(1, LANE), states, dout)
    return dq, df, dv, dg, dlb[0], jnp.sum(dnw.reshape(H, 8, LANE)[:, 0], axis=0)


CV_PAD = 32
ROWS = 256


def _colblk(S, off):
    return pl.BlockSpec((S, LANE), lambda j: (0, off // LANE + j))


def cv_fwd(proj, w32, bias):
    S = proj.shape[0]
    nchunk = S // ROWS

    def body(a_ref, g_ref, w_ref, b_ref, o_ref, zpad):
        zpad[pl.ds(0, CV_PAD), :] = jnp.zeros((CV_PAD, LANE), F32)

        def glu(c, _):
            r0 = pl.multiple_of(c * ROWS, ROWS)
            zpad[pl.ds(CV_PAD + r0, ROWS), :] = a_ref[pl.ds(r0, ROWS), :] * _sigmoid(g_ref[pl.ds(r0, ROWS), :])
            return 0

        lax.fori_loop(0, nchunk, glu, 0)

        def conv(c, _):
            r0 = pl.multiple_of(c * ROWS, ROWS)
            acc = jnp.broadcast_to(b_ref[...], (ROWS, LANE))
            for j in range(CV_KERNEL):
                acc = acc + w_ref[pl.ds(j, 1), :] * zpad[pl.ds(r0 + (CV_PAD - CV_KERNEL + 1) + j, ROWS), :]
            o_ref[pl.ds(r0, ROWS), :] = acc
            return 0

        lax.fori_loop(0, nchunk, conv, 0)

    return pl.pallas_call(
        body, name="cv_fwd", grid=(BRANCH_W // LANE,),
        in_specs=[_colblk(S, OFF_CV), _colblk(S, OFF_CV + BRANCH_W),
                  pl.BlockSpec((32, LANE), lambda j: (0, j)), pl.BlockSpec((1, LANE), lambda j: (0, j))],
        out_specs=pl.BlockSpec((S, LANE), lambda j: (0, j)), out_shape=jax.ShapeDtypeStruct((S, BRANCH_W), F32),
        scratch_shapes=[pltpu.VMEM((CV_PAD + S, LANE), F32)],
        compiler_params=_cp(("parallel",)),
    )(proj, proj, w32, bias.reshape(1, BRANCH_W))


def cv_bwd(proj, w32, dzc):
    S = proj.shape[0]
    nchunk = S // ROWS

    def body(a_ref, g_ref, w_ref, dz_ref, da_ref, dg_ref, dw_ref, db_ref, zpad, dpad):
        zpad[pl.ds(0, CV_PAD), :] = jnp.zeros((CV_PAD, LANE), F32)
        dpad[pl.ds(S, CV_PAD), :] = jnp.zeros((CV_PAD, LANE), F32)
        dw_ref[...] = jnp.zeros_like(dw_ref)

        def glu(c, dsum):
            r0 = pl.multiple_of(c * ROWS, ROWS)
            zpad[pl.ds(CV_PAD + r0, ROWS), :] = a_ref[pl.ds(r0, ROWS), :] * _sigmoid(g_ref[pl.ds(r0, ROWS), :])
            d = dz_ref[pl.ds(r0, ROWS), :]
            dpad[pl.ds(r0, ROWS), :] = d
            return dsum + jnp.sum(d, axis=0, keepdims=True)

        dsum = lax.fori_loop(0, nchunk, glu, jnp.zeros((1, LANE), F32))
        db_ref[...] = jnp.broadcast_to(dsum, db_ref.shape)

        def conv(c, _):
            r0 = pl.multiple_of(c * ROWS, ROWS)
            d = dpad[pl.ds(r0, ROWS), :]
            acc = jnp.zeros((ROWS, LANE), F32)
            for j in range(CV_KERNEL):
                acc = acc + w_ref[pl.ds(j, 1), :] * dpad[pl.ds(r0 + (CV_KERNEL - 1) - j, ROWS), :]
                zs = zpad[pl.ds(r0 + (CV_PAD - CV_KERNEL + 1) + j, ROWS), :]
                dw_ref[pl.ds(j, 1), :] += jnp.sum(d * zs, axis=0, keepdims=True)
            a = a_ref[pl.ds(r0, ROWS), :]
            sg = _sigmoid(g_ref[pl.ds(r0, ROWS), :])
            da_ref[pl.ds(r0, ROWS), :] = (acc * sg).astype(da_ref.dtype)
            dg_ref[pl.ds(r0, ROWS), :] = (acc * a * sg * (1.0 - sg)).astype(dg_ref.dtype)
            return 0

        lax.fori_loop(0, nchunk, conv, 0)

    blk = pl.BlockSpec((S, LANE), lambda j: (0, j))
    da, dg, dw, db = pl.pallas_call(
        body, name="cv_bwd", grid=(BRANCH_W // LANE,),
        in_specs=[_colblk(S, OFF_CV), _colblk(S, OFF_CV + BRANCH_W), pl.BlockSpec((32, LANE), lambda j: (0, j)), blk],
        out_specs=[blk, blk, pl.BlockSpec((32, LANE), lambda j: (0, j)), pl.BlockSpec((8, LANE), lambda j: (0, j))],
        out_shape=[jax.ShapeDtypeStruct((S, BRANCH_W), BF16), jax.ShapeDtypeStruct((S, BRANCH_W), BF16),
                   jax.ShapeDtypeStruct((32, BRANCH_W), F32), jax.ShapeDtypeStruct((8, BRANCH_W), F32)],
        scratch_shapes=[pltpu.VMEM((CV_PAD + S, LANE), F32), pltpu.VMEM((S + CV_PAD, LANE), F32)],
        compiler_params=_cp(("parallel",)),
    )(proj, proj, w32, dzc)
    return da, dg, dw, db[0]


def ln_silu_fwd(z, w, b, tm=512):
    S, C = z.shape
    tm = _tile(S, tm)

    def body(z_ref, w_ref, b_ref, o_ref):
        zv = z_ref[...]
        mu = jnp.mean(zv, axis=-1, keepdims=True)
        zc = zv - mu
        rstd = lax.rsqrt(jnp.mean(zc * zc, axis=-1, keepdims=True) + EPS)
        y = zc * rstd * w_ref[...] + b_ref[...]
        o_ref[...] = (y * _sigmoid(y)).astype(o_ref.dtype)

    row = pl.BlockSpec((tm, C), lambda i: (i, 0))
    vec = pl.BlockSpec((1, C), lambda i: (0, 0))
    return pl.pallas_call(
        body, name="ln_silu_fwd", grid=(S // tm,), in_specs=[row, vec, vec], out_specs=row,
        out_shape=jax.ShapeDtypeStruct((S, C), BF16), compiler_params=_cp(("parallel",)),
    )(z, w.reshape(1, C), b.reshape(1, C))


def ln_silu_bwd(z, w, b, dout, tm=512):
    S, C = z.shape
    tm = _tile(S, tm)

    def body(z_ref, w_ref, b_ref, do_ref, dz_ref, dw_ref, db_ref):
        zv = z_ref[...]
        wv = w_ref[...]
        mu = jnp.mean(zv, axis=-1, keepdims=True)
        zc = zv - mu
        rstd = lax.rsqrt(jnp.mean(zc * zc, axis=-1, keepdims=True) + EPS)
        xh = zc * rstd
        y = xh * wv + b_ref[...]
        dy = do_ref[...].astype(F32) * _dsilu(y, _sigmoid(y))

        @pl.when(pl.program_id(0) == 0)
        def _():
            dw_ref[...] = jnp.zeros_like(dw_ref)
            db_ref[...] = jnp.zeros_like(db_ref)

        dw_ref[...] += jnp.sum(dy * xh, axis=0, keepdims=True)
        db_ref[...] += jnp.sum(dy, axis=0, keepdims=True)
        dxh = dy * wv
        dz_ref[...] = rstd * (dxh - jnp.mean(dxh, axis=-1, keepdims=True) - xh * jnp.mean(dxh * xh, axis=-1, keepdims=True))

    row = pl.BlockSpec((tm, C), lambda i: (i, 0))
    vec = pl.BlockSpec((1, C), lambda i: (0, 0))
    dz, dw, db = pl.pallas_call(
        body, name="ln_silu_bwd", grid=(S // tm,), in_specs=[row, vec, vec, row], out_specs=[row, vec, vec],
        out_shape=[jax.ShapeDtypeStruct((S, C), F32), jax.ShapeDtypeStruct((1, C), F32), jax.ShapeDtypeStruct((1, C), F32)],
        compiler_params=_cp(("arbitrary",)),
    )(z, w.reshape(1, C), b.reshape(1, C), dout)
    return dz, dw.reshape(C), db.reshape(C)


PL_PAD = 16


def _pool_counts(r0, win):
    t = r0 + lax.broadcasted_iota(I32, (ROWS, LANE), 0)
    return jnp.minimum(t + 1, win).astype(F32)


def pool_fwd(proj, wg, scale):
    S = proj.shape[0]
    nchunk = S // ROWS

    def body(u_ref, w_ref, s_ref, o_ref, upad):
        g = pl.program_id(0)
        upad[pl.ds(0, PL_PAD), :] = jnp.zeros((PL_PAD, LANE), F32)

        def fill(c, _):
            r0 = pl.multiple_of(c * ROWS, ROWS)
            upad[pl.ds(PL_PAD + r0, ROWS), :] = u_ref[pl.ds(r0, ROWS), :]
            return 0

        lax.fori_loop(0, nchunk, fill, 0)
        wb = w_ref[...].astype(BF16)
        for gi, win in enumerate(POOL_WINDOWS):
            @pl.when(g == gi)
            def _(win=win):
                def chunk(c, _):
                    r0 = pl.multiple_of(c * ROWS, ROWS)
                    u = upad[pl.ds(PL_PAD + r0, ROWS), :]
                    ws = u
                    for j in range(1, win):
                        ws = ws + upad[pl.ds(PL_PAD + r0 - j, ROWS), :]
                    pooled = ws / _pool_counts(r0, win) - u
                    o_ref[pl.ds(r0, ROWS), :] = (_dot(pooled.astype(BF16), wb, NN) * s_ref[...]).astype(o_ref.dtype)
                    return 0

                lax.fori_loop(0, nchunk, chunk, 0)

    return pl.pallas_call(
        body, name="pool_fwd", grid=(len(POOL_WINDOWS),),
        in_specs=[_colblk(S, OFF_PL), pl.BlockSpec((None, LANE, LANE), lambda j: (j, 0, 0)), pl.BlockSpec((1, LANE), lambda j: (0, j))],
        out_specs=pl.BlockSpec((S, LANE), lambda j: (0, j)), out_shape=jax.ShapeDtypeStruct((S, BRANCH_W), BF16),
        scratch_shapes=[pltpu.VMEM((PL_PAD + S, LANE), F32)],
        compiler_params=_cp(("parallel",)),
    )(proj, wg, scale.reshape(1, BRANCH_W))


def pool_bwd(proj, wg, scale, dy):
    S = proj.shape[0]
    nchunk = S // ROWS

    def body(u_ref, w_ref, s_ref, dy_ref, du_ref, dw_ref, ds_ref, upad, dpn, nd):
        g = pl.program_id(0)
        upad[pl.ds(0, PL_PAD), :] = jnp.zeros((PL_PAD, LANE), F32)
        dpn[pl.ds(S, PL_PAD), :] = jnp.zeros((PL_PAD, LANE), F32)

        def fill(c, _):
            r0 = pl.multiple_of(c * ROWS, ROWS)
            upad[pl.ds(PL_PAD + r0, ROWS), :] = u_ref[pl.ds(r0, ROWS), :]
            return 0

        lax.fori_loop(0, nchunk, fill, 0)
        wb = w_ref[...].astype(BF16)
        sv = s_ref[...]
        for gi, win in enumerate(POOL_WINDOWS):
            @pl.when(g == gi)
            def _(win=win):
                def chunk(c, carry):
                    dw, dsc = carry
                    r0 = pl.multiple_of(c * ROWS, ROWS)
                    u = upad[pl.ds(PL_PAD + r0, ROWS), :]
                    ws = u
                    for j in range(1, win):
                        ws = ws + upad[pl.ds(PL_PAD + r0 - j, ROWS), :]
                    cnt = _pool_counts(r0, win)
                    pooled = (ws / cnt - u).astype(BF16)
                    dyv = dy_ref[pl.ds(r0, ROWS), :].astype(F32)
                    dsc = dsc + jnp.sum(dyv * _dot(pooled, wb, NN), axis=0, keepdims=True)
                    dys = (dyv * sv).astype(BF16)
                    dw = dw + _dot(pooled, dys, TN)
                    dp = _dot(dys, wb, NT)
                    dpn[pl.ds(r0, ROWS), :] = dp / cnt
                    nd[pl.ds(r0, ROWS), :] = -dp
                    return dw, dsc

                dw, dsc = lax.fori_loop(0, nchunk, chunk, (jnp.zeros((LANE, LANE), F32), jnp.zeros((1, LANE), F32)))
                dw_ref[...] = dw
                ds_ref[...] = jnp.broadcast_to(dsc, ds_ref.shape)

                def spread(c, _):
                    r0 = pl.multiple_of(c * ROWS, ROWS)
                    acc = nd[pl.ds(r0, ROWS), :]
                    for j in range(win):
                        acc = acc + dpn[pl.ds(r0 + j, ROWS), :]
                    du_ref[pl.ds(r0, ROWS), :] = acc.astype(du_ref.dtype)
                    return 0

                lax.fori_loop(0, nchunk, spread, 0)

    blk = pl.BlockSpec((S, LANE), lambda j: (0, j))
    du, dw, ds = pl.pallas_call(
        body, name="pool_bwd", grid=(len(POOL_WINDOWS),),
        in_specs=[_colblk(S, OFF_PL), pl.BlockSpec((None, LANE, LANE), lambda j: (j, 0, 0)), pl.BlockSpec((1, LANE), lambda j: (0, j)), blk],
        out_specs=[blk, pl.BlockSpec((None, LANE, LANE), lambda j: (j, 0, 0)), pl.BlockSpec((8, LANE), lambda j: (0, j))],
        out_shape=[jax.ShapeDtypeStruct((S, BRANCH_W), BF16), jax.ShapeDtypeStruct((len(POOL_WINDOWS), LANE, LANE), F32),
                   jax.ShapeDtypeStruct((8, BRANCH_W), F32)],
        scratch_shapes=[pltpu.VMEM((PL_PAD + S, LANE), F32), pltpu.VMEM((S + PL_PAD, LANE), F32), pltpu.VMEM((S, LANE), F32)],
        compiler_params=_cp(("parallel",)),
    )(proj, wg, scale.reshape(1, BRANCH_W), dy)
    return du, dw, ds[0]


LR_PAD = 8
GELU_C = math.sqrt(2.0 / math.pi)
GELU_A = 0.044715


def _gelu(y):
    return 0.5 * y * (1.0 + jnp.tanh(GELU_C * (y + GELU_A * y * y * y)))


def _dgelu(y):
    t = jnp.tanh(GELU_C * (y + GELU_A * y * y * y))
    return 0.5 * (1.0 + t) + 0.5 * y * (1.0 - t * t) * GELU_C * (1.0 + 3.0 * GELU_A * y * y)


def _lru_gates(xpad, r0, cw_ref, cb, wa, ba, wx, bx, sp8):
    xc = jnp.broadcast_to(cb, (ROWS, LANE))
    for j in range(LRU_CONV):
        xc = xc + cw_ref[pl.ds(j, 1), :] * xpad[pl.ds(r0 + (LR_PAD - LRU_CONV + 1) + j, ROWS), :]
    xb = xc.astype(BF16)
    r = _sigmoid(_dot(xb, wa, NN) + ba)
    ig = _sigmoid(_dot(xb, wx, NN) + bx)
    la = -sp8 * r
    a = jnp.exp(la)
    s = jnp.sqrt(-jnp.tanh(la) * (a * a + 1.0))
    return xc, r, ig, a, s


def lru_fwd(proj, cw8, cb, wa_bd, ba, wx_bd, bx, sp8):
    S = proj.shape[0]
    nchunk = S // ROWS

    def body(x_ref, y_ref, cw_ref, cb_ref, wa_ref, ba_ref, wx_ref, bx_ref, sp_ref, o_ref, h_ref, xpad, a_s):
        xpad[pl.ds(0, LR_PAD), :] = jnp.zeros((LR_PAD, LANE), F32)

        def fill(c, _):
            r0 = pl.multiple_of(c * ROWS, ROWS)
            xpad[pl.ds(LR_PAD + r0, ROWS), :] = x_ref[pl.ds(r0, ROWS), :]
            return 0

        lax.fori_loop(0, nchunk, fill, 0)
        wa = wa_ref[...].astype(BF16)
        wx = wx_ref[...].astype(BF16)

        def gates(c, _):
            r0 = pl.multiple_of(c * ROWS, ROWS)
            xc, r, ig, a, s = _lru_gates(xpad, r0, cw_ref, cb_ref[...], wa, ba_ref[...], wx, bx_ref[...], sp_ref[...])
            a_s[pl.ds(r0, ROWS), :] = a
            h_ref[pl.ds(r0, ROWS), :] = s * (ig * xc)
            return 0

        lax.fori_loop(0, nchunk, gates, 0)

        def scan(t, h):
            h = a_s[pl.ds(t, 1), :] * h + h_ref[pl.ds(t, 1), :]
            h_ref[pl.ds(t, 1), :] = h
            return h

        lax.fori_loop(0, S, scan, jnp.zeros((1, LANE), F32), unroll=8)

        def gate_out(c, _):
            r0 = pl.multiple_of(c * ROWS, ROWS)
            o_ref[pl.ds(r0, ROWS), :] = (h_ref[pl.ds(r0, ROWS), :] * _gelu(y_ref[pl.ds(r0, ROWS), :])).astype(o_ref.dtype)
            return 0

        lax.fori_loop(0, nchunk, gate_out, 0)

    vec = pl.BlockSpec((1, LANE), lambda j: (0, j))
    mat = pl.BlockSpec((None, LANE, LANE), lambda j: (j, 0, 0))
    blk = pl.BlockSpec((S, LANE), lambda j: (0, j))
    return pl.pallas_call(
        body, name="lru_fwd", grid=(BRANCH_W // LANE,),
        in_specs=[_colblk(S, OFF_LX), _colblk(S, OFF_LY), pl.BlockSpec((8, LANE), lambda j: (0, j)), vec, mat, vec, mat, vec, vec],
        out_specs=[blk, blk],
        out_shape=[jax.ShapeDtypeStruct((S, BRANCH_W), BF16), jax.ShapeDtypeStruct((S, BRANCH_W), F32)],
        scratch_shapes=[pltpu.VMEM((LR_PAD + S, LANE), F32), pltpu.VMEM((S, LANE), F32)],
        compiler_params=_cp(("parallel",)),
    )(proj, proj, cw8, cb.reshape(1, -1), wa_bd, ba.reshape(1, -1), wx_bd, bx.reshape(1, -1), sp8.reshape(1, -1))


def lru_bwd(proj, cw8, cb, wa_bd, ba, wx_bd, bx, sp8, h, dout):
    S = proj.shape[0]
    nchunk = S // ROWS

    def body(x_ref, y_ref, cw_ref, cb_ref, wa_ref, ba_ref, wx_ref, bx_ref, sp_ref, h_ref, do_ref,
             dx_ref, dy_ref, dcw_ref, dcb_ref, dwa_ref, dba_ref, dwx_ref, dbx_ref, dsp_ref,
             xpad, a_s, g_s, hpad, dxc):
        xpad[pl.ds(0, LR_PAD), :] = jnp.zeros((LR_PAD, LANE), F32)
        hpad[pl.ds(0, LR_PAD), :] = jnp.zeros((LR_PAD, LANE), F32)
        dxc[pl.ds(S, LR_PAD), :] = jnp.zeros((LR_PAD, LANE), F32)
        dcw_ref[...] = jnp.zeros_like(dcw_ref)
        wa = wa_ref[...].astype(BF16)
        wx = wx_ref[...].astype(BF16)
        cbv, bav, bxv, spv = cb_ref[...], ba_ref[...], bx_ref[...], sp_ref[...]

        def fill(c, _):
            r0 = pl.multiple_of(c * ROWS, ROWS)
            xpad[pl.ds(LR_PAD + r0, ROWS), :] = x_ref[pl.ds(r0, ROWS), :]
            hv = h_ref[pl.ds(r0, ROWS), :]
            hpad[pl.ds(LR_PAD + r0, ROWS), :] = hv
            yv = y_ref[pl.ds(r0, ROWS), :]
            dov = do_ref[pl.ds(r0, ROWS), :].astype(F32)
            g_s[pl.ds(r0, ROWS), :] = dov * _gelu(yv)
            dy_ref[pl.ds(r0, ROWS), :] = (dov * hv * _dgelu(yv)).astype(dy_ref.dtype)
            return 0

        lax.fori_loop(0, nchunk, fill, 0)

        def gates(c, _):
            r0 = pl.multiple_of(c * ROWS, ROWS)
            _, _, _, a, _ = _lru_gates(xpad, r0, cw_ref, cbv, wa, bav, wx, bxv, spv)
            a_s[pl.ds(r0, ROWS), :] = a
            return 0

        lax.fori_loop(0, nchunk, gates, 0)

        def rscan(i, carry):
            t = S - 1 - i
            gt = g_s[pl.ds(t, 1), :] + carry
            g_s[pl.ds(t, 1), :] = gt
            return a_s[pl.ds(t, 1), :] * gt

        lax.fori_loop(0, S, rscan, jnp.zeros((1, LANE), F32), unroll=8)

        def chain(c, carry):
            dwa, dwx, dba, dbx, dsp, dcb = carry
            r0 = pl.multiple_of(c * ROWS, ROWS)
            xc, r, ig, a, s = _lru_gates(xpad, r0, cw_ref, cbv, wa, bav, wx, bxv, spv)
            gt = g_s[pl.ds(r0, ROWS), :]
            hprev = hpad[pl.ds(r0 + LR_PAD - 1, ROWS), :]
            da = gt * hprev - gt * ig * xc * (a / s)
            dig = gt * s * xc
            dla = da * a
            dsp = dsp + jnp.sum(-dla * r, axis=0, keepdims=True)
            dpr = (-dla * spv) * r * (1.0 - r)
            dpi = dig * ig * (1.0 - ig)
            dprb, dpib, xb = dpr.astype(BF16), dpi.astype(BF16), xc.astype(BF16)
            d = gt * s * ig + _dot(dprb, wa, NT) + _dot(dpib, wx, NT)
            dwa = dwa + _dot(xb, dprb, TN)
            dwx = dwx + _dot(xb, dpib, TN)
            dba = dba + jnp.sum(dpr, axis=0, keepdims=True)
            dbx = dbx + jnp.sum(dpi, axis=0, keepdims=True)
            dcb = dcb + jnp.sum(d, axis=0, keepdims=True)
            dxc[pl.ds(r0, ROWS), :] = d
            for j in range(LRU_CONV):
                xs = xpad[pl.ds(r0 + (LR_PAD - LRU_CONV + 1) + j, ROWS), :]
                dcw_ref[pl.ds(j, 1), :] += jnp.sum(d * xs, axis=0, keepdims=True)
            return dwa, dwx, dba, dbx, dsp, dcb

        zm, zv = jnp.zeros((LANE, LANE), F32), jnp.zeros((1, LANE), F32)
        dwa, dwx, dba, dbx, dsp, dcb = lax.fori_loop(0, nchunk, chain, (zm, zm, zv, zv, zv, zv))
        dwa_ref[...] = dwa
        dwx_ref[...] = dwx
        dba_ref[...] = jnp.broadcast_to(dba, dba_ref.shape)
        dbx_ref[...] = jnp.broadcast_to(dbx, dbx_ref.shape)
        dsp_ref[...] = jnp.broadcast_to(dsp, dsp_ref.shape)
        dcb_ref[...] = jnp.broadcast_to(dcb, dcb_ref.shape)

        def convt(c, _):
            r0 = pl.multiple_of(c * ROWS, ROWS)
            acc = jnp.zeros((ROWS, LANE), F32)
            for j in range(LRU_CONV):
                acc = acc + cw_ref[pl.ds(j, 1), :] * dxc[pl.ds(r0 + (LRU_CONV - 1) - j, ROWS), :]
            dx_ref[pl.ds(r0, ROWS), :] = acc.astype(dx_ref.dtype)
            return 0

        lax.fori_loop(0, nchunk, convt, 0)

    vec = pl.BlockSpec((1, LANE), lambda j: (0, j))
    vec8 = pl.BlockSpec((8, LANE), lambda j: (0, j))
    mat = pl.BlockSpec((None, LANE, LANE), lambda j: (j, 0, 0))
    blk = pl.BlockSpec((S, LANE), lambda j: (0, j))
    nblk = BRANCH_W // LANE
    v8 = jax.ShapeDtypeStruct((8, BRANCH_W), F32)
    m4 = jax.ShapeDtypeStruct((nblk, LANE, LANE), F32)
    big = jax.ShapeDtypeStruct((S, BRANCH_W), BF16)
    seq = pltpu.VMEM((S, LANE), F32)
    dx, dy, dcw, dcb, dwa, dba, dwx, dbx, dsp = pl.pallas_call(
        body, name="lru_bwd", grid=(nblk,),
        in_specs=[_colblk(S, OFF_LX), _colblk(S, OFF_LY), vec8, vec, mat, vec, mat, vec, vec, blk, blk],
        out_specs=[blk, blk, vec8, vec8, mat, vec8, mat, vec8, vec8],
        out_shape=[big, big, v8, v8, m4, v8, m4, v8, v8],
        scratch_shapes=[pltpu.VMEM((LR_PAD + S, LANE), F32), seq, seq, pltpu.VMEM((LR_PAD + S, LANE), F32),
                        pltpu.VMEM((S + LR_PAD, LANE), F32)],
        compiler_params=_cp(("parallel",)),
    )(proj, proj, cw8, cb.reshape(1, -1), wa_bd, ba.reshape(1, -1), wx_bd, bx.reshape(1, -1), sp8.reshape(1, -1), h, dout)
    return dx, dy, dcw, dcb[0], dwa, dba[0], dwx, dbx[0], dsp[0]


MG_COLS = 512
N_BRANCH = 4


def _gate_spec(tm, k):
    return pl.BlockSpec((tm, MG_COLS), lambda j, i: (i, (OFF_GATE + k * D_MODEL) // MG_COLS + j))


def merge_fwd(ups, proj, gate_b, tm=256):
    S = proj.shape[0]
    tm = _tile(S, tm)

    def body(u0, u1, u2, u3, g0, g1, g2, g3, gb_ref, o_ref):
        acc = jnp.zeros((tm, MG_COLS), F32)
        for k, (u, g) in enumerate(((u0, g0), (u1, g1), (u2, g2), (u3, g3))):
            acc = acc + _sigmoid(g[...] + gb_ref[pl.ds(k, 1), :]) * u[...]
        o_ref[...] = acc.astype(o_ref.dtype)

    blk = pl.BlockSpec((tm, MG_COLS), lambda j, i: (i, j))
    return pl.pallas_call(
        body, name="merge_fwd", grid=(D_MODEL // MG_COLS, S // tm),
        in_specs=[blk] * N_BRANCH + [_gate_spec(tm, k) for k in range(N_BRANCH)] + [pl.BlockSpec((N_BRANCH, MG_COLS), lambda j, i: (0, j))],
        out_specs=blk, out_shape=jax.ShapeDtypeStruct((S, D_MODEL), BF16),
        compiler_params=_cp(("parallel", "parallel")),
    )(*ups, proj, proj, proj, proj, gate_b)


def merge_bwd(dmerged, ups, proj, gate_b, tm=256):
    S = proj.shape[0]
    tm = _tile(S, tm)

    def body(dm_ref, u0, u1, u2, u3, g0, g1, g2, g3, gb_ref, du0, du1, du2, du3, dg0, dg1, dg2, dg3, dgb_ref):
        @pl.when(pl.program_id(1) == 0)
        def _():
            dgb_ref[...] = jnp.zeros_like(dgb_ref)

        dm = dm_ref[...].astype(F32)
        for k, (u, g, du, dg) in enumerate(((u0, g0, du0, dg0), (u1, g1, du1, dg1), (u2, g2, du2, dg2), (u3, g3, du3, dg3))):
            sg = _sigmoid(g[...] + gb_ref[pl.ds(k, 1), :])
            du[...] = (dm * sg).astype(du.dtype)
            dgk = dm * u[...] * sg * (1.0 - sg)
            dg[...] = dgk.astype(dg.dtype)
            dgb_ref[pl.ds(8 * k, 8), :] += jnp.broadcast_to(jnp.sum(dgk, axis=0, keepdims=True), (8, MG_COLS))

    blk = pl.BlockSpec((tm, MG_COLS), lambda j, i: (i, j))
    big = jax.ShapeDtypeStruct((S, D_MODEL), BF16)
    outs = pl.pallas_call(
        body, name="merge_bwd", grid=(D_MODEL // MG_COLS, S // tm),
        in_specs=[blk] * (1 + N_BRANCH) + [_gate_spec(tm, k) for k in range(N_BRANCH)] + [pl.BlockSpec((N_BRANCH, MG_COLS), lambda j, i: (0, j))],
        out_specs=[blk] * (2 * N_BRANCH) + [pl.BlockSpec((8 * N_BRANCH, MG_COLS), lambda j, i: (0, j))],
        out_shape=[big] * (2 * N_BRANCH) + [jax.ShapeDtypeStruct((8 * N_BRANCH, D_MODEL), F32)],
        compiler_params=_cp(("parallel", "arbitrary")),
    )(dmerged, *ups, proj, proj, proj, proj, gate_b)
    return outs[:N_BRANCH], outs[N_BRANCH:2 * N_BRANCH], outs[-1].reshape(N_BRANCH, 8, D_MODEL)[:, 0]


def attn_fwd(q, kv, tm=512):
    S = q.shape[0]
    M = kv.shape[0]
    tm = _tile(S, tm)
    scale = XA_HD ** -0.5

    def body(q_ref, kv_ref, o_ref):
        for hh in range(XA_HEADS):
            cs = pl.ds(hh * XA_HD, XA_HD)
            qh = q_ref[:, cs]
            kh = kv_ref[:, cs]
            vh = kv_ref[:, pl.ds(D_MODEL + hh * XA_HD, XA_HD)]
            s = _dot(qh, kh, NT) * scale
            p = jnp.exp(s - jnp.max(s, axis=-1, keepdims=True))
            p = p / jnp.sum(p, axis=-1, keepdims=True)
            o_ref[:, cs] = _dot(p.astype(BF16), vh, NN).astype(o_ref.dtype)

    return pl.pallas_call(
        body, name="attn_fwd", grid=(S // tm,),
        in_specs=[pl.BlockSpec((tm, D_MODEL), lambda i: (i, 0)), pl.BlockSpec((M, 2 * D_MODEL), lambda i: (0, 0))],
        out_specs=pl.BlockSpec((tm, D_MODEL), lambda i: (i, 0)), out_shape=jax.ShapeDtypeStruct((S, D_MODEL), BF16),
        compiler_params=_cp(("parallel",)),
    )(q, kv)


def attn_bwd(q, kv, do, tm=512):
    S = q.shape[0]
    M = kv.shape[0]
    tm = _tile(S, tm)
    scale = XA_HD ** -0.5

    def body(q_ref, kv_ref, do_ref, dq_ref, dkv_ref):
        @pl.when(pl.program_id(0) == 0)
        def _():
            dkv_ref[...] = jnp.zeros_like(dkv_ref)

        for hh in range(XA_HEADS):
            cs = pl.ds(hh * XA_HD, XA_HD)
            vs = pl.ds(D_MODEL + hh * XA_HD, XA_HD)
            qh = q_ref[:, cs]
            kh = kv_ref[:, cs]
            vh = kv_ref[:, vs]
            doh = do_ref[:, cs]
            s = _dot(qh, kh, NT) * scale
            p = jnp.exp(s - jnp.max(s, axis=-1, keepdims=True))
            p = p / jnp.sum(p, axis=-1, keepdims=True)
            dp = _dot(doh, vh, NT)
            ds = (p * (dp - jnp.sum(dp * p, axis=-1, keepdims=True)) * scale).astype(BF16)
            dq_ref[:, cs] = _dot(ds, kh, NN).astype(dq_ref.dtype)
            dkv_ref[:, cs] += _dot(ds, qh, TN)
            dkv_ref[:, vs] += _dot(p.astype(BF16), doh, TN)

    row = pl.BlockSpec((tm, D_MODEL), lambda i: (i, 0))
    full = pl.BlockSpec((M, 2 * D_MODEL), lambda i: (0, 0))
    return pl.pallas_call(
        body, name="attn_bwd", grid=(S // tm,), in_specs=[row, full, row], out_specs=[row, full],
        out_shape=[jax.ShapeDtypeStruct((S, D_MODEL), BF16), jax.ShapeDtypeStruct((M, 2 * D_MODEL), F32)],
        compiler_params=_cp(("arbitrary",)),
    )(q, kv, do)


def sum_parts(parts, tm=256):
    n, R, C = parts.shape
    tm = _tile(R, tm)

    def body(p_ref, o_ref):
        acc = p_ref[0].astype(F32)
        for j in range(1, n):
            acc = acc + p_ref[j].astype(F32)
        o_ref[...] = acc

    return pl.pallas_call(
        body, name="sum_parts", grid=(R // tm,),
        in_specs=[pl.BlockSpec((n, tm, C), lambda i: (0, i, 0))], out_specs=pl.BlockSpec((tm, C), lambda i: (i, 0)),
        out_shape=jax.ShapeDtypeStruct((R, C), F32), compiler_params=_cp(("parallel",)),
    )(parts)


def adamw(w, g, m, v, tm=256):
    R, C = w.shape
    tm = _tile(R, tm)
    c1 = 1.0 / (1.0 - ADAM_B1 ** ADAM_STEP)
    c2 = 1.0 / (1.0 - ADAM_B2 ** ADAM_STEP)

    def body(w_ref, g_ref, m_ref, v_ref, d_ref, nm_ref, nv_ref):
        gv = g_ref[...]
        nm = ADAM_B1 * m_ref[...] + (1.0 - ADAM_B1) * gv
        nv = ADAM_B2 * v_ref[...] + (1.0 - ADAM_B2) * (gv * gv)
        nm_ref[...] = nm
        nv_ref[...] = nv
        d_ref[...] = -ADAM_LR * ((nm * c1) / (jnp.sqrt(nv * c2) + ADAM_EPS) + ADAM_WD * w_ref[...])

    blk = pl.BlockSpec((tm, C), lambda i: (i, 0))
    sd = jax.ShapeDtypeStruct((R, C), F32)
    return pl.pallas_call(
        body, name="adamw", grid=(R // tm,), in_specs=[blk] * 4, out_specs=[blk] * 3, out_shape=[sd] * 3,
        compiler_params=_cp(("parallel",)),
    )(w, g, m, v)


ANY = pl.BlockSpec(memory_space=pl.ANY)


def _place():
    return lax.axis_index("x"), lax.axis_index("y"), lax.axis_index("c")


def _slot(px, py, pc):
    return 4 * px + 2 * py + pc


def all_gather(name, shards):
    n = len(shards)

    def body(*refs):
        x_refs, out_refs = refs[:n], refs[n:2 * n]
        send_sems, recv_sems, local_sems = refs[2 * n:]
        x, y, c = _place()
        me, sibling = (x, y, c), (x, y, 1 - c)
        chips = [(1 - x, y), (x, 1 - y), (1 - x, 1 - y)]

        def copy(a, k, block, to, src=None):
            rows = out_refs[a].at[_slot(*block)]
            return pltpu.make_async_remote_copy(
                src_ref=rows if src is None else src, dst_ref=rows,
                send_sem=send_sems.at[7 * a + k], recv_sem=recv_sems.at[7 * a + k],
                device_id=to, device_id_type=MESH)

        mine = [pltpu.make_async_copy(x_refs[a], out_refs[a].at[_slot(*me)], local_sems.at[a]) for a in range(n)]
        for cp in mine:
            cp.start()
        first = []
        for a in range(n):
            first.append(copy(a, 0, me, sibling, src=x_refs[a]))
            first += [copy(a, 1 + j, me, (*chip, c), src=x_refs[a]) for j, chip in enumerate(chips)]
        for cp in first:
            cp.start()
        passed = []
        for a in range(n):
            for j, chip in enumerate(chips):
                copy(a, 1 + j, (*chip, c), me).wait_recv()
                cp = copy(a, 4 + j, (*chip, c), sibling)
                cp.start()
                passed.append(cp)
        for a in range(n):
            copy(a, 0, sibling, me).wait_recv()
            for j, chip in enumerate(chips):
                copy(a, 4 + j, (*chip, 1 - c), me).wait_recv()
        for cp in first + passed:
            cp.wait_send()
        for cp in mine:
            cp.wait()

    return pl.pallas_call(
        body, name=name, in_specs=[ANY] * n, out_specs=[ANY] * n,
        out_shape=[jax.ShapeDtypeStruct((N_DEV, *s.shape), s.dtype) for s in shards],
        scratch_shapes=[pltpu.SemaphoreType.DMA((7 * n,)), pltpu.SemaphoreType.DMA((7 * n,)), pltpu.SemaphoreType.DMA((n,))],
    )(*shards)


def exchange_blocks(name, grads):
    n = len(grads)
    rels = [(dx, dy, dc) for dx in (0, 1) for dy in (0, 1) for dc in (0, 1)][1:]

    def body(*refs):
        g_refs, land_refs = refs[:n], refs[n:2 * n]
        send_sems, recv_sems, local_sems = refs[2 * n:]
        x, y, c = _place()
        me = _slot(x, y, c)

        def peer(rel):
            return tuple(1 - v if d else v for v, d in zip((x, y, c), rel))

        mine = [pltpu.make_async_copy(g_refs[a].at[me], land_refs[a].at[me], local_sems.at[a]) for a in range(n)]
        for cp in mine:
            cp.start()
        sends = []
        for a in range(n):
            for r, rel in enumerate(rels):
                p = peer(rel)
                sends.append(pltpu.make_async_remote_copy(
                    src_ref=g_refs[a].at[_slot(*p)], dst_ref=land_refs[a].at[me],
                    send_sem=send_sems.at[7 * a + r], recv_sem=recv_sems.at[7 * a + r],
                    device_id=p, device_id_type=MESH))
        for cp in sends:
            cp.start()
        for a in range(n):
            for r, rel in enumerate(rels):
                p = peer(rel)
                landed = land_refs[a].at[_slot(*p)]
                pltpu.make_async_remote_copy(
                    src_ref=landed, dst_ref=landed, send_sem=send_sems.at[7 * a + r], recv_sem=recv_sems.at[7 * a + r],
                    device_id=(x, y, c), device_id_type=MESH).wait_recv()
        for cp in sends:
            cp.wait_send()
        for cp in mine:
            cp.wait()

    return pl.pallas_call(
        body, name=name, in_specs=[ANY] * n, out_specs=[ANY] * n,
        out_shape=[jax.ShapeDtypeStruct(g.shape, g.dtype) for g in grads],
        scratch_shapes=[pltpu.SemaphoreType.DMA((7 * n,)), pltpu.SemaphoreType.DMA((7 * n,)), pltpu.SemaphoreType.DMA((n,))],
    )(*grads)


WEIGHTS = ['norm_mix_w', 'w_in', 'hg_lb_raw', 'hg_norm_w', 'cv_dw_w', 'cv_dw_b', 'cv_ln_w', 'cv_ln_b', 'pl_w', 'pl_scale',
           'lru_conv_w', 'lru_conv_b', 'lru_wa', 'lru_ba', 'lru_wx', 'lru_bx', 'lru_lambda', 'gate_b', 'w_branch', 'w_out',
           'norm_mem_w', 'mem_norm_w', 'xa_wq', 'xa_wkv', 'xa_wo', 'norm_ffn_w', 'ffn_w1', 'ffn_w2', 'final_norm_w']
BIG = ('w_in', 'w_branch', 'w_out', 'xa_wq', 'xa_wkv', 'xa_wo', 'ffn_w1', 'ffn_w2')
SMALL_SHARDED = ('cv_dw_w', 'lru_conv_w', 'gate_b')
SMALL = tuple(n for n in WEIGHTS if n not in BIG and n not in SMALL_SHARDED)
PACK_ROWS = 256


def _pack(arrs):
    flat = jnp.concatenate([a.reshape(-1).astype(F32) for a in arrs])
    tile = PACK_ROWS * LANE
    padded = -(-flat.shape[0] // tile) * tile
    return jnp.pad(flat, (0, padded - flat.shape[0])).reshape(-1, LANE)


def _unpack(packed, shapes):
    flat = packed.reshape(-1)
    out, off = [], 0
    for s in shapes:
        n = math.prod(s)
        out.append(flat[off:off + n].reshape(s))
        off += n
    return out


def _gather_last(g, shard_shape):
    nd = len(shard_shape)
    full = jnp.moveaxis(g, 0, nd - 1)
    return full.reshape(*shard_shape[:-1], N_DEV * shard_shape[-1])


def _block_diag(w):
    w2 = w.reshape(4, 2, 64, 64)
    z = jnp.zeros((4, 64, 64), w.dtype)
    return jnp.concatenate([jnp.concatenate([w2[:, 0], z], axis=2), jnp.concatenate([z, w2[:, 1]], axis=2)], axis=1)


def _block_diag_t(d):
    return jnp.stack([d[:, :64, :64], d[:, 64:, 64:]], axis=1).reshape(8, 64, 64)


def _lower_bounds(raw):
    lb = jnp.cumsum(jax.nn.softmax(raw.astype(F32), axis=0), axis=0)
    return lb - lb[0:1]


def _decay_rates(lam):
    return (LRU_C * jax.nn.softplus(-lam.astype(F32))).reshape(DEPTH, BRANCH_W)


def _relu2(acc):
    r = jnp.maximum(acc, 0.0)
    return acc, r * r


def _relu2_grad(acc, u):
    return (acc * 2.0 * jnp.maximum(u, 0.0),)


def _add(acc, e):
    return (acc + e,)


def _layer_fwd(x0, mem, p, g):
    h1 = rms_fwd("rms_mix", x0, p['norm_mix_w'])
    proj = mm_nt("mm_in", h1, g['w_in'])[0]
    b_hg, states = hgrn_fwd(proj, p['lb'], p['hg_norm_w'])
    zc = cv_fwd(proj, p['cv_w32'], p['cv_dw_b'])
    b_cv = ln_silu_fwd(zc, p['cv_ln_w'], p['cv_ln_b'])
    b_pl = pool_fwd(proj, p['pl_w'], p['pl_scale'])
    b_lru, hst = lru_fwd(proj, p['lru_cw8'], p['lru_conv_b'], p['wa_bd'], p['lru_ba'], p['wx_bd'], p['lru_bx'], p['sp8'])
    branches = [b_hg, b_cv, b_pl, b_lru]
    ups = [mm_nn_cb("mm_up", branches[k], g['w_branch'][k])[0] for k in range(N_BRANCH)]
    merged = merge_fwd(ups, proj, p['gate_b'])
    x1 = mm_nn("mm_out", merged, g['w_out'], epi=_add, extras=(x0,))[0]
    h2 = rms_fwd("rms_mem", x1, p['norm_mem_w'])
    q = mm_nn("mm_q", h2, g['xa_wq'], out_dtype=BF16)[0]
    memn = rms_fwd("rms_memtok", mem, p['mem_norm_w'])
    kv = mm_nn_cb("mm_kv", memn, g['xa_wkv'], out_dtype=BF16)[0]
    oa = attn_fwd(q, kv)
    x2 = mm_nn("mm_o", oa, g['xa_wo'], epi=_add, extras=(x1,))[0]
    h3 = rms_fwd("rms_ffn", x2, p['norm_ffn_w'])
    u, act = mm_nn_cb("mm_ffn1", h3, g['ffn_w1'], epi=_relu2, out_dtypes=[F32, BF16])
    x3 = mm_nn("mm_ffn2", act, g['ffn_w2'], epi=_add, extras=(x2,))[0]
    res = dict(x0=x0, h1=h1, proj=proj, states=states, zc=zc, hst=hst, branches=branches, ups=ups, merged=merged,
               x1=x1, h2=h2, q=q, memn=memn, kv=kv, oa=oa, x2=x2, h3=h3, u=u, act=act)
    return x3, res


def _layer_bwd(dx3, mem, p, g, r):
    S = dx3.shape[0]
    gs, gb = {}, {}
    du = mm_nt("mm_dffn2", dx3, g['ffn_w2'], out_dtype=BF16, epi=_relu2_grad, extras=(r['u'],))[0]
    gb['ffn_w2'] = mm_tn("mm_gw2", r['act'], dx3).reshape(N_DEV, -1, D_MODEL)
    gb['ffn_w1'] = mm_tn_cb("mm_gw1", r['h3'], du, N_DEV)
    dh3 = mm_nt_cb("mm_dffn1", du, g['ffn_w1'], out_dtype=BF16)[0]
    dx2, gs['norm_ffn_w'] = rms_bwd("rmsb_ffn", r['x2'], p['norm_ffn_w'], dh3, dx3)
    doa = mm_nt("mm_do", dx2, g['xa_wo'], out_dtype=BF16)[0]
    gb['xa_wo'] = mm_tn("mm_gwo", r['oa'], dx2).reshape(N_DEV, -1, D_MODEL)
    dq, dkv = attn_bwd(r['q'], r['kv'], doa)
    gb['xa_wq'] = mm_tn("mm_gwq", r['h2'], dq).reshape(N_DEV, -1, D_MODEL)
    dh2 = mm_nt("mm_dq", dq, g['xa_wq'], out_dtype=BF16)[0]
    gb['xa_wkv'] = mm_tn_cb("mm_gwkv", r['memn'], dkv, N_DEV)
    dmemn = mm_nt_cb("mm_dkv", dkv, g['xa_wkv'], out_dtype=BF16)[0]
    _, gs['mem_norm_w'] = rms_bwd("rmsb_memtok", mem, p['mem_norm_w'], dmemn)
    dx1, gs['norm_mem_w'] = rms_bwd("rmsb_mem", r['x1'], p['norm_mem_w'], dh2, dx2)
    dmerged = mm_nt("mm_dout", dx1, g['w_out'], out_dtype=BF16)[0]
    gb['w_out'] = mm_tn("mm_gwout", r['merged'], dx1).reshape(N_DEV, -1, D_MODEL)
    dups, dgates, gs['gate_b'] = merge_bwd(dmerged, r['ups'], r['proj'], p['gate_b'])
    gb['w_branch'] = [mm_tn_cb("mm_gwb", r['branches'][k], dups[k], N_DEV) for k in range(N_BRANCH)]
    db = [mm_nt_cb("mm_dup", dups[k], g['w_branch'][k], out_dtype=BF16)[0] for k in range(N_BRANCH)]
    dq_, df_, dv_, dg_, gs['lb'], gs['hg_norm_w'] = hgrn_bwd(r['proj'], p['lb'], p['hg_norm_w'], r['states'], db[0])
    dzc, gs['cv_ln_w'], gs['cv_ln_b'] = ln_silu_bwd(r['zc'], p['cv_ln_w'], p['cv_ln_b'], db[1])
    dca, dcg, dcw, gs['cv_dw_b'] = cv_bwd(r['proj'], p['cv_w32'], dzc)
    gs['cv_dw_w'] = dcw[:CV_KERNEL]
    dpu, gs['pl_w'], gs['pl_scale'] = pool_bwd(r['proj'], p['pl_w'], p['pl_scale'], db[2])
    dlx, dly, dlcw, gs['lru_conv_b'], dwa, gs['lru_ba'], dwx, gs['lru_bx'], gs['sp8'] = lru_bwd(
        r['proj'], p['lru_cw8'], p['lru_conv_b'], p['wa_bd'], p['lru_ba'], p['wx_bd'], p['lru_bx'], p['sp8'], r['hst'], db[3])
    gs['lru_conv_w'] = dlcw[:LRU_CONV]
    gs['lru_wa'], gs['lru_wx'] = _block_diag_t(dwa), _block_diag_t(dwx)
    gs['lru_ba'], gs['lru_bx'] = gs['lru_ba'].reshape(8, 64), gs['lru_bx'].reshape(8, 64)
    dproj = jnp.concatenate([dq_, df_, dv_, dg_, dca, dcg, dpu, dlx, dly, *dgates], axis=1)
    gb['w_in'] = mm_tn("mm_gwin", dproj, r['h1']).reshape(N_DEV, -1, D_MODEL)
    dh1 = mm_nn("mm_din", dproj, g['w_in'], out_dtype=BF16)[0]
    dx0, gs['norm_mix_w'] = rms_bwd("rmsb_mix", r['x0'], p['norm_mix_w'], dh1, dx1)
    return dx0, gs, gb


def kernel(x, mem, norm_mix_w, w_in, hg_lb_raw, hg_norm_w, cv_dw_w, cv_dw_b, cv_ln_w, cv_ln_b, pl_w, pl_scale, lru_conv_w, lru_conv_b, lru_wa, lru_ba, lru_wx, lru_bx, lru_lambda, gate_b, w_branch, w_out, norm_mem_w, mem_norm_w, xa_wq, xa_wkv, xa_wo, norm_ffn_w, ffn_w1, ffn_w2, final_norm_w, loss_target, m_norm_mix_w, m_w_in, m_hg_lb_raw, m_hg_norm_w, m_cv_dw_w, m_cv_dw_b, m_cv_ln_w, m_cv_ln_b, m_pl_w, m_pl_scale, m_lru_conv_w, m_lru_conv_b, m_lru_wa, m_lru_ba, m_lru_wx, m_lru_bx, m_lru_lambda, m_gate_b, m_w_branch, m_w_out, m_norm_mem_w, m_mem_norm_w, m_xa_wq, m_xa_wkv, m_xa_wo, m_norm_ffn_w, m_ffn_w1, m_ffn_w2, m_final_norm_w, v_norm_mix_w, v_w_in, v_hg_lb_raw, v_hg_norm_w, v_cv_dw_w, v_cv_dw_b, v_cv_ln_w, v_cv_ln_b, v_pl_w, v_pl_scale, v_lru_conv_w, v_lru_conv_b, v_lru_wa, v_lru_ba, v_lru_wx, v_lru_bx, v_lru_lambda, v_gate_b, v_w_branch, v_w_out, v_norm_mem_w, v_mem_norm_w, v_xa_wq, v_xa_wkv, v_xa_wo, v_norm_ffn_w, v_ffn_w1, v_ffn_w2, v_final_norm_w):
    W = dict(zip(WEIGHTS, (norm_mix_w, w_in, hg_lb_raw, hg_norm_w, cv_dw_w, cv_dw_b, cv_ln_w, cv_ln_b, pl_w, pl_scale, lru_conv_w, lru_conv_b, lru_wa, lru_ba, lru_wx, lru_bx, lru_lambda, gate_b, w_branch, w_out, norm_mem_w, mem_norm_w, xa_wq, xa_wkv, xa_wo, norm_ffn_w, ffn_w1, ffn_w2, final_norm_w)))
    Mo = dict(zip(WEIGHTS, (m_norm_mix_w, m_w_in, m_hg_lb_raw, m_hg_norm_w, m_cv_dw_w, m_cv_dw_b, m_cv_ln_w, m_cv_ln_b, m_pl_w, m_pl_scale, m_lru_conv_w, m_lru_conv_b, m_lru_wa, m_lru_ba, m_lru_wx, m_lru_bx, m_lru_lambda, m_gate_b, m_w_branch, m_w_out, m_norm_mem_w, m_mem_norm_w, m_xa_wq, m_xa_wkv, m_xa_wo, m_norm_ffn_w, m_ffn_w1, m_ffn_w2, m_final_norm_w)))
    Vo = dict(zip(WEIGHTS, (v_norm_mix_w, v_w_in, v_hg_lb_raw, v_hg_norm_w, v_cv_dw_w, v_cv_dw_b, v_cv_ln_w, v_cv_ln_b, v_pl_w, v_pl_scale, v_lru_conv_w, v_lru_conv_b, v_lru_wa, v_lru_ba, v_lru_wx, v_lru_bx, v_lru_lambda, v_gate_b, v_w_branch, v_w_out, v_norm_mem_w, v_mem_norm_w, v_xa_wq, v_xa_wkv, v_xa_wo, v_norm_ffn_w, v_ffn_w1, v_ffn_w2, v_final_norm_w)))
    me = _slot(*_place())
    xs, mems, target = x[0], mem[0], loss_target[0]

    shard_shapes = [W[n].shape for n in SMALL_SHARDED]
    gathered = all_gather("ag_small", [_pack([W[n] for n in SMALL_SHARDED])])[0]
    parts = [jnp.stack(ps) for ps in zip(*[_unpack(gathered[d], shard_shapes) for d in range(N_DEV)])]
    full_small = {n: _gather_last(parts[i], shard_shapes[i]) for i, n in enumerate(SMALL_SHARDED)}
    lb_all, lb_vjp = jax.vjp(_lower_bounds, hg_lb_raw)
    sp8_all, sp8_vjp = jax.vjp(_decay_rates, lru_lambda)

    def layer_params(l):
        p = {n: W[n][l] for n in SMALL if n != 'final_norm_w'}
        p['lb'] = lb_all[l]
        p['sp8'] = sp8_all[l]
        p['cv_w32'] = jnp.pad(full_small['cv_dw_w'][l], ((0, 32 - CV_KERNEL), (0, 0)))
        p['lru_cw8'] = jnp.pad(full_small['lru_conv_w'][l], ((0, 8 - LRU_CONV), (0, 0)))
        p['gate_b'] = full_small['gate_b'][l]
        p['wa_bd'], p['wx_bd'] = _block_diag(lru_wa[l]), _block_diag(lru_wx[l])
        p['lru_ba'], p['lru_bx'] = lru_ba[l].reshape(-1), lru_bx[l].reshape(-1)
        return p

    def gather_layer(l):
        shards = [jnp.transpose(w_in[l]).astype(BF16)] + [w_branch[l, k].astype(BF16) for k in range(N_BRANCH)]
        shards += [w[l].astype(BF16) for w in (w_out, xa_wq, xa_wkv, xa_wo, ffn_w1, ffn_w2)]
        o = all_gather("ag_layer", shards)
        return dict(w_in=o[0].reshape(IN_W, D_MODEL), w_branch=o[1:5], w_out=o[5].reshape(D_MODEL, D_MODEL),
                    xa_wq=o[6].reshape(D_MODEL, D_MODEL), xa_wkv=o[7], xa_wo=o[8].reshape(D_MODEL, D_MODEL),
                    ffn_w1=o[9], ffn_w2=o[10].reshape(D_FF, D_MODEL))

    params = [layer_params(l) for l in range(DEPTH)]
    mats = [gather_layer(l) for l in range(DEPTH)]
    residuals = []
    xc = xs
    for l in range(DEPTH):
        xc, res = _layer_fwd(xc, mems, params[l], mats[l])
        residuals.append(res)
    loss_part, dx, g_final = loss_head(xc, final_norm_w, target)
    loss = lax.psum(loss_part, ("x", "y", "c"))

    small_grads = [None] * DEPTH
    big_grads = [None] * DEPTH
    for l in reversed(range(DEPTH)):
        dx, gs, gb = _layer_bwd(dx, mems, params[l], mats[l], residuals[l])
        small_grads[l] = gs
        order = [gb['w_in'], *gb['w_branch'], gb['w_out'], gb['xa_wq'], gb['xa_wkv'], gb['xa_wo'], gb['ffn_w1'], gb['ffn_w2']]
        landed = exchange_blocks("rs_layer", order)
        big_grads[l] = [sum_parts(t.reshape(N_DEV, -1, t.shape[-1])).reshape(t.shape[1:]) for t in landed]

    G = {}
    G['w_in'] = jnp.stack([jnp.transpose(big_grads[l][0]) for l in range(DEPTH)])
    G['w_branch'] = jnp.stack([jnp.stack(big_grads[l][1:5]) for l in range(DEPTH)])
    for i, n in enumerate(('w_out', 'xa_wq', 'xa_wkv', 'xa_wo', 'ffn_w1', 'ffn_w2')):
        G[n] = jnp.stack([big_grads[l][5 + i] for l in range(DEPTH)])

    def stacked(n):
        return jnp.stack([small_grads[l][n] for l in range(DEPTH)])

    part = {n: stacked(n) for n in SMALL if n not in ('final_norm_w', 'hg_lb_raw', 'lru_lambda')}
    part['final_norm_w'] = g_final
    part['hg_lb_raw'] = lb_vjp(stacked('lb'))[0]
    part['lru_lambda'] = sp8_vjp(stacked('sp8'))[0]
    for n in SMALL_SHARDED:
        part[n] = stacked(n)
    names = list(SMALL) + list(SMALL_SHARDED)
    full_shapes = [part[n].shape for n in names]
    everyone = all_gather("ag_grads", [_pack([part[n] for n in names])])[0]
    total = sum_parts(everyone)
    for n, t in zip(names, _unpack(total, full_shapes)):
        if n in SMALL_SHARDED:
            c = t.shape[-1] // N_DEV
            t = lax.dynamic_slice_in_dim(t, me * c, c, axis=t.ndim - 1)
        G[n] = t

    delta, new_m, new_v = {}, {}, {}
    for n in BIG:
        c = W[n].shape[-1]
        d, nm, nv = adamw(W[n].reshape(-1, c), G[n].reshape(-1, c), Mo[n].reshape(-1, c), Vo[n].reshape(-1, c))
        delta[n], new_m[n], new_v[n] = d.reshape(W[n].shape), nm.reshape(W[n].shape), nv.reshape(W[n].shape)
    shapes = [W[n].shape for n in names]
    d, nm, nv = adamw(_pack([W[n] for n in names]), _pack([G[n] for n in names]), _pack([Mo[n] for n in names]), _pack([Vo[n] for n in names]))
    for n, a, b, c in zip(names, _unpack(d, shapes), _unpack(nm, shapes), _unpack(nv, shapes)):
        delta[n], new_m[n], new_v[n] = a, b, c
    return (loss, dx[None], *[G[n] for n in WEIGHTS], *[delta[n] for n in WEIGHTS],
            *[new_m[n] for n in WEIGHTS], *[new_v[n] for n in WEIGHTS])
```

```python
import functools
import math

import jax
import jax.numpy as jnp
from jax import lax
from jax.experimental import pallas as pl
from jax.experimental.pallas import tpu as pltpu

F32 = jnp.float32
BF16 = jnp.bfloat16
I32 = jnp.int32

N_DEV = 8
D_MODEL = 1024
DEPTH = 4
CHUNK = 64
EPS = 1e-6
HG_HEADS = 4
BRANCH_W = 512
CV_KERNEL = 31
POOL_WINDOWS = (2, 4, 8, 16)
LRU_CONV = 4
LRU_C = 8.0
XA_HEADS = 4
XA_HD = D_MODEL // XA_HEADS
D_FF = 4 * D_MODEL
IN_W = 8704
OFF_Q, OFF_F, OFF_V, OFF_G, OFF_CV, OFF_PL, OFF_LX, OFF_LY, OFF_GATE = 0, 512, 1024, 1536, 2048, 3072, 3584, 4096, 4608
LANE = 128
ADAM_LR, ADAM_B1, ADAM_B2, ADAM_EPS, ADAM_WD, ADAM_STEP = 0.001, 0.9, 0.999, 1e-08, 0.01, 10
VMEM_LIMIT = 56 * 1024 * 1024
MESH = pl.DeviceIdType.MESH
NEG = -1e30


def _cp(sem, **kw):
    return pltpu.CompilerParams(dimension_semantics=sem, vmem_limit_bytes=VMEM_LIMIT, **kw)


def _sigmoid(x):
    return 1.0 / (1.0 + jnp.exp(-x))


def _dsilu(x, s):
    return s * (1.0 + x * (1.0 - s))


def _dot(a, b, cdims, precision=None):
    return lax.dot_general(a, b, (cdims, ((), ())), preferred_element_type=F32, precision=precision)


NN = ((1,), (0,))
NT = ((1,), (1,))
TN = ((0,), (0,))


def _mm(name, a, b, *, grid, a_spec, b_spec, o_specs, out_shapes, acc_shape, cdims, epi=None, extras=(), extra_specs=()):
    nk = grid[2]
    n_e, n_o = len(extras), len(out_shapes)

    def body(*refs):
        a_ref, b_ref = refs[0], refs[1]
        e_refs = refs[2:2 + n_e]
        o_refs = refs[2 + n_e:2 + n_e + n_o]

        def finish(acc):
            vals = epi(acc, *[r[...] for r in e_refs]) if epi is not None else (acc,)
            for r, v in zip(o_refs, vals):
                r[...] = v.astype(r.dtype)

        part = _dot(a_ref[...].astype(BF16), b_ref[...].astype(BF16), cdims)
        if nk == 1:
            finish(part)
        else:
            acc_ref = refs[-1]
            k = pl.program_id(2)

            @pl.when(k == 0)
            def _():
                acc_ref[...] = part

            @pl.when(k > 0)
            def _():
                acc_ref[...] += part

            @pl.when(k == nk - 1)
            def _():
                finish(acc_ref[...])

    return pl.pallas_call(
        body, name=name, grid=grid,
        in_specs=[a_spec, b_spec, *extra_specs], out_specs=list(o_specs), out_shape=list(out_shapes),
        scratch_shapes=[] if nk == 1 else [pltpu.VMEM(acc_shape, F32)],
        compiler_params=_cp(("parallel", "parallel", "arbitrary")),
    )(a, b, *extras)


def _tile(n, pref):
    t = min(n, pref)
    while n % t:
        t //= 2
    return t


def mm_nt(name, a, b, out_dtype=F32, epi=None, extras=(), n_out=1, out_dtypes=None, tm=1024, tn=512, tk=1024):
    M, K = a.shape
    N = b.shape[0]
    tm, tn, tk = _tile(M, tm), _tile(N, tn), _tile(K, tk)
    odt = out_dtypes or [out_dtype] * n_out
    o_spec = pl.BlockSpec((tm, tn), lambda i, j, k: (i, j))
    return _mm(name, a, b, grid=(M // tm, N // tn, K // tk),
               a_spec=pl.BlockSpec((tm, tk), lambda i, j, k: (i, k)),
               b_spec=pl.BlockSpec((tn, tk), lambda i, j, k: (j, k)),
               o_specs=[o_spec] * len(odt), out_shapes=[jax.ShapeDtypeStruct((M, N), d) for d in odt],
               acc_shape=(tm, tn), cdims=NT, epi=epi, extras=extras, extra_specs=[o_spec] * len(extras))


def mm_nn(name, a, b, out_dtype=F32, epi=None, extras=(), n_out=1, out_dtypes=None, tm=1024, tn=512, tk=1024):
    M, K = a.shape
    N = b.shape[1]
    tm, tn, tk = _tile(M, tm), _tile(N, tn), _tile(K, tk)
    odt = out_dtypes or [out_dtype] * n_out
    o_spec = pl.BlockSpec((tm, tn), lambda i, j, k: (i, j))
    return _mm(name, a, b, grid=(M // tm, N // tn, K // tk),
               a_spec=pl.BlockSpec((tm, tk), lambda i, j, k: (i, k)),
               b_spec=pl.BlockSpec((tk, tn), lambda i, j, k: (k, j)),
               o_specs=[o_spec] * len(odt), out_shapes=[jax.ShapeDtypeStruct((M, N), d) for d in odt],
               acc_shape=(tm, tn), cdims=NN, epi=epi, extras=extras, extra_specs=[o_spec] * len(extras))


def mm_tn(name, a, b, out_dtype=BF16, tm=512, tn=512, tk=1024):
    K, M = a.shape
    N = b.shape[1]
    tm, tn, tk = _tile(M, tm), _tile(N, tn), _tile(K, tk)
    return _mm(name, a, b, grid=(M // tm, N // tn, K // tk),
               a_spec=pl.BlockSpec((tk, tm), lambda i, j, k: (k, i)),
               b_spec=pl.BlockSpec((tk, tn), lambda i, j, k: (k, j)),
               o_specs=[pl.BlockSpec((tm, tn), lambda i, j, k: (i, j))],
               out_shapes=[jax.ShapeDtypeStruct((M, N), out_dtype)], acc_shape=(tm, tn), cdims=TN)[0]


def mm_nn_cb(name, a, b, out_dtype=F32, epi=None, out_dtypes=None, tm=1024):
    M, K = a.shape
    nb, _, c = b.shape
    tm = _tile(M, tm)
    odt = out_dtypes or [out_dtype]
    return _mm(name, a, b, grid=(M // tm, nb, 1),
               a_spec=pl.BlockSpec((tm, K), lambda i, j, k: (i, 0)),
               b_spec=pl.BlockSpec((None, K, c), lambda i, j, k: (j, 0, 0)),
               o_specs=[pl.BlockSpec((tm, c), lambda i, j, k: (i, j))] * len(odt),
               out_shapes=[jax.ShapeDtypeStruct((M, nb * c), d) for d in odt], acc_shape=(tm, c), cdims=NN, epi=epi)


def mm_nt_cb(name, a, b, out_dtype=F32, epi=None, extras=(), tm=1024, tn=512):
    M = a.shape[0]
    nb, K, c = b.shape
    tm, tn = _tile(M, tm), _tile(K, tn)
    o_spec = pl.BlockSpec((tm, tn), lambda i, j, k: (i, j))
    return _mm(name, a, b, grid=(M // tm, K // tn, nb),
               a_spec=pl.BlockSpec((tm, c), lambda i, j, k: (i, k)),
               b_spec=pl.BlockSpec((None, tn, c), lambda i, j, k: (k, j, 0)),
               o_specs=[o_spec], out_shapes=[jax.ShapeDtypeStruct((M, K), out_dtype)],
               acc_shape=(tm, tn), cdims=NT, epi=epi, extras=extras, extra_specs=[o_spec] * len(extras))


def mm_tn_cb(name, a, b, nb, out_dtype=BF16, tm=512, tk=1024):
    K, M = a.shape
    N = b.shape[1]
    c = N // nb
    tm, tk = _tile(M, tm), _tile(K, tk)
    return _mm(name, a, b, grid=(M // tm, nb, K // tk),
               a_spec=pl.BlockSpec((tk, tm), lambda i, j, k: (k, i)),
               b_spec=pl.BlockSpec((tk, c), lambda i, j, k: (k, j)),
               o_specs=[pl.BlockSpec((None, tm, c), lambda i, j, k: (j, i, 0))],
               out_shapes=[jax.ShapeDtypeStruct((nb, M, c), out_dtype)], acc_shape=(tm, c), cdims=TN)[0]


def rms_fwd(name, x, w, out_dtype=BF16, tm=512):
    S, D = x.shape
    tm = _tile(S, tm)

    def body(x_ref, w_ref, o_ref):
        xv = x_ref[...]
        r = lax.rsqrt(jnp.mean(xv * xv, axis=-1, keepdims=True) + EPS)
        o_ref[...] = (xv * r * w_ref[...]).astype(o_ref.dtype)

    return pl.pallas_call(
        body, name=name, grid=(S // tm,),
        in_specs=[pl.BlockSpec((tm, D), lambda i: (i, 0)), pl.BlockSpec((1, D), lambda i: (0, 0))],
        out_specs=pl.BlockSpec((tm, D), lambda i: (i, 0)), out_shape=jax.ShapeDtypeStruct((S, D), out_dtype),
        compiler_params=_cp(("parallel",)),
    )(x, w.reshape(1, D))


def rms_bwd(name, x, w, dh, dres=None, tm=512):
    S, D = x.shape
    tm = _tile(S, tm)
    has_res = dres is not None

    def body(*refs):
        if has_res:
            x_ref, w_ref, dh_ref, dres_ref, dx_ref, dw_ref = refs
        else:
            x_ref, w_ref, dh_ref, dx_ref, dw_ref = refs
        xv = x_ref[...]
        dhv = dh_ref[...].astype(F32)
        r = lax.rsqrt(jnp.mean(xv * xv, axis=-1, keepdims=True) + EPS)
        g = dhv * w_ref[...]
        dx = r * g - xv * (r * r * r) * jnp.mean(xv * g, axis=-1, keepdims=True)
        if has_res:
            dx = dx + dres_ref[...]
        dx_ref[...] = dx

        @pl.when(pl.program_id(0) == 0)
        def _():
            dw_ref[...] = jnp.zeros_like(dw_ref)

        dw_ref[...] += jnp.sum(dhv * xv * r, axis=0, keepdims=True)

    row = pl.BlockSpec((tm, D), lambda i: (i, 0))
    vec = pl.BlockSpec((1, D), lambda i: (0, 0))
    args = [x, w.reshape(1, D), dh] + ([dres] if has_res else [])
    dx, dw = pl.pallas_call(
        body, name=name, grid=(S // tm,),
        in_specs=[row, vec, row] + ([row] if has_res else []),
        out_specs=[row, vec], out_shape=[jax.ShapeDtypeStruct((S, D), F32), jax.ShapeDtypeStruct((1, D), F32)],
        compiler_params=_cp(("arbitrary",)),
    )(*args)
    return dx, dw.reshape(D)


def loss_head(x, w, target, tm=512):
    S, D = x.shape
    tm = _tile(S, tm)

    def body(x_ref, w_ref, t_ref, loss_ref, dx_ref, dw_ref):
        xv = x_ref[...]
        wv = w_ref[...]
        r = lax.rsqrt(jnp.mean(xv * xv, axis=-1, keepdims=True) + EPS)
        y = xv * r * wv
        err = y - t_ref[...]
        dy = err * (1.0 / D)
        g = dy * wv
        dx_ref[...] = r * g - xv * (r * r * r) * jnp.mean(xv * g, axis=-1, keepdims=True)

        @pl.when(pl.program_id(0) == 0)
        def _():
            dw_ref[...] = jnp.zeros_like(dw_ref)
            loss_ref[...] = jnp.zeros_like(loss_ref)

        dw_ref[...] += jnp.sum(dy * xv * r, axis=0, keepdims=True)
        part = 0.5 * jnp.sum(jnp.mean(err * err, axis=-1, keepdims=True), axis=0, keepdims=True)
        loss_ref[...] += jnp.broadcast_to(part, loss_ref.shape)

    row = pl.BlockSpec((tm, D), lambda i: (i, 0))
    vec = pl.BlockSpec((1, D), lambda i: (0, 0))
    loss, dx, dw = pl.pallas_call(
        body, name="loss_head", grid=(S // tm,),
        in_specs=[row, vec, row],
        out_specs=[pl.BlockSpec((1, LANE), lambda i: (0, 0)), row, vec],
        out_shape=[jax.ShapeDtypeStruct((1, LANE), F32), jax.ShapeDtypeStruct((S, D), F32), jax.ShapeDtypeStruct((1, D), F32)],
        compiler_params=_cp(("arbitrary",)),
    )(x, w.reshape(1, D), target)
    return loss[0, 0], dx, dw.reshape(D)


SUB = 16
HG_W = HG_HEADS * LANE
HG_PAD = CHUNK + 2 * SUB


def _hg_gates(q, f, lbv):
    sig = _sigmoid(f)
    fg = lbv + (1.0 - lbv) * sig
    sq = _sigmoid(q)
    return sig, fg, 1.0 - fg, sq, q * sq


def _hg_cumsum(logf):
    ri = lax.broadcasted_iota(I32, (CHUNK, CHUNK), 0)
    ci = lax.broadcasted_iota(I32, (CHUNK, CHUNK), 1)
    return _dot((ci <= ri).astype(F32), logf, NN, precision=lax.Precision.HIGHEST)


def _hg_rows():
    return lax.broadcasted_iota(I32, (CHUNK, LANE), 0)


def _hg_below(qf, kk, b, rows):
    blocks, parts = [jnp.zeros((SUB, CHUNK), F32)], []
    for i in range(1, CHUNK // SUB):
        bref = b[SUB * i - 1:SUB * i, :]
        rs = slice(SUB * i, SUB * (i + 1))
        eq = jnp.exp(b[rs] - bref)
        below = rows < SUB * i
        ek = jnp.exp(jnp.where(below, bref - b, NEG))
        qi = (qf[rs] * eq).astype(BF16)
        ki = (kk * ek).astype(BF16)
        blocks.append(_dot(qi, ki, NT))
        parts.append((qi, ki, eq, ek))
    return jnp.concatenate(blocks, axis=0), parts


def hgrn_fwd(proj, lb, nw):
    S = proj.shape[0]
    NC = S // CHUNK
    H = HG_HEADS

    def body(q_ref, f_ref, v_ref, g_ref, lb_ref, nw_ref, out_ref, st_out_ref, o_ref, st, kk_p, b_p, v_p):
        c = pl.program_id(0)

        @pl.when(c == 0)
        def _():
            st[...] = jnp.zeros_like(st)
            for p in (kk_p, b_p, v_p):
                p[...] = jnp.zeros_like(p)

        st_out_ref[...] = st[...]
        sig, fg, kk_all, sq, qf_all = _hg_gates(q_ref[...], f_ref[...], lb_ref[...])
        b_all = _hg_cumsum(jnp.log(fg))
        kk_p[pl.ds(SUB, CHUNK), :] = kk_all
        b_p[pl.ds(SUB, CHUNK), :] = b_all
        v_p[pl.ds(SUB, CHUNK), :] = v_ref[...]
        rows = _hg_rows()
        sub = rows & (SUB - 1)
        for h in range(H):
            cs = slice(h * LANE, (h + 1) * LANE)
            qf, kk, b, v, g = qf_all[:, cs], kk_all[:, cs], b_all[:, cs], v_ref[:, cs], g_ref[:, cs]
            st_in = st[h]
            o = jnp.zeros((CHUNK, LANE), F32)
            for tau in range(SUB):
                sh = pl.ds(SUB - tau, CHUNK)
                e = jnp.exp(jnp.where(sub >= tau, b - b_p[sh, cs], NEG))
                col = jnp.sum(qf * kk_p[sh, cs] * e, axis=1, keepdims=True)
                o = o + col * v_p[sh, cs]
            poff, _ = _hg_below(qf, kk, b, rows)
            vb = v.astype(BF16)
            bl = b[CHUNK - 1:CHUNK, :]
            o = o + _dot(poff.astype(BF16), vb, NN) + _dot((qf * jnp.exp(b)).astype(BF16), st_in.astype(BF16), NT)
            st[h] = st_in * jnp.exp(bl) + _dot(vb, (kk * jnp.exp(bl - b)).astype(BF16), TN)
            o_ref[:, cs] = o
            r = lax.rsqrt(jnp.mean(o * o, axis=-1, keepdims=True) + EPS)
            out_ref[:, cs] = (o * r * nw_ref[...] * (g * _sigmoid(g))).astype(out_ref.dtype)

    def seg(off):
        return pl.BlockSpec((CHUNK, HG_W), lambda c: (c, off // HG_W))

    blk = pl.BlockSpec((CHUNK, HG_W), lambda c: (c, 0))
    pad = pltpu.VMEM((HG_PAD, HG_W), F32)
    return pl.pallas_call(
        body, name="hgrn_fwd", grid=(NC,),
        in_specs=[seg(OFF_Q), seg(OFF_F), seg(OFF_V), seg(OFF_G),
                  pl.BlockSpec((1, HG_W), lambda c: (0, 0)), pl.BlockSpec((1, LANE), lambda c: (0, 0))],
        out_specs=[blk, pl.BlockSpec((None, H, LANE, LANE), lambda c: (c, 0, 0, 0)), blk],
        out_shape=[jax.ShapeDtypeStruct((S, HG_W), BF16), jax.ShapeDtypeStruct((NC, H, LANE, LANE), F32),
                   jax.ShapeDtypeStruct((S, HG_W), F32)],
        scratch_shapes=[pltpu.VMEM((H, LANE, LANE), F32), pad, pad, pad],
        compiler_params=_cp(("arbitrary",)),
    )(proj, proj, proj, proj, lb.reshape(1, HG_W), nw.reshape(1, LANE))


def hgrn_bwd(proj, lb, nw, states, o_pre, dout):
    S = proj.shape[0]
    NC = S // CHUNK
    H = HG_HEADS

    def body(q_ref, f_ref, v_ref, g_ref, lb_ref, nw_ref, st_ref, o_ref, do_ref,
             dq_ref, df_ref, dv_ref, dg_ref, dlb_ref, dnw_ref, dst, kk_p, b_p, v_p, qf_p, do_p, db_s, dkk_s):
        c = pl.program_id(0)

        @pl.when(c == 0)
        def _():
            dst[...] = jnp.zeros_like(dst)
            dlb_ref[...] = jnp.zeros_like(dlb_ref)
            dnw_ref[...] = jnp.zeros_like(dnw_ref)
            for p in (kk_p, b_p, v_p, qf_p, do_p):
                p[...] = jnp.zeros_like(p)

        q_all, g_all = q_ref[...], g_ref[...]
        lbv, nwv = lb_ref[...], nw_ref[...]
        sig, fg, kk_all, sq, qf_all = _hg_gates(q_all, f_ref[...], lbv)
        b_all = _hg_cumsum(jnp.log(fg))
        o_all = o_ref[...]
        dov = do_ref[...].astype(F32)
        sg = _sigmoid(g_all)
        gsg = g_all * sg
        dnw_acc = jnp.zeros((1, LANE), F32)
        for h in range(H):
            cs = slice(h * LANE, (h + 1) * LANE)
            o = o_all[:, cs]
            r = lax.rsqrt(jnp.mean(o * o, axis=-1, keepdims=True) + EPS)
            don = dov[:, cs] * gsg[:, cs]
            dnw_acc = dnw_acc + jnp.sum(don * o * r, axis=0, keepdims=True)
            gno = don * nwv
            do_p[pl.ds(SUB, CHUNK), cs] = r * gno - o * (r * r * r) * jnp.mean(o * gno, axis=-1, keepdims=True)
            dg_ref[:, cs] = (dov[:, cs] * (o * r * nwv) * _dsilu(g_all[:, cs], sg[:, cs])).astype(dg_ref.dtype)
        dnw_ref[...] += jnp.broadcast_to(dnw_acc, dnw_ref.shape)
        kk_p[pl.ds(SUB, CHUNK), :] = kk_all
        b_p[pl.ds(SUB, CHUNK), :] = b_all
        v_p[pl.ds(SUB, CHUNK), :] = v_ref[...]
        qf_p[pl.ds(SUB, CHUNK), :] = qf_all
        rows = _hg_rows()
        sub = rows & (SUB - 1)
        for h in range(H):
            cs = slice(h * LANE, (h + 1) * LANE)
            qf, kk, b, v = qf_all[:, cs], kk_all[:, cs], b_all[:, cs], v_ref[:, cs]
            do = do_p[pl.ds(SUB, CHUNK), cs]
            st_in, dstv = st_ref[h], dst[h]
            bl = b[CHUNK - 1:CHUNK, :]
            eb, ebl, el = jnp.exp(b), jnp.exp(bl - b), jnp.exp(bl)
            qe, ke = qf * eb, kk * ebl
            vb, dob, stb, dstb = v.astype(BF16), do.astype(BF16), st_in.astype(BF16), dstv.astype(BF16)
            w_ = _dot(vb, dstb, NN)
            dqf = eb * _dot(dob, stb, NN)
            dkk = ebl * w_
            dv = _dot(ke.astype(BF16), dstb, NT)
            dbl = el * jnp.sum(st_in * dstv, axis=0, keepdims=True) + jnp.sum(ke * w_, axis=0, keepdims=True)
            dst[h] = dstv * el + _dot(dob, qe.astype(BF16), TN)
            poff, parts = _hg_below(qf, kk, b, rows)
            dpoff = _dot(dob, vb, NT).astype(BF16)
            dv = dv + _dot(poff.astype(BF16), dob, TN)
            dq_blocks = [jnp.zeros((SUB, LANE), F32)]
            for i, (qi, ki, eq, ek) in enumerate(parts, start=1):
                dpi = dpoff[SUB * i:SUB * (i + 1), :]
                dq_blocks.append(_dot(dpi, ki, NN) * eq)
                dkk = dkk + _dot(dpi, qi, TN) * ek
            dqf = dqf + jnp.concatenate(dq_blocks, axis=0)
            for tau in range(SUB):
                sh = pl.ds(SUB - tau, CHUNK)
                kd = kk_p[sh, cs]
                e = jnp.exp(jnp.where(sub >= tau, b - b_p[sh, cs], NEG))
                dcol = jnp.sum(do * v_p[sh, cs], axis=1, keepdims=True)
                dqf = dqf + dcol * kd * e
            for tau in range(SUB):
                sh = pl.ds(SUB + tau, CHUNK)
                qu, dou = qf_p[sh, cs], do_p[sh, cs]
                e = jnp.exp(jnp.where(sub + tau < SUB, b_p[sh, cs] - b, NEG))
                qe_ = qu * e
                dcol = jnp.sum(dou * v, axis=1, keepdims=True)
                col = jnp.sum(qe_ * kk, axis=1, keepdims=True)
                dkk = dkk + dcol * qe_
                dv = dv + col * dou
            dv_ref[:, cs] = dv.astype(dv_ref.dtype)
            db = qf * dqf - kk * dkk
            db_s[:, cs] = db + jnp.where(rows == CHUNK - 1, dbl, 0.0)
            dkk_s[:, cs] = dkk
            dq_ref[:, cs] = (dqf * _dsilu(q_all[:, cs], sq[:, cs])).astype(dq_ref.dtype)
        ri = lax.broadcasted_iota(I32, (CHUNK, CHUNK), 0)
        ci = lax.broadcasted_iota(I32, (CHUNK, CHUNK), 1)
        dlogf = _dot((ci >= ri).astype(F32), db_s[...], NN, precision=lax.Precision.HIGHEST)
        dfg = dlogf / fg - dkk_s[...]
        df_ref[...] = (dfg * (1.0 - lbv) * sig * (1.0 - sig)).astype(df_ref.dtype)
        dlb_ref[...] += jnp.broadcast_to(jnp.sum(dfg * (1.0 - sig), axis=0, keepdims=True), dlb_ref.shape)

    def seg(off):
        return pl.BlockSpec((CHUNK, HG_W), lambda c: (NC - 1 - c, off // HG_W))

    blk = pl.BlockSpec((CHUNK, HG_W), lambda c: (NC - 1 - c, 0))
    osd = jax.ShapeDtypeStruct((S, HG_W), BF16)
    pad = pltpu.VMEM((HG_PAD, HG_W), F32)
    full = pltpu.VMEM((CHUNK, HG_W), F32)
    dq, df, dv, dg, dlb, dnw = pl.pallas_call(
        body, name="hgrn_bwd", grid=(NC,),
        in_specs=[seg(OFF_Q), seg(OFF_F), seg(OFF_V), seg(OFF_G),
                  pl.BlockSpec((1, HG_W), lambda c: (0, 0)), pl.BlockSpec((1, LANE), lambda c: (0, 0)),
                  pl.BlockSpec((None, H, LANE, LANE), lambda c: (NC - 1 - c, 0, 0, 0)), blk, blk],
        out_specs=[blk, blk, blk, blk, pl.BlockSpec((8, HG_W), lambda c: (0, 0)), pl.BlockSpec((8, LANE), lambda c: (0, 0))],
        out_shape=[osd, osd, osd, osd, jax.ShapeDtypeStruct((8, HG_W), F32), jax.ShapeDtypeStruct((8, LANE), F32)],
        scratch_shapes=[pltpu.VMEM((H, LANE, LANE), F32), pad, pad, pad, pad, pad, full, full],
        compiler_params=_cp(("arbitrary",)),
    )(proj, proj, proj, proj, lb.reshape(1, HG_W), nw.reshape(1, LANE), states, o_pre, dout)
    return dq, df, dv, dg, dlb[0], dnw[0]


CV_PAD = 32
ROWS = 256


def _colblk(S, off):
    return pl.BlockSpec((S, LANE), lambda j: (0, off // LANE + j))


def cv_fwd(proj, w32, bias):
    S = proj.shape[0]
    nchunk = S // ROWS

    def body(a_ref, g_ref, w_ref, b_ref, o_ref, zpad):
        zpad[pl.ds(0, CV_PAD), :] = jnp.zeros((CV_PAD, LANE), F32)

        def glu(c, _):
            r0 = pl.multiple_of(c * ROWS, ROWS)
            zpad[pl.ds(CV_PAD + r0, ROWS), :] = a_ref[pl.ds(r0, ROWS), :] * _sigmoid(g_ref[pl.ds(r0, ROWS), :])
            return 0

        lax.fori_loop(0, nchunk, glu, 0)

        def conv(c, _):
            r0 = pl.multiple_of(c * ROWS, ROWS)
            acc = jnp.broadcast_to(b_ref[...], (ROWS, LANE))
            for j in range(CV_KERNEL):
                acc = acc + w_ref[pl.ds(j, 1), :] * zpad[pl.ds(r0 + (CV_PAD - CV_KERNEL + 1) + j, ROWS), :]
            o_ref[pl.ds(r0, ROWS), :] = acc
            return 0

        lax.fori_loop(0, nchunk, conv, 0)

    return pl.pallas_call(
        body, name="cv_fwd", grid=(BRANCH_W // LANE,),
        in_specs=[_colblk(S, OFF_CV), _colblk(S, OFF_CV + BRANCH_W),
                  pl.BlockSpec((32, LANE), lambda j: (0, j)), pl.BlockSpec((1, LANE), lambda j: (0, j))],
        out_specs=pl.BlockSpec((S, LANE), lambda j: (0, j)), out_shape=jax.ShapeDtypeStruct((S, BRANCH_W), F32),
        scratch_shapes=[pltpu.VMEM((CV_PAD + S, LANE), F32)],
        compiler_params=_cp(("parallel",)),
    )(proj, proj, w32, bias.reshape(1, BRANCH_W))


def cv_bwd(proj, w32, dzc):
    S = proj.shape[0]
    nchunk = S // ROWS

    def body(a_ref, g_ref, w_ref, dz_ref, da_ref, dg_ref, dw_ref, db_ref, zpad, dpad):
        zpad[pl.ds(0, CV_PAD), :] = jnp.zeros((CV_PAD, LANE), F32)
        dpad[pl.ds(S, CV_PAD), :] = jnp.zeros((CV_PAD, LANE), F32)
        dw_ref[...] = jnp.zeros_like(dw_ref)

        def glu(c, dsum):
            r0 = pl.multiple_of(c * ROWS, ROWS)
            zpad[pl.ds(CV_PAD + r0, ROWS), :] = a_ref[pl.ds(r0, ROWS), :] * _sigmoid(g_ref[pl.ds(r0, ROWS), :])
            d = dz_ref[pl.ds(r0, ROWS), :]
            dpad[pl.ds(r0, ROWS), :] = d
            return dsum + jnp.sum(d, axis=0, keepdims=True)

        dsum = lax.fori_loop(0, nchunk, glu, jnp.zeros((1, LANE), F32))
        db_ref[...] = jnp.broadcast_to(dsum, db_ref.shape)

        def conv(c, _):
            r0 = pl.multiple_of(c * ROWS, ROWS)
            d = dpad[pl.ds(r0, ROWS), :]
            acc = jnp.zeros((ROWS, LANE), F32)
            for j in range(CV_KERNEL):
                acc = acc + w_ref[pl.ds(j, 1), :] * dpad[pl.ds(r0 + (CV_KERNEL - 1) - j, ROWS), :]
                zs = zpad[pl.ds(r0 + (CV_PAD - CV_KERNEL + 1) + j, ROWS), :]
                dw_ref[pl.ds(j, 1), :] += jnp.sum(d * zs, axis=0, keepdims=True)
            a = a_ref[pl.ds(r0, ROWS), :]
            sg = _sigmoid(g_ref[pl.ds(r0, ROWS), :])
            da_ref[pl.ds(r0, ROWS), :] = (acc * sg).astype(da_ref.dtype)
            dg_ref[pl.ds(r0, ROWS), :] = (acc * a * sg * (1.0 - sg)).astype(dg_ref.dtype)
            return 0

        lax.fori_loop(0, nchunk, conv, 0)

    blk = pl.BlockSpec((S, LANE), lambda j: (0, j))
    da, dg, dw, db = pl.pallas_call(
        body, name="cv_bwd", grid=(BRANCH_W // LANE,),
        in_specs=[_colblk(S, OFF_CV), _colblk(S, OFF_CV + BRANCH_W), pl.BlockSpec((32, LANE), lambda j: (0, j)), blk],
        out_specs=[blk, blk, pl.BlockSpec((32, LANE), lambda j: (0, j)), pl.BlockSpec((8, LANE), lambda j: (0, j))],
        out_shape=[jax.ShapeDtypeStruct((S, BRANCH_W), BF16), jax.ShapeDtypeStruct((S, BRANCH_W), BF16),
                   jax.ShapeDtypeStruct((32, BRANCH_W), F32), jax.ShapeDtypeStruct((8, BRANCH_W), F32)],
        scratch_shapes=[pltpu.VMEM((CV_PAD + S, LANE), F32), pltpu.VMEM((S + CV_PAD, LANE), F32)],
        compiler_params=_cp(("parallel",)),
    )(proj, proj, w32, dzc)
    return da, dg, dw, db[0]


def ln_silu_fwd(z, w, b, tm=512):
    S, C = z.shape
    tm = _tile(S, tm)

    def body(z_ref, w_ref, b_ref, o_ref):
        zv = z_ref[...]
        mu = jnp.mean(zv, axis=-1, keepdims=True)
        zc = zv - mu
        rstd = lax.rsqrt(jnp.mean(zc * zc, axis=-1, keepdims=True) + EPS)
        y = zc * rstd * w_ref[...] + b_ref[...]
        o_ref[...] = (y * _sigmoid(y)).astype(o_ref.dtype)

    row = pl.BlockSpec((tm, C), lambda i: (i, 0))
    vec = pl.BlockSpec((1, C), lambda i: (0, 0))
    return pl.pallas_call(
        body, name="ln_silu_fwd", grid=(S // tm,), in_specs=[row, vec, vec], out_specs=row,
        out_shape=jax.ShapeDtypeStruct((S, C), BF16), compiler_params=_cp(("parallel",)),
    )(z, w.reshape(1, C), b.reshape(1, C))


def ln_silu_bwd(z, w, b, dout, tm=512):
    S, C = z.shape
    tm = _tile(S, tm)

    def body(z_ref, w_ref, b_ref, do_ref, dz_ref, dw_ref, db_ref):
        zv = z_ref[...]
        wv = w_ref[...]
        mu = jnp.mean(zv, axis=-1, keepdims=True)
        zc = zv - mu
        rstd = lax.rsqrt(jnp.mean(zc * zc, axis=-1, keepdims=True) + EPS)
        xh = zc * rstd
        y = xh * wv + b_ref[...]
        dy = do_ref[...].astype(F32) * _dsilu(y, _sigmoid(y))

        @pl.when(pl.program_id(0) == 0)
        def _():
            dw_ref[...] = jnp.zeros_like(dw_ref)
            db_ref[...] = jnp.zeros_like(db_ref)

        dw_ref[...] += jnp.sum(dy * xh, axis=0, keepdims=True)
        db_ref[...] += jnp.sum(dy, axis=0, keepdims=True)
        dxh = dy * wv
        dz_ref[...] = rstd * (dxh - jnp.mean(dxh, axis=-1, keepdims=True) - xh * jnp.mean(dxh * xh, axis=-1, keepdims=True))

    row = pl.BlockSpec((tm, C), lambda i: (i, 0))
    vec = pl.BlockSpec((1, C), lambda i: (0, 0))
    dz, dw, db = pl.pallas_call(
        body, name="ln_silu_bwd", grid=(S // tm,), in_specs=[row, vec, vec, row], out_specs=[row, vec, vec],
        out_shape=[jax.ShapeDtypeStruct((S, C), F32), jax.ShapeDtypeStruct((1, C), F32), jax.ShapeDtypeStruct((1, C), F32)],
        compiler_params=_cp(("arbitrary",)),
    )(z, w.reshape(1, C), b.reshape(1, C), dout)
    return dz, dw.reshape(C), db.reshape(C)


PL_PAD = 16


def _pool_counts(r0, win):
    t = r0 + lax.broadcasted_iota(I32, (ROWS, LANE), 0)
    return jnp.minimum(t + 1, win).astype(F32)


def pool_fwd(proj, wg, scale):
    S = proj.shape[0]
    nchunk = S // ROWS

    def body(u_ref, w_ref, s_ref, o_ref, upad):
        g = pl.program_id(0)
        upad[pl.ds(0, PL_PAD), :] = jnp.zeros((PL_PAD, LANE), F32)

        def fill(c, _):
            r0 = pl.multiple_of(c * ROWS, ROWS)
            upad[pl.ds(PL_PAD + r0, ROWS), :] = u_ref[pl.ds(r0, ROWS), :]
            return 0

        lax.fori_loop(0, nchunk, fill, 0)
        wb = w_ref[...].astype(BF16)
        for gi, win in enumerate(POOL_WINDOWS):
            @pl.when(g == gi)
            def _(win=win):
                def chunk(c, _):
                    r0 = pl.multiple_of(c * ROWS, ROWS)
                    u = upad[pl.ds(PL_PAD + r0, ROWS), :]
                    ws = u
                    for j in range(1, win):
                        ws = ws + upad[pl.ds(PL_PAD + r0 - j, ROWS), :]
                    pooled = ws / _pool_counts(r0, win) - u
                    o_ref[pl.ds(r0, ROWS), :] = (_dot(pooled.astype(BF16), wb, NN) * s_ref[...]).astype(o_ref.dtype)
                    return 0

                lax.fori_loop(0, nchunk, chunk, 0)

    return pl.pallas_call(
        body, name="pool_fwd", grid=(len(POOL_WINDOWS),),
        in_specs=[_colblk(S, OFF_PL), pl.BlockSpec((None, LANE, LANE), lambda j: (j, 0, 0)), pl.BlockSpec((1, LANE), lambda j: (0, j))],
        out_specs=pl.BlockSpec((S, LANE), lambda j: (0, j)), out_shape=jax.ShapeDtypeStruct((S, BRANCH_W), BF16),
        scratch_shapes=[pltpu.VMEM((PL_PAD + S, LANE), F32)],
        compiler_params=_cp(("parallel",)),
    )(proj, wg, scale.reshape(1, BRANCH_W))


def pool_bwd(proj, wg, scale, dy):
    S = proj.shape[0]
    nchunk = S // ROWS

    def body(u_ref, w_ref, s_ref, dy_ref, du_ref, dw_ref, ds_ref, upad, dpn, nd):
        g = pl.program_id(0)
        upad[pl.ds(0, PL_PAD), :] = jnp.zeros((PL_PAD, LANE), F32)
        dpn[pl.ds(S, PL_PAD), :] = jnp.zeros((PL_PAD, LANE), F32)

        def fill(c, _):
            r0 = pl.multiple_of(c * ROWS, ROWS)
            upad[pl.ds(PL_PAD + r0, ROWS), :] = u_ref[pl.ds(r0, ROWS), :]
            return 0

        lax.fori_loop(0, nchunk, fill, 0)
        wb = w_ref[...].astype(BF16)
        sv = s_ref[...]
        for gi, win in enumerate(POOL_WINDOWS):
            @pl.when(g == gi)
            def _(win=win):
                def chunk(c, carry):
                    dw, dsc = carry
                    r0 = pl.multiple_of(c * ROWS, ROWS)
                    u = upad[pl.ds(PL_PAD + r0, ROWS), :]
                    ws = u
                    for j in range(1, win):
                        ws = ws + upad[pl.ds(PL_PAD + r0 - j, ROWS), :]
                    cnt = _pool_counts(r0, win)
                    pooled = (ws / cnt - u).astype(BF16)
                    dyv = dy_ref[pl.ds(r0, ROWS), :].astype(F32)
                    dsc = dsc + jnp.sum(dyv * _dot(pooled, wb, NN), axis=0, keepdims=True)
                    dys = (dyv * sv).astype(BF16)
                    dw = dw + _dot(pooled, dys, TN)
                    dp = _dot(dys, wb, NT)
                    dpn[pl.ds(r0, ROWS), :] = dp / cnt
                    nd[pl.ds(r0, ROWS), :] = -dp
                    return dw, dsc

                dw, dsc = lax.fori_loop(0, nchunk, chunk, (jnp.zeros((LANE, LANE), F32), jnp.zeros((1, LANE), F32)))
                dw_ref[...] = dw
                ds_ref[...] = jnp.broadcast_to(dsc, ds_ref.shape)

                def spread(c, _):
                    r0 = pl.multiple_of(c * ROWS, ROWS)
                    acc = nd[pl.ds(r0, ROWS), :]
                    for j in range(win):
                        acc = acc + dpn[pl.ds(r0 + j, ROWS), :]
                    du_ref[pl.ds(r0, ROWS), :] = acc.astype(du_ref.dtype)
                    return 0

                lax.fori_loop(0, nchunk, spread, 0)

    blk = pl.BlockSpec((S, LANE), lambda j: (0, j))
    du, dw, ds = pl.pallas_call(
        body, name="pool_bwd", grid=(len(POOL_WINDOWS),),
        in_specs=[_colblk(S, OFF_PL), pl.BlockSpec((None, LANE, LANE), lambda j: (j, 0, 0)), pl.BlockSpec((1, LANE), lambda j: (0, j)), blk],
        out_specs=[blk, pl.BlockSpec((None, LANE, LANE), lambda j: (j, 0, 0)), pl.BlockSpec((8, LANE), lambda j: (0, j))],
        out_shape=[jax.ShapeDtypeStruct((S, BRANCH_W), BF16), jax.ShapeDtypeStruct((len(POOL_WINDOWS), LANE, LANE), F32),
                   jax.ShapeDtypeStruct((8, BRANCH_W), F32)],
        scratch_shapes=[pltpu.VMEM((PL_PAD + S, LANE), F32), pltpu.VMEM((S + PL_PAD, LANE), F32), pltpu.VMEM((S, LANE), F32)],
        compiler_params=_cp(("parallel",)),
    )(proj, wg, scale.reshape(1, BRANCH_W), dy)
    return du, dw, ds[0]


LR_PAD = 8
GELU_C = math.sqrt(2.0 / math.pi)
GELU_A = 0.044715


def _gelu(y):
    return 0.5 * y * (1.0 + jnp.tanh(GELU_C * (y + GELU_A * y * y * y)))


def _dgelu(y):
    t = jnp.tanh(GELU_C * (y + GELU_A * y * y * y))
    return 0.5 * (1.0 + t) + 0.5 * y * (1.0 - t * t) * GELU_C * (1.0 + 3.0 * GELU_A * y * y)


def _lru_gates(xpad, r0, cw_ref, cb, wa, ba, wx, bx, sp8):
    xc = jnp.broadcast_to(cb, (ROWS, LANE))
    for j in range(LRU_CONV):
        xc = xc + cw_ref[pl.ds(j, 1), :] * xpad[pl.ds(r0 + (LR_PAD - LRU_CONV + 1) + j, ROWS), :]
    xb = xc.astype(BF16)
    r = _sigmoid(_dot(xb, wa, NN) + ba)
    ig = _sigmoid(_dot(xb, wx, NN) + bx)
    la = -sp8 * r
    a = jnp.exp(la)
    s = jnp.sqrt(-jnp.tanh(la) * (a * a + 1.0))
    return xc, r, ig, a, s


def lru_fwd(proj, cw8, cb, wa_bd, ba, wx_bd, bx, sp8):
    S = proj.shape[0]
    nchunk = S // ROWS

    def body(x_ref, y_ref, cw_ref, cb_ref, wa_ref, ba_ref, wx_ref, bx_ref, sp_ref, o_ref, h_ref, xpad, a_s):
        xpad[pl.ds(0, LR_PAD), :] = jnp.zeros((LR_PAD, LANE), F32)

        def fill(c, _):
            r0 = pl.multiple_of(c * ROWS, ROWS)
            xpad[pl.ds(LR_PAD + r0, ROWS), :] = x_ref[pl.ds(r0, ROWS), :]
            return 0

        lax.fori_loop(0, nchunk, fill, 0)
        wa = wa_ref[...].astype(BF16)
        wx = wx_ref[...].astype(BF16)

        def gates(c, _):
            r0 = pl.multiple_of(c * ROWS, ROWS)
            xc, r, ig, a, s = _lru_gates(xpad, r0, cw_ref, cb_ref[...], wa, ba_ref[...], wx, bx_ref[...], sp_ref[...])
            a_s[pl.ds(r0, ROWS), :] = a
            h_ref[pl.ds(r0, ROWS), :] = s * (ig * xc)
            return 0

        lax.fori_loop(0, nchunk, gates, 0)

        r8 = lax.broadcasted_iota(I32, (8, LANE), 0)

        def scan(i, h):
            base = pl.multiple_of(i * 8, 8)
            a8 = a_s[pl.ds(base, 8), :]
            b8 = h_ref[pl.ds(base, 8), :]
            out = jnp.zeros((8, LANE), F32)
            for u in range(8):
                h = a8[u:u + 1, :] * h + b8[u:u + 1, :]
                out = jnp.where(r8 == u, h, out)
            h_ref[pl.ds(base, 8), :] = out
            return h

        lax.fori_loop(0, S // 8, scan, jnp.zeros((1, LANE), F32))

        def gate_out(c, _):
            r0 = pl.multiple_of(c * ROWS, ROWS)
            o_ref[pl.ds(r0, ROWS), :] = (h_ref[pl.ds(r0, ROWS), :] * _gelu(y_ref[pl.ds(r0, ROWS), :])).astype(o_ref.dtype)
            return 0

        lax.fori_loop(0, nchunk, gate_out, 0)

    vec = pl.BlockSpec((1, LANE), lambda j: (0, j))
    mat = pl.BlockSpec((None, LANE, LANE), lambda j: (j, 0, 0))
    blk = pl.BlockSpec((S, LANE), lambda j: (0, j))
    return pl.pallas_call(
        body, name="lru_fwd", grid=(BRANCH_W // LANE,),
        in_specs=[_colblk(S, OFF_LX), _colblk(S, OFF_LY), pl.BlockSpec((8, LANE), lambda j: (0, j)), vec, mat, vec, mat, vec, vec],
        out_specs=[blk, blk],
        out_shape=[jax.ShapeDtypeStruct((S, BRANCH_W), BF16), jax.ShapeDtypeStruct((S, BRANCH_W), F32)],
        scratch_shapes=[pltpu.VMEM((LR_PAD + S, LANE), F32), pltpu.VMEM((S, LANE), F32)],
        compiler_params=_cp(("parallel",)),
    )(proj, proj, cw8, cb.reshape(1, -1), wa_bd, ba.reshape(1, -1), wx_bd, bx.reshape(1, -1), sp8.reshape(1, -1))


def lru_bwd(proj, cw8, cb, wa_bd, ba, wx_bd, bx, sp8, h, dout):
    S = proj.shape[0]
    nchunk = S // ROWS

    def body(x_ref, y_ref, cw_ref, cb_ref, wa_ref, ba_ref, wx_ref, bx_ref, sp_ref, h_ref, do_ref,
             dx_ref, dy_ref, dcw_ref, dcb_ref, dwa_ref, dba_ref, dwx_ref, dbx_ref, dsp_ref,
             xpad, a_s, g_s, hpad, dxc):
        xpad[pl.ds(0, LR_PAD), :] = jnp.zeros((LR_PAD, LANE), F32)
        hpad[pl.ds(0, LR_PAD), :] = jnp.zeros((LR_PAD, LANE), F32)
        dxc[pl.ds(S, LR_PAD), :] = jnp.zeros((LR_PAD, LANE), F32)
        dcw_ref[...] = jnp.zeros_like(dcw_ref)
        wa = wa_ref[...].astype(BF16)
        wx = wx_ref[...].astype(BF16)
        cbv, bav, bxv, spv = cb_ref[...], ba_ref[...], bx_ref[...], sp_ref[...]

        def fill(c, _):
            r0 = pl.multiple_of(c * ROWS, ROWS)
            xpad[pl.ds(LR_PAD + r0, ROWS), :] = x_ref[pl.ds(r0, ROWS), :]
            hv = h_ref[pl.ds(r0, ROWS), :]
            hpad[pl.ds(LR_PAD + r0, ROWS), :] = hv
            yv = y_ref[pl.ds(r0, ROWS), :]
            dov = do_ref[pl.ds(r0, ROWS), :].astype(F32)
            g_s[pl.ds(r0, ROWS), :] = dov * _gelu(yv)
            dy_ref[pl.ds(r0, ROWS), :] = (dov * hv * _dgelu(yv)).astype(dy_ref.dtype)
            return 0

        lax.fori_loop(0, nchunk, fill, 0)

        def gates(c, _):
            r0 = pl.multiple_of(c * ROWS, ROWS)
            _, _, _, a, _ = _lru_gates(xpad, r0, cw_ref, cbv, wa, bav, wx, bxv, spv)
            a_s[pl.ds(r0, ROWS), :] = a
            return 0

        lax.fori_loop(0, nchunk, gates, 0)

        r8 = lax.broadcasted_iota(I32, (8, LANE), 0)

        def rscan(i, carry):
            base = pl.multiple_of(S - 8 - i * 8, 8)
            a8 = a_s[pl.ds(base, 8), :]
            g8 = g_s[pl.ds(base, 8), :]
            out = jnp.zeros((8, LANE), F32)
            for u in reversed(range(8)):
                gt = g8[u:u + 1, :] + carry
                out = jnp.where(r8 == u, gt, out)
                carry = a8[u:u + 1, :] * gt
            g_s[pl.ds(base, 8), :] = out
            return carry

        lax.fori_loop(0, S // 8, rscan, jnp.zeros((1, LANE), F32))

        def chain(c, carry):
            dwa, dwx, dba, dbx, dsp, dcb = carry
            r0 = pl.multiple_of(c * ROWS, ROWS)
            xc, r, ig, a, s = _lru_gates(xpad, r0, cw_ref, cbv, wa, bav, wx, bxv, spv)
            gt = g_s[pl.ds(r0, ROWS), :]
            hprev = hpad[pl.ds(r0 + LR_PAD - 1, ROWS), :]
            da = gt * hprev - gt * ig * xc * (a / s)
            dig = gt * s * xc
            dla = da * a
            dsp = dsp + jnp.sum(-dla * r, axis=0, keepdims=True)
            dpr = (-dla * spv) * r * (1.0 - r)
            dpi = dig * ig * (1.0 - ig)
            dprb, dpib, xb = dpr.astype(BF16), dpi.astype(BF16), xc.astype(BF16)
            d = gt * s * ig + _dot(dprb, wa, NT) + _dot(dpib, wx, NT)
            dwa = dwa + _dot(xb, dprb, TN)
            dwx = dwx + _dot(xb, dpib, TN)
            dba = dba + jnp.sum(dpr, axis=0, keepdims=True)
            dbx = dbx + jnp.sum(dpi, axis=0, keepdims=True)
            dcb = dcb + jnp.sum(d, axis=0, keepdims=True)
            dxc[pl.ds(r0, ROWS), :] = d
            for j in range(LRU_CONV):
                xs = xpad[pl.ds(r0 + (LR_PAD - LRU_CONV + 1) + j, ROWS), :]
                dcw_ref[pl.ds(j, 1), :] += jnp.sum(d * xs, axis=0, keepdims=True)
            return dwa, dwx, dba, dbx, dsp, dcb

        zm, zv = jnp.zeros((LANE, LANE), F32), jnp.zeros((1, LANE), F32)
        dwa, dwx, dba, dbx, dsp, dcb = lax.fori_loop(0, nchunk, chain, (zm, zm, zv, zv, zv, zv))
        dwa_ref[...] = dwa
        dwx_ref[...] = dwx
        dba_ref[...] = jnp.broadcast_to(dba, dba_ref.shape)
        dbx_ref[...] = jnp.broadcast_to(dbx, dbx_ref.shape)
        dsp_ref[...] = jnp.broadcast_to(dsp, dsp_ref.shape)
        dcb_ref[...] = jnp.broadcast_to(dcb, dcb_ref.shape)

        def convt(c, _):
            r0 = pl.multiple_of(c * ROWS, ROWS)
            acc = jnp.zeros((ROWS, LANE), F32)
            for j in range(LRU_CONV):
                acc = acc + cw_ref[pl.ds(j, 1), :] * dxc[pl.ds(r0 + (LRU_CONV - 1) - j, ROWS), :]
            dx_ref[pl.ds(r0, ROWS), :] = acc.astype(dx_ref.dtype)
            return 0

        lax.fori_loop(0, nchunk, convt, 0)

    vec = pl.BlockSpec((1, LANE), lambda j: (0, j))
    vec8 = pl.BlockSpec((8, LANE), lambda j: (0, j))
    mat = pl.BlockSpec((None, LANE, LANE), lambda j: (j, 0, 0))
    blk = pl.BlockSpec((S, LANE), lambda j: (0, j))
    nblk = BRANCH_W // LANE
    v8 = jax.ShapeDtypeStruct((8, BRANCH_W), F32)
    m4 = jax.ShapeDtypeStruct((nblk, LANE, LANE), F32)
    big = jax.ShapeDtypeStruct((S, BRANCH_W), BF16)
    seq = pltpu.VMEM((S, LANE), F32)
    dx, dy, dcw, dcb, dwa, dba, dwx, dbx, dsp = pl.pallas_call(
        body, name="lru_bwd", grid=(nblk,),
        in_specs=[_colblk(S, OFF_LX), _colblk(S, OFF_LY), vec8, vec, mat, vec, mat, vec, vec, blk, blk],
        out_specs=[blk, blk, vec8, vec8, mat, vec8, mat, vec8, vec8],
        out_shape=[big, big, v8, v8, m4, v8, m4, v8, v8],
        scratch_shapes=[pltpu.VMEM((LR_PAD + S, LANE), F32), seq, seq, pltpu.VMEM((LR_PAD + S, LANE), F32),
                        pltpu.VMEM((S + LR_PAD, LANE), F32)],
        compiler_params=_cp(("parallel",)),
    )(proj, proj, cw8, cb.reshape(1, -1), wa_bd, ba.reshape(1, -1), wx_bd, bx.reshape(1, -1), sp8.reshape(1, -1), h, dout)
    return dx, dy, dcw, dcb[0], dwa, dba[0], dwx, dbx[0], dsp[0]


MG_COLS = 512
N_BRANCH = 4


def _gate_spec(tm, k):
    return pl.BlockSpec((tm, MG_COLS), lambda j, i: (i, (OFF_GATE + k * D_MODEL) // MG_COLS + j))


def merge_fwd(ups, proj, gate_b, tm=256):
    S = proj.shape[0]
    tm = _tile(S, tm)

    def body(u0, u1, u2, u3, g0, g1, g2, g3, gb_ref, o_ref):
        acc = jnp.zeros((tm, MG_COLS), F32)
        for k, (u, g) in enumerate(((u0, g0), (u1, g1), (u2, g2), (u3, g3))):
            acc = acc + _sigmoid(g[...] + gb_ref[pl.ds(k, 1), :]) * u[...]
        o_ref[...] = acc.astype(o_ref.dtype)

    blk = pl.BlockSpec((tm, MG_COLS), lambda j, i: (i, j))
    return pl.pallas_call(
        body, name="merge_fwd", grid=(D_MODEL // MG_COLS, S // tm),
        in_specs=[blk] * N_BRANCH + [_gate_spec(tm, k) for k in range(N_BRANCH)] + [pl.BlockSpec((N_BRANCH, MG_COLS), lambda j, i: (0, j))],
        out_specs=blk, out_shape=jax.ShapeDtypeStruct((S, D_MODEL), BF16),
        compiler_params=_cp(("parallel", "parallel")),
    )(*ups, proj, proj, proj, proj, gate_b)


def merge_bwd(dmerged, ups, proj, gate_b, tm=256):
    S = proj.shape[0]
    tm = _tile(S, tm)

    def body(dm_ref, u0, u1, u2, u3, g0, g1, g2, g3, gb_ref, du0, du1, du2, du3, dg0, dg1, dg2, dg3, dgb_ref):
        @pl.when(pl.program_id(1) == 0)
        def _():
            dgb_ref[...] = jnp.zeros_like(dgb_ref)

        dm = dm_ref[...].astype(F32)
        for k, (u, g, du, dg) in enumerate(((u0, g0, du0, dg0), (u1, g1, du1, dg1), (u2, g2, du2, dg2), (u3, g3, du3, dg3))):
            sg = _sigmoid(g[...] + gb_ref[pl.ds(k, 1), :])
            du[...] = (dm * sg).astype(du.dtype)
            dgk = dm * u[...] * sg * (1.0 - sg)
            dg[...] = dgk.astype(dg.dtype)
            dgb_ref[pl.ds(8 * k, 8), :] += jnp.broadcast_to(jnp.sum(dgk, axis=0, keepdims=True), (8, MG_COLS))

    blk = pl.BlockSpec((tm, MG_COLS), lambda j, i: (i, j))
    big = jax.ShapeDtypeStruct((S, D_MODEL), BF16)
    outs = pl.pallas_call(
        body, name="merge_bwd", grid=(D_MODEL // MG_COLS, S // tm),
        in_specs=[blk] * (1 + N_BRANCH) + [_gate_spec(tm, k) for k in range(N_BRANCH)] + [pl.BlockSpec((N_BRANCH, MG_COLS), lambda j, i: (0, j))],
        out_specs=[blk] * (2 * N_BRANCH) + [pl.BlockSpec((8 * N_BRANCH, MG_COLS), lambda j, i: (0, j))],
        out_shape=[big] * (2 * N_BRANCH) + [jax.ShapeDtypeStruct((8 * N_BRANCH, D_MODEL), F32)],
        compiler_params=_cp(("parallel", "arbitrary")),
    )(dmerged, *ups, proj, proj, proj, proj, gate_b)
    return outs[:N_BRANCH], outs[N_BRANCH:2 * N_BRANCH], outs[-1].reshape(N_BRANCH, 8, D_MODEL)[:, 0]


def attn_fwd(q, kv, tm=512):
    S = q.shape[0]
    M = kv.shape[0]
    tm = _tile(S, tm)
    scale = XA_HD ** -0.5

    def body(q_ref, kv_ref, o_ref):
        for hh in range(XA_HEADS):
            cs = pl.ds(hh * XA_HD, XA_HD)
            qh = q_ref[:, cs]
            kh = kv_ref[:, cs]
            vh = kv_ref[:, pl.ds(D_MODEL + hh * XA_HD, XA_HD)]
            s = _dot(qh, kh, NT) * scale
            p = jnp.exp(s - jnp.max(s, axis=-1, keepdims=True))
            p = p / jnp.sum(p, axis=-1, keepdims=True)
            o_ref[:, cs] = _dot(p.astype(BF16), vh, NN).astype(o_ref.dtype)

    return pl.pallas_call(
        body, name="attn_fwd", grid=(S // tm,),
        in_specs=[pl.BlockSpec((tm, D_MODEL), lambda i: (i, 0)), pl.BlockSpec((M, 2 * D_MODEL), lambda i: (0, 0))],
        out_specs=pl.BlockSpec((tm, D_MODEL), lambda i: (i, 0)), out_shape=jax.ShapeDtypeStruct((S, D_MODEL), BF16),
        compiler_params=_cp(("parallel",)),
    )(q, kv)


def attn_bwd(q, kv, do, tm=512):
    S = q.shape[0]
    M = kv.shape[0]
    tm = _tile(S, tm)
    scale = XA_HD ** -0.5

    def body(q_ref, kv_ref, do_ref, dq_ref, dkv_ref):
        @pl.when(pl.program_id(0) == 0)
        def _():
            dkv_ref[...] = jnp.zeros_like(dkv_ref)

        for hh in range(XA_HEADS):
            cs = pl.ds(hh * XA_HD, XA_HD)
            vs = pl.ds(D_MODEL + hh * XA_HD, XA_HD)
            qh = q_ref[:, cs]
            kh = kv_ref[:, cs]
            vh = kv_ref[:, vs]
            doh = do_ref[:, cs]
            s = _dot(qh, kh, NT) * scale
            p = jnp.exp(s - jnp.max(s, axis=-1, keepdims=True))
            p = p / jnp.sum(p, axis=-1, keepdims=True)
            dp = _dot(doh, vh, NT)
            ds = (p * (dp - jnp.sum(dp * p, axis=-1, keepdims=True)) * scale).astype(BF16)
            dq_ref[:, cs] = _dot(ds, kh, NN).astype(dq_ref.dtype)
            dkv_ref[:, cs] += _dot(ds, qh, TN)
            dkv_ref[:, vs] += _dot(p.astype(BF16), doh, TN)

    row = pl.BlockSpec((tm, D_MODEL), lambda i: (i, 0))
    full = pl.BlockSpec((M, 2 * D_MODEL), lambda i: (0, 0))
    return pl.pallas_call(
        body, name="attn_bwd", grid=(S // tm,), in_specs=[row, full, row], out_specs=[row, full],
        out_shape=[jax.ShapeDtypeStruct((S, D_MODEL), BF16), jax.ShapeDtypeStruct((M, 2 * D_MODEL), F32)],
        compiler_params=_cp(("arbitrary",)),
    )(q, kv, do)


def sum_parts(parts, tm=256):
    n, R, C = parts.shape
    tm = _tile(R, tm)

    def body(p_ref, o_ref):
        acc = p_ref[0].astype(F32)
        for j in range(1, n):
            acc = acc + p_ref[j].astype(F32)
        o_ref[...] = acc

    return pl.pallas_call(
        body, name="sum_parts", grid=(R // tm,),
        in_specs=[pl.BlockSpec((n, tm, C), lambda i: (0, i, 0))], out_specs=pl.BlockSpec((tm, C), lambda i: (i, 0)),
        out_shape=jax.ShapeDtypeStruct((R, C), F32), compiler_params=_cp(("parallel",)),
    )(parts)


def adamw(w, g, m, v, tm=256):
    R, C = w.shape
    tm = _tile(R, tm)
    c1 = 1.0 / (1.0 - ADAM_B1 ** ADAM_STEP)
    c2 = 1.0 / (1.0 - ADAM_B2 ** ADAM_STEP)

    def body(w_ref, g_ref, m_ref, v_ref, d_ref, nm_ref, nv_ref):
        gv = g_ref[...]
        nm = ADAM_B1 * m_ref[...] + (1.0 - ADAM_B1) * gv
        nv = ADAM_B2 * v_ref[...] + (1.0 - ADAM_B2) * (gv * gv)
        nm_ref[...] = nm
        nv_ref[...] = nv
        d_ref[...] = -ADAM_LR * ((nm * c1) / (jnp.sqrt(nv * c2) + ADAM_EPS) + ADAM_WD * w_ref[...])

    blk = pl.BlockSpec((tm, C), lambda i: (i, 0))
    sd = jax.ShapeDtypeStruct((R, C), F32)
    return pl.pallas_call(
        body, name="adamw", grid=(R // tm,), in_specs=[blk] * 4, out_specs=[blk] * 3, out_shape=[sd] * 3,
        compiler_params=_cp(("parallel",)),
    )(w, g, m, v)


ANY = pl.BlockSpec(memory_space=pl.ANY)


def _place():
    return lax.axis_index("x"), lax.axis_index("y"), lax.axis_index("c")


def _slot(px, py, pc):
    return 4 * px + 2 * py + pc


def all_gather(name, shards):
    n = len(shards)

    def body(*refs):
        x_refs, out_refs = refs[:n], refs[n:2 * n]
        send_sems, recv_sems, local_sems = refs[2 * n:]
        x, y, c = _place()
        me, sibling = (x, y, c), (x, y, 1 - c)
        chips = [(1 - x, y), (x, 1 - y), (1 - x, 1 - y)]

        def copy(a, k, block, to, src=None):
            rows = out_refs[a].at[_slot(*block)]
            return pltpu.make_async_remote_copy(
                src_ref=rows if src is None else src, dst_ref=rows,
                send_sem=send_sems.at[7 * a + k], recv_sem=recv_sems.at[7 * a + k],
                device_id=to, device_id_type=MESH)

        mine = [pltpu.make_async_copy(x_refs[a], out_refs[a].at[_slot(*me)], local_sems.at[a]) for a in range(n)]
        for cp in mine:
            cp.start()
        first = []
        for a in range(n):
            first.append(copy(a, 0, me, sibling, src=x_refs[a]))
            first += [copy(a, 1 + j, me, (*chip, c), src=x_refs[a]) for j, chip in enumerate(chips)]
        for cp in first:
            cp.start()
        passed = []
        for a in range(n):
            for j, chip in enumerate(chips):
                copy(a, 1 + j, (*chip, c), me).wait_recv()
                cp = copy(a, 4 + j, (*chip, c), sibling)
                cp.start()
                passed.append(cp)
        for a in range(n):
            copy(a, 0, sibling, me).wait_recv()
            for j, chip in enumerate(chips):
                copy(a, 4 + j, (*chip, 1 - c), me).wait_recv()
        for cp in first + passed:
            cp.wait_send()
        for cp in mine:
            cp.wait()

    return pl.pallas_call(
        body, name=name, in_specs=[ANY] * n, out_specs=[ANY] * n,
        out_shape=[jax.ShapeDtypeStruct((N_DEV, *s.shape), s.dtype) for s in shards],
        scratch_shapes=[pltpu.SemaphoreType.DMA((7 * n,)), pltpu.SemaphoreType.DMA((7 * n,)), pltpu.SemaphoreType.DMA((n,))],
    )(*shards)


def exchange_blocks(name, grads):
    n = len(grads)
    rels = [(dx, dy, dc) for dx in (0, 1) for dy in (0, 1) for dc in (0, 1)][1:]

    def body(*refs):
        g_refs, land_refs = refs[:n], refs[n:2 * n]
        send_sems, recv_sems, local_sems = refs[2 * n:]
        x, y, c = _place()
        me = _slot(x, y, c)

        def peer(rel):
            return tuple(1 - v if d else v for v, d in zip((x, y, c), rel))

        mine = [pltpu.make_async_copy(g_refs[a].at[me], land_refs[a].at[me], local_sems.at[a]) for a in range(n)]
        for cp in mine:
            cp.start()
        sends = []
        for a in range(n):
            for r, rel in enumerate(rels):
                p = peer(rel)
                sends.append(pltpu.make_async_remote_copy(
                    src_ref=g_refs[a].at[_slot(*p)], dst_ref=land_refs[a].at[me],
                    send_sem=send_sems.at[7 * a + r], recv_sem=recv_sems.at[7 * a + r],
                    device_id=p, device_id_type=MESH))
        for cp in sends:
            cp.start()
        for a in range(n):
            for r, rel in enumerate(rels):
                p = peer(rel)
                landed = land_refs[a].at[_slot(*p)]
                pltpu.make_async_remote_copy(
                    src_ref=landed, dst_ref=landed, send_sem=send_sems.at[7 * a + r], recv_sem=recv_sems.at[7 * a + r],
                    device_id=(x, y, c), device_id_type=MESH).wait_recv()
        for cp in sends:
            cp.wait_send()
        for cp in mine:
            cp.wait()

    return pl.pallas_call(
        body, name=name, in_specs=[ANY] * n, out_specs=[ANY] * n,
        out_shape=[jax.ShapeDtypeStruct(g.shape, g.dtype) for g in grads],
        scratch_shapes=[pltpu.SemaphoreType.DMA((7 * n,)), pltpu.SemaphoreType.DMA((7 * n,)), pltpu.SemaphoreType.DMA((n,))],
    )(*grads)


WEIGHTS = ['norm_mix_w', 'w_in', 'hg_lb_raw', 'hg_norm_w', 'cv_dw_w', 'cv_dw_b', 'cv_ln_w', 'cv_ln_b', 'pl_w', 'pl_scale',
           'lru_conv_w', 'lru_conv_b', 'lru_wa', 'lru_ba', 'lru_wx', 'lru_bx', 'lru_lambda', 'gate_b', 'w_branch', 'w_out',
           'norm_mem_w', 'mem_norm_w', 'xa_wq', 'xa_wkv', 'xa_wo', 'norm_ffn_w', 'ffn_w1', 'ffn_w2', 'final_norm_w']
BIG = ('w_in', 'w_branch', 'w_out', 'xa_wq', 'xa_wkv', 'xa_wo', 'ffn_w1', 'ffn_w2')
SMALL_SHARDED = ('cv_dw_w', 'lru_conv_w', 'gate_b')
SMALL = tuple(n for n in WEIGHTS if n not in BIG and n not in SMALL_SHARDED)
PACK_ROWS = 256


def _pack(arrs):
    flat = jnp.concatenate([a.reshape(-1).astype(F32) for a in arrs])
    tile = PACK_ROWS * LANE
    padded = -(-flat.shape[0] // tile) * tile
    return jnp.pad(flat, (0, padded - flat.shape[0])).reshape(-1, LANE)


def _unpack(packed, shapes):
    flat = packed.reshape(-1)
    out, off = [], 0
    for s in shapes:
        n = math.prod(s)
        out.append(flat[off:off + n].reshape(s))
        off += n
    return out


def _gather_last(g, shard_shape):
    nd = len(shard_shape)
    full = jnp.moveaxis(g, 0, nd - 1)
    return full.reshape(*shard_shape[:-1], N_DEV * shard_shape[-1])


def _block_diag(w):
    w2 = w.reshape(4, 2, 64, 64)
    z = jnp.zeros((4, 64, 64), w.dtype)
    return jnp.concatenate([jnp.concatenate([w2[:, 0], z], axis=2), jnp.concatenate([z, w2[:, 1]], axis=2)], axis=1)


def _block_diag_t(d):
    return jnp.stack([d[:, :64, :64], d[:, 64:, 64:]], axis=1).reshape(8, 64, 64)


def _lower_bounds(raw):
    lb = jnp.cumsum(jax.nn.softmax(raw.astype(F32), axis=0), axis=0)
    return lb - lb[0:1]


def _decay_rates(lam):
    return (LRU_C * jax.nn.softplus(-lam.astype(F32))).reshape(DEPTH, BRANCH_W)


def _relu2(acc):
    r = jnp.maximum(acc, 0.0)
    return acc, r * r


def _relu2_grad(acc, u):
    return (acc * 2.0 * jnp.maximum(u, 0.0),)


def _add(acc, e):
    return (acc + e,)


def _layer_fwd(x0, mem, p, g):
    h1 = rms_fwd("rms_mix", x0, p['norm_mix_w'])
    proj = mm_nt("mm_in", h1, g['w_in'])[0]
    b_hg, states, o_hg = hgrn_fwd(proj, p['lb'], p['hg_norm_w'])
    zc = cv_fwd(proj, p['cv_w32'], p['cv_dw_b'])
    b_cv = ln_silu_fwd(zc, p['cv_ln_w'], p['cv_ln_b'])
    b_pl = pool_fwd(proj, p['pl_w'], p['pl_scale'])
    b_lru, hst = lru_fwd(proj, p['lru_cw8'], p['lru_conv_b'], p['wa_bd'], p['lru_ba'], p['wx_bd'], p['lru_bx'], p['sp8'])
    branches = [b_hg, b_cv, b_pl, b_lru]
    ups = [mm_nn_cb("mm_up", branches[k], g['w_branch'][k])[0] for k in range(N_BRANCH)]
    merged = merge_fwd(ups, proj, p['gate_b'])
    x1 = mm_nn("mm_out", merged, g['w_out'], epi=_add, extras=(x0,))[0]
    h2 = rms_fwd("rms_mem", x1, p['norm_mem_w'])
    q = mm_nn("mm_q", h2, g['xa_wq'], out_dtype=BF16)[0]
    memn = rms_fwd("rms_memtok", mem, p['mem_norm_w'])
    kv = mm_nn_cb("mm_kv", memn, g['xa_wkv'], out_dtype=BF16)[0]
    oa = attn_fwd(q, kv)
    x2 = mm_nn("mm_o", oa, g['xa_wo'], epi=_add, extras=(x1,))[0]
    h3 = rms_fwd("rms_ffn", x2, p['norm_ffn_w'])
    u, act = mm_nn_cb("mm_ffn1", h3, g['ffn_w1'], epi=_relu2, out_dtypes=[F32, BF16])
    x3 = mm_nn("mm_ffn2", act, g['ffn_w2'], epi=_add, extras=(x2,))[0]
    res = dict(x0=x0, h1=h1, proj=proj, states=states, o_hg=o_hg, zc=zc, hst=hst, branches=branches, ups=ups, merged=merged,
               x1=x1, h2=h2, q=q, memn=memn, kv=kv, oa=oa, x2=x2, h3=h3, u=u, act=act)
    return x3, res


def _layer_bwd(dx3, mem, p, g, r):
    S = dx3.shape[0]
    gs, gb = {}, {}
    du = mm_nt("mm_dffn2", dx3, g['ffn_w2'], out_dtype=BF16, epi=_relu2_grad, extras=(r['u'],))[0]
    gb['ffn_w2'] = mm_tn("mm_gw2", r['act'], dx3).reshape(N_DEV, -1, D_MODEL)
    gb['ffn_w1'] = mm_tn_cb("mm_gw1", r['h3'], du, N_DEV)
    dh3 = mm_nt_cb("mm_dffn1", du, g['ffn_w1'], out_dtype=BF16)[0]
    dx2, gs['norm_ffn_w'] = rms_bwd("rmsb_ffn", r['x2'], p['norm_ffn_w'], dh3, dx3)
    doa = mm_nt("mm_do", dx2, g['xa_wo'], out_dtype=BF16)[0]
    gb['xa_wo'] = mm_tn("mm_gwo", r['oa'], dx2).reshape(N_DEV, -1, D_MODEL)
    dq, dkv = attn_bwd(r['q'], r['kv'], doa)
    gb['xa_wq'] = mm_tn("mm_gwq", r['h2'], dq).reshape(N_DEV, -1, D_MODEL)
    dh2 = mm_nt("mm_dq", dq, g['xa_wq'], out_dtype=BF16)[0]
    gb['xa_wkv'] = mm_tn_cb("mm_gwkv", r['memn'], dkv, N_DEV)
    dmemn = mm_nt_cb("mm_dkv", dkv, g['xa_wkv'], out_dtype=BF16)[0]
    _, gs['mem_norm_w'] = rms_bwd("rmsb_memtok", mem, p['mem_norm_w'], dmemn)
    dx1, gs['norm_mem_w'] = rms_bwd("rmsb_mem", r['x1'], p['norm_mem_w'], dh2, dx2)
    dmerged = mm_nt("mm_dout", dx1, g['w_out'], out_dtype=BF16)[0]
    gb['w_out'] = mm_tn("mm_gwout", r['merged'], dx1).reshape(N_DEV, -1, D_MODEL)
    dups, dgates, gs['gate_b'] = merge_bwd(dmerged, r['ups'], r['proj'], p['gate_b'])
    gb['w_branch'] = [mm_tn_cb("mm_gwb", r['branches'][k], dups[k], N_DEV) for k in range(N_BRANCH)]
    db = [mm_nt_cb("mm_dup", dups[k], g['w_branch'][k], out_dtype=BF16)[0] for k in range(N_BRANCH)]
    dq_, df_, dv_, dg_, gs['lb'], gs['hg_norm_w'] = hgrn_bwd(r['proj'], p['lb'], p['hg_norm_w'], r['states'], r['o_hg'], db[0])
    dzc, gs['cv_ln_w'], gs['cv_ln_b'] = ln_silu_bwd(r['zc'], p['cv_ln_w'], p['cv_ln_b'], db[1])
    dca, dcg, dcw, gs['cv_dw_b'] = cv_bwd(r['proj'], p['cv_w32'], dzc)
    gs['cv_dw_w'] = dcw[:CV_KERNEL]
    dpu, gs['pl_w'], gs['pl_scale'] = pool_bwd(r['proj'], p['pl_w'], p['pl_scale'], db[2])
    dlx, dly, dlcw, gs['lru_conv_b'], dwa, gs['lru_ba'], dwx, gs['lru_bx'], gs['sp8'] = lru_bwd(
        r['proj'], p['lru_cw8'], p['lru_conv_b'], p['wa_bd'], p['lru_ba'], p['wx_bd'], p['lru_bx'], p['sp8'], r['hst'], db[3])
    gs['lru_conv_w'] = dlcw[:LRU_CONV]
    gs['lru_wa'], gs['lru_wx'] = _block_diag_t(dwa), _block_diag_t(dwx)
    gs['lru_ba'], gs['lru_bx'] = gs['lru_ba'].reshape(8, 64), gs['lru_bx'].reshape(8, 64)
    dproj = jnp.concatenate([dq_, df_, dv_, dg_, dca, dcg, dpu, dlx, dly, *dgates], axis=1)
    gb['w_in'] = mm_tn("mm_gwin", dproj, r['h1']).reshape(N_DEV, -1, D_MODEL)
    dh1 = mm_nn("mm_din", dproj, g['w_in'], out_dtype=BF16)[0]
    dx0, gs['norm_mix_w'] = rms_bwd("rmsb_mix", r['x0'], p['norm_mix_w'], dh1, dx1)
    return dx0, gs, gb


def kernel(x, mem, norm_mix_w, w_in, hg_lb_raw, hg_norm_w, cv_dw_w, cv_dw_b, cv_ln_w, cv_ln_b, pl_w, pl_scale, lru_conv_w, lru_conv_b, lru_wa, lru_ba, lru_wx, lru_bx, lru_lambda, gate_b, w_branch, w_out, norm_mem_w, mem_norm_w, xa_wq, xa_wkv, xa_wo, norm_ffn_w, ffn_w1, ffn_w2, final_norm_w, loss_target, m_norm_mix_w, m_w_in, m_hg_lb_raw, m_hg_norm_w, m_cv_dw_w, m_cv_dw_b, m_cv_ln_w, m_cv_ln_b, m_pl_w, m_pl_scale, m_lru_conv_w, m_lru_conv_b, m_lru_wa, m_lru_ba, m_lru_wx, m_lru_bx, m_lru_lambda, m_gate_b, m_w_branch, m_w_out, m_norm_mem_w, m_mem_norm_w, m_xa_wq, m_xa_wkv, m_xa_wo, m_norm_ffn_w, m_ffn_w1, m_ffn_w2, m_final_norm_w, v_norm_mix_w, v_w_in, v_hg_lb_raw, v_hg_norm_w, v_cv_dw_w, v_cv_dw_b, v_cv_ln_w, v_cv_ln_b, v_pl_w, v_pl_scale, v_lru_conv_w, v_lru_conv_b, v_lru_wa, v_lru_ba, v_lru_wx, v_lru_bx, v_lru_lambda, v_gate_b, v_w_branch, v_w_out, v_norm_mem_w, v_mem_norm_w, v_xa_wq, v_xa_wkv, v_xa_wo, v_norm_ffn_w, v_ffn_w1, v_ffn_w2, v_final_norm_w):
    W = dict(zip(WEIGHTS, (norm_mix_w, w_in, hg_lb_raw, hg_norm_w, cv_dw_w, cv_dw_b, cv_ln_w, cv_ln_b, pl_w, pl_scale, lru_conv_w, lru_conv_b, lru_wa, lru_ba, lru_wx, lru_bx, lru_lambda, gate_b, w_branch, w_out, norm_mem_w, mem_norm_w, xa_wq, xa_wkv, xa_wo, norm_ffn_w, ffn_w1, ffn_w2, final_norm_w)))
    Mo = dict(zip(WEIGHTS, (m_norm_mix_w, m_w_in, m_hg_lb_raw, m_hg_norm_w, m_cv_dw_w, m_cv_dw_b, m_cv_ln_w, m_cv_ln_b, m_pl_w, m_pl_scale, m_lru_conv_w, m_lru_conv_b, m_lru_wa, m_lru_ba, m_lru_wx, m_lru_bx, m_lru_lambda, m_gate_b, m_w_branch, m_w_out, m_norm_mem_w, m_mem_norm_w, m_xa_wq, m_xa_wkv, m_xa_wo, m_norm_ffn_w, m_ffn_w1, m_ffn_w2, m_final_norm_w)))
    Vo = dict(zip(WEIGHTS, (v_norm_mix_w, v_w_in, v_hg_lb_raw, v_hg_norm_w, v_cv_dw_w, v_cv_dw_b, v_cv_ln_w, v_cv_ln_b, v_pl_w, v_pl_scale, v_lru_conv_w, v_lru_conv_b, v_lru_wa, v_lru_ba, v_lru_wx, v_lru_bx, v_lru_lambda, v_gate_b, v_w_branch, v_w_out, v_norm_mem_w, v_mem_norm_w, v_xa_wq, v_xa_wkv, v_xa_wo, v_norm_ffn_w, v_ffn_w1, v_ffn_w2, v_final_norm_w)))
    me = _slot(*_place())
    xs, mems, target = x[0], mem[0], loss_target[0]

    shard_shapes = [W[n].shape for n in SMALL_SHARDED]
    gathered = all_gather("ag_small", [_pack([W[n] for n in SMALL_SHARDED])])[0]
    parts = [jnp.stack(ps) for ps in zip(*[_unpack(gathered[d], shard_shapes) for d in range(N_DEV)])]
    full_small = {n: _gather_last(parts[i], shard_shapes[i]) for i, n in enumerate(SMALL_SHARDED)}
    lb_all, lb_vjp = jax.vjp(_lower_bounds, hg_lb_raw)
    sp8_all, sp8_vjp = jax.vjp(_decay_rates, lru_lambda)

    def layer_params(l):
        p = {n: W[n][l] for n in SMALL if n != 'final_norm_w'}
        p['lb'] = lb_all[l]
        p['sp8'] = sp8_all[l]
        p['cv_w32'] = jnp.pad(full_small['cv_dw_w'][l], ((0, 32 - CV_KERNEL), (0, 0)))
        p['lru_cw8'] = jnp.pad(full_small['lru_conv_w'][l], ((0, 8 - LRU_CONV), (0, 0)))
        p['gate_b'] = full_small['gate_b'][l]
        p['wa_bd'], p['wx_bd'] = _block_diag(lru_wa[l]), _block_diag(lru_wx[l])
        p['lru_ba'], p['lru_bx'] = lru_ba[l].reshape(-1), lru_bx[l].reshape(-1)
        return p

    def gather_layer(l):
        shards = [jnp.transpose(w_in[l]).astype(BF16)] + [w_branch[l, k].astype(BF16) for k in range(N_BRANCH)]
        shards += [w[l].astype(BF16) for w in (w_out, xa_wq, xa_wkv, xa_wo, ffn_w1, ffn_w2)]
        o = all_gather("ag_layer", shards)
        return dict(w_in=o[0].reshape(IN_W, D_MODEL), w_branch=o[1:5], w_out=o[5].reshape(D_MODEL, D_MODEL),
                    xa_wq=o[6].reshape(D_MODEL, D_MODEL), xa_wkv=o[7], xa_wo=o[8].reshape(D_MODEL, D_MODEL),
                    ffn_w1=o[9], ffn_w2=o[10].reshape(D_FF, D_MODEL))

    params = [layer_params(l) for l in range(DEPTH)]
    mats = [gather_layer(l) for l in range(DEPTH)]
    residuals = []
    xc = xs
    for l in range(DEPTH):
        xc, res = _layer_fwd(xc, mems, params[l], mats[l])
        residuals.append(res)
    loss_part, dx, g_final = loss_head(xc, final_norm_w, target)
    loss = lax.psum(loss_part, ("x", "y", "c"))

    small_grads = [None] * DEPTH
    big_grads = [None] * DEPTH
    for l in reversed(range(DEPTH)):
        dx, gs, gb = _layer_bwd(dx, mems, params[l], mats[l], residuals[l])
        small_grads[l] = gs
        order = [gb['w_in'], *gb['w_branch'], gb['w_out'], gb['xa_wq'], gb['xa_wkv'], gb['xa_wo'], gb['ffn_w1'], gb['ffn_w2']]
        landed = exchange_blocks("rs_layer", order)
        big_grads[l] = [sum_parts(t.reshape(N_DEV, -1, t.shape[-1])).reshape(t.shape[1:]) for t in landed]

    G = {}
    G['w_in'] = jnp.stack([jnp.transpose(big_grads[l][0]) for l in range(DEPTH)])
    G['w_branch'] = jnp.stack([jnp.stack(big_grads[l][1:5]) for l in range(DEPTH)])
    for i, n in enumerate(('w_out', 'xa_wq', 'xa_wkv', 'xa_wo', 'ffn_w1', 'ffn_w2')):
        G[n] = jnp.stack([big_grads[l][5 + i] for l in range(DEPTH)])

    def stacked(n):
        return jnp.stack([small_grads[l][n] for l in range(DEPTH)])

    part = {n: stacked(n) for n in SMALL if n not in ('final_norm_w', 'hg_lb_raw', 'lru_lambda')}
    part['final_norm_w'] = g_final
    part['hg_lb_raw'] = lb_vjp(stacked('lb'))[0]
    part['lru_lambda'] = sp8_vjp(stacked('sp8'))[0]
    for n in SMALL_SHARDED:
        part[n] = stacked(n)
    names = list(SMALL) + list(SMALL_SHARDED)
    full_shapes = [part[n].shape for n in names]
    everyone = all_gather("ag_grads", [_pack([part[n] for n in names])])[0]
    total = sum_parts(everyone)
    for n, t in zip(names, _unpack(total, full_shapes)):
        if n in SMALL_SHARDED:
            c = t.shape[-1] // N_DEV
            t = lax.dynamic_slice_in_dim(t, me * c, c, axis=t.ndim - 1)
        G[n] = t

    delta, new_m, new_v = {}, {}, {}
    for n in BIG:
        c = W[n].shape[-1]
        d, nm, nv = adamw(W[n].reshape(-1, c), G[n].reshape(-1, c), Mo[n].reshape(-1, c), Vo[n].reshape(-1, c))
        delta[n], new_m[n], new_v[n] = d.reshape(W[n].shape), nm.reshape(W[n].shape), nv.reshape(W[n].shape)
    shapes = [W[n].shape for n in names]
    d, nm, nv = adamw(_pack([W[n] for n in names]), _pack([G[n] for n in names]), _pack([Mo[n] for n in names]), _pack([Vo[n] for n in names]))
    for n, a, b, c in zip(names, _unpack(d, shapes), _unpack(nm, shapes), _unpack(nv, shapes)):
        delta[n], new_m[n], new_v[n] = a, b, c
    return (loss, dx[None], *[G[n] for n in WEIGHTS], *[delta[n] for n in WEIGHTS],
            *[new_m[n] for n in WEIGHTS], *[new_v[n] for n in WEIGHTS])
```

```python
import functools
import math

import jax
import jax.numpy as jnp
from jax import lax
from jax.experimental import pallas as pl
from jax.experimental.pallas import tpu as pltpu

F32 = jnp.float32
BF16 = jnp.bfloat16
I32 = jnp.int32

N_DEV = 8
D_MODEL = 1024
DEPTH = 4
CHUNK = 64
EPS = 1e-6
HG_HEADS = 4
BRANCH_W = 512
CV_KERNEL = 31
POOL_WINDOWS = (2, 4, 8, 16)
LRU_CONV = 4
LRU_C = 8.0
XA_HEADS = 4
XA_HD = D_MODEL // XA_HEADS
D_FF = 4 * D_MODEL
IN_W = 8704
OFF_Q, OFF_F, OFF_V, OFF_G, OFF_CV, OFF_PL, OFF_LX, OFF_LY, OFF_GATE = 0, 512, 1024, 1536, 2048, 3072, 3584, 4096, 4608
LANE = 128
ADAM_LR, ADAM_B1, ADAM_B2, ADAM_EPS, ADAM_WD, ADAM_STEP = 0.001, 0.9, 0.999, 1e-08, 0.01, 10
VMEM_LIMIT = 56 * 1024 * 1024
MESH = pl.DeviceIdType.MESH
NEG = -1e30
ANY_SPACE = pl.BlockSpec(memory_space=pl.ANY)


def _cp(sem, **kw):
    return pltpu.CompilerParams(dimension_semantics=sem, vmem_limit_bytes=VMEM_LIMIT, **kw)


def _sigmoid(x):
    return 1.0 / (1.0 + jnp.exp(-x))


def _dsilu(x, s):
    return s * (1.0 + x * (1.0 - s))


def _dot(a, b, cdims, precision=None):
    return lax.dot_general(a, b, (cdims, ((), ())), preferred_element_type=F32, precision=precision)


NN = ((1,), (0,))
NT = ((1,), (1,))
TN = ((0,), (0,))


def _mm(name, a, b, *, grid, a_spec, b_spec, o_specs, out_shapes, acc_shape, cdims, epi=None, extras=(), extra_specs=(), after=()):
    nk = grid[2]
    n_e, n_o = len(extras), len(out_shapes)
    extras = (*extras, *after)
    extra_specs = (*extra_specs, *[ANY_SPACE] * len(after))

    def body(*refs):
        a_ref, b_ref = refs[0], refs[1]
        e_refs = refs[2:2 + n_e]
        o_refs = refs[2 + len(extras):2 + len(extras) + n_o]

        def finish(acc):
            vals = epi(acc, *[r[...] for r in e_refs]) if epi is not None else (acc,)
            for r, v in zip(o_refs, vals):
                r[...] = v.astype(r.dtype)

        part = _dot(a_ref[...].astype(BF16), b_ref[...].astype(BF16), cdims)
        if nk == 1:
            finish(part)
        else:
            acc_ref = refs[-1]
            k = pl.program_id(2)

            @pl.when(k == 0)
            def _():
                acc_ref[...] = part

            @pl.when(k > 0)
            def _():
                acc_ref[...] += part

            @pl.when(k == nk - 1)
            def _():
                finish(acc_ref[...])

    return pl.pallas_call(
        body, name=name, grid=grid,
        in_specs=[a_spec, b_spec, *extra_specs], out_specs=list(o_specs), out_shape=list(out_shapes),
        scratch_shapes=[] if nk == 1 else [pltpu.VMEM(acc_shape, F32)],
        compiler_params=_cp(("parallel", "parallel", "arbitrary")),
    )(a, b, *extras)


def _tile(n, pref):
    t = min(n, pref)
    while n % t:
        t //= 2
    return t


def mm_nt(name, a, b, out_dtype=F32, epi=None, extras=(), n_out=1, out_dtypes=None, tm=1024, tn=512, tk=1024, after=()):
    M, K = a.shape
    N = b.shape[0]
    tm, tn, tk = _tile(M, tm), _tile(N, tn), _tile(K, tk)
    odt = out_dtypes or [out_dtype] * n_out
    o_spec = pl.BlockSpec((tm, tn), lambda i, j, k: (i, j))
    return _mm(name, a, b, grid=(M // tm, N // tn, K // tk),
               a_spec=pl.BlockSpec((tm, tk), lambda i, j, k: (i, k)),
               b_spec=pl.BlockSpec((tn, tk), lambda i, j, k: (j, k)),
               o_specs=[o_spec] * len(odt), out_shapes=[jax.ShapeDtypeStruct((M, N), d) for d in odt],
               acc_shape=(tm, tn), cdims=NT, epi=epi, extras=extras, extra_specs=[o_spec] * len(extras), after=after)


def mm_nn(name, a, b, out_dtype=F32, epi=None, extras=(), n_out=1, out_dtypes=None, tm=1024, tn=512, tk=1024):
    M, K = a.shape
    N = b.shape[1]
    tm, tn, tk = _tile(M, tm), _tile(N, tn), _tile(K, tk)
    odt = out_dtypes or [out_dtype] * n_out
    o_spec = pl.BlockSpec((tm, tn), lambda i, j, k: (i, j))
    return _mm(name, a, b, grid=(M // tm, N // tn, K // tk),
               a_spec=pl.BlockSpec((tm, tk), lambda i, j, k: (i, k)),
               b_spec=pl.BlockSpec((tk, tn), lambda i, j, k: (k, j)),
               o_specs=[o_spec] * len(odt), out_shapes=[jax.ShapeDtypeStruct((M, N), d) for d in odt],
               acc_shape=(tm, tn), cdims=NN, epi=epi, extras=extras, extra_specs=[o_spec] * len(extras))


def mm_tn(name, a, b, out_dtype=BF16, tm=512, tn=512, tk=1024):
    K, M = a.shape
    N = b.shape[1]
    tm, tn, tk = _tile(M, tm), _tile(N, tn), _tile(K, tk)
    return _mm(name, a, b, grid=(M // tm, N // tn, K // tk),
               a_spec=pl.BlockSpec((tk, tm), lambda i, j, k: (k, i)),
               b_spec=pl.BlockSpec((tk, tn), lambda i, j, k: (k, j)),
               o_specs=[pl.BlockSpec((tm, tn), lambda i, j, k: (i, j))],
               out_shapes=[jax.ShapeDtypeStruct((M, N), out_dtype)], acc_shape=(tm, tn), cdims=TN)[0]


def mm_nn_cb(name, a, b, out_dtype=F32, epi=None, out_dtypes=None, tm=1024):
    M, K = a.shape
    nb, _, c = b.shape
    tm = _tile(M, tm)
    odt = out_dtypes or [out_dtype]
    return _mm(name, a, b, grid=(M // tm, nb, 1),
               a_spec=pl.BlockSpec((tm, K), lambda i, j, k: (i, 0)),
               b_spec=pl.BlockSpec((None, K, c), lambda i, j, k: (j, 0, 0)),
               o_specs=[pl.BlockSpec((tm, c), lambda i, j, k: (i, j))] * len(odt),
               out_shapes=[jax.ShapeDtypeStruct((M, nb * c), d) for d in odt], acc_shape=(tm, c), cdims=NN, epi=epi)


def mm_nt_cb(name, a, b, out_dtype=F32, epi=None, extras=(), tm=1024, tn=512):
    M = a.shape[0]
    nb, K, c = b.shape
    tm, tn = _tile(M, tm), _tile(K, tn)
    o_spec = pl.BlockSpec((tm, tn), lambda i, j, k: (i, j))
    return _mm(name, a, b, grid=(M // tm, K // tn, nb),
               a_spec=pl.BlockSpec((tm, c), lambda i, j, k: (i, k)),
               b_spec=pl.BlockSpec((None, tn, c), lambda i, j, k: (k, j, 0)),
               o_specs=[o_spec], out_shapes=[jax.ShapeDtypeStruct((M, K), out_dtype)],
               acc_shape=(tm, tn), cdims=NT, epi=epi, extras=extras, extra_specs=[o_spec] * len(extras))


def mm_tn_cb(name, a, b, nb, out_dtype=BF16, tm=512, tk=1024):
    K, M = a.shape
    N = b.shape[1]
    c = N // nb
    tm, tk = _tile(M, tm), _tile(K, tk)
    return _mm(name, a, b, grid=(M // tm, nb, K // tk),
               a_spec=pl.BlockSpec((tk, tm), lambda i, j, k: (k, i)),
               b_spec=pl.BlockSpec((tk, c), lambda i, j, k: (k, j)),
               o_specs=[pl.BlockSpec((None, tm, c), lambda i, j, k: (j, i, 0))],
               out_shapes=[jax.ShapeDtypeStruct((nb, M, c), out_dtype)], acc_shape=(tm, c), cdims=TN)[0]


def rms_fwd(name, x, w, out_dtype=BF16, tm=512, after=()):
    S, D = x.shape
    tm = _tile(S, tm)

    def body(x_ref, w_ref, *rest):
        o_ref = rest[-1]
        xv = x_ref[...]
        r = lax.rsqrt(jnp.mean(xv * xv, axis=-1, keepdims=True) + EPS)
        o_ref[...] = (xv * r * w_ref[...]).astype(o_ref.dtype)

    return pl.pallas_call(
        body, name=name, grid=(S // tm,),
        in_specs=[pl.BlockSpec((tm, D), lambda i: (i, 0)), pl.BlockSpec((1, D), lambda i: (0, 0))] + [ANY_SPACE] * len(after),
        out_specs=pl.BlockSpec((tm, D), lambda i: (i, 0)), out_shape=jax.ShapeDtypeStruct((S, D), out_dtype),
        compiler_params=_cp(("parallel",)),
    )(x, w.reshape(1, D), *after)


def rms_bwd(name, x, w, dh, dres=None, tm=512):
    S, D = x.shape
    tm = _tile(S, tm)
    has_res = dres is not None

    def body(*refs):
        if has_res:
            x_ref, w_ref, dh_ref, dres_ref, dx_ref, dw_ref = refs
        else:
            x_ref, w_ref, dh_ref, dx_ref, dw_ref = refs
        xv = x_ref[...]
        dhv = dh_ref[...].astype(F32)
        r = lax.rsqrt(jnp.mean(xv * xv, axis=-1, keepdims=True) + EPS)
        g = dhv * w_ref[...]
        dx = r * g - xv * (r * r * r) * jnp.mean(xv * g, axis=-1, keepdims=True)
        if has_res:
            dx = dx + dres_ref[...]
        dx_ref[...] = dx

        @pl.when(pl.program_id(0) == 0)
        def _():
            dw_ref[...] = jnp.zeros_like(dw_ref)

        dw_ref[...] += jnp.sum(dhv * xv * r, axis=0, keepdims=True)

    row = pl.BlockSpec((tm, D), lambda i: (i, 0))
    vec = pl.BlockSpec((1, D), lambda i: (0, 0))
    args = [x, w.reshape(1, D), dh] + ([dres] if has_res else [])
    dx, dw = pl.pallas_call(
        body, name=name, grid=(S // tm,),
        in_specs=[row, vec, row] + ([row] if has_res else []),
        out_specs=[row, vec], out_shape=[jax.ShapeDtypeStruct((S, D), F32), jax.ShapeDtypeStruct((1, D), F32)],
        compiler_params=_cp(("arbitrary",)),
    )(*args)
    return dx, dw.reshape(D)


def loss_head(x, w, target, tm=512):
    S, D = x.shape
    tm = _tile(S, tm)

    def body(x_ref, w_ref, t_ref, loss_ref, dx_ref, dw_ref):
        xv = x_ref[...]
        wv = w_ref[...]
        r = lax.rsqrt(jnp.mean(xv * xv, axis=-1, keepdims=True) + EPS)
        y = xv * r * wv
        err = y - t_ref[...]
        dy = err * (1.0 / D)
        g = dy * wv
        dx_ref[...] = r * g - xv * (r * r * r) * jnp.mean(xv * g, axis=-1, keepdims=True)

        @pl.when(pl.program_id(0) == 0)
        def _():
            dw_ref[...] = jnp.zeros_like(dw_ref)
            loss_ref[...] = jnp.zeros_like(loss_ref)

        dw_ref[...] += jnp.sum(dy * xv * r, axis=0, keepdims=True)
        part = 0.5 * jnp.sum(jnp.mean(err * err, axis=-1, keepdims=True), axis=0, keepdims=True)
        loss_ref[...] += jnp.broadcast_to(part, loss_ref.shape)

    row = pl.BlockSpec((tm, D), lambda i: (i, 0))
    vec = pl.BlockSpec((1, D), lambda i: (0, 0))
    loss, dx, dw = pl.pallas_call(
        body, name="loss_head", grid=(S // tm,),
        in_specs=[row, vec, row],
        out_specs=[pl.BlockSpec((1, LANE), lambda i: (0, 0)), row, vec],
        out_shape=[jax.ShapeDtypeStruct((1, LANE), F32), jax.ShapeDtypeStruct((S, D), F32), jax.ShapeDtypeStruct((1, D), F32)],
        compiler_params=_cp(("arbitrary",)),
    )(x, w.reshape(1, D), target)
    return loss[0, 0], dx, dw.reshape(D)


SUB = 16
HG_W = HG_HEADS * LANE
HG_PAD = CHUNK + 2 * SUB


def _hg_gates(q, f, lbv):
    sig = _sigmoid(f)
    fg = lbv + (1.0 - lbv) * sig
    sq = _sigmoid(q)
    return sig, fg, 1.0 - fg, sq, q * sq


def _hg_cumsum(logf):
    ri = lax.broadcasted_iota(I32, (CHUNK, CHUNK), 0)
    ci = lax.broadcasted_iota(I32, (CHUNK, CHUNK), 1)
    return _dot((ci <= ri).astype(F32), logf, NN, precision=lax.Precision.HIGHEST)


def _hg_rows():
    return lax.broadcasted_iota(I32, (CHUNK, LANE), 0)


def _hg_below(qf, kk, b, rows):
    blocks, parts = [jnp.zeros((SUB, CHUNK), F32)], []
    for i in range(1, CHUNK // SUB):
        bref = b[SUB * i - 1:SUB * i, :]
        rs = slice(SUB * i, SUB * (i + 1))
        eq = jnp.exp(b[rs] - bref)
        below = rows < SUB * i
        ek = jnp.exp(jnp.where(below, bref - b, NEG))
        qi = (qf[rs] * eq).astype(BF16)
        ki = (kk * ek).astype(BF16)
        blocks.append(_dot(qi, ki, NT))
        parts.append((qi, ki, eq, ek))
    return jnp.concatenate(blocks, axis=0), parts


def hgrn_fwd(proj, lb, nw):
    S = proj.shape[0]
    NC = S // CHUNK
    H = HG_HEADS

    def body(q_ref, f_ref, v_ref, g_ref, lb_ref, nw_ref, out_ref, st_out_ref, o_ref, st, kk_p, b_p, v_p):
        c = pl.program_id(0)

        @pl.when(c == 0)
        def _():
            st[...] = jnp.zeros_like(st)
            for p in (kk_p, b_p, v_p):
                p[...] = jnp.zeros_like(p)

        st_out_ref[...] = st[...]
        sig, fg, kk_all, sq, qf_all = _hg_gates(q_ref[...], f_ref[...], lb_ref[...])
        b_all = _hg_cumsum(jnp.log(fg))
        kk_p[pl.ds(SUB, CHUNK), :] = kk_all
        b_p[pl.ds(SUB, CHUNK), :] = b_all
        v_p[pl.ds(SUB, CHUNK), :] = v_ref[...]
        rows = _hg_rows()
        sub = rows & (SUB - 1)
        for h in range(H):
            cs = slice(h * LANE, (h + 1) * LANE)
            qf, kk, b, v, g = qf_all[:, cs], kk_all[:, cs], b_all[:, cs], v_ref[:, cs], g_ref[:, cs]
            st_in = st[h]
            o = jnp.zeros((CHUNK, LANE), F32)
            for tau in range(SUB):
                sh = pl.ds(SUB - tau, CHUNK)
                e = jnp.exp(jnp.where(sub >= tau, b - b_p[sh, cs], NEG))
                col = jnp.sum(qf * kk_p[sh, cs] * e, axis=1, keepdims=True)
                o = o + col * v_p[sh, cs]
            poff, _ = _hg_below(qf, kk, b, rows)
            vb = v.astype(BF16)
            bl = b[CHUNK - 1:CHUNK, :]
            o = o + _dot(poff.astype(BF16), vb, NN) + _dot((qf * jnp.exp(b)).astype(BF16), st_in.astype(BF16), NT)
            st[h] = st_in * jnp.exp(bl) + _dot(vb, (kk * jnp.exp(bl - b)).astype(BF16), TN)
            o_ref[:, cs] = o
            r = lax.rsqrt(jnp.mean(o * o, axis=-1, keepdims=True) + EPS)
            out_ref[:, cs] = (o * r * nw_ref[...] * (g * _sigmoid(g))).astype(out_ref.dtype)

    def seg(off):
        return pl.BlockSpec((CHUNK, HG_W), lambda c: (c, off // HG_W))

    blk = pl.BlockSpec((CHUNK, HG_W), lambda c: (c, 0))
    pad = pltpu.VMEM((HG_PAD, HG_W), F32)
    return pl.pallas_call(
        body, name="hgrn_fwd", grid=(NC,),
        in_specs=[seg(OFF_Q), seg(OFF_F), seg(OFF_V), seg(OFF_G),
                  pl.BlockSpec((1, HG_W), lambda c: (0, 0)), pl.BlockSpec((1, LANE), lambda c: (0, 0))],
        out_specs=[blk, pl.BlockSpec((None, H, LANE, LANE), lambda c: (c, 0, 0, 0)), blk],
        out_shape=[jax.ShapeDtypeStruct((S, HG_W), BF16), jax.ShapeDtypeStruct((NC, H, LANE, LANE), F32),
                   jax.ShapeDtypeStruct((S, HG_W), F32)],
        scratch_shapes=[pltpu.VMEM((H, LANE, LANE), F32), pad, pad, pad],
        compiler_params=_cp(("arbitrary",)),
    )(proj, proj, proj, proj, lb.reshape(1, HG_W), nw.reshape(1, LANE))


def hgrn_bwd(proj, lb, nw, states, o_pre, dout):
    S = proj.shape[0]
    NC = S // CHUNK
    H = HG_HEADS

    def body(q_ref, f_ref, v_ref, g_ref, lb_ref, nw_ref, st_ref, o_ref, do_ref,
             dq_ref, df_ref, dv_ref, dg_ref, dlb_ref, dnw_ref, dst, kk_p, b_p, v_p, qf_p, do_p, db_s, dkk_s):
        c = pl.program_id(0)

        @pl.when(c == 0)
        def _():
            dst[...] = jnp.zeros_like(dst)
            dlb_ref[...] = jnp.zeros_like(dlb_ref)
            dnw_ref[...] = jnp.zeros_like(dnw_ref)
            for p in (kk_p, b_p, v_p, qf_p, do_p):
                p[...] = jnp.zeros_like(p)

        q_all, g_all = q_ref[...], g_ref[...]
        lbv, nwv = lb_ref[...], nw_ref[...]
        sig, fg, kk_all, sq, qf_all = _hg_gates(q_all, f_ref[...], lbv)
        b_all = _hg_cumsum(jnp.log(fg))
        o_all = o_ref[...]
        dov = do_ref[...].astype(F32)
        sg = _sigmoid(g_all)
        gsg = g_all * sg
        dnw_acc = jnp.zeros((1, LANE), F32)
        for h in range(H):
            cs = slice(h * LANE, (h + 1) * LANE)
            o = o_all[:, cs]
            r = lax.rsqrt(jnp.mean(o * o, axis=-1, keepdims=True) + EPS)
            don = dov[:, cs] * gsg[:, cs]
            dnw_acc = dnw_acc + jnp.sum(don * o * r, axis=0, keepdims=True)
            gno = don * nwv
            do_p[pl.ds(SUB, CHUNK), cs] = r * gno - o * (r * r * r) * jnp.mean(o * gno, axis=-1, keepdims=True)
            dg_ref[:, cs] = (dov[:, cs] * (o * r * nwv) * _dsilu(g_all[:, cs], sg[:, cs])).astype(dg_ref.dtype)
        dnw_ref[...] += jnp.broadcast_to(dnw_acc, dnw_ref.shape)
        kk_p[pl.ds(SUB, CHUNK), :] = kk_all
        b_p[pl.ds(SUB, CHUNK), :] = b_all
        v_p[pl.ds(SUB, CHUNK), :] = v_ref[...]
        qf_p[pl.ds(SUB, CHUNK), :] = qf_all
        rows = _hg_rows()
        sub = rows & (SUB - 1)
        for h in range(H):
            cs = slice(h * LANE, (h + 1) * LANE)
            qf, kk, b, v = qf_all[:, cs], kk_all[:, cs], b_all[:, cs], v_ref[:, cs]
            do = do_p[pl.ds(SUB, CHUNK), cs]
            st_in, dstv = st_ref[h], dst[h]
            bl = b[CHUNK - 1:CHUNK, :]
            eb, ebl, el = jnp.exp(b), jnp.exp(bl - b), jnp.exp(bl)
            qe, ke = qf * eb, kk * ebl
            vb, dob, stb, dstb = v.astype(BF16), do.astype(BF16), st_in.astype(BF16), dstv.astype(BF16)
            w_ = _dot(vb, dstb, NN)
            dqf = eb * _dot(dob, stb, NN)
            dkk = ebl * w_
            dv = _dot(ke.astype(BF16), dstb, NT)
            dbl = el * jnp.sum(st_in * dstv, axis=0, keepdims=True) + jnp.sum(ke * w_, axis=0, keepdims=True)
            dst[h] = dstv * el + _dot(dob, qe.astype(BF16), TN)
            poff, parts = _hg_below(qf, kk, b, rows)
            dpoff = _dot(dob, vb, NT).astype(BF16)
            dv = dv + _dot(poff.astype(BF16), dob, TN)
            dq_blocks = [jnp.zeros((SUB, LANE), F32)]
            for i, (qi, ki, eq, ek) in enumerate(parts, start=1):
                dpi = dpoff[SUB * i:SUB * (i + 1), :]
                dq_blocks.append(_dot(dpi, ki, NN) * eq)
                dkk = dkk + _dot(dpi, qi, TN) * ek
            dqf = dqf + jnp.concatenate(dq_blocks, axis=0)
            for tau in range(SUB):
                sh = pl.ds(SUB - tau, CHUNK)
                kd = kk_p[sh, cs]
                e = jnp.exp(jnp.where(sub >= tau, b - b_p[sh, cs], NEG))
                dcol = jnp.sum(do * v_p[sh, cs], axis=1, keepdims=True)
                dqf = dqf + dcol * kd * e
            for tau in range(SUB):
                sh = pl.ds(SUB + tau, CHUNK)
                qu, dou = qf_p[sh, cs], do_p[sh, cs]
                e = jnp.exp(jnp.where(sub + tau < SUB, b_p[sh, cs] - b, NEG))
                qe_ = qu * e
                dcol = jnp.sum(dou * v, axis=1, keepdims=True)
                col = jnp.sum(qe_ * kk, axis=1, keepdims=True)
                dkk = dkk + dcol * qe_
                dv = dv + col * dou
            dv_ref[:, cs] = dv.astype(dv_ref.dtype)
            db = qf * dqf - kk * dkk
            db_s[:, cs] = db + jnp.where(rows == CHUNK - 1, dbl, 0.0)
            dkk_s[:, cs] = dkk
            dq_ref[:, cs] = (dqf * _dsilu(q_all[:, cs], sq[:, cs])).astype(dq_ref.dtype)
        ri = lax.broadcasted_iota(I32, (CHUNK, CHUNK), 0)
        ci = lax.broadcasted_iota(I32, (CHUNK, CHUNK), 1)
        dlogf = _dot((ci >= ri).astype(F32), db_s[...], NN, precision=lax.Precision.HIGHEST)
        dfg = dlogf / fg - dkk_s[...]
        df_ref[...] = (dfg * (1.0 - lbv) * sig * (1.0 - sig)).astype(df_ref.dtype)
        dlb_ref[...] += jnp.broadcast_to(jnp.sum(dfg * (1.0 - sig), axis=0, keepdims=True), dlb_ref.shape)

    def seg(off):
        return pl.BlockSpec((CHUNK, HG_W), lambda c: (NC - 1 - c, off // HG_W))

    blk = pl.BlockSpec((CHUNK, HG_W), lambda c: (NC - 1 - c, 0))
    osd = jax.ShapeDtypeStruct((S, HG_W), BF16)
    pad = pltpu.VMEM((HG_PAD, HG_W), F32)
    full = pltpu.VMEM((CHUNK, HG_W), F32)
    dq, df, dv, dg, dlb, dnw = pl.pallas_call(
        body, name="hgrn_bwd", grid=(NC,),
        in_specs=[seg(OFF_Q), seg(OFF_F), seg(OFF_V), seg(OFF_G),
                  pl.BlockSpec((1, HG_W), lambda c: (0, 0)), pl.BlockSpec((1, LANE), lambda c: (0, 0)),
                  pl.BlockSpec((None, H, LANE, LANE), lambda c: (NC - 1 - c, 0, 0, 0)), blk, blk],
        out_specs=[blk, blk, blk, blk, pl.BlockSpec((8, HG_W), lambda c: (0, 0)), pl.BlockSpec((8, LANE), lambda c: (0, 0))],
        out_shape=[osd, osd, osd, osd, jax.ShapeDtypeStruct((8, HG_W), F32), jax.ShapeDtypeStruct((8, LANE), F32)],
        scratch_shapes=[pltpu.VMEM((H, LANE, LANE), F32), pad, pad, pad, pad, pad, full, full],
        compiler_params=_cp(("arbitrary",)),
    )(proj, proj, proj, proj, lb.reshape(1, HG_W), nw.reshape(1, LANE), states, o_pre, dout)
    return dq, df, dv, dg, dlb[0], dnw[0]


CV_PAD = 32
ROWS = 256


def _colblk(S, off):
    return pl.BlockSpec((S, LANE), lambda j: (0, off // LANE + j))


def cv_fwd(proj, w32, bias):
    S = proj.shape[0]
    nchunk = S // ROWS

    def body(a_ref, g_ref, w_ref, b_ref, o_ref, zpad):
        zpad[pl.ds(0, CV_PAD), :] = jnp.zeros((CV_PAD, LANE), F32)

        def glu(c, _):
            r0 = pl.multiple_of(c * ROWS, ROWS)
            zpad[pl.ds(CV_PAD + r0, ROWS), :] = a_ref[pl.ds(r0, ROWS), :] * _sigmoid(g_ref[pl.ds(r0, ROWS), :])
            return 0

        lax.fori_loop(0, nchunk, glu, 0)

        def conv(c, _):
            r0 = pl.multiple_of(c * ROWS, ROWS)
            acc = jnp.broadcast_to(b_ref[...], (ROWS, LANE))
            for j in range(CV_KERNEL):
                acc = acc + w_ref[pl.ds(j, 1), :] * zpad[pl.ds(r0 + (CV_PAD - CV_KERNEL + 1) + j, ROWS), :]
            o_ref[pl.ds(r0, ROWS), :] = acc
            return 0

        lax.fori_loop(0, nchunk, conv, 0)

    return pl.pallas_call(
        body, name="cv_fwd", grid=(BRANCH_W // LANE,),
        in_specs=[_colblk(S, OFF_CV), _colblk(S, OFF_CV + BRANCH_W),
                  pl.BlockSpec((32, LANE), lambda j: (0, j)), pl.BlockSpec((1, LANE), lambda j: (0, j))],
        out_specs=pl.BlockSpec((S, LANE), lambda j: (0, j)), out_shape=jax.ShapeDtypeStruct((S, BRANCH_W), F32),
        scratch_shapes=[pltpu.VMEM((CV_PAD + S, LANE), F32)],
        compiler_params=_cp(("parallel",)),
    )(proj, proj, w32, bias.reshape(1, BRANCH_W))


def cv_bwd(proj, w32, dzc):
    S = proj.shape[0]
    nchunk = S // ROWS

    def body(a_ref, g_ref, w_ref, dz_ref, da_ref, dg_ref, dw_ref, db_ref, zpad, dpad):
        zpad[pl.ds(0, CV_PAD), :] = jnp.zeros((CV_PAD, LANE), F32)
        dpad[pl.ds(S, CV_PAD), :] = jnp.zeros((CV_PAD, LANE), F32)
        dw_ref[...] = jnp.zeros_like(dw_ref)

        def glu(c, dsum):
            r0 = pl.multiple_of(c * ROWS, ROWS)
            zpad[pl.ds(CV_PAD + r0, ROWS), :] = a_ref[pl.ds(r0, ROWS), :] * _sigmoid(g_ref[pl.ds(r0, ROWS), :])
            d = dz_ref[pl.ds(r0, ROWS), :]
            dpad[pl.ds(r0, ROWS), :] = d
            return dsum + jnp.sum(d, axis=0, keepdims=True)

        dsum = lax.fori_loop(0, nchunk, glu, jnp.zeros((1, LANE), F32))
        db_ref[...] = jnp.broadcast_to(dsum, db_ref.shape)

        def conv(c, _):
            r0 = pl.multiple_of(c * ROWS, ROWS)
            d = dpad[pl.ds(r0, ROWS), :]
            acc = jnp.zeros((ROWS, LANE), F32)
            for j in range(CV_KERNEL):
                acc = acc + w_ref[pl.ds(j, 1), :] * dpad[pl.ds(r0 + (CV_KERNEL - 1) - j, ROWS), :]
                zs = zpad[pl.ds(r0 + (CV_PAD - CV_KERNEL + 1) + j, ROWS), :]
                dw_ref[pl.ds(j, 1), :] += jnp.sum(d * zs, axis=0, keepdims=True)
            a = a_ref[pl.ds(r0, ROWS), :]
            sg = _sigmoid(g_ref[pl.ds(r0, ROWS), :])
            da_ref[pl.ds(r0, ROWS), :] = (acc * sg).astype(da_ref.dtype)
            dg_ref[pl.ds(r0, ROWS), :] = (acc * a * sg * (1.0 - sg)).astype(dg_ref.dtype)
            return 0

        lax.fori_loop(0, nchunk, conv, 0)

    blk = pl.BlockSpec((S, LANE), lambda j: (0, j))
    da, dg, dw, db = pl.pallas_call(
        body, name="cv_bwd", grid=(BRANCH_W // LANE,),
        in_specs=[_colblk(S, OFF_CV), _colblk(S, OFF_CV + BRANCH_W), pl.BlockSpec((32, LANE), lambda j: (0, j)), blk],
        out_specs=[blk, blk, pl.BlockSpec((32, LANE), lambda j: (0, j)), pl.BlockSpec((8, LANE), lambda j: (0, j))],
        out_shape=[jax.ShapeDtypeStruct((S, BRANCH_W), BF16), jax.ShapeDtypeStruct((S, BRANCH_W), BF16),
                   jax.ShapeDtypeStruct((32, BRANCH_W), F32), jax.ShapeDtypeStruct((8, BRANCH_W), F32)],
        scratch_shapes=[pltpu.VMEM((CV_PAD + S, LANE), F32), pltpu.VMEM((S + CV_PAD, LANE), F32)],
        compiler_params=_cp(("parallel",)),
    )(proj, proj, w32, dzc)
    return da, dg, dw, db[0]


def ln_silu_fwd(z, w, b, tm=512):
    S, C = z.shape
    tm = _tile(S, tm)

    def body(z_ref, w_ref, b_ref, o_ref):
        zv = z_ref[...]
        mu = jnp.mean(zv, axis=-1, keepdims=True)
        zc = zv - mu
        rstd = lax.rsqrt(jnp.mean(zc * zc, axis=-1, keepdims=True) + EPS)
        y = zc * rstd * w_ref[...] + b_ref[...]
        o_ref[...] = (y * _sigmoid(y)).astype(o_ref.dtype)

    row = pl.BlockSpec((tm, C), lambda i: (i, 0))
    vec = pl.BlockSpec((1, C), lambda i: (0, 0))
    return pl.pallas_call(
        body, name="ln_silu_fwd", grid=(S // tm,), in_specs=[row, vec, vec], out_specs=row,
        out_shape=jax.ShapeDtypeStruct((S, C), BF16), compiler_params=_cp(("parallel",)),
    )(z, w.reshape(1, C), b.reshape(1, C))


def ln_silu_bwd(z, w, b, dout, tm=512):
    S, C = z.shape
    tm = _tile(S, tm)

    def body(z_ref, w_ref, b_ref, do_ref, dz_ref, dw_ref, db_ref):
        zv = z_ref[...]
        wv = w_ref[...]
        mu = jnp.mean(zv, axis=-1, keepdims=True)
        zc = zv - mu
        rstd = lax.rsqrt(jnp.mean(zc * zc, axis=-1, keepdims=True) + EPS)
        xh = zc * rstd
        y = xh * wv + b_ref[...]
        dy = do_ref[...].astype(F32) * _dsilu(y, _sigmoid(y))

        @pl.when(pl.program_id(0) == 0)
        def _():
            dw_ref[...] = jnp.zeros_like(dw_ref)
            db_ref[...] = jnp.zeros_like(db_ref)

        dw_ref[...] += jnp.sum(dy * xh, axis=0, keepdims=True)
        db_ref[...] += jnp.sum(dy, axis=0, keepdims=True)
        dxh = dy * wv
        dz_ref[...] = rstd * (dxh - jnp.mean(dxh, axis=-1, keepdims=True) - xh * jnp.mean(dxh * xh, axis=-1, keepdims=True))

    row = pl.BlockSpec((tm, C), lambda i: (i, 0))
    vec = pl.BlockSpec((1, C), lambda i: (0, 0))
    dz, dw, db = pl.pallas_call(
        body, name="ln_silu_bwd", grid=(S // tm,), in_specs=[row, vec, vec, row], out_specs=[row, vec, vec],
        out_shape=[jax.ShapeDtypeStruct((S, C), F32), jax.ShapeDtypeStruct((1, C), F32), jax.ShapeDtypeStruct((1, C), F32)],
        compiler_params=_cp(("arbitrary",)),
    )(z, w.reshape(1, C), b.reshape(1, C), dout)
    return dz, dw.reshape(C), db.reshape(C)


PL_PAD = 16


def _pool_counts(r0, win):
    t = r0 + lax.broadcasted_iota(I32, (ROWS, LANE), 0)
    return jnp.minimum(t + 1, win).astype(F32)


def pool_fwd(proj, wg, scale):
    S = proj.shape[0]
    nchunk = S // ROWS

    def body(u_ref, w_ref, s_ref, o_ref, upad):
        g = pl.program_id(0)
        upad[pl.ds(0, PL_PAD), :] = jnp.zeros((PL_PAD, LANE), F32)

        def fill(c, _):
            r0 = pl.multiple_of(c * ROWS, ROWS)
            upad[pl.ds(PL_PAD + r0, ROWS), :] = u_ref[pl.ds(r0, ROWS), :]
            return 0

        lax.fori_loop(0, nchunk, fill, 0)
        wb = w_ref[...].astype(BF16)
        for gi, win in enumerate(POOL_WINDOWS):
            @pl.when(g == gi)
            def _(win=win):
                def chunk(c, _):
                    r0 = pl.multiple_of(c * ROWS, ROWS)
                    u = upad[pl.ds(PL_PAD + r0, ROWS), :]
                    ws = u
                    for j in range(1, win):
                        ws = ws + upad[pl.ds(PL_PAD + r0 - j, ROWS), :]
                    pooled = ws / _pool_counts(r0, win) - u
                    o_ref[pl.ds(r0, ROWS), :] = (_dot(pooled.astype(BF16), wb, NN) * s_ref[...]).astype(o_ref.dtype)
                    return 0

                lax.fori_loop(0, nchunk, chunk, 0)

    return pl.pallas_call(
        body, name="pool_fwd", grid=(len(POOL_WINDOWS),),
        in_specs=[_colblk(S, OFF_PL), pl.BlockSpec((None, LANE, LANE), lambda j: (j, 0, 0)), pl.BlockSpec((1, LANE), lambda j: (0, j))],
        out_specs=pl.BlockSpec((S, LANE), lambda j: (0, j)), out_shape=jax.ShapeDtypeStruct((S, BRANCH_W), BF16),
        scratch_shapes=[pltpu.VMEM((PL_PAD + S, LANE), F32)],
        compiler_params=_cp(("parallel",)),
    )(proj, wg, scale.reshape(1, BRANCH_W))


def pool_bwd(proj, wg, scale, dy):
    S = proj.shape[0]
    nchunk = S // ROWS

    def body(u_ref, w_ref, s_ref, dy_ref, du_ref, dw_ref, ds_ref, upad, dpn, nd):
        g = pl.program_id(0)
        upad[pl.ds(0, PL_PAD), :] = jnp.zeros((PL_PAD, LANE), F32)
        dpn[pl.ds(S, PL_PAD), :] = jnp.zeros((PL_PAD, LANE), F32)

        def fill(c, _):
            r0 = pl.multiple_of(c * ROWS, ROWS)
            upad[pl.ds(PL_PAD + r0, ROWS), :] = u_ref[pl.ds(r0, ROWS), :]
            return 0

        lax.fori_loop(0, nchunk, fill, 0)
        wb = w_ref[...].astype(BF16)
        sv = s_ref[...]
        for gi, win in enumerate(POOL_WINDOWS):
            @pl.when(g == gi)
            def _(win=win):
                def chunk(c, carry):
                    dw, dsc = carry
                    r0 = pl.multiple_of(c * ROWS, ROWS)
                    u = upad[pl.ds(PL_PAD + r0, ROWS), :]
                    ws = u
                    for j in range(1, win):
                        ws = ws + upad[pl.ds(PL_PAD + r0 - j, ROWS), :]
                    cnt = _pool_counts(r0, win)
                    pooled = (ws / cnt - u).astype(BF16)
                    dyv = dy_ref[pl.ds(r0, ROWS), :].astype(F32)
                    dsc = dsc + jnp.sum(dyv * _dot(pooled, wb, NN), axis=0, keepdims=True)
                    dys = (dyv * sv).astype(BF16)
                    dw = dw + _dot(pooled, dys, TN)
                    dp = _dot(dys, wb, NT)
                    dpn[pl.ds(r0, ROWS), :] = dp / cnt
                    nd[pl.ds(r0, ROWS), :] = -dp
                    return dw, dsc

                dw, dsc = lax.fori_loop(0, nchunk, chunk, (jnp.zeros((LANE, LANE), F32), jnp.zeros((1, LANE), F32)))
                dw_ref[...] = dw
                ds_ref[...] = jnp.broadcast_to(dsc, ds_ref.shape)

                def spread(c, _):
                    r0 = pl.multiple_of(c * ROWS, ROWS)
                    acc = nd[pl.ds(r0, ROWS), :]
                    for j in range(win):
                        acc = acc + dpn[pl.ds(r0 + j, ROWS), :]
                    du_ref[pl.ds(r0, ROWS), :] = acc.astype(du_ref.dtype)
                    return 0

                lax.fori_loop(0, nchunk, spread, 0)

    blk = pl.BlockSpec((S, LANE), lambda j: (0, j))
    du, dw, ds = pl.pallas_call(
        body, name="pool_bwd", grid=(len(POOL_WINDOWS),),
        in_specs=[_colblk(S, OFF_PL), pl.BlockSpec((None, LANE, LANE), lambda j: (j, 0, 0)), pl.BlockSpec((1, LANE), lambda j: (0, j)), blk],
        out_specs=[blk, pl.BlockSpec((None, LANE, LANE), lambda j: (j, 0, 0)), pl.BlockSpec((8, LANE), lambda j: (0, j))],
        out_shape=[jax.ShapeDtypeStruct((S, BRANCH_W), BF16), jax.ShapeDtypeStruct((len(POOL_WINDOWS), LANE, LANE), F32),
                   jax.ShapeDtypeStruct((8, BRANCH_W), F32)],
        scratch_shapes=[pltpu.VMEM((PL_PAD + S, LANE), F32), pltpu.VMEM((S + PL_PAD, LANE), F32), pltpu.VMEM((S, LANE), F32)],
        compiler_params=_cp(("parallel",)),
    )(proj, wg, scale.reshape(1, BRANCH_W), dy)
    return du, dw, ds[0]


LR_PAD = 8
GELU_C = math.sqrt(2.0 / math.pi)
GELU_A = 0.044715


def _gelu(y):
    return 0.5 * y * (1.0 + jnp.tanh(GELU_C * (y + GELU_A * y * y * y)))


def _dgelu(y):
    t = jnp.tanh(GELU_C * (y + GELU_A * y * y * y))
    return 0.5 * (1.0 + t) + 0.5 * y * (1.0 - t * t) * GELU_C * (1.0 + 3.0 * GELU_A * y * y)


def _lru_gates(xpad, r0, cw_ref, cb, wa, ba, wx, bx, sp8):
    xc = jnp.broadcast_to(cb, (ROWS, LANE))
    for j in range(LRU_CONV):
        xc = xc + cw_ref[pl.ds(j, 1), :] * xpad[pl.ds(r0 + (LR_PAD - LRU_CONV + 1) + j, ROWS), :]
    xb = xc.astype(BF16)
    r = _sigmoid(_dot(xb, wa, NN) + ba)
    ig = _sigmoid(_dot(xb, wx, NN) + bx)
    la = -sp8 * r
    a = jnp.exp(la)
    s = jnp.sqrt(-jnp.tanh(la) * (a * a + 1.0))
    return xc, r, ig, a, s


def lru_fwd(proj, cw8, cb, wa_bd, ba, wx_bd, bx, sp8):
    S = proj.shape[0]
    nchunk = S // ROWS

    def body(x_ref, y_ref, cw_ref, cb_ref, wa_ref, ba_ref, wx_ref, bx_ref, sp_ref, o_ref, h_ref, xpad, a_s):
        xpad[pl.ds(0, LR_PAD), :] = jnp.zeros((LR_PAD, LANE), F32)

        def fill(c, _):
            r0 = pl.multiple_of(c * ROWS, ROWS)
            xpad[pl.ds(LR_PAD + r0, ROWS), :] = x_ref[pl.ds(r0, ROWS), :]
            return 0

        lax.fori_loop(0, nchunk, fill, 0)
        wa = wa_ref[...].astype(BF16)
        wx = wx_ref[...].astype(BF16)

        def gates(c, _):
            r0 = pl.multiple_of(c * ROWS, ROWS)
            xc, r, ig, a, s = _lru_gates(xpad, r0, cw_ref, cb_ref[...], wa, ba_ref[...], wx, bx_ref[...], sp_ref[...])
            a_s[pl.ds(r0, ROWS), :] = a
            h_ref[pl.ds(r0, ROWS), :] = s * (ig * xc)
            return 0

        lax.fori_loop(0, nchunk, gates, 0)

        r8 = lax.broadcasted_iota(I32, (8, LANE), 0)

        def scan(i, h):
            base = pl.multiple_of(i * 8, 8)
            a8 = a_s[pl.ds(base, 8), :]
            b8 = h_ref[pl.ds(base, 8), :]
            out = jnp.zeros((8, LANE), F32)
            for u in range(8):
                h = a8[u:u + 1, :] * h + b8[u:u + 1, :]
                out = jnp.where(r8 == u, h, out)
            h_ref[pl.ds(base, 8), :] = out
            return h

        lax.fori_loop(0, S // 8, scan, jnp.zeros((1, LANE), F32))

        def gate_out(c, _):
            r0 = pl.multiple_of(c * ROWS, ROWS)
            o_ref[pl.ds(r0, ROWS), :] = (h_ref[pl.ds(r0, ROWS), :] * _gelu(y_ref[pl.ds(r0, ROWS), :])).astype(o_ref.dtype)
            return 0

        lax.fori_loop(0, nchunk, gate_out, 0)

    vec = pl.BlockSpec((1, LANE), lambda j: (0, j))
    mat = pl.BlockSpec((None, LANE, LANE), lambda j: (j, 0, 0))
    blk = pl.BlockSpec((S, LANE), lambda j: (0, j))
    return pl.pallas_call(
        body, name="lru_fwd", grid=(BRANCH_W // LANE,),
        in_specs=[_colblk(S, OFF_LX), _colblk(S, OFF_LY), pl.BlockSpec((8, LANE), lambda j: (0, j)), vec, mat, vec, mat, vec, vec],
        out_specs=[blk, blk],
        out_shape=[jax.ShapeDtypeStruct((S, BRANCH_W), BF16), jax.ShapeDtypeStruct((S, BRANCH_W), F32)],
        scratch_shapes=[pltpu.VMEM((LR_PAD + S, LANE), F32), pltpu.VMEM((S, LANE), F32)],
        compiler_params=_cp(("parallel",)),
    )(proj, proj, cw8, cb.reshape(1, -1), wa_bd, ba.reshape(1, -1), wx_bd, bx.reshape(1, -1), sp8.reshape(1, -1))


def lru_bwd(proj, cw8, cb, wa_bd, ba, wx_bd, bx, sp8, h, dout):
    S = proj.shape[0]
    nchunk = S // ROWS

    def body(x_ref, y_ref, cw_ref, cb_ref, wa_ref, ba_ref, wx_ref, bx_ref, sp_ref, h_ref, do_ref,
             dx_ref, dy_ref, dcw_ref, dcb_ref, dwa_ref, dba_ref, dwx_ref, dbx_ref, dsp_ref,
             xpad, a_s, g_s, hpad, dxc):
        xpad[pl.ds(0, LR_PAD), :] = jnp.zeros((LR_PAD, LANE), F32)
        hpad[pl.ds(0, LR_PAD), :] = jnp.zeros((LR_PAD, LANE), F32)
        dxc[pl.ds(S, LR_PAD), :] = jnp.zeros((LR_PAD, LANE), F32)
        dcw_ref[...] = jnp.zeros_like(dcw_ref)
        wa = wa_ref[...].astype(BF16)
        wx = wx_ref[...].astype(BF16)
        cbv, bav, bxv, spv = cb_ref[...], ba_ref[...], bx_ref[...], sp_ref[...]

        def fill(c, _):
            r0 = pl.multiple_of(c * ROWS, ROWS)
            xpad[pl.ds(LR_PAD + r0, ROWS), :] = x_ref[pl.ds(r0, ROWS), :]
            hv = h_ref[pl.ds(r0, ROWS), :]
            hpad[pl.ds(LR_PAD + r0, ROWS), :] = hv
            yv = y_ref[pl.ds(r0, ROWS), :]
            dov = do_ref[pl.ds(r0, ROWS), :].astype(F32)
            g_s[pl.ds(r0, ROWS), :] = dov * _gelu(yv)
            dy_ref[pl.ds(r0, ROWS), :] = (dov * hv * _dgelu(yv)).astype(dy_ref.dtype)
            return 0

        lax.fori_loop(0, nchunk, fill, 0)

        def gates(c, _):
            r0 = pl.multiple_of(c * ROWS, ROWS)
            _, _, _, a, _ = _lru_gates(xpad, r0, cw_ref, cbv, wa, bav, wx, bxv, spv)
            a_s[pl.ds(r0, ROWS), :] = a
            return 0

        lax.fori_loop(0, nchunk, gates, 0)

        r8 = lax.broadcasted_iota(I32, (8, LANE), 0)

        def rscan(i, carry):
            base = pl.multiple_of(S - 8 - i * 8, 8)
            a8 = a_s[pl.ds(base, 8), :]
            g8 = g_s[pl.ds(base, 8), :]
            out = jnp.zeros((8, LANE), F32)
            for u in reversed(range(8)):
                gt = g8[u:u + 1, :] + carry
                out = jnp.where(r8 == u, gt, out)
                carry = a8[u:u + 1, :] * gt
            g_s[pl.ds(base, 8), :] = out
            return carry

        lax.fori_loop(0, S // 8, rscan, jnp.zeros((1, LANE), F32))

        def chain(c, carry):
            dwa, dwx, dba, dbx, dsp, dcb = carry
            r0 = pl.multiple_of(c * ROWS, ROWS)
            xc, r, ig, a, s = _lru_gates(xpad, r0, cw_ref, cbv, wa, bav, wx, bxv, spv)
            gt = g_s[pl.ds(r0, ROWS), :]
            hprev = hpad[pl.ds(r0 + LR_PAD - 1, ROWS), :]
            da = gt * hprev - gt * ig * xc * (a / s)
            dig = gt * s * xc
            dla = da * a
            dsp = dsp + jnp.sum(-dla * r, axis=0, keepdims=True)
            dpr = (-dla * spv) * r * (1.0 - r)
            dpi = dig * ig * (1.0 - ig)
            dprb, dpib, xb = dpr.astype(BF16), dpi.astype(BF16), xc.astype(BF16)
            d = gt * s * ig + _dot(dprb, wa, NT) + _dot(dpib, wx, NT)
            dwa = dwa + _dot(xb, dprb, TN)
            dwx = dwx + _dot(xb, dpib, TN)
            dba = dba + jnp.sum(dpr, axis=0, keepdims=True)
            dbx = dbx + jnp.sum(dpi, axis=0, keepdims=True)
            dcb = dcb + jnp.sum(d, axis=0, keepdims=True)
            dxc[pl.ds(r0, ROWS), :] = d
            for j in range(LRU_CONV):
                xs = xpad[pl.ds(r0 + (LR_PAD - LRU_CONV + 1) + j, ROWS), :]
                dcw_ref[pl.ds(j, 1), :] += jnp.sum(d * xs, axis=0, keepdims=True)
            return dwa, dwx, dba, dbx, dsp, dcb

        zm, zv = jnp.zeros((LANE, LANE), F32), jnp.zeros((1, LANE), F32)
        dwa, dwx, dba, dbx, dsp, dcb = lax.fori_loop(0, nchunk, chain, (zm, zm, zv, zv, zv, zv))
        dwa_ref[...] = dwa
        dwx_ref[...] = dwx
        dba_ref[...] = jnp.broadcast_to(dba, dba_ref.shape)
        dbx_ref[...] = jnp.broadcast_to(dbx, dbx_ref.shape)
        dsp_ref[...] = jnp.broadcast_to(dsp, dsp_ref.shape)
        dcb_ref[...] = jnp.broadcast_to(dcb, dcb_ref.shape)

        def convt(c, _):
            r0 = pl.multiple_of(c * ROWS, ROWS)
            acc = jnp.zeros((ROWS, LANE), F32)
            for j in range(LRU_CONV):
                acc = acc + cw_ref[pl.ds(j, 1), :] * dxc[pl.ds(r0 + (LRU_CONV - 1) - j, ROWS), :]
            dx_ref[pl.ds(r0, ROWS), :] = acc.astype(dx_ref.dtype)
            return 0

        lax.fori_loop(0, nchunk, convt, 0)

    vec = pl.BlockSpec((1, LANE), lambda j: (0, j))
    vec8 = pl.BlockSpec((8, LANE), lambda j: (0, j))
    mat = pl.BlockSpec((None, LANE, LANE), lambda j: (j, 0, 0))
    blk = pl.BlockSpec((S, LANE), lambda j: (0, j))
    nblk = BRANCH_W // LANE
    v8 = jax.ShapeDtypeStruct((8, BRANCH_W), F32)
    m4 = jax.ShapeDtypeStruct((nblk, LANE, LANE), F32)
    big = jax.ShapeDtypeStruct((S, BRANCH_W), BF16)
    seq = pltpu.VMEM((S, LANE), F32)
    dx, dy, dcw, dcb, dwa, dba, dwx, dbx, dsp = pl.pallas_call(
        body, name="lru_bwd", grid=(nblk,),
        in_specs=[_colblk(S, OFF_LX), _colblk(S, OFF_LY), vec8, vec, mat, vec, mat, vec, vec, blk, blk],
        out_specs=[blk, blk, vec8, vec8, mat, vec8, mat, vec8, vec8],
        out_shape=[big, big, v8, v8, m4, v8, m4, v8, v8],
        scratch_shapes=[pltpu.VMEM((LR_PAD + S, LANE), F32), seq, seq, pltpu.VMEM((LR_PAD + S, LANE), F32),
                        pltpu.VMEM((S + LR_PAD, LANE), F32)],
        compiler_params=_cp(("parallel",)),
    )(proj, proj, cw8, cb.reshape(1, -1), wa_bd, ba.reshape(1, -1), wx_bd, bx.reshape(1, -1), sp8.reshape(1, -1), h, dout)
    return dx, dy, dcw, dcb[0], dwa, dba[0], dwx, dbx[0], dsp[0]


MG_COLS = 512
N_BRANCH = 4


def _gate_spec(tm, k):
    return pl.BlockSpec((tm, MG_COLS), lambda j, i: (i, (OFF_GATE + k * D_MODEL) // MG_COLS + j))


def merge_fwd(ups, proj, gate_b, tm=256):
    S = proj.shape[0]
    tm = _tile(S, tm)

    def body(u0, u1, u2, u3, g0, g1, g2, g3, gb_ref, o_ref):
        acc = jnp.zeros((tm, MG_COLS), F32)
        for k, (u, g) in enumerate(((u0, g0), (u1, g1), (u2, g2), (u3, g3))):
            acc = acc + _sigmoid(g[...] + gb_ref[pl.ds(k, 1), :]) * u[...]
        o_ref[...] = acc.astype(o_ref.dtype)

    blk = pl.BlockSpec((tm, MG_COLS), lambda j, i: (i, j))
    return pl.pallas_call(
        body, name="merge_fwd", grid=(D_MODEL // MG_COLS, S // tm),
        in_specs=[blk] * N_BRANCH + [_gate_spec(tm, k) for k in range(N_BRANCH)] + [pl.BlockSpec((N_BRANCH, MG_COLS), lambda j, i: (0, j))],
        out_specs=blk, out_shape=jax.ShapeDtypeStruct((S, D_MODEL), BF16),
        compiler_params=_cp(("parallel", "parallel")),
    )(*ups, proj, proj, proj, proj, gate_b)


def merge_bwd(dmerged, ups, proj, gate_b, tm=256):
    S = proj.shape[0]
    tm = _tile(S, tm)

    def body(dm_ref, u0, u1, u2, u3, g0, g1, g2, g3, gb_ref, du0, du1, du2, du3, dg0, dg1, dg2, dg3, dgb_ref):
        @pl.when(pl.program_id(1) == 0)
        def _():
            dgb_ref[...] = jnp.zeros_like(dgb_ref)

        dm = dm_ref[...].astype(F32)
        for k, (u, g, du, dg) in enumerate(((u0, g0, du0, dg0), (u1, g1, du1, dg1), (u2, g2, du2, dg2), (u3, g3, du3, dg3))):
            sg = _sigmoid(g[...] + gb_ref[pl.ds(k, 1), :])
            du[...] = (dm * sg).astype(du.dtype)
            dgk = dm * u[...] * sg * (1.0 - sg)
            dg[...] = dgk.astype(dg.dtype)
            dgb_ref[pl.ds(8 * k, 8), :] += jnp.broadcast_to(jnp.sum(dgk, axis=0, keepdims=True), (8, MG_COLS))

    blk = pl.BlockSpec((tm, MG_COLS), lambda j, i: (i, j))
    big = jax.ShapeDtypeStruct((S, D_MODEL), BF16)
    outs = pl.pallas_call(
        body, name="merge_bwd", grid=(D_MODEL // MG_COLS, S // tm),
        in_specs=[blk] * (1 + N_BRANCH) + [_gate_spec(tm, k) for k in range(N_BRANCH)] + [pl.BlockSpec((N_BRANCH, MG_COLS), lambda j, i: (0, j))],
        out_specs=[blk] * (2 * N_BRANCH) + [pl.BlockSpec((8 * N_BRANCH, MG_COLS), lambda j, i: (0, j))],
        out_shape=[big] * (2 * N_BRANCH) + [jax.ShapeDtypeStruct((8 * N_BRANCH, D_MODEL), F32)],
        compiler_params=_cp(("parallel", "arbitrary")),
    )(dmerged, *ups, proj, proj, proj, proj, gate_b)
    return outs[:N_BRANCH], outs[N_BRANCH:2 * N_BRANCH], outs[-1].reshape(N_BRANCH, 8, D_MODEL)[:, 0]


def attn_fwd(q, kv, tm=512):
    S = q.shape[0]
    M = kv.shape[0]
    tm = _tile(S, tm)
    scale = XA_HD ** -0.5

    def body(q_ref, kv_ref, o_ref):
        for hh in range(XA_HEADS):
            cs = pl.ds(hh * XA_HD, XA_HD)
            qh = q_ref[:, cs]
            kh = kv_ref[:, cs]
            vh = kv_ref[:, pl.ds(D_MODEL + hh * XA_HD, XA_HD)]
            s = _dot(qh, kh, NT) * scale
            p = jnp.exp(s - jnp.max(s, axis=-1, keepdims=True))
            p = p / jnp.sum(p, axis=-1, keepdims=True)
            o_ref[:, cs] = _dot(p.astype(BF16), vh, NN).astype(o_ref.dtype)

    return pl.pallas_call(
        body, name="attn_fwd", grid=(S // tm,),
        in_specs=[pl.BlockSpec((tm, D_MODEL), lambda i: (i, 0)), pl.BlockSpec((M, 2 * D_MODEL), lambda i: (0, 0))],
        out_specs=pl.BlockSpec((tm, D_MODEL), lambda i: (i, 0)), out_shape=jax.ShapeDtypeStruct((S, D_MODEL), BF16),
        compiler_params=_cp(("parallel",)),
    )(q, kv)


def attn_bwd(q, kv, do, tm=512):
    S = q.shape[0]
    M = kv.shape[0]
    tm = _tile(S, tm)
    scale = XA_HD ** -0.5

    def body(q_ref, kv_ref, do_ref, dq_ref, dkv_ref):
        @pl.when(pl.program_id(0) == 0)
        def _():
            dkv_ref[...] = jnp.zeros_like(dkv_ref)

        for hh in range(XA_HEADS):
            cs = pl.ds(hh * XA_HD, XA_HD)
            vs = pl.ds(D_MODEL + hh * XA_HD, XA_HD)
            qh = q_ref[:, cs]
            kh = kv_ref[:, cs]
            vh = kv_ref[:, vs]
            doh = do_ref[:, cs]
            s = _dot(qh, kh, NT) * scale
            p = jnp.exp(s - jnp.max(s, axis=-1, keepdims=True))
            p = p / jnp.sum(p, axis=-1, keepdims=True)
            dp = _dot(doh, vh, NT)
            ds = (p * (dp - jnp.sum(dp * p, axis=-1, keepdims=True)) * scale).astype(BF16)
            dq_ref[:, cs] = _dot(ds, kh, NN).astype(dq_ref.dtype)
            dkv_ref[:, cs] += _dot(ds, qh, TN)
            dkv_ref[:, vs] += _dot(p.astype(BF16), doh, TN)

    row = pl.BlockSpec((tm, D_MODEL), lambda i: (i, 0))
    full = pl.BlockSpec((M, 2 * D_MODEL), lambda i: (0, 0))
    return pl.pallas_call(
        body, name="attn_bwd", grid=(S // tm,), in_specs=[row, full, row], out_specs=[row, full],
        out_shape=[jax.ShapeDtypeStruct((S, D_MODEL), BF16), jax.ShapeDtypeStruct((M, 2 * D_MODEL), F32)],
        compiler_params=_cp(("arbitrary",)),
    )(q, kv, do)


def sum_parts(parts, own=None, tm=256):
    n, R, C = parts.shape
    tm = _tile(R, tm)
    has_own = own is not None

    def body(*refs):
        p_ref, o_ref = refs[0], refs[-1]
        acc = refs[1][...].astype(F32) if has_own else p_ref[0].astype(F32)
        for j in range(0 if has_own else 1, n):
            acc = acc + p_ref[j].astype(F32)
        o_ref[...] = acc

    row = pl.BlockSpec((tm, C), lambda i: (i, 0))
    return pl.pallas_call(
        body, name="sum_parts", grid=(R // tm,),
        in_specs=[pl.BlockSpec((n, tm, C), lambda i: (0, i, 0))] + ([row] if has_own else []), out_specs=row,
        out_shape=jax.ShapeDtypeStruct((R, C), F32), compiler_params=_cp(("parallel",)),
    )(*([parts, own] if has_own else [parts]))


def adamw(w, g, m, v, tm=256):
    R, C = w.shape
    tm = _tile(R, tm)
    c1 = 1.0 / (1.0 - ADAM_B1 ** ADAM_STEP)
    c2 = 1.0 / (1.0 - ADAM_B2 ** ADAM_STEP)

    def body(w_ref, g_ref, m_ref, v_ref, d_ref, nm_ref, nv_ref):
        gv = g_ref[...]
        nm = ADAM_B1 * m_ref[...] + (1.0 - ADAM_B1) * gv
        nv = ADAM_B2 * v_ref[...] + (1.0 - ADAM_B2) * (gv * gv)
        nm_ref[...] = nm
        nv_ref[...] = nv
        d_ref[...] = -ADAM_LR * ((nm * c1) / (jnp.sqrt(nv * c2) + ADAM_EPS) + ADAM_WD * w_ref[...])

    blk = pl.BlockSpec((tm, C), lambda i: (i, 0))
    sd = jax.ShapeDtypeStruct((R, C), F32)
    return pl.pallas_call(
        body, name="adamw", grid=(R // tm,), in_specs=[blk] * 4, out_specs=[blk] * 3, out_shape=[sd] * 3,
        compiler_params=_cp(("parallel",)),
    )(w, g, m, v)


ANY = pl.BlockSpec(memory_space=pl.ANY)


def _place():
    return lax.axis_index("x"), lax.axis_index("y"), lax.axis_index("c")


def _slot(px, py, pc):
    return 4 * px + 2 * py + pc


def all_gather(name, shards, after=()):
    n = len(shards)
    n_in = n + len(after)

    def body(*refs):
        x_refs, out_refs = refs[:n], refs[n_in:n_in + n]
        send_sems, recv_sems, local_sems = refs[n_in + n:]
        x, y, c = _place()
        me, sibling = (x, y, c), (x, y, 1 - c)
        chips = [(1 - x, y), (x, 1 - y), (1 - x, 1 - y)]

        def copy(a, k, block, to, src=None):
            rows = out_refs[a].at[_slot(*block)]
            return pltpu.make_async_remote_copy(
                src_ref=rows if src is None else src, dst_ref=rows,
                send_sem=send_sems.at[7 * a + k], recv_sem=recv_sems.at[7 * a + k],
                device_id=to, device_id_type=MESH)

        mine = [pltpu.make_async_copy(x_refs[a], out_refs[a].at[_slot(*me)], local_sems.at[a]) for a in range(n)]
        for cp in mine:
            cp.start()
        first = []
        for a in range(n):
            first.append(copy(a, 0, me, sibling, src=x_refs[a]))
            first += [copy(a, 1 + j, me, (*chip, c), src=x_refs[a]) for j, chip in enumerate(chips)]
        for cp in first:
            cp.start()
        passed = []
        for a in range(n):
            for j, chip in enumerate(chips):
                copy(a, 1 + j, (*chip, c), me).wait_recv()
                cp = copy(a, 4 + j, (*chip, c), sibling)
                cp.start()
                passed.append(cp)
        for a in range(n):
            copy(a, 0, sibling, me).wait_recv()
            for j, chip in enumerate(chips):
                copy(a, 4 + j, (*chip, 1 - c), me).wait_recv()
        for cp in first + passed:
            cp.wait_send()
        for cp in mine:
            cp.wait()

    return pl.pallas_call(
        body, name=name, in_specs=[ANY] * n_in, out_specs=[ANY] * n,
        out_shape=[jax.ShapeDtypeStruct((N_DEV, *s.shape), s.dtype) for s in shards],
        scratch_shapes=[pltpu.SemaphoreType.DMA((7 * n,)), pltpu.SemaphoreType.DMA((7 * n,)), pltpu.SemaphoreType.DMA((n,))],
    )(*shards, *after)


HBM = pl.BlockSpec(memory_space=pltpu.HBM)
SEM = pl.BlockSpec(memory_space=pltpu.SEMAPHORE)
EFFECT = pltpu.SideEffectType.DATAFLOW_SIDE_EFFECTING
N_PEER = N_DEV - 1
RELATIONS = [(dx, dy, dc) for dx in (0, 1) for dy in (0, 1) for dc in (0, 1)][1:]


def _peer(place, rel):
    return tuple(1 - v if d else v for v, d in zip(place, rel))


def gather_start(name, shards, me, before):
    n = len(shards)

    def body(*refs):
        x_refs, land_refs = refs[:n], refs[n:2 * n]
        send_sems, recv_sems = refs[2 * n + len(before):2 * n + len(before) + 2]
        token = refs[-1]
        place = _place()
        mine = _slot(*place)
        for a in range(n):
            for rel in RELATIONS:
                pltpu.make_async_remote_copy(
                    src_ref=x_refs[a], dst_ref=land_refs[a].at[mine], send_sem=send_sems.at[a], recv_sem=recv_sems.at[a],
                    device_id=_peer(place, rel), device_id_type=MESH).start()
        token[...] = jnp.zeros_like(token)

    lands = [lax.dynamic_update_index_in_dim(lax.empty((N_DEV, *s.shape), s.dtype), s, me, 0) for s in shards]
    outs = pl.pallas_call(
        body, name=name,
        in_specs=[HBM] * (2 * n) + [ANY] * len(before),
        out_specs=[SEM, SEM] + [HBM] * (2 * n) + [pl.BlockSpec(memory_space=pltpu.VMEM)],
        out_shape=[pltpu.SemaphoreType.DMA((n,)), pltpu.SemaphoreType.DMA((n,))]
        + [pltpu.HBM(t.shape, t.dtype) for t in (*shards, *lands)] + [jax.ShapeDtypeStruct((8, LANE), F32)],
        input_output_aliases={i: 2 + i for i in range(2 * n)},
        compiler_params=pltpu.CompilerParams(has_side_effects=EFFECT),
    )(*[pltpu.with_memory_space_constraint(t, pltpu.HBM) for t in (*shards, *lands)], *before)
    return (outs[0], outs[1], outs[2:2 + n], outs[2 + n:2 + 2 * n]), outs[-1]


def gather_wait(name, state, after):
    send_sems, recv_sems, shards, lands = state
    n = len(shards)

    def body(*refs):
        land_refs = refs[n:2 * n]
        s_sems, r_sems = refs[2 * n:2 * n + 2]
        place = _place()
        for a in range(n):
            seven = land_refs[a].at[pl.ds(0, N_PEER)]
            cp = pltpu.make_async_remote_copy(
                src_ref=seven, dst_ref=seven, send_sem=s_sems.at[a], recv_sem=r_sems.at[a], device_id=place, device_id_type=MESH)
            cp.wait_send()
            cp.wait_recv()

    outs = pl.pallas_call(
        body, name=name,
        in_specs=[HBM] * (2 * n) + [SEM, SEM] + [ANY] * len(after), out_specs=[HBM] * (2 * n),
        out_shape=[pltpu.HBM(t.shape, t.dtype) for t in (*shards, *lands)],
        input_output_aliases={i: i for i in range(2 * n)},
        compiler_params=pltpu.CompilerParams(has_side_effects=EFFECT),
    )(*shards, *lands, send_sems, recv_sems, *after)
    return outs[n:]


def exchange_start(name, grads, before):
    n = len(grads)

    def body(*refs):
        g_refs, land_refs = refs[:n], refs[n:2 * n]
        send_sems, recv_sems = refs[2 * n + 1:2 * n + 3]
        token = refs[-1]
        place = _place()
        for a in range(n):
            for r, rel in enumerate(RELATIONS):
                p = _peer(place, rel)
                pltpu.make_async_remote_copy(
                    src_ref=g_refs[a].at[_slot(*p)], dst_ref=land_refs[a].at[r],
                    send_sem=send_sems.at[a], recv_sem=recv_sems.at[a], device_id=p, device_id_type=MESH).start()
        token[...] = jnp.zeros_like(token)

    lands = [lax.empty((N_PEER, *g.shape[1:]), g.dtype) for g in grads]
    outs = pl.pallas_call(
        body, name=name,
        in_specs=[HBM] * (2 * n) + [ANY],
        out_specs=[SEM, SEM] + [HBM] * (2 * n) + [pl.BlockSpec(memory_space=pltpu.VMEM)],
        out_shape=[pltpu.SemaphoreType.DMA((n,)), pltpu.SemaphoreType.DMA((n,))]
        + [pltpu.HBM(g.shape, g.dtype) for g in grads] + [pltpu.HBM(t.shape, t.dtype) for t in lands]
        + [jax.ShapeDtypeStruct((8, LANE), F32)],
        input_output_aliases={i: 2 + i for i in range(2 * n)},
        compiler_params=pltpu.CompilerParams(has_side_effects=EFFECT),
    )(*[pltpu.with_memory_space_constraint(t, pltpu.HBM) for t in (*grads, *lands)], before)
    return (outs[0], outs[1], outs[2:2 + n], outs[2 + n:2 + 2 * n]), outs[-1]


def exchange_wait(name, state, after):
    send_sems, recv_sems, grads, lands = state
    n = len(grads)

    def body(*refs):
        g_refs, land_refs = refs[:n], refs[n:2 * n]
        s_sems, r_sems = refs[2 * n:2 * n + 2]
        place = _place()
        for a in range(n):
            cp = pltpu.make_async_remote_copy(
                src_ref=g_refs[a].at[pl.ds(0, N_PEER)], dst_ref=land_refs[a],
                send_sem=s_sems.at[a], recv_sem=r_sems.at[a], device_id=place, device_id_type=MESH)
            cp.wait_send()
            cp.wait_recv()

    outs = pl.pallas_call(
        body, name=name,
        in_specs=[HBM] * (2 * n) + [SEM, SEM, ANY], out_specs=[HBM] * (2 * n),
        out_shape=[pltpu.HBM(t.shape, t.dtype) for t in (*grads, *lands)],
        input_output_aliases={i: i for i in range(2 * n)},
        compiler_params=pltpu.CompilerParams(has_side_effects=EFFECT),
    )(*grads, *lands, send_sems, recv_sems, after)
    return outs[:n], outs[n:]


WEIGHTS = ['norm_mix_w', 'w_in', 'hg_lb_raw', 'hg_norm_w', 'cv_dw_w', 'cv_dw_b', 'cv_ln_w', 'cv_ln_b', 'pl_w', 'pl_scale',
           'lru_conv_w', 'lru_conv_b', 'lru_wa', 'lru_ba', 'lru_wx', 'lru_bx', 'lru_lambda', 'gate_b', 'w_branch', 'w_out',
           'norm_mem_w', 'mem_norm_w', 'xa_wq', 'xa_wkv', 'xa_wo', 'norm_ffn_w', 'ffn_w1', 'ffn_w2', 'final_norm_w']
BIG = ('w_in', 'w_branch', 'w_out', 'xa_wq', 'xa_wkv', 'xa_wo', 'ffn_w1', 'ffn_w2')
SMALL_SHARDED = ('cv_dw_w', 'lru_conv_w', 'gate_b')
SMALL = tuple(n for n in WEIGHTS if n not in BIG and n not in SMALL_SHARDED)
PACK_ROWS = 256


def _pack(arrs):
    flat = jnp.concatenate([a.reshape(-1).astype(F32) for a in arrs])
    tile = PACK_ROWS * LANE
    padded = -(-flat.shape[0] // tile) * tile
    return jnp.pad(flat, (0, padded - flat.shape[0])).reshape(-1, LANE)


def _unpack(packed, shapes):
    flat = packed.reshape(-1)
    out, off = [], 0
    for s in shapes:
        n = math.prod(s)
        out.append(flat[off:off + n].reshape(s))
        off += n
    return out


def _gather_last(g, shard_shape):
    nd = len(shard_shape)
    full = jnp.moveaxis(g, 0, nd - 1)
    return full.reshape(*shard_shape[:-1], N_DEV * shard_shape[-1])


def _block_diag(w):
    w2 = w.reshape(4, 2, 64, 64)
    z = jnp.zeros((4, 64, 64), w.dtype)
    return jnp.concatenate([jnp.concatenate([w2[:, 0], z], axis=2), jnp.concatenate([z, w2[:, 1]], axis=2)], axis=1)


def _block_diag_t(d):
    return jnp.stack([d[:, :64, :64], d[:, 64:, 64:]], axis=1).reshape(8, 64, 64)


def _lower_bounds(raw):
    lb = jnp.cumsum(jax.nn.softmax(raw.astype(F32), axis=0), axis=0)
    return lb - lb[0:1]


def _decay_rates(lam):
    return (LRU_C * jax.nn.softplus(-lam.astype(F32))).reshape(DEPTH, BRANCH_W)


def _relu2(acc):
    r = jnp.maximum(acc, 0.0)
    return acc, r * r


def _relu2_grad(acc, u):
    return (acc * 2.0 * jnp.maximum(u, 0.0),)


def _add(acc, e):
    return (acc + e,)


def _layer_fwd(x0, mem, p, g, after=()):
    h1 = rms_fwd("rms_mix", x0, p['norm_mix_w'], after=after)
    proj = mm_nt("mm_in", h1, g['w_in'])[0]
    b_hg, states, o_hg = hgrn_fwd(proj, p['lb'], p['hg_norm_w'])
    zc = cv_fwd(proj, p['cv_w32'], p['cv_dw_b'])
    b_cv = ln_silu_fwd(zc, p['cv_ln_w'], p['cv_ln_b'])
    b_pl = pool_fwd(proj, p['pl_w'], p['pl_scale'])
    b_lru, hst = lru_fwd(proj, p['lru_cw8'], p['lru_conv_b'], p['wa_bd'], p['lru_ba'], p['wx_bd'], p['lru_bx'], p['sp8'])
    branches = [b_hg, b_cv, b_pl, b_lru]
    ups = [mm_nn_cb("mm_up", branches[k], g['w_branch'][k])[0] for k in range(N_BRANCH)]
    merged = merge_fwd(ups, proj, p['gate_b'])
    x1 = mm_nn("mm_out", merged, g['w_out'], epi=_add, extras=(x0,))[0]
    h2 = rms_fwd("rms_mem", x1, p['norm_mem_w'])
    q = mm_nn("mm_q", h2, g['xa_wq'], out_dtype=BF16)[0]
    memn = rms_fwd("rms_memtok", mem, p['mem_norm_w'])
    kv = mm_nn_cb("mm_kv", memn, g['xa_wkv'], out_dtype=BF16)[0]
    oa = attn_fwd(q, kv)
    x2 = mm_nn("mm_o", oa, g['xa_wo'], epi=_add, extras=(x1,))[0]
    h3 = rms_fwd("rms_ffn", x2, p['norm_ffn_w'])
    u, act = mm_nn_cb("mm_ffn1", h3, g['ffn_w1'], epi=_relu2, out_dtypes=[F32, BF16])
    x3 = mm_nn("mm_ffn2", act, g['ffn_w2'], epi=_add, extras=(x2,))[0]
    res = dict(x0=x0, h1=h1, proj=proj, states=states, o_hg=o_hg, zc=zc, hst=hst, branches=branches, ups=ups, merged=merged,
               x1=x1, h2=h2, q=q, memn=memn, kv=kv, oa=oa, x2=x2, h3=h3, u=u, act=act)
    return x3, res


def _layer_bwd(dx3, mem, p, g, r, after=()):
    gs, gb = {}, {}
    du = mm_nt("mm_dffn2", dx3, g['ffn_w2'], out_dtype=BF16, epi=_relu2_grad, extras=(r['u'],), after=after)[0]
    gb['ffn_w2'] = mm_tn("mm_gw2", r['act'], dx3).reshape(N_DEV, -1, D_MODEL)
    gb['ffn_w1'] = mm_tn_cb("mm_gw1", r['h3'], du, N_DEV)
    dh3 = mm_nt_cb("mm_dffn1", du, g['ffn_w1'], out_dtype=BF16)[0]
    dx2, gs['norm_ffn_w'] = rms_bwd("rmsb_ffn", r['x2'], p['norm_ffn_w'], dh3, dx3)
    doa = mm_nt("mm_do", dx2, g['xa_wo'], out_dtype=BF16)[0]
    gb['xa_wo'] = mm_tn("mm_gwo", r['oa'], dx2).reshape(N_DEV, -1, D_MODEL)
    dq, dkv = attn_bwd(r['q'], r['kv'], doa)
    gb['xa_wq'] = mm_tn("mm_gwq", r['h2'], dq).reshape(N_DEV, -1, D_MODEL)
    dh2 = mm_nt("mm_dq", dq, g['xa_wq'], out_dtype=BF16)[0]
    gb['xa_wkv'] = mm_tn_cb("mm_gwkv", r['memn'], dkv, N_DEV)
    dmemn = mm_nt_cb("mm_dkv", dkv, g['xa_wkv'], out_dtype=BF16)[0]
    _, gs['mem_norm_w'] = rms_bwd("rmsb_memtok", mem, p['mem_norm_w'], dmemn)
    dx1, gs['norm_mem_w'] = rms_bwd("rmsb_mem", r['x1'], p['norm_mem_w'], dh2, dx2)
    dmerged = mm_nt("mm_dout", dx1, g['w_out'], out_dtype=BF16)[0]
    gb['w_out'] = mm_tn("mm_gwout", r['merged'], dx1).reshape(N_DEV, -1, D_MODEL)
    dups, dgates, gs['gate_b'] = merge_bwd(dmerged, r['ups'], r['proj'], p['gate_b'])
    gb['w_branch'] = [mm_tn_cb("mm_gwb", r['branches'][k], dups[k], N_DEV) for k in range(N_BRANCH)]
    db = [mm_nt_cb("mm_dup", dups[k], g['w_branch'][k], out_dtype=BF16)[0] for k in range(N_BRANCH)]
    dq_, df_, dv_, dg_, gs['lb'], gs['hg_norm_w'] = hgrn_bwd(r['proj'], p['lb'], p['hg_norm_w'], r['states'], r['o_hg'], db[0])
    dzc, gs['cv_ln_w'], gs['cv_ln_b'] = ln_silu_bwd(r['zc'], p['cv_ln_w'], p['cv_ln_b'], db[1])
    dca, dcg, dcw, gs['cv_dw_b'] = cv_bwd(r['proj'], p['cv_w32'], dzc)
    gs['cv_dw_w'] = dcw[:CV_KERNEL]
    dpu, gs['pl_w'], gs['pl_scale'] = pool_bwd(r['proj'], p['pl_w'], p['pl_scale'], db[2])
    dlx, dly, dlcw, gs['lru_conv_b'], dwa, gs['lru_ba'], dwx, gs['lru_bx'], gs['sp8'] = lru_bwd(
        r['proj'], p['lru_cw8'], p['lru_conv_b'], p['wa_bd'], p['lru_ba'], p['wx_bd'], p['lru_bx'], p['sp8'], r['hst'], db[3])
    gs['lru_conv_w'] = dlcw[:LRU_CONV]
    gs['lru_wa'], gs['lru_wx'] = _block_diag_t(dwa), _block_diag_t(dwx)
    gs['lru_ba'], gs['lru_bx'] = gs['lru_ba'].reshape(8, 64), gs['lru_bx'].reshape(8, 64)
    dproj = jnp.concatenate([dq_, df_, dv_, dg_, dca, dcg, dpu, dlx, dly, *dgates], axis=1)
    gb['w_in'] = mm_tn("mm_gwin", dproj, r['h1']).reshape(N_DEV, -1, D_MODEL)
    dh1 = mm_nn("mm_din", dproj, g['w_in'], out_dtype=BF16)[0]
    dx0, gs['norm_mix_w'] = rms_bwd("rmsb_mix", r['x0'], p['norm_mix_w'], dh1, dx1)
    return dx0, gs, gb


def kernel(x, mem, norm_mix_w, w_in, hg_lb_raw, hg_norm_w, cv_dw_w, cv_dw_b, cv_ln_w, cv_ln_b, pl_w, pl_scale, lru_conv_w, lru_conv_b, lru_wa, lru_ba, lru_wx, lru_bx, lru_lambda, gate_b, w_branch, w_out, norm_mem_w, mem_norm_w, xa_wq, xa_wkv, xa_wo, norm_ffn_w, ffn_w1, ffn_w2, final_norm_w, loss_target, m_norm_mix_w, m_w_in, m_hg_lb_raw, m_hg_norm_w, m_cv_dw_w, m_cv_dw_b, m_cv_ln_w, m_cv_ln_b, m_pl_w, m_pl_scale, m_lru_conv_w, m_lru_conv_b, m_lru_wa, m_lru_ba, m_lru_wx, m_lru_bx, m_lru_lambda, m_gate_b, m_w_branch, m_w_out, m_norm_mem_w, m_mem_norm_w, m_xa_wq, m_xa_wkv, m_xa_wo, m_norm_ffn_w, m_ffn_w1, m_ffn_w2, m_final_norm_w, v_norm_mix_w, v_w_in, v_hg_lb_raw, v_hg_norm_w, v_cv_dw_w, v_cv_dw_b, v_cv_ln_w, v_cv_ln_b, v_pl_w, v_pl_scale, v_lru_conv_w, v_lru_conv_b, v_lru_wa, v_lru_ba, v_lru_wx, v_lru_bx, v_lru_lambda, v_gate_b, v_w_branch, v_w_out, v_norm_mem_w, v_mem_norm_w, v_xa_wq, v_xa_wkv, v_xa_wo, v_norm_ffn_w, v_ffn_w1, v_ffn_w2, v_final_norm_w):
    W = dict(zip(WEIGHTS, (norm_mix_w, w_in, hg_lb_raw, hg_norm_w, cv_dw_w, cv_dw_b, cv_ln_w, cv_ln_b, pl_w, pl_scale, lru_conv_w, lru_conv_b, lru_wa, lru_ba, lru_wx, lru_bx, lru_lambda, gate_b, w_branch, w_out, norm_mem_w, mem_norm_w, xa_wq, xa_wkv, xa_wo, norm_ffn_w, ffn_w1, ffn_w2, final_norm_w)))
    Mo = dict(zip(WEIGHTS, (m_norm_mix_w, m_w_in, m_hg_lb_raw, m_hg_norm_w, m_cv_dw_w, m_cv_dw_b, m_cv_ln_w, m_cv_ln_b, m_pl_w, m_pl_scale, m_lru_conv_w, m_lru_conv_b, m_lru_wa, m_lru_ba, m_lru_wx, m_lru_bx, m_lru_lambda, m_gate_b, m_w_branch, m_w_out, m_norm_mem_w, m_mem_norm_w, m_xa_wq, m_xa_wkv, m_xa_wo, m_norm_ffn_w, m_ffn_w1, m_ffn_w2, m_final_norm_w)))
    Vo = dict(zip(WEIGHTS, (v_norm_mix_w, v_w_in, v_hg_lb_raw, v_hg_norm_w, v_cv_dw_w, v_cv_dw_b, v_cv_ln_w, v_cv_ln_b, v_pl_w, v_pl_scale, v_lru_conv_w, v_lru_conv_b, v_lru_wa, v_lru_ba, v_lru_wx, v_lru_bx, v_lru_lambda, v_gate_b, v_w_branch, v_w_out, v_norm_mem_w, v_mem_norm_w, v_xa_wq, v_xa_wkv, v_xa_wo, v_norm_ffn_w, v_ffn_w1, v_ffn_w2, v_final_norm_w)))
    me = _slot(*_place())
    xs, mems, target = x[0], mem[0], loss_target[0]

    shard_shapes = [W[n].shape for n in SMALL_SHARDED]
    gathered = all_gather("ag_small", [_pack([W[n] for n in SMALL_SHARDED])])[0]
    parts = [jnp.stack(ps) for ps in zip(*[_unpack(gathered[d], shard_shapes) for d in range(N_DEV)])]
    full_small = {n: _gather_last(parts[i], shard_shapes[i]) for i, n in enumerate(SMALL_SHARDED)}
    lb_all, lb_vjp = jax.vjp(_lower_bounds, hg_lb_raw)
    sp8_all, sp8_vjp = jax.vjp(_decay_rates, lru_lambda)

    def layer_params(l):
        p = {n: W[n][l] for n in SMALL if n != 'final_norm_w'}
        p['lb'] = lb_all[l]
        p['sp8'] = sp8_all[l]
        p['cv_w32'] = jnp.pad(full_small['cv_dw_w'][l], ((0, 32 - CV_KERNEL), (0, 0)))
        p['lru_cw8'] = jnp.pad(full_small['lru_conv_w'][l], ((0, 8 - LRU_CONV), (0, 0)))
        p['gate_b'] = full_small['gate_b'][l]
        p['wa_bd'], p['wx_bd'] = _block_diag(lru_wa[l]), _block_diag(lru_wx[l])
        p['lru_ba'], p['lru_bx'] = lru_ba[l].reshape(-1), lru_bx[l].reshape(-1)
        return p

    def start_gather(l, before):
        shards = [jnp.transpose(w_in[l]).astype(BF16)] + [w_branch[l, k].astype(BF16) for k in range(N_BRANCH)]
        shards += [w[l].astype(BF16) for w in (w_out, xa_wq, xa_wkv, xa_wo, ffn_w1, ffn_w2)]
        return gather_start(f"ag_start{l}", shards, me, before)

    def finish_gather(l, state, after):
        o = gather_wait(f"ag_wait{l}", state, after)
        return dict(w_in=o[0].reshape(IN_W, D_MODEL), w_branch=o[1:5], w_out=o[5].reshape(D_MODEL, D_MODEL),
                    xa_wq=o[6].reshape(D_MODEL, D_MODEL), xa_wkv=o[7], xa_wo=o[8].reshape(D_MODEL, D_MODEL),
                    ffn_w1=o[9], ffn_w2=o[10].reshape(D_FF, D_MODEL))

    params = [layer_params(l) for l in range(DEPTH)]
    mats, residuals = [], []
    xc = xs
    gstate, _ = start_gather(0, ())
    for l in range(DEPTH):
        mats.append(finish_gather(l, gstate, (xc,)))
        token = ()
        if l + 1 < DEPTH:
            gstate, tok = start_gather(l + 1, (mats[l]['w_out'],))
            token = (tok,)
        xc, res = _layer_fwd(xc, mems, params[l], mats[l], after=token)
        residuals.append(res)
    loss_part, dx, g_final = loss_head(xc, final_norm_w, target)
    loss = lax.psum(loss_part, ("x", "y", "c"))

    small_grads = [None] * DEPTH
    big_grads = [None] * DEPTH

    def land(l, state, after):
        sent, landed = exchange_wait(f"rs_wait{l}", state, after)
        big_grads[l] = [
            sum_parts(t.reshape(N_PEER, -1, t.shape[-1]),
                      lax.dynamic_index_in_dim(s, me, 0, keepdims=False).reshape(-1, s.shape[-1])).reshape(t.shape[1:])
            for s, t in zip(sent, landed)]

    state, token = None, ()
    for l in reversed(range(DEPTH)):
        dx, gs, gb = _layer_bwd(dx, mems, params[l], mats[l], residuals[l], after=token)
        small_grads[l] = gs
        if state is not None:
            land(l + 1, state, dx)
        order = [gb['w_in'], *gb['w_branch'], gb['w_out'], gb['xa_wq'], gb['xa_wkv'], gb['xa_wo'], gb['ffn_w1'], gb['ffn_w2']]
        state, tok = exchange_start(f"rs_start{l}", order, dx)
        token = (tok,)

    def stacked(n):
        return jnp.stack([small_grads[l][n] for l in range(DEPTH)])

    part = {n: stacked(n) for n in SMALL if n not in ('final_norm_w', 'hg_lb_raw', 'lru_lambda')}
    part['final_norm_w'] = g_final
    part['hg_lb_raw'] = lb_vjp(stacked('lb'))[0]
    part['lru_lambda'] = sp8_vjp(stacked('sp8'))[0]
    for n in SMALL_SHARDED:
        part[n] = stacked(n)
    names = list(SMALL) + list(SMALL_SHARDED)
    full_shapes = [part[n].shape for n in names]
    everyone = all_gather("ag_grads", [_pack([part[n] for n in names])], after=token)[0]
    total = sum_parts(everyone)
    land(0, state, total)

    G = {}
    G['w_in'] = jnp.stack([jnp.transpose(big_grads[l][0]) for l in range(DEPTH)])
    G['w_branch'] = jnp.stack([jnp.stack(big_grads[l][1:5]) for l in range(DEPTH)])
    for i, n in enumerate(('w_out', 'xa_wq', 'xa_wkv', 'xa_wo', 'ffn_w1', 'ffn_w2')):
        G[n] = jnp.stack([big_grads[l][5 + i] for l in range(DEPTH)])
    for n, t in zip(names, _unpack(total, full_shapes)):
        if n in SMALL_SHARDED:
            c = t.shape[-1] // N_DEV
            t = lax.dynamic_slice_in_dim(t, me * c, c, axis=t.ndim - 1)
        G[n] = t

    delta, new_m, new_v = {}, {}, {}
    for n in BIG:
        c = W[n].shape[-1]
        d, nm, nv = adamw(W[n].reshape(-1, c), G[n].reshape(-1, c), Mo[n].reshape(-1, c), Vo[n].reshape(-1, c))
        delta[n], new_m[n], new_v[n] = d.reshape(W[n].shape), nm.reshape(W[n].shape), nv.reshape(W[n].shape)
    shapes = [W[n].shape for n in names]
    d, nm, nv = adamw(_pack([W[n] for n in names]), _pack([G[n] for n in names]), _pack([Mo[n] for n in names]), _pack([Vo[n] for n in names]))
    for n, a, b, c in zip(names, _unpack(d, shapes), _unpack(nm, shapes), _unpack(nv, shapes)):
        delta[n], new_m[n], new_v[n] = a, b, c
    return (loss, dx[None], *[G[n] for n in WEIGHTS], *[delta[n] for n in WEIGHTS],
            *[new_m[n] for n in WEIGHTS], *[new_v[n] for n in WEIGHTS])
```

```python
import functools
import math

import jax
import jax.numpy as jnp
from jax import lax
from jax.experimental import pallas as pl
from jax.experimental.pallas import tpu as pltpu

F32 = jnp.float32
BF16 = jnp.bfloat16
I32 = jnp.int32

N_DEV = 8
D_MODEL = 1024
DEPTH = 4
CHUNK = 64
EPS = 1e-6
HG_HEADS = 4
BRANCH_W = 512
CV_KERNEL = 31
POOL_WINDOWS = (2, 4, 8, 16)
LRU_CONV = 4
LRU_C = 8.0
XA_HEADS = 4
XA_HD = D_MODEL // XA_HEADS
D_FF = 4 * D_MODEL
IN_W = 8704
OFF_Q, OFF_F, OFF_V, OFF_G, OFF_CV, OFF_PL, OFF_LX, OFF_LY, OFF_GATE = 0, 512, 1024, 1536, 2048, 3072, 3584, 4096, 4608
LANE = 128
ADAM_LR, ADAM_B1, ADAM_B2, ADAM_EPS, ADAM_WD, ADAM_STEP = 0.001, 0.9, 0.999, 1e-08, 0.01, 10
VMEM_LIMIT = 56 * 1024 * 1024
MESH = pl.DeviceIdType.MESH
NEG = -1e30
ANY_SPACE = pl.BlockSpec(memory_space=pl.ANY)


def _cp(sem, **kw):
    return pltpu.CompilerParams(dimension_semantics=sem, vmem_limit_bytes=VMEM_LIMIT, **kw)


def _sigmoid(x):
    return 1.0 / (1.0 + jnp.exp(-x))


def _dsilu(x, s):
    return s * (1.0 + x * (1.0 - s))


def _dot(a, b, cdims, precision=None):
    return lax.dot_general(a, b, (cdims, ((), ())), preferred_element_type=F32, precision=precision)


NN = ((1,), (0,))
NT = ((1,), (1,))
TN = ((0,), (0,))


def _mm(name, a, b, *, grid, a_spec, b_spec, o_specs, out_shapes, acc_shape, cdims, epi=None, extras=(), extra_specs=(), after=()):
    nk = grid[2]
    n_e, n_o = len(extras), len(out_shapes)
    extras = (*extras, *after)
    extra_specs = (*extra_specs, *[ANY_SPACE] * len(after))

    def body(*refs):
        a_ref, b_ref = refs[0], refs[1]
        e_refs = refs[2:2 + n_e]
        o_refs = refs[2 + len(extras):2 + len(extras) + n_o]

        def finish(acc):
            vals = epi(acc, *[r[...] for r in e_refs]) if epi is not None else (acc,)
            for r, v in zip(o_refs, vals):
                r[...] = v.astype(r.dtype)

        part = _dot(a_ref[...].astype(BF16), b_ref[...].astype(BF16), cdims)
        if nk == 1:
            finish(part)
        else:
            acc_ref = refs[-1]
            k = pl.program_id(2)

            @pl.when(k == 0)
            def _():
                acc_ref[...] = part

            @pl.when(k > 0)
            def _():
                acc_ref[...] += part

            @pl.when(k == nk - 1)
            def _():
                finish(acc_ref[...])

    return pl.pallas_call(
        body, name=name, grid=grid,
        in_specs=[a_spec, b_spec, *extra_specs], out_specs=list(o_specs), out_shape=list(out_shapes),
        scratch_shapes=[] if nk == 1 else [pltpu.VMEM(acc_shape, F32)],
        compiler_params=_cp(("parallel", "parallel", "arbitrary")),
    )(a, b, *extras)


def _tile(n, pref):
    t = min(n, pref)
    while n % t:
        t //= 2
    return t


def mm_nt(name, a, b, out_dtype=F32, epi=None, extras=(), n_out=1, out_dtypes=None, tm=1024, tn=1024, tk=2048, after=()):
    M, K = a.shape
    N = b.shape[0]
    tm, tn, tk = _tile(M, tm), _tile(N, tn), _tile(K, tk)
    odt = out_dtypes or [out_dtype] * n_out
    o_spec = pl.BlockSpec((tm, tn), lambda i, j, k: (i, j))
    return _mm(name, a, b, grid=(M // tm, N // tn, K // tk),
               a_spec=pl.BlockSpec((tm, tk), lambda i, j, k: (i, k)),
               b_spec=pl.BlockSpec((tn, tk), lambda i, j, k: (j, k)),
               o_specs=[o_spec] * len(odt), out_shapes=[jax.ShapeDtypeStruct((M, N), d) for d in odt],
               acc_shape=(tm, tn), cdims=NT, epi=epi, extras=extras, extra_specs=[o_spec] * len(extras), after=after)


def mm_nn(name, a, b, out_dtype=F32, epi=None, extras=(), n_out=1, out_dtypes=None, tm=1024, tn=1024, tk=2048):
    M, K = a.shape
    N = b.shape[1]
    tm, tn, tk = _tile(M, tm), _tile(N, tn), _tile(K, tk)
    odt = out_dtypes or [out_dtype] * n_out
    o_spec = pl.BlockSpec((tm, tn), lambda i, j, k: (i, j))
    return _mm(name, a, b, grid=(M // tm, N // tn, K // tk),
               a_spec=pl.BlockSpec((tm, tk), lambda i, j, k: (i, k)),
               b_spec=pl.BlockSpec((tk, tn), lambda i, j, k: (k, j)),
               o_specs=[o_spec] * len(odt), out_shapes=[jax.ShapeDtypeStruct((M, N), d) for d in odt],
               acc_shape=(tm, tn), cdims=NN, epi=epi, extras=extras, extra_specs=[o_spec] * len(extras))


def mm_tn(name, a, b, out_dtype=BF16, tm=1024, tn=1024, tk=1024):
    K, M = a.shape
    N = b.shape[1]
    tm, tn, tk = _tile(M, tm), _tile(N, tn), _tile(K, tk)
    return _mm(name, a, b, grid=(M // tm, N // tn, K // tk),
               a_spec=pl.BlockSpec((tk, tm), lambda i, j, k: (k, i)),
               b_spec=pl.BlockSpec((tk, tn), lambda i, j, k: (k, j)),
               o_specs=[pl.BlockSpec((tm, tn), lambda i, j, k: (i, j))],
               out_shapes=[jax.ShapeDtypeStruct((M, N), out_dtype)], acc_shape=(tm, tn), cdims=TN)[0]


def mm_nn_cb(name, a, b, out_dtype=F32, epi=None, out_dtypes=None, tm=1024):
    M, K = a.shape
    nb, _, c = b.shape
    tm = _tile(M, tm)
    odt = out_dtypes or [out_dtype]
    return _mm(name, a, b, grid=(M // tm, nb, 1),
               a_spec=pl.BlockSpec((tm, K), lambda i, j, k: (i, 0)),
               b_spec=pl.BlockSpec((None, K, c), lambda i, j, k: (j, 0, 0)),
               o_specs=[pl.BlockSpec((tm, c), lambda i, j, k: (i, j))] * len(odt),
               out_shapes=[jax.ShapeDtypeStruct((M, nb * c), d) for d in odt], acc_shape=(tm, c), cdims=NN, epi=epi)


def mm_nt_cb(name, a, b, out_dtype=F32, epi=None, extras=(), tm=1024, tn=512):
    M = a.shape[0]
    nb, K, c = b.shape
    tm, tn = _tile(M, tm), _tile(K, tn)
    o_spec = pl.BlockSpec((tm, tn), lambda i, j, k: (i, j))
    return _mm(name, a, b, grid=(M // tm, K // tn, nb),
               a_spec=pl.BlockSpec((tm, c), lambda i, j, k: (i, k)),
               b_spec=pl.BlockSpec((None, tn, c), lambda i, j, k: (k, j, 0)),
               o_specs=[o_spec], out_shapes=[jax.ShapeDtypeStruct((M, K), out_dtype)],
               acc_shape=(tm, tn), cdims=NT, epi=epi, extras=extras, extra_specs=[o_spec] * len(extras))


def mm_tn_cb(name, a, b, nb, out_dtype=BF16, tm=1024, tk=2048):
    K, M = a.shape
    N = b.shape[1]
    c = N // nb
    tm, tk = _tile(M, tm), _tile(K, tk)
    return _mm(name, a, b, grid=(M // tm, nb, K // tk),
               a_spec=pl.BlockSpec((tk, tm), lambda i, j, k: (k, i)),
               b_spec=pl.BlockSpec((tk, c), lambda i, j, k: (k, j)),
               o_specs=[pl.BlockSpec((None, tm, c), lambda i, j, k: (j, i, 0))],
               out_shapes=[jax.ShapeDtypeStruct((nb, M, c), out_dtype)], acc_shape=(tm, c), cdims=TN)[0]


def rms_fwd(name, x, w, out_dtype=BF16, tm=512, after=()):
    S, D = x.shape
    tm = _tile(S, tm)

    def body(x_ref, w_ref, *rest):
        o_ref = rest[-1]
        xv = x_ref[...]
        r = lax.rsqrt(jnp.mean(xv * xv, axis=-1, keepdims=True) + EPS)
        o_ref[...] = (xv * r * w_ref[...]).astype(o_ref.dtype)

    return pl.pallas_call(
        body, name=name, grid=(S // tm,),
        in_specs=[pl.BlockSpec((tm, D), lambda i: (i, 0)), pl.BlockSpec((1, D), lambda i: (0, 0))] + [ANY_SPACE] * len(after),
        out_specs=pl.BlockSpec((tm, D), lambda i: (i, 0)), out_shape=jax.ShapeDtypeStruct((S, D), out_dtype),
        compiler_params=_cp(("parallel",)),
    )(x, w.reshape(1, D), *after)


def rms_bwd(name, x, w, dh, dres=None, tm=512):
    S, D = x.shape
    tm = _tile(S, tm)
    has_res = dres is not None

    def body(*refs):
        if has_res:
            x_ref, w_ref, dh_ref, dres_ref, dx_ref, dw_ref = refs
        else:
            x_ref, w_ref, dh_ref, dx_ref, dw_ref = refs
        xv = x_ref[...]
        dhv = dh_ref[...].astype(F32)
        r = lax.rsqrt(jnp.mean(xv * xv, axis=-1, keepdims=True) + EPS)
        g = dhv * w_ref[...]
        dx = r * g - xv * (r * r * r) * jnp.mean(xv * g, axis=-1, keepdims=True)
        if has_res:
            dx = dx + dres_ref[...]
        dx_ref[...] = dx

        @pl.when(pl.program_id(0) == 0)
        def _():
            dw_ref[...] = jnp.zeros_like(dw_ref)

        dw_ref[...] += jnp.sum(dhv * xv * r, axis=0, keepdims=True)

    row = pl.BlockSpec((tm, D), lambda i: (i, 0))
    vec = pl.BlockSpec((1, D), lambda i: (0, 0))
    args = [x, w.reshape(1, D), dh] + ([dres] if has_res else [])
    dx, dw = pl.pallas_call(
        body, name=name, grid=(S // tm,),
        in_specs=[row, vec, row] + ([row] if has_res else []),
        out_specs=[row, vec], out_shape=[jax.ShapeDtypeStruct((S, D), F32), jax.ShapeDtypeStruct((1, D), F32)],
        compiler_params=_cp(("arbitrary",)),
    )(*args)
    return dx, dw.reshape(D)


def loss_head(x, w, target, tm=512):
    S, D = x.shape
    tm = _tile(S, tm)

    def body(x_ref, w_ref, t_ref, loss_ref, dx_ref, dw_ref):
        xv = x_ref[...]
        wv = w_ref[...]
        r = lax.rsqrt(jnp.mean(xv * xv, axis=-1, keepdims=True) + EPS)
        y = xv * r * wv
        err = y - t_ref[...]
        dy = err * (1.0 / D)
        g = dy * wv
        dx_ref[...] = r * g - xv * (r * r * r) * jnp.mean(xv * g, axis=-1, keepdims=True)

        @pl.when(pl.program_id(0) == 0)
        def _():
            dw_ref[...] = jnp.zeros_like(dw_ref)
            loss_ref[...] = jnp.zeros_like(loss_ref)

        dw_ref[...] += jnp.sum(dy * xv * r, axis=0, keepdims=True)
        part = 0.5 * jnp.sum(jnp.mean(err * err, axis=-1, keepdims=True), axis=0, keepdims=True)
        loss_ref[...] += jnp.broadcast_to(part, loss_ref.shape)

    row = pl.BlockSpec((tm, D), lambda i: (i, 0))
    vec = pl.BlockSpec((1, D), lambda i: (0, 0))
    loss, dx, dw = pl.pallas_call(
        body, name="loss_head", grid=(S // tm,),
        in_specs=[row, vec, row],
        out_specs=[pl.BlockSpec((1, LANE), lambda i: (0, 0)), row, vec],
        out_shape=[jax.ShapeDtypeStruct((1, LANE), F32), jax.ShapeDtypeStruct((S, D), F32), jax.ShapeDtypeStruct((1, D), F32)],
        compiler_params=_cp(("arbitrary",)),
    )(x, w.reshape(1, D), target)
    return loss[0, 0], dx, dw.reshape(D)


SUB = 16
HG_W = HG_HEADS * LANE
HG_PAD = CHUNK + 2 * SUB


def _hg_gates(q, f, lbv):
    sig = _sigmoid(f)
    fg = lbv + (1.0 - lbv) * sig
    sq = _sigmoid(q)
    return sig, fg, 1.0 - fg, sq, q * sq


def _hg_cumsum(logf):
    ri = lax.broadcasted_iota(I32, (CHUNK, CHUNK), 0)
    ci = lax.broadcasted_iota(I32, (CHUNK, CHUNK), 1)
    return _dot((ci <= ri).astype(F32), logf, NN, precision=lax.Precision.HIGHEST)


def _hg_rows():
    return lax.broadcasted_iota(I32, (CHUNK, LANE), 0)


def _hg_below(qf, kk, b, rows):
    blocks, parts = [jnp.zeros((SUB, CHUNK), F32)], []
    for i in range(1, CHUNK // SUB):
        bref = b[SUB * i - 1:SUB * i, :]
        rs = slice(SUB * i, SUB * (i + 1))
        eq = jnp.exp(b[rs] - bref)
        below = rows < SUB * i
        ek = jnp.exp(jnp.where(below, bref - b, NEG))
        qi = (qf[rs] * eq).astype(BF16)
        ki = (kk * ek).astype(BF16)
        blocks.append(_dot(qi, ki, NT))
        parts.append((qi, ki, eq, ek))
    return jnp.concatenate(blocks, axis=0), parts


def hgrn_fwd(proj, lb, nw, after=()):
    S = proj.shape[0]
    NC = S // CHUNK
    H = HG_HEADS

    def body(q_ref, f_ref, v_ref, g_ref, lb_ref, nw_ref, *rest):
        out_ref, st_out_ref, o_ref, st, kk_p, b_p, v_p = rest[len(after):]
        c = pl.program_id(0)

        @pl.when(c == 0)
        def _():
            st[...] = jnp.zeros_like(st)
            for p in (kk_p, b_p, v_p):
                p[...] = jnp.zeros_like(p)

        st_out_ref[...] = st[...]
        sig, fg, kk_all, sq, qf_all = _hg_gates(q_ref[...], f_ref[...], lb_ref[...])
        b_all = _hg_cumsum(jnp.log(fg))
        kk_p[pl.ds(SUB, CHUNK), :] = kk_all
        b_p[pl.ds(SUB, CHUNK), :] = b_all
        v_p[pl.ds(SUB, CHUNK), :] = v_ref[...]
        rows = _hg_rows()
        sub = rows & (SUB - 1)
        for h in range(H):
            cs = slice(h * LANE, (h + 1) * LANE)
            qf, kk, b, v, g = qf_all[:, cs], kk_all[:, cs], b_all[:, cs], v_ref[:, cs], g_ref[:, cs]
            st_in = st[h]
            o = jnp.zeros((CHUNK, LANE), F32)
            for tau in range(SUB):
                sh = pl.ds(SUB - tau, CHUNK)
                e = jnp.exp(jnp.where(sub >= tau, b - b_p[sh, cs], NEG))
                col = jnp.sum(qf * kk_p[sh, cs] * e, axis=1, keepdims=True)
                o = o + col * v_p[sh, cs]
            poff, _ = _hg_below(qf, kk, b, rows)
            vb = v.astype(BF16)
            bl = b[CHUNK - 1:CHUNK, :]
            o = o + _dot(poff.astype(BF16), vb, NN) + _dot((qf * jnp.exp(b)).astype(BF16), st_in.astype(BF16), NT)
            st[h] = st_in * jnp.exp(bl) + _dot(vb, (kk * jnp.exp(bl - b)).astype(BF16), TN)
            o_ref[:, cs] = o
            r = lax.rsqrt(jnp.mean(o * o, axis=-1, keepdims=True) + EPS)
            out_ref[:, cs] = (o * r * nw_ref[...] * (g * _sigmoid(g))).astype(out_ref.dtype)

    def seg(off):
        return pl.BlockSpec((CHUNK, HG_W), lambda c: (c, off // HG_W))

    blk = pl.BlockSpec((CHUNK, HG_W), lambda c: (c, 0))
    pad = pltpu.VMEM((HG_PAD, HG_W), F32)
    return pl.pallas_call(
        body, name="hgrn_fwd", grid=(NC,),
        in_specs=[seg(OFF_Q), seg(OFF_F), seg(OFF_V), seg(OFF_G),
                  pl.BlockSpec((1, HG_W), lambda c: (0, 0)), pl.BlockSpec((1, LANE), lambda c: (0, 0))] + [ANY_SPACE] * len(after),
        out_specs=[blk, pl.BlockSpec((None, H, LANE, LANE), lambda c: (c, 0, 0, 0)), blk],
        out_shape=[jax.ShapeDtypeStruct((S, HG_W), BF16), jax.ShapeDtypeStruct((NC, H, LANE, LANE), F32),
                   jax.ShapeDtypeStruct((S, HG_W), F32)],
        scratch_shapes=[pltpu.VMEM((H, LANE, LANE), F32), pad, pad, pad],
        compiler_params=_cp(("arbitrary",)),
    )(proj, proj, proj, proj, lb.reshape(1, HG_W), nw.reshape(1, LANE), *after)


def hgrn_bwd(proj, lb, nw, states, o_pre, dout):
    S = proj.shape[0]
    NC = S // CHUNK
    H = HG_HEADS

    def body(q_ref, f_ref, v_ref, g_ref, lb_ref, nw_ref, st_ref, o_ref, do_ref,
             dq_ref, df_ref, dv_ref, dg_ref, dlb_ref, dnw_ref, dst, kk_p, b_p, v_p, qf_p, do_p, db_s, dkk_s):
        c = pl.program_id(0)

        @pl.when(c == 0)
        def _():
            dst[...] = jnp.zeros_like(dst)
            dlb_ref[...] = jnp.zeros_like(dlb_ref)
            dnw_ref[...] = jnp.zeros_like(dnw_ref)
            for p in (kk_p, b_p, v_p, qf_p, do_p):
                p[...] = jnp.zeros_like(p)

        q_all, g_all = q_ref[...], g_ref[...]
        lbv, nwv = lb_ref[...], nw_ref[...]
        sig, fg, kk_all, sq, qf_all = _hg_gates(q_all, f_ref[...], lbv)
        b_all = _hg_cumsum(jnp.log(fg))
        o_all = o_ref[...]
        dov = do_ref[...].astype(F32)
        sg = _sigmoid(g_all)
        gsg = g_all * sg
        dnw_acc = jnp.zeros((1, LANE), F32)
        for h in range(H):
            cs = slice(h * LANE, (h + 1) * LANE)
            o = o_all[:, cs]
            r = lax.rsqrt(jnp.mean(o * o, axis=-1, keepdims=True) + EPS)
            don = dov[:, cs] * gsg[:, cs]
            dnw_acc = dnw_acc + jnp.sum(don * o * r, axis=0, keepdims=True)
            gno = don * nwv
            do_p[pl.ds(SUB, CHUNK), cs] = r * gno - o * (r * r * r) * jnp.mean(o * gno, axis=-1, keepdims=True)
            dg_ref[:, cs] = (dov[:, cs] * (o * r * nwv) * _dsilu(g_all[:, cs], sg[:, cs])).astype(dg_ref.dtype)
        dnw_ref[...] += jnp.broadcast_to(dnw_acc, dnw_ref.shape)
        kk_p[pl.ds(SUB, CHUNK), :] = kk_all
        b_p[pl.ds(SUB, CHUNK), :] = b_all
        v_p[pl.ds(SUB, CHUNK), :] = v_ref[...]
        qf_p[pl.ds(SUB, CHUNK), :] = qf_all
        rows = _hg_rows()
        sub = rows & (SUB - 1)
        for h in range(H):
            cs = slice(h * LANE, (h + 1) * LANE)
            qf, kk, b, v = qf_all[:, cs], kk_all[:, cs], b_all[:, cs], v_ref[:, cs]
            do = do_p[pl.ds(SUB, CHUNK), cs]
            st_in, dstv = st_ref[h], dst[h]
            bl = b[CHUNK - 1:CHUNK, :]
            eb, ebl, el = jnp.exp(b), jnp.exp(bl - b), jnp.exp(bl)
            qe, ke = qf * eb, kk * ebl
            vb, dob, stb, dstb = v.astype(BF16), do.astype(BF16), st_in.astype(BF16), dstv.astype(BF16)
            w_ = _dot(vb, dstb, NN)
            dqf = eb * _dot(dob, stb, NN)
            dkk = ebl * w_
            dv = _dot(ke.astype(BF16), dstb, NT)
            dbl = el * jnp.sum(st_in * dstv, axis=0, keepdims=True) + jnp.sum(ke * w_, axis=0, keepdims=True)
            dst[h] = dstv * el + _dot(dob, qe.astype(BF16), TN)
            poff, parts = _hg_below(qf, kk, b, rows)
            dpoff = _dot(dob, vb, NT).astype(BF16)
            dv = dv + _dot(poff.astype(BF16), dob, TN)
            dq_blocks = [jnp.zeros((SUB, LANE), F32)]
            for i, (qi, ki, eq, ek) in enumerate(parts, start=1):
                dpi = dpoff[SUB * i:SUB * (i + 1), :]
                dq_blocks.append(_dot(dpi, ki, NN) * eq)
                dkk = dkk + _dot(dpi, qi, TN) * ek
            dqf = dqf + jnp.concatenate(dq_blocks, axis=0)
            for tau in range(SUB):
                sh = pl.ds(SUB - tau, CHUNK)
                kd = kk_p[sh, cs]
                e = jnp.exp(jnp.where(sub >= tau, b - b_p[sh, cs], NEG))
                dcol = jnp.sum(do * v_p[sh, cs], axis=1, keepdims=True)
                dqf = dqf + dcol * kd * e
            for tau in range(SUB):
                sh = pl.ds(SUB + tau, CHUNK)
                qu, dou = qf_p[sh, cs], do_p[sh, cs]
                e = jnp.exp(jnp.where(sub + tau < SUB, b_p[sh, cs] - b, NEG))
                qe_ = qu * e
                dcol = jnp.sum(dou * v, axis=1, keepdims=True)
                col = jnp.sum(qe_ * kk, axis=1, keepdims=True)
                dkk = dkk + dcol * qe_
                dv = dv + col * dou
            dv_ref[:, cs] = dv.astype(dv_ref.dtype)
            db = qf * dqf - kk * dkk
            db_s[:, cs] = db + jnp.where(rows == CHUNK - 1, dbl, 0.0)
            dkk_s[:, cs] = dkk
            dq_ref[:, cs] = (dqf * _dsilu(q_all[:, cs], sq[:, cs])).astype(dq_ref.dtype)
        ri = lax.broadcasted_iota(I32, (CHUNK, CHUNK), 0)
        ci = lax.broadcasted_iota(I32, (CHUNK, CHUNK), 1)
        dlogf = _dot((ci >= ri).astype(F32), db_s[...], NN, precision=lax.Precision.HIGHEST)
        dfg = dlogf / fg - dkk_s[...]
        df_ref[...] = (dfg * (1.0 - lbv) * sig * (1.0 - sig)).astype(df_ref.dtype)
        dlb_ref[...] += jnp.broadcast_to(jnp.sum(dfg * (1.0 - sig), axis=0, keepdims=True), dlb_ref.shape)

    def seg(off):
        return pl.BlockSpec((CHUNK, HG_W), lambda c: (NC - 1 - c, off // HG_W))

    blk = pl.BlockSpec((CHUNK, HG_W), lambda c: (NC - 1 - c, 0))
    osd = jax.ShapeDtypeStruct((S, HG_W), BF16)
    pad = pltpu.VMEM((HG_PAD, HG_W), F32)
    full = pltpu.VMEM((CHUNK, HG_W), F32)
    dq, df, dv, dg, dlb, dnw = pl.pallas_call(
        body, name="hgrn_bwd", grid=(NC,),
        in_specs=[seg(OFF_Q), seg(OFF_F), seg(OFF_V), seg(OFF_G),
                  pl.BlockSpec((1, HG_W), lambda c: (0, 0)), pl.BlockSpec((1, LANE), lambda c: (0, 0)),
                  pl.BlockSpec((None, H, LANE, LANE), lambda c: (NC - 1 - c, 0, 0, 0)), blk, blk],
        out_specs=[blk, blk, blk, blk, pl.BlockSpec((8, HG_W), lambda c: (0, 0)), pl.BlockSpec((8, LANE), lambda c: (0, 0))],
        out_shape=[osd, osd, osd, osd, jax.ShapeDtypeStruct((8, HG_W), F32), jax.ShapeDtypeStruct((8, LANE), F32)],
        scratch_shapes=[pltpu.VMEM((H, LANE, LANE), F32), pad, pad, pad, pad, pad, full, full],
        compiler_params=_cp(("arbitrary",)),
    )(proj, proj, proj, proj, lb.reshape(1, HG_W), nw.reshape(1, LANE), states, o_pre, dout)
    return dq, df, dv, dg, dlb[0], dnw[0]


CV_PAD = 32
ROWS = 256


def _colblk(S, off):
    return pl.BlockSpec((S, LANE), lambda j: (0, off // LANE + j))


def cv_fwd(proj, w32, bias):
    S = proj.shape[0]
    nchunk = S // ROWS

    def body(a_ref, g_ref, w_ref, b_ref, o_ref, zpad):
        zpad[pl.ds(0, CV_PAD), :] = jnp.zeros((CV_PAD, LANE), F32)

        def glu(c, _):
            r0 = pl.multiple_of(c * ROWS, ROWS)
            zpad[pl.ds(CV_PAD + r0, ROWS), :] = a_ref[pl.ds(r0, ROWS), :] * _sigmoid(g_ref[pl.ds(r0, ROWS), :])
            return 0

        lax.fori_loop(0, nchunk, glu, 0)

        def conv(c, _):
            r0 = pl.multiple_of(c * ROWS, ROWS)
            acc = jnp.broadcast_to(b_ref[...], (ROWS, LANE))
            for j in range(CV_KERNEL):
                acc = acc + w_ref[pl.ds(j, 1), :] * zpad[pl.ds(r0 + (CV_PAD - CV_KERNEL + 1) + j, ROWS), :]
            o_ref[pl.ds(r0, ROWS), :] = acc
            return 0

        lax.fori_loop(0, nchunk, conv, 0)

    return pl.pallas_call(
        body, name="cv_fwd", grid=(BRANCH_W // LANE,),
        in_specs=[_colblk(S, OFF_CV), _colblk(S, OFF_CV + BRANCH_W),
                  pl.BlockSpec((32, LANE), lambda j: (0, j)), pl.BlockSpec((1, LANE), lambda j: (0, j))],
        out_specs=pl.BlockSpec((S, LANE), lambda j: (0, j)), out_shape=jax.ShapeDtypeStruct((S, BRANCH_W), F32),
        scratch_shapes=[pltpu.VMEM((CV_PAD + S, LANE), F32)],
        compiler_params=_cp(("parallel",)),
    )(proj, proj, w32, bias.reshape(1, BRANCH_W))


def cv_bwd(proj, w32, dzc):
    S = proj.shape[0]
    nchunk = S // ROWS

    def body(a_ref, g_ref, w_ref, dz_ref, da_ref, dg_ref, dw_ref, db_ref, zpad, dpad):
        zpad[pl.ds(0, CV_PAD), :] = jnp.zeros((CV_PAD, LANE), F32)
        dpad[pl.ds(S, CV_PAD), :] = jnp.zeros((CV_PAD, LANE), F32)
        dw_ref[...] = jnp.zeros_like(dw_ref)

        def glu(c, dsum):
            r0 = pl.multiple_of(c * ROWS, ROWS)
            zpad[pl.ds(CV_PAD + r0, ROWS), :] = a_ref[pl.ds(r0, ROWS), :] * _sigmoid(g_ref[pl.ds(r0, ROWS), :])
            d = dz_ref[pl.ds(r0, ROWS), :]
            dpad[pl.ds(r0, ROWS), :] = d
            return dsum + jnp.sum(d, axis=0, keepdims=True)

        dsum = lax.fori_loop(0, nchunk, glu, jnp.zeros((1, LANE), F32))
        db_ref[...] = jnp.broadcast_to(dsum, db_ref.shape)

        def conv(c, _):
            r0 = pl.multiple_of(c * ROWS, ROWS)
            d = dpad[pl.ds(r0, ROWS), :]
            acc = jnp.zeros((ROWS, LANE), F32)
            for j in range(CV_KERNEL):
                acc = acc + w_ref[pl.ds(j, 1), :] * dpad[pl.ds(r0 + (CV_KERNEL - 1) - j, ROWS), :]
                zs = zpad[pl.ds(r0 + (CV_PAD - CV_KERNEL + 1) + j, ROWS), :]
                dw_ref[pl.ds(j, 1), :] += jnp.sum(d * zs, axis=0, keepdims=True)
            a = a_ref[pl.ds(r0, ROWS), :]
            sg = _sigmoid(g_ref[pl.ds(r0, ROWS), :])
            da_ref[pl.ds(r0, ROWS), :] = (acc * sg).astype(da_ref.dtype)
            dg_ref[pl.ds(r0, ROWS), :] = (acc * a * sg * (1.0 - sg)).astype(dg_ref.dtype)
            return 0

        lax.fori_loop(0, nchunk, conv, 0)

    blk = pl.BlockSpec((S, LANE), lambda j: (0, j))
    da, dg, dw, db = pl.pallas_call(
        body, name="cv_bwd", grid=(BRANCH_W // LANE,),
        in_specs=[_colblk(S, OFF_CV), _colblk(S, OFF_CV + BRANCH_W), pl.BlockSpec((32, LANE), lambda j: (0, j)), blk],
        out_specs=[blk, blk, pl.BlockSpec((32, LANE), lambda j: (0, j)), pl.BlockSpec((8, LANE), lambda j: (0, j))],
        out_shape=[jax.ShapeDtypeStruct((S, BRANCH_W), BF16), jax.ShapeDtypeStruct((S, BRANCH_W), BF16),
                   jax.ShapeDtypeStruct((32, BRANCH_W), F32), jax.ShapeDtypeStruct((8, BRANCH_W), F32)],
        scratch_shapes=[pltpu.VMEM((CV_PAD + S, LANE), F32), pltpu.VMEM((S + CV_PAD, LANE), F32)],
        compiler_params=_cp(("parallel",)),
    )(proj, proj, w32, dzc)
    return da, dg, dw, db[0]


def ln_silu_fwd(z, w, b, tm=512):
    S, C = z.shape
    tm = _tile(S, tm)

    def body(z_ref, w_ref, b_ref, o_ref):
        zv = z_ref[...]
        mu = jnp.mean(zv, axis=-1, keepdims=True)
        zc = zv - mu
        rstd = lax.rsqrt(jnp.mean(zc * zc, axis=-1, keepdims=True) + EPS)
        y = zc * rstd * w_ref[...] + b_ref[...]
        o_ref[...] = (y * _sigmoid(y)).astype(o_ref.dtype)

    row = pl.BlockSpec((tm, C), lambda i: (i, 0))
    vec = pl.BlockSpec((1, C), lambda i: (0, 0))
    return pl.pallas_call(
        body, name="ln_silu_fwd", grid=(S // tm,), in_specs=[row, vec, vec], out_specs=row,
        out_shape=jax.ShapeDtypeStruct((S, C), BF16), compiler_params=_cp(("parallel",)),
    )(z, w.reshape(1, C), b.reshape(1, C))


def ln_silu_bwd(z, w, b, dout, tm=512):
    S, C = z.shape
    tm = _tile(S, tm)

    def body(z_ref, w_ref, b_ref, do_ref, dz_ref, dw_ref, db_ref):
        zv = z_ref[...]
        wv = w_ref[...]
        mu = jnp.mean(zv, axis=-1, keepdims=True)
        zc = zv - mu
        rstd = lax.rsqrt(jnp.mean(zc * zc, axis=-1, keepdims=True) + EPS)
        xh = zc * rstd
        y = xh * wv + b_ref[...]
        dy = do_ref[...].astype(F32) * _dsilu(y, _sigmoid(y))

        @pl.when(pl.program_id(0) == 0)
        def _():
            dw_ref[...] = jnp.zeros_like(dw_ref)
            db_ref[...] = jnp.zeros_like(db_ref)

        dw_ref[...] += jnp.sum(dy * xh, axis=0, keepdims=True)
        db_ref[...] += jnp.sum(dy, axis=0, keepdims=True)
        dxh = dy * wv
        dz_ref[...] = rstd * (dxh - jnp.mean(dxh, axis=-1, keepdims=True) - xh * jnp.mean(dxh * xh, axis=-1, keepdims=True))

    row = pl.BlockSpec((tm, C), lambda i: (i, 0))
    vec = pl.BlockSpec((1, C), lambda i: (0, 0))
    dz, dw, db = pl.pallas_call(
        body, name="ln_silu_bwd", grid=(S // tm,), in_specs=[row, vec, vec, row], out_specs=[row, vec, vec],
        out_shape=[jax.ShapeDtypeStruct((S, C), F32), jax.ShapeDtypeStruct((1, C), F32), jax.ShapeDtypeStruct((1, C), F32)],
        compiler_params=_cp(("arbitrary",)),
    )(z, w.reshape(1, C), b.reshape(1, C), dout)
    return dz, dw.reshape(C), db.reshape(C)


PL_PAD = 16


def _pool_counts(r0, win):
    t = r0 + lax.broadcasted_iota(I32, (ROWS, LANE), 0)
    return jnp.minimum(t + 1, win).astype(F32)


def pool_fwd(proj, wg, scale):
    S = proj.shape[0]
    nchunk = S // ROWS

    def body(u_ref, w_ref, s_ref, o_ref, upad):
        g = pl.program_id(0)
        upad[pl.ds(0, PL_PAD), :] = jnp.zeros((PL_PAD, LANE), F32)

        def fill(c, _):
            r0 = pl.multiple_of(c * ROWS, ROWS)
            upad[pl.ds(PL_PAD + r0, ROWS), :] = u_ref[pl.ds(r0, ROWS), :]
            return 0

        lax.fori_loop(0, nchunk, fill, 0)
        wb = w_ref[...].astype(BF16)
        for gi, win in enumerate(POOL_WINDOWS):
            @pl.when(g == gi)
            def _(win=win):
                def chunk(c, _):
                    r0 = pl.multiple_of(c * ROWS, ROWS)
                    u = upad[pl.ds(PL_PAD + r0, ROWS), :]
                    ws = u
                    for j in range(1, win):
                        ws = ws + upad[pl.ds(PL_PAD + r0 - j, ROWS), :]
                    pooled = ws / _pool_counts(r0, win) - u
                    o_ref[pl.ds(r0, ROWS), :] = (_dot(pooled.astype(BF16), wb, NN) * s_ref[...]).astype(o_ref.dtype)
                    return 0

                lax.fori_loop(0, nchunk, chunk, 0)

    return pl.pallas_call(
        body, name="pool_fwd", grid=(len(POOL_WINDOWS),),
        in_specs=[_colblk(S, OFF_PL), pl.BlockSpec((None, LANE, LANE), lambda j: (j, 0, 0)), pl.BlockSpec((1, LANE), lambda j: (0, j))],
        out_specs=pl.BlockSpec((S, LANE), lambda j: (0, j)), out_shape=jax.ShapeDtypeStruct((S, BRANCH_W), BF16),
        scratch_shapes=[pltpu.VMEM((PL_PAD + S, LANE), F32)],
        compiler_params=_cp(("parallel",)),
    )(proj, wg, scale.reshape(1, BRANCH_W))


def pool_bwd(proj, wg, scale, dy):
    S = proj.shape[0]
    nchunk = S // ROWS

    def body(u_ref, w_ref, s_ref, dy_ref, du_ref, dw_ref, ds_ref, upad, dpn, nd):
        g = pl.program_id(0)
        upad[pl.ds(0, PL_PAD), :] = jnp.zeros((PL_PAD, LANE), F32)
        dpn[pl.ds(S, PL_PAD), :] = jnp.zeros((PL_PAD, LANE), F32)

        def fill(c, _):
            r0 = pl.multiple_of(c * ROWS, ROWS)
            upad[pl.ds(PL_PAD + r0, ROWS), :] = u_ref[pl.ds(r0, ROWS), :]
            return 0

        lax.fori_loop(0, nchunk, fill, 0)
        wb = w_ref[...].astype(BF16)
        sv = s_ref[...]
        for gi, win in enumerate(POOL_WINDOWS):
            @pl.when(g == gi)
            def _(win=win):
                def chunk(c, carry):
                    dw, dsc = carry
                    r0 = pl.multiple_of(c * ROWS, ROWS)
                    u = upad[pl.ds(PL_PAD + r0, ROWS), :]
                    ws = u
                    for j in range(1, win):
                        ws = ws + upad[pl.ds(PL_PAD + r0 - j, ROWS), :]
                    cnt = _pool_counts(r0, win)
                    pooled = (ws / cnt - u).astype(BF16)
                    dyv = dy_ref[pl.ds(r0, ROWS), :].astype(F32)
                    dsc = dsc + jnp.sum(dyv * _dot(pooled, wb, NN), axis=0, keepdims=True)
                    dys = (dyv * sv).astype(BF16)
                    dw = dw + _dot(pooled, dys, TN)
                    dp = _dot(dys, wb, NT)
                    dpn[pl.ds(r0, ROWS), :] = dp / cnt
                    nd[pl.ds(r0, ROWS), :] = -dp
                    return dw, dsc

                dw, dsc = lax.fori_loop(0, nchunk, chunk, (jnp.zeros((LANE, LANE), F32), jnp.zeros((1, LANE), F32)))
                dw_ref[...] = dw
                ds_ref[...] = jnp.broadcast_to(dsc, ds_ref.shape)

                def spread(c, _):
                    r0 = pl.multiple_of(c * ROWS, ROWS)
                    acc = nd[pl.ds(r0, ROWS), :]
                    for j in range(win):
                        acc = acc + dpn[pl.ds(r0 + j, ROWS), :]
                    du_ref[pl.ds(r0, ROWS), :] = acc.astype(du_ref.dtype)
                    return 0

                lax.fori_loop(0, nchunk, spread, 0)

    blk = pl.BlockSpec((S, LANE), lambda j: (0, j))
    du, dw, ds = pl.pallas_call(
        body, name="pool_bwd", grid=(len(POOL_WINDOWS),),
        in_specs=[_colblk(S, OFF_PL), pl.BlockSpec((None, LANE, LANE), lambda j: (j, 0, 0)), pl.BlockSpec((1, LANE), lambda j: (0, j)), blk],
        out_specs=[blk, pl.BlockSpec((None, LANE, LANE), lambda j: (j, 0, 0)), pl.BlockSpec((8, LANE), lambda j: (0, j))],
        out_shape=[jax.ShapeDtypeStruct((S, BRANCH_W), BF16), jax.ShapeDtypeStruct((len(POOL_WINDOWS), LANE, LANE), F32),
                   jax.ShapeDtypeStruct((8, BRANCH_W), F32)],
        scratch_shapes=[pltpu.VMEM((PL_PAD + S, LANE), F32), pltpu.VMEM((S + PL_PAD, LANE), F32), pltpu.VMEM((S, LANE), F32)],
        compiler_params=_cp(("parallel",)),
    )(proj, wg, scale.reshape(1, BRANCH_W), dy)
    return du, dw, ds[0]


LR_PAD = 8
GELU_C = math.sqrt(2.0 / math.pi)
GELU_A = 0.044715


def _gelu(y):
    return 0.5 * y * (1.0 + jnp.tanh(GELU_C * (y + GELU_A * y * y * y)))


def _dgelu(y):
    t = jnp.tanh(GELU_C * (y + GELU_A * y * y * y))
    return 0.5 * (1.0 + t) + 0.5 * y * (1.0 - t * t) * GELU_C * (1.0 + 3.0 * GELU_A * y * y)


def _lru_gates(xpad, r0, cw_ref, cb, wa, ba, wx, bx, sp8):
    xc = jnp.broadcast_to(cb, (ROWS, LANE))
    for j in range(LRU_CONV):
        xc = xc + cw_ref[pl.ds(j, 1), :] * xpad[pl.ds(r0 + (LR_PAD - LRU_CONV + 1) + j, ROWS), :]
    xb = xc.astype(BF16)
    r = _sigmoid(_dot(xb, wa, NN) + ba)
    ig = _sigmoid(_dot(xb, wx, NN) + bx)
    la = -sp8 * r
    a = jnp.exp(la)
    s = jnp.sqrt(-jnp.tanh(la) * (a * a + 1.0))
    return xc, r, ig, a, s


def lru_fwd(proj, cw8, cb, wa_bd, ba, wx_bd, bx, sp8):
    S = proj.shape[0]
    nchunk = S // ROWS

    def body(x_ref, y_ref, cw_ref, cb_ref, wa_ref, ba_ref, wx_ref, bx_ref, sp_ref, o_ref, h_ref, xpad, a_s):
        xpad[pl.ds(0, LR_PAD), :] = jnp.zeros((LR_PAD, LANE), F32)

        def fill(c, _):
            r0 = pl.multiple_of(c * ROWS, ROWS)
            xpad[pl.ds(LR_PAD + r0, ROWS), :] = x_ref[pl.ds(r0, ROWS), :]
            return 0

        lax.fori_loop(0, nchunk, fill, 0)
        wa = wa_ref[...].astype(BF16)
        wx = wx_ref[...].astype(BF16)

        def gates(c, _):
            r0 = pl.multiple_of(c * ROWS, ROWS)
            xc, r, ig, a, s = _lru_gates(xpad, r0, cw_ref, cb_ref[...], wa, ba_ref[...], wx, bx_ref[...], sp_ref[...])
            a_s[pl.ds(r0, ROWS), :] = a
            h_ref[pl.ds(r0, ROWS), :] = s * (ig * xc)
            return 0

        lax.fori_loop(0, nchunk, gates, 0)

        r8 = lax.broadcasted_iota(I32, (8, LANE), 0)

        def scan(i, h):
            base = pl.multiple_of(i * 8, 8)
            a8 = a_s[pl.ds(base, 8), :]
            b8 = h_ref[pl.ds(base, 8), :]
            out = jnp.zeros((8, LANE), F32)
            for u in range(8):
                h = a8[u:u + 1, :] * h + b8[u:u + 1, :]
                out = jnp.where(r8 == u, h, out)
            h_ref[pl.ds(base, 8), :] = out
            return h

        lax.fori_loop(0, S // 8, scan, jnp.zeros((1, LANE), F32))

        def gate_out(c, _):
            r0 = pl.multiple_of(c * ROWS, ROWS)
            o_ref[pl.ds(r0, ROWS), :] = (h_ref[pl.ds(r0, ROWS), :] * _gelu(y_ref[pl.ds(r0, ROWS), :])).astype(o_ref.dtype)
            return 0

        lax.fori_loop(0, nchunk, gate_out, 0)

    vec = pl.BlockSpec((1, LANE), lambda j: (0, j))
    mat = pl.BlockSpec((None, LANE, LANE), lambda j: (j, 0, 0))
    blk = pl.BlockSpec((S, LANE), lambda j: (0, j))
    return pl.pallas_call(
        body, name="lru_fwd", grid=(BRANCH_W // LANE,),
        in_specs=[_colblk(S, OFF_LX), _colblk(S, OFF_LY), pl.BlockSpec((8, LANE), lambda j: (0, j)), vec, mat, vec, mat, vec, vec],
        out_specs=[blk, blk],
        out_shape=[jax.ShapeDtypeStruct((S, BRANCH_W), BF16), jax.ShapeDtypeStruct((S, BRANCH_W), F32)],
        scratch_shapes=[pltpu.VMEM((LR_PAD + S, LANE), F32), pltpu.VMEM((S, LANE), F32)],
        compiler_params=_cp(("parallel",)),
    )(proj, proj, cw8, cb.reshape(1, -1), wa_bd, ba.reshape(1, -1), wx_bd, bx.reshape(1, -1), sp8.reshape(1, -1))


def lru_bwd(proj, cw8, cb, wa_bd, ba, wx_bd, bx, sp8, h, dout):
    S = proj.shape[0]
    nchunk = S // ROWS

    def body(x_ref, y_ref, cw_ref, cb_ref, wa_ref, ba_ref, wx_ref, bx_ref, sp_ref, h_ref, do_ref,
             dx_ref, dy_ref, dcw_ref, dcb_ref, dwa_ref, dba_ref, dwx_ref, dbx_ref, dsp_ref,
             xpad, a_s, g_s, hpad, dxc):
        xpad[pl.ds(0, LR_PAD), :] = jnp.zeros((LR_PAD, LANE), F32)
        hpad[pl.ds(0, LR_PAD), :] = jnp.zeros((LR_PAD, LANE), F32)
        dxc[pl.ds(S, LR_PAD), :] = jnp.zeros((LR_PAD, LANE), F32)
        dcw_ref[...] = jnp.zeros_like(dcw_ref)
        wa = wa_ref[...].astype(BF16)
        wx = wx_ref[...].astype(BF16)
        cbv, bav, bxv, spv = cb_ref[...], ba_ref[...], bx_ref[...], sp_ref[...]

        def fill(c, _):
            r0 = pl.multiple_of(c * ROWS, ROWS)
            xpad[pl.ds(LR_PAD + r0, ROWS), :] = x_ref[pl.ds(r0, ROWS), :]
            hv = h_ref[pl.ds(r0, ROWS), :]
            hpad[pl.ds(LR_PAD + r0, ROWS), :] = hv
            yv = y_ref[pl.ds(r0, ROWS), :]
            dov = do_ref[pl.ds(r0, ROWS), :].astype(F32)
            g_s[pl.ds(r0, ROWS), :] = dov * _gelu(yv)
            dy_ref[pl.ds(r0, ROWS), :] = (dov * hv * _dgelu(yv)).astype(dy_ref.dtype)
            return 0

        lax.fori_loop(0, nchunk, fill, 0)

        def gates(c, _):
            r0 = pl.multiple_of(c * ROWS, ROWS)
            _, _, _, a, _ = _lru_gates(xpad, r0, cw_ref, cbv, wa, bav, wx, bxv, spv)
            a_s[pl.ds(r0, ROWS), :] = a
            return 0

        lax.fori_loop(0, nchunk, gates, 0)

        r8 = lax.broadcasted_iota(I32, (8, LANE), 0)

        def rscan(i, carry):
            base = pl.multiple_of(S - 8 - i * 8, 8)
            a8 = a_s[pl.ds(base, 8), :]
            g8 = g_s[pl.ds(base, 8), :]
            out = jnp.zeros((8, LANE), F32)
            for u in reversed(range(8)):
                gt = g8[u:u + 1, :] + carry
                out = jnp.where(r8 == u, gt, out)
                carry = a8[u:u + 1, :] * gt
            g_s[pl.ds(base, 8), :] = out
            return carry

        lax.fori_loop(0, S // 8, rscan, jnp.zeros((1, LANE), F32))

        def chain(c, carry):
            dwa, dwx, dba, dbx, dsp, dcb = carry
            r0 = pl.multiple_of(c * ROWS, ROWS)
            xc, r, ig, a, s = _lru_gates(xpad, r0, cw_ref, cbv, wa, bav, wx, bxv, spv)
            gt = g_s[pl.ds(r0, ROWS), :]
            hprev = hpad[pl.ds(r0 + LR_PAD - 1, ROWS), :]
            da = gt * hprev - gt * ig * xc * (a / s)
            dig = gt * s * xc
            dla = da * a
            dsp = dsp + jnp.sum(-dla * r, axis=0, keepdims=True)
            dpr = (-dla * spv) * r * (1.0 - r)
            dpi = dig * ig * (1.0 - ig)
            dprb, dpib, xb = dpr.astype(BF16), dpi.astype(BF16), xc.astype(BF16)
            d = gt * s * ig + _dot(dprb, wa, NT) + _dot(dpib, wx, NT)
            dwa = dwa + _dot(xb, dprb, TN)
            dwx = dwx + _dot(xb, dpib, TN)
            dba = dba + jnp.sum(dpr, axis=0, keepdims=True)
            dbx = dbx + jnp.sum(dpi, axis=0, keepdims=True)
            dcb = dcb + jnp.sum(d, axis=0, keepdims=True)
            dxc[pl.ds(r0, ROWS), :] = d
            for j in range(LRU_CONV):
                xs = xpad[pl.ds(r0 + (LR_PAD - LRU_CONV + 1) + j, ROWS), :]
                dcw_ref[pl.ds(j, 1), :] += jnp.sum(d * xs, axis=0, keepdims=True)
            return dwa, dwx, dba, dbx, dsp, dcb

        zm, zv = jnp.zeros((LANE, LANE), F32), jnp.zeros((1, LANE), F32)
        dwa, dwx, dba, dbx, dsp, dcb = lax.fori_loop(0, nchunk, chain, (zm, zm, zv, zv, zv, zv))
        dwa_ref[...] = dwa
        dwx_ref[...] = dwx
        dba_ref[...] = jnp.broadcast_to(dba, dba_ref.shape)
        dbx_ref[...] = jnp.broadcast_to(dbx, dbx_ref.shape)
        dsp_ref[...] = jnp.broadcast_to(dsp, dsp_ref.shape)
        dcb_ref[...] = jnp.broadcast_to(dcb, dcb_ref.shape)

        def convt(c, _):
            r0 = pl.multiple_of(c * ROWS, ROWS)
            acc = jnp.zeros((ROWS, LANE), F32)
            for j in range(LRU_CONV):
                acc = acc + cw_ref[pl.ds(j, 1), :] * dxc[pl.ds(r0 + (LRU_CONV - 1) - j, ROWS), :]
            dx_ref[pl.ds(r0, ROWS), :] = acc.astype(dx_ref.dtype)
            return 0

        lax.fori_loop(0, nchunk, convt, 0)

    vec = pl.BlockSpec((1, LANE), lambda j: (0, j))
    vec8 = pl.BlockSpec((8, LANE), lambda j: (0, j))
    mat = pl.BlockSpec((None, LANE, LANE), lambda j: (j, 0, 0))
    blk = pl.BlockSpec((S, LANE), lambda j: (0, j))
    nblk = BRANCH_W // LANE
    v8 = jax.ShapeDtypeStruct((8, BRANCH_W), F32)
    m4 = jax.ShapeDtypeStruct((nblk, LANE, LANE), F32)
    big = jax.ShapeDtypeStruct((S, BRANCH_W), BF16)
    seq = pltpu.VMEM((S, LANE), F32)
    dx, dy, dcw, dcb, dwa, dba, dwx, dbx, dsp = pl.pallas_call(
        body, name="lru_bwd", grid=(nblk,),
        in_specs=[_colblk(S, OFF_LX), _colblk(S, OFF_LY), vec8, vec, mat, vec, mat, vec, vec, blk, blk],
        out_specs=[blk, blk, vec8, vec8, mat, vec8, mat, vec8, vec8],
        out_shape=[big, big, v8, v8, m4, v8, m4, v8, v8],
        scratch_shapes=[pltpu.VMEM((LR_PAD + S, LANE), F32), seq, seq, pltpu.VMEM((LR_PAD + S, LANE), F32),
                        pltpu.VMEM((S + LR_PAD, LANE), F32)],
        compiler_params=_cp(("parallel",)),
    )(proj, proj, cw8, cb.reshape(1, -1), wa_bd, ba.reshape(1, -1), wx_bd, bx.reshape(1, -1), sp8.reshape(1, -1), h, dout)
    return dx, dy, dcw, dcb[0], dwa, dba[0], dwx, dbx[0], dsp[0]


MG_COLS = 512
N_BRANCH = 4


def _gate_spec(tm, k):
    return pl.BlockSpec((tm, MG_COLS), lambda j, i: (i, (OFF_GATE + k * D_MODEL) // MG_COLS + j))


def merge_fwd(ups, proj, gate_b, tm=256):
    S = proj.shape[0]
    tm = _tile(S, tm)

    def body(u0, u1, u2, u3, g0, g1, g2, g3, gb_ref, o_ref):
        acc = jnp.zeros((tm, MG_COLS), F32)
        for k, (u, g) in enumerate(((u0, g0), (u1, g1), (u2, g2), (u3, g3))):
            acc = acc + _sigmoid(g[...] + gb_ref[pl.ds(k, 1), :]) * u[...]
        o_ref[...] = acc.astype(o_ref.dtype)

    blk = pl.BlockSpec((tm, MG_COLS), lambda j, i: (i, j))
    return pl.pallas_call(
        body, name="merge_fwd", grid=(D_MODEL // MG_COLS, S // tm),
        in_specs=[blk] * N_BRANCH + [_gate_spec(tm, k) for k in range(N_BRANCH)] + [pl.BlockSpec((N_BRANCH, MG_COLS), lambda j, i: (0, j))],
        out_specs=blk, out_shape=jax.ShapeDtypeStruct((S, D_MODEL), BF16),
        compiler_params=_cp(("parallel", "parallel")),
    )(*ups, proj, proj, proj, proj, gate_b)


def merge_bwd(dmerged, ups, proj, gate_b, tm=256):
    S = proj.shape[0]
    tm = _tile(S, tm)

    def body(dm_ref, u0, u1, u2, u3, g0, g1, g2, g3, gb_ref, du0, du1, du2, du3, dg0, dg1, dg2, dg3, dgb_ref):
        @pl.when(pl.program_id(1) == 0)
        def _():
            dgb_ref[...] = jnp.zeros_like(dgb_ref)

        dm = dm_ref[...].astype(F32)
        for k, (u, g, du, dg) in enumerate(((u0, g0, du0, dg0), (u1, g1, du1, dg1), (u2, g2, du2, dg2), (u3, g3, du3, dg3))):
            sg = _sigmoid(g[...] + gb_ref[pl.ds(k, 1), :])
            du[...] = (dm * sg).astype(du.dtype)
            dgk = dm * u[...] * sg * (1.0 - sg)
            dg[...] = dgk.astype(dg.dtype)
            dgb_ref[pl.ds(8 * k, 8), :] += jnp.broadcast_to(jnp.sum(dgk, axis=0, keepdims=True), (8, MG_COLS))

    blk = pl.BlockSpec((tm, MG_COLS), lambda j, i: (i, j))
    big = jax.ShapeDtypeStruct((S, D_MODEL), BF16)
    outs = pl.pallas_call(
        body, name="merge_bwd", grid=(D_MODEL // MG_COLS, S // tm),
        in_specs=[blk] * (1 + N_BRANCH) + [_gate_spec(tm, k) for k in range(N_BRANCH)] + [pl.BlockSpec((N_BRANCH, MG_COLS), lambda j, i: (0, j))],
        out_specs=[blk] * (2 * N_BRANCH) + [pl.BlockSpec((8 * N_BRANCH, MG_COLS), lambda j, i: (0, j))],
        out_shape=[big] * (2 * N_BRANCH) + [jax.ShapeDtypeStruct((8 * N_BRANCH, D_MODEL), F32)],
        compiler_params=_cp(("parallel", "arbitrary")),
    )(dmerged, *ups, proj, proj, proj, proj, gate_b)
    return outs[:N_BRANCH], outs[N_BRANCH:2 * N_BRANCH], outs[-1].reshape(N_BRANCH, 8, D_MODEL)[:, 0]


def attn_fwd(q, kv, tm=512):
    S = q.shape[0]
    M = kv.shape[0]
    tm = _tile(S, tm)
    scale = XA_HD ** -0.5

    def body(q_ref, kv_ref, o_ref):
        for hh in range(XA_HEADS):
            cs = pl.ds(hh * XA_HD, XA_HD)
            qh = q_ref[:, cs]
            kh = kv_ref[:, cs]
            vh = kv_ref[:, pl.ds(D_MODEL + hh * XA_HD, XA_HD)]
            s = _dot(qh, kh, NT) * scale
            p = jnp.exp(s - jnp.max(s, axis=-1, keepdims=True))
            p = p / jnp.sum(p, axis=-1, keepdims=True)
            o_ref[:, cs] = _dot(p.astype(BF16), vh, NN).astype(o_ref.dtype)

    return pl.pallas_call(
        body, name="attn_fwd", grid=(S // tm,),
        in_specs=[pl.BlockSpec((tm, D_MODEL), lambda i: (i, 0)), pl.BlockSpec((M, 2 * D_MODEL), lambda i: (0, 0))],
        out_specs=pl.BlockSpec((tm, D_MODEL), lambda i: (i, 0)), out_shape=jax.ShapeDtypeStruct((S, D_MODEL), BF16),
        compiler_params=_cp(("parallel",)),
    )(q, kv)


def attn_bwd(q, kv, do, tm=512):
    S = q.shape[0]
    M = kv.shape[0]
    tm = _tile(S, tm)
    scale = XA_HD ** -0.5

    def body(q_ref, kv_ref, do_ref, dq_ref, dkv_ref):
        @pl.when(pl.program_id(0) == 0)
        def _():
            dkv_ref[...] = jnp.zeros_like(dkv_ref)

        for hh in range(XA_HEADS):
            cs = pl.ds(hh * XA_HD, XA_HD)
            vs = pl.ds(D_MODEL + hh * XA_HD, XA_HD)
            qh = q_ref[:, cs]
            kh = kv_ref[:, cs]
            vh = kv_ref[:, vs]
            doh = do_ref[:, cs]
            s = _dot(qh, kh, NT) * scale
            p = jnp.exp(s - jnp.max(s, axis=-1, keepdims=True))
            p = p / jnp.sum(p, axis=-1, keepdims=True)
            dp = _dot(doh, vh, NT)
            ds = (p * (dp - jnp.sum(dp * p, axis=-1, keepdims=True)) * scale).astype(BF16)
            dq_ref[:, cs] = _dot(ds, kh, NN).astype(dq_ref.dtype)
            dkv_ref[:, cs] += _dot(ds, qh, TN)
            dkv_ref[:, vs] += _dot(p.astype(BF16), doh, TN)

    row = pl.BlockSpec((tm, D_MODEL), lambda i: (i, 0))
    full = pl.BlockSpec((M, 2 * D_MODEL), lambda i: (0, 0))
    return pl.pallas_call(
        body, name="attn_bwd", grid=(S // tm,), in_specs=[row, full, row], out_specs=[row, full],
        out_shape=[jax.ShapeDtypeStruct((S, D_MODEL), BF16), jax.ShapeDtypeStruct((M, 2 * D_MODEL), F32)],
        compiler_params=_cp(("arbitrary",)),
    )(q, kv, do)


def sum_parts(parts, own=None, tm=256):
    n, R, C = parts.shape
    tm = _tile(R, tm)
    has_own = own is not None

    def body(*refs):
        p_ref, o_ref = refs[0], refs[-1]
        acc = refs[1][...].astype(F32) if has_own else p_ref[0].astype(F32)
        for j in range(0 if has_own else 1, n):
            acc = acc + p_ref[j].astype(F32)
        o_ref[...] = acc

    row = pl.BlockSpec((tm, C), lambda i: (i, 0))
    return pl.pallas_call(
        body, name="sum_parts", grid=(R // tm,),
        in_specs=[pl.BlockSpec((n, tm, C), lambda i: (0, i, 0))] + ([row] if has_own else []), out_specs=row,
        out_shape=jax.ShapeDtypeStruct((R, C), F32), compiler_params=_cp(("parallel",)),
    )(*([parts, own] if has_own else [parts]))


def adamw(w, g, m, v, tm=256):
    R, C = w.shape
    tm = _tile(R, tm)
    c1 = 1.0 / (1.0 - ADAM_B1 ** ADAM_STEP)
    c2 = 1.0 / (1.0 - ADAM_B2 ** ADAM_STEP)

    def body(w_ref, g_ref, m_ref, v_ref, d_ref, nm_ref, nv_ref):
        gv = g_ref[...]
        nm = ADAM_B1 * m_ref[...] + (1.0 - ADAM_B1) * gv
        nv = ADAM_B2 * v_ref[...] + (1.0 - ADAM_B2) * (gv * gv)
        nm_ref[...] = nm
        nv_ref[...] = nv
        d_ref[...] = -ADAM_LR * ((nm * c1) / (jnp.sqrt(nv * c2) + ADAM_EPS) + ADAM_WD * w_ref[...])

    blk = pl.BlockSpec((tm, C), lambda i: (i, 0))
    sd = jax.ShapeDtypeStruct((R, C), F32)
    return pl.pallas_call(
        body, name="adamw", grid=(R // tm,), in_specs=[blk] * 4, out_specs=[blk] * 3, out_shape=[sd] * 3,
        compiler_params=_cp(("parallel",)),
    )(w, g, m, v)


ANY = pl.BlockSpec(memory_space=pl.ANY)


def _place():
    return lax.axis_index("x"), lax.axis_index("y"), lax.axis_index("c")


def _slot(px, py, pc):
    return 4 * px + 2 * py + pc


def all_gather(name, shards, after=()):
    n = len(shards)
    n_in = n + len(after)

    def body(*refs):
        x_refs, out_refs = refs[:n], refs[n_in:n_in + n]
        send_sems, recv_sems, local_sems = refs[n_in + n:]
        x, y, c = _place()
        me, sibling = (x, y, c), (x, y, 1 - c)
        chips = [(1 - x, y), (x, 1 - y), (1 - x, 1 - y)]

        def copy(a, k, block, to, src=None):
            rows = out_refs[a].at[_slot(*block)]
            return pltpu.make_async_remote_copy(
                src_ref=rows if src is None else src, dst_ref=rows,
                send_sem=send_sems.at[7 * a + k], recv_sem=recv_sems.at[7 * a + k],
                device_id=to, device_id_type=MESH)

        mine = [pltpu.make_async_copy(x_refs[a], out_refs[a].at[_slot(*me)], local_sems.at[a]) for a in range(n)]
        for cp in mine:
            cp.start()
        first = []
        for a in range(n):
            first.append(copy(a, 0, me, sibling, src=x_refs[a]))
            first += [copy(a, 1 + j, me, (*chip, c), src=x_refs[a]) for j, chip in enumerate(chips)]
        for cp in first:
            cp.start()
        passed = []
        for a in range(n):
            for j, chip in enumerate(chips):
                copy(a, 1 + j, (*chip, c), me).wait_recv()
                cp = copy(a, 4 + j, (*chip, c), sibling)
                cp.start()
                passed.append(cp)
        for a in range(n):
            copy(a, 0, sibling, me).wait_recv()
            for j, chip in enumerate(chips):
                copy(a, 4 + j, (*chip, 1 - c), me).wait_recv()
        for cp in first + passed:
            cp.wait_send()
        for cp in mine:
            cp.wait()

    return pl.pallas_call(
        body, name=name, in_specs=[ANY] * n_in, out_specs=[ANY] * n,
        out_shape=[jax.ShapeDtypeStruct((N_DEV, *s.shape), s.dtype) for s in shards],
        scratch_shapes=[pltpu.SemaphoreType.DMA((7 * n,)), pltpu.SemaphoreType.DMA((7 * n,)), pltpu.SemaphoreType.DMA((n,))],
    )(*shards, *after)


HBM = pl.BlockSpec(memory_space=pltpu.HBM)
SEM = pl.BlockSpec(memory_space=pltpu.SEMAPHORE)
EFFECT = pltpu.SideEffectType.DATAFLOW_SIDE_EFFECTING
N_PEER = N_DEV - 1
RELATIONS = [(dx, dy, dc) for dx in (0, 1) for dy in (0, 1) for dc in (0, 1)][1:]


def _peer(place, rel):
    return tuple(1 - v if d else v for v, d in zip(place, rel))


def gather_start(name, shards, me, before):
    n = len(shards)

    def body(*refs):
        x_refs, land_refs = refs[:n], refs[n:2 * n]
        send_sems, recv_sems = refs[2 * n + len(before):2 * n + len(before) + 2]
        token = refs[-1]
        place = _place()
        mine = _slot(*place)
        for a in range(n):
            for rel in RELATIONS:
                pltpu.make_async_remote_copy(
                    src_ref=x_refs[a], dst_ref=land_refs[a].at[mine], send_sem=send_sems.at[a], recv_sem=recv_sems.at[a],
                    device_id=_peer(place, rel), device_id_type=MESH).start()
        token[...] = jnp.zeros_like(token)

    lands = [lax.dynamic_update_index_in_dim(lax.empty((N_DEV, *s.shape), s.dtype), s, me, 0) for s in shards]
    outs = pl.pallas_call(
        body, name=name,
        in_specs=[HBM] * (2 * n) + [ANY] * len(before),
        out_specs=[SEM, SEM] + [HBM] * (2 * n) + [pl.BlockSpec(memory_space=pltpu.VMEM)],
        out_shape=[pltpu.SemaphoreType.DMA((n,)), pltpu.SemaphoreType.DMA((n,))]
        + [pltpu.HBM(t.shape, t.dtype) for t in (*shards, *lands)] + [jax.ShapeDtypeStruct((8, LANE), F32)],
        input_output_aliases={i: 2 + i for i in range(2 * n)},
        compiler_params=pltpu.CompilerParams(has_side_effects=EFFECT),
    )(*[pltpu.with_memory_space_constraint(t, pltpu.HBM) for t in (*shards, *lands)], *before)
    return (outs[0], outs[1], outs[2:2 + n], outs[2 + n:2 + 2 * n]), outs[-1]


def gather_wait(name, state, after):
    send_sems, recv_sems, shards, lands = state
    n = len(shards)

    def body(*refs):
        land_refs = refs[n:2 * n]
        s_sems, r_sems = refs[2 * n:2 * n + 2]
        place = _place()
        for a in range(n):
            seven = land_refs[a].at[pl.ds(0, N_PEER)]
            cp = pltpu.make_async_remote_copy(
                src_ref=seven, dst_ref=seven, send_sem=s_sems.at[a], recv_sem=r_sems.at[a], device_id=place, device_id_type=MESH)
            cp.wait_send()
            cp.wait_recv()

    outs = pl.pallas_call(
        body, name=name,
        in_specs=[HBM] * (2 * n) + [SEM, SEM] + [ANY] * len(after), out_specs=[HBM] * (2 * n),
        out_shape=[pltpu.HBM(t.shape, t.dtype) for t in (*shards, *lands)],
        input_output_aliases={i: i for i in range(2 * n)},
        compiler_params=pltpu.CompilerParams(has_side_effects=EFFECT),
    )(*shards, *lands, send_sems, recv_sems, *after)
    return outs[n:]


def exchange_start(name, grads, before):
    n = len(grads)

    def body(*refs):
        g_refs, land_refs = refs[:n], refs[n:2 * n]
        send_sems, recv_sems = refs[2 * n + 1:2 * n + 3]
        token = refs[-1]
        place = _place()
        for a in range(n):
            for r, rel in enumerate(RELATIONS):
                p = _peer(place, rel)
                pltpu.make_async_remote_copy(
                    src_ref=g_refs[a].at[_slot(*p)], dst_ref=land_refs[a].at[r],
                    send_sem=send_sems.at[a], recv_sem=recv_sems.at[a], device_id=p, device_id_type=MESH).start()
        token[...] = jnp.zeros_like(token)

    lands = [lax.empty((N_PEER, *g.shape[1:]), g.dtype) for g in grads]
    outs = pl.pallas_call(
        body, name=name,
        in_specs=[HBM] * (2 * n) + [ANY],
        out_specs=[SEM, SEM] + [HBM] * (2 * n) + [pl.BlockSpec(memory_space=pltpu.VMEM)],
        out_shape=[pltpu.SemaphoreType.DMA((n,)), pltpu.SemaphoreType.DMA((n,))]
        + [pltpu.HBM(g.shape, g.dtype) for g in grads] + [pltpu.HBM(t.shape, t.dtype) for t in lands]
        + [jax.ShapeDtypeStruct((8, LANE), F32)],
        input_output_aliases={i: 2 + i for i in range(2 * n)},
        compiler_params=pltpu.CompilerParams(has_side_effects=EFFECT),
    )(*[pltpu.with_memory_space_constraint(t, pltpu.HBM) for t in (*grads, *lands)], before)
    return (outs[0], outs[1], outs[2:2 + n], outs[2 + n:2 + 2 * n]), outs[-1]


def exchange_wait(name, state, after):
    send_sems, recv_sems, grads, lands = state
    n = len(grads)

    def body(*refs):
        g_refs, land_refs = refs[:n], refs[n:2 * n]
        s_sems, r_sems = refs[2 * n:2 * n + 2]
        place = _place()
        for a in range(n):
            cp = pltpu.make_async_remote_copy(
                src_ref=g_refs[a].at[pl.ds(0, N_PEER)], dst_ref=land_refs[a],
                send_sem=s_sems.at[a], recv_sem=r_sems.at[a], device_id=place, device_id_type=MESH)
            cp.wait_send()
            cp.wait_recv()

    outs = pl.pallas_call(
        body, name=name,
        in_specs=[HBM] * (2 * n) + [SEM, SEM, ANY], out_specs=[HBM] * (2 * n),
        out_shape=[pltpu.HBM(t.shape, t.dtype) for t in (*grads, *lands)],
        input_output_aliases={i: i for i in range(2 * n)},
        compiler_params=pltpu.CompilerParams(has_side_effects=EFFECT),
    )(*grads, *lands, send_sems, recv_sems, after)
    return outs[:n], outs[n:]


WEIGHTS = ['norm_mix_w', 'w_in', 'hg_lb_raw', 'hg_norm_w', 'cv_dw_w', 'cv_dw_b', 'cv_ln_w', 'cv_ln_b', 'pl_w', 'pl_scale',
           'lru_conv_w', 'lru_conv_b', 'lru_wa', 'lru_ba', 'lru_wx', 'lru_bx', 'lru_lambda', 'gate_b', 'w_branch', 'w_out',
           'norm_mem_w', 'mem_norm_w', 'xa_wq', 'xa_wkv', 'xa_wo', 'norm_ffn_w', 'ffn_w1', 'ffn_w2', 'final_norm_w']
BIG = ('w_in', 'w_branch', 'w_out', 'xa_wq', 'xa_wkv', 'xa_wo', 'ffn_w1', 'ffn_w2')
SMALL_SHARDED = ('cv_dw_w', 'lru_conv_w', 'gate_b')
SMALL = tuple(n for n in WEIGHTS if n not in BIG and n not in SMALL_SHARDED)
PACK_ROWS = 256


def _pack(arrs):
    flat = jnp.concatenate([a.reshape(-1).astype(F32) for a in arrs])
    tile = PACK_ROWS * LANE
    padded = -(-flat.shape[0] // tile) * tile
    return jnp.pad(flat, (0, padded - flat.shape[0])).reshape(-1, LANE)


def _unpack(packed, shapes):
    flat = packed.reshape(-1)
    out, off = [], 0
    for s in shapes:
        n = math.prod(s)
        out.append(flat[off:off + n].reshape(s))
        off += n
    return out


def _gather_last(g, shard_shape):
    nd = len(shard_shape)
    full = jnp.moveaxis(g, 0, nd - 1)
    return full.reshape(*shard_shape[:-1], N_DEV * shard_shape[-1])


def _natural(blocks):
    nb, k, c = blocks.shape
    return jnp.transpose(blocks, (1, 0, 2)).reshape(k, nb * c)


def _blocked(mat):
    k, n = mat.shape
    return jnp.transpose(mat.reshape(k, N_DEV, n // N_DEV), (1, 0, 2))


def _block_diag(w):
    w2 = w.reshape(4, 2, 64, 64)
    z = jnp.zeros((4, 64, 64), w.dtype)
    return jnp.concatenate([jnp.concatenate([w2[:, 0], z], axis=2), jnp.concatenate([z, w2[:, 1]], axis=2)], axis=1)


def _block_diag_t(d):
    return jnp.stack([d[:, :64, :64], d[:, 64:, 64:]], axis=1).reshape(8, 64, 64)


def _lower_bounds(raw):
    lb = jnp.cumsum(jax.nn.softmax(raw.astype(F32), axis=0), axis=0)
    return lb - lb[0:1]


def _decay_rates(lam):
    return (LRU_C * jax.nn.softplus(-lam.astype(F32))).reshape(DEPTH, BRANCH_W)


def _relu2(acc):
    r = jnp.maximum(acc, 0.0)
    return acc, r * r


def _relu2_grad(acc, u):
    return (acc * 2.0 * jnp.maximum(u, 0.0),)


def _add(acc, e):
    return (acc + e,)


def _layer_fwd(x0, mem, p, g, rest):
    h1 = rms_fwd("rms_mix", x0, p['norm_mix_w'])
    proj = mm_nt("mm_in", h1, g['w_in'], tn=2176)[0]
    more, after = rest(proj)
    g = {**g, **more}
    b_hg, states, o_hg = hgrn_fwd(proj, p['lb'], p['hg_norm_w'], after=after)
    zc = cv_fwd(proj, p['cv_w32'], p['cv_dw_b'])
    b_cv = ln_silu_fwd(zc, p['cv_ln_w'], p['cv_ln_b'])
    b_pl = pool_fwd(proj, p['pl_w'], p['pl_scale'])
    b_lru, hst = lru_fwd(proj, p['lru_cw8'], p['lru_conv_b'], p['wa_bd'], p['lru_ba'], p['wx_bd'], p['lru_bx'], p['sp8'])
    branches = [b_hg, b_cv, b_pl, b_lru]
    ups = [mm_nn("mm_up", branches[k], g['w_branch'][k], tm=2048)[0] for k in range(N_BRANCH)]
    merged = merge_fwd(ups, proj, p['gate_b'])
    x1 = mm_nn("mm_out", merged, g['w_out'], epi=_add, extras=(x0,))[0]
    h2 = rms_fwd("rms_mem", x1, p['norm_mem_w'])
    q = mm_nn("mm_q", h2, g['xa_wq'], out_dtype=BF16)[0]
    memn = rms_fwd("rms_memtok", mem, p['mem_norm_w'])
    kv = mm_nn("mm_kv", memn, g['xa_wkv'], out_dtype=BF16, tn=2048)[0]
    oa = attn_fwd(q, kv)
    x2 = mm_nn("mm_o", oa, g['xa_wo'], epi=_add, extras=(x1,))[0]
    h3 = rms_fwd("rms_ffn", x2, p['norm_ffn_w'])
    u, act = mm_nn("mm_ffn1", h3, g['ffn_w1'], epi=_relu2, out_dtypes=[F32, BF16])
    x3 = mm_nn("mm_ffn2", act, g['ffn_w2'], epi=_add, extras=(x2,))[0]
    res = dict(x0=x0, h1=h1, proj=proj, states=states, o_hg=o_hg, zc=zc, hst=hst, branches=branches, ups=ups, merged=merged,
               x1=x1, h2=h2, q=q, memn=memn, kv=kv, oa=oa, x2=x2, h3=h3, u=u, act=act)
    return x3, res, g


def _layer_bwd(dx3, mem, p, g, r, after=(), midway=None):
    gs, gb = {}, {}
    du = mm_nt("mm_dffn2", dx3, g['ffn_w2'], out_dtype=BF16, epi=_relu2_grad, extras=(r['u'],), after=after)[0]
    gb['ffn_w2'] = mm_tn("mm_gw2", r['act'], dx3).reshape(N_DEV, -1, D_MODEL)
    gb['ffn_w1'] = mm_tn_cb("mm_gw1", r['h3'], du, N_DEV)
    dh3 = mm_nt("mm_dffn1", du, g['ffn_w1'], out_dtype=BF16)[0]
    dx2, gs['norm_ffn_w'] = rms_bwd("rmsb_ffn", r['x2'], p['norm_ffn_w'], dh3, dx3)
    doa = mm_nt("mm_do", dx2, g['xa_wo'], out_dtype=BF16)[0]
    gb['xa_wo'] = mm_tn("mm_gwo", r['oa'], dx2).reshape(N_DEV, -1, D_MODEL)
    dq, dkv = attn_bwd(r['q'], r['kv'], doa)
    gb['xa_wq'] = mm_tn("mm_gwq", r['h2'], dq).reshape(N_DEV, -1, D_MODEL)
    dh2 = mm_nt("mm_dq", dq, g['xa_wq'], out_dtype=BF16)[0]
    gb['xa_wkv'] = mm_tn_cb("mm_gwkv", r['memn'], dkv, N_DEV)
    dmemn = mm_nt("mm_dkv", dkv, g['xa_wkv'], out_dtype=BF16)[0]
    _, gs['mem_norm_w'] = rms_bwd("rmsb_memtok", mem, p['mem_norm_w'], dmemn)
    dx1, gs['norm_mem_w'] = rms_bwd("rmsb_mem", r['x1'], p['norm_mem_w'], dh2, dx2)
    after = midway(gb, dx1) if midway is not None else ()
    gb = {}
    dmerged = mm_nt("mm_dout", dx1, g['w_out'], out_dtype=BF16, after=after)[0]
    gb['w_out'] = mm_tn("mm_gwout", r['merged'], dx1).reshape(N_DEV, -1, D_MODEL)
    dups, dgates, gs['gate_b'] = merge_bwd(dmerged, r['ups'], r['proj'], p['gate_b'])
    for k in range(N_BRANCH):
        gb[f'w_branch{k}'] = _blocked(mm_tn("mm_gwb", r['branches'][k], dups[k], tk=2048))
    db = [mm_nt("mm_dup", dups[k], g['w_branch'][k], out_dtype=BF16, tm=2048)[0] for k in range(N_BRANCH)]
    dq_, df_, dv_, dg_, gs['lb'], gs['hg_norm_w'] = hgrn_bwd(r['proj'], p['lb'], p['hg_norm_w'], r['states'], r['o_hg'], db[0])
    dzc, gs['cv_ln_w'], gs['cv_ln_b'] = ln_silu_bwd(r['zc'], p['cv_ln_w'], p['cv_ln_b'], db[1])
    dca, dcg, dcw, gs['cv_dw_b'] = cv_bwd(r['proj'], p['cv_w32'], dzc)
    gs['cv_dw_w'] = dcw[:CV_KERNEL]
    dpu, gs['pl_w'], gs['pl_scale'] = pool_bwd(r['proj'], p['pl_w'], p['pl_scale'], db[2])
    dlx, dly, dlcw, gs['lru_conv_b'], dwa, gs['lru_ba'], dwx, gs['lru_bx'], gs['sp8'] = lru_bwd(
        r['proj'], p['lru_cw8'], p['lru_conv_b'], p['wa_bd'], p['lru_ba'], p['wx_bd'], p['lru_bx'], p['sp8'], r['hst'], db[3])
    gs['lru_conv_w'] = dlcw[:LRU_CONV]
    gs['lru_wa'], gs['lru_wx'] = _block_diag_t(dwa), _block_diag_t(dwx)
    gs['lru_ba'], gs['lru_bx'] = gs['lru_ba'].reshape(8, 64), gs['lru_bx'].reshape(8, 64)
    dproj = jnp.concatenate([dq_, df_, dv_, dg_, dca, dcg, dpu, dlx, dly, *dgates], axis=1)
    gb['w_in'] = mm_tn("mm_gwin", dproj, r['h1'], tm=2176).reshape(N_DEV, -1, D_MODEL)
    dh1 = mm_nn("mm_din", dproj, g['w_in'], out_dtype=BF16, tk=2176)[0]
    dx0, gs['norm_mix_w'] = rms_bwd("rmsb_mix", r['x0'], p['norm_mix_w'], dh1, dx1)
    return dx0, gs, gb


def kernel(x, mem, norm_mix_w, w_in, hg_lb_raw, hg_norm_w, cv_dw_w, cv_dw_b, cv_ln_w, cv_ln_b, pl_w, pl_scale, lru_conv_w, lru_conv_b, lru_wa, lru_ba, lru_wx, lru_bx, lru_lambda, gate_b, w_branch, w_out, norm_mem_w, mem_norm_w, xa_wq, xa_wkv, xa_wo, norm_ffn_w, ffn_w1, ffn_w2, final_norm_w, loss_target, m_norm_mix_w, m_w_in, m_hg_lb_raw, m_hg_norm_w, m_cv_dw_w, m_cv_dw_b, m_cv_ln_w, m_cv_ln_b, m_pl_w, m_pl_scale, m_lru_conv_w, m_lru_conv_b, m_lru_wa, m_lru_ba, m_lru_wx, m_lru_bx, m_lru_lambda, m_gate_b, m_w_branch, m_w_out, m_norm_mem_w, m_mem_norm_w, m_xa_wq, m_xa_wkv, m_xa_wo, m_norm_ffn_w, m_ffn_w1, m_ffn_w2, m_final_norm_w, v_norm_mix_w, v_w_in, v_hg_lb_raw, v_hg_norm_w, v_cv_dw_w, v_cv_dw_b, v_cv_ln_w, v_cv_ln_b, v_pl_w, v_pl_scale, v_lru_conv_w, v_lru_conv_b, v_lru_wa, v_lru_ba, v_lru_wx, v_lru_bx, v_lru_lambda, v_gate_b, v_w_branch, v_w_out, v_norm_mem_w, v_mem_norm_w, v_xa_wq, v_xa_wkv, v_xa_wo, v_norm_ffn_w, v_ffn_w1, v_ffn_w2, v_final_norm_w):
    W = dict(zip(WEIGHTS, (norm_mix_w, w_in, hg_lb_raw, hg_norm_w, cv_dw_w, cv_dw_b, cv_ln_w, cv_ln_b, pl_w, pl_scale, lru_conv_w, lru_conv_b, lru_wa, lru_ba, lru_wx, lru_bx, lru_lambda, gate_b, w_branch, w_out, norm_mem_w, mem_norm_w, xa_wq, xa_wkv, xa_wo, norm_ffn_w, ffn_w1, ffn_w2, final_norm_w)))
    Mo = dict(zip(WEIGHTS, (m_norm_mix_w, m_w_in, m_hg_lb_raw, m_hg_norm_w, m_cv_dw_w, m_cv_dw_b, m_cv_ln_w, m_cv_ln_b, m_pl_w, m_pl_scale, m_lru_conv_w, m_lru_conv_b, m_lru_wa, m_lru_ba, m_lru_wx, m_lru_bx, m_lru_lambda, m_gate_b, m_w_branch, m_w_out, m_norm_mem_w, m_mem_norm_w, m_xa_wq, m_xa_wkv, m_xa_wo, m_norm_ffn_w, m_ffn_w1, m_ffn_w2, m_final_norm_w)))
    Vo = dict(zip(WEIGHTS, (v_norm_mix_w, v_w_in, v_hg_lb_raw, v_hg_norm_w, v_cv_dw_w, v_cv_dw_b, v_cv_ln_w, v_cv_ln_b, v_pl_w, v_pl_scale, v_lru_conv_w, v_lru_conv_b, v_lru_wa, v_lru_ba, v_lru_wx, v_lru_bx, v_lru_lambda, v_gate_b, v_w_branch, v_w_out, v_norm_mem_w, v_mem_norm_w, v_xa_wq, v_xa_wkv, v_xa_wo, v_norm_ffn_w, v_ffn_w1, v_ffn_w2, v_final_norm_w)))
    me = _slot(*_place())
    xs, mems, target = x[0], mem[0], loss_target[0]

    shard_shapes = [W[n].shape for n in SMALL_SHARDED]
    gathered = all_gather("ag_small", [_pack([W[n] for n in SMALL_SHARDED])])[0]
    parts = [jnp.stack(ps) for ps in zip(*[_unpack(gathered[d], shard_shapes) for d in range(N_DEV)])]
    full_small = {n: _gather_last(parts[i], shard_shapes[i]) for i, n in enumerate(SMALL_SHARDED)}
    lb_all, lb_vjp = jax.vjp(_lower_bounds, hg_lb_raw)
    sp8_all, sp8_vjp = jax.vjp(_decay_rates, lru_lambda)

    def layer_params(l):
        p = {n: W[n][l] for n in SMALL if n != 'final_norm_w'}
        p['lb'] = lb_all[l]
        p['sp8'] = sp8_all[l]
        p['cv_w32'] = jnp.pad(full_small['cv_dw_w'][l], ((0, 32 - CV_KERNEL), (0, 0)))
        p['lru_cw8'] = jnp.pad(full_small['lru_conv_w'][l], ((0, 8 - LRU_CONV), (0, 0)))
        p['gate_b'] = full_small['gate_b'][l]
        p['wa_bd'], p['wx_bd'] = _block_diag(lru_wa[l]), _block_diag(lru_wx[l])
        p['lru_ba'], p['lru_bx'] = lru_ba[l].reshape(-1), lru_bx[l].reshape(-1)
        return p

    def start_gather(l, before):
        first = [jnp.transpose(w_in[l]).astype(BF16)]
        others = [w_branch[l, k].astype(BF16) for k in range(N_BRANCH)]
        others += [w[l].astype(BF16) for w in (w_out, xa_wq, xa_wkv, xa_wo, ffn_w1, ffn_w2)]
        state_a, tok_a = gather_start(f"ag_start{l}a", first, me, before)
        state_b, tok_b = gather_start(f"ag_start{l}b", others, me, (*before, tok_a))
        return state_a, state_b, (tok_a, tok_b)

    def finish_first(l, state, after):
        o = gather_wait(f"ag_wait{l}a", state, after)
        return dict(w_in=o[0].reshape(IN_W, D_MODEL))

    def finish_others(l, state, after):
        o = gather_wait(f"ag_wait{l}b", state, after)
        return dict(w_branch=[_natural(t) for t in o[0:4]], w_out=o[4].reshape(D_MODEL, D_MODEL),
                    xa_wq=o[5].reshape(D_MODEL, D_MODEL), xa_wkv=_natural(o[6]), xa_wo=o[7].reshape(D_MODEL, D_MODEL),
                    ffn_w1=_natural(o[8]), ffn_w2=o[9].reshape(D_FF, D_MODEL))

    params = [layer_params(l) for l in range(DEPTH)]
    mats, residuals = [], []
    xc = xs
    gathers = {0: start_gather(0, ())}
    for l in range(DEPTH):
        state_a, state_b, _ = gathers.pop(l)

        def rest(proj, l=l, state_b=state_b):
            more = finish_others(l, state_b, (proj,))
            if l + 1 == DEPTH:
                return more, ()
            gathers[l + 1] = start_gather(l + 1, (more['w_out'],))
            return more, gathers[l + 1][2]

        xc, res, g = _layer_fwd(xc, mems, params[l], finish_first(l, state_a, (xc,)), rest)
        mats.append(g)
        residuals.append(res)
    loss_part, dx, g_final = loss_head(xc, final_norm_w, target)
    loss = lax.psum(loss_part, ("x", "y", "c"))

    small_grads = [None] * DEPTH
    big_grads = [{} for _ in range(DEPTH)]
    pending = []

    def send(l, group, blocks, before):
        names = list(blocks)
        state, tok = exchange_start(f"rs_start{l}{group}", [blocks[n] for n in names], before)
        pending.append((l, group, names, state))
        return (tok,)

    def land(after):
        l, group, names, state = pending.pop(0)
        sent, landed = exchange_wait(f"rs_wait{l}{group}", state, after)
        for n, s, t in zip(names, sent, landed):
            own = lax.dynamic_index_in_dim(s, me, 0, keepdims=False).reshape(-1, s.shape[-1])
            big_grads[l][n] = sum_parts(t.reshape(N_PEER, -1, t.shape[-1]), own).reshape(t.shape[1:])

    token = ()
    for l in reversed(range(DEPTH)):
        dx, gs, gb = _layer_bwd(dx, mems, params[l], mats[l], residuals[l], after=token,
                                midway=lambda top, dx1, l=l: send(l, "a", top, dx1))
        small_grads[l] = gs
        while pending[0][0] > l:
            land(dx)
        token = send(l, "b", gb, dx)

    def stacked(n):
        return jnp.stack([small_grads[l][n] for l in range(DEPTH)])

    part = {n: stacked(n) for n in SMALL if n not in ('final_norm_w', 'hg_lb_raw', 'lru_lambda')}
    part['final_norm_w'] = g_final
    part['hg_lb_raw'] = lb_vjp(stacked('lb'))[0]
    part['lru_lambda'] = sp8_vjp(stacked('sp8'))[0]
    for n in SMALL_SHARDED:
        part[n] = stacked(n)
    names = list(SMALL) + list(SMALL_SHARDED)
    full_shapes = [part[n].shape for n in names]
    everyone = all_gather("ag_grads", [_pack([part[n] for n in names])], after=token)[0]
    total = sum_parts(everyone)
    while pending:
        land(total)

    G = {}
    G['w_in'] = jnp.stack([jnp.transpose(big_grads[l]['w_in']) for l in range(DEPTH)])
    G['w_branch'] = jnp.stack([jnp.stack([big_grads[l][f'w_branch{k}'] for k in range(N_BRANCH)]) for l in range(DEPTH)])
    for n in ('w_out', 'xa_wq', 'xa_wkv', 'xa_wo', 'ffn_w1', 'ffn_w2'):
        G[n] = jnp.stack([big_grads[l][n] for l in range(DEPTH)])
    for n, t in zip(names, _unpack(total, full_shapes)):
        if n in SMALL_SHARDED:
            c = t.shape[-1] // N_DEV
            t = lax.dynamic_slice_in_dim(t, me * c, c, axis=t.ndim - 1)
        G[n] = t

    delta, new_m, new_v = {}, {}, {}
    for n in BIG:
        c = W[n].shape[-1]
        d, nm, nv = adamw(W[n].reshape(-1, c), G[n].reshape(-1, c), Mo[n].reshape(-1, c), Vo[n].reshape(-1, c))
        delta[n], new_m[n], new_v[n] = d.reshape(W[n].shape), nm.reshape(W[n].shape), nv.reshape(W[n].shape)
    shapes = [W[n].shape for n in names]
    d, nm, nv = adamw(_pack([W[n] for n in names]), _pack([G[n] for n in names]), _pack([Mo[n] for n in names]), _pack([Vo[n] for n in names]))
    for n, a, b, c in zip(names, _unpack(d, shapes), _unpack(nm, shapes), _unpack(nv, shapes)):
        delta[n], new_m[n], new_v[n] = a, b, c
    return (loss, dx[None], *[G[n] for n in WEIGHTS], *[delta[n] for n in WEIGHTS],
            *[new_m[n] for n in WEIGHTS], *[new_v[n] for n in WEIGHTS])
```

```python
import functools
import math

import jax
import jax.numpy as jnp
from jax import lax
from jax.experimental import pallas as pl
from jax.experimental.pallas import tpu as pltpu

F32 = jnp.float32
BF16 = jnp.bfloat16
I32 = jnp.int32

N_DEV = 8
D_MODEL = 1024
DEPTH = 4
CHUNK = 64
EPS = 1e-6
HG_HEADS = 4
BRANCH_W = 512
CV_KERNEL = 31
POOL_WINDOWS = (2, 4, 8, 16)
LRU_CONV = 4
LRU_C = 8.0
XA_HEADS = 4
XA_HD = D_MODEL // XA_HEADS
D_FF = 4 * D_MODEL
IN_W = 8704
OFF_Q, OFF_F, OFF_V, OFF_G, OFF_CV, OFF_PL, OFF_LX, OFF_LY, OFF_GATE = 0, 512, 1024, 1536, 2048, 3072, 3584, 4096, 4608
LANE = 128
ADAM_LR, ADAM_B1, ADAM_B2, ADAM_EPS, ADAM_WD, ADAM_STEP = 0.001, 0.9, 0.999, 1e-08, 0.01, 10
VMEM_LIMIT = 56 * 1024 * 1024
MESH = pl.DeviceIdType.MESH
NEG = -1e30
ANY_SPACE = pl.BlockSpec(memory_space=pl.ANY)


def _cp(sem, **kw):
    return pltpu.CompilerParams(dimension_semantics=sem, vmem_limit_bytes=VMEM_LIMIT, **kw)


def _sigmoid(x):
    return 1.0 / (1.0 + jnp.exp(-x))


def _dsilu(x, s):
    return s * (1.0 + x * (1.0 - s))


def _dot(a, b, cdims, precision=None):
    return lax.dot_general(a, b, (cdims, ((), ())), preferred_element_type=F32, precision=precision)


NN = ((1,), (0,))
NT = ((1,), (1,))
TN = ((0,), (0,))


def _mm(name, a, b, *, grid, a_spec, b_spec, o_specs, out_shapes, acc_shape, cdims, epi=None, extras=(), extra_specs=(), after=()):
    nk = grid[2]
    n_e, n_o = len(extras), len(out_shapes)
    extras = (*extras, *after)
    extra_specs = (*extra_specs, *[ANY_SPACE] * len(after))

    def body(*refs):
        a_ref, b_ref = refs[0], refs[1]
        e_refs = refs[2:2 + n_e]
        o_refs = refs[2 + len(extras):2 + len(extras) + n_o]

        def finish(acc):
            vals = epi(acc, *[r[...] for r in e_refs]) if epi is not None else (acc,)
            for r, v in zip(o_refs, vals):
                r[...] = v.astype(r.dtype)

        part = _dot(a_ref[...].astype(BF16), b_ref[...].astype(BF16), cdims)
        if nk == 1:
            finish(part)
        else:
            acc_ref = refs[-1]
            k = pl.program_id(2)

            @pl.when(k == 0)
            def _():
                acc_ref[...] = part

            @pl.when(k > 0)
            def _():
                acc_ref[...] += part

            @pl.when(k == nk - 1)
            def _():
                finish(acc_ref[...])

    return pl.pallas_call(
        body, name=name, grid=grid,
        in_specs=[a_spec, b_spec, *extra_specs], out_specs=list(o_specs), out_shape=list(out_shapes),
        scratch_shapes=[] if nk == 1 else [pltpu.VMEM(acc_shape, F32)],
        compiler_params=_cp(("parallel", "parallel", "arbitrary")),
    )(a, b, *extras)


def _tile(n, pref):
    t = min(n, pref)
    while n % t:
        t //= 2
    return t


def mm_nt(name, a, b, out_dtype=F32, epi=None, extras=(), n_out=1, out_dtypes=None, tm=1024, tn=1024, tk=2048, after=()):
    M, K = a.shape
    N = b.shape[0]
    tm, tn, tk = _tile(M, tm), _tile(N, tn), _tile(K, tk)
    odt = out_dtypes or [out_dtype] * n_out
    o_spec = pl.BlockSpec((tm, tn), lambda i, j, k: (i, j))
    return _mm(name, a, b, grid=(M // tm, N // tn, K // tk),
               a_spec=pl.BlockSpec((tm, tk), lambda i, j, k: (i, k)),
               b_spec=pl.BlockSpec((tn, tk), lambda i, j, k: (j, k)),
               o_specs=[o_spec] * len(odt), out_shapes=[jax.ShapeDtypeStruct((M, N), d) for d in odt],
               acc_shape=(tm, tn), cdims=NT, epi=epi, extras=extras, extra_specs=[o_spec] * len(extras), after=after)


def mm_nn(name, a, b, out_dtype=F32, epi=None, extras=(), n_out=1, out_dtypes=None, tm=1024, tn=1024, tk=2048):
    M, K = a.shape
    N = b.shape[1]
    tm, tn, tk = _tile(M, tm), _tile(N, tn), _tile(K, tk)
    odt = out_dtypes or [out_dtype] * n_out
    o_spec = pl.BlockSpec((tm, tn), lambda i, j, k: (i, j))
    return _mm(name, a, b, grid=(M // tm, N // tn, K // tk),
               a_spec=pl.BlockSpec((tm, tk), lambda i, j, k: (i, k)),
               b_spec=pl.BlockSpec((tk, tn), lambda i, j, k: (k, j)),
               o_specs=[o_spec] * len(odt), out_shapes=[jax.ShapeDtypeStruct((M, N), d) for d in odt],
               acc_shape=(tm, tn), cdims=NN, epi=epi, extras=extras, extra_specs=[o_spec] * len(extras))


def mm_tn(name, a, b, out_dtype=BF16, tm=1024, tn=1024, tk=1024):
    K, M = a.shape
    N = b.shape[1]
    tm, tn, tk = _tile(M, tm), _tile(N, tn), _tile(K, tk)
    return _mm(name, a, b, grid=(M // tm, N // tn, K // tk),
               a_spec=pl.BlockSpec((tk, tm), lambda i, j, k: (k, i)),
               b_spec=pl.BlockSpec((tk, tn), lambda i, j, k: (k, j)),
               o_specs=[pl.BlockSpec((tm, tn), lambda i, j, k: (i, j))],
               out_shapes=[jax.ShapeDtypeStruct((M, N), out_dtype)], acc_shape=(tm, tn), cdims=TN)[0]


def mm_nn_cb(name, a, b, out_dtype=F32, epi=None, out_dtypes=None, tm=1024):
    M, K = a.shape
    nb, _, c = b.shape
    tm = _tile(M, tm)
    odt = out_dtypes or [out_dtype]
    return _mm(name, a, b, grid=(M // tm, nb, 1),
               a_spec=pl.BlockSpec((tm, K), lambda i, j, k: (i, 0)),
               b_spec=pl.BlockSpec((None, K, c), lambda i, j, k: (j, 0, 0)),
               o_specs=[pl.BlockSpec((tm, c), lambda i, j, k: (i, j))] * len(odt),
               out_shapes=[jax.ShapeDtypeStruct((M, nb * c), d) for d in odt], acc_shape=(tm, c), cdims=NN, epi=epi)


def mm_nt_cb(name, a, b, out_dtype=F32, epi=None, extras=(), tm=1024, tn=512):
    M = a.shape[0]
    nb, K, c = b.shape
    tm, tn = _tile(M, tm), _tile(K, tn)
    o_spec = pl.BlockSpec((tm, tn), lambda i, j, k: (i, j))
    return _mm(name, a, b, grid=(M // tm, K // tn, nb),
               a_spec=pl.BlockSpec((tm, c), lambda i, j, k: (i, k)),
               b_spec=pl.BlockSpec((None, tn, c), lambda i, j, k: (k, j, 0)),
               o_specs=[o_spec], out_shapes=[jax.ShapeDtypeStruct((M, K), out_dtype)],
               acc_shape=(tm, tn), cdims=NT, epi=epi, extras=extras, extra_specs=[o_spec] * len(extras))


def mm_tn_cb(name, a, b, nb, out_dtype=BF16, tm=1024, tk=2048):
    K, M = a.shape
    N = b.shape[1]
    c = N // nb
    tm, tk = _tile(M, tm), _tile(K, tk)
    return _mm(name, a, b, grid=(M // tm, nb, K // tk),
               a_spec=pl.BlockSpec((tk, tm), lambda i, j, k: (k, i)),
               b_spec=pl.BlockSpec((tk, c), lambda i, j, k: (k, j)),
               o_specs=[pl.BlockSpec((None, tm, c), lambda i, j, k: (j, i, 0))],
               out_shapes=[jax.ShapeDtypeStruct((nb, M, c), out_dtype)], acc_shape=(tm, c), cdims=TN)[0]


def rms_fwd(name, x, w, out_dtype=BF16, tm=512, after=()):
    S, D = x.shape
    tm = _tile(S, tm)

    def body(x_ref, w_ref, *rest):
        o_ref = rest[-1]
        xv = x_ref[...]
        r = lax.rsqrt(jnp.mean(xv * xv, axis=-1, keepdims=True) + EPS)
        o_ref[...] = (xv * r * w_ref[...]).astype(o_ref.dtype)

    return pl.pallas_call(
        body, name=name, grid=(S // tm,),
        in_specs=[pl.BlockSpec((tm, D), lambda i: (i, 0)), pl.BlockSpec((1, D), lambda i: (0, 0))] + [ANY_SPACE] * len(after),
        out_specs=pl.BlockSpec((tm, D), lambda i: (i, 0)), out_shape=jax.ShapeDtypeStruct((S, D), out_dtype),
        compiler_params=_cp(("parallel",)),
    )(x, w.reshape(1, D), *after)


def rms_bwd(name, x, w, dh, dres=None, tm=512):
    S, D = x.shape
    tm = _tile(S, tm)
    has_res = dres is not None

    def body(*refs):
        if has_res:
            x_ref, w_ref, dh_ref, dres_ref, dx_ref, dw_ref = refs
        else:
            x_ref, w_ref, dh_ref, dx_ref, dw_ref = refs
        xv = x_ref[...]
        dhv = dh_ref[...].astype(F32)
        r = lax.rsqrt(jnp.mean(xv * xv, axis=-1, keepdims=True) + EPS)
        g = dhv * w_ref[...]
        dx = r * g - xv * (r * r * r) * jnp.mean(xv * g, axis=-1, keepdims=True)
        if has_res:
            dx = dx + dres_ref[...]
        dx_ref[...] = dx

        @pl.when(pl.program_id(0) == 0)
        def _():
            dw_ref[...] = jnp.zeros_like(dw_ref)

        dw_ref[...] += jnp.sum(dhv * xv * r, axis=0, keepdims=True)

    row = pl.BlockSpec((tm, D), lambda i: (i, 0))
    vec = pl.BlockSpec((1, D), lambda i: (0, 0))
    args = [x, w.reshape(1, D), dh] + ([dres] if has_res else [])
    dx, dw = pl.pallas_call(
        body, name=name, grid=(S // tm,),
        in_specs=[row, vec, row] + ([row] if has_res else []),
        out_specs=[row, vec], out_shape=[jax.ShapeDtypeStruct((S, D), F32), jax.ShapeDtypeStruct((1, D), F32)],
        compiler_params=_cp(("arbitrary",)),
    )(*args)
    return dx, dw.reshape(D)


def loss_head(x, w, target, tm=512):
    S, D = x.shape
    tm = _tile(S, tm)

    def body(x_ref, w_ref, t_ref, loss_ref, dx_ref, dw_ref):
        xv = x_ref[...]
        wv = w_ref[...]
        r = lax.rsqrt(jnp.mean(xv * xv, axis=-1, keepdims=True) + EPS)
        y = xv * r * wv
        err = y - t_ref[...]
        dy = err * (1.0 / D)
        g = dy * wv
        dx_ref[...] = r * g - xv * (r * r * r) * jnp.mean(xv * g, axis=-1, keepdims=True)

        @pl.when(pl.program_id(0) == 0)
        def _():
            dw_ref[...] = jnp.zeros_like(dw_ref)
            loss_ref[...] = jnp.zeros_like(loss_ref)

        dw_ref[...] += jnp.sum(dy * xv * r, axis=0, keepdims=True)
        part = 0.5 * jnp.sum(jnp.mean(err * err, axis=-1, keepdims=True), axis=0, keepdims=True)
        loss_ref[...] += jnp.broadcast_to(part, loss_ref.shape)

    row = pl.BlockSpec((tm, D), lambda i: (i, 0))
    vec = pl.BlockSpec((1, D), lambda i: (0, 0))
    loss, dx, dw = pl.pallas_call(
        body, name="loss_head", grid=(S // tm,),
        in_specs=[row, vec, row],
        out_specs=[pl.BlockSpec((1, LANE), lambda i: (0, 0)), row, vec],
        out_shape=[jax.ShapeDtypeStruct((1, LANE), F32), jax.ShapeDtypeStruct((S, D), F32), jax.ShapeDtypeStruct((1, D), F32)],
        compiler_params=_cp(("arbitrary",)),
    )(x, w.reshape(1, D), target)
    return loss[0, 0], dx, dw.reshape(D)


SUB = 16
HG_W = HG_HEADS * LANE
HG_PAD = CHUNK + 2 * SUB


def _hg_gates(q, f, lbv):
    sig = _sigmoid(f)
    fg = lbv + (1.0 - lbv) * sig
    sq = _sigmoid(q)
    return sig, fg, 1.0 - fg, sq, q * sq


def _hg_cumsum(logf):
    ri = lax.broadcasted_iota(I32, (CHUNK, CHUNK), 0)
    ci = lax.broadcasted_iota(I32, (CHUNK, CHUNK), 1)
    return _dot((ci <= ri).astype(F32), logf, NN, precision=lax.Precision.HIGHEST)


def _hg_rows():
    return lax.broadcasted_iota(I32, (CHUNK, LANE), 0)


def _hg_below(qf, kk, b, rows):
    blocks, parts = [jnp.zeros((SUB, CHUNK), F32)], []
    for i in range(1, CHUNK // SUB):
        bref = b[SUB * i - 1:SUB * i, :]
        rs = slice(SUB * i, SUB * (i + 1))
        eq = jnp.exp(b[rs] - bref)
        below = rows < SUB * i
        ek = jnp.exp(jnp.where(below, bref - b, NEG))
        qi = (qf[rs] * eq).astype(BF16)
        ki = (kk * ek).astype(BF16)
        blocks.append(_dot(qi, ki, NT))
        parts.append((qi, ki, eq, ek))
    return jnp.concatenate(blocks, axis=0), parts


def hgrn_fwd(proj, lb, nw, after=()):
    S = proj.shape[0]
    NC = S // CHUNK
    H = HG_HEADS

    def body(q_ref, f_ref, v_ref, g_ref, lb_ref, nw_ref, *rest):
        out_ref, st_out_ref, o_ref, st, kk_p, b_p, v_p = rest[len(after):]
        c = pl.program_id(0)

        @pl.when(c == 0)
        def _():
            st[...] = jnp.zeros_like(st)
            for p in (kk_p, b_p, v_p):
                p[...] = jnp.zeros_like(p)

        st_out_ref[...] = st[...]
        sig, fg, kk_all, sq, qf_all = _hg_gates(q_ref[...], f_ref[...], lb_ref[...])
        b_all = _hg_cumsum(jnp.log(fg))
        kk_p[pl.ds(SUB, CHUNK), :] = kk_all
        b_p[pl.ds(SUB, CHUNK), :] = b_all
        v_p[pl.ds(SUB, CHUNK), :] = v_ref[...]
        rows = _hg_rows()
        sub = rows & (SUB - 1)
        for h in range(H):
            cs = slice(h * LANE, (h + 1) * LANE)
            qf, kk, b, v, g = qf_all[:, cs], kk_all[:, cs], b_all[:, cs], v_ref[:, cs], g_ref[:, cs]
            st_in = st[h]
            o = jnp.zeros((CHUNK, LANE), F32)
            for tau in range(SUB):
                sh = pl.ds(SUB - tau, CHUNK)
                e = jnp.exp(jnp.where(sub >= tau, b - b_p[sh, cs], NEG))
                col = jnp.sum(qf * kk_p[sh, cs] * e, axis=1, keepdims=True)
                o = o + col * v_p[sh, cs]
            poff, _ = _hg_below(qf, kk, b, rows)
            vb = v.astype(BF16)
            bl = b[CHUNK - 1:CHUNK, :]
            o = o + _dot(poff.astype(BF16), vb, NN) + _dot((qf * jnp.exp(b)).astype(BF16), st_in.astype(BF16), NT)
            st[h] = st_in * jnp.exp(bl) + _dot(vb, (kk * jnp.exp(bl - b)).astype(BF16), TN)
            o_ref[:, cs] = o
            r = lax.rsqrt(jnp.mean(o * o, axis=-1, keepdims=True) + EPS)
            out_ref[:, cs] = (o * r * nw_ref[...] * (g * _sigmoid(g))).astype(out_ref.dtype)

    def seg(off):
        return pl.BlockSpec((CHUNK, HG_W), lambda c: (c, off // HG_W))

    blk = pl.BlockSpec((CHUNK, HG_W), lambda c: (c, 0))
    pad = pltpu.VMEM((HG_PAD, HG_W), F32)
    return pl.pallas_call(
        body, name="hgrn_fwd", grid=(NC,),
        in_specs=[seg(OFF_Q), seg(OFF_F), seg(OFF_V), seg(OFF_G),
                  pl.BlockSpec((1, HG_W), lambda c: (0, 0)), pl.BlockSpec((1, LANE), lambda c: (0, 0))] + [ANY_SPACE] * len(after),
        out_specs=[blk, pl.BlockSpec((None, H, LANE, LANE), lambda c: (c, 0, 0, 0)), blk],
        out_shape=[jax.ShapeDtypeStruct((S, HG_W), BF16), jax.ShapeDtypeStruct((NC, H, LANE, LANE), F32),
                   jax.ShapeDtypeStruct((S, HG_W), F32)],
        scratch_shapes=[pltpu.VMEM((H, LANE, LANE), F32), pad, pad, pad],
        compiler_params=_cp(("arbitrary",)),
    )(proj, proj, proj, proj, lb.reshape(1, HG_W), nw.reshape(1, LANE), *after)


def hgrn_bwd(proj, lb, nw, states, o_pre, dout):
    S = proj.shape[0]
    NC = S // CHUNK
    H = HG_HEADS

    def body(q_ref, f_ref, v_ref, g_ref, lb_ref, nw_ref, st_ref, o_ref, do_ref,
             dq_ref, df_ref, dv_ref, dg_ref, dlb_ref, dnw_ref, dst, kk_p, b_p, v_p, qf_p, do_p, db_s, dkk_s):
        c = pl.program_id(0)

        @pl.when(c == 0)
        def _():
            dst[...] = jnp.zeros_like(dst)
            dlb_ref[...] = jnp.zeros_like(dlb_ref)
            dnw_ref[...] = jnp.zeros_like(dnw_ref)
            for p in (kk_p, b_p, v_p, qf_p, do_p):
                p[...] = jnp.zeros_like(p)

        q_all, g_all = q_ref[...], g_ref[...]
        lbv, nwv = lb_ref[...], nw_ref[...]
        sig, fg, kk_all, sq, qf_all = _hg_gates(q_all, f_ref[...], lbv)
        b_all = _hg_cumsum(jnp.log(fg))
        o_all = o_ref[...]
        dov = do_ref[...].astype(F32)
        sg = _sigmoid(g_all)
        gsg = g_all * sg
        dnw_acc = jnp.zeros((1, LANE), F32)
        for h in range(H):
            cs = slice(h * LANE, (h + 1) * LANE)
            o = o_all[:, cs]
            r = lax.rsqrt(jnp.mean(o * o, axis=-1, keepdims=True) + EPS)
            don = dov[:, cs] * gsg[:, cs]
            dnw_acc = dnw_acc + jnp.sum(don * o * r, axis=0, keepdims=True)
            gno = don * nwv
            do_p[pl.ds(SUB, CHUNK), cs] = r * gno - o * (r * r * r) * jnp.mean(o * gno, axis=-1, keepdims=True)
            dg_ref[:, cs] = (dov[:, cs] * (o * r * nwv) * _dsilu(g_all[:, cs], sg[:, cs])).astype(dg_ref.dtype)
        dnw_ref[...] += jnp.broadcast_to(dnw_acc, dnw_ref.shape)
        kk_p[pl.ds(SUB, CHUNK), :] = kk_all
        b_p[pl.ds(SUB, CHUNK), :] = b_all
        v_p[pl.ds(SUB, CHUNK), :] = v_ref[...]
        qf_p[pl.ds(SUB, CHUNK), :] = qf_all
        rows = _hg_rows()
        sub = rows & (SUB - 1)
        for h in range(H):
            cs = slice(h * LANE, (h + 1) * LANE)
            qf, kk, b, v = qf_all[:, cs], kk_all[:, cs], b_all[:, cs], v_ref[:, cs]
            do = do_p[pl.ds(SUB, CHUNK), cs]
            st_in, dstv = st_ref[h], dst[h]
            bl = b[CHUNK - 1:CHUNK, :]
            eb, ebl, el = jnp.exp(b), jnp.exp(bl - b), jnp.exp(bl)
            qe, ke = qf * eb, kk * ebl
            vb, dob, stb, dstb = v.astype(BF16), do.astype(BF16), st_in.astype(BF16), dstv.astype(BF16)
            w_ = _dot(vb, dstb, NN)
            dqf = eb * _dot(dob, stb, NN)
            dkk = ebl * w_
            dv = _dot(ke.astype(BF16), dstb, NT)
            dbl = el * jnp.sum(st_in * dstv, axis=0, keepdims=True) + jnp.sum(ke * w_, axis=0, keepdims=True)
            dst[h] = dstv * el + _dot(dob, qe.astype(BF16), TN)
            poff, parts = _hg_below(qf, kk, b, rows)
            dpoff = _dot(dob, vb, NT).astype(BF16)
            dv = dv + _dot(poff.astype(BF16), dob, TN)
            dq_blocks = [jnp.zeros((SUB, LANE), F32)]
            for i, (qi, ki, eq, ek) in enumerate(parts, start=1):
                dpi = dpoff[SUB * i:SUB * (i + 1), :]
                dq_blocks.append(_dot(dpi, ki, NN) * eq)
                dkk = dkk + _dot(dpi, qi, TN) * ek
            dqf = dqf + jnp.concatenate(dq_blocks, axis=0)
            for tau in range(SUB):
                sh = pl.ds(SUB - tau, CHUNK)
                kd = kk_p[sh, cs]
                e = jnp.exp(jnp.where(sub >= tau, b - b_p[sh, cs], NEG))
                dcol = jnp.sum(do * v_p[sh, cs], axis=1, keepdims=True)
                dqf = dqf + dcol * kd * e
            for tau in range(SUB):
                sh = pl.ds(SUB + tau, CHUNK)
                qu, dou = qf_p[sh, cs], do_p[sh, cs]
                e = jnp.exp(jnp.where(sub + tau < SUB, b_p[sh, cs] - b, NEG))
                qe_ = qu * e
                dcol = jnp.sum(dou * v, axis=1, keepdims=True)
                col = jnp.sum(qe_ * kk, axis=1, keepdims=True)
                dkk = dkk + dcol * qe_
                dv = dv + col * dou
            dv_ref[:, cs] = dv.astype(dv_ref.dtype)
            db = qf * dqf - kk * dkk
            db_s[:, cs] = db + jnp.where(rows == CHUNK - 1, dbl, 0.0)
            dkk_s[:, cs] = dkk
            dq_ref[:, cs] = (dqf * _dsilu(q_all[:, cs], sq[:, cs])).astype(dq_ref.dtype)
        ri = lax.broadcasted_iota(I32, (CHUNK, CHUNK), 0)
        ci = lax.broadcasted_iota(I32, (CHUNK, CHUNK), 1)
        dlogf = _dot((ci >= ri).astype(F32), db_s[...], NN, precision=lax.Precision.HIGHEST)
        dfg = dlogf / fg - dkk_s[...]
        df_ref[...] = (dfg * (1.0 - lbv) * sig * (1.0 - sig)).astype(df_ref.dtype)
        dlb_ref[...] += jnp.broadcast_to(jnp.sum(dfg * (1.0 - sig), axis=0, keepdims=True), dlb_ref.shape)

    def seg(off):
        return pl.BlockSpec((CHUNK, HG_W), lambda c: (NC - 1 - c, off // HG_W))

    blk = pl.BlockSpec((CHUNK, HG_W), lambda c: (NC - 1 - c, 0))
    osd = jax.ShapeDtypeStruct((S, HG_W), BF16)
    pad = pltpu.VMEM((HG_PAD, HG_W), F32)
    full = pltpu.VMEM((CHUNK, HG_W), F32)
    dq, df, dv, dg, dlb, dnw = pl.pallas_call(
        body, name="hgrn_bwd", grid=(NC,),
        in_specs=[seg(OFF_Q), seg(OFF_F), seg(OFF_V), seg(OFF_G),
                  pl.BlockSpec((1, HG_W), lambda c: (0, 0)), pl.BlockSpec((1, LANE), lambda c: (0, 0)),
                  pl.BlockSpec((None, H, LANE, LANE), lambda c: (NC - 1 - c, 0, 0, 0)), blk, blk],
        out_specs=[blk, blk, blk, blk, pl.BlockSpec((8, HG_W), lambda c: (0, 0)), pl.BlockSpec((8, LANE), lambda c: (0, 0))],
        out_shape=[osd, osd, osd, osd, jax.ShapeDtypeStruct((8, HG_W), F32), jax.ShapeDtypeStruct((8, LANE), F32)],
        scratch_shapes=[pltpu.VMEM((H, LANE, LANE), F32), pad, pad, pad, pad, pad, full, full],
        compiler_params=_cp(("arbitrary",)),
    )(proj, proj, proj, proj, lb.reshape(1, HG_W), nw.reshape(1, LANE), states, o_pre, dout)
    return dq, df, dv, dg, dlb[0], dnw[0]


CV_PAD = 32
ROWS = 256


def _colblk(S, off):
    return pl.BlockSpec((S, LANE), lambda j: (0, off // LANE + j))


def cv_fwd(proj, w32, bias):
    S = proj.shape[0]
    nchunk = S // ROWS

    def body(a_ref, g_ref, w_ref, b_ref, o_ref, zpad):
        zpad[pl.ds(0, CV_PAD), :] = jnp.zeros((CV_PAD, LANE), F32)

        def glu(c, _):
            r0 = pl.multiple_of(c * ROWS, ROWS)
            zpad[pl.ds(CV_PAD + r0, ROWS), :] = a_ref[pl.ds(r0, ROWS), :] * _sigmoid(g_ref[pl.ds(r0, ROWS), :])
            return 0

        lax.fori_loop(0, nchunk, glu, 0)

        def conv(c, _):
            r0 = pl.multiple_of(c * ROWS, ROWS)
            acc = jnp.broadcast_to(b_ref[...], (ROWS, LANE))
            for j in range(CV_KERNEL):
                acc = acc + w_ref[pl.ds(j, 1), :] * zpad[pl.ds(r0 + (CV_PAD - CV_KERNEL + 1) + j, ROWS), :]
            o_ref[pl.ds(r0, ROWS), :] = acc
            return 0

        lax.fori_loop(0, nchunk, conv, 0)

    return pl.pallas_call(
        body, name="cv_fwd", grid=(BRANCH_W // LANE,),
        in_specs=[_colblk(S, OFF_CV), _colblk(S, OFF_CV + BRANCH_W),
                  pl.BlockSpec((32, LANE), lambda j: (0, j)), pl.BlockSpec((1, LANE), lambda j: (0, j))],
        out_specs=pl.BlockSpec((S, LANE), lambda j: (0, j)), out_shape=jax.ShapeDtypeStruct((S, BRANCH_W), F32),
        scratch_shapes=[pltpu.VMEM((CV_PAD + S, LANE), F32)],
        compiler_params=_cp(("parallel",)),
    )(proj, proj, w32, bias.reshape(1, BRANCH_W))


def cv_bwd(proj, w32, dzc):
    S = proj.shape[0]
    nchunk = S // ROWS

    def body(a_ref, g_ref, w_ref, dz_ref, da_ref, dg_ref, dw_ref, db_ref, zpad, dpad):
        zpad[pl.ds(0, CV_PAD), :] = jnp.zeros((CV_PAD, LANE), F32)
        dpad[pl.ds(S, CV_PAD), :] = jnp.zeros((CV_PAD, LANE), F32)
        dw_ref[...] = jnp.zeros_like(dw_ref)

        def glu(c, dsum):
            r0 = pl.multiple_of(c * ROWS, ROWS)
            zpad[pl.ds(CV_PAD + r0, ROWS), :] = a_ref[pl.ds(r0, ROWS), :] * _sigmoid(g_ref[pl.ds(r0, ROWS), :])
            d = dz_ref[pl.ds(r0, ROWS), :]
            dpad[pl.ds(r0, ROWS), :] = d
            return dsum + jnp.sum(d, axis=0, keepdims=True)

        dsum = lax.fori_loop(0, nchunk, glu, jnp.zeros((1, LANE), F32))
        db_ref[...] = jnp.broadcast_to(dsum, db_ref.shape)

        def conv(c, _):
            r0 = pl.multiple_of(c * ROWS, ROWS)
            d = dpad[pl.ds(r0, ROWS), :]
            acc = jnp.zeros((ROWS, LANE), F32)
            for j in range(CV_KERNEL):
                acc = acc + w_ref[pl.ds(j, 1), :] * dpad[pl.ds(r0 + (CV_KERNEL - 1) - j, ROWS), :]
                zs = zpad[pl.ds(r0 + (CV_PAD - CV_KERNEL + 1) + j, ROWS), :]
                dw_ref[pl.ds(j, 1), :] += jnp.sum(d * zs, axis=0, keepdims=True)
            a = a_ref[pl.ds(r0, ROWS), :]
            sg = _sigmoid(g_ref[pl.ds(r0, ROWS), :])
            da_ref[pl.ds(r0, ROWS), :] = (acc * sg).astype(da_ref.dtype)
            dg_ref[pl.ds(r0, ROWS), :] = (acc * a * sg * (1.0 - sg)).astype(dg_ref.dtype)
            return 0

        lax.fori_loop(0, nchunk, conv, 0)

    blk = pl.BlockSpec((S, LANE), lambda j: (0, j))
    da, dg, dw, db = pl.pallas_call(
        body, name="cv_bwd", grid=(BRANCH_W // LANE,),
        in_specs=[_colblk(S, OFF_CV), _colblk(S, OFF_CV + BRANCH_W), pl.BlockSpec((32, LANE), lambda j: (0, j)), blk],
        out_specs=[blk, blk, pl.BlockSpec((32, LANE), lambda j: (0, j)), pl.BlockSpec((8, LANE), lambda j: (0, j))],
        out_shape=[jax.ShapeDtypeStruct((S, BRANCH_W), BF16), jax.ShapeDtypeStruct((S, BRANCH_W), BF16),
                   jax.ShapeDtypeStruct((32, BRANCH_W), F32), jax.ShapeDtypeStruct((8, BRANCH_W), F32)],
        scratch_shapes=[pltpu.VMEM((CV_PAD + S, LANE), F32), pltpu.VMEM((S + CV_PAD, LANE), F32)],
        compiler_params=_cp(("parallel",)),
    )(proj, proj, w32, dzc)
    return da, dg, dw, db[0]


def ln_silu_fwd(z, w, b, tm=512):
    S, C = z.shape
    tm = _tile(S, tm)

    def body(z_ref, w_ref, b_ref, o_ref):
        zv = z_ref[...]
        mu = jnp.mean(zv, axis=-1, keepdims=True)
        zc = zv - mu
        rstd = lax.rsqrt(jnp.mean(zc * zc, axis=-1, keepdims=True) + EPS)
        y = zc * rstd * w_ref[...] + b_ref[...]
        o_ref[...] = (y * _sigmoid(y)).astype(o_ref.dtype)

    row = pl.BlockSpec((tm, C), lambda i: (i, 0))
    vec = pl.BlockSpec((1, C), lambda i: (0, 0))
    return pl.pallas_call(
        body, name="ln_silu_fwd", grid=(S // tm,), in_specs=[row, vec, vec], out_specs=row,
        out_shape=jax.ShapeDtypeStruct((S, C), BF16), compiler_params=_cp(("parallel",)),
    )(z, w.reshape(1, C), b.reshape(1, C))


def ln_silu_bwd(z, w, b, dout, tm=512):
    S, C = z.shape
    tm = _tile(S, tm)

    def body(z_ref, w_ref, b_ref, do_ref, dz_ref, dw_ref, db_ref):
        zv = z_ref[...]
        wv = w_ref[...]
        mu = jnp.mean(zv, axis=-1, keepdims=True)
        zc = zv - mu
        rstd = lax.rsqrt(jnp.mean(zc * zc, axis=-1, keepdims=True) + EPS)
        xh = zc * rstd
        y = xh * wv + b_ref[...]
        dy = do_ref[...].astype(F32) * _dsilu(y, _sigmoid(y))

        @pl.when(pl.program_id(0) == 0)
        def _():
            dw_ref[...] = jnp.zeros_like(dw_ref)
            db_ref[...] = jnp.zeros_like(db_ref)

        dw_ref[...] += jnp.sum(dy * xh, axis=0, keepdims=True)
        db_ref[...] += jnp.sum(dy, axis=0, keepdims=True)
        dxh = dy * wv
        dz_ref[...] = rstd * (dxh - jnp.mean(dxh, axis=-1, keepdims=True) - xh * jnp.mean(dxh * xh, axis=-1, keepdims=True))

    row = pl.BlockSpec((tm, C), lambda i: (i, 0))
    vec = pl.BlockSpec((1, C), lambda i: (0, 0))
    dz, dw, db = pl.pallas_call(
        body, name="ln_silu_bwd", grid=(S // tm,), in_specs=[row, vec, vec, row], out_specs=[row, vec, vec],
        out_shape=[jax.ShapeDtypeStruct((S, C), F32), jax.ShapeDtypeStruct((1, C), F32), jax.ShapeDtypeStruct((1, C), F32)],
        compiler_params=_cp(("arbitrary",)),
    )(z, w.reshape(1, C), b.reshape(1, C), dout)
    return dz, dw.reshape(C), db.reshape(C)


PL_PAD = 16


def _pool_counts(r0, win):
    t = r0 + lax.broadcasted_iota(I32, (ROWS, LANE), 0)
    return jnp.minimum(t + 1, win).astype(F32)


def pool_fwd(proj, wg, scale):
    S = proj.shape[0]
    nchunk = S // ROWS

    def body(u_ref, w_ref, s_ref, o_ref, upad):
        g = pl.program_id(0)
        upad[pl.ds(0, PL_PAD), :] = jnp.zeros((PL_PAD, LANE), F32)

        def fill(c, _):
            r0 = pl.multiple_of(c * ROWS, ROWS)
            upad[pl.ds(PL_PAD + r0, ROWS), :] = u_ref[pl.ds(r0, ROWS), :]
            return 0

        lax.fori_loop(0, nchunk, fill, 0)
        wb = w_ref[...].astype(BF16)
        for gi, win in enumerate(POOL_WINDOWS):
            @pl.when(g == gi)
            def _(win=win):
                def chunk(c, _):
                    r0 = pl.multiple_of(c * ROWS, ROWS)
                    u = upad[pl.ds(PL_PAD + r0, ROWS), :]
                    ws = u
                    for j in range(1, win):
                        ws = ws + upad[pl.ds(PL_PAD + r0 - j, ROWS), :]
                    pooled = ws / _pool_counts(r0, win) - u
                    o_ref[pl.ds(r0, ROWS), :] = (_dot(pooled.astype(BF16), wb, NN) * s_ref[...]).astype(o_ref.dtype)
                    return 0

                lax.fori_loop(0, nchunk, chunk, 0)

    return pl.pallas_call(
        body, name="pool_fwd", grid=(len(POOL_WINDOWS),),
        in_specs=[_colblk(S, OFF_PL), pl.BlockSpec((None, LANE, LANE), lambda j: (j, 0, 0)), pl.BlockSpec((1, LANE), lambda j: (0, j))],
        out_specs=pl.BlockSpec((S, LANE), lambda j: (0, j)), out_shape=jax.ShapeDtypeStruct((S, BRANCH_W), BF16),
        scratch_shapes=[pltpu.VMEM((PL_PAD + S, LANE), F32)],
        compiler_params=_cp(("parallel",)),
    )(proj, wg, scale.reshape(1, BRANCH_W))


def pool_bwd(proj, wg, scale, dy):
    S = proj.shape[0]
    nchunk = S // ROWS

    def body(u_ref, w_ref, s_ref, dy_ref, du_ref, dw_ref, ds_ref, upad, dpn, nd):
        g = pl.program_id(0)
        upad[pl.ds(0, PL_PAD), :] = jnp.zeros((PL_PAD, LANE), F32)
        dpn[pl.ds(S, PL_PAD), :] = jnp.zeros((PL_PAD, LANE), F32)

        def fill(c, _):
            r0 = pl.multiple_of(c * ROWS, ROWS)
            upad[pl.ds(PL_PAD + r0, ROWS), :] = u_ref[pl.ds(r0, ROWS), :]
            return 0

        lax.fori_loop(0, nchunk, fill, 0)
        wb = w_ref[...].astype(BF16)
        sv = s_ref[...]
        for gi, win in enumerate(POOL_WINDOWS):
            @pl.when(g == gi)
            def _(win=win):
                def chunk(c, carry):
                    dw, dsc = carry
                    r0 = pl.multiple_of(c * ROWS, ROWS)
                    u = upad[pl.ds(PL_PAD + r0, ROWS), :]
                    ws = u
                    for j in range(1, win):
                        ws = ws + upad[pl.ds(PL_PAD + r0 - j, ROWS), :]
                    cnt = _pool_counts(r0, win)
                    pooled = (ws / cnt - u).astype(BF16)
                    dyv = dy_ref[pl.ds(r0, ROWS), :].astype(F32)
                    dsc = dsc + jnp.sum(dyv * _dot(pooled, wb, NN), axis=0, keepdims=True)
                    dys = (dyv * sv).astype(BF16)
                    dw = dw + _dot(pooled, dys, TN)
                    dp = _dot(dys, wb, NT)
                    dpn[pl.ds(r0, ROWS), :] = dp / cnt
                    nd[pl.ds(r0, ROWS), :] = -dp
                    return dw, dsc

                dw, dsc = lax.fori_loop(0, nchunk, chunk, (jnp.zeros((LANE, LANE), F32), jnp.zeros((1, LANE), F32)))
                dw_ref[...] = dw
                ds_ref[...] = jnp.broadcast_to(dsc, ds_ref.shape)

                def spread(c, _):
                    r0 = pl.multiple_of(c * ROWS, ROWS)
                    acc = nd[pl.ds(r0, ROWS), :]
                    for j in range(win):
                        acc = acc + dpn[pl.ds(r0 + j, ROWS), :]
                    du_ref[pl.ds(r0, ROWS), :] = acc.astype(du_ref.dtype)
                    return 0

                lax.fori_loop(0, nchunk, spread, 0)

    blk = pl.BlockSpec((S, LANE), lambda j: (0, j))
    du, dw, ds = pl.pallas_call(
        body, name="pool_bwd", grid=(len(POOL_WINDOWS),),
        in_specs=[_colblk(S, OFF_PL), pl.BlockSpec((None, LANE, LANE), lambda j: (j, 0, 0)), pl.BlockSpec((1, LANE), lambda j: (0, j)), blk],
        out_specs=[blk, pl.BlockSpec((None, LANE, LANE), lambda j: (j, 0, 0)), pl.BlockSpec((8, LANE), lambda j: (0, j))],
        out_shape=[jax.ShapeDtypeStruct((S, BRANCH_W), BF16), jax.ShapeDtypeStruct((len(POOL_WINDOWS), LANE, LANE), F32),
                   jax.ShapeDtypeStruct((8, BRANCH_W), F32)],
        scratch_shapes=[pltpu.VMEM((PL_PAD + S, LANE), F32), pltpu.VMEM((S + PL_PAD, LANE), F32), pltpu.VMEM((S, LANE), F32)],
        compiler_params=_cp(("parallel",)),
    )(proj, wg, scale.reshape(1, BRANCH_W), dy)
    return du, dw, ds[0]


LR_PAD = 8
SCAN_TILES = 4
GELU_C = math.sqrt(2.0 / math.pi)
GELU_A = 0.044715


def _gelu(y):
    return 0.5 * y * (1.0 + jnp.tanh(GELU_C * (y + GELU_A * y * y * y)))


def _dgelu(y):
    t = jnp.tanh(GELU_C * (y + GELU_A * y * y * y))
    return 0.5 * (1.0 + t) + 0.5 * y * (1.0 - t * t) * GELU_C * (1.0 + 3.0 * GELU_A * y * y)


def _lru_gates(xpad, r0, cw_ref, cb, wa, ba, wx, bx, sp8):
    xc = jnp.broadcast_to(cb, (ROWS, LANE))
    for j in range(LRU_CONV):
        xc = xc + cw_ref[pl.ds(j, 1), :] * xpad[pl.ds(r0 + (LR_PAD - LRU_CONV + 1) + j, ROWS), :]
    xb = xc.astype(BF16)
    r = _sigmoid(_dot(xb, wa, NN) + ba)
    ig = _sigmoid(_dot(xb, wx, NN) + bx)
    la = -sp8 * r
    a = jnp.exp(la)
    s = jnp.sqrt(-jnp.tanh(la) * (a * a + 1.0))
    return xc, r, ig, a, s


def _tile_scan(a, b, r8, up):
    for s in (1, 2, 4):
        keep = (r8 < 8 - s) if up else (r8 >= s)
        shift = 8 - s if up else s
        a_sh = jnp.where(keep, pltpu.roll(a, shift, 0), 1.0)
        b_sh = jnp.where(keep, pltpu.roll(b, shift, 0), 0.0)
        b = b + a * b_sh
        a = a * a_sh
    return a, b


def lru_fwd(proj, cw8, cb, wa_bd, ba, wx_bd, bx, sp8):
    S = proj.shape[0]
    nchunk = S // ROWS

    def body(x_ref, y_ref, cw_ref, cb_ref, wa_ref, ba_ref, wx_ref, bx_ref, sp_ref, o_ref, h_ref, xpad, a_s):
        xpad[pl.ds(0, LR_PAD), :] = jnp.zeros((LR_PAD, LANE), F32)

        def fill(c, _):
            r0 = pl.multiple_of(c * ROWS, ROWS)
            xpad[pl.ds(LR_PAD + r0, ROWS), :] = x_ref[pl.ds(r0, ROWS), :]
            return 0

        lax.fori_loop(0, nchunk, fill, 0)
        wa = wa_ref[...].astype(BF16)
        wx = wx_ref[...].astype(BF16)

        def gates(c, _):
            r0 = pl.multiple_of(c * ROWS, ROWS)
            xc, r, ig, a, s = _lru_gates(xpad, r0, cw_ref, cb_ref[...], wa, ba_ref[...], wx, bx_ref[...], sp_ref[...])
            a_s[pl.ds(r0, ROWS), :] = a
            h_ref[pl.ds(r0, ROWS), :] = s * (ig * xc)
            return 0

        lax.fori_loop(0, nchunk, gates, 0)

        r8 = lax.broadcasted_iota(I32, (8, LANE), 0)

        def scan(i, h):
            bases = [pl.multiple_of(i * (8 * SCAN_TILES) + 8 * j, 8) for j in range(SCAN_TILES)]
            maps = [_tile_scan(a_s[pl.ds(b, 8), :], h_ref[pl.ds(b, 8), :], r8, False) for b in bases]
            for b, (ca, cb_) in zip(bases, maps):
                out = cb_ + ca * h
                h_ref[pl.ds(b, 8), :] = out
                h = out[7:8, :]
            return h

        lax.fori_loop(0, S // (8 * SCAN_TILES), scan, jnp.zeros((1, LANE), F32))

        def gate_out(c, _):
            r0 = pl.multiple_of(c * ROWS, ROWS)
            o_ref[pl.ds(r0, ROWS), :] = (h_ref[pl.ds(r0, ROWS), :] * _gelu(y_ref[pl.ds(r0, ROWS), :])).astype(o_ref.dtype)
            return 0

        lax.fori_loop(0, nchunk, gate_out, 0)

    vec = pl.BlockSpec((1, LANE), lambda j: (0, j))
    mat = pl.BlockSpec((None, LANE, LANE), lambda j: (j, 0, 0))
    blk = pl.BlockSpec((S, LANE), lambda j: (0, j))
    return pl.pallas_call(
        body, name="lru_fwd", grid=(BRANCH_W // LANE,),
        in_specs=[_colblk(S, OFF_LX), _colblk(S, OFF_LY), pl.BlockSpec((8, LANE), lambda j: (0, j)), vec, mat, vec, mat, vec, vec],
        out_specs=[blk, blk],
        out_shape=[jax.ShapeDtypeStruct((S, BRANCH_W), BF16), jax.ShapeDtypeStruct((S, BRANCH_W), F32)],
        scratch_shapes=[pltpu.VMEM((LR_PAD + S, LANE), F32), pltpu.VMEM((S, LANE), F32)],
        compiler_params=_cp(("parallel",)),
    )(proj, proj, cw8, cb.reshape(1, -1), wa_bd, ba.reshape(1, -1), wx_bd, bx.reshape(1, -1), sp8.reshape(1, -1))


def lru_bwd(proj, cw8, cb, wa_bd, ba, wx_bd, bx, sp8, h, dout):
    S = proj.shape[0]
    nchunk = S // ROWS

    def body(x_ref, y_ref, cw_ref, cb_ref, wa_ref, ba_ref, wx_ref, bx_ref, sp_ref, h_ref, do_ref,
             dx_ref, dy_ref, dcw_ref, dcb_ref, dwa_ref, dba_ref, dwx_ref, dbx_ref, dsp_ref,
             xpad, a_s, g_s, hpad, dxc):
        xpad[pl.ds(0, LR_PAD), :] = jnp.zeros((LR_PAD, LANE), F32)
        hpad[pl.ds(0, LR_PAD), :] = jnp.zeros((LR_PAD, LANE), F32)
        dxc[pl.ds(S, LR_PAD), :] = jnp.zeros((LR_PAD, LANE), F32)
        dcw_ref[...] = jnp.zeros_like(dcw_ref)
        wa = wa_ref[...].astype(BF16)
        wx = wx_ref[...].astype(BF16)
        cbv, bav, bxv, spv = cb_ref[...], ba_ref[...], bx_ref[...], sp_ref[...]

        def fill(c, _):
            r0 = pl.multiple_of(c * ROWS, ROWS)
            xpad[pl.ds(LR_PAD + r0, ROWS), :] = x_ref[pl.ds(r0, ROWS), :]
            hv = h_ref[pl.ds(r0, ROWS), :]
            hpad[pl.ds(LR_PAD + r0, ROWS), :] = hv
            yv = y_ref[pl.ds(r0, ROWS), :]
            dov = do_ref[pl.ds(r0, ROWS), :].astype(F32)
            g_s[pl.ds(r0, ROWS), :] = dov * _gelu(yv)
            dy_ref[pl.ds(r0, ROWS), :] = (dov * hv * _dgelu(yv)).astype(dy_ref.dtype)
            return 0

        lax.fori_loop(0, nchunk, fill, 0)

        def gates(c, _):
            r0 = pl.multiple_of(c * ROWS, ROWS)
            _, _, _, a, _ = _lru_gates(xpad, r0, cw_ref, cbv, wa, bav, wx, bxv, spv)
            a_s[pl.ds(r0, ROWS), :] = a
            return 0

        lax.fori_loop(0, nchunk, gates, 0)

        r8 = lax.broadcasted_iota(I32, (8, LANE), 0)

        def rscan(i, carry):
            bases = [pl.multiple_of(S - 8 - i * (8 * SCAN_TILES) - 8 * j, 8) for j in range(SCAN_TILES)]
            firsts, maps = [], []
            for b in bases:
                a8 = a_s[pl.ds(b, 8), :]
                above = jnp.where(r8 < 7, pltpu.roll(a8, 7, 0), 1.0)
                firsts.append(a8[0:1, :])
                maps.append(_tile_scan(above, g_s[pl.ds(b, 8), :], r8, True))
            for b, a0, (ca, cb_) in zip(bases, firsts, maps):
                out = cb_ + ca * carry
                g_s[pl.ds(b, 8), :] = out
                carry = a0 * out[0:1, :]
            return carry

        lax.fori_loop(0, S // (8 * SCAN_TILES), rscan, jnp.zeros((1, LANE), F32))

        def chain(c, carry):
            dwa, dwx, dba, dbx, dsp, dcb = carry
            r0 = pl.multiple_of(c * ROWS, ROWS)
            xc, r, ig, a, s = _lru_gates(xpad, r0, cw_ref, cbv, wa, bav, wx, bxv, spv)
            gt = g_s[pl.ds(r0, ROWS), :]
            hprev = hpad[pl.ds(r0 + LR_PAD - 1, ROWS), :]
            da = gt * hprev - gt * ig * xc * (a / s)
            dig = gt * s * xc
            dla = da * a
            dsp = dsp + jnp.sum(-dla * r, axis=0, keepdims=True)
            dpr = (-dla * spv) * r * (1.0 - r)
            dpi = dig * ig * (1.0 - ig)
            dprb, dpib, xb = dpr.astype(BF16), dpi.astype(BF16), xc.astype(BF16)
            d = gt * s * ig + _dot(dprb, wa, NT) + _dot(dpib, wx, NT)
            dwa = dwa + _dot(xb, dprb, TN)
            dwx = dwx + _dot(xb, dpib, TN)
            dba = dba + jnp.sum(dpr, axis=0, keepdims=True)
            dbx = dbx + jnp.sum(dpi, axis=0, keepdims=True)
            dcb = dcb + jnp.sum(d, axis=0, keepdims=True)
            dxc[pl.ds(r0, ROWS), :] = d
            for j in range(LRU_CONV):
                xs = xpad[pl.ds(r0 + (LR_PAD - LRU_CONV + 1) + j, ROWS), :]
                dcw_ref[pl.ds(j, 1), :] += jnp.sum(d * xs, axis=0, keepdims=True)
            return dwa, dwx, dba, dbx, dsp, dcb

        zm, zv = jnp.zeros((LANE, LANE), F32), jnp.zeros((1, LANE), F32)
        dwa, dwx, dba, dbx, dsp, dcb = lax.fori_loop(0, nchunk, chain, (zm, zm, zv, zv, zv, zv))
        dwa_ref[...] = dwa
        dwx_ref[...] = dwx
        dba_ref[...] = jnp.broadcast_to(dba, dba_ref.shape)
        dbx_ref[...] = jnp.broadcast_to(dbx, dbx_ref.shape)
        dsp_ref[...] = jnp.broadcast_to(dsp, dsp_ref.shape)
        dcb_ref[...] = jnp.broadcast_to(dcb, dcb_ref.shape)

        def convt(c, _):
            r0 = pl.multiple_of(c * ROWS, ROWS)
            acc = jnp.zeros((ROWS, LANE), F32)
            for j in range(LRU_CONV):
                acc = acc + cw_ref[pl.ds(j, 1), :] * dxc[pl.ds(r0 + (LRU_CONV - 1) - j, ROWS), :]
            dx_ref[pl.ds(r0, ROWS), :] = acc.astype(dx_ref.dtype)
            return 0

        lax.fori_loop(0, nchunk, convt, 0)

    vec = pl.BlockSpec((1, LANE), lambda j: (0, j))
    vec8 = pl.BlockSpec((8, LANE), lambda j: (0, j))
    mat = pl.BlockSpec((None, LANE, LANE), lambda j: (j, 0, 0))
    blk = pl.BlockSpec((S, LANE), lambda j: (0, j))
    nblk = BRANCH_W // LANE
    v8 = jax.ShapeDtypeStruct((8, BRANCH_W), F32)
    m4 = jax.ShapeDtypeStruct((nblk, LANE, LANE), F32)
    big = jax.ShapeDtypeStruct((S, BRANCH_W), BF16)
    seq = pltpu.VMEM((S, LANE), F32)
    dx, dy, dcw, dcb, dwa, dba, dwx, dbx, dsp = pl.pallas_call(
        body, name="lru_bwd", grid=(nblk,),
        in_specs=[_colblk(S, OFF_LX), _colblk(S, OFF_LY), vec8, vec, mat, vec, mat, vec, vec, blk, blk],
        out_specs=[blk, blk, vec8, vec8, mat, vec8, mat, vec8, vec8],
        out_shape=[big, big, v8, v8, m4, v8, m4, v8, v8],
        scratch_shapes=[pltpu.VMEM((LR_PAD + S, LANE), F32), seq, seq, pltpu.VMEM((LR_PAD + S, LANE), F32),
                        pltpu.VMEM((S + LR_PAD, LANE), F32)],
        compiler_params=_cp(("parallel",)),
    )(proj, proj, cw8, cb.reshape(1, -1), wa_bd, ba.reshape(1, -1), wx_bd, bx.reshape(1, -1), sp8.reshape(1, -1), h, dout)
    return dx, dy, dcw, dcb[0], dwa, dba[0], dwx, dbx[0], dsp[0]


MG_COLS = 512
N_BRANCH = 4


def _gate_spec(tm, k):
    return pl.BlockSpec((tm, MG_COLS), lambda j, i: (i, (OFF_GATE + k * D_MODEL) // MG_COLS + j))


def merge_fwd(ups, proj, gate_b, tm=256):
    S = proj.shape[0]
    tm = _tile(S, tm)

    def body(u0, u1, u2, u3, g0, g1, g2, g3, gb_ref, o_ref):
        acc = jnp.zeros((tm, MG_COLS), F32)
        for k, (u, g) in enumerate(((u0, g0), (u1, g1), (u2, g2), (u3, g3))):
            acc = acc + _sigmoid(g[...] + gb_ref[pl.ds(k, 1), :]) * u[...]
        o_ref[...] = acc.astype(o_ref.dtype)

    blk = pl.BlockSpec((tm, MG_COLS), lambda j, i: (i, j))
    return pl.pallas_call(
        body, name="merge_fwd", grid=(D_MODEL // MG_COLS, S // tm),
        in_specs=[blk] * N_BRANCH + [_gate_spec(tm, k) for k in range(N_BRANCH)] + [pl.BlockSpec((N_BRANCH, MG_COLS), lambda j, i: (0, j))],
        out_specs=blk, out_shape=jax.ShapeDtypeStruct((S, D_MODEL), BF16),
        compiler_params=_cp(("parallel", "parallel")),
    )(*ups, proj, proj, proj, proj, gate_b)


def merge_bwd(dmerged, ups, proj, gate_b, tm=256):
    S = proj.shape[0]
    tm = _tile(S, tm)

    def body(dm_ref, u0, u1, u2, u3, g0, g1, g2, g3, gb_ref, du0, du1, du2, du3, dg0, dg1, dg2, dg3, dgb_ref):
        @pl.when(pl.program_id(1) == 0)
        def _():
            dgb_ref[...] = jnp.zeros_like(dgb_ref)

        dm = dm_ref[...].astype(F32)
        for k, (u, g, du, dg) in enumerate(((u0, g0, du0, dg0), (u1, g1, du1, dg1), (u2, g2, du2, dg2), (u3, g3, du3, dg3))):
            sg = _sigmoid(g[...] + gb_ref[pl.ds(k, 1), :])
            du[...] = (dm * sg).astype(du.dtype)
            dgk = dm * u[...] * sg * (1.0 - sg)
            dg[...] = dgk.astype(dg.dtype)
            dgb_ref[pl.ds(8 * k, 8), :] += jnp.broadcast_to(jnp.sum(dgk, axis=0, keepdims=True), (8, MG_COLS))

    blk = pl.BlockSpec((tm, MG_COLS), lambda j, i: (i, j))
    big = jax.ShapeDtypeStruct((S, D_MODEL), BF16)
    outs = pl.pallas_call(
        body, name="merge_bwd", grid=(D_MODEL // MG_COLS, S // tm),
        in_specs=[blk] * (1 + N_BRANCH) + [_gate_spec(tm, k) for k in range(N_BRANCH)] + [pl.BlockSpec((N_BRANCH, MG_COLS), lambda j, i: (0, j))],
        out_specs=[blk] * (2 * N_BRANCH) + [pl.BlockSpec((8 * N_BRANCH, MG_COLS), lambda j, i: (0, j))],
        out_shape=[big] * (2 * N_BRANCH) + [jax.ShapeDtypeStruct((8 * N_BRANCH, D_MODEL), F32)],
        compiler_params=_cp(("parallel", "arbitrary")),
    )(dmerged, *ups, proj, proj, proj, proj, gate_b)
    return outs[:N_BRANCH], outs[N_BRANCH:2 * N_BRANCH], outs[-1].reshape(N_BRANCH, 8, D_MODEL)[:, 0]


def attn_fwd(q, kv, tm=512):
    S = q.shape[0]
    M = kv.shape[0]
    tm = _tile(S, tm)
    scale = XA_HD ** -0.5

    def body(q_ref, kv_ref, o_ref):
        for hh in range(XA_HEADS):
            cs = pl.ds(hh * XA_HD, XA_HD)
            qh = q_ref[:, cs]
            kh = kv_ref[:, cs]
            vh = kv_ref[:, pl.ds(D_MODEL + hh * XA_HD, XA_HD)]
            s = _dot(qh, kh, NT) * scale
            p = jnp.exp(s - jnp.max(s, axis=-1, keepdims=True))
            p = p / jnp.sum(p, axis=-1, keepdims=True)
            o_ref[:, cs] = _dot(p.astype(BF16), vh, NN).astype(o_ref.dtype)

    return pl.pallas_call(
        body, name="attn_fwd", grid=(S // tm,),
        in_specs=[pl.BlockSpec((tm, D_MODEL), lambda i: (i, 0)), pl.BlockSpec((M, 2 * D_MODEL), lambda i: (0, 0))],
        out_specs=pl.BlockSpec((tm, D_MODEL), lambda i: (i, 0)), out_shape=jax.ShapeDtypeStruct((S, D_MODEL), BF16),
        compiler_params=_cp(("parallel",)),
    )(q, kv)


def attn_bwd(q, kv, do, tm=512):
    S = q.shape[0]
    M = kv.shape[0]
    tm = _tile(S, tm)
    scale = XA_HD ** -0.5

    def body(q_ref, kv_ref, do_ref, dq_ref, dkv_ref):
        @pl.when(pl.program_id(0) == 0)
        def _():
            dkv_ref[...] = jnp.zeros_like(dkv_ref)

        for hh in range(XA_HEADS):
            cs = pl.ds(hh * XA_HD, XA_HD)
            vs = pl.ds(D_MODEL + hh * XA_HD, XA_HD)
            qh = q_ref[:, cs]
            kh = kv_ref[:, cs]
            vh = kv_ref[:, vs]
            doh = do_ref[:, cs]
            s = _dot(qh, kh, NT) * scale
            p = jnp.exp(s - jnp.max(s, axis=-1, keepdims=True))
            p = p / jnp.sum(p, axis=-1, keepdims=True)
            dp = _dot(doh, vh, NT)
            ds = (p * (dp - jnp.sum(dp * p, axis=-1, keepdims=True)) * scale).astype(BF16)
            dq_ref[:, cs] = _dot(ds, kh, NN).astype(dq_ref.dtype)
            dkv_ref[:, cs] += _dot(ds, qh, TN)
            dkv_ref[:, vs] += _dot(p.astype(BF16), doh, TN)

    row = pl.BlockSpec((tm, D_MODEL), lambda i: (i, 0))
    full = pl.BlockSpec((M, 2 * D_MODEL), lambda i: (0, 0))
    return pl.pallas_call(
        body, name="attn_bwd", grid=(S // tm,), in_specs=[row, full, row], out_specs=[row, full],
        out_shape=[jax.ShapeDtypeStruct((S, D_MODEL), BF16), jax.ShapeDtypeStruct((M, 2 * D_MODEL), F32)],
        compiler_params=_cp(("arbitrary",)),
    )(q, kv, do)


def sum_parts(parts, own=None, tm=256):
    n, R, C = parts.shape
    tm = _tile(R, tm)
    has_own = own is not None

    def body(*refs):
        p_ref, o_ref = refs[0], refs[-1]
        acc = refs[1][...].astype(F32) if has_own else p_ref[0].astype(F32)
        for j in range(0 if has_own else 1, n):
            acc = acc + p_ref[j].astype(F32)
        o_ref[...] = acc

    row = pl.BlockSpec((tm, C), lambda i: (i, 0))
    return pl.pallas_call(
        body, name="sum_parts", grid=(R // tm,),
        in_specs=[pl.BlockSpec((n, tm, C), lambda i: (0, i, 0))] + ([row] if has_own else []), out_specs=row,
        out_shape=jax.ShapeDtypeStruct((R, C), F32), compiler_params=_cp(("parallel",)),
    )(*([parts, own] if has_own else [parts]))


def adamw(w, g, m, v, tm=256):
    R, C = w.shape
    tm = _tile(R, tm)
    c1 = 1.0 / (1.0 - ADAM_B1 ** ADAM_STEP)
    c2 = 1.0 / (1.0 - ADAM_B2 ** ADAM_STEP)

    def body(w_ref, g_ref, m_ref, v_ref, d_ref, nm_ref, nv_ref):
        gv = g_ref[...]
        nm = ADAM_B1 * m_ref[...] + (1.0 - ADAM_B1) * gv
        nv = ADAM_B2 * v_ref[...] + (1.0 - ADAM_B2) * (gv * gv)
        nm_ref[...] = nm
        nv_ref[...] = nv
        d_ref[...] = -ADAM_LR * ((nm * c1) / (jnp.sqrt(nv * c2) + ADAM_EPS) + ADAM_WD * w_ref[...])

    blk = pl.BlockSpec((tm, C), lambda i: (i, 0))
    sd = jax.ShapeDtypeStruct((R, C), F32)
    return pl.pallas_call(
        body, name="adamw", grid=(R // tm,), in_specs=[blk] * 4, out_specs=[blk] * 3, out_shape=[sd] * 3,
        compiler_params=_cp(("parallel",)),
    )(w, g, m, v)


ANY = pl.BlockSpec(memory_space=pl.ANY)


def _place():
    return lax.axis_index("x"), lax.axis_index("y"), lax.axis_index("c")


def _slot(px, py, pc):
    return 4 * px + 2 * py + pc


def all_gather(name, shards, after=()):
    n = len(shards)
    n_in = n + len(after)

    def body(*refs):
        x_refs, out_refs = refs[:n], refs[n_in:n_in + n]
        send_sems, recv_sems, local_sems = refs[n_in + n:]
        x, y, c = _place()
        me, sibling = (x, y, c), (x, y, 1 - c)
        chips = [(1 - x, y), (x, 1 - y), (1 - x, 1 - y)]

        def copy(a, k, block, to, src=None):
            rows = out_refs[a].at[_slot(*block)]
            return pltpu.make_async_remote_copy(
                src_ref=rows if src is None else src, dst_ref=rows,
                send_sem=send_sems.at[7 * a + k], recv_sem=recv_sems.at[7 * a + k],
                device_id=to, device_id_type=MESH)

        mine = [pltpu.make_async_copy(x_refs[a], out_refs[a].at[_slot(*me)], local_sems.at[a]) for a in range(n)]
        for cp in mine:
            cp.start()
        first = []
        for a in range(n):
            first.append(copy(a, 0, me, sibling, src=x_refs[a]))
            first += [copy(a, 1 + j, me, (*chip, c), src=x_refs[a]) for j, chip in enumerate(chips)]
        for cp in first:
            cp.start()
        passed = []
        for a in range(n):
            for j, chip in enumerate(chips):
                copy(a, 1 + j, (*chip, c), me).wait_recv()
                cp = copy(a, 4 + j, (*chip, c), sibling)
                cp.start()
                passed.append(cp)
        for a in range(n):
            copy(a, 0, sibling, me).wait_recv()
            for j, chip in enumerate(chips):
                copy(a, 4 + j, (*chip, 1 - c), me).wait_recv()
        for cp in first + passed:
            cp.wait_send()
        for cp in mine:
            cp.wait()

    return pl.pallas_call(
        body, name=name, in_specs=[ANY] * n_in, out_specs=[ANY] * n,
        out_shape=[jax.ShapeDtypeStruct((N_DEV, *s.shape), s.dtype) for s in shards],
        scratch_shapes=[pltpu.SemaphoreType.DMA((7 * n,)), pltpu.SemaphoreType.DMA((7 * n,)), pltpu.SemaphoreType.DMA((n,))],
    )(*shards, *after)


HBM = pl.BlockSpec(memory_space=pltpu.HBM)
SEM = pl.BlockSpec(memory_space=pltpu.SEMAPHORE)
EFFECT = pltpu.SideEffectType.DATAFLOW_SIDE_EFFECTING
N_PEER = N_DEV - 1
RELATIONS = [(dx, dy, dc) for dx in (0, 1) for dy in (0, 1) for dc in (0, 1)][1:]


def _peer(place, rel):
    return tuple(1 - v if d else v for v, d in zip(place, rel))


def gather_start(name, shards, me, before):
    n = len(shards)

    def body(*refs):
        x_refs, land_refs = refs[:n], refs[n:2 * n]
        send_sems, recv_sems = refs[2 * n + len(before):2 * n + len(before) + 2]
        token = refs[-1]
        place = _place()
        mine = _slot(*place)
        for a in range(n):
            for rel in RELATIONS:
                pltpu.make_async_remote_copy(
                    src_ref=x_refs[a], dst_ref=land_refs[a].at[mine], send_sem=send_sems.at[a], recv_sem=recv_sems.at[a],
                    device_id=_peer(place, rel), device_id_type=MESH).start()
        token[...] = jnp.zeros_like(token)

    lands = [lax.dynamic_update_index_in_dim(lax.empty((N_DEV, *s.shape), s.dtype), s, me, 0) for s in shards]
    outs = pl.pallas_call(
        body, name=name,
        in_specs=[HBM] * (2 * n) + [ANY] * len(before),
        out_specs=[SEM, SEM] + [HBM] * (2 * n) + [pl.BlockSpec(memory_space=pltpu.VMEM)],
        out_shape=[pltpu.SemaphoreType.DMA((n,)), pltpu.SemaphoreType.DMA((n,))]
        + [pltpu.HBM(t.shape, t.dtype) for t in (*shards, *lands)] + [jax.ShapeDtypeStruct((8, LANE), F32)],
        input_output_aliases={i: 2 + i for i in range(2 * n)},
        compiler_params=pltpu.CompilerParams(has_side_effects=EFFECT),
    )(*[pltpu.with_memory_space_constraint(t, pltpu.HBM) for t in (*shards, *lands)], *before)
    return (outs[0], outs[1], outs[2:2 + n], outs[2 + n:2 + 2 * n]), outs[-1]


def gather_wait(name, state, after):
    send_sems, recv_sems, shards, lands = state
    n = len(shards)

    def body(*refs):
        land_refs = refs[n:2 * n]
        s_sems, r_sems = refs[2 * n:2 * n + 2]
        place = _place()
        for a in range(n):
            seven = land_refs[a].at[pl.ds(0, N_PEER)]
            cp = pltpu.make_async_remote_copy(
                src_ref=seven, dst_ref=seven, send_sem=s_sems.at[a], recv_sem=r_sems.at[a], device_id=place, device_id_type=MESH)
            cp.wait_send()
            cp.wait_recv()

    outs = pl.pallas_call(
        body, name=name,
        in_specs=[HBM] * (2 * n) + [SEM, SEM] + [ANY] * len(after), out_specs=[HBM] * (2 * n),
        out_shape=[pltpu.HBM(t.shape, t.dtype) for t in (*shards, *lands)],
        input_output_aliases={i: i for i in range(2 * n)},
        compiler_params=pltpu.CompilerParams(has_side_effects=EFFECT),
    )(*shards, *lands, send_sems, recv_sems, *after)
    return outs[n:]


def exchange_start(name, grads, before):
    n = len(grads)

    def body(*refs):
        g_refs, land_refs = refs[:n], refs[n:2 * n]
        send_sems, recv_sems = refs[2 * n + 1:2 * n + 3]
        token = refs[-1]
        place = _place()
        for a in range(n):
            for r, rel in enumerate(RELATIONS):
                p = _peer(place, rel)
                pltpu.make_async_remote_copy(
                    src_ref=g_refs[a].at[_slot(*p)], dst_ref=land_refs[a].at[r],
                    send_sem=send_sems.at[a], recv_sem=recv_sems.at[a], device_id=p, device_id_type=MESH).start()
        token[...] = jnp.zeros_like(token)

    lands = [lax.empty((N_PEER, *g.shape[1:]), g.dtype) for g in grads]
    outs = pl.pallas_call(
        body, name=name,
        in_specs=[HBM] * (2 * n) + [ANY],
        out_specs=[SEM, SEM] + [HBM] * (2 * n) + [pl.BlockSpec(memory_space=pltpu.VMEM)],
        out_shape=[pltpu.SemaphoreType.DMA((n,)), pltpu.SemaphoreType.DMA((n,))]
        + [pltpu.HBM(g.shape, g.dtype) for g in grads] + [pltpu.HBM(t.shape, t.dtype) for t in lands]
        + [jax.ShapeDtypeStruct((8, LANE), F32)],
        input_output_aliases={i: 2 + i for i in range(2 * n)},
        compiler_params=pltpu.CompilerParams(has_side_effects=EFFECT),
    )(*[pltpu.with_memory_space_constraint(t, pltpu.HBM) for t in (*grads, *lands)], before)
    return (outs[0], outs[1], outs[2:2 + n], outs[2 + n:2 + 2 * n]), outs[-1]


def exchange_wait(name, state, after):
    send_sems, recv_sems, grads, lands = state
    n = len(grads)

    def body(*refs):
        g_refs, land_refs = refs[:n], refs[n:2 * n]
        s_sems, r_sems = refs[2 * n:2 * n + 2]
        place = _place()
        for a in range(n):
            cp = pltpu.make_async_remote_copy(
                src_ref=g_refs[a].at[pl.ds(0, N_PEER)], dst_ref=land_refs[a],
                send_sem=s_sems.at[a], recv_sem=r_sems.at[a], device_id=place, device_id_type=MESH)
            cp.wait_send()
            cp.wait_recv()

    outs = pl.pallas_call(
        body, name=name,
        in_specs=[HBM] * (2 * n) + [SEM, SEM, ANY], out_specs=[HBM] * (2 * n),
        out_shape=[pltpu.HBM(t.shape, t.dtype) for t in (*grads, *lands)],
        input_output_aliases={i: i for i in range(2 * n)},
        compiler_params=pltpu.CompilerParams(has_side_effects=EFFECT),
    )(*grads, *lands, send_sems, recv_sems, after)
    return outs[:n], outs[n:]


WEIGHTS = ['norm_mix_w', 'w_in', 'hg_lb_raw', 'hg_norm_w', 'cv_dw_w', 'cv_dw_b', 'cv_ln_w', 'cv_ln_b', 'pl_w', 'pl_scale',
           'lru_conv_w', 'lru_conv_b', 'lru_wa', 'lru_ba', 'lru_wx', 'lru_bx', 'lru_lambda', 'gate_b', 'w_branch', 'w_out',
           'norm_mem_w', 'mem_norm_w', 'xa_wq', 'xa_wkv', 'xa_wo', 'norm_ffn_w', 'ffn_w1', 'ffn_w2', 'final_norm_w']
BIG = ('w_in', 'w_branch', 'w_out', 'xa_wq', 'xa_wkv', 'xa_wo', 'ffn_w1', 'ffn_w2')
SMALL_SHARDED = ('cv_dw_w', 'lru_conv_w', 'gate_b')
SMALL = tuple(n for n in WEIGHTS if n not in BIG and n not in SMALL_SHARDED)
PACK_ROWS = 256


def _pack(arrs):
    flat = jnp.concatenate([a.reshape(-1).astype(F32) for a in arrs])
    tile = PACK_ROWS * LANE
    padded = -(-flat.shape[0] // tile) * tile
    return jnp.pad(flat, (0, padded - flat.shape[0])).reshape(-1, LANE)


def _unpack(packed, shapes):
    flat = packed.reshape(-1)
    out, off = [], 0
    for s in shapes:
        n = math.prod(s)
        out.append(flat[off:off + n].reshape(s))
        off += n
    return out


def _gather_last(g, shard_shape):
    nd = len(shard_shape)
    full = jnp.moveaxis(g, 0, nd - 1)
    return full.reshape(*shard_shape[:-1], N_DEV * shard_shape[-1])


def _natural(blocks):
    nb, k, c = blocks.shape
    return jnp.transpose(blocks, (1, 0, 2)).reshape(k, nb * c)


def _blocked(mat):
    k, n = mat.shape
    return jnp.transpose(mat.reshape(k, N_DEV, n // N_DEV), (1, 0, 2))


def _block_diag(w):
    w2 = w.reshape(4, 2, 64, 64)
    z = jnp.zeros((4, 64, 64), w.dtype)
    return jnp.concatenate([jnp.concatenate([w2[:, 0], z], axis=2), jnp.concatenate([z, w2[:, 1]], axis=2)], axis=1)


def _block_diag_t(d):
    return jnp.stack([d[:, :64, :64], d[:, 64:, 64:]], axis=1).reshape(8, 64, 64)


def _lower_bounds(raw):
    lb = jnp.cumsum(jax.nn.softmax(raw.astype(F32), axis=0), axis=0)
    return lb - lb[0:1]


def _decay_rates(lam):
    return (LRU_C * jax.nn.softplus(-lam.astype(F32))).reshape(DEPTH, BRANCH_W)


def _relu2(acc):
    r = jnp.maximum(acc, 0.0)
    return acc, r * r


def _relu2_grad(acc, u):
    return (acc * 2.0 * jnp.maximum(u, 0.0),)


def _add(acc, e):
    return (acc + e,)


def _layer_fwd(x0, mem, p, g, rest):
    h1 = rms_fwd("rms_mix", x0, p['norm_mix_w'])
    proj = mm_nt("mm_in", h1, g['w_in'], tn=2176)[0]
    more, after = rest(proj)
    g = {**g, **more}
    b_hg, states, o_hg = hgrn_fwd(proj, p['lb'], p['hg_norm_w'], after=after)
    zc = cv_fwd(proj, p['cv_w32'], p['cv_dw_b'])
    b_cv = ln_silu_fwd(zc, p['cv_ln_w'], p['cv_ln_b'])
    b_pl = pool_fwd(proj, p['pl_w'], p['pl_scale'])
    b_lru, hst = lru_fwd(proj, p['lru_cw8'], p['lru_conv_b'], p['wa_bd'], p['lru_ba'], p['wx_bd'], p['lru_bx'], p['sp8'])
    branches = [b_hg, b_cv, b_pl, b_lru]
    ups = [mm_nn("mm_up", branches[k], g['w_branch'][k], tm=2048)[0] for k in range(N_BRANCH)]
    merged = merge_fwd(ups, proj, p['gate_b'])
    x1 = mm_nn("mm_out", merged, g['w_out'], epi=_add, extras=(x0,))[0]
    h2 = rms_fwd("rms_mem", x1, p['norm_mem_w'])
    q = mm_nn("mm_q", h2, g['xa_wq'], out_dtype=BF16)[0]
    memn = rms_fwd("rms_memtok", mem, p['mem_norm_w'])
    kv = mm_nn("mm_kv", memn, g['xa_wkv'], out_dtype=BF16, tn=2048)[0]
    oa = attn_fwd(q, kv)
    x2 = mm_nn("mm_o", oa, g['xa_wo'], epi=_add, extras=(x1,))[0]
    h3 = rms_fwd("rms_ffn", x2, p['norm_ffn_w'])
    u, act = mm_nn("mm_ffn1", h3, g['ffn_w1'], epi=_relu2, out_dtypes=[F32, BF16])
    x3 = mm_nn("mm_ffn2", act, g['ffn_w2'], epi=_add, extras=(x2,))[0]
    res = dict(x0=x0, h1=h1, proj=proj, states=states, o_hg=o_hg, zc=zc, hst=hst, branches=branches, ups=ups, merged=merged,
               x1=x1, h2=h2, q=q, memn=memn, kv=kv, oa=oa, x2=x2, h3=h3, u=u, act=act)
    return x3, res, g


def _layer_bwd(dx3, mem, p, g, r, after=(), midway=None):
    gs, gb = {}, {}
    du = mm_nt("mm_dffn2", dx3, g['ffn_w2'], out_dtype=BF16, epi=_relu2_grad, extras=(r['u'],), after=after)[0]
    gb['ffn_w2'] = mm_tn("mm_gw2", r['act'], dx3).reshape(N_DEV, -1, D_MODEL)
    gb['ffn_w1'] = mm_tn_cb("mm_gw1", r['h3'], du, N_DEV)
    dh3 = mm_nt("mm_dffn1", du, g['ffn_w1'], out_dtype=BF16)[0]
    dx2, gs['norm_ffn_w'] = rms_bwd("rmsb_ffn", r['x2'], p['norm_ffn_w'], dh3, dx3)
    doa = mm_nt("mm_do", dx2, g['xa_wo'], out_dtype=BF16)[0]
    gb['xa_wo'] = mm_tn("mm_gwo", r['oa'], dx2).reshape(N_DEV, -1, D_MODEL)
    dq, dkv = attn_bwd(r['q'], r['kv'], doa)
    gb['xa_wq'] = mm_tn("mm_gwq", r['h2'], dq).reshape(N_DEV, -1, D_MODEL)
    dh2 = mm_nt("mm_dq", dq, g['xa_wq'], out_dtype=BF16)[0]
    gb['xa_wkv'] = mm_tn_cb("mm_gwkv", r['memn'], dkv, N_DEV)
    dmemn = mm_nt("mm_dkv", dkv, g['xa_wkv'], out_dtype=BF16)[0]
    _, gs['mem_norm_w'] = rms_bwd("rmsb_memtok", mem, p['mem_norm_w'], dmemn)
    dx1, gs['norm_mem_w'] = rms_bwd("rmsb_mem", r['x1'], p['norm_mem_w'], dh2, dx2)
    after = midway(gb, dx1) if midway is not None else ()
    gb = {}
    dmerged = mm_nt("mm_dout", dx1, g['w_out'], out_dtype=BF16, after=after)[0]
    gb['w_out'] = mm_tn("mm_gwout", r['merged'], dx1).reshape(N_DEV, -1, D_MODEL)
    dups, dgates, gs['gate_b'] = merge_bwd(dmerged, r['ups'], r['proj'], p['gate_b'])
    for k in range(N_BRANCH):
        gb[f'w_branch{k}'] = _blocked(mm_tn("mm_gwb", r['branches'][k], dups[k], tk=2048))
    db = [mm_nt("mm_dup", dups[k], g['w_branch'][k], out_dtype=BF16, tm=2048)[0] for k in range(N_BRANCH)]
    dq_, df_, dv_, dg_, gs['lb'], gs['hg_norm_w'] = hgrn_bwd(r['proj'], p['lb'], p['hg_norm_w'], r['states'], r['o_hg'], db[0])
    dzc, gs['cv_ln_w'], gs['cv_ln_b'] = ln_silu_bwd(r['zc'], p['cv_ln_w'], p['cv_ln_b'], db[1])
    dca, dcg, dcw, gs['cv_dw_b'] = cv_bwd(r['proj'], p['cv_w32'], dzc)
    gs['cv_dw_w'] = dcw[:CV_KERNEL]
    dpu, gs['pl_w'], gs['pl_scale'] = pool_bwd(r['proj'], p['pl_w'], p['pl_scale'], db[2])
    dlx, dly, dlcw, gs['lru_conv_b'], dwa, gs['lru_ba'], dwx, gs['lru_bx'], gs['sp8'] = lru_bwd(
        r['proj'], p['lru_cw8'], p['lru_conv_b'], p['wa_bd'], p['lru_ba'], p['wx_bd'], p['lru_bx'], p['sp8'], r['hst'], db[3])
    gs['lru_conv_w'] = dlcw[:LRU_CONV]
    gs['lru_wa'], gs['lru_wx'] = _block_diag_t(dwa), _block_diag_t(dwx)
    gs['lru_ba'], gs['lru_bx'] = gs['lru_ba'].reshape(8, 64), gs['lru_bx'].reshape(8, 64)
    dproj = jnp.concatenate([dq_, df_, dv_, dg_, dca, dcg, dpu, dlx, dly, *dgates], axis=1)
    gb['w_in'] = mm_tn("mm_gwin", dproj, r['h1'], tm=2176).reshape(N_DEV, -1, D_MODEL)
    dh1 = mm_nn("mm_din", dproj, g['w_in'], out_dtype=BF16, tk=2176)[0]
    dx0, gs['norm_mix_w'] = rms_bwd("rmsb_mix", r['x0'], p['norm_mix_w'], dh1, dx1)
    return dx0, gs, gb


def kernel(x, mem, norm_mix_w, w_in, hg_lb_raw, hg_norm_w, cv_dw_w, cv_dw_b, cv_ln_w, cv_ln_b, pl_w, pl_scale, lru_conv_w, lru_conv_b, lru_wa, lru_ba, lru_wx, lru_bx, lru_lambda, gate_b, w_branch, w_out, norm_mem_w, mem_norm_w, xa_wq, xa_wkv, xa_wo, norm_ffn_w, ffn_w1, ffn_w2, final_norm_w, loss_target, m_norm_mix_w, m_w_in, m_hg_lb_raw, m_hg_norm_w, m_cv_dw_w, m_cv_dw_b, m_cv_ln_w, m_cv_ln_b, m_pl_w, m_pl_scale, m_lru_conv_w, m_lru_conv_b, m_lru_wa, m_lru_ba, m_lru_wx, m_lru_bx, m_lru_lambda, m_gate_b, m_w_branch, m_w_out, m_norm_mem_w, m_mem_norm_w, m_xa_wq, m_xa_wkv, m_xa_wo, m_norm_ffn_w, m_ffn_w1, m_ffn_w2, m_final_norm_w, v_norm_mix_w, v_w_in, v_hg_lb_raw, v_hg_norm_w, v_cv_dw_w, v_cv_dw_b, v_cv_ln_w, v_cv_ln_b, v_pl_w, v_pl_scale, v_lru_conv_w, v_lru_conv_b, v_lru_wa, v_lru_ba, v_lru_wx, v_lru_bx, v_lru_lambda, v_gate_b, v_w_branch, v_w_out, v_norm_mem_w, v_mem_norm_w, v_xa_wq, v_xa_wkv, v_xa_wo, v_norm_ffn_w, v_ffn_w1, v_ffn_w2, v_final_norm_w):
    W = dict(zip(WEIGHTS, (norm_mix_w, w_in, hg_lb_raw, hg_norm_w, cv_dw_w, cv_dw_b, cv_ln_w, cv_ln_b, pl_w, pl_scale, lru_conv_w, lru_conv_b, lru_wa, lru_ba, lru_wx, lru_bx, lru_lambda, gate_b, w_branch, w_out, norm_mem_w, mem_norm_w, xa_wq, xa_wkv, xa_wo, norm_ffn_w, ffn_w1, ffn_w2, final_norm_w)))
    Mo = dict(zip(WEIGHTS, (m_norm_mix_w, m_w_in, m_hg_lb_raw, m_hg_norm_w, m_cv_dw_w, m_cv_dw_b, m_cv_ln_w, m_cv_ln_b, m_pl_w, m_pl_scale, m_lru_conv_w, m_lru_conv_b, m_lru_wa, m_lru_ba, m_lru_wx, m_lru_bx, m_lru_lambda, m_gate_b, m_w_branch, m_w_out, m_norm_mem_w, m_mem_norm_w, m_xa_wq, m_xa_wkv, m_xa_wo, m_norm_ffn_w, m_ffn_w1, m_ffn_w2, m_final_norm_w)))
    Vo = dict(zip(WEIGHTS, (v_norm_mix_w, v_w_in, v_hg_lb_raw, v_hg_norm_w, v_cv_dw_w, v_cv_dw_b, v_cv_ln_w, v_cv_ln_b, v_pl_w, v_pl_scale, v_lru_conv_w, v_lru_conv_b, v_lru_wa, v_lru_ba, v_lru_wx, v_lru_bx, v_lru_lambda, v_gate_b, v_w_branch, v_w_out, v_norm_mem_w, v_mem_norm_w, v_xa_wq, v_xa_wkv, v_xa_wo, v_norm_ffn_w, v_ffn_w1, v_ffn_w2, v_final_norm_w)))
    me = _slot(*_place())
    xs, mems, target = x[0], mem[0], loss_target[0]

    shard_shapes = [W[n].shape for n in SMALL_SHARDED]
    gathered = all_gather("ag_small", [_pack([W[n] for n in SMALL_SHARDED])])[0]
    parts = [jnp.stack(ps) for ps in zip(*[_unpack(gathered[d], shard_shapes) for d in range(N_DEV)])]
    full_small = {n: _gather_last(parts[i], shard_shapes[i]) for i, n in enumerate(SMALL_SHARDED)}
    lb_all, lb_vjp = jax.vjp(_lower_bounds, hg_lb_raw)
    sp8_all, sp8_vjp = jax.vjp(_decay_rates, lru_lambda)

    def layer_params(l):
        p = {n: W[n][l] for n in SMALL if n != 'final_norm_w'}
        p['lb'] = lb_all[l]
        p['sp8'] = sp8_all[l]
        p['cv_w32'] = jnp.pad(full_small['cv_dw_w'][l], ((0, 32 - CV_KERNEL), (0, 0)))
        p['lru_cw8'] = jnp.pad(full_small['lru_conv_w'][l], ((0, 8 - LRU_CONV), (0, 0)))
        p['gate_b'] = full_small['gate_b'][l]
        p['wa_bd'], p['wx_bd'] = _block_diag(lru_wa[l]), _block_diag(lru_wx[l])
        p['lru_ba'], p['lru_bx'] = lru_ba[l].reshape(-1), lru_bx[l].reshape(-1)
        return p

    def shards_of(l):
        first = [jnp.transpose(w_in[l]).astype(BF16)]
        others = [w_branch[l, k].astype(BF16) for k in range(N_BRANCH)]
        others += [w[l].astype(BF16) for w in (w_out, xa_wq, xa_wkv, xa_wo, ffn_w1, ffn_w2)]
        return first, others

    def start_gather(l, before):
        first, others = shards_of(l)
        state_a, tok_a = gather_start(f"ag_start{l}a", first, me, before)
        state_b, tok_b = gather_start(f"ag_start{l}b", others, me, (*before, tok_a))
        return state_a, state_b, (tok_a, tok_b)

    def first_of(o):
        return dict(w_in=o[0].reshape(IN_W, D_MODEL))

    def others_of(o):
        return dict(w_branch=[_natural(t) for t in o[0:4]], w_out=o[4].reshape(D_MODEL, D_MODEL),
                    xa_wq=o[5].reshape(D_MODEL, D_MODEL), xa_wkv=_natural(o[6]), xa_wo=o[7].reshape(D_MODEL, D_MODEL),
                    ffn_w1=_natural(o[8]), ffn_w2=o[9].reshape(D_FF, D_MODEL))

    params = [layer_params(l) for l in range(DEPTH)]
    mats, residuals = [], []
    xc = xs
    first, others = shards_of(0)
    whole = all_gather("ag_layer0", first + others)
    gathers = {}
    for l in range(DEPTH):
        if l == 0:
            g_first, g_others = first_of(whole[:1]), others_of(whole[1:])
        else:
            state_a, state_b, _ = gathers.pop(l)
            g_first = first_of(gather_wait(f"ag_wait{l}a", state_a, (xc,)))

        def rest(proj, l=l):
            more = g_others if l == 0 else others_of(gather_wait(f"ag_wait{l}b", state_b, (proj,)))
            if l + 1 == DEPTH:
                return more, ()
            gathers[l + 1] = start_gather(l + 1, (more['w_out'],))
            return more, gathers[l + 1][2]

        xc, res, g = _layer_fwd(xc, mems, params[l], g_first, rest)
        mats.append(g)
        residuals.append(res)
    loss_part, dx, g_final = loss_head(xc, final_norm_w, target)
    loss = lax.psum(loss_part, ("x", "y", "c"))

    small_grads = [None] * DEPTH
    big_grads = [{} for _ in range(DEPTH)]
    pending = []

    def send(l, group, blocks, before):
        names = list(blocks)
        state, tok = exchange_start(f"rs_start{l}{group}", [blocks[n] for n in names], before)
        pending.append((l, group, names, state))
        return (tok,)

    def land(after):
        l, group, names, state = pending.pop(0)
        sent, landed = exchange_wait(f"rs_wait{l}{group}", state, after)
        for n, s, t in zip(names, sent, landed):
            own = lax.dynamic_index_in_dim(s, me, 0, keepdims=False).reshape(-1, s.shape[-1])
            big_grads[l][n] = sum_parts(t.reshape(N_PEER, -1, t.shape[-1]), own).reshape(t.shape[1:])

    token = ()
    for l in reversed(range(DEPTH)):
        dx, gs, gb = _layer_bwd(dx, mems, params[l], mats[l], residuals[l], after=token,
                                midway=lambda top, dx1, l=l: send(l, "a", top, dx1))
        small_grads[l] = gs
        while pending[0][0] > l:
            land(dx)
        token = send(l, "b", gb, dx)

    def stacked(n):
        return jnp.stack([small_grads[l][n] for l in range(DEPTH)])

    part = {n: stacked(n) for n in SMALL if n not in ('final_norm_w', 'hg_lb_raw', 'lru_lambda')}
    part['final_norm_w'] = g_final
    part['hg_lb_raw'] = lb_vjp(stacked('lb'))[0]
    part['lru_lambda'] = sp8_vjp(stacked('sp8'))[0]
    for n in SMALL_SHARDED:
        part[n] = stacked(n)
    names = list(SMALL) + list(SMALL_SHARDED)
    full_shapes = [part[n].shape for n in names]
    everyone = all_gather("ag_grads", [_pack([part[n] for n in names])], after=token)[0]
    total = sum_parts(everyone)
    while pending:
        land(total)

    G = {}
    G['w_in'] = jnp.stack([jnp.transpose(big_grads[l]['w_in']) for l in range(DEPTH)])
    G['w_branch'] = jnp.stack([jnp.stack([big_grads[l][f'w_branch{k}'] for k in range(N_BRANCH)]) for l in range(DEPTH)])
    for n in ('w_out', 'xa_wq', 'xa_wkv', 'xa_wo', 'ffn_w1', 'ffn_w2'):
        G[n] = jnp.stack([big_grads[l][n] for l in range(DEPTH)])
    for n, t in zip(names, _unpack(total, full_shapes)):
        if n in SMALL_SHARDED:
            c = t.shape[-1] // N_DEV
            t = lax.dynamic_slice_in_dim(t, me * c, c, axis=t.ndim - 1)
        G[n] = t

    delta, new_m, new_v = {}, {}, {}
    for n in BIG:
        c = W[n].shape[-1]
        d, nm, nv = adamw(W[n].reshape(-1, c), G[n].reshape(-1, c), Mo[n].reshape(-1, c), Vo[n].reshape(-1, c))
        delta[n], new_m[n], new_v[n] = d.reshape(W[n].shape), nm.reshape(W[n].shape), nv.reshape(W[n].shape)
    shapes = [W[n].shape for n in names]
    d, nm, nv = adamw(_pack([W[n] for n in names]), _pack([G[n] for n in names]), _pack([Mo[n] for n in names]), _pack([Vo[n] for n in names]))
    for n, a, b, c in zip(names, _unpack(d, shapes), _unpack(nm, shapes), _unpack(nv, shapes)):
        delta[n], new_m[n], new_v[n] = a, b, c
    return (loss, dx[None], *[G[n] for n in WEIGHTS], *[delta[n] for n in WEIGHTS],
            *[new_m[n] for n in WEIGHTS], *[new_v[n] for n in WEIGHTS])
```

```python
import functools
import math

import jax
import jax.numpy as jnp
from jax import lax
from jax.experimental import pallas as pl
from jax.experimental.pallas import tpu as pltpu

F32 = jnp.float32
BF16 = jnp.bfloat16
I32 = jnp.int32

N_DEV = 8
D_MODEL = 1024
DEPTH = 4
CHUNK = 64
EPS = 1e-6
HG_HEADS = 4
BRANCH_W = 512
CV_KERNEL = 31
POOL_WINDOWS = (2, 4, 8, 16)
LRU_CONV = 4
LRU_C = 8.0
XA_HEADS = 4
XA_HD = D_MODEL // XA_HEADS
D_FF = 4 * D_MODEL
IN_W = 8704
OFF_Q, OFF_F, OFF_V, OFF_G, OFF_CV, OFF_PL, OFF_LX, OFF_LY, OFF_GATE = 0, 512, 1024, 1536, 2048, 3072, 3584, 4096, 4608
LANE = 128
ADAM_LR, ADAM_B1, ADAM_B2, ADAM_EPS, ADAM_WD, ADAM_STEP = 0.001, 0.9, 0.999, 1e-08, 0.01, 10
VMEM_LIMIT = 56 * 1024 * 1024
MESH = pl.DeviceIdType.MESH
NEG = -1e30
ANY_SPACE = pl.BlockSpec(memory_space=pl.ANY)


def _cp(sem, **kw):
    return pltpu.CompilerParams(dimension_semantics=sem, vmem_limit_bytes=VMEM_LIMIT, **kw)


def _sigmoid(x):
    return 1.0 / (1.0 + jnp.exp(-x))


def _dsilu(x, s):
    return s * (1.0 + x * (1.0 - s))


def _dot(a, b, cdims, precision=None):
    return lax.dot_general(a, b, (cdims, ((), ())), preferred_element_type=F32, precision=precision)


NN = ((1,), (0,))
NT = ((1,), (1,))
TN = ((0,), (0,))


def _mm(name, a, b, *, grid, a_spec, b_spec, o_specs, out_shapes, acc_shape, cdims, epi=None, extras=(), extra_specs=(), after=()):
    nk = grid[2]
    n_e, n_o = len(extras), len(out_shapes)
    extras = (*extras, *after)
    extra_specs = (*extra_specs, *[ANY_SPACE] * len(after))

    def body(*refs):
        a_ref, b_ref = refs[0], refs[1]
        e_refs = refs[2:2 + n_e]
        o_refs = refs[2 + len(extras):2 + len(extras) + n_o]

        def finish(acc):
            vals = epi(acc, *[r[...] for r in e_refs]) if epi is not None else (acc,)
            for r, v in zip(o_refs, vals):
                r[...] = v.astype(r.dtype)

        part = _dot(a_ref[...].astype(BF16), b_ref[...].astype(BF16), cdims)
        if nk == 1:
            finish(part)
        else:
            acc_ref = refs[-1]
            k = pl.program_id(2)

            @pl.when(k == 0)
            def _():
                acc_ref[...] = part

            @pl.when(k > 0)
            def _():
                acc_ref[...] += part

            @pl.when(k == nk - 1)
            def _():
                finish(acc_ref[...])

    return pl.pallas_call(
        body, name=name, grid=grid,
        in_specs=[a_spec, b_spec, *extra_specs], out_specs=list(o_specs), out_shape=list(out_shapes),
        scratch_shapes=[] if nk == 1 else [pltpu.VMEM(acc_shape, F32)],
        compiler_params=_cp(("parallel", "parallel", "arbitrary")),
    )(a, b, *extras)


def _tile(n, pref):
    t = min(n, pref)
    while n % t:
        t //= 2
    return t


def mm_nt(name, a, b, out_dtype=F32, epi=None, extras=(), n_out=1, out_dtypes=None, tm=1024, tn=1024, tk=2048, after=()):
    M, K = a.shape
    N = b.shape[0]
    tm, tn, tk = _tile(M, tm), _tile(N, tn), _tile(K, tk)
    odt = out_dtypes or [out_dtype] * n_out
    o_spec = pl.BlockSpec((tm, tn), lambda i, j, k: (i, j))
    return _mm(name, a, b, grid=(M // tm, N // tn, K // tk),
               a_spec=pl.BlockSpec((tm, tk), lambda i, j, k: (i, k)),
               b_spec=pl.BlockSpec((tn, tk), lambda i, j, k: (j, k)),
               o_specs=[o_spec] * len(odt), out_shapes=[jax.ShapeDtypeStruct((M, N), d) for d in odt],
               acc_shape=(tm, tn), cdims=NT, epi=epi, extras=extras, extra_specs=[o_spec] * len(extras), after=after)


def mm_nn(name, a, b, out_dtype=F32, epi=None, extras=(), n_out=1, out_dtypes=None, tm=1024, tn=1024, tk=2048, after=()):
    M, K = a.shape
    N = b.shape[1]
    tm, tn, tk = _tile(M, tm), _tile(N, tn), _tile(K, tk)
    odt = out_dtypes or [out_dtype] * n_out
    o_spec = pl.BlockSpec((tm, tn), lambda i, j, k: (i, j))
    return _mm(name, a, b, grid=(M // tm, N // tn, K // tk),
               a_spec=pl.BlockSpec((tm, tk), lambda i, j, k: (i, k)),
               b_spec=pl.BlockSpec((tk, tn), lambda i, j, k: (k, j)),
               o_specs=[o_spec] * len(odt), out_shapes=[jax.ShapeDtypeStruct((M, N), d) for d in odt],
               acc_shape=(tm, tn), cdims=NN, epi=epi, extras=extras, extra_specs=[o_spec] * len(extras), after=after)


def mm_tn(name, a, b, out_dtype=BF16, tm=1024, tn=1024, tk=1024):
    K, M = a.shape
    N = b.shape[1]
    tm, tn, tk = _tile(M, tm), _tile(N, tn), _tile(K, tk)
    return _mm(name, a, b, grid=(M // tm, N // tn, K // tk),
               a_spec=pl.BlockSpec((tk, tm), lambda i, j, k: (k, i)),
               b_spec=pl.BlockSpec((tk, tn), lambda i, j, k: (k, j)),
               o_specs=[pl.BlockSpec((tm, tn), lambda i, j, k: (i, j))],
               out_shapes=[jax.ShapeDtypeStruct((M, N), out_dtype)], acc_shape=(tm, tn), cdims=TN)[0]


def mm_nn_cb(name, a, b, out_dtype=F32, epi=None, out_dtypes=None, tm=1024):
    M, K = a.shape
    nb, _, c = b.shape
    tm = _tile(M, tm)
    odt = out_dtypes or [out_dtype]
    return _mm(name, a, b, grid=(M // tm, nb, 1),
               a_spec=pl.BlockSpec((tm, K), lambda i, j, k: (i, 0)),
               b_spec=pl.BlockSpec((None, K, c), lambda i, j, k: (j, 0, 0)),
               o_specs=[pl.BlockSpec((tm, c), lambda i, j, k: (i, j))] * len(odt),
               out_shapes=[jax.ShapeDtypeStruct((M, nb * c), d) for d in odt], acc_shape=(tm, c), cdims=NN, epi=epi)


def mm_nt_cb(name, a, b, out_dtype=F32, epi=None, extras=(), tm=1024, tn=512):
    M = a.shape[0]
    nb, K, c = b.shape
    tm, tn = _tile(M, tm), _tile(K, tn)
    o_spec = pl.BlockSpec((tm, tn), lambda i, j, k: (i, j))
    return _mm(name, a, b, grid=(M // tm, K // tn, nb),
               a_spec=pl.BlockSpec((tm, c), lambda i, j, k: (i, k)),
               b_spec=pl.BlockSpec((None, tn, c), lambda i, j, k: (k, j, 0)),
               o_specs=[o_spec], out_shapes=[jax.ShapeDtypeStruct((M, K), out_dtype)],
               acc_shape=(tm, tn), cdims=NT, epi=epi, extras=extras, extra_specs=[o_spec] * len(extras))


def mm_tn_cb(name, a, b, nb, out_dtype=BF16, tm=1024, tk=2048):
    K, M = a.shape
    N = b.shape[1]
    c = N // nb
    tm, tk = _tile(M, tm), _tile(K, tk)
    return _mm(name, a, b, grid=(M // tm, nb, K // tk),
               a_spec=pl.BlockSpec((tk, tm), lambda i, j, k: (k, i)),
               b_spec=pl.BlockSpec((tk, c), lambda i, j, k: (k, j)),
               o_specs=[pl.BlockSpec((None, tm, c), lambda i, j, k: (j, i, 0))],
               out_shapes=[jax.ShapeDtypeStruct((nb, M, c), out_dtype)], acc_shape=(tm, c), cdims=TN)[0]


def rms_fwd(name, x, w, out_dtype=BF16, tm=512, after=()):
    S, D = x.shape
    tm = _tile(S, tm)

    def body(x_ref, w_ref, *rest):
        o_ref = rest[-1]
        xv = x_ref[...]
        r = lax.rsqrt(jnp.mean(xv * xv, axis=-1, keepdims=True) + EPS)
        o_ref[...] = (xv * r * w_ref[...]).astype(o_ref.dtype)

    return pl.pallas_call(
        body, name=name, grid=(S // tm,),
        in_specs=[pl.BlockSpec((tm, D), lambda i: (i, 0)), pl.BlockSpec((1, D), lambda i: (0, 0))] + [ANY_SPACE] * len(after),
        out_specs=pl.BlockSpec((tm, D), lambda i: (i, 0)), out_shape=jax.ShapeDtypeStruct((S, D), out_dtype),
        compiler_params=_cp(("parallel",)),
    )(x, w.reshape(1, D), *after)


def rms_bwd(name, x, w, dh, dres=None, tm=512):
    S, D = x.shape
    tm = _tile(S, tm)
    has_res = dres is not None

    def body(*refs):
        if has_res:
            x_ref, w_ref, dh_ref, dres_ref, dx_ref, dw_ref = refs
        else:
            x_ref, w_ref, dh_ref, dx_ref, dw_ref = refs
        xv = x_ref[...]
        dhv = dh_ref[...].astype(F32)
        r = lax.rsqrt(jnp.mean(xv * xv, axis=-1, keepdims=True) + EPS)
        g = dhv * w_ref[...]
        dx = r * g - xv * (r * r * r) * jnp.mean(xv * g, axis=-1, keepdims=True)
        if has_res:
            dx = dx + dres_ref[...]
        dx_ref[...] = dx

        @pl.when(pl.program_id(0) == 0)
        def _():
            dw_ref[...] = jnp.zeros_like(dw_ref)

        dw_ref[...] += jnp.sum(dhv * xv * r, axis=0, keepdims=True)

    row = pl.BlockSpec((tm, D), lambda i: (i, 0))
    vec = pl.BlockSpec((1, D), lambda i: (0, 0))
    args = [x, w.reshape(1, D), dh] + ([dres] if has_res else [])
    dx, dw = pl.pallas_call(
        body, name=name, grid=(S // tm,),
        in_specs=[row, vec, row] + ([row] if has_res else []),
        out_specs=[row, vec], out_shape=[jax.ShapeDtypeStruct((S, D), F32), jax.ShapeDtypeStruct((1, D), F32)],
        compiler_params=_cp(("arbitrary",)),
    )(*args)
    return dx, dw.reshape(D)


def loss_head(x, w, target, tm=512):
    S, D = x.shape
    tm = _tile(S, tm)

    def body(x_ref, w_ref, t_ref, loss_ref, dx_ref, dw_ref):
        xv = x_ref[...]
        wv = w_ref[...]
        r = lax.rsqrt(jnp.mean(xv * xv, axis=-1, keepdims=True) + EPS)
        y = xv * r * wv
        err = y - t_ref[...]
        dy = err * (1.0 / D)
        g = dy * wv
        dx_ref[...] = r * g - xv * (r * r * r) * jnp.mean(xv * g, axis=-1, keepdims=True)

        @pl.when(pl.program_id(0) == 0)
        def _():
            dw_ref[...] = jnp.zeros_like(dw_ref)
            loss_ref[...] = jnp.zeros_like(loss_ref)

        dw_ref[...] += jnp.sum(dy * xv * r, axis=0, keepdims=True)
        part = 0.5 * jnp.sum(jnp.mean(err * err, axis=-1, keepdims=True), axis=0, keepdims=True)
        loss_ref[...] += jnp.broadcast_to(part, loss_ref.shape)

    row = pl.BlockSpec((tm, D), lambda i: (i, 0))
    vec = pl.BlockSpec((1, D), lambda i: (0, 0))
    loss, dx, dw = pl.pallas_call(
        body, name="loss_head", grid=(S // tm,),
        in_specs=[row, vec, row],
        out_specs=[pl.BlockSpec((1, LANE), lambda i: (0, 0)), row, vec],
        out_shape=[jax.ShapeDtypeStruct((1, LANE), F32), jax.ShapeDtypeStruct((S, D), F32), jax.ShapeDtypeStruct((1, D), F32)],
        compiler_params=_cp(("arbitrary",)),
    )(x, w.reshape(1, D), target)
    return loss[0, 0], dx, dw.reshape(D)


SUB = 16
HG_W = HG_HEADS * LANE


def _hg_gates(q, f, lbv):
    sig = _sigmoid(f)
    fg = lbv + (1.0 - lbv) * sig
    sq = _sigmoid(q)
    return sig, fg, 1.0 - fg, sq, q * sq


def _hg_cumsum(logf):
    ri = lax.broadcasted_iota(I32, (CHUNK, CHUNK), 0)
    ci = lax.broadcasted_iota(I32, (CHUNK, CHUNK), 1)
    return _dot((ci <= ri).astype(F32), logf, NN, precision=lax.Precision.HIGHEST)


def _hg_rows():
    return lax.broadcasted_iota(I32, (CHUNK, LANE), 0)


def _hg_below(qf, kk, b, rows):
    blocks, parts = [jnp.zeros((SUB, CHUNK), F32)], []
    for i in range(1, CHUNK // SUB):
        bref = b[SUB * i - 1:SUB * i, :]
        rs = slice(SUB * i, SUB * (i + 1))
        eq = jnp.exp(b[rs] - bref)
        below = rows < SUB * i
        ek = jnp.exp(jnp.where(below, bref - b, NEG))
        qi = (qf[rs] * eq).astype(BF16)
        ki = (kk * ek).astype(BF16)
        blocks.append(_dot(qi, ki, NT))
        parts.append((qi, ki, eq, ek))
    return jnp.concatenate(blocks, axis=0), parts


def hgrn_fwd(proj, lb, nw, after=()):
    S = proj.shape[0]
    NC = S // CHUNK
    H = HG_HEADS

    def body(q_ref, f_ref, v_ref, g_ref, lb_ref, nw_ref, *rest):
        out_ref, st_out_ref, o_ref, st, kk_s, b_s = rest[len(after):]
        c = pl.program_id(0)

        @pl.when(c == 0)
        def _():
            st[...] = jnp.zeros_like(st)

        st_out_ref[...] = st[...]
        sig, fg, kk_all, sq, qf_all = _hg_gates(q_ref[...], f_ref[...], lb_ref[...])
        b_all = _hg_cumsum(jnp.log(fg))
        kk_s[...] = kk_all
        b_s[...] = b_all
        rows = _hg_rows()
        r16 = lax.broadcasted_iota(I32, (SUB, LANE), 0)
        for h in range(H):
            cs = slice(h * LANE, (h + 1) * LANE)
            qf, kk, b, v, g = qf_all[:, cs], kk_all[:, cs], b_all[:, cs], v_ref[:, cs], g_ref[:, cs]
            st_in = st[h]
            diag = []
            for i in range(CHUNK // SUB):
                rs = slice(SUB * i, SUB * (i + 1))
                acc = jnp.zeros((SUB, LANE), F32)
                for j in range(SUB):
                    row = pl.ds(SUB * i + j, 1)
                    e = jnp.exp(jnp.where(r16 >= j, b[rs] - b_s[row, cs], NEG))
                    col = jnp.sum(qf[rs] * (kk_s[row, cs] * e), axis=1, keepdims=True)
                    acc = acc + col * v_ref[row, cs]
                diag.append(acc)
            poff, _ = _hg_below(qf, kk, b, rows)
            vb = v.astype(BF16)
            bl = b[CHUNK - 1:CHUNK, :]
            o = (jnp.concatenate(diag, axis=0) + _dot(poff.astype(BF16), vb, NN)
                 + _dot((qf * jnp.exp(b)).astype(BF16), st_in.astype(BF16), NT))
            st[h] = st_in * jnp.exp(bl) + _dot(vb, (kk * jnp.exp(bl - b)).astype(BF16), TN)
            o_ref[:, cs] = o
            r = lax.rsqrt(jnp.mean(o * o, axis=-1, keepdims=True) + EPS)
            out_ref[:, cs] = (o * r * nw_ref[...] * (g * _sigmoid(g))).astype(out_ref.dtype)

    def seg(off):
        return pl.BlockSpec((CHUNK, HG_W), lambda c: (c, off // HG_W))

    blk = pl.BlockSpec((CHUNK, HG_W), lambda c: (c, 0))
    full = pltpu.VMEM((CHUNK, HG_W), F32)
    return pl.pallas_call(
        body, name="hgrn_fwd", grid=(NC,),
        in_specs=[seg(OFF_Q), seg(OFF_F), seg(OFF_V), seg(OFF_G),
                  pl.BlockSpec((1, HG_W), lambda c: (0, 0)), pl.BlockSpec((1, LANE), lambda c: (0, 0))] + [ANY_SPACE] * len(after),
        out_specs=[blk, pl.BlockSpec((None, H, LANE, LANE), lambda c: (c, 0, 0, 0)), blk],
        out_shape=[jax.ShapeDtypeStruct((S, HG_W), BF16), jax.ShapeDtypeStruct((NC, H, LANE, LANE), F32),
                   jax.ShapeDtypeStruct((S, HG_W), F32)],
        scratch_shapes=[pltpu.VMEM((H, LANE, LANE), F32), full, full],
        compiler_params=_cp(("arbitrary",)),
    )(proj, proj, proj, proj, lb.reshape(1, HG_W), nw.reshape(1, LANE), *after)


def hgrn_bwd(proj, lb, nw, states, o_pre, dout):
    S = proj.shape[0]
    NC = S // CHUNK
    H = HG_HEADS

    def body(q_ref, f_ref, v_ref, g_ref, lb_ref, nw_ref, st_ref, o_ref, do_ref,
             dq_ref, df_ref, dv_ref, dg_ref, dlb_ref, dnw_ref, dst, kk_s, b_s, do_s, db_s, dkk_s, dkk_d, dv_d):
        c = pl.program_id(0)

        @pl.when(c == 0)
        def _():
            dst[...] = jnp.zeros_like(dst)
            dlb_ref[...] = jnp.zeros_like(dlb_ref)
            dnw_ref[...] = jnp.zeros_like(dnw_ref)

        q_all, g_all = q_ref[...], g_ref[...]
        lbv, nwv = lb_ref[...], nw_ref[...]
        sig, fg, kk_all, sq, qf_all = _hg_gates(q_all, f_ref[...], lbv)
        b_all = _hg_cumsum(jnp.log(fg))
        o_all = o_ref[...]
        dov = do_ref[...].astype(F32)
        sg = _sigmoid(g_all)
        gsg = g_all * sg
        dnw_acc = jnp.zeros((1, LANE), F32)
        for h in range(H):
            cs = slice(h * LANE, (h + 1) * LANE)
            o = o_all[:, cs]
            r = lax.rsqrt(jnp.mean(o * o, axis=-1, keepdims=True) + EPS)
            don = dov[:, cs] * gsg[:, cs]
            dnw_acc = dnw_acc + jnp.sum(don * o * r, axis=0, keepdims=True)
            gno = don * nwv
            do_s[:, cs] = r * gno - o * (r * r * r) * jnp.mean(o * gno, axis=-1, keepdims=True)
            dg_ref[:, cs] = (dov[:, cs] * (o * r * nwv) * _dsilu(g_all[:, cs], sg[:, cs])).astype(dg_ref.dtype)
        dnw_ref[...] += jnp.broadcast_to(dnw_acc, dnw_ref.shape)
        kk_s[...] = kk_all
        b_s[...] = b_all
        rows = _hg_rows()
        r16 = lax.broadcasted_iota(I32, (SUB, LANE), 0)
        for h in range(H):
            cs = slice(h * LANE, (h + 1) * LANE)
            qf, kk, b, v = qf_all[:, cs], kk_all[:, cs], b_all[:, cs], v_ref[:, cs]
            do = do_s[:, cs]
            st_in, dstv = st_ref[h], dst[h]
            bl = b[CHUNK - 1:CHUNK, :]
            eb, ebl, el = jnp.exp(b), jnp.exp(bl - b), jnp.exp(bl)
            qe, ke = qf * eb, kk * ebl
            vb, dob, stb, dstb = v.astype(BF16), do.astype(BF16), st_in.astype(BF16), dstv.astype(BF16)
            w_ = _dot(vb, dstb, NN)
            dqf = eb * _dot(dob, stb, NN)
            dkk = ebl * w_
            dv = _dot(ke.astype(BF16), dstb, NT)
            dbl = el * jnp.sum(st_in * dstv, axis=0, keepdims=True) + jnp.sum(ke * w_, axis=0, keepdims=True)
            dst[h] = dstv * el + _dot(dob, qe.astype(BF16), TN)
            poff, parts = _hg_below(qf, kk, b, rows)
            dpoff = _dot(dob, vb, NT).astype(BF16)
            dv = dv + _dot(poff.astype(BF16), dob, TN)
            dq_blocks = [jnp.zeros((SUB, LANE), F32)]
            for i, (qi, ki, eq, ek) in enumerate(parts, start=1):
                dpi = dpoff[SUB * i:SUB * (i + 1), :]
                dq_blocks.append(_dot(dpi, ki, NN) * eq)
                dkk = dkk + _dot(dpi, qi, TN) * ek
            dqf = dqf + jnp.concatenate(dq_blocks, axis=0)
            dq_diag = []
            for i in range(CHUNK // SUB):
                rs = slice(SUB * i, SUB * (i + 1))
                acc = jnp.zeros((SUB, LANE), F32)
                for j in range(SUB):
                    row = pl.ds(SUB * i + j, 1)
                    ks = kk_s[row, cs]
                    e = jnp.exp(jnp.where(r16 >= j, b[rs] - b_s[row, cs], NEG))
                    x = jnp.sum(do[rs] * v_ref[row, cs], axis=1, keepdims=True) * e
                    acc = acc + x * ks
                    dkk_d[row, cs] = jnp.sum(x * qf[rs], axis=0, keepdims=True)
                    col = jnp.sum(qf[rs] * (ks * e), axis=1, keepdims=True)
                    dv_d[row, cs] = jnp.sum(col * do[rs], axis=0, keepdims=True)
                dq_diag.append(acc)
            dqf = dqf + jnp.concatenate(dq_diag, axis=0)
            dkk = dkk + dkk_d[:, cs]
            dv = dv + dv_d[:, cs]
            dv_ref[:, cs] = dv.astype(dv_ref.dtype)
            db = qf * dqf - kk * dkk
            db_s[:, cs] = db + jnp.where(rows == CHUNK - 1, dbl, 0.0)
            dkk_s[:, cs] = dkk
            dq_ref[:, cs] = (dqf * _dsilu(q_all[:, cs], sq[:, cs])).astype(dq_ref.dtype)
        ri = lax.broadcasted_iota(I32, (CHUNK, CHUNK), 0)
        ci = lax.broadcasted_iota(I32, (CHUNK, CHUNK), 1)
        dlogf = _dot((ci >= ri).astype(F32), db_s[...], NN, precision=lax.Precision.HIGHEST)
        dfg = dlogf / fg - dkk_s[...]
        df_ref[...] = (dfg * (1.0 - lbv) * sig * (1.0 - sig)).astype(df_ref.dtype)
        dlb_ref[...] += jnp.broadcast_to(jnp.sum(dfg * (1.0 - sig), axis=0, keepdims=True), dlb_ref.shape)

    def seg(off):
        return pl.BlockSpec((CHUNK, HG_W), lambda c: (NC - 1 - c, off // HG_W))

    blk = pl.BlockSpec((CHUNK, HG_W), lambda c: (NC - 1 - c, 0))
    osd = jax.ShapeDtypeStruct((S, HG_W), BF16)
    full = pltpu.VMEM((CHUNK, HG_W), F32)
    dq, df, dv, dg, dlb, dnw = pl.pallas_call(
        body, name="hgrn_bwd", grid=(NC,),
        in_specs=[seg(OFF_Q), seg(OFF_F), seg(OFF_V), seg(OFF_G),
                  pl.BlockSpec((1, HG_W), lambda c: (0, 0)), pl.BlockSpec((1, LANE), lambda c: (0, 0)),
                  pl.BlockSpec((None, H, LANE, LANE), lambda c: (NC - 1 - c, 0, 0, 0)), blk, blk],
        out_specs=[blk, blk, blk, blk, pl.BlockSpec((8, HG_W), lambda c: (0, 0)), pl.BlockSpec((8, LANE), lambda c: (0, 0))],
        out_shape=[osd, osd, osd, osd, jax.ShapeDtypeStruct((8, HG_W), F32), jax.ShapeDtypeStruct((8, LANE), F32)],
        scratch_shapes=[pltpu.VMEM((H, LANE, LANE), F32)] + [full] * 7,
        compiler_params=_cp(("arbitrary",)),
    )(proj, proj, proj, proj, lb.reshape(1, HG_W), nw.reshape(1, LANE), states, o_pre, dout)
    return dq, df, dv, dg, dlb[0], dnw[0]


CV_PAD = 32
ROWS = 256


def _colblk(S, off):
    return pl.BlockSpec((S, LANE), lambda j: (0, off // LANE + j))


def cv_fwd(proj, w32, bias):
    S = proj.shape[0]
    nchunk = S // ROWS

    def body(a_ref, g_ref, w_ref, b_ref, o_ref, zpad):
        zpad[pl.ds(0, CV_PAD), :] = jnp.zeros((CV_PAD, LANE), F32)

        def glu(c, _):
            r0 = pl.multiple_of(c * ROWS, ROWS)
            zpad[pl.ds(CV_PAD + r0, ROWS), :] = a_ref[pl.ds(r0, ROWS), :] * _sigmoid(g_ref[pl.ds(r0, ROWS), :])
            return 0

        lax.fori_loop(0, nchunk, glu, 0)

        def conv(c, _):
            r0 = pl.multiple_of(c * ROWS, ROWS)
            acc = jnp.broadcast_to(b_ref[...], (ROWS, LANE))
            for j in range(CV_KERNEL):
                acc = acc + w_ref[pl.ds(j, 1), :] * zpad[pl.ds(r0 + (CV_PAD - CV_KERNEL + 1) + j, ROWS), :]
            o_ref[pl.ds(r0, ROWS), :] = acc
            return 0

        lax.fori_loop(0, nchunk, conv, 0)

    return pl.pallas_call(
        body, name="cv_fwd", grid=(BRANCH_W // LANE,),
        in_specs=[_colblk(S, OFF_CV), _colblk(S, OFF_CV + BRANCH_W),
                  pl.BlockSpec((32, LANE), lambda j: (0, j)), pl.BlockSpec((1, LANE), lambda j: (0, j))],
        out_specs=pl.BlockSpec((S, LANE), lambda j: (0, j)), out_shape=jax.ShapeDtypeStruct((S, BRANCH_W), F32),
        scratch_shapes=[pltpu.VMEM((CV_PAD + S, LANE), F32)],
        compiler_params=_cp(("parallel",)),
    )(proj, proj, w32, bias.reshape(1, BRANCH_W))


def cv_bwd(proj, w32, dzc):
    S = proj.shape[0]
    nchunk = S // ROWS

    def body(a_ref, g_ref, w_ref, dz_ref, da_ref, dg_ref, dw_ref, db_ref, zpad, dpad):
        zpad[pl.ds(0, CV_PAD), :] = jnp.zeros((CV_PAD, LANE), F32)
        dpad[pl.ds(S, CV_PAD), :] = jnp.zeros((CV_PAD, LANE), F32)
        dw_ref[...] = jnp.zeros_like(dw_ref)

        def glu(c, dsum):
            r0 = pl.multiple_of(c * ROWS, ROWS)
            zpad[pl.ds(CV_PAD + r0, ROWS), :] = a_ref[pl.ds(r0, ROWS), :] * _sigmoid(g_ref[pl.ds(r0, ROWS), :])
            d = dz_ref[pl.ds(r0, ROWS), :]
            dpad[pl.ds(r0, ROWS), :] = d
            return dsum + jnp.sum(d, axis=0, keepdims=True)

        dsum = lax.fori_loop(0, nchunk, glu, jnp.zeros((1, LANE), F32))
        db_ref[...] = jnp.broadcast_to(dsum, db_ref.shape)

        def conv(c, _):
            r0 = pl.multiple_of(c * ROWS, ROWS)
            d = dpad[pl.ds(r0, ROWS), :]
            acc = jnp.zeros((ROWS, LANE), F32)
            for j in range(CV_KERNEL):
                acc = acc + w_ref[pl.ds(j, 1), :] * dpad[pl.ds(r0 + (CV_KERNEL - 1) - j, ROWS), :]
                zs = zpad[pl.ds(r0 + (CV_PAD - CV_KERNEL + 1) + j, ROWS), :]
                dw_ref[pl.ds(j, 1), :] += jnp.sum(d * zs, axis=0, keepdims=True)
            a = a_ref[pl.ds(r0, ROWS), :]
            sg = _sigmoid(g_ref[pl.ds(r0, ROWS), :])
            da_ref[pl.ds(r0, ROWS), :] = (acc * sg).astype(da_ref.dtype)
            dg_ref[pl.ds(r0, ROWS), :] = (acc * a * sg * (1.0 - sg)).astype(dg_ref.dtype)
            return 0

        lax.fori_loop(0, nchunk, conv, 0)

    blk = pl.BlockSpec((S, LANE), lambda j: (0, j))
    da, dg, dw, db = pl.pallas_call(
        body, name="cv_bwd", grid=(BRANCH_W // LANE,),
        in_specs=[_colblk(S, OFF_CV), _colblk(S, OFF_CV + BRANCH_W), pl.BlockSpec((32, LANE), lambda j: (0, j)), blk],
        out_specs=[blk, blk, pl.BlockSpec((32, LANE), lambda j: (0, j)), pl.BlockSpec((8, LANE), lambda j: (0, j))],
        out_shape=[jax.ShapeDtypeStruct((S, BRANCH_W), BF16), jax.ShapeDtypeStruct((S, BRANCH_W), BF16),
                   jax.ShapeDtypeStruct((32, BRANCH_W), F32), jax.ShapeDtypeStruct((8, BRANCH_W), F32)],
        scratch_shapes=[pltpu.VMEM((CV_PAD + S, LANE), F32), pltpu.VMEM((S + CV_PAD, LANE), F32)],
        compiler_params=_cp(("parallel",)),
    )(proj, proj, w32, dzc)
    return da, dg, dw, db[0]


def ln_silu_fwd(z, w, b, tm=512):
    S, C = z.shape
    tm = _tile(S, tm)

    def body(z_ref, w_ref, b_ref, o_ref):
        zv = z_ref[...]
        mu = jnp.mean(zv, axis=-1, keepdims=True)
        zc = zv - mu
        rstd = lax.rsqrt(jnp.mean(zc * zc, axis=-1, keepdims=True) + EPS)
        y = zc * rstd * w_ref[...] + b_ref[...]
        o_ref[...] = (y * _sigmoid(y)).astype(o_ref.dtype)

    row = pl.BlockSpec((tm, C), lambda i: (i, 0))
    vec = pl.BlockSpec((1, C), lambda i: (0, 0))
    return pl.pallas_call(
        body, name="ln_silu_fwd", grid=(S // tm,), in_specs=[row, vec, vec], out_specs=row,
        out_shape=jax.ShapeDtypeStruct((S, C), BF16), compiler_params=_cp(("parallel",)),
    )(z, w.reshape(1, C), b.reshape(1, C))


def ln_silu_bwd(z, w, b, dout, tm=512):
    S, C = z.shape
    tm = _tile(S, tm)

    def body(z_ref, w_ref, b_ref, do_ref, dz_ref, dw_ref, db_ref):
        zv = z_ref[...]
        wv = w_ref[...]
        mu = jnp.mean(zv, axis=-1, keepdims=True)
        zc = zv - mu
        rstd = lax.rsqrt(jnp.mean(zc * zc, axis=-1, keepdims=True) + EPS)
        xh = zc * rstd
        y = xh * wv + b_ref[...]
        dy = do_ref[...].astype(F32) * _dsilu(y, _sigmoid(y))

        @pl.when(pl.program_id(0) == 0)
        def _():
            dw_ref[...] = jnp.zeros_like(dw_ref)
            db_ref[...] = jnp.zeros_like(db_ref)

        dw_ref[...] += jnp.sum(dy * xh, axis=0, keepdims=True)
        db_ref[...] += jnp.sum(dy, axis=0, keepdims=True)
        dxh = dy * wv
        dz_ref[...] = rstd * (dxh - jnp.mean(dxh, axis=-1, keepdims=True) - xh * jnp.mean(dxh * xh, axis=-1, keepdims=True))

    row = pl.BlockSpec((tm, C), lambda i: (i, 0))
    vec = pl.BlockSpec((1, C), lambda i: (0, 0))
    dz, dw, db = pl.pallas_call(
        body, name="ln_silu_bwd", grid=(S // tm,), in_specs=[row, vec, vec, row], out_specs=[row, vec, vec],
        out_shape=[jax.ShapeDtypeStruct((S, C), F32), jax.ShapeDtypeStruct((1, C), F32), jax.ShapeDtypeStruct((1, C), F32)],
        compiler_params=_cp(("arbitrary",)),
    )(z, w.reshape(1, C), b.reshape(1, C), dout)
    return dz, dw.reshape(C), db.reshape(C)


PL_PAD = 16


def _pool_counts(r0, win):
    t = r0 + lax.broadcasted_iota(I32, (ROWS, LANE), 0)
    return jnp.minimum(t + 1, win).astype(F32)


def pool_fwd(proj, wg, scale):
    S = proj.shape[0]
    nchunk = S // ROWS

    def body(u_ref, w_ref, s_ref, o_ref, upad):
        g = pl.program_id(0)
        upad[pl.ds(0, PL_PAD), :] = jnp.zeros((PL_PAD, LANE), F32)

        def fill(c, _):
            r0 = pl.multiple_of(c * ROWS, ROWS)
            upad[pl.ds(PL_PAD + r0, ROWS), :] = u_ref[pl.ds(r0, ROWS), :]
            return 0

        lax.fori_loop(0, nchunk, fill, 0)
        wb = w_ref[...].astype(BF16)
        for gi, win in enumerate(POOL_WINDOWS):
            @pl.when(g == gi)
            def _(win=win):
                def chunk(c, _):
                    r0 = pl.multiple_of(c * ROWS, ROWS)
                    u = upad[pl.ds(PL_PAD + r0, ROWS), :]
                    ws = u
                    for j in range(1, win):
                        ws = ws + upad[pl.ds(PL_PAD + r0 - j, ROWS), :]
                    pooled = ws / _pool_counts(r0, win) - u
                    o_ref[pl.ds(r0, ROWS), :] = (_dot(pooled.astype(BF16), wb, NN) * s_ref[...]).astype(o_ref.dtype)
                    return 0

                lax.fori_loop(0, nchunk, chunk, 0)

    return pl.pallas_call(
        body, name="pool_fwd", grid=(len(POOL_WINDOWS),),
        in_specs=[_colblk(S, OFF_PL), pl.BlockSpec((None, LANE, LANE), lambda j: (j, 0, 0)), pl.BlockSpec((1, LANE), lambda j: (0, j))],
        out_specs=pl.BlockSpec((S, LANE), lambda j: (0, j)), out_shape=jax.ShapeDtypeStruct((S, BRANCH_W), BF16),
        scratch_shapes=[pltpu.VMEM((PL_PAD + S, LANE), F32)],
        compiler_params=_cp(("parallel",)),
    )(proj, wg, scale.reshape(1, BRANCH_W))


def pool_bwd(proj, wg, scale, dy):
    S = proj.shape[0]
    nchunk = S // ROWS

    def body(u_ref, w_ref, s_ref, dy_ref, du_ref, dw_ref, ds_ref, upad, dpn, nd):
        g = pl.program_id(0)
        upad[pl.ds(0, PL_PAD), :] = jnp.zeros((PL_PAD, LANE), F32)
        dpn[pl.ds(S, PL_PAD), :] = jnp.zeros((PL_PAD, LANE), F32)

        def fill(c, _):
            r0 = pl.multiple_of(c * ROWS, ROWS)
            upad[pl.ds(PL_PAD + r0, ROWS), :] = u_ref[pl.ds(r0, ROWS), :]
            return 0

        lax.fori_loop(0, nchunk, fill, 0)
        wb = w_ref[...].astype(BF16)
        sv = s_ref[...]
        for gi, win in enumerate(POOL_WINDOWS):
            @pl.when(g == gi)
            def _(win=win):
                def chunk(c, carry):
                    dw, dsc = carry
                    r0 = pl.multiple_of(c * ROWS, ROWS)
                    u = upad[pl.ds(PL_PAD + r0, ROWS), :]
                    ws = u
                    for j in range(1, win):
                        ws = ws + upad[pl.ds(PL_PAD + r0 - j, ROWS), :]
                    cnt = _pool_counts(r0, win)
                    pooled = (ws / cnt - u).astype(BF16)
                    dyv = dy_ref[pl.ds(r0, ROWS), :].astype(F32)
                    dsc = dsc + jnp.sum(dyv * _dot(pooled, wb, NN), axis=0, keepdims=True)
                    dys = (dyv * sv).astype(BF16)
                    dw = dw + _dot(pooled, dys, TN)
                    dp = _dot(dys, wb, NT)
                    dpn[pl.ds(r0, ROWS), :] = dp / cnt
                    nd[pl.ds(r0, ROWS), :] = -dp
                    return dw, dsc

                dw, dsc = lax.fori_loop(0, nchunk, chunk, (jnp.zeros((LANE, LANE), F32), jnp.zeros((1, LANE), F32)))
                dw_ref[...] = dw
                ds_ref[...] = jnp.broadcast_to(dsc, ds_ref.shape)

                def spread(c, _):
                    r0 = pl.multiple_of(c * ROWS, ROWS)
                    acc = nd[pl.ds(r0, ROWS), :]
                    for j in range(win):
                        acc = acc + dpn[pl.ds(r0 + j, ROWS), :]
                    du_ref[pl.ds(r0, ROWS), :] = acc.astype(du_ref.dtype)
                    return 0

                lax.fori_loop(0, nchunk, spread, 0)

    blk = pl.BlockSpec((S, LANE), lambda j: (0, j))
    du, dw, ds = pl.pallas_call(
        body, name="pool_bwd", grid=(len(POOL_WINDOWS),),
        in_specs=[_colblk(S, OFF_PL), pl.BlockSpec((None, LANE, LANE), lambda j: (j, 0, 0)), pl.BlockSpec((1, LANE), lambda j: (0, j)), blk],
        out_specs=[blk, pl.BlockSpec((None, LANE, LANE), lambda j: (j, 0, 0)), pl.BlockSpec((8, LANE), lambda j: (0, j))],
        out_shape=[jax.ShapeDtypeStruct((S, BRANCH_W), BF16), jax.ShapeDtypeStruct((len(POOL_WINDOWS), LANE, LANE), F32),
                   jax.ShapeDtypeStruct((8, BRANCH_W), F32)],
        scratch_shapes=[pltpu.VMEM((PL_PAD + S, LANE), F32), pltpu.VMEM((S + PL_PAD, LANE), F32), pltpu.VMEM((S, LANE), F32)],
        compiler_params=_cp(("parallel",)),
    )(proj, wg, scale.reshape(1, BRANCH_W), dy)
    return du, dw, ds[0]


LR_PAD = 8
SCAN_TILES = 4
GELU_C = math.sqrt(2.0 / math.pi)
GELU_A = 0.044715


def _gelu(y):
    return 0.5 * y * (1.0 + jnp.tanh(GELU_C * (y + GELU_A * y * y * y)))


def _dgelu(y):
    t = jnp.tanh(GELU_C * (y + GELU_A * y * y * y))
    return 0.5 * (1.0 + t) + 0.5 * y * (1.0 - t * t) * GELU_C * (1.0 + 3.0 * GELU_A * y * y)


def _lru_gates(xpad, r0, cw_ref, cb, wa, ba, wx, bx, sp8):
    xc = jnp.broadcast_to(cb, (ROWS, LANE))
    for j in range(LRU_CONV):
        xc = xc + cw_ref[pl.ds(j, 1), :] * xpad[pl.ds(r0 + (LR_PAD - LRU_CONV + 1) + j, ROWS), :]
    xb = xc.astype(BF16)
    r = _sigmoid(_dot(xb, wa, NN) + ba)
    ig = _sigmoid(_dot(xb, wx, NN) + bx)
    la = -sp8 * r
    a = jnp.exp(la)
    s = jnp.sqrt(-jnp.tanh(la) * (a * a + 1.0))
    return xc, r, ig, a, s


def _tile_scan(a, b, r8, up):
    for s in (1, 2, 4):
        keep = (r8 < 8 - s) if up else (r8 >= s)
        shift = 8 - s if up else s
        a_sh = jnp.where(keep, pltpu.roll(a, shift, 0), 1.0)
        b_sh = jnp.where(keep, pltpu.roll(b, shift, 0), 0.0)
        b = b + a * b_sh
        a = a * a_sh
    return a, b


def lru_fwd(proj, cw8, cb, wa_bd, ba, wx_bd, bx, sp8):
    S = proj.shape[0]
    nchunk = S // ROWS

    def body(x_ref, y_ref, cw_ref, cb_ref, wa_ref, ba_ref, wx_ref, bx_ref, sp_ref, o_ref, h_ref, xpad, a_s):
        xpad[pl.ds(0, LR_PAD), :] = jnp.zeros((LR_PAD, LANE), F32)

        def fill(c, _):
            r0 = pl.multiple_of(c * ROWS, ROWS)
            xpad[pl.ds(LR_PAD + r0, ROWS), :] = x_ref[pl.ds(r0, ROWS), :]
            return 0

        lax.fori_loop(0, nchunk, fill, 0)
        wa = wa_ref[...].astype(BF16)
        wx = wx_ref[...].astype(BF16)

        def gates(c, _):
            r0 = pl.multiple_of(c * ROWS, ROWS)
            xc, r, ig, a, s = _lru_gates(xpad, r0, cw_ref, cb_ref[...], wa, ba_ref[...], wx, bx_ref[...], sp_ref[...])
            a_s[pl.ds(r0, ROWS), :] = a
            h_ref[pl.ds(r0, ROWS), :] = s * (ig * xc)
            return 0

        lax.fori_loop(0, nchunk, gates, 0)

        r8 = lax.broadcasted_iota(I32, (8, LANE), 0)

        def scan(i, h):
            bases = [pl.multiple_of(i * (8 * SCAN_TILES) + 8 * j, 8) for j in range(SCAN_TILES)]
            maps = [_tile_scan(a_s[pl.ds(b, 8), :], h_ref[pl.ds(b, 8), :], r8, False) for b in bases]
            for b, (ca, cb_) in zip(bases, maps):
                out = cb_ + ca * h
                h_ref[pl.ds(b, 8), :] = out
                h = out[7:8, :]
            return h

        lax.fori_loop(0, S // (8 * SCAN_TILES), scan, jnp.zeros((1, LANE), F32))

        def gate_out(c, _):
            r0 = pl.multiple_of(c * ROWS, ROWS)
            o_ref[pl.ds(r0, ROWS), :] = (h_ref[pl.ds(r0, ROWS), :] * _gelu(y_ref[pl.ds(r0, ROWS), :])).astype(o_ref.dtype)
            return 0

        lax.fori_loop(0, nchunk, gate_out, 0)

    vec = pl.BlockSpec((1, LANE), lambda j: (0, j))
    mat = pl.BlockSpec((None, LANE, LANE), lambda j: (j, 0, 0))
    blk = pl.BlockSpec((S, LANE), lambda j: (0, j))
    return pl.pallas_call(
        body, name="lru_fwd", grid=(BRANCH_W // LANE,),
        in_specs=[_colblk(S, OFF_LX), _colblk(S, OFF_LY), pl.BlockSpec((8, LANE), lambda j: (0, j)), vec, mat, vec, mat, vec, vec],
        out_specs=[blk, blk],
        out_shape=[jax.ShapeDtypeStruct((S, BRANCH_W), BF16), jax.ShapeDtypeStruct((S, BRANCH_W), F32)],
        scratch_shapes=[pltpu.VMEM((LR_PAD + S, LANE), F32), pltpu.VMEM((S, LANE), F32)],
        compiler_params=_cp(("parallel",)),
    )(proj, proj, cw8, cb.reshape(1, -1), wa_bd, ba.reshape(1, -1), wx_bd, bx.reshape(1, -1), sp8.reshape(1, -1))


def lru_bwd(proj, cw8, cb, wa_bd, ba, wx_bd, bx, sp8, h, dout):
    S = proj.shape[0]
    nchunk = S // ROWS

    def body(x_ref, y_ref, cw_ref, cb_ref, wa_ref, ba_ref, wx_ref, bx_ref, sp_ref, h_ref, do_ref,
             dx_ref, dy_ref, dcw_ref, dcb_ref, dwa_ref, dba_ref, dwx_ref, dbx_ref, dsp_ref,
             xpad, a_s, g_s, hpad, dxc):
        xpad[pl.ds(0, LR_PAD), :] = jnp.zeros((LR_PAD, LANE), F32)
        hpad[pl.ds(0, LR_PAD), :] = jnp.zeros((LR_PAD, LANE), F32)
        dxc[pl.ds(S, LR_PAD), :] = jnp.zeros((LR_PAD, LANE), F32)
        dcw_ref[...] = jnp.zeros_like(dcw_ref)
        wa = wa_ref[...].astype(BF16)
        wx = wx_ref[...].astype(BF16)
        cbv, bav, bxv, spv = cb_ref[...], ba_ref[...], bx_ref[...], sp_ref[...]

        def fill(c, _):
            r0 = pl.multiple_of(c * ROWS, ROWS)
            xpad[pl.ds(LR_PAD + r0, ROWS), :] = x_ref[pl.ds(r0, ROWS), :]
            hv = h_ref[pl.ds(r0, ROWS), :]
            hpad[pl.ds(LR_PAD + r0, ROWS), :] = hv
            yv = y_ref[pl.ds(r0, ROWS), :]
            dov = do_ref[pl.ds(r0, ROWS), :].astype(F32)
            g_s[pl.ds(r0, ROWS), :] = dov * _gelu(yv)
            dy_ref[pl.ds(r0, ROWS), :] = (dov * hv * _dgelu(yv)).astype(dy_ref.dtype)
            return 0

        lax.fori_loop(0, nchunk, fill, 0)

        def gates(c, _):
            r0 = pl.multiple_of(c * ROWS, ROWS)
            _, _, _, a, _ = _lru_gates(xpad, r0, cw_ref, cbv, wa, bav, wx, bxv, spv)
            a_s[pl.ds(r0, ROWS), :] = a
            return 0

        lax.fori_loop(0, nchunk, gates, 0)

        r8 = lax.broadcasted_iota(I32, (8, LANE), 0)

        def rscan(i, carry):
            bases = [pl.multiple_of(S - 8 - i * (8 * SCAN_TILES) - 8 * j, 8) for j in range(SCAN_TILES)]
            firsts, maps = [], []
            for b in bases:
                a8 = a_s[pl.ds(b, 8), :]
                above = jnp.where(r8 < 7, pltpu.roll(a8, 7, 0), 1.0)
                firsts.append(a8[0:1, :])
                maps.append(_tile_scan(above, g_s[pl.ds(b, 8), :], r8, True))
            for b, a0, (ca, cb_) in zip(bases, firsts, maps):
                out = cb_ + ca * carry
                g_s[pl.ds(b, 8), :] = out
                carry = a0 * out[0:1, :]
            return carry

        lax.fori_loop(0, S // (8 * SCAN_TILES), rscan, jnp.zeros((1, LANE), F32))

        def chain(c, carry):
            dwa, dwx, dba, dbx, dsp, dcb = carry
            r0 = pl.multiple_of(c * ROWS, ROWS)
            xc, r, ig, a, s = _lru_gates(xpad, r0, cw_ref, cbv, wa, bav, wx, bxv, spv)
            gt = g_s[pl.ds(r0, ROWS), :]
            hprev = hpad[pl.ds(r0 + LR_PAD - 1, ROWS), :]
            da = gt * hprev - gt * ig * xc * (a / s)
            dig = gt * s * xc
            dla = da * a
            dsp = dsp + jnp.sum(-dla * r, axis=0, keepdims=True)
            dpr = (-dla * spv) * r * (1.0 - r)
            dpi = dig * ig * (1.0 - ig)
            dprb, dpib, xb = dpr.astype(BF16), dpi.astype(BF16), xc.astype(BF16)
            d = gt * s * ig + _dot(dprb, wa, NT) + _dot(dpib, wx, NT)
            dwa = dwa + _dot(xb, dprb, TN)
            dwx = dwx + _dot(xb, dpib, TN)
            dba = dba + jnp.sum(dpr, axis=0, keepdims=True)
            dbx = dbx + jnp.sum(dpi, axis=0, keepdims=True)
            dcb = dcb + jnp.sum(d, axis=0, keepdims=True)
            dxc[pl.ds(r0, ROWS), :] = d
            for j in range(LRU_CONV):
                xs = xpad[pl.ds(r0 + (LR_PAD - LRU_CONV + 1) + j, ROWS), :]
                dcw_ref[pl.ds(j, 1), :] += jnp.sum(d * xs, axis=0, keepdims=True)
            return dwa, dwx, dba, dbx, dsp, dcb

        zm, zv = jnp.zeros((LANE, LANE), F32), jnp.zeros((1, LANE), F32)
        dwa, dwx, dba, dbx, dsp, dcb = lax.fori_loop(0, nchunk, chain, (zm, zm, zv, zv, zv, zv))
        dwa_ref[...] = dwa
        dwx_ref[...] = dwx
        dba_ref[...] = jnp.broadcast_to(dba, dba_ref.shape)
        dbx_ref[...] = jnp.broadcast_to(dbx, dbx_ref.shape)
        dsp_ref[...] = jnp.broadcast_to(dsp, dsp_ref.shape)
        dcb_ref[...] = jnp.broadcast_to(dcb, dcb_ref.shape)

        def convt(c, _):
            r0 = pl.multiple_of(c * ROWS, ROWS)
            acc = jnp.zeros((ROWS, LANE), F32)
            for j in range(LRU_CONV):
                acc = acc + cw_ref[pl.ds(j, 1), :] * dxc[pl.ds(r0 + (LRU_CONV - 1) - j, ROWS), :]
            dx_ref[pl.ds(r0, ROWS), :] = acc.astype(dx_ref.dtype)
            return 0

        lax.fori_loop(0, nchunk, convt, 0)

    vec = pl.BlockSpec((1, LANE), lambda j: (0, j))
    vec8 = pl.BlockSpec((8, LANE), lambda j: (0, j))
    mat = pl.BlockSpec((None, LANE, LANE), lambda j: (j, 0, 0))
    blk = pl.BlockSpec((S, LANE), lambda j: (0, j))
    nblk = BRANCH_W // LANE
    v8 = jax.ShapeDtypeStruct((8, BRANCH_W), F32)
    m4 = jax.ShapeDtypeStruct((nblk, LANE, LANE), F32)
    big = jax.ShapeDtypeStruct((S, BRANCH_W), BF16)
    seq = pltpu.VMEM((S, LANE), F32)
    dx, dy, dcw, dcb, dwa, dba, dwx, dbx, dsp = pl.pallas_call(
        body, name="lru_bwd", grid=(nblk,),
        in_specs=[_colblk(S, OFF_LX), _colblk(S, OFF_LY), vec8, vec, mat, vec, mat, vec, vec, blk, blk],
        out_specs=[blk, blk, vec8, vec8, mat, vec8, mat, vec8, vec8],
        out_shape=[big, big, v8, v8, m4, v8, m4, v8, v8],
        scratch_shapes=[pltpu.VMEM((LR_PAD + S, LANE), F32), seq, seq, pltpu.VMEM((LR_PAD + S, LANE), F32),
                        pltpu.VMEM((S + LR_PAD, LANE), F32)],
        compiler_params=_cp(("parallel",)),
    )(proj, proj, cw8, cb.reshape(1, -1), wa_bd, ba.reshape(1, -1), wx_bd, bx.reshape(1, -1), sp8.reshape(1, -1), h, dout)
    return dx, dy, dcw, dcb[0], dwa, dba[0], dwx, dbx[0], dsp[0]


MG_COLS = 512
N_BRANCH = 4


def _gate_spec(tm, k):
    return pl.BlockSpec((tm, MG_COLS), lambda j, i: (i, (OFF_GATE + k * D_MODEL) // MG_COLS + j))


def merge_fwd(ups, proj, gate_b, tm=256):
    S = proj.shape[0]
    tm = _tile(S, tm)

    def body(u0, u1, u2, u3, g0, g1, g2, g3, gb_ref, o_ref):
        acc = jnp.zeros((tm, MG_COLS), F32)
        for k, (u, g) in enumerate(((u0, g0), (u1, g1), (u2, g2), (u3, g3))):
            acc = acc + _sigmoid(g[...] + gb_ref[pl.ds(k, 1), :]) * u[...]
        o_ref[...] = acc.astype(o_ref.dtype)

    blk = pl.BlockSpec((tm, MG_COLS), lambda j, i: (i, j))
    return pl.pallas_call(
        body, name="merge_fwd", grid=(D_MODEL // MG_COLS, S // tm),
        in_specs=[blk] * N_BRANCH + [_gate_spec(tm, k) for k in range(N_BRANCH)] + [pl.BlockSpec((N_BRANCH, MG_COLS), lambda j, i: (0, j))],
        out_specs=blk, out_shape=jax.ShapeDtypeStruct((S, D_MODEL), BF16),
        compiler_params=_cp(("parallel", "parallel")),
    )(*ups, proj, proj, proj, proj, gate_b)


def merge_bwd(dmerged, ups, proj, gate_b, tm=256):
    S = proj.shape[0]
    tm = _tile(S, tm)

    def body(dm_ref, u0, u1, u2, u3, g0, g1, g2, g3, gb_ref, du0, du1, du2, du3, dg0, dg1, dg2, dg3, dgb_ref):
        @pl.when(pl.program_id(1) == 0)
        def _():
            dgb_ref[...] = jnp.zeros_like(dgb_ref)

        dm = dm_ref[...].astype(F32)
        for k, (u, g, du, dg) in enumerate(((u0, g0, du0, dg0), (u1, g1, du1, dg1), (u2, g2, du2, dg2), (u3, g3, du3, dg3))):
            sg = _sigmoid(g[...] + gb_ref[pl.ds(k, 1), :])
            du[...] = (dm * sg).astype(du.dtype)
            dgk = dm * u[...] * sg * (1.0 - sg)
            dg[...] = dgk.astype(dg.dtype)
            dgb_ref[pl.ds(8 * k, 8), :] += jnp.broadcast_to(jnp.sum(dgk, axis=0, keepdims=True), (8, MG_COLS))

    blk = pl.BlockSpec((tm, MG_COLS), lambda j, i: (i, j))
    big = jax.ShapeDtypeStruct((S, D_MODEL), BF16)
    outs = pl.pallas_call(
        body, name="merge_bwd", grid=(D_MODEL // MG_COLS, S // tm),
        in_specs=[blk] * (1 + N_BRANCH) + [_gate_spec(tm, k) for k in range(N_BRANCH)] + [pl.BlockSpec((N_BRANCH, MG_COLS), lambda j, i: (0, j))],
        out_specs=[blk] * (2 * N_BRANCH) + [pl.BlockSpec((8 * N_BRANCH, MG_COLS), lambda j, i: (0, j))],
        out_shape=[big] * (2 * N_BRANCH) + [jax.ShapeDtypeStruct((8 * N_BRANCH, D_MODEL), F32)],
        compiler_params=_cp(("parallel", "arbitrary")),
    )(dmerged, *ups, proj, proj, proj, proj, gate_b)
    return outs[:N_BRANCH], outs[N_BRANCH:2 * N_BRANCH], outs[-1].reshape(N_BRANCH, 8, D_MODEL)[:, 0]


def attn_fwd(q, kv, tm=512):
    S = q.shape[0]
    M = kv.shape[0]
    tm = _tile(S, tm)
    scale = XA_HD ** -0.5

    def body(q_ref, kv_ref, o_ref):
        for hh in range(XA_HEADS):
            cs = pl.ds(hh * XA_HD, XA_HD)
            qh = q_ref[:, cs]
            kh = kv_ref[:, cs]
            vh = kv_ref[:, pl.ds(D_MODEL + hh * XA_HD, XA_HD)]
            s = _dot(qh, kh, NT) * scale
            p = jnp.exp(s - jnp.max(s, axis=-1, keepdims=True))
            p = p / jnp.sum(p, axis=-1, keepdims=True)
            o_ref[:, cs] = _dot(p.astype(BF16), vh, NN).astype(o_ref.dtype)

    return pl.pallas_call(
        body, name="attn_fwd", grid=(S // tm,),
        in_specs=[pl.BlockSpec((tm, D_MODEL), lambda i: (i, 0)), pl.BlockSpec((M, 2 * D_MODEL), lambda i: (0, 0))],
        out_specs=pl.BlockSpec((tm, D_MODEL), lambda i: (i, 0)), out_shape=jax.ShapeDtypeStruct((S, D_MODEL), BF16),
        compiler_params=_cp(("parallel",)),
    )(q, kv)


def attn_bwd(q, kv, do, tm=512):
    S = q.shape[0]
    M = kv.shape[0]
    tm = _tile(S, tm)
    scale = XA_HD ** -0.5

    def body(q_ref, kv_ref, do_ref, dq_ref, dkv_ref):
        @pl.when(pl.program_id(0) == 0)
        def _():
            dkv_ref[...] = jnp.zeros_like(dkv_ref)

        for hh in range(XA_HEADS):
            cs = pl.ds(hh * XA_HD, XA_HD)
            vs = pl.ds(D_MODEL + hh * XA_HD, XA_HD)
            qh = q_ref[:, cs]
            kh = kv_ref[:, cs]
            vh = kv_ref[:, vs]
            doh = do_ref[:, cs]
            s = _dot(qh, kh, NT) * scale
            p = jnp.exp(s - jnp.max(s, axis=-1, keepdims=True))
            p = p / jnp.sum(p, axis=-1, keepdims=True)
            dp = _dot(doh, vh, NT)
            ds = (p * (dp - jnp.sum(dp * p, axis=-1, keepdims=True)) * scale).astype(BF16)
            dq_ref[:, cs] = _dot(ds, kh, NN).astype(dq_ref.dtype)
            dkv_ref[:, cs] += _dot(ds, qh, TN)
            dkv_ref[:, vs] += _dot(p.astype(BF16), doh, TN)

    row = pl.BlockSpec((tm, D_MODEL), lambda i: (i, 0))
    full = pl.BlockSpec((M, 2 * D_MODEL), lambda i: (0, 0))
    return pl.pallas_call(
        body, name="attn_bwd", grid=(S // tm,), in_specs=[row, full, row], out_specs=[row, full],
        out_shape=[jax.ShapeDtypeStruct((S, D_MODEL), BF16), jax.ShapeDtypeStruct((M, 2 * D_MODEL), F32)],
        compiler_params=_cp(("arbitrary",)),
    )(q, kv, do)


def sum_parts(parts, own=None, tm=256):
    n, R, C = parts.shape
    tm = _tile(R, tm)
    has_own = own is not None

    def body(*refs):
        p_ref, o_ref = refs[0], refs[-1]
        acc = refs[1][...].astype(F32) if has_own else p_ref[0].astype(F32)
        for j in range(0 if has_own else 1, n):
            acc = acc + p_ref[j].astype(F32)
        o_ref[...] = acc

    row = pl.BlockSpec((tm, C), lambda i: (i, 0))
    return pl.pallas_call(
        body, name="sum_parts", grid=(R // tm,),
        in_specs=[pl.BlockSpec((n, tm, C), lambda i: (0, i, 0))] + ([row] if has_own else []), out_specs=row,
        out_shape=jax.ShapeDtypeStruct((R, C), F32), compiler_params=_cp(("parallel",)),
    )(*([parts, own] if has_own else [parts]))


def adamw(w, g, m, v, tm=256):
    R, C = w.shape
    tm = _tile(R, tm)
    c1 = 1.0 / (1.0 - ADAM_B1 ** ADAM_STEP)
    c2 = 1.0 / (1.0 - ADAM_B2 ** ADAM_STEP)

    def body(w_ref, g_ref, m_ref, v_ref, d_ref, nm_ref, nv_ref):
        gv = g_ref[...]
        nm = ADAM_B1 * m_ref[...] + (1.0 - ADAM_B1) * gv
        nv = ADAM_B2 * v_ref[...] + (1.0 - ADAM_B2) * (gv * gv)
        nm_ref[...] = nm
        nv_ref[...] = nv
        d_ref[...] = -ADAM_LR * ((nm * c1) / (jnp.sqrt(nv * c2) + ADAM_EPS) + ADAM_WD * w_ref[...])

    blk = pl.BlockSpec((tm, C), lambda i: (i, 0))
    sd = jax.ShapeDtypeStruct((R, C), F32)
    return pl.pallas_call(
        body, name="adamw", grid=(R // tm,), in_specs=[blk] * 4, out_specs=[blk] * 3, out_shape=[sd] * 3,
        compiler_params=_cp(("parallel",)),
    )(w, g, m, v)


ANY = pl.BlockSpec(memory_space=pl.ANY)


def _place():
    return lax.axis_index("x"), lax.axis_index("y"), lax.axis_index("c")


def _slot(px, py, pc):
    return 4 * px + 2 * py + pc


def all_gather(name, shards, after=()):
    n = len(shards)
    n_in = n + len(after)

    def body(*refs):
        x_refs, out_refs = refs[:n], refs[n_in:n_in + n]
        send_sems, recv_sems, local_sems = refs[n_in + n:]
        x, y, c = _place()
        me, sibling = (x, y, c), (x, y, 1 - c)
        chips = [(1 - x, y), (x, 1 - y), (1 - x, 1 - y)]

        def copy(a, k, block, to, src=None):
            rows = out_refs[a].at[_slot(*block)]
            return pltpu.make_async_remote_copy(
                src_ref=rows if src is None else src, dst_ref=rows,
                send_sem=send_sems.at[7 * a + k], recv_sem=recv_sems.at[7 * a + k],
                device_id=to, device_id_type=MESH)

        mine = [pltpu.make_async_copy(x_refs[a], out_refs[a].at[_slot(*me)], local_sems.at[a]) for a in range(n)]
        for cp in mine:
            cp.start()
        first = []
        for a in range(n):
            first.append(copy(a, 0, me, sibling, src=x_refs[a]))
            first += [copy(a, 1 + j, me, (*chip, c), src=x_refs[a]) for j, chip in enumerate(chips)]
        for cp in first:
            cp.start()
        passed = []
        for a in range(n):
            for j, chip in enumerate(chips):
                copy(a, 1 + j, (*chip, c), me).wait_recv()
                cp = copy(a, 4 + j, (*chip, c), sibling)
                cp.start()
                passed.append(cp)
        for a in range(n):
            copy(a, 0, sibling, me).wait_recv()
            for j, chip in enumerate(chips):
                copy(a, 4 + j, (*chip, 1 - c), me).wait_recv()
        for cp in first + passed:
            cp.wait_send()
        for cp in mine:
            cp.wait()

    return pl.pallas_call(
        body, name=name, in_specs=[ANY] * n_in, out_specs=[ANY] * n,
        out_shape=[jax.ShapeDtypeStruct((N_DEV, *s.shape), s.dtype) for s in shards],
        scratch_shapes=[pltpu.SemaphoreType.DMA((7 * n,)), pltpu.SemaphoreType.DMA((7 * n,)), pltpu.SemaphoreType.DMA((n,))],
    )(*shards, *after)


HBM = pl.BlockSpec(memory_space=pltpu.HBM)
SEM = pl.BlockSpec(memory_space=pltpu.SEMAPHORE)
EFFECT = pltpu.SideEffectType.DATAFLOW_SIDE_EFFECTING
N_PEER = N_DEV - 1
RELATIONS = [(dx, dy, dc) for dx in (0, 1) for dy in (0, 1) for dc in (0, 1)][1:]


def _peer(place, rel):
    return tuple(1 - v if d else v for v, d in zip(place, rel))


def gather_start(name, shards, me, before):
    n = len(shards)

    def body(*refs):
        x_refs, land_refs = refs[:n], refs[n:2 * n]
        send_sems, recv_sems = refs[2 * n + len(before):2 * n + len(before) + 2]
        token = refs[-1]
        place = _place()
        mine = _slot(*place)
        for a in range(n):
            for rel in RELATIONS:
                pltpu.make_async_remote_copy(
                    src_ref=x_refs[a], dst_ref=land_refs[a].at[mine], send_sem=send_sems.at[a], recv_sem=recv_sems.at[a],
                    device_id=_peer(place, rel), device_id_type=MESH).start()
        token[...] = jnp.zeros_like(token)

    lands = [lax.dynamic_update_index_in_dim(lax.empty((N_DEV, *s.shape), s.dtype), s, me, 0) for s in shards]
    outs = pl.pallas_call(
        body, name=name,
        in_specs=[HBM] * (2 * n) + [ANY] * len(before),
        out_specs=[SEM, SEM] + [HBM] * (2 * n) + [pl.BlockSpec(memory_space=pltpu.VMEM)],
        out_shape=[pltpu.SemaphoreType.DMA((n,)), pltpu.SemaphoreType.DMA((n,))]
        + [pltpu.HBM(t.shape, t.dtype) for t in (*shards, *lands)] + [jax.ShapeDtypeStruct((8, LANE), F32)],
        input_output_aliases={i: 2 + i for i in range(2 * n)},
        compiler_params=pltpu.CompilerParams(has_side_effects=EFFECT),
    )(*[pltpu.with_memory_space_constraint(t, pltpu.HBM) for t in (*shards, *lands)], *before)
    return (outs[0], outs[1], outs[2:2 + n], outs[2 + n:2 + 2 * n]), outs[-1]


def gather_wait(name, state, after):
    send_sems, recv_sems, shards, lands = state
    n = len(shards)

    def body(*refs):
        land_refs = refs[n:2 * n]
        s_sems, r_sems = refs[2 * n:2 * n + 2]
        place = _place()
        for a in range(n):
            seven = land_refs[a].at[pl.ds(0, N_PEER)]
            cp = pltpu.make_async_remote_copy(
                src_ref=seven, dst_ref=seven, send_sem=s_sems.at[a], recv_sem=r_sems.at[a], device_id=place, device_id_type=MESH)
            cp.wait_send()
            cp.wait_recv()

    outs = pl.pallas_call(
        body, name=name,
        in_specs=[HBM] * (2 * n) + [SEM, SEM] + [ANY] * len(after), out_specs=[HBM] * (2 * n),
        out_shape=[pltpu.HBM(t.shape, t.dtype) for t in (*shards, *lands)],
        input_output_aliases={i: i for i in range(2 * n)},
        compiler_params=pltpu.CompilerParams(has_side_effects=EFFECT),
    )(*shards, *lands, send_sems, recv_sems, *after)
    return outs[n:]


def exchange_start(name, grads, before):
    n = len(grads)

    def body(*refs):
        g_refs, land_refs = refs[:n], refs[n:2 * n]
        send_sems, recv_sems = refs[2 * n + 1:2 * n + 3]
        token = refs[-1]
        place = _place()
        for a in range(n):
            for r, rel in enumerate(RELATIONS):
                p = _peer(place, rel)
                pltpu.make_async_remote_copy(
                    src_ref=g_refs[a].at[_slot(*p)], dst_ref=land_refs[a].at[r],
                    send_sem=send_sems.at[a], recv_sem=recv_sems.at[a], device_id=p, device_id_type=MESH).start()
        token[...] = jnp.zeros_like(token)

    lands = [lax.empty((N_PEER, *g.shape[1:]), g.dtype) for g in grads]
    outs = pl.pallas_call(
        body, name=name,
        in_specs=[HBM] * (2 * n) + [ANY],
        out_specs=[SEM, SEM] + [HBM] * (2 * n) + [pl.BlockSpec(memory_space=pltpu.VMEM)],
        out_shape=[pltpu.SemaphoreType.DMA((n,)), pltpu.SemaphoreType.DMA((n,))]
        + [pltpu.HBM(g.shape, g.dtype) for g in grads] + [pltpu.HBM(t.shape, t.dtype) for t in lands]
        + [jax.ShapeDtypeStruct((8, LANE), F32)],
        input_output_aliases={i: 2 + i for i in range(2 * n)},
        compiler_params=pltpu.CompilerParams(has_side_effects=EFFECT),
    )(*[pltpu.with_memory_space_constraint(t, pltpu.HBM) for t in (*grads, *lands)], before)
    return (outs[0], outs[1], outs[2:2 + n], outs[2 + n:2 + 2 * n]), outs[-1]


def exchange_wait(name, state, after):
    send_sems, recv_sems, grads, lands = state
    n = len(grads)

    def body(*refs):
        g_refs, land_refs = refs[:n], refs[n:2 * n]
        s_sems, r_sems = refs[2 * n:2 * n + 2]
        place = _place()
        for a in range(n):
            cp = pltpu.make_async_remote_copy(
                src_ref=g_refs[a].at[pl.ds(0, N_PEER)], dst_ref=land_refs[a],
                send_sem=s_sems.at[a], recv_sem=r_sems.at[a], device_id=place, device_id_type=MESH)
            cp.wait_send()
            cp.wait_recv()

    outs = pl.pallas_call(
        body, name=name,
        in_specs=[HBM] * (2 * n) + [SEM, SEM, ANY], out_specs=[HBM] * (2 * n),
        out_shape=[pltpu.HBM(t.shape, t.dtype) for t in (*grads, *lands)],
        input_output_aliases={i: i for i in range(2 * n)},
        compiler_params=pltpu.CompilerParams(has_side_effects=EFFECT),
    )(*grads, *lands, send_sems, recv_sems, after)
    return outs[:n], outs[n:]


WEIGHTS = ['norm_mix_w', 'w_in', 'hg_lb_raw', 'hg_norm_w', 'cv_dw_w', 'cv_dw_b', 'cv_ln_w', 'cv_ln_b', 'pl_w', 'pl_scale',
           'lru_conv_w', 'lru_conv_b', 'lru_wa', 'lru_ba', 'lru_wx', 'lru_bx', 'lru_lambda', 'gate_b', 'w_branch', 'w_out',
           'norm_mem_w', 'mem_norm_w', 'xa_wq', 'xa_wkv', 'xa_wo', 'norm_ffn_w', 'ffn_w1', 'ffn_w2', 'final_norm_w']
BIG = ('w_in', 'w_branch', 'w_out', 'xa_wq', 'xa_wkv', 'xa_wo', 'ffn_w1', 'ffn_w2')
SMALL_SHARDED = ('cv_dw_w', 'lru_conv_w', 'gate_b')
SMALL = tuple(n for n in WEIGHTS if n not in BIG and n not in SMALL_SHARDED)
PACK_ROWS = 256


def _pack(arrs):
    flat = jnp.concatenate([a.reshape(-1).astype(F32) for a in arrs])
    tile = PACK_ROWS * LANE
    padded = -(-flat.shape[0] // tile) * tile
    return jnp.pad(flat, (0, padded - flat.shape[0])).reshape(-1, LANE)


def _unpack(packed, shapes):
    flat = packed.reshape(-1)
    out, off = [], 0
    for s in shapes:
        n = math.prod(s)
        out.append(flat[off:off + n].reshape(s))
        off += n
    return out


def _gather_last(g, shard_shape):
    nd = len(shard_shape)
    full = jnp.moveaxis(g, 0, nd - 1)
    return full.reshape(*shard_shape[:-1], N_DEV * shard_shape[-1])


def _natural(blocks):
    nb, k, c = blocks.shape
    return jnp.transpose(blocks, (1, 0, 2)).reshape(k, nb * c)


def _blocked(mat):
    k, n = mat.shape
    return jnp.transpose(mat.reshape(k, N_DEV, n // N_DEV), (1, 0, 2))


def _block_diag(w):
    w2 = w.reshape(4, 2, 64, 64)
    z = jnp.zeros((4, 64, 64), w.dtype)
    return jnp.concatenate([jnp.concatenate([w2[:, 0], z], axis=2), jnp.concatenate([z, w2[:, 1]], axis=2)], axis=1)


def _block_diag_t(d):
    return jnp.stack([d[:, :64, :64], d[:, 64:, 64:]], axis=1).reshape(8, 64, 64)


def _lower_bounds(raw):
    lb = jnp.cumsum(jax.nn.softmax(raw.astype(F32), axis=0), axis=0)
    return lb - lb[0:1]


def _decay_rates(lam):
    return (LRU_C * jax.nn.softplus(-lam.astype(F32))).reshape(DEPTH, BRANCH_W)


def _relu2(acc):
    r = jnp.maximum(acc, 0.0)
    return acc, r * r


def _relu2_grad(acc, u):
    return (acc * 2.0 * jnp.maximum(u, 0.0),)


def _add(acc, e):
    return (acc + e,)


def _layer_fwd(x0, mem, p, g, rest):
    h1 = rms_fwd("rms_mix", x0, p['norm_mix_w'])
    proj = mm_nt("mm_in", h1, g['w_in'], tn=2176)[0]
    more, after = rest(proj)
    g = {**g, **more}
    b_hg, states, o_hg = hgrn_fwd(proj, p['lb'], p['hg_norm_w'], after=after)
    zc = cv_fwd(proj, p['cv_w32'], p['cv_dw_b'])
    b_cv = ln_silu_fwd(zc, p['cv_ln_w'], p['cv_ln_b'])
    b_pl = pool_fwd(proj, p['pl_w'], p['pl_scale'])
    b_lru, hst = lru_fwd(proj, p['lru_cw8'], p['lru_conv_b'], p['wa_bd'], p['lru_ba'], p['wx_bd'], p['lru_bx'], p['sp8'])
    branches = [b_hg, b_cv, b_pl, b_lru]
    ups = [mm_nn("mm_up", branches[k], g['w_branch'][k], tm=2048)[0] for k in range(N_BRANCH)]
    merged = merge_fwd(ups, proj, p['gate_b'])
    x1 = mm_nn("mm_out", merged, g['w_out'], epi=_add, extras=(x0,))[0]
    h2 = rms_fwd("rms_mem", x1, p['norm_mem_w'])
    q = mm_nn("mm_q", h2, g['xa_wq'], out_dtype=BF16)[0]
    memn = rms_fwd("rms_memtok", mem, p['mem_norm_w'])
    kv = mm_nn("mm_kv", memn, g['xa_wkv'], out_dtype=BF16, tn=2048)[0]
    oa = attn_fwd(q, kv)
    x2 = mm_nn("mm_o", oa, g['xa_wo'], epi=_add, extras=(x1,))[0]
    h3 = rms_fwd("rms_ffn", x2, p['norm_ffn_w'])
    u, act = mm_nn("mm_ffn1", h3, g['ffn_w1'], epi=_relu2, out_dtypes=[F32, BF16])
    x3 = mm_nn("mm_ffn2", act, g['ffn_w2'], epi=_add, extras=(x2,))[0]
    res = dict(x0=x0, h1=h1, proj=proj, states=states, o_hg=o_hg, zc=zc, hst=hst, branches=branches, ups=ups, merged=merged,
               x1=x1, h2=h2, q=q, memn=memn, kv=kv, oa=oa, x2=x2, h3=h3, u=u, act=act)
    return x3, res, g


def _layer_bwd(dx3, mem, p, g, r, after, midway, finish):
    gs, gb = {}, {}
    du = mm_nt("mm_dffn2", dx3, g['ffn_w2'], out_dtype=BF16, epi=_relu2_grad, extras=(r['u'],), after=after)[0]
    gb['ffn_w2'] = mm_tn("mm_gw2", r['act'], dx3).reshape(N_DEV, -1, D_MODEL)
    gb['ffn_w1'] = mm_tn_cb("mm_gw1", r['h3'], du, N_DEV)
    dh3 = mm_nt("mm_dffn1", du, g['ffn_w1'], out_dtype=BF16)[0]
    dx2, gs['norm_ffn_w'] = rms_bwd("rmsb_ffn", r['x2'], p['norm_ffn_w'], dh3, dx3)
    doa = mm_nt("mm_do", dx2, g['xa_wo'], out_dtype=BF16)[0]
    gb['xa_wo'] = mm_tn("mm_gwo", r['oa'], dx2).reshape(N_DEV, -1, D_MODEL)
    dq, dkv = attn_bwd(r['q'], r['kv'], doa)
    gb['xa_wq'] = mm_tn("mm_gwq", r['h2'], dq).reshape(N_DEV, -1, D_MODEL)
    dh2 = mm_nt("mm_dq", dq, g['xa_wq'], out_dtype=BF16)[0]
    gb['xa_wkv'] = mm_tn_cb("mm_gwkv", r['memn'], dkv, N_DEV)
    dmemn = mm_nt("mm_dkv", dkv, g['xa_wkv'], out_dtype=BF16)[0]
    _, gs['mem_norm_w'] = rms_bwd("rmsb_memtok", mem, p['mem_norm_w'], dmemn)
    dx1, gs['norm_mem_w'] = rms_bwd("rmsb_mem", r['x1'], p['norm_mem_w'], dh2, dx2)
    after = midway(gb, dx1)
    gb = {}
    dmerged = mm_nt("mm_dout", dx1, g['w_out'], out_dtype=BF16, after=after)[0]
    gb['w_out'] = mm_tn("mm_gwout", r['merged'], dx1).reshape(N_DEV, -1, D_MODEL)
    dups, dgates, gs['gate_b'] = merge_bwd(dmerged, r['ups'], r['proj'], p['gate_b'])
    for k in range(N_BRANCH):
        gb[f'w_branch{k}'] = _blocked(mm_tn("mm_gwb", r['branches'][k], dups[k], tk=2048))
    db = [mm_nt("mm_dup", dups[k], g['w_branch'][k], out_dtype=BF16, tm=2048)[0] for k in range(N_BRANCH)]
    dq_, df_, dv_, dg_, gs['lb'], gs['hg_norm_w'] = hgrn_bwd(r['proj'], p['lb'], p['hg_norm_w'], r['states'], r['o_hg'], db[0])
    dzc, gs['cv_ln_w'], gs['cv_ln_b'] = ln_silu_bwd(r['zc'], p['cv_ln_w'], p['cv_ln_b'], db[1])
    dca, dcg, dcw, gs['cv_dw_b'] = cv_bwd(r['proj'], p['cv_w32'], dzc)
    gs['cv_dw_w'] = dcw[:CV_KERNEL]
    dpu, gs['pl_w'], gs['pl_scale'] = pool_bwd(r['proj'], p['pl_w'], p['pl_scale'], db[2])
    dlx, dly, dlcw, gs['lru_conv_b'], dwa, gs['lru_ba'], dwx, gs['lru_bx'], gs['sp8'] = lru_bwd(
        r['proj'], p['lru_cw8'], p['lru_conv_b'], p['wa_bd'], p['lru_ba'], p['wx_bd'], p['lru_bx'], p['sp8'], r['hst'], db[3])
    gs['lru_conv_w'] = dlcw[:LRU_CONV]
    gs['lru_wa'], gs['lru_wx'] = _block_diag_t(dwa), _block_diag_t(dwx)
    gs['lru_ba'], gs['lru_bx'] = gs['lru_ba'].reshape(8, 64), gs['lru_bx'].reshape(8, 64)
    dproj = jnp.concatenate([dq_, df_, dv_, dg_, dca, dcg, dpu, dlx, dly, *dgates], axis=1)
    gb['w_in'] = mm_tn("mm_gwin", dproj, r['h1'], tm=2176).reshape(N_DEV, -1, D_MODEL)
    dh1 = mm_nn("mm_din", dproj, g['w_in'], out_dtype=BF16, tk=2176, after=finish(gb, dx1))[0]
    dx0, gs['norm_mix_w'] = rms_bwd("rmsb_mix", r['x0'], p['norm_mix_w'], dh1, dx1)
    return dx0, gs


def kernel(x, mem, norm_mix_w, w_in, hg_lb_raw, hg_norm_w, cv_dw_w, cv_dw_b, cv_ln_w, cv_ln_b, pl_w, pl_scale, lru_conv_w, lru_conv_b, lru_wa, lru_ba, lru_wx, lru_bx, lru_lambda, gate_b, w_branch, w_out, norm_mem_w, mem_norm_w, xa_wq, xa_wkv, xa_wo, norm_ffn_w, ffn_w1, ffn_w2, final_norm_w, loss_target, m_norm_mix_w, m_w_in, m_hg_lb_raw, m_hg_norm_w, m_cv_dw_w, m_cv_dw_b, m_cv_ln_w, m_cv_ln_b, m_pl_w, m_pl_scale, m_lru_conv_w, m_lru_conv_b, m_lru_wa, m_lru_ba, m_lru_wx, m_lru_bx, m_lru_lambda, m_gate_b, m_w_branch, m_w_out, m_norm_mem_w, m_mem_norm_w, m_xa_wq, m_xa_wkv, m_xa_wo, m_norm_ffn_w, m_ffn_w1, m_ffn_w2, m_final_norm_w, v_norm_mix_w, v_w_in, v_hg_lb_raw, v_hg_norm_w, v_cv_dw_w, v_cv_dw_b, v_cv_ln_w, v_cv_ln_b, v_pl_w, v_pl_scale, v_lru_conv_w, v_lru_conv_b, v_lru_wa, v_lru_ba, v_lru_wx, v_lru_bx, v_lru_lambda, v_gate_b, v_w_branch, v_w_out, v_norm_mem_w, v_mem_norm_w, v_xa_wq, v_xa_wkv, v_xa_wo, v_norm_ffn_w, v_ffn_w1, v_ffn_w2, v_final_norm_w):
    W = dict(zip(WEIGHTS, (norm_mix_w, w_in, hg_lb_raw, hg_norm_w, cv_dw_w, cv_dw_b, cv_ln_w, cv_ln_b, pl_w, pl_scale, lru_conv_w, lru_conv_b, lru_wa, lru_ba, lru_wx, lru_bx, lru_lambda, gate_b, w_branch, w_out, norm_mem_w, mem_norm_w, xa_wq, xa_wkv, xa_wo, norm_ffn_w, ffn_w1, ffn_w2, final_norm_w)))
    Mo = dict(zip(WEIGHTS, (m_norm_mix_w, m_w_in, m_hg_lb_raw, m_hg_norm_w, m_cv_dw_w, m_cv_dw_b, m_cv_ln_w, m_cv_ln_b, m_pl_w, m_pl_scale, m_lru_conv_w, m_lru_conv_b, m_lru_wa, m_lru_ba, m_lru_wx, m_lru_bx, m_lru_lambda, m_gate_b, m_w_branch, m_w_out, m_norm_mem_w, m_mem_norm_w, m_xa_wq, m_xa_wkv, m_xa_wo, m_norm_ffn_w, m_ffn_w1, m_ffn_w2, m_final_norm_w)))
    Vo = dict(zip(WEIGHTS, (v_norm_mix_w, v_w_in, v_hg_lb_raw, v_hg_norm_w, v_cv_dw_w, v_cv_dw_b, v_cv_ln_w, v_cv_ln_b, v_pl_w, v_pl_scale, v_lru_conv_w, v_lru_conv_b, v_lru_wa, v_lru_ba, v_lru_wx, v_lru_bx, v_lru_lambda, v_gate_b, v_w_branch, v_w_out, v_norm_mem_w, v_mem_norm_w, v_xa_wq, v_xa_wkv, v_xa_wo, v_norm_ffn_w, v_ffn_w1, v_ffn_w2, v_final_norm_w)))
    me = _slot(*_place())
    xs, mems, target = x[0], mem[0], loss_target[0]

    shard_shapes = [W[n].shape for n in SMALL_SHARDED]
    gathered = all_gather("ag_small", [_pack([W[n] for n in SMALL_SHARDED])])[0]
    parts = [jnp.stack(ps) for ps in zip(*[_unpack(gathered[d], shard_shapes) for d in range(N_DEV)])]
    full_small = {n: _gather_last(parts[i], shard_shapes[i]) for i, n in enumerate(SMALL_SHARDED)}
    lb_all, lb_vjp = jax.vjp(_lower_bounds, hg_lb_raw)
    sp8_all, sp8_vjp = jax.vjp(_decay_rates, lru_lambda)

    def layer_params(l):
        p = {n: W[n][l] for n in SMALL if n != 'final_norm_w'}
        p['lb'] = lb_all[l]
        p['sp8'] = sp8_all[l]
        p['cv_w32'] = jnp.pad(full_small['cv_dw_w'][l], ((0, 32 - CV_KERNEL), (0, 0)))
        p['lru_cw8'] = jnp.pad(full_small['lru_conv_w'][l], ((0, 8 - LRU_CONV), (0, 0)))
        p['gate_b'] = full_small['gate_b'][l]
        p['wa_bd'], p['wx_bd'] = _block_diag(lru_wa[l]), _block_diag(lru_wx[l])
        p['lru_ba'], p['lru_bx'] = lru_ba[l].reshape(-1), lru_bx[l].reshape(-1)
        return p

    def shards_of(l):
        first = [jnp.transpose(w_in[l]).astype(BF16)]
        others = [w_branch[l, k].astype(BF16) for k in range(N_BRANCH)]
        others += [w[l].astype(BF16) for w in (w_out, xa_wq, xa_wkv, xa_wo, ffn_w1, ffn_w2)]
        return first, others

    def start_gather(l, before):
        first, others = shards_of(l)
        state_a, tok_a = gather_start(f"ag_start{l}a", first, me, before)
        state_b, tok_b = gather_start(f"ag_start{l}b", others, me, (*before, tok_a))
        return state_a, state_b, (tok_a, tok_b)

    def first_of(o):
        return dict(w_in=o[0].reshape(IN_W, D_MODEL))

    def others_of(o):
        return dict(w_branch=[_natural(t) for t in o[0:4]], w_out=o[4].reshape(D_MODEL, D_MODEL),
                    xa_wq=o[5].reshape(D_MODEL, D_MODEL), xa_wkv=_natural(o[6]), xa_wo=o[7].reshape(D_MODEL, D_MODEL),
                    ffn_w1=_natural(o[8]), ffn_w2=o[9].reshape(D_FF, D_MODEL))

    params = [layer_params(l) for l in range(DEPTH)]
    mats, residuals = [], []
    xc = xs
    first, others = shards_of(0)
    whole = all_gather("ag_layer0", first + others)
    gathers = {}
    for l in range(DEPTH):
        if l == 0:
            g_first, g_others = first_of(whole[:1]), others_of(whole[1:])
        else:
            state_a, state_b, _ = gathers.pop(l)
            g_first = first_of(gather_wait(f"ag_wait{l}a", state_a, (xc,)))

        def rest(proj, l=l):
            more = g_others if l == 0 else others_of(gather_wait(f"ag_wait{l}b", state_b, (proj,)))
            if l + 1 == DEPTH:
                return more, ()
            gathers[l + 1] = start_gather(l + 1, (more['w_out'],))
            return more, gathers[l + 1][2]

        xc, res, g = _layer_fwd(xc, mems, params[l], g_first, rest)
        mats.append(g)
        residuals.append(res)
    loss_part, dx, g_final = loss_head(xc, final_norm_w, target)
    loss = lax.psum(loss_part, ("x", "y", "c"))

    small_grads = [None] * DEPTH
    big_grads = [{} for _ in range(DEPTH)]
    pending = []

    def send(l, group, blocks, before):
        names = list(blocks)
        state, tok = exchange_start(f"rs_start{l}{group}", [blocks[n] for n in names], before)
        pending.append((l, group, names, state))
        return (tok,)

    def land(after):
        l, group, names, state = pending.pop(0)
        sent, landed = exchange_wait(f"rs_wait{l}{group}", state, after)
        for n, s, t in zip(names, sent, landed):
            own = lax.dynamic_index_in_dim(s, me, 0, keepdims=False).reshape(-1, s.shape[-1])
            big_grads[l][n] = sum_parts(t.reshape(N_PEER, -1, t.shape[-1]), own).reshape(t.shape[1:])

    token = ()
    for l in reversed(range(DEPTH)):
        dx, small_grads[l] = _layer_bwd(dx, mems, params[l], mats[l], residuals[l], token,
                                        lambda blocks, dx1, l=l: send(l, "a", blocks, dx1),
                                        lambda blocks, dx1, l=l: send(l, "b", blocks, dx1))
        token = ()
        while pending[0][0] > l:
            land(dx)

    def stacked(n):
        return jnp.stack([small_grads[l][n] for l in range(DEPTH)])

    part = {n: stacked(n) for n in SMALL if n not in ('final_norm_w', 'hg_lb_raw', 'lru_lambda')}
    part['final_norm_w'] = g_final
    part['hg_lb_raw'] = lb_vjp(stacked('lb'))[0]
    part['lru_lambda'] = sp8_vjp(stacked('sp8'))[0]
    for n in SMALL_SHARDED:
        part[n] = stacked(n)
    names = list(SMALL) + list(SMALL_SHARDED)
    full_shapes = [part[n].shape for n in names]
    everyone = all_gather("ag_grads", [_pack([part[n] for n in names])], after=token)[0]
    total = sum_parts(everyone)
    while pending:
        land(total)

    G = {}
    G['w_in'] = jnp.stack([jnp.transpose(big_grads[l]['w_in']) for l in range(DEPTH)])
    G['w_branch'] = jnp.stack([jnp.stack([big_grads[l][f'w_branch{k}'] for k in range(N_BRANCH)]) for l in range(DEPTH)])
    for n in ('w_out', 'xa_wq', 'xa_wkv', 'xa_wo', 'ffn_w1', 'ffn_w2'):
        G[n] = jnp.stack([big_grads[l][n] for l in range(DEPTH)])
    for n, t in zip(names, _unpack(total, full_shapes)):
        if n in SMALL_SHARDED:
            c = t.shape[-1] // N_DEV
            t = lax.dynamic_slice_in_dim(t, me * c, c, axis=t.ndim - 1)
        G[n] = t

    delta, new_m, new_v = {}, {}, {}
    for n in BIG:
        c = W[n].shape[-1]
        d, nm, nv = adamw(W[n].reshape(-1, c), G[n].reshape(-1, c), Mo[n].reshape(-1, c), Vo[n].reshape(-1, c))
        delta[n], new_m[n], new_v[n] = d.reshape(W[n].shape), nm.reshape(W[n].shape), nv.reshape(W[n].shape)
    shapes = [W[n].shape for n in names]
    d, nm, nv = adamw(_pack([W[n] for n in names]), _pack([G[n] for n in names]), _pack([Mo[n] for n in names]), _pack([Vo[n] for n in names]))
    for n, a, b, c in zip(names, _unpack(d, shapes), _unpack(nm, shapes), _unpack(nv, shapes)):
        delta[n], new_m[n], new_v[n] = a, b, c
    return (loss, dx[None], *[G[n] for n in WEIGHTS], *[delta[n] for n in WEIGHTS],
            *[new_m[n] for n in WEIGHTS], *[new_v[n] for n in WEIGHTS])
```

```python
import functools
import math

import jax
import jax.numpy as jnp
from jax import lax
from jax.experimental import pallas as pl
from jax.experimental.pallas import tpu as pltpu

F32 = jnp.float32
BF16 = jnp.bfloat16
I32 = jnp.int32

N_DEV = 8
D_MODEL = 1024
DEPTH = 4
CHUNK = 64
EPS = 1e-6
HG_HEADS = 4
BRANCH_W = 512
CV_KERNEL = 31
POOL_WINDOWS = (2, 4, 8, 16)
LRU_CONV = 4
LRU_C = 8.0
XA_HEADS = 4
XA_HD = D_MODEL // XA_HEADS
D_FF = 4 * D_MODEL
IN_W = 8704
OFF_Q, OFF_F, OFF_V, OFF_G, OFF_CV, OFF_PL, OFF_LX, OFF_LY, OFF_GATE = 0, 512, 1024, 1536, 2048, 3072, 3584, 4096, 4608
LANE = 128
ADAM_LR, ADAM_B1, ADAM_B2, ADAM_EPS, ADAM_WD, ADAM_STEP = 0.001, 0.9, 0.999, 1e-08, 0.01, 10
VMEM_LIMIT = 56 * 1024 * 1024
MESH = pl.DeviceIdType.MESH
NEG = -1e30
ANY_SPACE = pl.BlockSpec(memory_space=pl.ANY)


def _cp(sem, **kw):
    return pltpu.CompilerParams(dimension_semantics=sem, vmem_limit_bytes=VMEM_LIMIT, **kw)


def _sigmoid(x):
    return 1.0 / (1.0 + jnp.exp(-x))


def _dsilu(x, s):
    return s * (1.0 + x * (1.0 - s))


def _dot(a, b, cdims, precision=None):
    return lax.dot_general(a, b, (cdims, ((), ())), preferred_element_type=F32, precision=precision)


NN = ((1,), (0,))
NT = ((1,), (1,))
TN = ((0,), (0,))


def _mm(name, a, b, *, grid, a_spec, b_spec, o_specs, out_shapes, acc_shape, cdims, epi=None, extras=(), extra_specs=(), after=()):
    nk = grid[2]
    n_e, n_o = len(extras), len(out_shapes)
    extras = (*extras, *after)
    extra_specs = (*extra_specs, *[ANY_SPACE] * len(after))

    def body(*refs):
        a_ref, b_ref = refs[0], refs[1]
        e_refs = refs[2:2 + n_e]
        o_refs = refs[2 + len(extras):2 + len(extras) + n_o]

        def finish(acc):
            vals = epi(acc, *[r[...] for r in e_refs]) if epi is not None else (acc,)
            for r, v in zip(o_refs, vals):
                r[...] = v.astype(r.dtype)

        part = _dot(a_ref[...].astype(BF16), b_ref[...].astype(BF16), cdims)
        if nk == 1:
            finish(part)
        else:
            acc_ref = refs[-1]
            k = pl.program_id(2)

            @pl.when(k == 0)
            def _():
                acc_ref[...] = part

            @pl.when(k > 0)
            def _():
                acc_ref[...] += part

            @pl.when(k == nk - 1)
            def _():
                finish(acc_ref[...])

    return pl.pallas_call(
        body, name=name, grid=grid,
        in_specs=[a_spec, b_spec, *extra_specs], out_specs=list(o_specs), out_shape=list(out_shapes),
        scratch_shapes=[] if nk == 1 else [pltpu.VMEM(acc_shape, F32)],
        compiler_params=_cp(("parallel", "parallel", "arbitrary")),
    )(a, b, *extras)


def _tile(n, pref):
    t = min(n, pref)
    while n % t:
        t //= 2
    return t


def mm_nt(name, a, b, out_dtype=F32, epi=None, extras=(), n_out=1, out_dtypes=None, tm=1024, tn=1024, tk=2048, after=()):
    M, K = a.shape
    N = b.shape[0]
    tm, tn, tk = _tile(M, tm), _tile(N, tn), _tile(K, tk)
    odt = out_dtypes or [out_dtype] * n_out
    o_spec = pl.BlockSpec((tm, tn), lambda i, j, k: (i, j))
    return _mm(name, a, b, grid=(M // tm, N // tn, K // tk),
               a_spec=pl.BlockSpec((tm, tk), lambda i, j, k: (i, k)),
               b_spec=pl.BlockSpec((tn, tk), lambda i, j, k: (j, k)),
               o_specs=[o_spec] * len(odt), out_shapes=[jax.ShapeDtypeStruct((M, N), d) for d in odt],
               acc_shape=(tm, tn), cdims=NT, epi=epi, extras=extras, extra_specs=[o_spec] * len(extras), after=after)


def mm_nn(name, a, b, out_dtype=F32, epi=None, extras=(), n_out=1, out_dtypes=None, tm=1024, tn=1024, tk=2048, after=()):
    M, K = a.shape
    N = b.shape[1]
    tm, tn, tk = _tile(M, tm), _tile(N, tn), _tile(K, tk)
    odt = out_dtypes or [out_dtype] * n_out
    o_spec = pl.BlockSpec((tm, tn), lambda i, j, k: (i, j))
    return _mm(name, a, b, grid=(M // tm, N // tn, K // tk),
               a_spec=pl.BlockSpec((tm, tk), lambda i, j, k: (i, k)),
               b_spec=pl.BlockSpec((tk, tn), lambda i, j, k: (k, j)),
               o_specs=[o_spec] * len(odt), out_shapes=[jax.ShapeDtypeStruct((M, N), d) for d in odt],
               acc_shape=(tm, tn), cdims=NN, epi=epi, extras=extras, extra_specs=[o_spec] * len(extras), after=after)


def mm_tn(name, a, b, out_dtype=BF16, tm=1024, tn=1024, tk=1024):
    K, M = a.shape
    N = b.shape[1]
    tm, tn, tk = _tile(M, tm), _tile(N, tn), _tile(K, tk)
    return _mm(name, a, b, grid=(M // tm, N // tn, K // tk),
               a_spec=pl.BlockSpec((tk, tm), lambda i, j, k: (k, i)),
               b_spec=pl.BlockSpec((tk, tn), lambda i, j, k: (k, j)),
               o_specs=[pl.BlockSpec((tm, tn), lambda i, j, k: (i, j))],
               out_shapes=[jax.ShapeDtypeStruct((M, N), out_dtype)], acc_shape=(tm, tn), cdims=TN)[0]


def mm_nn_cb(name, a, b, out_dtype=F32, epi=None, out_dtypes=None, tm=1024):
    M, K = a.shape
    nb, _, c = b.shape
    tm = _tile(M, tm)
    odt = out_dtypes or [out_dtype]
    return _mm(name, a, b, grid=(M // tm, nb, 1),
               a_spec=pl.BlockSpec((tm, K), lambda i, j, k: (i, 0)),
               b_spec=pl.BlockSpec((None, K, c), lambda i, j, k: (j, 0, 0)),
               o_specs=[pl.BlockSpec((tm, c), lambda i, j, k: (i, j))] * len(odt),
               out_shapes=[jax.ShapeDtypeStruct((M, nb * c), d) for d in odt], acc_shape=(tm, c), cdims=NN, epi=epi)


def mm_nt_cb(name, a, b, out_dtype=F32, epi=None, extras=(), tm=1024, tn=512):
    M = a.shape[0]
    nb, K, c = b.shape
    tm, tn = _tile(M, tm), _tile(K, tn)
    o_spec = pl.BlockSpec((tm, tn), lambda i, j, k: (i, j))
    return _mm(name, a, b, grid=(M // tm, K // tn, nb),
               a_spec=pl.BlockSpec((tm, c), lambda i, j, k: (i, k)),
               b_spec=pl.BlockSpec((None, tn, c), lambda i, j, k: (k, j, 0)),
               o_specs=[o_spec], out_shapes=[jax.ShapeDtypeStruct((M, K), out_dtype)],
               acc_shape=(tm, tn), cdims=NT, epi=epi, extras=extras, extra_specs=[o_spec] * len(extras))


def mm_tn_cb(name, a, b, nb, out_dtype=BF16, tm=1024, tk=2048):
    K, M = a.shape
    N = b.shape[1]
    c = N // nb
    tm, tk = _tile(M, tm), _tile(K, tk)
    return _mm(name, a, b, grid=(M // tm, nb, K // tk),
               a_spec=pl.BlockSpec((tk, tm), lambda i, j, k: (k, i)),
               b_spec=pl.BlockSpec((tk, c), lambda i, j, k: (k, j)),
               o_specs=[pl.BlockSpec((None, tm, c), lambda i, j, k: (j, i, 0))],
               out_shapes=[jax.ShapeDtypeStruct((nb, M, c), out_dtype)], acc_shape=(tm, c), cdims=TN)[0]


def rms_fwd(name, x, w, out_dtype=BF16, tm=512, after=()):
    S, D = x.shape
    tm = _tile(S, tm)

    def body(x_ref, w_ref, *rest):
        o_ref = rest[-1]
        xv = x_ref[...]
        r = lax.rsqrt(jnp.mean(xv * xv, axis=-1, keepdims=True) + EPS)
        o_ref[...] = (xv * r * w_ref[...]).astype(o_ref.dtype)

    return pl.pallas_call(
        body, name=name, grid=(S // tm,),
        in_specs=[pl.BlockSpec((tm, D), lambda i: (i, 0)), pl.BlockSpec((1, D), lambda i: (0, 0))] + [ANY_SPACE] * len(after),
        out_specs=pl.BlockSpec((tm, D), lambda i: (i, 0)), out_shape=jax.ShapeDtypeStruct((S, D), out_dtype),
        compiler_params=_cp(("parallel",)),
    )(x, w.reshape(1, D), *after)


def rms_bwd(name, x, w, dh, dres=None, tm=512):
    S, D = x.shape
    tm = _tile(S, tm)
    has_res = dres is not None

    def body(*refs):
        if has_res:
            x_ref, w_ref, dh_ref, dres_ref, dx_ref, dw_ref = refs
        else:
            x_ref, w_ref, dh_ref, dx_ref, dw_ref = refs
        xv = x_ref[...]
        dhv = dh_ref[...].astype(F32)
        r = lax.rsqrt(jnp.mean(xv * xv, axis=-1, keepdims=True) + EPS)
        g = dhv * w_ref[...]
        dx = r * g - xv * (r * r * r) * jnp.mean(xv * g, axis=-1, keepdims=True)
        if has_res:
            dx = dx + dres_ref[...]
        dx_ref[...] = dx

        @pl.when(pl.program_id(0) == 0)
        def _():
            dw_ref[...] = jnp.zeros_like(dw_ref)

        dw_ref[...] += jnp.sum(dhv * xv * r, axis=0, keepdims=True)

    row = pl.BlockSpec((tm, D), lambda i: (i, 0))
    vec = pl.BlockSpec((1, D), lambda i: (0, 0))
    args = [x, w.reshape(1, D), dh] + ([dres] if has_res else [])
    dx, dw = pl.pallas_call(
        body, name=name, grid=(S // tm,),
        in_specs=[row, vec, row] + ([row] if has_res else []),
        out_specs=[row, vec], out_shape=[jax.ShapeDtypeStruct((S, D), F32), jax.ShapeDtypeStruct((1, D), F32)],
        compiler_params=_cp(("arbitrary",)),
    )(*args)
    return dx, dw.reshape(D)


def loss_head(x, w, target, tm=512):
    S, D = x.shape
    tm = _tile(S, tm)

    def body(x_ref, w_ref, t_ref, loss_ref, dx_ref, dw_ref):
        xv = x_ref[...]
        wv = w_ref[...]
        r = lax.rsqrt(jnp.mean(xv * xv, axis=-1, keepdims=True) + EPS)
        y = xv * r * wv
        err = y - t_ref[...]
        dy = err * (1.0 / D)
        g = dy * wv
        dx_ref[...] = r * g - xv * (r * r * r) * jnp.mean(xv * g, axis=-1, keepdims=True)

        @pl.when(pl.program_id(0) == 0)
        def _():
            dw_ref[...] = jnp.zeros_like(dw_ref)
            loss_ref[...] = jnp.zeros_like(loss_ref)

        dw_ref[...] += jnp.sum(dy * xv * r, axis=0, keepdims=True)
        part = 0.5 * jnp.sum(jnp.mean(err * err, axis=-1, keepdims=True), axis=0, keepdims=True)
        loss_ref[...] += jnp.broadcast_to(part, loss_ref.shape)

    row = pl.BlockSpec((tm, D), lambda i: (i, 0))
    vec = pl.BlockSpec((1, D), lambda i: (0, 0))
    loss, dx, dw = pl.pallas_call(
        body, name="loss_head", grid=(S // tm,),
        in_specs=[row, vec, row],
        out_specs=[pl.BlockSpec((1, LANE), lambda i: (0, 0)), row, vec],
        out_shape=[jax.ShapeDtypeStruct((1, LANE), F32), jax.ShapeDtypeStruct((S, D), F32), jax.ShapeDtypeStruct((1, D), F32)],
        compiler_params=_cp(("arbitrary",)),
    )(x, w.reshape(1, D), target)
    return loss[0, 0], dx, dw.reshape(D)


SUB = 16
HG_W = HG_HEADS * LANE


def _hg_gates(q, f, lbv):
    sig = _sigmoid(f)
    fg = lbv + (1.0 - lbv) * sig
    sq = _sigmoid(q)
    return sig, fg, 1.0 - fg, sq, q * sq


def _hg_cumsum(logf):
    ri = lax.broadcasted_iota(I32, (CHUNK, CHUNK), 0)
    ci = lax.broadcasted_iota(I32, (CHUNK, CHUNK), 1)
    return _dot((ci <= ri).astype(F32), logf, NN, precision=lax.Precision.HIGHEST)


def _hg_rows():
    return lax.broadcasted_iota(I32, (CHUNK, LANE), 0)


def _hg_below(qf, kk, b, rows):
    blocks, parts = [jnp.zeros((SUB, CHUNK), F32)], []
    for i in range(1, CHUNK // SUB):
        bref = b[SUB * i - 1:SUB * i, :]
        rs = slice(SUB * i, SUB * (i + 1))
        eq = jnp.exp(b[rs] - bref)
        below = rows < SUB * i
        ek = jnp.exp(jnp.where(below, bref - b, NEG))
        qi = (qf[rs] * eq).astype(BF16)
        ki = (kk * ek).astype(BF16)
        blocks.append(_dot(qi, ki, NT))
        parts.append((qi, ki, eq, ek))
    return jnp.concatenate(blocks, axis=0), parts


def hgrn_fwd(proj, lb, nw, after=()):
    S = proj.shape[0]
    NC = S // CHUNK
    H = HG_HEADS

    def body(q_ref, f_ref, v_ref, g_ref, lb_ref, nw_ref, *rest):
        out_ref, st_out_ref, o_ref, st, kk_s, b_s = rest[len(after):]
        c = pl.program_id(0)

        @pl.when(c == 0)
        def _():
            st[...] = jnp.zeros_like(st)

        st_out_ref[...] = st[...]
        sig, fg, kk_all, sq, qf_all = _hg_gates(q_ref[...], f_ref[...], lb_ref[...])
        b_all = _hg_cumsum(jnp.log(fg))
        kk_s[...] = kk_all
        b_s[...] = b_all
        rows = _hg_rows()
        r16 = lax.broadcasted_iota(I32, (SUB, LANE), 0)
        for h in range(H):
            cs = slice(h * LANE, (h + 1) * LANE)
            qf, kk, b, v, g = qf_all[:, cs], kk_all[:, cs], b_all[:, cs], v_ref[:, cs], g_ref[:, cs]
            st_in = st[h]
            diag = []
            for i in range(CHUNK // SUB):
                rs = slice(SUB * i, SUB * (i + 1))
                acc = jnp.zeros((SUB, LANE), F32)
                for j in range(SUB):
                    row = pl.ds(SUB * i + j, 1)
                    e = jnp.exp(jnp.where(r16 >= j, b[rs] - b_s[row, cs], NEG))
                    col = jnp.sum(qf[rs] * (kk_s[row, cs] * e), axis=1, keepdims=True)
                    acc = acc + col * v_ref[row, cs]
                diag.append(acc)
            poff, _ = _hg_below(qf, kk, b, rows)
            vb = v.astype(BF16)
            bl = b[CHUNK - 1:CHUNK, :]
            o = (jnp.concatenate(diag, axis=0) + _dot(poff.astype(BF16), vb, NN)
                 + _dot((qf * jnp.exp(b)).astype(BF16), st_in.astype(BF16), NT))
            st[h] = st_in * jnp.exp(bl) + _dot(vb, (kk * jnp.exp(bl - b)).astype(BF16), TN)
            o_ref[:, cs] = o
            r = lax.rsqrt(jnp.mean(o * o, axis=-1, keepdims=True) + EPS)
            out_ref[:, cs] = (o * r * nw_ref[...] * (g * _sigmoid(g))).astype(out_ref.dtype)

    def seg(off):
        return pl.BlockSpec((CHUNK, HG_W), lambda c: (c, off // HG_W))

    blk = pl.BlockSpec((CHUNK, HG_W), lambda c: (c, 0))
    full = pltpu.VMEM((CHUNK, HG_W), F32)
    return pl.pallas_call(
        body, name="hgrn_fwd", grid=(NC,),
        in_specs=[seg(OFF_Q), seg(OFF_F), seg(OFF_V), seg(OFF_G),
                  pl.BlockSpec((1, HG_W), lambda c: (0, 0)), pl.BlockSpec((1, LANE), lambda c: (0, 0))] + [ANY_SPACE] * len(after),
        out_specs=[blk, pl.BlockSpec((None, H, LANE, LANE), lambda c: (c, 0, 0, 0)), blk],
        out_shape=[jax.ShapeDtypeStruct((S, HG_W), BF16), jax.ShapeDtypeStruct((NC, H, LANE, LANE), F32),
                   jax.ShapeDtypeStruct((S, HG_W), F32)],
        scratch_shapes=[pltpu.VMEM((H, LANE, LANE), F32), full, full],
        compiler_params=_cp(("arbitrary",)),
    )(proj, proj, proj, proj, lb.reshape(1, HG_W), nw.reshape(1, LANE), *after)


def hgrn_bwd(proj, lb, nw, states, o_pre, dout):
    S = proj.shape[0]
    NC = S // CHUNK
    H = HG_HEADS

    def body(q_ref, f_ref, v_ref, g_ref, lb_ref, nw_ref, st_ref, o_ref, do_ref,
             dq_ref, df_ref, dv_ref, dg_ref, dlb_ref, dnw_ref, dst, kk_s, b_s, do_s, db_s, dkk_s, dkk_d, dv_d):
        c = pl.program_id(0)

        @pl.when(c == 0)
        def _():
            dst[...] = jnp.zeros_like(dst)
            dlb_ref[...] = jnp.zeros_like(dlb_ref)
            dnw_ref[...] = jnp.zeros_like(dnw_ref)

        q_all, g_all = q_ref[...], g_ref[...]
        lbv, nwv = lb_ref[...], nw_ref[...]
        sig, fg, kk_all, sq, qf_all = _hg_gates(q_all, f_ref[...], lbv)
        b_all = _hg_cumsum(jnp.log(fg))
        o_all = o_ref[...]
        dov = do_ref[...].astype(F32)
        sg = _sigmoid(g_all)
        gsg = g_all * sg
        dnw_acc = jnp.zeros((1, LANE), F32)
        for h in range(H):
            cs = slice(h * LANE, (h + 1) * LANE)
            o = o_all[:, cs]
            r = lax.rsqrt(jnp.mean(o * o, axis=-1, keepdims=True) + EPS)
            don = dov[:, cs] * gsg[:, cs]
            dnw_acc = dnw_acc + jnp.sum(don * o * r, axis=0, keepdims=True)
            gno = don * nwv
            do_s[:, cs] = r * gno - o * (r * r * r) * jnp.mean(o * gno, axis=-1, keepdims=True)
            dg_ref[:, cs] = (dov[:, cs] * (o * r * nwv) * _dsilu(g_all[:, cs], sg[:, cs])).astype(dg_ref.dtype)
        dnw_ref[...] += jnp.broadcast_to(dnw_acc, dnw_ref.shape)
        kk_s[...] = kk_all
        b_s[...] = b_all
        rows = _hg_rows()
        r16 = lax.broadcasted_iota(I32, (SUB, LANE), 0)
        for h in range(H):
            cs = slice(h * LANE, (h + 1) * LANE)
            qf, kk, b, v = qf_all[:, cs], kk_all[:, cs], b_all[:, cs], v_ref[:, cs]
            do = do_s[:, cs]
            st_in, dstv = st_ref[h], dst[h]
            bl = b[CHUNK - 1:CHUNK, :]
            eb, ebl, el = jnp.exp(b), jnp.exp(bl - b), jnp.exp(bl)
            qe, ke = qf * eb, kk * ebl
            vb, dob, stb, dstb = v.astype(BF16), do.astype(BF16), st_in.astype(BF16), dstv.astype(BF16)
            w_ = _dot(vb, dstb, NN)
            dqf = eb * _dot(dob, stb, NN)
            dkk = ebl * w_
            dv = _dot(ke.astype(BF16), dstb, NT)
            dbl = el * jnp.sum(st_in * dstv, axis=0, keepdims=True) + jnp.sum(ke * w_, axis=0, keepdims=True)
            dst[h] = dstv * el + _dot(dob, qe.astype(BF16), TN)
            poff, parts = _hg_below(qf, kk, b, rows)
            dpoff = _dot(dob, vb, NT).astype(BF16)
            dv = dv + _dot(poff.astype(BF16), dob, TN)
            dq_blocks = [jnp.zeros((SUB, LANE), F32)]
            for i, (qi, ki, eq, ek) in enumerate(parts, start=1):
                dpi = dpoff[SUB * i:SUB * (i + 1), :]
                dq_blocks.append(_dot(dpi, ki, NN) * eq)
                dkk = dkk + _dot(dpi, qi, TN) * ek
            dqf = dqf + jnp.concatenate(dq_blocks, axis=0)
            dq_diag = []
            for i in range(CHUNK // SUB):
                rs = slice(SUB * i, SUB * (i + 1))
                acc = jnp.zeros((SUB, LANE), F32)
                for j in range(SUB):
                    row = pl.ds(SUB * i + j, 1)
                    ks = kk_s[row, cs]
                    e = jnp.exp(jnp.where(r16 >= j, b[rs] - b_s[row, cs], NEG))
                    x = jnp.sum(do[rs] * v_ref[row, cs], axis=1, keepdims=True) * e
                    acc = acc + x * ks
                    dkk_d[row, cs] = jnp.sum(x * qf[rs], axis=0, keepdims=True)
                    col = jnp.sum(qf[rs] * (ks * e), axis=1, keepdims=True)
                    dv_d[row, cs] = jnp.sum(col * do[rs], axis=0, keepdims=True)
                dq_diag.append(acc)
            dqf = dqf + jnp.concatenate(dq_diag, axis=0)
            dkk = dkk + dkk_d[:, cs]
            dv = dv + dv_d[:, cs]
            dv_ref[:, cs] = dv.astype(dv_ref.dtype)
            db = qf * dqf - kk * dkk
            db_s[:, cs] = db + jnp.where(rows == CHUNK - 1, dbl, 0.0)
            dkk_s[:, cs] = dkk
            dq_ref[:, cs] = (dqf * _dsilu(q_all[:, cs], sq[:, cs])).astype(dq_ref.dtype)
        ri = lax.broadcasted_iota(I32, (CHUNK, CHUNK), 0)
        ci = lax.broadcasted_iota(I32, (CHUNK, CHUNK), 1)
        dlogf = _dot((ci >= ri).astype(F32), db_s[...], NN, precision=lax.Precision.HIGHEST)
        dfg = dlogf / fg - dkk_s[...]
        df_ref[...] = (dfg * (1.0 - lbv) * sig * (1.0 - sig)).astype(df_ref.dtype)
        dlb_ref[...] += jnp.broadcast_to(jnp.sum(dfg * (1.0 - sig), axis=0, keepdims=True), dlb_ref.shape)

    def seg(off):
        return pl.BlockSpec((CHUNK, HG_W), lambda c: (NC - 1 - c, off // HG_W))

    blk = pl.BlockSpec((CHUNK, HG_W), lambda c: (NC - 1 - c, 0))
    osd = jax.ShapeDtypeStruct((S, HG_W), BF16)
    full = pltpu.VMEM((CHUNK, HG_W), F32)
    dq, df, dv, dg, dlb, dnw = pl.pallas_call(
        body, name="hgrn_bwd", grid=(NC,),
        in_specs=[seg(OFF_Q), seg(OFF_F), seg(OFF_V), seg(OFF_G),
                  pl.BlockSpec((1, HG_W), lambda c: (0, 0)), pl.BlockSpec((1, LANE), lambda c: (0, 0)),
                  pl.BlockSpec((None, H, LANE, LANE), lambda c: (NC - 1 - c, 0, 0, 0)), blk, blk],
        out_specs=[blk, blk, blk, blk, pl.BlockSpec((8, HG_W), lambda c: (0, 0)), pl.BlockSpec((8, LANE), lambda c: (0, 0))],
        out_shape=[osd, osd, osd, osd, jax.ShapeDtypeStruct((8, HG_W), F32), jax.ShapeDtypeStruct((8, LANE), F32)],
        scratch_shapes=[pltpu.VMEM((H, LANE, LANE), F32)] + [full] * 7,
        compiler_params=_cp(("arbitrary",)),
    )(proj, proj, proj, proj, lb.reshape(1, HG_W), nw.reshape(1, LANE), states, o_pre, dout)
    return dq, df, dv, dg, dlb[0], dnw[0]


CV_PAD = 32
ROWS = 256


def _colblk(S, off):
    return pl.BlockSpec((S, LANE), lambda j: (0, off // LANE + j))


def cv_fwd(proj, w32, bias):
    S = proj.shape[0]
    nchunk = S // ROWS

    def body(a_ref, g_ref, w_ref, b_ref, o_ref, zpad):
        zpad[pl.ds(0, CV_PAD), :] = jnp.zeros((CV_PAD, LANE), F32)

        def glu(c, _):
            r0 = pl.multiple_of(c * ROWS, ROWS)
            zpad[pl.ds(CV_PAD + r0, ROWS), :] = a_ref[pl.ds(r0, ROWS), :] * _sigmoid(g_ref[pl.ds(r0, ROWS), :])
            return 0

        lax.fori_loop(0, nchunk, glu, 0)

        def conv(c, _):
            r0 = pl.multiple_of(c * ROWS, ROWS)
            acc = jnp.broadcast_to(b_ref[...], (ROWS, LANE))
            for j in range(CV_KERNEL):
                acc = acc + w_ref[pl.ds(j, 1), :] * zpad[pl.ds(r0 + (CV_PAD - CV_KERNEL + 1) + j, ROWS), :]
            o_ref[pl.ds(r0, ROWS), :] = acc
            return 0

        lax.fori_loop(0, nchunk, conv, 0)

    return pl.pallas_call(
        body, name="cv_fwd", grid=(BRANCH_W // LANE,),
        in_specs=[_colblk(S, OFF_CV), _colblk(S, OFF_CV + BRANCH_W),
                  pl.BlockSpec((32, LANE), lambda j: (0, j)), pl.BlockSpec((1, LANE), lambda j: (0, j))],
        out_specs=pl.BlockSpec((S, LANE), lambda j: (0, j)), out_shape=jax.ShapeDtypeStruct((S, BRANCH_W), F32),
        scratch_shapes=[pltpu.VMEM((CV_PAD + S, LANE), F32)],
        compiler_params=_cp(("parallel",)),
    )(proj, proj, w32, bias.reshape(1, BRANCH_W))


def cv_bwd(proj, w32, dzc):
    S = proj.shape[0]
    nchunk = S // ROWS

    def body(a_ref, g_ref, w_ref, dz_ref, da_ref, dg_ref, dw_ref, db_ref, zpad, dpad):
        zpad[pl.ds(0, CV_PAD), :] = jnp.zeros((CV_PAD, LANE), F32)
        dpad[pl.ds(S, CV_PAD), :] = jnp.zeros((CV_PAD, LANE), F32)
        dw_ref[...] = jnp.zeros_like(dw_ref)

        def glu(c, dsum):
            r0 = pl.multiple_of(c * ROWS, ROWS)
            zpad[pl.ds(CV_PAD + r0, ROWS), :] = a_ref[pl.ds(r0, ROWS), :] * _sigmoid(g_ref[pl.ds(r0, ROWS), :])
            d = dz_ref[pl.ds(r0, ROWS), :]
            dpad[pl.ds(r0, ROWS), :] = d
            return dsum + jnp.sum(d, axis=0, keepdims=True)

        dsum = lax.fori_loop(0, nchunk, glu, jnp.zeros((1, LANE), F32))
        db_ref[...] = jnp.broadcast_to(dsum, db_ref.shape)

        def conv(c, _):
            r0 = pl.multiple_of(c * ROWS, ROWS)
            d = dpad[pl.ds(r0, ROWS), :]
            acc = jnp.zeros((ROWS, LANE), F32)
            for j in range(CV_KERNEL):
                acc = acc + w_ref[pl.ds(j, 1), :] * dpad[pl.ds(r0 + (CV_KERNEL - 1) - j, ROWS), :]
                zs = zpad[pl.ds(r0 + (CV_PAD - CV_KERNEL + 1) + j, ROWS), :]
                dw_ref[pl.ds(j, 1), :] += jnp.sum(d * zs, axis=0, keepdims=True)
            a = a_ref[pl.ds(r0, ROWS), :]
            sg = _sigmoid(g_ref[pl.ds(r0, ROWS), :])
            da_ref[pl.ds(r0, ROWS), :] = (acc * sg).astype(da_ref.dtype)
            dg_ref[pl.ds(r0, ROWS), :] = (acc * a * sg * (1.0 - sg)).astype(dg_ref.dtype)
            return 0

        lax.fori_loop(0, nchunk, conv, 0)

    blk = pl.BlockSpec((S, LANE), lambda j: (0, j))
    da, dg, dw, db = pl.pallas_call(
        body, name="cv_bwd", grid=(BRANCH_W // LANE,),
        in_specs=[_colblk(S, OFF_CV), _colblk(S, OFF_CV + BRANCH_W), pl.BlockSpec((32, LANE), lambda j: (0, j)), blk],
        out_specs=[blk, blk, pl.BlockSpec((32, LANE), lambda j: (0, j)), pl.BlockSpec((8, LANE), lambda j: (0, j))],
        out_shape=[jax.ShapeDtypeStruct((S, BRANCH_W), BF16), jax.ShapeDtypeStruct((S, BRANCH_W), BF16),
                   jax.ShapeDtypeStruct((32, BRANCH_W), F32), jax.ShapeDtypeStruct((8, BRANCH_W), F32)],
        scratch_shapes=[pltpu.VMEM((CV_PAD + S, LANE), F32), pltpu.VMEM((S + CV_PAD, LANE), F32)],
        compiler_params=_cp(("parallel",)),
    )(proj, proj, w32, dzc)
    return da, dg, dw, db[0]


def ln_silu_fwd(z, w, b, tm=512):
    S, C = z.shape
    tm = _tile(S, tm)

    def body(z_ref, w_ref, b_ref, o_ref):
        zv = z_ref[...]
        mu = jnp.mean(zv, axis=-1, keepdims=True)
        zc = zv - mu
        rstd = lax.rsqrt(jnp.mean(zc * zc, axis=-1, keepdims=True) + EPS)
        y = zc * rstd * w_ref[...] + b_ref[...]
        o_ref[...] = (y * _sigmoid(y)).astype(o_ref.dtype)

    row = pl.BlockSpec((tm, C), lambda i: (i, 0))
    vec = pl.BlockSpec((1, C), lambda i: (0, 0))
    return pl.pallas_call(
        body, name="ln_silu_fwd", grid=(S // tm,), in_specs=[row, vec, vec], out_specs=row,
        out_shape=jax.ShapeDtypeStruct((S, C), BF16), compiler_params=_cp(("parallel",)),
    )(z, w.reshape(1, C), b.reshape(1, C))


def ln_silu_bwd(z, w, b, dout, tm=512):
    S, C = z.shape
    tm = _tile(S, tm)

    def body(z_ref, w_ref, b_ref, do_ref, dz_ref, dw_ref, db_ref):
        zv = z_ref[...]
        wv = w_ref[...]
        mu = jnp.mean(zv, axis=-1, keepdims=True)
        zc = zv - mu
        rstd = lax.rsqrt(jnp.mean(zc * zc, axis=-1, keepdims=True) + EPS)
        xh = zc * rstd
        y = xh * wv + b_ref[...]
        dy = do_ref[...].astype(F32) * _dsilu(y, _sigmoid(y))

        @pl.when(pl.program_id(0) == 0)
        def _():
            dw_ref[...] = jnp.zeros_like(dw_ref)
            db_ref[...] = jnp.zeros_like(db_ref)

        dw_ref[...] += jnp.sum(dy * xh, axis=0, keepdims=True)
        db_ref[...] += jnp.sum(dy, axis=0, keepdims=True)
        dxh = dy * wv
        dz_ref[...] = rstd * (dxh - jnp.mean(dxh, axis=-1, keepdims=True) - xh * jnp.mean(dxh * xh, axis=-1, keepdims=True))

    row = pl.BlockSpec((tm, C), lambda i: (i, 0))
    vec = pl.BlockSpec((1, C), lambda i: (0, 0))
    dz, dw, db = pl.pallas_call(
        body, name="ln_silu_bwd", grid=(S // tm,), in_specs=[row, vec, vec, row], out_specs=[row, vec, vec],
        out_shape=[jax.ShapeDtypeStruct((S, C), F32), jax.ShapeDtypeStruct((1, C), F32), jax.ShapeDtypeStruct((1, C), F32)],
        compiler_params=_cp(("arbitrary",)),
    )(z, w.reshape(1, C), b.reshape(1, C), dout)
    return dz, dw.reshape(C), db.reshape(C)


PL_PAD = 16


def _pool_counts(r0, win):
    t = r0 + lax.broadcasted_iota(I32, (ROWS, LANE), 0)
    return jnp.minimum(t + 1, win).astype(F32)


def pool_fwd(proj, wg, scale):
    S = proj.shape[0]
    nchunk = S // ROWS

    def body(u_ref, w_ref, s_ref, o_ref, upad):
        g = pl.program_id(0)
        upad[pl.ds(0, PL_PAD), :] = jnp.zeros((PL_PAD, LANE), F32)

        def fill(c, _):
            r0 = pl.multiple_of(c * ROWS, ROWS)
            upad[pl.ds(PL_PAD + r0, ROWS), :] = u_ref[pl.ds(r0, ROWS), :]
            return 0

        lax.fori_loop(0, nchunk, fill, 0)
        wb = w_ref[...].astype(BF16)
        for gi, win in enumerate(POOL_WINDOWS):
            @pl.when(g == gi)
            def _(win=win):
                def chunk(c, _):
                    r0 = pl.multiple_of(c * ROWS, ROWS)
                    u = upad[pl.ds(PL_PAD + r0, ROWS), :]
                    ws = u
                    for j in range(1, win):
                        ws = ws + upad[pl.ds(PL_PAD + r0 - j, ROWS), :]
                    pooled = ws / _pool_counts(r0, win) - u
                    o_ref[pl.ds(r0, ROWS), :] = (_dot(pooled.astype(BF16), wb, NN) * s_ref[...]).astype(o_ref.dtype)
                    return 0

                lax.fori_loop(0, nchunk, chunk, 0)

    return pl.pallas_call(
        body, name="pool_fwd", grid=(len(POOL_WINDOWS),),
        in_specs=[_colblk(S, OFF_PL), pl.BlockSpec((None, LANE, LANE), lambda j: (j, 0, 0)), pl.BlockSpec((1, LANE), lambda j: (0, j))],
        out_specs=pl.BlockSpec((S, LANE), lambda j: (0, j)), out_shape=jax.ShapeDtypeStruct((S, BRANCH_W), BF16),
        scratch_shapes=[pltpu.VMEM((PL_PAD + S, LANE), F32)],
        compiler_params=_cp(("parallel",)),
    )(proj, wg, scale.reshape(1, BRANCH_W))


def pool_bwd(proj, wg, scale, dy):
    S = proj.shape[0]
    nchunk = S // ROWS

    def body(u_ref, w_ref, s_ref, dy_ref, du_ref, dw_ref, ds_ref, upad, dpn, nd):
        g = pl.program_id(0)
        upad[pl.ds(0, PL_PAD), :] = jnp.zeros((PL_PAD, LANE), F32)
        dpn[pl.ds(S, PL_PAD), :] = jnp.zeros((PL_PAD, LANE), F32)

        def fill(c, _):
            r0 = pl.multiple_of(c * ROWS, ROWS)
            upad[pl.ds(PL_PAD + r0, ROWS), :] = u_ref[pl.ds(r0, ROWS), :]
            return 0

        lax.fori_loop(0, nchunk, fill, 0)
        wb = w_ref[...].astype(BF16)
        sv = s_ref[...]
        for gi, win in enumerate(POOL_WINDOWS):
            @pl.when(g == gi)
            def _(win=win):
                def chunk(c, carry):
                    dw, dsc = carry
                    r0 = pl.multiple_of(c * ROWS, ROWS)
                    u = upad[pl.ds(PL_PAD + r0, ROWS), :]
                    ws = u
                    for j in range(1, win):
                        ws = ws + upad[pl.ds(PL_PAD + r0 - j, ROWS), :]
                    cnt = _pool_counts(r0, win)
                    pooled = (ws / cnt - u).astype(BF16)
                    dyv = dy_ref[pl.ds(r0, ROWS), :].astype(F32)
                    dsc = dsc + jnp.sum(dyv * _dot(pooled, wb, NN), axis=0, keepdims=True)
                    dys = (dyv * sv).astype(BF16)
                    dw = dw + _dot(pooled, dys, TN)
                    dp = _dot(dys, wb, NT)
                    dpn[pl.ds(r0, ROWS), :] = dp / cnt
                    nd[pl.ds(r0, ROWS), :] = -dp
                    return dw, dsc

                dw, dsc = lax.fori_loop(0, nchunk, chunk, (jnp.zeros((LANE, LANE), F32), jnp.zeros((1, LANE), F32)))
                dw_ref[...] = dw
                ds_ref[...] = jnp.broadcast_to(dsc, ds_ref.shape)

                def spread(c, _):
                    r0 = pl.multiple_of(c * ROWS, ROWS)
                    acc = nd[pl.ds(r0, ROWS), :]
                    for j in range(win):
                        acc = acc + dpn[pl.ds(r0 + j, ROWS), :]
                    du_ref[pl.ds(r0, ROWS), :] = acc.astype(du_ref.dtype)
                    return 0

                lax.fori_loop(0, nchunk, spread, 0)

    blk = pl.BlockSpec((S, LANE), lambda j: (0, j))
    du, dw, ds = pl.pallas_call(
        body, name="pool_bwd", grid=(len(POOL_WINDOWS),),
        in_specs=[_colblk(S, OFF_PL), pl.BlockSpec((None, LANE, LANE), lambda j: (j, 0, 0)), pl.BlockSpec((1, LANE), lambda j: (0, j)), blk],
        out_specs=[blk, pl.BlockSpec((None, LANE, LANE), lambda j: (j, 0, 0)), pl.BlockSpec((8, LANE), lambda j: (0, j))],
        out_shape=[jax.ShapeDtypeStruct((S, BRANCH_W), BF16), jax.ShapeDtypeStruct((len(POOL_WINDOWS), LANE, LANE), F32),
                   jax.ShapeDtypeStruct((8, BRANCH_W), F32)],
        scratch_shapes=[pltpu.VMEM((PL_PAD + S, LANE), F32), pltpu.VMEM((S + PL_PAD, LANE), F32), pltpu.VMEM((S, LANE), F32)],
        compiler_params=_cp(("parallel",)),
    )(proj, wg, scale.reshape(1, BRANCH_W), dy)
    return du, dw, ds[0]


LR_PAD = 8
SCAN_TILES = 4
GELU_C = math.sqrt(2.0 / math.pi)
GELU_A = 0.044715


def _gelu(y):
    return 0.5 * y * (1.0 + jnp.tanh(GELU_C * (y + GELU_A * y * y * y)))


def _dgelu(y):
    t = jnp.tanh(GELU_C * (y + GELU_A * y * y * y))
    return 0.5 * (1.0 + t) + 0.5 * y * (1.0 - t * t) * GELU_C * (1.0 + 3.0 * GELU_A * y * y)


def _lru_gates(xpad, r0, cw_ref, cb, wa, ba, wx, bx, sp8):
    xc = jnp.broadcast_to(cb, (ROWS, LANE))
    for j in range(LRU_CONV):
        xc = xc + cw_ref[pl.ds(j, 1), :] * xpad[pl.ds(r0 + (LR_PAD - LRU_CONV + 1) + j, ROWS), :]
    xb = xc.astype(BF16)
    r = _sigmoid(_dot(xb, wa, NN) + ba)
    ig = _sigmoid(_dot(xb, wx, NN) + bx)
    la = -sp8 * r
    a = jnp.exp(la)
    s = jnp.sqrt(-jnp.tanh(la) * (a * a + 1.0))
    return xc, r, ig, a, s


def _tile_scan(a, b, r8, up):
    for s in (1, 2, 4):
        keep = (r8 < 8 - s) if up else (r8 >= s)
        shift = 8 - s if up else s
        a_sh = jnp.where(keep, pltpu.roll(a, shift, 0), 1.0)
        b_sh = jnp.where(keep, pltpu.roll(b, shift, 0), 0.0)
        b = b + a * b_sh
        a = a * a_sh
    return a, b


def lru_fwd(proj, cw8, cb, wa_bd, ba, wx_bd, bx, sp8):
    S = proj.shape[0]
    nchunk = S // ROWS

    def body(x_ref, y_ref, cw_ref, cb_ref, wa_ref, ba_ref, wx_ref, bx_ref, sp_ref, o_ref, h_ref, xpad, a_s):
        xpad[pl.ds(0, LR_PAD), :] = jnp.zeros((LR_PAD, LANE), F32)

        def fill(c, _):
            r0 = pl.multiple_of(c * ROWS, ROWS)
            xpad[pl.ds(LR_PAD + r0, ROWS), :] = x_ref[pl.ds(r0, ROWS), :]
            return 0

        lax.fori_loop(0, nchunk, fill, 0)
        wa = wa_ref[...].astype(BF16)
        wx = wx_ref[...].astype(BF16)

        def gates(c, _):
            r0 = pl.multiple_of(c * ROWS, ROWS)
            xc, r, ig, a, s = _lru_gates(xpad, r0, cw_ref, cb_ref[...], wa, ba_ref[...], wx, bx_ref[...], sp_ref[...])
            a_s[pl.ds(r0, ROWS), :] = a
            h_ref[pl.ds(r0, ROWS), :] = s * (ig * xc)
            return 0

        lax.fori_loop(0, nchunk, gates, 0)

        r8 = lax.broadcasted_iota(I32, (8, LANE), 0)

        def scan(i, h):
            bases = [pl.multiple_of(i * (8 * SCAN_TILES) + 8 * j, 8) for j in range(SCAN_TILES)]
            maps = [_tile_scan(a_s[pl.ds(b, 8), :], h_ref[pl.ds(b, 8), :], r8, False) for b in bases]
            for b, (ca, cb_) in zip(bases, maps):
                out = cb_ + ca * h
                h_ref[pl.ds(b, 8), :] = out
                h = out[7:8, :]
            return h

        lax.fori_loop(0, S // (8 * SCAN_TILES), scan, jnp.zeros((1, LANE), F32))

        def gate_out(c, _):
            r0 = pl.multiple_of(c * ROWS, ROWS)
            o_ref[pl.ds(r0, ROWS), :] = (h_ref[pl.ds(r0, ROWS), :] * _gelu(y_ref[pl.ds(r0, ROWS), :])).astype(o_ref.dtype)
            return 0

        lax.fori_loop(0, nchunk, gate_out, 0)

    vec = pl.BlockSpec((1, LANE), lambda j: (0, j))
    mat = pl.BlockSpec((None, LANE, LANE), lambda j: (j, 0, 0))
    blk = pl.BlockSpec((S, LANE), lambda j: (0, j))
    return pl.pallas_call(
        body, name="lru_fwd", grid=(BRANCH_W // LANE,),
        in_specs=[_colblk(S, OFF_LX), _colblk(S, OFF_LY), pl.BlockSpec((8, LANE), lambda j: (0, j)), vec, mat, vec, mat, vec, vec],
        out_specs=[blk, blk],
        out_shape=[jax.ShapeDtypeStruct((S, BRANCH_W), BF16), jax.ShapeDtypeStruct((S, BRANCH_W), F32)],
        scratch_shapes=[pltpu.VMEM((LR_PAD + S, LANE), F32), pltpu.VMEM((S, LANE), F32)],
        compiler_params=_cp(("parallel",)),
    )(proj, proj, cw8, cb.reshape(1, -1), wa_bd, ba.reshape(1, -1), wx_bd, bx.reshape(1, -1), sp8.reshape(1, -1))


def lru_bwd(proj, cw8, cb, wa_bd, ba, wx_bd, bx, sp8, h, dout):
    S = proj.shape[0]
    nchunk = S // ROWS

    def body(x_ref, y_ref, cw_ref, cb_ref, wa_ref, ba_ref, wx_ref, bx_ref, sp_ref, h_ref, do_ref,
             dx_ref, dy_ref, dcw_ref, dcb_ref, dwa_ref, dba_ref, dwx_ref, dbx_ref, dsp_ref,
             xpad, a_s, g_s, hpad, dxc):
        xpad[pl.ds(0, LR_PAD), :] = jnp.zeros((LR_PAD, LANE), F32)
        hpad[pl.ds(0, LR_PAD), :] = jnp.zeros((LR_PAD, LANE), F32)
        dxc[pl.ds(S, LR_PAD), :] = jnp.zeros((LR_PAD, LANE), F32)
        dcw_ref[...] = jnp.zeros_like(dcw_ref)
        wa = wa_ref[...].astype(BF16)
        wx = wx_ref[...].astype(BF16)
        cbv, bav, bxv, spv = cb_ref[...], ba_ref[...], bx_ref[...], sp_ref[...]

        def fill(c, _):
            r0 = pl.multiple_of(c * ROWS, ROWS)
            xpad[pl.ds(LR_PAD + r0, ROWS), :] = x_ref[pl.ds(r0, ROWS), :]
            hv = h_ref[pl.ds(r0, ROWS), :]
            hpad[pl.ds(LR_PAD + r0, ROWS), :] = hv
            yv = y_ref[pl.ds(r0, ROWS), :]
            dov = do_ref[pl.ds(r0, ROWS), :].astype(F32)
            g_s[pl.ds(r0, ROWS), :] = dov * _gelu(yv)
            dy_ref[pl.ds(r0, ROWS), :] = (dov * hv * _dgelu(yv)).astype(dy_ref.dtype)
            return 0

        lax.fori_loop(0, nchunk, fill, 0)

        def gates(c, _):
            r0 = pl.multiple_of(c * ROWS, ROWS)
            _, _, _, a, _ = _lru_gates(xpad, r0, cw_ref, cbv, wa, bav, wx, bxv, spv)
            a_s[pl.ds(r0, ROWS), :] = a
            return 0

        lax.fori_loop(0, nchunk, gates, 0)

        r8 = lax.broadcasted_iota(I32, (8, LANE), 0)

        def rscan(i, carry):
            bases = [pl.multiple_of(S - 8 - i * (8 * SCAN_TILES) - 8 * j, 8) for j in range(SCAN_TILES)]
            firsts, maps = [], []
            for b in bases:
                a8 = a_s[pl.ds(b, 8), :]
                above = jnp.where(r8 < 7, pltpu.roll(a8, 7, 0), 1.0)
                firsts.append(a8[0:1, :])
                maps.append(_tile_scan(above, g_s[pl.ds(b, 8), :], r8, True))
            for b, a0, (ca, cb_) in zip(bases, firsts, maps):
                out = cb_ + ca * carry
                g_s[pl.ds(b, 8), :] = out
                carry = a0 * out[0:1, :]
            return carry

        lax.fori_loop(0, S // (8 * SCAN_TILES), rscan, jnp.zeros((1, LANE), F32))

        def chain(c, carry):
            dwa, dwx, dba, dbx, dsp, dcb = carry
            r0 = pl.multiple_of(c * ROWS, ROWS)
            xc, r, ig, a, s = _lru_gates(xpad, r0, cw_ref, cbv, wa, bav, wx, bxv, spv)
            gt = g_s[pl.ds(r0, ROWS), :]
            hprev = hpad[pl.ds(r0 + LR_PAD - 1, ROWS), :]
            da = gt * hprev - gt * ig * xc * (a / s)
            dig = gt * s * xc
            dla = da * a
            dsp = dsp + jnp.sum(-dla * r, axis=0, keepdims=True)
            dpr = (-dla * spv) * r * (1.0 - r)
            dpi = dig * ig * (1.0 - ig)
            dprb, dpib, xb = dpr.astype(BF16), dpi.astype(BF16), xc.astype(BF16)
            d = gt * s * ig + _dot(dprb, wa, NT) + _dot(dpib, wx, NT)
            dwa = dwa + _dot(xb, dprb, TN)
            dwx = dwx + _dot(xb, dpib, TN)
            dba = dba + jnp.sum(dpr, axis=0, keepdims=True)
            dbx = dbx + jnp.sum(dpi, axis=0, keepdims=True)
            dcb = dcb + jnp.sum(d, axis=0, keepdims=True)
            dxc[pl.ds(r0, ROWS), :] = d
            for j in range(LRU_CONV):
                xs = xpad[pl.ds(r0 + (LR_PAD - LRU_CONV + 1) + j, ROWS), :]
                dcw_ref[pl.ds(j, 1), :] += jnp.sum(d * xs, axis=0, keepdims=True)
            return dwa, dwx, dba, dbx, dsp, dcb

        zm, zv = jnp.zeros((LANE, LANE), F32), jnp.zeros((1, LANE), F32)
        dwa, dwx, dba, dbx, dsp, dcb = lax.fori_loop(0, nchunk, chain, (zm, zm, zv, zv, zv, zv))
        dwa_ref[...] = dwa
        dwx_ref[...] = dwx
        dba_ref[...] = jnp.broadcast_to(dba, dba_ref.shape)
        dbx_ref[...] = jnp.broadcast_to(dbx, dbx_ref.shape)
        dsp_ref[...] = jnp.broadcast_to(dsp, dsp_ref.shape)
        dcb_ref[...] = jnp.broadcast_to(dcb, dcb_ref.shape)

        def convt(c, _):
            r0 = pl.multiple_of(c * ROWS, ROWS)
            acc = jnp.zeros((ROWS, LANE), F32)
            for j in range(LRU_CONV):
                acc = acc + cw_ref[pl.ds(j, 1), :] * dxc[pl.ds(r0 + (LRU_CONV - 1) - j, ROWS), :]
            dx_ref[pl.ds(r0, ROWS), :] = acc.astype(dx_ref.dtype)
            return 0

        lax.fori_loop(0, nchunk, convt, 0)

    vec = pl.BlockSpec((1, LANE), lambda j: (0, j))
    vec8 = pl.BlockSpec((8, LANE), lambda j: (0, j))
    mat = pl.BlockSpec((None, LANE, LANE), lambda j: (j, 0, 0))
    blk = pl.BlockSpec((S, LANE), lambda j: (0, j))
    nblk = BRANCH_W // LANE
    v8 = jax.ShapeDtypeStruct((8, BRANCH_W), F32)
    m4 = jax.ShapeDtypeStruct((nblk, LANE, LANE), F32)
    big = jax.ShapeDtypeStruct((S, BRANCH_W), BF16)
    seq = pltpu.VMEM((S, LANE), F32)
    dx, dy, dcw, dcb, dwa, dba, dwx, dbx, dsp = pl.pallas_call(
        body, name="lru_bwd", grid=(nblk,),
        in_specs=[_colblk(S, OFF_LX), _colblk(S, OFF_LY), vec8, vec, mat, vec, mat, vec, vec, blk, blk],
        out_specs=[blk, blk, vec8, vec8, mat, vec8, mat, vec8, vec8],
        out_shape=[big, big, v8, v8, m4, v8, m4, v8, v8],
        scratch_shapes=[pltpu.VMEM((LR_PAD + S, LANE), F32), seq, seq, pltpu.VMEM((LR_PAD + S, LANE), F32),
                        pltpu.VMEM((S + LR_PAD, LANE), F32)],
        compiler_params=_cp(("parallel",)),
    )(proj, proj, cw8, cb.reshape(1, -1), wa_bd, ba.reshape(1, -1), wx_bd, bx.reshape(1, -1), sp8.reshape(1, -1), h, dout)
    return dx, dy, dcw, dcb[0], dwa, dba[0], dwx, dbx[0], dsp[0]


MG_COLS = 512
N_BRANCH = 4


def _gate_spec(tm, k):
    return pl.BlockSpec((tm, MG_COLS), lambda j, i: (i, (OFF_GATE + k * D_MODEL) // MG_COLS + j))


def merge_fwd(ups, proj, gate_b, tm=256):
    S = proj.shape[0]
    tm = _tile(S, tm)

    def body(u0, u1, u2, u3, g0, g1, g2, g3, gb_ref, o_ref):
        acc = jnp.zeros((tm, MG_COLS), F32)
        for k, (u, g) in enumerate(((u0, g0), (u1, g1), (u2, g2), (u3, g3))):
            acc = acc + _sigmoid(g[...] + gb_ref[pl.ds(k, 1), :]) * u[...]
        o_ref[...] = acc.astype(o_ref.dtype)

    blk = pl.BlockSpec((tm, MG_COLS), lambda j, i: (i, j))
    return pl.pallas_call(
        body, name="merge_fwd", grid=(D_MODEL // MG_COLS, S // tm),
        in_specs=[blk] * N_BRANCH + [_gate_spec(tm, k) for k in range(N_BRANCH)] + [pl.BlockSpec((N_BRANCH, MG_COLS), lambda j, i: (0, j))],
        out_specs=blk, out_shape=jax.ShapeDtypeStruct((S, D_MODEL), BF16),
        compiler_params=_cp(("parallel", "parallel")),
    )(*ups, proj, proj, proj, proj, gate_b)


def merge_bwd(dmerged, ups, proj, gate_b, tm=256):
    S = proj.shape[0]
    tm = _tile(S, tm)

    def body(dm_ref, u0, u1, u2, u3, g0, g1, g2, g3, gb_ref, du0, du1, du2, du3, dg0, dg1, dg2, dg3, dgb_ref):
        @pl.when(pl.program_id(1) == 0)
        def _():
            dgb_ref[...] = jnp.zeros_like(dgb_ref)

        dm = dm_ref[...].astype(F32)
        for k, (u, g, du, dg) in enumerate(((u0, g0, du0, dg0), (u1, g1, du1, dg1), (u2, g2, du2, dg2), (u3, g3, du3, dg3))):
            sg = _sigmoid(g[...] + gb_ref[pl.ds(k, 1), :])
            du[...] = (dm * sg).astype(du.dtype)
            dgk = dm * u[...] * sg * (1.0 - sg)
            dg[...] = dgk.astype(dg.dtype)
            dgb_ref[pl.ds(8 * k, 8), :] += jnp.broadcast_to(jnp.sum(dgk, axis=0, keepdims=True), (8, MG_COLS))

    blk = pl.BlockSpec((tm, MG_COLS), lambda j, i: (i, j))
    big = jax.ShapeDtypeStruct((S, D_MODEL), BF16)
    outs = pl.pallas_call(
        body, name="merge_bwd", grid=(D_MODEL // MG_COLS, S // tm),
        in_specs=[blk] * (1 + N_BRANCH) + [_gate_spec(tm, k) for k in range(N_BRANCH)] + [pl.BlockSpec((N_BRANCH, MG_COLS), lambda j, i: (0, j))],
        out_specs=[blk] * (2 * N_BRANCH) + [pl.BlockSpec((8 * N_BRANCH, MG_COLS), lambda j, i: (0, j))],
        out_shape=[big] * (2 * N_BRANCH) + [jax.ShapeDtypeStruct((8 * N_BRANCH, D_MODEL), F32)],
        compiler_params=_cp(("parallel", "arbitrary")),
    )(dmerged, *ups, proj, proj, proj, proj, gate_b)
    return outs[:N_BRANCH], outs[N_BRANCH:2 * N_BRANCH], outs[-1].reshape(N_BRANCH, 8, D_MODEL)[:, 0]


def attn_fwd(q, kv, tm=512):
    S = q.shape[0]
    M = kv.shape[0]
    tm = _tile(S, tm)
    scale = XA_HD ** -0.5

    def body(q_ref, kv_ref, o_ref):
        for hh in range(XA_HEADS):
            cs = pl.ds(hh * XA_HD, XA_HD)
            qh = q_ref[:, cs]
            kh = kv_ref[:, cs]
            vh = kv_ref[:, pl.ds(D_MODEL + hh * XA_HD, XA_HD)]
            s = _dot(qh, kh, NT) * scale
            p = jnp.exp(s - jnp.max(s, axis=-1, keepdims=True))
            p = p / jnp.sum(p, axis=-1, keepdims=True)
            o_ref[:, cs] = _dot(p.astype(BF16), vh, NN).astype(o_ref.dtype)

    return pl.pallas_call(
        body, name="attn_fwd", grid=(S // tm,),
        in_specs=[pl.BlockSpec((tm, D_MODEL), lambda i: (i, 0)), pl.BlockSpec((M, 2 * D_MODEL), lambda i: (0, 0))],
        out_specs=pl.BlockSpec((tm, D_MODEL), lambda i: (i, 0)), out_shape=jax.ShapeDtypeStruct((S, D_MODEL), BF16),
        compiler_params=_cp(("parallel",)),
    )(q, kv)


def attn_bwd(q, kv, do, tm=512):
    S = q.shape[0]
    M = kv.shape[0]
    tm = _tile(S, tm)
    scale = XA_HD ** -0.5

    def body(q_ref, kv_ref, do_ref, dq_ref, dkv_ref):
        @pl.when(pl.program_id(0) == 0)
        def _():
            dkv_ref[...] = jnp.zeros_like(dkv_ref)

        for hh in range(XA_HEADS):
            cs = pl.ds(hh * XA_HD, XA_HD)
            vs = pl.ds(D_MODEL + hh * XA_HD, XA_HD)
            qh = q_ref[:, cs]
            kh = kv_ref[:, cs]
            vh = kv_ref[:, vs]
            doh = do_ref[:, cs]
            s = _dot(qh, kh, NT) * scale
            p = jnp.exp(s - jnp.max(s, axis=-1, keepdims=True))
            p = p / jnp.sum(p, axis=-1, keepdims=True)
            dp = _dot(doh, vh, NT)
            ds = (p * (dp - jnp.sum(dp * p, axis=-1, keepdims=True)) * scale).astype(BF16)
            dq_ref[:, cs] = _dot(ds, kh, NN).astype(dq_ref.dtype)
            dkv_ref[:, cs] += _dot(ds, qh, TN)
            dkv_ref[:, vs] += _dot(p.astype(BF16), doh, TN)

    row = pl.BlockSpec((tm, D_MODEL), lambda i: (i, 0))
    full = pl.BlockSpec((M, 2 * D_MODEL), lambda i: (0, 0))
    return pl.pallas_call(
        body, name="attn_bwd", grid=(S // tm,), in_specs=[row, full, row], out_specs=[row, full],
        out_shape=[jax.ShapeDtypeStruct((S, D_MODEL), BF16), jax.ShapeDtypeStruct((M, 2 * D_MODEL), F32)],
        compiler_params=_cp(("arbitrary",)),
    )(q, kv, do)


def sum_parts(parts, own=None, tm=256):
    n, R, C = parts.shape
    tm = _tile(R, tm)
    has_own = own is not None

    def body(*refs):
        p_ref, o_ref = refs[0], refs[-1]
        acc = refs[1][...].astype(F32) if has_own else p_ref[0].astype(F32)
        for j in range(0 if has_own else 1, n):
            acc = acc + p_ref[j].astype(F32)
        o_ref[...] = acc

    row = pl.BlockSpec((tm, C), lambda i: (i, 0))
    return pl.pallas_call(
        body, name="sum_parts", grid=(R // tm,),
        in_specs=[pl.BlockSpec((n, tm, C), lambda i: (0, i, 0))] + ([row] if has_own else []), out_specs=row,
        out_shape=jax.ShapeDtypeStruct((R, C), F32), compiler_params=_cp(("parallel",)),
    )(*([parts, own] if has_own else [parts]))


def adamw(w, g, m, v, tm=256):
    R, C = w.shape
    tm = _tile(R, tm)
    c1 = 1.0 / (1.0 - ADAM_B1 ** ADAM_STEP)
    c2 = 1.0 / (1.0 - ADAM_B2 ** ADAM_STEP)

    def body(w_ref, g_ref, m_ref, v_ref, d_ref, nm_ref, nv_ref):
        gv = g_ref[...]
        nm = ADAM_B1 * m_ref[...] + (1.0 - ADAM_B1) * gv
        nv = ADAM_B2 * v_ref[...] + (1.0 - ADAM_B2) * (gv * gv)
        nm_ref[...] = nm
        nv_ref[...] = nv
        d_ref[...] = -ADAM_LR * ((nm * c1) / (jnp.sqrt(nv * c2) + ADAM_EPS) + ADAM_WD * w_ref[...])

    blk = pl.BlockSpec((tm, C), lambda i: (i, 0))
    sd = jax.ShapeDtypeStruct((R, C), F32)
    return pl.pallas_call(
        body, name="adamw", grid=(R // tm,), in_specs=[blk] * 4, out_specs=[blk] * 3, out_shape=[sd] * 3,
        compiler_params=_cp(("parallel",)),
    )(w, g, m, v)


ANY = pl.BlockSpec(memory_space=pl.ANY)


def _place():
    return lax.axis_index("x"), lax.axis_index("y"), lax.axis_index("c")


def _slot(px, py, pc):
    return 4 * px + 2 * py + pc


def all_gather(name, shards, after=()):
    n = len(shards)
    n_in = n + len(after)

    def body(*refs):
        x_refs, out_refs = refs[:n], refs[n_in:n_in + n]
        send_sems, recv_sems, local_sems = refs[n_in + n:]
        x, y, c = _place()
        me, sibling = (x, y, c), (x, y, 1 - c)
        chips = [(1 - x, y), (x, 1 - y), (1 - x, 1 - y)]

        def copy(a, k, block, to, src=None):
            rows = out_refs[a].at[_slot(*block)]
            return pltpu.make_async_remote_copy(
                src_ref=rows if src is None else src, dst_ref=rows,
                send_sem=send_sems.at[7 * a + k], recv_sem=recv_sems.at[7 * a + k],
                device_id=to, device_id_type=MESH)

        mine = [pltpu.make_async_copy(x_refs[a], out_refs[a].at[_slot(*me)], local_sems.at[a]) for a in range(n)]
        for cp in mine:
            cp.start()
        first = []
        for a in range(n):
            first.append(copy(a, 0, me, sibling, src=x_refs[a]))
            first += [copy(a, 1 + j, me, (*chip, c), src=x_refs[a]) for j, chip in enumerate(chips)]
        for cp in first:
            cp.start()
        passed = []
        for a in range(n):
            for j, chip in enumerate(chips):
                copy(a, 1 + j, (*chip, c), me).wait_recv()
                cp = copy(a, 4 + j, (*chip, c), sibling)
                cp.start()
                passed.append(cp)
        for a in range(n):
            copy(a, 0, sibling, me).wait_recv()
            for j, chip in enumerate(chips):
                copy(a, 4 + j, (*chip, 1 - c), me).wait_recv()
        for cp in first + passed:
            cp.wait_send()
        for cp in mine:
            cp.wait()

    return pl.pallas_call(
        body, name=name, in_specs=[ANY] * n_in, out_specs=[ANY] * n,
        out_shape=[jax.ShapeDtypeStruct((N_DEV, *s.shape), s.dtype) for s in shards],
        scratch_shapes=[pltpu.SemaphoreType.DMA((7 * n,)), pltpu.SemaphoreType.DMA((7 * n,)), pltpu.SemaphoreType.DMA((n,))],
    )(*shards, *after)


HBM = pl.BlockSpec(memory_space=pltpu.HBM)
SEM = pl.BlockSpec(memory_space=pltpu.SEMAPHORE)
EFFECT = pltpu.SideEffectType.DATAFLOW_SIDE_EFFECTING
N_PEER = N_DEV - 1
RELATIONS = [(dx, dy, dc) for dx in (0, 1) for dy in (0, 1) for dc in (0, 1)][1:]


def _peer(place, rel):
    return tuple(1 - v if d else v for v, d in zip(place, rel))


def gather_start(name, shards, me, before):
    n = len(shards)

    def body(*refs):
        x_refs, land_refs = refs[:n], refs[n:2 * n]
        send_sems, recv_sems = refs[2 * n + len(before):2 * n + len(before) + 2]
        token = refs[-1]
        place = _place()
        mine = _slot(*place)
        for a in range(n):
            for rel in RELATIONS:
                pltpu.make_async_remote_copy(
                    src_ref=x_refs[a], dst_ref=land_refs[a].at[mine], send_sem=send_sems.at[a], recv_sem=recv_sems.at[a],
                    device_id=_peer(place, rel), device_id_type=MESH).start()
        token[...] = jnp.zeros_like(token)

    lands = [lax.dynamic_update_index_in_dim(lax.empty((N_DEV, *s.shape), s.dtype), s, me, 0) for s in shards]
    outs = pl.pallas_call(
        body, name=name,
        in_specs=[HBM] * (2 * n) + [ANY] * len(before),
        out_specs=[SEM, SEM] + [HBM] * (2 * n) + [pl.BlockSpec(memory_space=pltpu.VMEM)],
        out_shape=[pltpu.SemaphoreType.DMA((n,)), pltpu.SemaphoreType.DMA((n,))]
        + [pltpu.HBM(t.shape, t.dtype) for t in (*shards, *lands)] + [jax.ShapeDtypeStruct((8, LANE), F32)],
        input_output_aliases={i: 2 + i for i in range(2 * n)},
        compiler_params=pltpu.CompilerParams(has_side_effects=EFFECT),
    )(*[pltpu.with_memory_space_constraint(t, pltpu.HBM) for t in (*shards, *lands)], *before)
    return (outs[0], outs[1], outs[2:2 + n], outs[2 + n:2 + 2 * n]), outs[-1]


def gather_wait(name, state, after):
    send_sems, recv_sems, shards, lands = state
    n = len(shards)

    def body(*refs):
        land_refs = refs[n:2 * n]
        s_sems, r_sems = refs[2 * n:2 * n + 2]
        place = _place()
        for a in range(n):
            seven = land_refs[a].at[pl.ds(0, N_PEER)]
            cp = pltpu.make_async_remote_copy(
                src_ref=seven, dst_ref=seven, send_sem=s_sems.at[a], recv_sem=r_sems.at[a], device_id=place, device_id_type=MESH)
            cp.wait_send()
            cp.wait_recv()

    outs = pl.pallas_call(
        body, name=name,
        in_specs=[HBM] * (2 * n) + [SEM, SEM] + [ANY] * len(after), out_specs=[HBM] * (2 * n),
        out_shape=[pltpu.HBM(t.shape, t.dtype) for t in (*shards, *lands)],
        input_output_aliases={i: i for i in range(2 * n)},
        compiler_params=pltpu.CompilerParams(has_side_effects=EFFECT),
    )(*shards, *lands, send_sems, recv_sems, *after)
    return outs[n:]


def exchange_start(name, grads, before):
    n = len(grads)

    def body(*refs):
        g_refs, land_refs = refs[:n], refs[n:2 * n]
        send_sems, recv_sems = refs[2 * n + 1:2 * n + 3]
        token = refs[-1]
        place = _place()
        for a in range(n):
            for r, rel in enumerate(RELATIONS):
                p = _peer(place, rel)
                pltpu.make_async_remote_copy(
                    src_ref=g_refs[a].at[_slot(*p)], dst_ref=land_refs[a].at[r],
                    send_sem=send_sems.at[a], recv_sem=recv_sems.at[a], device_id=p, device_id_type=MESH).start()
        token[...] = jnp.zeros_like(token)

    lands = [lax.empty((N_PEER, *g.shape[1:]), g.dtype) for g in grads]
    outs = pl.pallas_call(
        body, name=name,
        in_specs=[HBM] * (2 * n) + [ANY],
        out_specs=[SEM, SEM] + [HBM] * (2 * n) + [pl.BlockSpec(memory_space=pltpu.VMEM)],
        out_shape=[pltpu.SemaphoreType.DMA((n,)), pltpu.SemaphoreType.DMA((n,))]
        + [pltpu.HBM(g.shape, g.dtype) for g in grads] + [pltpu.HBM(t.shape, t.dtype) for t in lands]
        + [jax.ShapeDtypeStruct((8, LANE), F32)],
        input_output_aliases={i: 2 + i for i in range(2 * n)},
        compiler_params=pltpu.CompilerParams(has_side_effects=EFFECT),
    )(*[pltpu.with_memory_space_constraint(t, pltpu.HBM) for t in (*grads, *lands)], before)
    return (outs[0], outs[1], outs[2:2 + n], outs[2 + n:2 + 2 * n]), outs[-1]


def exchange_wait(name, state, after):
    send_sems, recv_sems, grads, lands = state
    n = len(grads)

    def body(*refs):
        g_refs, land_refs = refs[:n], refs[n:2 * n]
        s_sems, r_sems = refs[2 * n:2 * n + 2]
        place = _place()
        for a in range(n):
            cp = pltpu.make_async_remote_copy(
                src_ref=g_refs[a].at[pl.ds(0, N_PEER)], dst_ref=land_refs[a],
                send_sem=s_sems.at[a], recv_sem=r_sems.at[a], device_id=place, device_id_type=MESH)
            cp.wait_send()
            cp.wait_recv()

    outs = pl.pallas_call(
        body, name=name,
        in_specs=[HBM] * (2 * n) + [SEM, SEM, ANY], out_specs=[HBM] * (2 * n),
        out_shape=[pltpu.HBM(t.shape, t.dtype) for t in (*grads, *lands)],
        input_output_aliases={i: i for i in range(2 * n)},
        compiler_params=pltpu.CompilerParams(has_side_effects=EFFECT),
    )(*grads, *lands, send_sems, recv_sems, after)
    return outs[:n], outs[n:]


WEIGHTS = ['norm_mix_w', 'w_in', 'hg_lb_raw', 'hg_norm_w', 'cv_dw_w', 'cv_dw_b', 'cv_ln_w', 'cv_ln_b', 'pl_w', 'pl_scale',
           'lru_conv_w', 'lru_conv_b', 'lru_wa', 'lru_ba', 'lru_wx', 'lru_bx', 'lru_lambda', 'gate_b', 'w_branch', 'w_out',
           'norm_mem_w', 'mem_norm_w', 'xa_wq', 'xa_wkv', 'xa_wo', 'norm_ffn_w', 'ffn_w1', 'ffn_w2', 'final_norm_w']
BIG = ('w_in', 'w_branch', 'w_out', 'xa_wq', 'xa_wkv', 'xa_wo', 'ffn_w1', 'ffn_w2')
SMALL_SHARDED = ('cv_dw_w', 'lru_conv_w', 'gate_b')
SMALL = tuple(n for n in WEIGHTS if n not in BIG and n not in SMALL_SHARDED)
PACK_ROWS = 256


def _pack(arrs):
    flat = jnp.concatenate([a.reshape(-1).astype(F32) for a in arrs])
    tile = PACK_ROWS * LANE
    padded = -(-flat.shape[0] // tile) * tile
    return jnp.pad(flat, (0, padded - flat.shape[0])).reshape(-1, LANE)


def _unpack(packed, shapes):
    flat = packed.reshape(-1)
    out, off = [], 0
    for s in shapes:
        n = math.prod(s)
        out.append(flat[off:off + n].reshape(s))
        off += n
    return out


def _gather_last(g, shard_shape):
    nd = len(shard_shape)
    full = jnp.moveaxis(g, 0, nd - 1)
    return full.reshape(*shard_shape[:-1], N_DEV * shard_shape[-1])


def _natural(blocks):
    nb, k, c = blocks.shape
    return jnp.transpose(blocks, (1, 0, 2)).reshape(k, nb * c)


def _blocked(mat):
    k, n = mat.shape
    return jnp.transpose(mat.reshape(k, N_DEV, n // N_DEV), (1, 0, 2))


def _block_diag(w):
    w2 = w.reshape(4, 2, 64, 64)
    z = jnp.zeros((4, 64, 64), w.dtype)
    return jnp.concatenate([jnp.concatenate([w2[:, 0], z], axis=2), jnp.concatenate([z, w2[:, 1]], axis=2)], axis=1)


def _block_diag_t(d):
    return jnp.stack([d[:, :64, :64], d[:, 64:, 64:]], axis=1).reshape(8, 64, 64)


def _lower_bounds(raw):
    lb = jnp.cumsum(jax.nn.softmax(raw.astype(F32), axis=0), axis=0)
    return lb - lb[0:1]


def _decay_rates(lam):
    return (LRU_C * jax.nn.softplus(-lam.astype(F32))).reshape(DEPTH, BRANCH_W)


def _relu2(acc):
    r = jnp.maximum(acc, 0.0)
    return acc, r * r


def _relu2_grad(acc, u):
    return (acc * 2.0 * jnp.maximum(u, 0.0),)


def _add(acc, e):
    return (acc + e,)


def _layer_fwd(x0, mem, p, g, rest):
    h1 = rms_fwd("rms_mix", x0, p['norm_mix_w'])
    proj = mm_nt("mm_in", h1, g['w_in'], tn=2176)[0]
    more, after = rest(proj)
    g = {**g, **more}
    b_hg, states, o_hg = hgrn_fwd(proj, p['lb'], p['hg_norm_w'], after=after)
    zc = cv_fwd(proj, p['cv_w32'], p['cv_dw_b'])
    b_cv = ln_silu_fwd(zc, p['cv_ln_w'], p['cv_ln_b'])
    b_pl = pool_fwd(proj, p['pl_w'], p['pl_scale'])
    b_lru, hst = lru_fwd(proj, p['lru_cw8'], p['lru_conv_b'], p['wa_bd'], p['lru_ba'], p['wx_bd'], p['lru_bx'], p['sp8'])
    branches = [b_hg, b_cv, b_pl, b_lru]
    ups = [mm_nn("mm_up", branches[k], g['w_branch'][k], out_dtype=BF16, tm=2048)[0] for k in range(N_BRANCH)]
    merged = merge_fwd(ups, proj, p['gate_b'])
    x1 = mm_nn("mm_out", merged, g['w_out'], epi=_add, extras=(x0,))[0]
    h2 = rms_fwd("rms_mem", x1, p['norm_mem_w'])
    q = mm_nn("mm_q", h2, g['xa_wq'], out_dtype=BF16)[0]
    memn = rms_fwd("rms_memtok", mem, p['mem_norm_w'])
    kv = mm_nn("mm_kv", memn, g['xa_wkv'], out_dtype=BF16, tn=2048)[0]
    oa = attn_fwd(q, kv)
    x2 = mm_nn("mm_o", oa, g['xa_wo'], epi=_add, extras=(x1,))[0]
    h3 = rms_fwd("rms_ffn", x2, p['norm_ffn_w'])
    u, act = mm_nn("mm_ffn1", h3, g['ffn_w1'], epi=_relu2, out_dtypes=[BF16, BF16])
    x3 = mm_nn("mm_ffn2", act, g['ffn_w2'], epi=_add, extras=(x2,))[0]
    res = dict(x0=x0, h1=h1, proj=proj, states=states, o_hg=o_hg, zc=zc, hst=hst, branches=branches, ups=ups, merged=merged,
               x1=x1, h2=h2, q=q, memn=memn, kv=kv, oa=oa, x2=x2, h3=h3, u=u, act=act)
    return x3, res, g


def _layer_bwd(dx3, mem, p, g, r, midway, finish):
    gs, gb = {}, {}
    du = mm_nt("mm_dffn2", dx3, g['ffn_w2'], out_dtype=BF16, epi=_relu2_grad, extras=(r['u'],))[0]
    gb['ffn_w2'] = mm_tn("mm_gw2", r['act'], dx3).reshape(N_DEV, -1, D_MODEL)
    gb['ffn_w1'] = mm_tn_cb("mm_gw1", r['h3'], du, N_DEV)
    dh3 = mm_nt("mm_dffn1", du, g['ffn_w1'], out_dtype=BF16)[0]
    dx2, gs['norm_ffn_w'] = rms_bwd("rmsb_ffn", r['x2'], p['norm_ffn_w'], dh3, dx3)
    doa = mm_nt("mm_do", dx2, g['xa_wo'], out_dtype=BF16)[0]
    gb['xa_wo'] = mm_tn("mm_gwo", r['oa'], dx2).reshape(N_DEV, -1, D_MODEL)
    dq, dkv = attn_bwd(r['q'], r['kv'], doa)
    gb['xa_wq'] = mm_tn("mm_gwq", r['h2'], dq).reshape(N_DEV, -1, D_MODEL)
    dh2 = mm_nt("mm_dq", dq, g['xa_wq'], out_dtype=BF16)[0]
    gb['xa_wkv'] = mm_tn_cb("mm_gwkv", r['memn'], dkv, N_DEV)
    dmemn = mm_nt("mm_dkv", dkv, g['xa_wkv'], out_dtype=BF16)[0]
    _, gs['mem_norm_w'] = rms_bwd("rmsb_memtok", mem, p['mem_norm_w'], dmemn)
    dx1, gs['norm_mem_w'] = rms_bwd("rmsb_mem", r['x1'], p['norm_mem_w'], dh2, dx2)
    after = midway(gb, dx1)
    gb = {}
    dmerged = mm_nt("mm_dout", dx1, g['w_out'], out_dtype=BF16, after=after)[0]
    gb['w_out'] = mm_tn("mm_gwout", r['merged'], dx1).reshape(N_DEV, -1, D_MODEL)
    dups, dgates, gs['gate_b'] = merge_bwd(dmerged, r['ups'], r['proj'], p['gate_b'])
    for k in range(N_BRANCH):
        gb[f'w_branch{k}'] = _blocked(mm_tn("mm_gwb", r['branches'][k], dups[k], tk=2048))
    db = [mm_nt("mm_dup", dups[k], g['w_branch'][k], out_dtype=BF16, tm=2048)[0] for k in range(N_BRANCH)]
    dq_, df_, dv_, dg_, gs['lb'], gs['hg_norm_w'] = hgrn_bwd(r['proj'], p['lb'], p['hg_norm_w'], r['states'], r['o_hg'], db[0])
    dzc, gs['cv_ln_w'], gs['cv_ln_b'] = ln_silu_bwd(r['zc'], p['cv_ln_w'], p['cv_ln_b'], db[1])
    dca, dcg, dcw, gs['cv_dw_b'] = cv_bwd(r['proj'], p['cv_w32'], dzc)
    gs['cv_dw_w'] = dcw[:CV_KERNEL]
    dpu, gs['pl_w'], gs['pl_scale'] = pool_bwd(r['proj'], p['pl_w'], p['pl_scale'], db[2])
    dlx, dly, dlcw, gs['lru_conv_b'], dwa, gs['lru_ba'], dwx, gs['lru_bx'], gs['sp8'] = lru_bwd(
        r['proj'], p['lru_cw8'], p['lru_conv_b'], p['wa_bd'], p['lru_ba'], p['wx_bd'], p['lru_bx'], p['sp8'], r['hst'], db[3])
    gs['lru_conv_w'] = dlcw[:LRU_CONV]
    gs['lru_wa'], gs['lru_wx'] = _block_diag_t(dwa), _block_diag_t(dwx)
    gs['lru_ba'], gs['lru_bx'] = gs['lru_ba'].reshape(8, 64), gs['lru_bx'].reshape(8, 64)
    dproj = jnp.concatenate([dq_, df_, dv_, dg_, dca, dcg, dpu, dlx, dly, *dgates], axis=1)
    gb['w_in'] = mm_tn("mm_gwin", dproj, r['h1'], tm=2176).reshape(N_DEV, -1, D_MODEL)
    dh1 = mm_nn("mm_din", dproj, g['w_in'], out_dtype=BF16, tk=2176, after=finish(gb, dx1))[0]
    dx0, gs['norm_mix_w'] = rms_bwd("rmsb_mix", r['x0'], p['norm_mix_w'], dh1, dx1)
    return dx0, gs


def kernel(x, mem, norm_mix_w, w_in, hg_lb_raw, hg_norm_w, cv_dw_w, cv_dw_b, cv_ln_w, cv_ln_b, pl_w, pl_scale, lru_conv_w, lru_conv_b, lru_wa, lru_ba, lru_wx, lru_bx, lru_lambda, gate_b, w_branch, w_out, norm_mem_w, mem_norm_w, xa_wq, xa_wkv, xa_wo, norm_ffn_w, ffn_w1, ffn_w2, final_norm_w, loss_target, m_norm_mix_w, m_w_in, m_hg_lb_raw, m_hg_norm_w, m_cv_dw_w, m_cv_dw_b, m_cv_ln_w, m_cv_ln_b, m_pl_w, m_pl_scale, m_lru_conv_w, m_lru_conv_b, m_lru_wa, m_lru_ba, m_lru_wx, m_lru_bx, m_lru_lambda, m_gate_b, m_w_branch, m_w_out, m_norm_mem_w, m_mem_norm_w, m_xa_wq, m_xa_wkv, m_xa_wo, m_norm_ffn_w, m_ffn_w1, m_ffn_w2, m_final_norm_w, v_norm_mix_w, v_w_in, v_hg_lb_raw, v_hg_norm_w, v_cv_dw_w, v_cv_dw_b, v_cv_ln_w, v_cv_ln_b, v_pl_w, v_pl_scale, v_lru_conv_w, v_lru_conv_b, v_lru_wa, v_lru_ba, v_lru_wx, v_lru_bx, v_lru_lambda, v_gate_b, v_w_branch, v_w_out, v_norm_mem_w, v_mem_norm_w, v_xa_wq, v_xa_wkv, v_xa_wo, v_norm_ffn_w, v_ffn_w1, v_ffn_w2, v_final_norm_w):
    W = dict(zip(WEIGHTS, (norm_mix_w, w_in, hg_lb_raw, hg_norm_w, cv_dw_w, cv_dw_b, cv_ln_w, cv_ln_b, pl_w, pl_scale, lru_conv_w, lru_conv_b, lru_wa, lru_ba, lru_wx, lru_bx, lru_lambda, gate_b, w_branch, w_out, norm_mem_w, mem_norm_w, xa_wq, xa_wkv, xa_wo, norm_ffn_w, ffn_w1, ffn_w2, final_norm_w)))
    Mo = dict(zip(WEIGHTS, (m_norm_mix_w, m_w_in, m_hg_lb_raw, m_hg_norm_w, m_cv_dw_w, m_cv_dw_b, m_cv_ln_w, m_cv_ln_b, m_pl_w, m_pl_scale, m_lru_conv_w, m_lru_conv_b, m_lru_wa, m_lru_ba, m_lru_wx, m_lru_bx, m_lru_lambda, m_gate_b, m_w_branch, m_w_out, m_norm_mem_w, m_mem_norm_w, m_xa_wq, m_xa_wkv, m_xa_wo, m_norm_ffn_w, m_ffn_w1, m_ffn_w2, m_final_norm_w)))
    Vo = dict(zip(WEIGHTS, (v_norm_mix_w, v_w_in, v_hg_lb_raw, v_hg_norm_w, v_cv_dw_w, v_cv_dw_b, v_cv_ln_w, v_cv_ln_b, v_pl_w, v_pl_scale, v_lru_conv_w, v_lru_conv_b, v_lru_wa, v_lru_ba, v_lru_wx, v_lru_bx, v_lru_lambda, v_gate_b, v_w_branch, v_w_out, v_norm_mem_w, v_mem_norm_w, v_xa_wq, v_xa_wkv, v_xa_wo, v_norm_ffn_w, v_ffn_w1, v_ffn_w2, v_final_norm_w)))
    me = _slot(*_place())
    xs, mems, target = x[0], mem[0], loss_target[0]

    shard_shapes = [W[n].shape for n in SMALL_SHARDED]
    gathered = all_gather("ag_small", [_pack([W[n] for n in SMALL_SHARDED])])[0]
    parts = [jnp.stack(ps) for ps in zip(*[_unpack(gathered[d], shard_shapes) for d in range(N_DEV)])]
    full_small = {n: _gather_last(parts[i], shard_shapes[i]) for i, n in enumerate(SMALL_SHARDED)}
    lb_all, lb_vjp = jax.vjp(_lower_bounds, hg_lb_raw)
    sp8_all, sp8_vjp = jax.vjp(_decay_rates, lru_lambda)

    def layer_params(l):
        p = {n: W[n][l] for n in SMALL if n != 'final_norm_w'}
        p['lb'] = lb_all[l]
        p['sp8'] = sp8_all[l]
        p['cv_w32'] = jnp.pad(full_small['cv_dw_w'][l], ((0, 32 - CV_KERNEL), (0, 0)))
        p['lru_cw8'] = jnp.pad(full_small['lru_conv_w'][l], ((0, 8 - LRU_CONV), (0, 0)))
        p['gate_b'] = full_small['gate_b'][l]
        p['wa_bd'], p['wx_bd'] = _block_diag(lru_wa[l]), _block_diag(lru_wx[l])
        p['lru_ba'], p['lru_bx'] = lru_ba[l].reshape(-1), lru_bx[l].reshape(-1)
        return p

    def shards_of(l):
        first = [jnp.transpose(w_in[l]).astype(BF16)]
        others = [w_branch[l, k].astype(BF16) for k in range(N_BRANCH)]
        others += [w[l].astype(BF16) for w in (w_out, xa_wq, xa_wkv, xa_wo, ffn_w1, ffn_w2)]
        return first, others

    def start_gather(l, before):
        first, others = shards_of(l)
        state_a, tok_a = gather_start(f"ag_start{l}a", first, me, before)
        state_b, tok_b = gather_start(f"ag_start{l}b", others, me, (*before, tok_a))
        return state_a, state_b, (tok_a, tok_b)

    def first_of(o):
        return dict(w_in=o[0].reshape(IN_W, D_MODEL))

    def others_of(o):
        return dict(w_branch=[_natural(t) for t in o[0:4]], w_out=o[4].reshape(D_MODEL, D_MODEL),
                    xa_wq=o[5].reshape(D_MODEL, D_MODEL), xa_wkv=_natural(o[6]), xa_wo=o[7].reshape(D_MODEL, D_MODEL),
                    ffn_w1=_natural(o[8]), ffn_w2=o[9].reshape(D_FF, D_MODEL))

    params = [layer_params(l) for l in range(DEPTH)]
    mats, residuals = [], []
    xc = xs
    first, others = shards_of(0)
    whole = all_gather("ag_layer0", first + others)
    gathers = {}
    for l in range(DEPTH):
        if l == 0:
            g_first, g_others = first_of(whole[:1]), others_of(whole[1:])
        else:
            state_a, state_b, _ = gathers.pop(l)
            g_first = first_of(gather_wait(f"ag_wait{l}a", state_a, (xc,)))

        def rest(proj, l=l):
            more = g_others if l == 0 else others_of(gather_wait(f"ag_wait{l}b", state_b, (proj,)))
            if l + 1 == DEPTH:
                return more, ()
            gathers[l + 1] = start_gather(l + 1, (more['w_out'],))
            return more, gathers[l + 1][2]

        xc, res, g = _layer_fwd(xc, mems, params[l], g_first, rest)
        mats.append(g)
        residuals.append(res)
    loss_part, dx, g_final = loss_head(xc, final_norm_w, target)
    loss = lax.psum(loss_part, ("x", "y", "c"))

    small_grads = [None] * DEPTH
    big_grads = [{} for _ in range(DEPTH)]
    pending = []

    def send(l, group, blocks, before):
        names = list(blocks)
        state, tok = exchange_start(f"rs_start{l}{group}", [blocks[n] for n in names], before)
        pending.append((l, group, names, state))
        return (tok,)

    def land(after):
        l, group, names, state = pending.pop(0)
        sent, landed = exchange_wait(f"rs_wait{l}{group}", state, after)
        for n, s, t in zip(names, sent, landed):
            own = lax.dynamic_index_in_dim(s, me, 0, keepdims=False).reshape(-1, s.shape[-1])
            big_grads[l][n] = sum_parts(t.reshape(N_PEER, -1, t.shape[-1]), own).reshape(t.shape[1:])

    for l in reversed(range(DEPTH)):
        dx, small_grads[l] = _layer_bwd(dx, mems, params[l], mats[l], residuals[l],
                                        lambda blocks, dx1, l=l: send(l, "a", blocks, dx1),
                                        lambda blocks, dx1, l=l: send(l, "b", blocks, dx1))
        while pending[0][0] > l:
            land(dx)

    def stacked(n):
        return jnp.stack([small_grads[l][n] for l in range(DEPTH)])

    part = {n: stacked(n) for n in SMALL if n not in ('final_norm_w', 'hg_lb_raw', 'lru_lambda')}
    part['final_norm_w'] = g_final
    part['hg_lb_raw'] = lb_vjp(stacked('lb'))[0]
    part['lru_lambda'] = sp8_vjp(stacked('sp8'))[0]
    for n in SMALL_SHARDED:
        part[n] = stacked(n)
    names = list(SMALL) + list(SMALL_SHARDED)
    full_shapes = [part[n].shape for n in names]
    packed = _pack([part[n] for n in names])
    state, _ = exchange_start("rs_small_start", [packed.reshape(N_DEV, -1, LANE)], packed)
    sent, landed = exchange_wait("rs_small_wait", state, packed)
    mine = sum_parts(landed[0], lax.dynamic_index_in_dim(sent[0], me, 0, keepdims=False))
    total = all_gather("ag_grads", [mine])[0].reshape(-1, LANE)
    while pending:
        land(total)

    G = {}
    G['w_in'] = jnp.stack([jnp.transpose(big_grads[l]['w_in']) for l in range(DEPTH)])
    G['w_branch'] = jnp.stack([jnp.stack([big_grads[l][f'w_branch{k}'] for k in range(N_BRANCH)]) for l in range(DEPTH)])
    for n in ('w_out', 'xa_wq', 'xa_wkv', 'xa_wo', 'ffn_w1', 'ffn_w2'):
        G[n] = jnp.stack([big_grads[l][n] for l in range(DEPTH)])
    for n, t in zip(names, _unpack(total, full_shapes)):
        if n in SMALL_SHARDED:
            c = t.shape[-1] // N_DEV
            t = lax.dynamic_slice_in_dim(t, me * c, c, axis=t.ndim - 1)
        G[n] = t

    delta, new_m, new_v = {}, {}, {}
    for n in BIG:
        c = W[n].shape[-1]
        d, nm, nv = adamw(W[n].reshape(-1, c), G[n].reshape(-1, c), Mo[n].reshape(-1, c), Vo[n].reshape(-1, c))
        delta[n], new_m[n], new_v[n] = d.reshape(W[n].shape), nm.reshape(W[n].shape), nv.reshape(W[n].shape)
    shapes = [W[n].shape for n in names]
    d, nm, nv = adamw(_pack([W[n] for n in names]), _pack([G[n] for n in names]), _pack([Mo[n] for n in names]), _pack([Vo[n] for n in names]))
    for n, a, b, c in zip(names, _unpack(d, shapes), _unpack(nm, shapes), _unpack(nv, shapes)):
        delta[n], new_m[n], new_v[n] = a, b, c
    return (loss, dx[None], *[G[n] for n in WEIGHTS], *[delta[n] for n in WEIGHTS],
            *[new_m[n] for n in WEIGHTS], *[new_v[n] for n in WEIGHTS])
```

```python
import functools
import math

import jax
import jax.numpy as jnp
from jax import lax
from jax.experimental import pallas as pl
from jax.experimental.pallas import tpu as pltpu

F32 = jnp.float32
BF16 = jnp.bfloat16
I32 = jnp.int32

N_DEV = 8
D_MODEL = 1024
DEPTH = 4
CHUNK = 64
EPS = 1e-6
HG_HEADS = 4
BRANCH_W = 512
CV_KERNEL = 31
POOL_WINDOWS = (2, 4, 8, 16)
LRU_CONV = 4
LRU_C = 8.0
XA_HEADS = 4
XA_HD = D_MODEL // XA_HEADS
D_FF = 4 * D_MODEL
IN_W = 8704
OFF_Q, OFF_F, OFF_V, OFF_G, OFF_CV, OFF_PL, OFF_LX, OFF_LY, OFF_GATE = 0, 512, 1024, 1536, 2048, 3072, 3584, 4096, 4608
LANE = 128
ADAM_LR, ADAM_B1, ADAM_B2, ADAM_EPS, ADAM_WD, ADAM_STEP = 0.001, 0.9, 0.999, 1e-08, 0.01, 10
VMEM_LIMIT = 56 * 1024 * 1024
MESH = pl.DeviceIdType.MESH
NEG = -1e30
ANY_SPACE = pl.BlockSpec(memory_space=pl.ANY)


def _cp(sem, **kw):
    return pltpu.CompilerParams(dimension_semantics=sem, vmem_limit_bytes=VMEM_LIMIT, **kw)


def _sigmoid(x):
    return 1.0 / (1.0 + jnp.exp(-x))


def _dsilu(x, s):
    return s * (1.0 + x * (1.0 - s))


def _dot(a, b, cdims, precision=None):
    return lax.dot_general(a, b, (cdims, ((), ())), preferred_element_type=F32, precision=precision)


NN = ((1,), (0,))
NT = ((1,), (1,))
TN = ((0,), (0,))


def _mm(name, a, b, *, grid, a_spec, b_spec, o_specs, out_shapes, acc_shape, cdims, epi=None, extras=(), extra_specs=(), after=()):
    nk = grid[2]
    n_e, n_o = len(extras), len(out_shapes)
    extras = (*extras, *after)
    extra_specs = (*extra_specs, *[ANY_SPACE] * len(after))

    def body(*refs):
        a_ref, b_ref = refs[0], refs[1]
        e_refs = refs[2:2 + n_e]
        o_refs = refs[2 + len(extras):2 + len(extras) + n_o]

        def finish(acc):
            vals = epi(acc, *[r[...] for r in e_refs]) if epi is not None else (acc,)
            for r, v in zip(o_refs, vals):
                r[...] = v.astype(r.dtype)

        part = _dot(a_ref[...].astype(BF16), b_ref[...].astype(BF16), cdims)
        if nk == 1:
            finish(part)
        else:
            acc_ref = refs[-1]
            k = pl.program_id(2)

            @pl.when(k == 0)
            def _():
                acc_ref[...] = part

            @pl.when(k > 0)
            def _():
                acc_ref[...] += part

            @pl.when(k == nk - 1)
            def _():
                finish(acc_ref[...])

    return pl.pallas_call(
        body, name=name, grid=grid,
        in_specs=[a_spec, b_spec, *extra_specs], out_specs=list(o_specs), out_shape=list(out_shapes),
        scratch_shapes=[] if nk == 1 else [pltpu.VMEM(acc_shape, F32)],
        compiler_params=_cp(("parallel", "parallel", "arbitrary")),
    )(a, b, *extras)


def _tile(n, pref):
    t = min(n, pref)
    while n % t:
        t //= 2
    return t


def mm_nt(name, a, b, out_dtype=F32, epi=None, extras=(), n_out=1, out_dtypes=None, tm=1024, tn=1024, tk=2048, after=()):
    M, K = a.shape
    N = b.shape[0]
    tm, tn, tk = _tile(M, tm), _tile(N, tn), _tile(K, tk)
    odt = out_dtypes or [out_dtype] * n_out
    o_spec = pl.BlockSpec((tm, tn), lambda i, j, k: (i, j))
    return _mm(name, a, b, grid=(M // tm, N // tn, K // tk),
               a_spec=pl.BlockSpec((tm, tk), lambda i, j, k: (i, k)),
               b_spec=pl.BlockSpec((tn, tk), lambda i, j, k: (j, k)),
               o_specs=[o_spec] * len(odt), out_shapes=[jax.ShapeDtypeStruct((M, N), d) for d in odt],
               acc_shape=(tm, tn), cdims=NT, epi=epi, extras=extras, extra_specs=[o_spec] * len(extras), after=after)


def mm_nn(name, a, b, out_dtype=F32, epi=None, extras=(), n_out=1, out_dtypes=None, tm=1024, tn=1024, tk=2048, after=()):
    M, K = a.shape
    N = b.shape[1]
    tm, tn, tk = _tile(M, tm), _tile(N, tn), _tile(K, tk)
    odt = out_dtypes or [out_dtype] * n_out
    o_spec = pl.BlockSpec((tm, tn), lambda i, j, k: (i, j))
    return _mm(name, a, b, grid=(M // tm, N // tn, K // tk),
               a_spec=pl.BlockSpec((tm, tk), lambda i, j, k: (i, k)),
               b_spec=pl.BlockSpec((tk, tn), lambda i, j, k: (k, j)),
               o_specs=[o_spec] * len(odt), out_shapes=[jax.ShapeDtypeStruct((M, N), d) for d in odt],
               acc_shape=(tm, tn), cdims=NN, epi=epi, extras=extras, extra_specs=[o_spec] * len(extras), after=after)


def mm_tn(name, a, b, out_dtype=BF16, tm=1024, tn=1024, tk=1024):
    K, M = a.shape
    N = b.shape[1]
    tm, tn, tk = _tile(M, tm), _tile(N, tn), _tile(K, tk)
    return _mm(name, a, b, grid=(M // tm, N // tn, K // tk),
               a_spec=pl.BlockSpec((tk, tm), lambda i, j, k: (k, i)),
               b_spec=pl.BlockSpec((tk, tn), lambda i, j, k: (k, j)),
               o_specs=[pl.BlockSpec((tm, tn), lambda i, j, k: (i, j))],
               out_shapes=[jax.ShapeDtypeStruct((M, N), out_dtype)], acc_shape=(tm, tn), cdims=TN)[0]


def mm_branch_nn(name, a, b, out_dtype=BF16, tm=1024, after=()):
    M = a.shape[0]
    G, K, N = b.shape
    tm = _tile(M, tm)
    return _mm(name, a, b, grid=(M // tm, G, 1),
               a_spec=pl.BlockSpec((tm, K), lambda i, g, k: (i, g)),
               b_spec=pl.BlockSpec((None, K, N), lambda i, g, k: (g, 0, 0)),
               o_specs=[pl.BlockSpec((tm, N), lambda i, g, k: (i, g))],
               out_shapes=[jax.ShapeDtypeStruct((M, G * N), out_dtype)], acc_shape=(tm, N), cdims=NN, after=after)[0]


def mm_branch_nt(name, a, b, out_dtype=BF16, tm=1024):
    M = a.shape[0]
    G, K, N = b.shape
    tm = _tile(M, tm)
    return _mm(name, a, b, grid=(M // tm, G, 1),
               a_spec=pl.BlockSpec((tm, N), lambda i, g, k: (i, g)),
               b_spec=pl.BlockSpec((None, K, N), lambda i, g, k: (g, 0, 0)),
               o_specs=[pl.BlockSpec((tm, K), lambda i, g, k: (i, g))],
               out_shapes=[jax.ShapeDtypeStruct((M, G * K), out_dtype)], acc_shape=(tm, K), cdims=NT)[0]


def mm_branch_tn(name, a, b, groups, out_dtype=BF16, tk=2048):
    T = a.shape[0]
    K, N = a.shape[1] // groups, b.shape[1] // groups
    tk = _tile(T, tk)
    return _mm(name, a, b, grid=(groups, 1, T // tk),
               a_spec=pl.BlockSpec((tk, K), lambda g, j, k: (k, g)),
               b_spec=pl.BlockSpec((tk, N), lambda g, j, k: (k, g)),
               o_specs=[pl.BlockSpec((None, K, N), lambda g, j, k: (g, 0, 0))],
               out_shapes=[jax.ShapeDtypeStruct((groups, K, N), out_dtype)], acc_shape=(K, N), cdims=TN)[0]


def mm_tn_cb(name, a, b, nb, out_dtype=BF16, tm=1024, tk=2048):
    K, M = a.shape
    N = b.shape[1]
    c = N // nb
    tm, tk = _tile(M, tm), _tile(K, tk)
    return _mm(name, a, b, grid=(M // tm, nb, K // tk),
               a_spec=pl.BlockSpec((tk, tm), lambda i, j, k: (k, i)),
               b_spec=pl.BlockSpec((tk, c), lambda i, j, k: (k, j)),
               o_specs=[pl.BlockSpec((None, tm, c), lambda i, j, k: (j, i, 0))],
               out_shapes=[jax.ShapeDtypeStruct((nb, M, c), out_dtype)], acc_shape=(tm, c), cdims=TN)[0]


def rms_fwd(name, x, w, out_dtype=BF16, tm=512):
    S, D = x.shape
    tm = _tile(S, tm)

    def body(x_ref, w_ref, o_ref):
        xv = x_ref[...]
        r = lax.rsqrt(jnp.mean(xv * xv, axis=-1, keepdims=True) + EPS)
        o_ref[...] = (xv * r * w_ref[...]).astype(o_ref.dtype)

    return pl.pallas_call(
        body, name=name, grid=(S // tm,),
        in_specs=[pl.BlockSpec((tm, D), lambda i: (i, 0)), pl.BlockSpec((1, D), lambda i: (0, 0))],
        out_specs=pl.BlockSpec((tm, D), lambda i: (i, 0)), out_shape=jax.ShapeDtypeStruct((S, D), out_dtype),
        compiler_params=_cp(("parallel",)),
    )(x, w.reshape(1, D))


def rms_bwd(name, x, w, dh, dres=None, tm=512):
    S, D = x.shape
    tm = _tile(S, tm)
    has_res = dres is not None

    def body(*refs):
        if has_res:
            x_ref, w_ref, dh_ref, dres_ref, dx_ref, dw_ref = refs
        else:
            x_ref, w_ref, dh_ref, dx_ref, dw_ref = refs
        xv = x_ref[...]
        dhv = dh_ref[...].astype(F32)
        r = lax.rsqrt(jnp.mean(xv * xv, axis=-1, keepdims=True) + EPS)
        g = dhv * w_ref[...]
        dx = r * g - xv * (r * r * r) * jnp.mean(xv * g, axis=-1, keepdims=True)
        if has_res:
            dx = dx + dres_ref[...]
        dx_ref[...] = dx

        @pl.when(pl.program_id(0) == 0)
        def _():
            dw_ref[...] = jnp.zeros_like(dw_ref)

        dw_ref[...] += jnp.sum(dhv * xv * r, axis=0, keepdims=True)

    row = pl.BlockSpec((tm, D), lambda i: (i, 0))
    vec = pl.BlockSpec((1, D), lambda i: (0, 0))
    args = [x, w.reshape(1, D), dh] + ([dres] if has_res else [])
    dx, dw = pl.pallas_call(
        body, name=name, grid=(S // tm,),
        in_specs=[row, vec, row] + ([row] if has_res else []),
        out_specs=[row, vec], out_shape=[jax.ShapeDtypeStruct((S, D), F32), jax.ShapeDtypeStruct((1, D), F32)],
        compiler_params=_cp(("arbitrary",)),
    )(*args)
    return dx, dw.reshape(D)


def loss_head(x, w, target, tm=512):
    S, D = x.shape
    tm = _tile(S, tm)

    def body(x_ref, w_ref, t_ref, loss_ref, dx_ref, dw_ref):
        xv = x_ref[...]
        wv = w_ref[...]
        r = lax.rsqrt(jnp.mean(xv * xv, axis=-1, keepdims=True) + EPS)
        y = xv * r * wv
        err = y - t_ref[...]
        dy = err * (1.0 / D)
        g = dy * wv
        dx_ref[...] = r * g - xv * (r * r * r) * jnp.mean(xv * g, axis=-1, keepdims=True)

        @pl.when(pl.program_id(0) == 0)
        def _():
            dw_ref[...] = jnp.zeros_like(dw_ref)
            loss_ref[...] = jnp.zeros_like(loss_ref)

        dw_ref[...] += jnp.sum(dy * xv * r, axis=0, keepdims=True)
        part = 0.5 * jnp.sum(jnp.mean(err * err, axis=-1, keepdims=True), axis=0, keepdims=True)
        loss_ref[...] += jnp.broadcast_to(part, loss_ref.shape)

    row = pl.BlockSpec((tm, D), lambda i: (i, 0))
    vec = pl.BlockSpec((1, D), lambda i: (0, 0))
    loss, dx, dw = pl.pallas_call(
        body, name="loss_head", grid=(S // tm,),
        in_specs=[row, vec, row],
        out_specs=[pl.BlockSpec((1, LANE), lambda i: (0, 0)), row, vec],
        out_shape=[jax.ShapeDtypeStruct((1, LANE), F32), jax.ShapeDtypeStruct((S, D), F32), jax.ShapeDtypeStruct((1, D), F32)],
        compiler_params=_cp(("arbitrary",)),
    )(x, w.reshape(1, D), target)
    return loss[0, 0], dx, dw.reshape(D)


SUB = 16
HG_W = HG_HEADS * LANE


def _hg_gates(q, f, lbv):
    sig = _sigmoid(f)
    fg = lbv + (1.0 - lbv) * sig
    sq = _sigmoid(q)
    return sig, fg, 1.0 - fg, sq, q * sq


def _hg_cumsum(logf):
    ri = lax.broadcasted_iota(I32, (CHUNK, CHUNK), 0)
    ci = lax.broadcasted_iota(I32, (CHUNK, CHUNK), 1)
    return _dot((ci <= ri).astype(F32), logf, NN, precision=lax.Precision.HIGHEST)


def _hg_rows():
    return lax.broadcasted_iota(I32, (CHUNK, LANE), 0)


def _hg_below(qf, kk, b, rows):
    blocks, parts = [jnp.zeros((SUB, CHUNK), F32)], []
    for i in range(1, CHUNK // SUB):
        bref = b[SUB * i - 1:SUB * i, :]
        rs = slice(SUB * i, SUB * (i + 1))
        eq = jnp.exp(b[rs] - bref)
        below = rows < SUB * i
        ek = jnp.exp(jnp.where(below, bref - b, NEG))
        qi = (qf[rs] * eq).astype(BF16)
        ki = (kk * ek).astype(BF16)
        blocks.append(_dot(qi, ki, NT))
        parts.append((qi, ki, eq, ek))
    return jnp.concatenate(blocks, axis=0), parts


def hgrn_fwd(proj, lb, nw):
    S = proj.shape[0]
    NC = S // CHUNK
    H = HG_HEADS

    def body(q_ref, f_ref, v_ref, g_ref, lb_ref, nw_ref, out_ref, st_out_ref, o_ref, st, kk_s, b_s):
        c = pl.program_id(0)

        @pl.when(c == 0)
        def _():
            st[...] = jnp.zeros_like(st)

        st_out_ref[...] = st[...]
        sig, fg, kk_all, sq, qf_all = _hg_gates(q_ref[...], f_ref[...], lb_ref[...])
        b_all = _hg_cumsum(jnp.log(fg))
        kk_s[...] = kk_all
        b_s[...] = b_all
        rows = _hg_rows()
        r16 = lax.broadcasted_iota(I32, (SUB, LANE), 0)
        for h in range(H):
            cs = slice(h * LANE, (h + 1) * LANE)
            qf, kk, b, v, g = qf_all[:, cs], kk_all[:, cs], b_all[:, cs], v_ref[:, cs], g_ref[:, cs]
            st_in = st[h]
            diag = []
            for i in range(CHUNK // SUB):
                rs = slice(SUB * i, SUB * (i + 1))
                acc = jnp.zeros((SUB, LANE), F32)
                for j in range(SUB):
                    row = pl.ds(SUB * i + j, 1)
                    e = jnp.exp(jnp.where(r16 >= j, b[rs] - b_s[row, cs], NEG))
                    col = jnp.sum(qf[rs] * (kk_s[row, cs] * e), axis=1, keepdims=True)
                    acc = acc + col * v_ref[row, cs]
                diag.append(acc)
            poff, _ = _hg_below(qf, kk, b, rows)
            vb = v.astype(BF16)
            bl = b[CHUNK - 1:CHUNK, :]
            o = (jnp.concatenate(diag, axis=0) + _dot(poff.astype(BF16), vb, NN)
                 + _dot((qf * jnp.exp(b)).astype(BF16), st_in.astype(BF16), NT))
            st[h] = st_in * jnp.exp(bl) + _dot(vb, (kk * jnp.exp(bl - b)).astype(BF16), TN)
            o_ref[:, cs] = o
            r = lax.rsqrt(jnp.mean(o * o, axis=-1, keepdims=True) + EPS)
            out_ref[:, cs] = (o * r * nw_ref[...] * (g * _sigmoid(g))).astype(out_ref.dtype)

    def seg(off):
        return pl.BlockSpec((CHUNK, HG_W), lambda c: (c, off // HG_W))

    blk = pl.BlockSpec((CHUNK, HG_W), lambda c: (c, 0))
    full = pltpu.VMEM((CHUNK, HG_W), F32)
    return pl.pallas_call(
        body, name="hgrn_fwd", grid=(NC,),
        in_specs=[seg(OFF_Q), seg(OFF_F), seg(OFF_V), seg(OFF_G),
                  pl.BlockSpec((1, HG_W), lambda c: (0, 0)), pl.BlockSpec((1, LANE), lambda c: (0, 0))],
        out_specs=[blk, pl.BlockSpec((None, H, LANE, LANE), lambda c: (c, 0, 0, 0)), blk],
        out_shape=[jax.ShapeDtypeStruct((S, 4 * HG_W), BF16), jax.ShapeDtypeStruct((NC, H, LANE, LANE), F32),
                   jax.ShapeDtypeStruct((S, HG_W), F32)],
        scratch_shapes=[pltpu.VMEM((H, LANE, LANE), F32), full, full],
        compiler_params=_cp(("arbitrary",)),
    )(proj, proj, proj, proj, lb.reshape(1, HG_W), nw.reshape(1, LANE))


def hgrn_bwd(proj, lb, nw, states, o_pre, dbcat, dproj):
    S = proj.shape[0]
    NC = S // CHUNK
    H = HG_HEADS

    def body(q_ref, f_ref, v_ref, g_ref, lb_ref, nw_ref, st_ref, o_ref, do_ref, _,
             dp_ref, dlb_ref, dnw_ref, dst, kk_s, b_s, do_s, db_s, dkk_s, dkk_d, dv_d):
        c = pl.program_id(0)
        dq_ref, df_ref, dv_ref, dg_ref = (dp_ref.at[:, pl.ds(off, HG_W)] for off in (OFF_Q, OFF_F, OFF_V, OFF_G))

        @pl.when(c == 0)
        def _():
            dst[...] = jnp.zeros_like(dst)
            dlb_ref[...] = jnp.zeros_like(dlb_ref)
            dnw_ref[...] = jnp.zeros_like(dnw_ref)

        q_all, g_all = q_ref[...], g_ref[...]
        lbv, nwv = lb_ref[...], nw_ref[...]
        sig, fg, kk_all, sq, qf_all = _hg_gates(q_all, f_ref[...], lbv)
        b_all = _hg_cumsum(jnp.log(fg))
        o_all = o_ref[...]
        dov = do_ref[...].astype(F32)
        sg = _sigmoid(g_all)
        gsg = g_all * sg
        dnw_acc = jnp.zeros((1, LANE), F32)
        for h in range(H):
            cs = slice(h * LANE, (h + 1) * LANE)
            o = o_all[:, cs]
            r = lax.rsqrt(jnp.mean(o * o, axis=-1, keepdims=True) + EPS)
            don = dov[:, cs] * gsg[:, cs]
            dnw_acc = dnw_acc + jnp.sum(don * o * r, axis=0, keepdims=True)
            gno = don * nwv
            do_s[:, cs] = r * gno - o * (r * r * r) * jnp.mean(o * gno, axis=-1, keepdims=True)
            dg_ref[:, cs] = (dov[:, cs] * (o * r * nwv) * _dsilu(g_all[:, cs], sg[:, cs])).astype(dg_ref.dtype)
        dnw_ref[...] += jnp.broadcast_to(dnw_acc, dnw_ref.shape)
        kk_s[...] = kk_all
        b_s[...] = b_all
        rows = _hg_rows()
        r16 = lax.broadcasted_iota(I32, (SUB, LANE), 0)
        for h in range(H):
            cs = slice(h * LANE, (h + 1) * LANE)
            qf, kk, b, v = qf_all[:, cs], kk_all[:, cs], b_all[:, cs], v_ref[:, cs]
            do = do_s[:, cs]
            st_in, dstv = st_ref[h], dst[h]
            bl = b[CHUNK - 1:CHUNK, :]
            eb, ebl, el = jnp.exp(b), jnp.exp(bl - b), jnp.exp(bl)
            qe, ke = qf * eb, kk * ebl
            vb, dob, stb, dstb = v.astype(BF16), do.astype(BF16), st_in.astype(BF16), dstv.astype(BF16)
            w_ = _dot(vb, dstb, NN)
            dqf = eb * _dot(dob, stb, NN)
            dkk = ebl * w_
            dv = _dot(ke.astype(BF16), dstb, NT)
            dbl = el * jnp.sum(st_in * dstv, axis=0, keepdims=True) + jnp.sum(ke * w_, axis=0, keepdims=True)
            dst[h] = dstv * el + _dot(dob, qe.astype(BF16), TN)
            poff, parts = _hg_below(qf, kk, b, rows)
            dpoff = _dot(dob, vb, NT).astype(BF16)
            dv = dv + _dot(poff.astype(BF16), dob, TN)
            dq_blocks = [jnp.zeros((SUB, LANE), F32)]
            for i, (qi, ki, eq, ek) in enumerate(parts, start=1):
                dpi = dpoff[SUB * i:SUB * (i + 1), :]
                dq_blocks.append(_dot(dpi, ki, NN) * eq)
                dkk = dkk + _dot(dpi, qi, TN) * ek
            dqf = dqf + jnp.concatenate(dq_blocks, axis=0)
            dq_diag = []
            for i in range(CHUNK // SUB):
                rs = slice(SUB * i, SUB * (i + 1))
                acc = jnp.zeros((SUB, LANE), F32)
                for j in range(SUB):
                    row = pl.ds(SUB * i + j, 1)
                    ks = kk_s[row, cs]
                    e = jnp.exp(jnp.where(r16 >= j, b[rs] - b_s[row, cs], NEG))
                    x = jnp.sum(do[rs] * v_ref[row, cs], axis=1, keepdims=True) * e
                    acc = acc + x * ks
                    dkk_d[row, cs] = jnp.sum(x * qf[rs], axis=0, keepdims=True)
                    col = jnp.sum(qf[rs] * (ks * e), axis=1, keepdims=True)
                    dv_d[row, cs] = jnp.sum(col * do[rs], axis=0, keepdims=True)
                dq_diag.append(acc)
            dqf = dqf + jnp.concatenate(dq_diag, axis=0)
            dkk = dkk + dkk_d[:, cs]
            dv = dv + dv_d[:, cs]
            dv_ref[:, cs] = dv.astype(dv_ref.dtype)
            db = qf * dqf - kk * dkk
            db_s[:, cs] = db + jnp.where(rows == CHUNK - 1, dbl, 0.0)
            dkk_s[:, cs] = dkk
            dq_ref[:, cs] = (dqf * _dsilu(q_all[:, cs], sq[:, cs])).astype(dq_ref.dtype)
        ri = lax.broadcasted_iota(I32, (CHUNK, CHUNK), 0)
        ci = lax.broadcasted_iota(I32, (CHUNK, CHUNK), 1)
        dlogf = _dot((ci >= ri).astype(F32), db_s[...], NN, precision=lax.Precision.HIGHEST)
        dfg = dlogf / fg - dkk_s[...]
        df_ref[...] = (dfg * (1.0 - lbv) * sig * (1.0 - sig)).astype(df_ref.dtype)
        dlb_ref[...] += jnp.broadcast_to(jnp.sum(dfg * (1.0 - sig), axis=0, keepdims=True), dlb_ref.shape)

    def seg(off):
        return pl.BlockSpec((CHUNK, HG_W), lambda c: (NC - 1 - c, off // HG_W))

    blk = pl.BlockSpec((CHUNK, HG_W), lambda c: (NC - 1 - c, 0))
    full = pltpu.VMEM((CHUNK, HG_W), F32)
    dproj, dlb, dnw = pl.pallas_call(
        body, name="hgrn_bwd", grid=(NC,),
        in_specs=[seg(OFF_Q), seg(OFF_F), seg(OFF_V), seg(OFF_G),
                  pl.BlockSpec((1, HG_W), lambda c: (0, 0)), pl.BlockSpec((1, LANE), lambda c: (0, 0)),
                  pl.BlockSpec((None, H, LANE, LANE), lambda c: (NC - 1 - c, 0, 0, 0)), blk, blk, ANY_SPACE],
        out_specs=[pl.BlockSpec((CHUNK, 4 * HG_W), lambda c: (NC - 1 - c, 0)),
                   pl.BlockSpec((8, HG_W), lambda c: (0, 0)), pl.BlockSpec((8, LANE), lambda c: (0, 0))],
        out_shape=[jax.ShapeDtypeStruct(dproj.shape, dproj.dtype), jax.ShapeDtypeStruct((8, HG_W), F32),
                   jax.ShapeDtypeStruct((8, LANE), F32)],
        input_output_aliases={9: 0},
        scratch_shapes=[pltpu.VMEM((H, LANE, LANE), F32)] + [full] * 7,
        compiler_params=_cp(("arbitrary",)),
    )(proj, proj, proj, proj, lb.reshape(1, HG_W), nw.reshape(1, LANE), states, o_pre, dbcat, dproj)
    return dproj, dlb[0], dnw[0]


CV_PAD = 32
ROWS = 256


def _colblk(S, off):
    return pl.BlockSpec((S, LANE), lambda j: (0, off // LANE + j))


def cv_fwd(proj, w32, bias):
    S = proj.shape[0]
    nchunk = S // ROWS

    def body(a_ref, g_ref, w_ref, b_ref, o_ref, zpad):
        zpad[pl.ds(0, CV_PAD), :] = jnp.zeros((CV_PAD, LANE), F32)

        def glu(c, _):
            r0 = pl.multiple_of(c * ROWS, ROWS)
            zpad[pl.ds(CV_PAD + r0, ROWS), :] = a_ref[pl.ds(r0, ROWS), :] * _sigmoid(g_ref[pl.ds(r0, ROWS), :])
            return 0

        lax.fori_loop(0, nchunk, glu, 0)

        def conv(c, _):
            r0 = pl.multiple_of(c * ROWS, ROWS)
            acc = jnp.broadcast_to(b_ref[...], (ROWS, LANE))
            for j in range(CV_KERNEL):
                acc = acc + w_ref[pl.ds(j, 1), :] * zpad[pl.ds(r0 + (CV_PAD - CV_KERNEL + 1) + j, ROWS), :]
            o_ref[pl.ds(r0, ROWS), :] = acc
            return 0

        lax.fori_loop(0, nchunk, conv, 0)

    return pl.pallas_call(
        body, name="cv_fwd", grid=(BRANCH_W // LANE,),
        in_specs=[_colblk(S, OFF_CV), _colblk(S, OFF_CV + BRANCH_W),
                  pl.BlockSpec((32, LANE), lambda j: (0, j)), pl.BlockSpec((1, LANE), lambda j: (0, j))],
        out_specs=pl.BlockSpec((S, LANE), lambda j: (0, j)), out_shape=jax.ShapeDtypeStruct((S, BRANCH_W), F32),
        scratch_shapes=[pltpu.VMEM((CV_PAD + S, LANE), F32)],
        compiler_params=_cp(("parallel",)),
    )(proj, proj, w32, bias.reshape(1, BRANCH_W))


def cv_bwd(proj, w32, dzc):
    S = proj.shape[0]
    nchunk = S // ROWS

    def body(a_ref, g_ref, w_ref, dz_ref, da_ref, dg_ref, dw_ref, db_ref, zpad, dpad):
        zpad[pl.ds(0, CV_PAD), :] = jnp.zeros((CV_PAD, LANE), F32)
        dpad[pl.ds(S, CV_PAD), :] = jnp.zeros((CV_PAD, LANE), F32)
        dw_ref[...] = jnp.zeros_like(dw_ref)

        def glu(c, dsum):
            r0 = pl.multiple_of(c * ROWS, ROWS)
            zpad[pl.ds(CV_PAD + r0, ROWS), :] = a_ref[pl.ds(r0, ROWS), :] * _sigmoid(g_ref[pl.ds(r0, ROWS), :])
            d = dz_ref[pl.ds(r0, ROWS), :]
            dpad[pl.ds(r0, ROWS), :] = d
            return dsum + jnp.sum(d, axis=0, keepdims=True)

        dsum = lax.fori_loop(0, nchunk, glu, jnp.zeros((1, LANE), F32))
        db_ref[...] = jnp.broadcast_to(dsum, db_ref.shape)

        def conv(c, _):
            r0 = pl.multiple_of(c * ROWS, ROWS)
            d = dpad[pl.ds(r0, ROWS), :]
            acc = jnp.zeros((ROWS, LANE), F32)
            for j in range(CV_KERNEL):
                acc = acc + w_ref[pl.ds(j, 1), :] * dpad[pl.ds(r0 + (CV_KERNEL - 1) - j, ROWS), :]
                zs = zpad[pl.ds(r0 + (CV_PAD - CV_KERNEL + 1) + j, ROWS), :]
                dw_ref[pl.ds(j, 1), :] += jnp.sum(d * zs, axis=0, keepdims=True)
            a = a_ref[pl.ds(r0, ROWS), :]
            sg = _sigmoid(g_ref[pl.ds(r0, ROWS), :])
            da_ref[pl.ds(r0, ROWS), :] = (acc * sg).astype(da_ref.dtype)
            dg_ref[pl.ds(r0, ROWS), :] = (acc * a * sg * (1.0 - sg)).astype(dg_ref.dtype)
            return 0

        lax.fori_loop(0, nchunk, conv, 0)

    blk = pl.BlockSpec((S, LANE), lambda j: (0, j))
    da, dg, dw, db = pl.pallas_call(
        body, name="cv_bwd", grid=(BRANCH_W // LANE,),
        in_specs=[_colblk(S, OFF_CV), _colblk(S, OFF_CV + BRANCH_W), pl.BlockSpec((32, LANE), lambda j: (0, j)), blk],
        out_specs=[blk, blk, pl.BlockSpec((32, LANE), lambda j: (0, j)), pl.BlockSpec((8, LANE), lambda j: (0, j))],
        out_shape=[jax.ShapeDtypeStruct((S, BRANCH_W), BF16), jax.ShapeDtypeStruct((S, BRANCH_W), BF16),
                   jax.ShapeDtypeStruct((32, BRANCH_W), F32), jax.ShapeDtypeStruct((8, BRANCH_W), F32)],
        scratch_shapes=[pltpu.VMEM((CV_PAD + S, LANE), F32), pltpu.VMEM((S + CV_PAD, LANE), F32)],
        compiler_params=_cp(("parallel",)),
    )(proj, proj, w32, dzc)
    return da, dg, dw, db[0]


def ln_silu_fwd(z, w, b, bcat, tm=512):
    S, C = z.shape
    tm = _tile(S, tm)

    def body(z_ref, w_ref, b_ref, _, o_ref):
        zv = z_ref[...]
        mu = jnp.mean(zv, axis=-1, keepdims=True)
        zc = zv - mu
        rstd = lax.rsqrt(jnp.mean(zc * zc, axis=-1, keepdims=True) + EPS)
        y = zc * rstd * w_ref[...] + b_ref[...]
        o_ref[...] = (y * _sigmoid(y)).astype(o_ref.dtype)

    row = pl.BlockSpec((tm, C), lambda i: (i, 0))
    vec = pl.BlockSpec((1, C), lambda i: (0, 0))
    return pl.pallas_call(
        body, name="ln_silu_fwd", grid=(S // tm,), in_specs=[row, vec, vec, ANY_SPACE],
        out_specs=pl.BlockSpec((tm, C), lambda i: (i, 1)), out_shape=jax.ShapeDtypeStruct(bcat.shape, bcat.dtype),
        input_output_aliases={3: 0}, compiler_params=_cp(("parallel",)),
    )(z, w.reshape(1, C), b.reshape(1, C), bcat)


def ln_silu_bwd(z, w, b, dbcat, tm=512):
    S, C = z.shape
    tm = _tile(S, tm)

    def body(z_ref, w_ref, b_ref, do_ref, dz_ref, dw_ref, db_ref):
        zv = z_ref[...]
        wv = w_ref[...]
        mu = jnp.mean(zv, axis=-1, keepdims=True)
        zc = zv - mu
        rstd = lax.rsqrt(jnp.mean(zc * zc, axis=-1, keepdims=True) + EPS)
        xh = zc * rstd
        y = xh * wv + b_ref[...]
        dy = do_ref[...].astype(F32) * _dsilu(y, _sigmoid(y))

        @pl.when(pl.program_id(0) == 0)
        def _():
            dw_ref[...] = jnp.zeros_like(dw_ref)
            db_ref[...] = jnp.zeros_like(db_ref)

        dw_ref[...] += jnp.sum(dy * xh, axis=0, keepdims=True)
        db_ref[...] += jnp.sum(dy, axis=0, keepdims=True)
        dxh = dy * wv
        dz_ref[...] = rstd * (dxh - jnp.mean(dxh, axis=-1, keepdims=True) - xh * jnp.mean(dxh * xh, axis=-1, keepdims=True))

    row = pl.BlockSpec((tm, C), lambda i: (i, 0))
    vec = pl.BlockSpec((1, C), lambda i: (0, 0))
    dz, dw, db = pl.pallas_call(
        body, name="ln_silu_bwd", grid=(S // tm,), in_specs=[row, vec, vec, pl.BlockSpec((tm, C), lambda i: (i, 1))],
        out_specs=[row, vec, vec],
        out_shape=[jax.ShapeDtypeStruct((S, C), F32), jax.ShapeDtypeStruct((1, C), F32), jax.ShapeDtypeStruct((1, C), F32)],
        compiler_params=_cp(("arbitrary",)),
    )(z, w.reshape(1, C), b.reshape(1, C), dbcat)
    return dz, dw.reshape(C), db.reshape(C)


PL_PAD = 16


def _pool_counts(r0, win):
    t = r0 + lax.broadcasted_iota(I32, (ROWS, LANE), 0)
    return jnp.minimum(t + 1, win).astype(F32)


def pool_fwd(proj, wg, scale, bcat):
    S = proj.shape[0]
    nchunk = S // ROWS

    def body(u_ref, w_ref, s_ref, _, o_ref, upad):
        g = pl.program_id(0)
        upad[pl.ds(0, PL_PAD), :] = jnp.zeros((PL_PAD, LANE), F32)

        def fill(c, _):
            r0 = pl.multiple_of(c * ROWS, ROWS)
            upad[pl.ds(PL_PAD + r0, ROWS), :] = u_ref[pl.ds(r0, ROWS), :]
            return 0

        lax.fori_loop(0, nchunk, fill, 0)
        wb = w_ref[...].astype(BF16)
        for gi, win in enumerate(POOL_WINDOWS):
            @pl.when(g == gi)
            def _(win=win):
                def chunk(c, _):
                    r0 = pl.multiple_of(c * ROWS, ROWS)
                    u = upad[pl.ds(PL_PAD + r0, ROWS), :]
                    ws = u
                    for j in range(1, win):
                        ws = ws + upad[pl.ds(PL_PAD + r0 - j, ROWS), :]
                    pooled = ws / _pool_counts(r0, win) - u
                    o_ref[pl.ds(r0, ROWS), :] = (_dot(pooled.astype(BF16), wb, NN) * s_ref[...]).astype(o_ref.dtype)
                    return 0

                lax.fori_loop(0, nchunk, chunk, 0)

    return pl.pallas_call(
        body, name="pool_fwd", grid=(len(POOL_WINDOWS),),
        in_specs=[_colblk(S, OFF_PL), pl.BlockSpec((None, LANE, LANE), lambda j: (j, 0, 0)), pl.BlockSpec((1, LANE), lambda j: (0, j)),
                  ANY_SPACE],
        out_specs=pl.BlockSpec((S, LANE), lambda j: (0, 2 * BRANCH_W // LANE + j)),
        out_shape=jax.ShapeDtypeStruct(bcat.shape, bcat.dtype), input_output_aliases={3: 0},
        scratch_shapes=[pltpu.VMEM((PL_PAD + S, LANE), F32)],
        compiler_params=_cp(("parallel",)),
    )(proj, wg, scale.reshape(1, BRANCH_W), bcat)


def pool_bwd(proj, wg, scale, dbcat, dproj):
    S = proj.shape[0]
    nchunk = S // ROWS

    def body(u_ref, w_ref, s_ref, dy_ref, _, du_ref, dw_ref, ds_ref, upad, dpn, nd):
        g = pl.program_id(0)
        upad[pl.ds(0, PL_PAD), :] = jnp.zeros((PL_PAD, LANE), F32)
        dpn[pl.ds(S, PL_PAD), :] = jnp.zeros((PL_PAD, LANE), F32)

        def fill(c, _):
            r0 = pl.multiple_of(c * ROWS, ROWS)
            upad[pl.ds(PL_PAD + r0, ROWS), :] = u_ref[pl.ds(r0, ROWS), :]
            return 0

        lax.fori_loop(0, nchunk, fill, 0)
        wb = w_ref[...].astype(BF16)
        sv = s_ref[...]
        for gi, win in enumerate(POOL_WINDOWS):
            @pl.when(g == gi)
            def _(win=win):
                def chunk(c, carry):
                    dw, dsc = carry
                    r0 = pl.multiple_of(c * ROWS, ROWS)
                    u = upad[pl.ds(PL_PAD + r0, ROWS), :]
                    ws = u
                    for j in range(1, win):
                        ws = ws + upad[pl.ds(PL_PAD + r0 - j, ROWS), :]
                    cnt = _pool_counts(r0, win)
                    pooled = (ws / cnt - u).astype(BF16)
                    dyv = dy_ref[pl.ds(r0, ROWS), :].astype(F32)
                    dsc = dsc + jnp.sum(dyv * _dot(pooled, wb, NN), axis=0, keepdims=True)
                    dys = (dyv * sv).astype(BF16)
                    dw = dw + _dot(pooled, dys, TN)
                    dp = _dot(dys, wb, NT)
                    dpn[pl.ds(r0, ROWS), :] = dp / cnt
                    nd[pl.ds(r0, ROWS), :] = -dp
                    return dw, dsc

                dw, dsc = lax.fori_loop(0, nchunk, chunk, (jnp.zeros((LANE, LANE), F32), jnp.zeros((1, LANE), F32)))
                dw_ref[...] = dw
                ds_ref[...] = jnp.broadcast_to(dsc, ds_ref.shape)

                def spread(c, _):
                    r0 = pl.multiple_of(c * ROWS, ROWS)
                    acc = nd[pl.ds(r0, ROWS), :]
                    for j in range(win):
                        acc = acc + dpn[pl.ds(r0 + j, ROWS), :]
                    du_ref[pl.ds(r0, ROWS), :] = acc.astype(du_ref.dtype)
                    return 0

                lax.fori_loop(0, nchunk, spread, 0)

    dproj, dw, ds = pl.pallas_call(
        body, name="pool_bwd", grid=(len(POOL_WINDOWS),),
        in_specs=[_colblk(S, OFF_PL), pl.BlockSpec((None, LANE, LANE), lambda j: (j, 0, 0)), pl.BlockSpec((1, LANE), lambda j: (0, j)),
                  pl.BlockSpec((S, LANE), lambda j: (0, 2 * BRANCH_W // LANE + j)), ANY_SPACE],
        out_specs=[_colblk(S, OFF_PL), pl.BlockSpec((None, LANE, LANE), lambda j: (j, 0, 0)), pl.BlockSpec((8, LANE), lambda j: (0, j))],
        out_shape=[jax.ShapeDtypeStruct(dproj.shape, dproj.dtype), jax.ShapeDtypeStruct((len(POOL_WINDOWS), LANE, LANE), F32),
                   jax.ShapeDtypeStruct((8, BRANCH_W), F32)],
        input_output_aliases={4: 0},
        scratch_shapes=[pltpu.VMEM((PL_PAD + S, LANE), F32), pltpu.VMEM((S + PL_PAD, LANE), F32), pltpu.VMEM((S, LANE), F32)],
        compiler_params=_cp(("parallel",)),
    )(proj, wg, scale.reshape(1, BRANCH_W), dbcat, dproj)
    return dproj, dw, ds[0]


LR_PAD = 8
SCAN_TILES = 4
GELU_C = math.sqrt(2.0 / math.pi)
GELU_A = 0.044715


def _gelu(y):
    return 0.5 * y * (1.0 + jnp.tanh(GELU_C * (y + GELU_A * y * y * y)))


def _dgelu(y):
    t = jnp.tanh(GELU_C * (y + GELU_A * y * y * y))
    return 0.5 * (1.0 + t) + 0.5 * y * (1.0 - t * t) * GELU_C * (1.0 + 3.0 * GELU_A * y * y)


def _lru_gates(xpad, r0, cw_ref, cb, wa, ba, wx, bx, sp8):
    xc = jnp.broadcast_to(cb, (ROWS, LANE))
    for j in range(LRU_CONV):
        xc = xc + cw_ref[pl.ds(j, 1), :] * xpad[pl.ds(r0 + (LR_PAD - LRU_CONV + 1) + j, ROWS), :]
    xb = xc.astype(BF16)
    r = _sigmoid(_dot(xb, wa, NN) + ba)
    ig = _sigmoid(_dot(xb, wx, NN) + bx)
    la = -sp8 * r
    a = jnp.exp(la)
    s = jnp.sqrt(-jnp.tanh(la) * (a * a + 1.0))
    return xc, r, ig, a, s


def _tile_scan(a, b, r8, up):
    for s in (1, 2, 4):
        keep = (r8 < 8 - s) if up else (r8 >= s)
        shift = 8 - s if up else s
        a_sh = jnp.where(keep, pltpu.roll(a, shift, 0), 1.0)
        b_sh = jnp.where(keep, pltpu.roll(b, shift, 0), 0.0)
        b = b + a * b_sh
        a = a * a_sh
    return a, b


def lru_fwd(proj, cw8, cb, wa_bd, ba, wx_bd, bx, sp8, bcat):
    S = proj.shape[0]
    nchunk = S // ROWS

    def body(x_ref, y_ref, cw_ref, cb_ref, wa_ref, ba_ref, wx_ref, bx_ref, sp_ref, _, o_ref, h_ref, xpad, a_s):
        xpad[pl.ds(0, LR_PAD), :] = jnp.zeros((LR_PAD, LANE), F32)

        def fill(c, _):
            r0 = pl.multiple_of(c * ROWS, ROWS)
            xpad[pl.ds(LR_PAD + r0, ROWS), :] = x_ref[pl.ds(r0, ROWS), :]
            return 0

        lax.fori_loop(0, nchunk, fill, 0)
        wa = wa_ref[...].astype(BF16)
        wx = wx_ref[...].astype(BF16)

        def gates(c, _):
            r0 = pl.multiple_of(c * ROWS, ROWS)
            xc, r, ig, a, s = _lru_gates(xpad, r0, cw_ref, cb_ref[...], wa, ba_ref[...], wx, bx_ref[...], sp_ref[...])
            a_s[pl.ds(r0, ROWS), :] = a
            h_ref[pl.ds(r0, ROWS), :] = s * (ig * xc)
            return 0

        lax.fori_loop(0, nchunk, gates, 0)

        r8 = lax.broadcasted_iota(I32, (8, LANE), 0)

        def scan(i, h):
            bases = [pl.multiple_of(i * (8 * SCAN_TILES) + 8 * j, 8) for j in range(SCAN_TILES)]
            maps = [_tile_scan(a_s[pl.ds(b, 8), :], h_ref[pl.ds(b, 8), :], r8, False) for b in bases]
            for b, (ca, cb_) in zip(bases, maps):
                out = cb_ + ca * h
                h_ref[pl.ds(b, 8), :] = out
                h = out[7:8, :]
            return h

        lax.fori_loop(0, S // (8 * SCAN_TILES), scan, jnp.zeros((1, LANE), F32))

        def gate_out(c, _):
            r0 = pl.multiple_of(c * ROWS, ROWS)
            o_ref[pl.ds(r0, ROWS), :] = (h_ref[pl.ds(r0, ROWS), :] * _gelu(y_ref[pl.ds(r0, ROWS), :])).astype(o_ref.dtype)
            return 0

        lax.fori_loop(0, nchunk, gate_out, 0)

    vec = pl.BlockSpec((1, LANE), lambda j: (0, j))
    mat = pl.BlockSpec((None, LANE, LANE), lambda j: (j, 0, 0))
    blk = pl.BlockSpec((S, LANE), lambda j: (0, j))
    return pl.pallas_call(
        body, name="lru_fwd", grid=(BRANCH_W // LANE,),
        in_specs=[_colblk(S, OFF_LX), _colblk(S, OFF_LY), pl.BlockSpec((8, LANE), lambda j: (0, j)), vec, mat, vec, mat, vec, vec,
                  ANY_SPACE],
        out_specs=[pl.BlockSpec((S, LANE), lambda j: (0, 3 * BRANCH_W // LANE + j)), blk],
        out_shape=[jax.ShapeDtypeStruct(bcat.shape, bcat.dtype), jax.ShapeDtypeStruct((S, BRANCH_W), F32)],
        input_output_aliases={9: 0},
        scratch_shapes=[pltpu.VMEM((LR_PAD + S, LANE), F32), pltpu.VMEM((S, LANE), F32)],
        compiler_params=_cp(("parallel",)),
    )(proj, proj, cw8, cb.reshape(1, -1), wa_bd, ba.reshape(1, -1), wx_bd, bx.reshape(1, -1), sp8.reshape(1, -1), bcat)


def lru_bwd(proj, cw8, cb, wa_bd, ba, wx_bd, bx, sp8, h, dbcat):
    S = proj.shape[0]
    nchunk = S // ROWS

    def body(x_ref, y_ref, cw_ref, cb_ref, wa_ref, ba_ref, wx_ref, bx_ref, sp_ref, h_ref, do_ref,
             dx_ref, dy_ref, dcw_ref, dcb_ref, dwa_ref, dba_ref, dwx_ref, dbx_ref, dsp_ref,
             xpad, a_s, g_s, hpad, dxc):
        xpad[pl.ds(0, LR_PAD), :] = jnp.zeros((LR_PAD, LANE), F32)
        hpad[pl.ds(0, LR_PAD), :] = jnp.zeros((LR_PAD, LANE), F32)
        dxc[pl.ds(S, LR_PAD), :] = jnp.zeros((LR_PAD, LANE), F32)
        dcw_ref[...] = jnp.zeros_like(dcw_ref)
        wa = wa_ref[...].astype(BF16)
        wx = wx_ref[...].astype(BF16)
        cbv, bav, bxv, spv = cb_ref[...], ba_ref[...], bx_ref[...], sp_ref[...]

        def fill(c, _):
            r0 = pl.multiple_of(c * ROWS, ROWS)
            xpad[pl.ds(LR_PAD + r0, ROWS), :] = x_ref[pl.ds(r0, ROWS), :]
            hv = h_ref[pl.ds(r0, ROWS), :]
            hpad[pl.ds(LR_PAD + r0, ROWS), :] = hv
            yv = y_ref[pl.ds(r0, ROWS), :]
            dov = do_ref[pl.ds(r0, ROWS), :].astype(F32)
            g_s[pl.ds(r0, ROWS), :] = dov * _gelu(yv)
            dy_ref[pl.ds(r0, ROWS), :] = (dov * hv * _dgelu(yv)).astype(dy_ref.dtype)
            return 0

        lax.fori_loop(0, nchunk, fill, 0)

        def gates(c, _):
            r0 = pl.multiple_of(c * ROWS, ROWS)
            _, _, _, a, _ = _lru_gates(xpad, r0, cw_ref, cbv, wa, bav, wx, bxv, spv)
            a_s[pl.ds(r0, ROWS), :] = a
            return 0

        lax.fori_loop(0, nchunk, gates, 0)

        r8 = lax.broadcasted_iota(I32, (8, LANE), 0)

        def rscan(i, carry):
            bases = [pl.multiple_of(S - 8 - i * (8 * SCAN_TILES) - 8 * j, 8) for j in range(SCAN_TILES)]
            firsts, maps = [], []
            for b in bases:
                a8 = a_s[pl.ds(b, 8), :]
                above = jnp.where(r8 < 7, pltpu.roll(a8, 7, 0), 1.0)
                firsts.append(a8[0:1, :])
                maps.append(_tile_scan(above, g_s[pl.ds(b, 8), :], r8, True))
            for b, a0, (ca, cb_) in zip(bases, firsts, maps):
                out = cb_ + ca * carry
                g_s[pl.ds(b, 8), :] = out
                carry = a0 * out[0:1, :]
            return carry

        lax.fori_loop(0, S // (8 * SCAN_TILES), rscan, jnp.zeros((1, LANE), F32))

        def chain(c, carry):
            dwa, dwx, dba, dbx, dsp, dcb = carry
            r0 = pl.multiple_of(c * ROWS, ROWS)
            xc, r, ig, a, s = _lru_gates(xpad, r0, cw_ref, cbv, wa, bav, wx, bxv, spv)
            gt = g_s[pl.ds(r0, ROWS), :]
            hprev = hpad[pl.ds(r0 + LR_PAD - 1, ROWS), :]
            da = gt * hprev - gt * ig * xc * (a / s)
            dig = gt * s * xc
            dla = da * a
            dsp = dsp + jnp.sum(-dla * r, axis=0, keepdims=True)
            dpr = (-dla * spv) * r * (1.0 - r)
            dpi = dig * ig * (1.0 - ig)
            dprb, dpib, xb = dpr.astype(BF16), dpi.astype(BF16), xc.astype(BF16)
            d = gt * s * ig + _dot(dprb, wa, NT) + _dot(dpib, wx, NT)
            dwa = dwa + _dot(xb, dprb, TN)
            dwx = dwx + _dot(xb, dpib, TN)
            dba = dba + jnp.sum(dpr, axis=0, keepdims=True)
            dbx = dbx + jnp.sum(dpi, axis=0, keepdims=True)
            dcb = dcb + jnp.sum(d, axis=0, keepdims=True)
            dxc[pl.ds(r0, ROWS), :] = d
            for j in range(LRU_CONV):
                xs = xpad[pl.ds(r0 + (LR_PAD - LRU_CONV + 1) + j, ROWS), :]
                dcw_ref[pl.ds(j, 1), :] += jnp.sum(d * xs, axis=0, keepdims=True)
            return dwa, dwx, dba, dbx, dsp, dcb

        zm, zv = jnp.zeros((LANE, LANE), F32), jnp.zeros((1, LANE), F32)
        dwa, dwx, dba, dbx, dsp, dcb = lax.fori_loop(0, nchunk, chain, (zm, zm, zv, zv, zv, zv))
        dwa_ref[...] = dwa
        dwx_ref[...] = dwx
        dba_ref[...] = jnp.broadcast_to(dba, dba_ref.shape)
        dbx_ref[...] = jnp.broadcast_to(dbx, dbx_ref.shape)
        dsp_ref[...] = jnp.broadcast_to(dsp, dsp_ref.shape)
        dcb_ref[...] = jnp.broadcast_to(dcb, dcb_ref.shape)

        def convt(c, _):
            r0 = pl.multiple_of(c * ROWS, ROWS)
            acc = jnp.zeros((ROWS, LANE), F32)
            for j in range(LRU_CONV):
                acc = acc + cw_ref[pl.ds(j, 1), :] * dxc[pl.ds(r0 + (LRU_CONV - 1) - j, ROWS), :]
            dx_ref[pl.ds(r0, ROWS), :] = acc.astype(dx_ref.dtype)
            return 0

        lax.fori_loop(0, nchunk, convt, 0)

    vec = pl.BlockSpec((1, LANE), lambda j: (0, j))
    vec8 = pl.BlockSpec((8, LANE), lambda j: (0, j))
    mat = pl.BlockSpec((None, LANE, LANE), lambda j: (j, 0, 0))
    blk = pl.BlockSpec((S, LANE), lambda j: (0, j))
    nblk = BRANCH_W // LANE
    v8 = jax.ShapeDtypeStruct((8, BRANCH_W), F32)
    m4 = jax.ShapeDtypeStruct((nblk, LANE, LANE), F32)
    big = jax.ShapeDtypeStruct((S, BRANCH_W), BF16)
    seq = pltpu.VMEM((S, LANE), F32)
    dx, dy, dcw, dcb, dwa, dba, dwx, dbx, dsp = pl.pallas_call(
        body, name="lru_bwd", grid=(nblk,),
        in_specs=[_colblk(S, OFF_LX), _colblk(S, OFF_LY), vec8, vec, mat, vec, mat, vec, vec, blk,
                  pl.BlockSpec((S, LANE), lambda j: (0, 3 * BRANCH_W // LANE + j))],
        out_specs=[blk, blk, vec8, vec8, mat, vec8, mat, vec8, vec8],
        out_shape=[big, big, v8, v8, m4, v8, m4, v8, v8],
        scratch_shapes=[pltpu.VMEM((LR_PAD + S, LANE), F32), seq, seq, pltpu.VMEM((LR_PAD + S, LANE), F32),
                        pltpu.VMEM((S + LR_PAD, LANE), F32)],
        compiler_params=_cp(("parallel",)),
    )(proj, proj, cw8, cb.reshape(1, -1), wa_bd, ba.reshape(1, -1), wx_bd, bx.reshape(1, -1), sp8.reshape(1, -1), h, dbcat)
    return dx, dy, dcw, dcb[0], dwa, dba[0], dwx, dbx[0], dsp[0]


MG_COLS = 512
N_BRANCH = 4
BCAT_W = N_BRANCH * BRANCH_W


def _gate_spec(tm, k):
    return pl.BlockSpec((tm, MG_COLS), lambda j, i: (i, (OFF_GATE + k * D_MODEL) // MG_COLS + j))


def _up_spec(tm, k):
    return pl.BlockSpec((tm, MG_COLS), lambda j, i: (i, k * D_MODEL // MG_COLS + j))


def merge_fwd(ups, proj, gate_b, tm=256):
    S = proj.shape[0]
    tm = _tile(S, tm)

    def body(u0, u1, u2, u3, g0, g1, g2, g3, gb_ref, o_ref):
        acc = jnp.zeros((tm, MG_COLS), F32)
        for k, (u, g) in enumerate(((u0, g0), (u1, g1), (u2, g2), (u3, g3))):
            acc = acc + _sigmoid(g[...] + gb_ref[pl.ds(k, 1), :]) * u[...]
        o_ref[...] = acc.astype(o_ref.dtype)

    blk = pl.BlockSpec((tm, MG_COLS), lambda j, i: (i, j))
    return pl.pallas_call(
        body, name="merge_fwd", grid=(D_MODEL // MG_COLS, S // tm),
        in_specs=[_up_spec(tm, k) for k in range(N_BRANCH)] + [_gate_spec(tm, k) for k in range(N_BRANCH)]
        + [pl.BlockSpec((N_BRANCH, MG_COLS), lambda j, i: (0, j))],
        out_specs=blk, out_shape=jax.ShapeDtypeStruct((S, D_MODEL), BF16),
        compiler_params=_cp(("parallel", "parallel")),
    )(ups, ups, ups, ups, proj, proj, proj, proj, gate_b)


def merge_bwd(dmerged, ups, proj, gate_b, tm=256):
    S = proj.shape[0]
    tm = _tile(S, tm)
    halves = D_MODEL // MG_COLS

    def body(dm_ref, u_ref, g_ref, gb_ref, du_ref, dg_ref, dgb_ref):
        k = pl.program_id(0)

        @pl.when(pl.program_id(2) == 0)
        def _():
            dgb_ref[...] = jnp.zeros_like(dgb_ref)

        dm = dm_ref[...].astype(F32)
        sg = _sigmoid(g_ref[...] + gb_ref[pl.ds(k, 1), :])
        du_ref[...] = (dm * sg).astype(du_ref.dtype)
        dgk = dm * u_ref[...] * sg * (1.0 - sg)
        dg_ref[...] = dgk.astype(dg_ref.dtype)
        dgb_ref[...] += jnp.broadcast_to(jnp.sum(dgk, axis=0, keepdims=True), dgb_ref.shape)

    dups, dproj, dgb = pl.pallas_call(
        body, name="merge_bwd", grid=(N_BRANCH, halves, S // tm),
        in_specs=[pl.BlockSpec((tm, MG_COLS), lambda k, j, i: (i, j)),
                  pl.BlockSpec((tm, MG_COLS), lambda k, j, i: (i, k * halves + j)),
                  pl.BlockSpec((tm, MG_COLS), lambda k, j, i: (i, OFF_GATE // MG_COLS + k * halves + j)),
                  pl.BlockSpec((N_BRANCH, MG_COLS), lambda k, j, i: (0, j))],
        out_specs=[pl.BlockSpec((tm, MG_COLS), lambda k, j, i: (i, k * halves + j)),
                   pl.BlockSpec((tm, MG_COLS), lambda k, j, i: (i, OFF_GATE // MG_COLS + k * halves + j)),
                   pl.BlockSpec((8, MG_COLS), lambda k, j, i: (k, j))],
        out_shape=[jax.ShapeDtypeStruct((S, N_BRANCH * D_MODEL), BF16), jax.ShapeDtypeStruct((S, IN_W), BF16),
                   jax.ShapeDtypeStruct((8 * N_BRANCH, D_MODEL), F32)],
        compiler_params=_cp(("parallel", "parallel", "arbitrary")),
    )(dmerged, ups, proj, gate_b)
    return dups, dproj, dgb.reshape(N_BRANCH, 8, D_MODEL)[:, 0]


def attn_fwd(q, kv, tm=512):
    S = q.shape[0]
    M = kv.shape[0]
    tm = _tile(S, tm)
    scale = XA_HD ** -0.5

    def body(q_ref, kv_ref, o_ref):
        for hh in range(XA_HEADS):
            cs = pl.ds(hh * XA_HD, XA_HD)
            qh = q_ref[:, cs]
            kh = kv_ref[:, cs]
            vh = kv_ref[:, pl.ds(D_MODEL + hh * XA_HD, XA_HD)]
            s = _dot(qh, kh, NT) * scale
            p = jnp.exp(s - jnp.max(s, axis=-1, keepdims=True))
            p = p / jnp.sum(p, axis=-1, keepdims=True)
            o_ref[:, cs] = _dot(p.astype(BF16), vh, NN).astype(o_ref.dtype)

    return pl.pallas_call(
        body, name="attn_fwd", grid=(S // tm,),
        in_specs=[pl.BlockSpec((tm, D_MODEL), lambda i: (i, 0)), pl.BlockSpec((M, 2 * D_MODEL), lambda i: (0, 0))],
        out_specs=pl.BlockSpec((tm, D_MODEL), lambda i: (i, 0)), out_shape=jax.ShapeDtypeStruct((S, D_MODEL), BF16),
        compiler_params=_cp(("parallel",)),
    )(q, kv)


def attn_bwd(q, kv, do, tm=512):
    S = q.shape[0]
    M = kv.shape[0]
    tm = _tile(S, tm)
    scale = XA_HD ** -0.5

    def body(q_ref, kv_ref, do_ref, dq_ref, dkv_ref):
        @pl.when(pl.program_id(0) == 0)
        def _():
            dkv_ref[...] = jnp.zeros_like(dkv_ref)

        for hh in range(XA_HEADS):
            cs = pl.ds(hh * XA_HD, XA_HD)
            vs = pl.ds(D_MODEL + hh * XA_HD, XA_HD)
            qh = q_ref[:, cs]
            kh = kv_ref[:, cs]
            vh = kv_ref[:, vs]
            doh = do_ref[:, cs]
            s = _dot(qh, kh, NT) * scale
            p = jnp.exp(s - jnp.max(s, axis=-1, keepdims=True))
            p = p / jnp.sum(p, axis=-1, keepdims=True)
            dp = _dot(doh, vh, NT)
            ds = (p * (dp - jnp.sum(dp * p, axis=-1, keepdims=True)) * scale).astype(BF16)
            dq_ref[:, cs] = _dot(ds, kh, NN).astype(dq_ref.dtype)
            dkv_ref[:, cs] += _dot(ds, qh, TN)
            dkv_ref[:, vs] += _dot(p.astype(BF16), doh, TN)

    row = pl.BlockSpec((tm, D_MODEL), lambda i: (i, 0))
    full = pl.BlockSpec((M, 2 * D_MODEL), lambda i: (0, 0))
    return pl.pallas_call(
        body, name="attn_bwd", grid=(S // tm,), in_specs=[row, full, row], out_specs=[row, full],
        out_shape=[jax.ShapeDtypeStruct((S, D_MODEL), BF16), jax.ShapeDtypeStruct((M, 2 * D_MODEL), F32)],
        compiler_params=_cp(("arbitrary",)),
    )(q, kv, do)


def sum_parts(parts, own=None, tm=256):
    n, R, C = parts.shape
    tm = _tile(R, tm)
    has_own = own is not None

    def body(*refs):
        p_ref, o_ref = refs[0], refs[-1]
        acc = refs[1][...].astype(F32) if has_own else p_ref[0].astype(F32)
        for j in range(0 if has_own else 1, n):
            acc = acc + p_ref[j].astype(F32)
        o_ref[...] = acc

    row = pl.BlockSpec((tm, C), lambda i: (i, 0))
    return pl.pallas_call(
        body, name="sum_parts", grid=(R // tm,),
        in_specs=[pl.BlockSpec((n, tm, C), lambda i: (0, i, 0))] + ([row] if has_own else []), out_specs=row,
        out_shape=jax.ShapeDtypeStruct((R, C), F32), compiler_params=_cp(("parallel",)),
    )(*([parts, own] if has_own else [parts]))


def adamw(w, g, m, v, tm=256):
    R, C = w.shape
    tm = _tile(R, tm)
    c1 = 1.0 / (1.0 - ADAM_B1 ** ADAM_STEP)
    c2 = 1.0 / (1.0 - ADAM_B2 ** ADAM_STEP)

    def body(w_ref, g_ref, m_ref, v_ref, d_ref, nm_ref, nv_ref):
        gv = g_ref[...]
        nm = ADAM_B1 * m_ref[...] + (1.0 - ADAM_B1) * gv
        nv = ADAM_B2 * v_ref[...] + (1.0 - ADAM_B2) * (gv * gv)
        nm_ref[...] = nm
        nv_ref[...] = nv
        d_ref[...] = -ADAM_LR * ((nm * c1) / (jnp.sqrt(nv * c2) + ADAM_EPS) + ADAM_WD * w_ref[...])

    blk = pl.BlockSpec((tm, C), lambda i: (i, 0))
    sd = jax.ShapeDtypeStruct((R, C), F32)
    return pl.pallas_call(
        body, name="adamw", grid=(R // tm,), in_specs=[blk] * 4, out_specs=[blk] * 3, out_shape=[sd] * 3,
        compiler_params=_cp(("parallel",)),
    )(w, g, m, v)


ANY = pl.BlockSpec(memory_space=pl.ANY)


def _place():
    return lax.axis_index("x"), lax.axis_index("y"), lax.axis_index("c")


def _slot(px, py, pc):
    return 4 * px + 2 * py + pc


def all_gather(name, shards, after=()):
    n = len(shards)
    n_in = n + len(after)

    def body(*refs):
        x_refs, out_refs = refs[:n], refs[n_in:n_in + n]
        send_sems, recv_sems, local_sems = refs[n_in + n:]
        x, y, c = _place()
        me, sibling = (x, y, c), (x, y, 1 - c)
        chips = [(1 - x, y), (x, 1 - y), (1 - x, 1 - y)]

        def copy(a, k, block, to, src=None):
            rows = out_refs[a].at[_slot(*block)]
            return pltpu.make_async_remote_copy(
                src_ref=rows if src is None else src, dst_ref=rows,
                send_sem=send_sems.at[7 * a + k], recv_sem=recv_sems.at[7 * a + k],
                device_id=to, device_id_type=MESH)

        mine = [pltpu.make_async_copy(x_refs[a], out_refs[a].at[_slot(*me)], local_sems.at[a]) for a in range(n)]
        for cp in mine:
            cp.start()
        first = []
        for a in range(n):
            first.append(copy(a, 0, me, sibling, src=x_refs[a]))
            first += [copy(a, 1 + j, me, (*chip, c), src=x_refs[a]) for j, chip in enumerate(chips)]
        for cp in first:
            cp.start()
        passed = []
        for a in range(n):
            for j, chip in enumerate(chips):
                copy(a, 1 + j, (*chip, c), me).wait_recv()
                cp = copy(a, 4 + j, (*chip, c), sibling)
                cp.start()
                passed.append(cp)
        for a in range(n):
            copy(a, 0, sibling, me).wait_recv()
            for j, chip in enumerate(chips):
                copy(a, 4 + j, (*chip, 1 - c), me).wait_recv()
        for cp in first + passed:
            cp.wait_send()
        for cp in mine:
            cp.wait()

    return pl.pallas_call(
        body, name=name, in_specs=[ANY] * n_in, out_specs=[ANY] * n,
        out_shape=[jax.ShapeDtypeStruct((N_DEV, *s.shape), s.dtype) for s in shards],
        scratch_shapes=[pltpu.SemaphoreType.DMA((7 * n,)), pltpu.SemaphoreType.DMA((7 * n,)), pltpu.SemaphoreType.DMA((n,))],
    )(*shards, *after)


HBM = pl.BlockSpec(memory_space=pltpu.HBM)
SEM = pl.BlockSpec(memory_space=pltpu.SEMAPHORE)
EFFECT = pltpu.SideEffectType.DATAFLOW_SIDE_EFFECTING
N_PEER = N_DEV - 1
RELATIONS = [(dx, dy, dc) for dx in (0, 1) for dy in (0, 1) for dc in (0, 1)][1:]


def _peer(place, rel):
    return tuple(1 - v if d else v for v, d in zip(place, rel))


def gather_start(name, shards, me, before):
    n = len(shards)

    def body(*refs):
        x_refs, land_refs = refs[:n], refs[n:2 * n]
        send_sems, recv_sems = refs[2 * n + len(before):2 * n + len(before) + 2]
        token = refs[-1]
        place = _place()
        mine = _slot(*place)
        for a in range(n):
            for rel in RELATIONS:
                pltpu.make_async_remote_copy(
                    src_ref=x_refs[a], dst_ref=land_refs[a].at[mine], send_sem=send_sems.at[a], recv_sem=recv_sems.at[a],
                    device_id=_peer(place, rel), device_id_type=MESH).start()
        token[...] = jnp.zeros_like(token)

    lands = [lax.dynamic_update_index_in_dim(lax.empty((N_DEV, *s.shape), s.dtype), s, me, 0) for s in shards]
    outs = pl.pallas_call(
        body, name=name,
        in_specs=[HBM] * (2 * n) + [ANY] * len(before),
        out_specs=[SEM, SEM] + [HBM] * (2 * n) + [pl.BlockSpec(memory_space=pltpu.VMEM)],
        out_shape=[pltpu.SemaphoreType.DMA((n,)), pltpu.SemaphoreType.DMA((n,))]
        + [pltpu.HBM(t.shape, t.dtype) for t in (*shards, *lands)] + [jax.ShapeDtypeStruct((8, LANE), F32)],
        input_output_aliases={i: 2 + i for i in range(2 * n)},
        compiler_params=pltpu.CompilerParams(has_side_effects=EFFECT),
    )(*[pltpu.with_memory_space_constraint(t, pltpu.HBM) for t in (*shards, *lands)], *before)
    return (outs[0], outs[1], outs[2:2 + n], outs[2 + n:2 + 2 * n]), outs[-1]


def gather_wait(name, state, after):
    send_sems, recv_sems, shards, lands = state
    n = len(shards)

    def body(*refs):
        land_refs = refs[n:2 * n]
        s_sems, r_sems = refs[2 * n:2 * n + 2]
        place = _place()
        for a in range(n):
            seven = land_refs[a].at[pl.ds(0, N_PEER)]
            cp = pltpu.make_async_remote_copy(
                src_ref=seven, dst_ref=seven, send_sem=s_sems.at[a], recv_sem=r_sems.at[a], device_id=place, device_id_type=MESH)
            cp.wait_send()
            cp.wait_recv()

    outs = pl.pallas_call(
        body, name=name,
        in_specs=[HBM] * (2 * n) + [SEM, SEM] + [ANY] * len(after), out_specs=[HBM] * (2 * n),
        out_shape=[pltpu.HBM(t.shape, t.dtype) for t in (*shards, *lands)],
        input_output_aliases={i: i for i in range(2 * n)},
        compiler_params=pltpu.CompilerParams(has_side_effects=EFFECT),
    )(*shards, *lands, send_sems, recv_sems, *after)
    return outs[n:]


def exchange_start(name, grads, before):
    n = len(grads)

    def body(*refs):
        g_refs, land_refs = refs[:n], refs[n:2 * n]
        send_sems, recv_sems = refs[2 * n + 1:2 * n + 3]
        token = refs[-1]
        place = _place()
        for a in range(n):
            for r, rel in enumerate(RELATIONS):
                p = _peer(place, rel)
                pltpu.make_async_remote_copy(
                    src_ref=g_refs[a].at[_slot(*p)], dst_ref=land_refs[a].at[r],
                    send_sem=send_sems.at[a], recv_sem=recv_sems.at[a], device_id=p, device_id_type=MESH).start()
        token[...] = jnp.zeros_like(token)

    lands = [lax.empty((N_PEER, *g.shape[1:]), g.dtype) for g in grads]
    outs = pl.pallas_call(
        body, name=name,
        in_specs=[HBM] * (2 * n) + [ANY],
        out_specs=[SEM, SEM] + [HBM] * (2 * n) + [pl.BlockSpec(memory_space=pltpu.VMEM)],
        out_shape=[pltpu.SemaphoreType.DMA((n,)), pltpu.SemaphoreType.DMA((n,))]
        + [pltpu.HBM(g.shape, g.dtype) for g in grads] + [pltpu.HBM(t.shape, t.dtype) for t in lands]
        + [jax.ShapeDtypeStruct((8, LANE), F32)],
        input_output_aliases={i: 2 + i for i in range(2 * n)},
        compiler_params=pltpu.CompilerParams(has_side_effects=EFFECT),
    )(*[pltpu.with_memory_space_constraint(t, pltpu.HBM) for t in (*grads, *lands)], before)
    return (outs[0], outs[1], outs[2:2 + n], outs[2 + n:2 + 2 * n]), outs[-1]


def exchange_wait(name, state, after):
    send_sems, recv_sems, grads, lands = state
    n = len(grads)

    def body(*refs):
        g_refs, land_refs = refs[:n], refs[n:2 * n]
        s_sems, r_sems = refs[2 * n:2 * n + 2]
        place = _place()
        for a in range(n):
            cp = pltpu.make_async_remote_copy(
                src_ref=g_refs[a].at[pl.ds(0, N_PEER)], dst_ref=land_refs[a],
                send_sem=s_sems.at[a], recv_sem=r_sems.at[a], device_id=place, device_id_type=MESH)
            cp.wait_send()
            cp.wait_recv()

    outs = pl.pallas_call(
        body, name=name,
        in_specs=[HBM] * (2 * n) + [SEM, SEM, ANY], out_specs=[HBM] * (2 * n),
        out_shape=[pltpu.HBM(t.shape, t.dtype) for t in (*grads, *lands)],
        input_output_aliases={i: i for i in range(2 * n)},
        compiler_params=pltpu.CompilerParams(has_side_effects=EFFECT),
    )(*grads, *lands, send_sems, recv_sems, after)
    return outs[:n], outs[n:]


WEIGHTS = ['norm_mix_w', 'w_in', 'hg_lb_raw', 'hg_norm_w', 'cv_dw_w', 'cv_dw_b', 'cv_ln_w', 'cv_ln_b', 'pl_w', 'pl_scale',
           'lru_conv_w', 'lru_conv_b', 'lru_wa', 'lru_ba', 'lru_wx', 'lru_bx', 'lru_lambda', 'gate_b', 'w_branch', 'w_out',
           'norm_mem_w', 'mem_norm_w', 'xa_wq', 'xa_wkv', 'xa_wo', 'norm_ffn_w', 'ffn_w1', 'ffn_w2', 'final_norm_w']
BIG = ('w_in', 'w_branch', 'w_out', 'xa_wq', 'xa_wkv', 'xa_wo', 'ffn_w1', 'ffn_w2')
SMALL_SHARDED = ('cv_dw_w', 'lru_conv_w', 'gate_b')
SMALL = tuple(n for n in WEIGHTS if n not in BIG and n not in SMALL_SHARDED)
PACK_ROWS = 256


def _pack(arrs):
    flat = jnp.concatenate([a.reshape(-1).astype(F32) for a in arrs])
    tile = PACK_ROWS * LANE
    padded = -(-flat.shape[0] // tile) * tile
    return jnp.pad(flat, (0, padded - flat.shape[0])).reshape(-1, LANE)


def _unpack(packed, shapes):
    flat = packed.reshape(-1)
    out, off = [], 0
    for s in shapes:
        n = math.prod(s)
        out.append(flat[off:off + n].reshape(s))
        off += n
    return out


def _gather_last(g, shard_shape):
    nd = len(shard_shape)
    full = jnp.moveaxis(g, 0, nd - 1)
    return full.reshape(*shard_shape[:-1], N_DEV * shard_shape[-1])


def _natural(blocks):
    nb, k, c = blocks.shape
    return jnp.transpose(blocks, (1, 0, 2)).reshape(k, nb * c)


def _block_diag(w):
    w2 = w.reshape(4, 2, 64, 64)
    z = jnp.zeros((4, 64, 64), w.dtype)
    return jnp.concatenate([jnp.concatenate([w2[:, 0], z], axis=2), jnp.concatenate([z, w2[:, 1]], axis=2)], axis=1)


def _block_diag_t(d):
    return jnp.stack([d[:, :64, :64], d[:, 64:, 64:]], axis=1).reshape(8, 64, 64)


def _lower_bounds(raw):
    lb = jnp.cumsum(jax.nn.softmax(raw.astype(F32), axis=0), axis=0)
    return lb - lb[0:1]


def _decay_rates(lam):
    return (LRU_C * jax.nn.softplus(-lam.astype(F32))).reshape(DEPTH, BRANCH_W)


def _relu2(acc):
    r = jnp.maximum(acc, 0.0)
    return acc, r * r


def _relu2_grad(acc, u):
    return (acc * 2.0 * jnp.maximum(u, 0.0),)


def _add(acc, e):
    return (acc + e,)


def _layer_fwd(x0, mem, p, g, rest):
    h1 = rms_fwd("rms_mix", x0, p['norm_mix_w'])
    proj = mm_nt("mm_in", h1, g['w_in'], tn=2176)[0]
    bcat, states, o_hg = hgrn_fwd(proj, p['lb'], p['hg_norm_w'])
    zc = cv_fwd(proj, p['cv_w32'], p['cv_dw_b'])
    bcat = ln_silu_fwd(zc, p['cv_ln_w'], p['cv_ln_b'], bcat)
    bcat = pool_fwd(proj, p['pl_w'], p['pl_scale'], bcat)
    bcat, hst = lru_fwd(proj, p['lru_cw8'], p['lru_conv_b'], p['wa_bd'], p['lru_ba'], p['wx_bd'], p['lru_bx'], p['sp8'], bcat)
    more, after = rest(bcat)
    g = {**g, **more}
    ups = mm_branch_nn("mm_up", bcat, g['w_branch'], tm=2048, after=after)
    merged = merge_fwd(ups, proj, p['gate_b'])
    x1 = mm_nn("mm_out", merged, g['w_out'], epi=_add, extras=(x0,))[0]
    h2 = rms_fwd("rms_mem", x1, p['norm_mem_w'])
    q = mm_nn("mm_q", h2, g['xa_wq'], out_dtype=BF16)[0]
    memn = rms_fwd("rms_memtok", mem, p['mem_norm_w'])
    kv = mm_nn("mm_kv", memn, g['xa_wkv'], out_dtype=BF16, tn=2048)[0]
    oa = attn_fwd(q, kv)
    x2 = mm_nn("mm_o", oa, g['xa_wo'], epi=_add, extras=(x1,))[0]
    h3 = rms_fwd("rms_ffn", x2, p['norm_ffn_w'])
    u, act = mm_nn("mm_ffn1", h3, g['ffn_w1'], epi=_relu2, out_dtypes=[BF16, BF16])
    x3 = mm_nn("mm_ffn2", act, g['ffn_w2'], epi=_add, extras=(x2,))[0]
    res = dict(x0=x0, h1=h1, proj=proj, states=states, o_hg=o_hg, zc=zc, hst=hst, bcat=bcat, ups=ups, merged=merged,
               x1=x1, h2=h2, q=q, memn=memn, kv=kv, oa=oa, x2=x2, h3=h3, u=u, act=act)
    return x3, res, g


def _layer_bwd(dx3, mem, p, g, r, midway, finish):
    gs, gb = {}, {}
    du = mm_nt("mm_dffn2", dx3, g['ffn_w2'], out_dtype=BF16, epi=_relu2_grad, extras=(r['u'],))[0]
    gb['ffn_w2'] = mm_tn("mm_gw2", r['act'], dx3).reshape(N_DEV, -1, D_MODEL)
    gb['ffn_w1'] = mm_tn_cb("mm_gw1", r['h3'], du, N_DEV)
    dh3 = mm_nt("mm_dffn1", du, g['ffn_w1'], out_dtype=BF16)[0]
    dx2, gs['norm_ffn_w'] = rms_bwd("rmsb_ffn", r['x2'], p['norm_ffn_w'], dh3, dx3)
    doa = mm_nt("mm_do", dx2, g['xa_wo'], out_dtype=BF16)[0]
    gb['xa_wo'] = mm_tn("mm_gwo", r['oa'], dx2).reshape(N_DEV, -1, D_MODEL)
    dq, dkv = attn_bwd(r['q'], r['kv'], doa)
    gb['xa_wq'] = mm_tn("mm_gwq", r['h2'], dq).reshape(N_DEV, -1, D_MODEL)
    dh2 = mm_nt("mm_dq", dq, g['xa_wq'], out_dtype=BF16)[0]
    gb['xa_wkv'] = mm_tn_cb("mm_gwkv", r['memn'], dkv, N_DEV)
    dmemn = mm_nt("mm_dkv", dkv, g['xa_wkv'], out_dtype=BF16)[0]
    _, gs['mem_norm_w'] = rms_bwd("rmsb_memtok", mem, p['mem_norm_w'], dmemn)
    dx1, gs['norm_mem_w'] = rms_bwd("rmsb_mem", r['x1'], p['norm_mem_w'], dh2, dx2)
    after = midway(gb, dx1)
    gb = {}
    dmerged = mm_nt("mm_dout", dx1, g['w_out'], out_dtype=BF16, after=after)[0]
    gb['w_out'] = mm_tn("mm_gwout", r['merged'], dx1).reshape(N_DEV, -1, D_MODEL)
    dups, dproj, gs['gate_b'] = merge_bwd(dmerged, r['ups'], r['proj'], p['gate_b'])
    gwb = mm_branch_tn("mm_gwb", r['bcat'], dups, N_BRANCH)
    gb['w_branch'] = jnp.transpose(gwb.reshape(N_BRANCH, BRANCH_W, N_DEV, -1), (2, 0, 1, 3))
    dbcat = mm_branch_nt("mm_dup", dups, g['w_branch'], tm=2048)
    dproj, gs['lb'], gs['hg_norm_w'] = hgrn_bwd(r['proj'], p['lb'], p['hg_norm_w'], r['states'], r['o_hg'], dbcat, dproj)
    dzc, gs['cv_ln_w'], gs['cv_ln_b'] = ln_silu_bwd(r['zc'], p['cv_ln_w'], p['cv_ln_b'], dbcat)
    dca, dcg, dcw, gs['cv_dw_b'] = cv_bwd(r['proj'], p['cv_w32'], dzc)
    gs['cv_dw_w'] = dcw[:CV_KERNEL]
    dproj, gs['pl_w'], gs['pl_scale'] = pool_bwd(r['proj'], p['pl_w'], p['pl_scale'], dbcat, dproj)
    dlx, dly, dlcw, gs['lru_conv_b'], dwa, gs['lru_ba'], dwx, gs['lru_bx'], gs['sp8'] = lru_bwd(
        r['proj'], p['lru_cw8'], p['lru_conv_b'], p['wa_bd'], p['lru_ba'], p['wx_bd'], p['lru_bx'], p['sp8'], r['hst'], dbcat)
    gs['lru_conv_w'] = dlcw[:LRU_CONV]
    gs['lru_wa'], gs['lru_wx'] = _block_diag_t(dwa), _block_diag_t(dwx)
    gs['lru_ba'], gs['lru_bx'] = gs['lru_ba'].reshape(8, 64), gs['lru_bx'].reshape(8, 64)
    for off, piece in ((OFF_CV, dca), (OFF_CV + BRANCH_W, dcg), (OFF_LX, dlx), (OFF_LY, dly)):
        dproj = lax.dynamic_update_slice(dproj, piece, (0, off))
    gb['w_in'] = mm_tn("mm_gwin", dproj, r['h1'], tm=2176).reshape(N_DEV, -1, D_MODEL)
    dh1 = mm_nn("mm_din", dproj, g['w_in'], out_dtype=BF16, tk=2176, after=finish(gb, dx1))[0]
    dx0, gs['norm_mix_w'] = rms_bwd("rmsb_mix", r['x0'], p['norm_mix_w'], dh1, dx1)
    return dx0, gs


def kernel(x, mem, norm_mix_w, w_in, hg_lb_raw, hg_norm_w, cv_dw_w, cv_dw_b, cv_ln_w, cv_ln_b, pl_w, pl_scale, lru_conv_w, lru_conv_b, lru_wa, lru_ba, lru_wx, lru_bx, lru_lambda, gate_b, w_branch, w_out, norm_mem_w, mem_norm_w, xa_wq, xa_wkv, xa_wo, norm_ffn_w, ffn_w1, ffn_w2, final_norm_w, loss_target, m_norm_mix_w, m_w_in, m_hg_lb_raw, m_hg_norm_w, m_cv_dw_w, m_cv_dw_b, m_cv_ln_w, m_cv_ln_b, m_pl_w, m_pl_scale, m_lru_conv_w, m_lru_conv_b, m_lru_wa, m_lru_ba, m_lru_wx, m_lru_bx, m_lru_lambda, m_gate_b, m_w_branch, m_w_out, m_norm_mem_w, m_mem_norm_w, m_xa_wq, m_xa_wkv, m_xa_wo, m_norm_ffn_w, m_ffn_w1, m_ffn_w2, m_final_norm_w, v_norm_mix_w, v_w_in, v_hg_lb_raw, v_hg_norm_w, v_cv_dw_w, v_cv_dw_b, v_cv_ln_w, v_cv_ln_b, v_pl_w, v_pl_scale, v_lru_conv_w, v_lru_conv_b, v_lru_wa, v_lru_ba, v_lru_wx, v_lru_bx, v_lru_lambda, v_gate_b, v_w_branch, v_w_out, v_norm_mem_w, v_mem_norm_w, v_xa_wq, v_xa_wkv, v_xa_wo, v_norm_ffn_w, v_ffn_w1, v_ffn_w2, v_final_norm_w):
    W = dict(zip(WEIGHTS, (norm_mix_w, w_in, hg_lb_raw, hg_norm_w, cv_dw_w, cv_dw_b, cv_ln_w, cv_ln_b, pl_w, pl_scale, lru_conv_w, lru_conv_b, lru_wa, lru_ba, lru_wx, lru_bx, lru_lambda, gate_b, w_branch, w_out, norm_mem_w, mem_norm_w, xa_wq, xa_wkv, xa_wo, norm_ffn_w, ffn_w1, ffn_w2, final_norm_w)))
    Mo = dict(zip(WEIGHTS, (m_norm_mix_w, m_w_in, m_hg_lb_raw, m_hg_norm_w, m_cv_dw_w, m_cv_dw_b, m_cv_ln_w, m_cv_ln_b, m_pl_w, m_pl_scale, m_lru_conv_w, m_lru_conv_b, m_lru_wa, m_lru_ba, m_lru_wx, m_lru_bx, m_lru_lambda, m_gate_b, m_w_branch, m_w_out, m_norm_mem_w, m_mem_norm_w, m_xa_wq, m_xa_wkv, m_xa_wo, m_norm_ffn_w, m_ffn_w1, m_ffn_w2, m_final_norm_w)))
    Vo = dict(zip(WEIGHTS, (v_norm_mix_w, v_w_in, v_hg_lb_raw, v_hg_norm_w, v_cv_dw_w, v_cv_dw_b, v_cv_ln_w, v_cv_ln_b, v_pl_w, v_pl_scale, v_lru_conv_w, v_lru_conv_b, v_lru_wa, v_lru_ba, v_lru_wx, v_lru_bx, v_lru_lambda, v_gate_b, v_w_branch, v_w_out, v_norm_mem_w, v_mem_norm_w, v_xa_wq, v_xa_wkv, v_xa_wo, v_norm_ffn_w, v_ffn_w1, v_ffn_w2, v_final_norm_w)))
    me = _slot(*_place())
    xs, mems, target = x[0], mem[0], loss_target[0]

    shard_shapes = [W[n].shape for n in SMALL_SHARDED]
    gathered = all_gather("ag_small", [_pack([W[n] for n in SMALL_SHARDED])])[0]
    parts = [jnp.stack(ps) for ps in zip(*[_unpack(gathered[d], shard_shapes) for d in range(N_DEV)])]
    full_small = {n: _gather_last(parts[i], shard_shapes[i]) for i, n in enumerate(SMALL_SHARDED)}
    lb_all, lb_vjp = jax.vjp(_lower_bounds, hg_lb_raw)
    sp8_all, sp8_vjp = jax.vjp(_decay_rates, lru_lambda)

    def layer_params(l):
        p = {n: W[n][l] for n in SMALL if n != 'final_norm_w'}
        p['lb'] = lb_all[l]
        p['sp8'] = sp8_all[l]
        p['cv_w32'] = jnp.pad(full_small['cv_dw_w'][l], ((0, 32 - CV_KERNEL), (0, 0)))
        p['lru_cw8'] = jnp.pad(full_small['lru_conv_w'][l], ((0, 8 - LRU_CONV), (0, 0)))
        p['gate_b'] = full_small['gate_b'][l]
        p['wa_bd'], p['wx_bd'] = _block_diag(lru_wa[l]), _block_diag(lru_wx[l])
        p['lru_ba'], p['lru_bx'] = lru_ba[l].reshape(-1), lru_bx[l].reshape(-1)
        return p

    def shards_of(l):
        first = [jnp.transpose(w_in[l]).astype(BF16)]
        others = [w[l].astype(BF16) for w in (w_branch, w_out, xa_wq, xa_wkv, xa_wo, ffn_w1, ffn_w2)]
        return first, others

    def start_gather(l, before):
        first, others = shards_of(l)
        state_a, tok_a = gather_start(f"ag_start{l}a", first, me, before)
        state_b, tok_b = gather_start(f"ag_start{l}b", others, me, (*before, tok_a))
        return state_a, state_b, (tok_a, tok_b)

    def first_of(o):
        return dict(w_in=o[0].reshape(IN_W, D_MODEL))

    def others_of(o):
        wb = jnp.transpose(o[0], (1, 2, 0, 3)).reshape(N_BRANCH, BRANCH_W, D_MODEL)
        return dict(w_branch=wb, w_out=o[1].reshape(D_MODEL, D_MODEL),
                    xa_wq=o[2].reshape(D_MODEL, D_MODEL), xa_wkv=_natural(o[3]), xa_wo=o[4].reshape(D_MODEL, D_MODEL),
                    ffn_w1=_natural(o[5]), ffn_w2=o[6].reshape(D_FF, D_MODEL))

    params = [layer_params(l) for l in range(DEPTH)]
    mats, residuals = [], []
    xc = xs
    first, others = shards_of(0)
    whole = all_gather("ag_layer0", first + others)
    gathers = {}
    for l in range(DEPTH):
        if l == 0:
            g_first, g_others = first_of(whole[:1]), others_of(whole[1:])
        else:
            state_a, state_b, _ = gathers.pop(l)
            g_first = first_of(gather_wait(f"ag_wait{l}a", state_a, (xc,)))

        def rest(proj, l=l):
            more = g_others if l == 0 else others_of(gather_wait(f"ag_wait{l}b", state_b, (proj,)))
            if l + 1 == DEPTH:
                return more, ()
            gathers[l + 1] = start_gather(l + 1, (more['w_out'],))
            return more, gathers[l + 1][2]

        xc, res, g = _layer_fwd(xc, mems, params[l], g_first, rest)
        mats.append(g)
        residuals.append(res)
    loss_part, dx, g_final = loss_head(xc, final_norm_w, target)
    loss = lax.psum(loss_part, ("x", "y", "c"))

    small_grads = [None] * DEPTH
    big_grads = [{} for _ in range(DEPTH)]
    pending = []

    def send(l, group, blocks, before):
        names = list(blocks)
        state, tok = exchange_start(f"rs_start{l}{group}", [blocks[n] for n in names], before)
        pending.append((l, group, names, state))
        return (tok,)

    def land(after):
        l, group, names, state = pending.pop(0)
        sent, landed = exchange_wait(f"rs_wait{l}{group}", state, after)
        for n, s, t in zip(names, sent, landed):
            own = lax.dynamic_index_in_dim(s, me, 0, keepdims=False).reshape(-1, s.shape[-1])
            big_grads[l][n] = sum_parts(t.reshape(N_PEER, -1, t.shape[-1]), own).reshape(t.shape[1:])

    for l in reversed(range(DEPTH)):
        dx, small_grads[l] = _layer_bwd(dx, mems, params[l], mats[l], residuals[l],
                                        lambda blocks, dx1, l=l: send(l, "a", blocks, dx1),
                                        lambda blocks, dx1, l=l: send(l, "b", blocks, dx1))
        while pending[0][0] > l:
            land(dx)

    def stacked(n):
        return jnp.stack([small_grads[l][n] for l in range(DEPTH)])

    part = {n: stacked(n) for n in SMALL if n not in ('final_norm_w', 'hg_lb_raw', 'lru_lambda')}
    part['final_norm_w'] = g_final
    part['hg_lb_raw'] = lb_vjp(stacked('lb'))[0]
    part['lru_lambda'] = sp8_vjp(stacked('sp8'))[0]
    for n in SMALL_SHARDED:
        part[n] = stacked(n)
    names = list(SMALL) + list(SMALL_SHARDED)
    full_shapes = [part[n].shape for n in names]
    packed = _pack([part[n] for n in names])
    state, _ = exchange_start("rs_small_start", [packed.reshape(N_DEV, -1, LANE)], packed)
    sent, landed = exchange_wait("rs_small_wait", state, packed)
    mine = sum_parts(landed[0], lax.dynamic_index_in_dim(sent[0], me, 0, keepdims=False))
    total = all_gather("ag_grads", [mine])[0].reshape(-1, LANE)
    while pending:
        land(total)

    G = {}
    G['w_in'] = jnp.stack([jnp.transpose(big_grads[l]['w_in']) for l in range(DEPTH)])
    for n in ('w_branch', 'w_out', 'xa_wq', 'xa_wkv', 'xa_wo', 'ffn_w1', 'ffn_w2'):
        G[n] = jnp.stack([big_grads[l][n] for l in range(DEPTH)])
    for n, t in zip(names, _unpack(total, full_shapes)):
        if n in SMALL_SHARDED:
            c = t.shape[-1] // N_DEV
            t = lax.dynamic_slice_in_dim(t, me * c, c, axis=t.ndim - 1)
        G[n] = t

    delta, new_m, new_v = {}, {}, {}
    for n in BIG:
        c = W[n].shape[-1]
        d, nm, nv = adamw(W[n].reshape(-1, c), G[n].reshape(-1, c), Mo[n].reshape(-1, c), Vo[n].reshape(-1, c))
        delta[n], new_m[n], new_v[n] = d.reshape(W[n].shape), nm.reshape(W[n].shape), nv.reshape(W[n].shape)
    shapes = [W[n].shape for n in names]
    d, nm, nv = adamw(_pack([W[n] for n in names]), _pack([G[n] for n in names]), _pack([Mo[n] for n in names]), _pack([Vo[n] for n in names]))
    for n, a, b, c in zip(names, _unpack(d, shapes), _unpack(nm, shapes), _unpack(nv, shapes)):
        delta[n], new_m[n], new_v[n] = a, b, c
    return (loss, dx[None], *[G[n] for n in WEIGHTS], *[delta[n] for n in WEIGHTS],
            *[new_m[n] for n in WEIGHTS], *[new_v[n] for n in WEIGHTS])
```

```python
import functools
import math

import jax
import jax.numpy as jnp
from jax import lax
from jax.experimental import pallas as pl
from jax.experimental.pallas import tpu as pltpu

F32 = jnp.float32
BF16 = jnp.bfloat16
I32 = jnp.int32

N_DEV = 8
D_MODEL = 1024
DEPTH = 4
CHUNK = 64
EPS = 1e-6
HG_HEADS = 4
BRANCH_W = 512
CV_KERNEL = 31
POOL_WINDOWS = (2, 4, 8, 16)
LRU_CONV = 4
LRU_C = 8.0
XA_HEADS = 4
XA_HD = D_MODEL // XA_HEADS
D_FF = 4 * D_MODEL
IN_W = 8704
OFF_Q, OFF_F, OFF_V, OFF_G, OFF_CV, OFF_PL, OFF_LX, OFF_LY, OFF_GATE = 0, 512, 1024, 1536, 2048, 3072, 3584, 4096, 4608
LANE = 128
ADAM_LR, ADAM_B1, ADAM_B2, ADAM_EPS, ADAM_WD, ADAM_STEP = 0.001, 0.9, 0.999, 1e-08, 0.01, 10
VMEM_LIMIT = 56 * 1024 * 1024
MESH = pl.DeviceIdType.MESH
NEG = -1e30
ANY_SPACE = pl.BlockSpec(memory_space=pl.ANY)


def _cp(sem, **kw):
    return pltpu.CompilerParams(dimension_semantics=sem, vmem_limit_bytes=VMEM_LIMIT, **kw)


def _sigmoid(x):
    return 1.0 / (1.0 + jnp.exp(-x))


def _dsilu(x, s):
    return s * (1.0 + x * (1.0 - s))


def _dot(a, b, cdims, precision=None):
    return lax.dot_general(a, b, (cdims, ((), ())), preferred_element_type=F32, precision=precision)


NN = ((1,), (0,))
NT = ((1,), (1,))
TN = ((0,), (0,))


def _mm(name, a, b, *, grid, a_spec, b_spec, o_specs, out_shapes, acc_shape, cdims, epi=None, extras=(), extra_specs=(), after=()):
    nk = grid[2]
    n_e, n_o = len(extras), len(out_shapes)
    extras = (*extras, *after)
    extra_specs = (*extra_specs, *[ANY_SPACE] * len(after))

    def body(*refs):
        a_ref, b_ref = refs[0], refs[1]
        e_refs = refs[2:2 + n_e]
        o_refs = refs[2 + len(extras):2 + len(extras) + n_o]

        def finish(acc):
            vals = epi(acc, *[r[...] for r in e_refs]) if epi is not None else (acc,)
            for r, v in zip(o_refs, vals):
                r[...] = v.astype(r.dtype)

        part = _dot(a_ref[...].astype(BF16), b_ref[...].astype(BF16), cdims)
        if nk == 1:
            finish(part)
        else:
            acc_ref = refs[-1]
            k = pl.program_id(2)

            @pl.when(k == 0)
            def _():
                acc_ref[...] = part

            @pl.when(k > 0)
            def _():
                acc_ref[...] += part

            @pl.when(k == nk - 1)
            def _():
                finish(acc_ref[...])

    return pl.pallas_call(
        body, name=name, grid=grid,
        in_specs=[a_spec, b_spec, *extra_specs], out_specs=list(o_specs), out_shape=list(out_shapes),
        scratch_shapes=[] if nk == 1 else [pltpu.VMEM(acc_shape, F32)],
        compiler_params=_cp(("parallel", "parallel", "arbitrary")),
    )(a, b, *extras)


def _tile(n, pref):
    t = min(n, pref)
    while n % t:
        t //= 2
    return t


def mm_nt(name, a, b, out_dtype=F32, epi=None, extras=(), n_out=1, out_dtypes=None, tm=1024, tn=1024, tk=2048, after=()):
    M, K = a.shape
    N = b.shape[0]
    tm, tn, tk = _tile(M, tm), _tile(N, tn), _tile(K, tk)
    odt = out_dtypes or [out_dtype] * n_out
    o_spec = pl.BlockSpec((tm, tn), lambda i, j, k: (i, j))
    return _mm(name, a, b, grid=(M // tm, N // tn, K // tk),
               a_spec=pl.BlockSpec((tm, tk), lambda i, j, k: (i, k)),
               b_spec=pl.BlockSpec((tn, tk), lambda i, j, k: (j, k)),
               o_specs=[o_spec] * len(odt), out_shapes=[jax.ShapeDtypeStruct((M, N), d) for d in odt],
               acc_shape=(tm, tn), cdims=NT, epi=epi, extras=extras, extra_specs=[o_spec] * len(extras), after=after)


def mm_nn(name, a, b, out_dtype=F32, epi=None, extras=(), n_out=1, out_dtypes=None, tm=1024, tn=1024, tk=2048, after=()):
    M, K = a.shape
    N = b.shape[1]
    tm, tn, tk = _tile(M, tm), _tile(N, tn), _tile(K, tk)
    odt = out_dtypes or [out_dtype] * n_out
    o_spec = pl.BlockSpec((tm, tn), lambda i, j, k: (i, j))
    return _mm(name, a, b, grid=(M // tm, N // tn, K // tk),
               a_spec=pl.BlockSpec((tm, tk), lambda i, j, k: (i, k)),
               b_spec=pl.BlockSpec((tk, tn), lambda i, j, k: (k, j)),
               o_specs=[o_spec] * len(odt), out_shapes=[jax.ShapeDtypeStruct((M, N), d) for d in odt],
               acc_shape=(tm, tn), cdims=NN, epi=epi, extras=extras, extra_specs=[o_spec] * len(extras), after=after)


def mm_tn(name, a, b, out_dtype=BF16, tm=1024, tn=1024, tk=1024):
    K, M = a.shape
    N = b.shape[1]
    tm, tn, tk = _tile(M, tm), _tile(N, tn), _tile(K, tk)
    return _mm(name, a, b, grid=(M // tm, N // tn, K // tk),
               a_spec=pl.BlockSpec((tk, tm), lambda i, j, k: (k, i)),
               b_spec=pl.BlockSpec((tk, tn), lambda i, j, k: (k, j)),
               o_specs=[pl.BlockSpec((tm, tn), lambda i, j, k: (i, j))],
               out_shapes=[jax.ShapeDtypeStruct((M, N), out_dtype)], acc_shape=(tm, tn), cdims=TN)[0]


def mm_branch_nn(name, a, b, out_dtype=BF16, tm=1024, after=()):
    M = a.shape[0]
    G, K, N = b.shape
    tm = _tile(M, tm)
    return _mm(name, a, b, grid=(M // tm, G, 1),
               a_spec=pl.BlockSpec((tm, K), lambda i, g, k: (i, g)),
               b_spec=pl.BlockSpec((None, K, N), lambda i, g, k: (g, 0, 0)),
               o_specs=[pl.BlockSpec((tm, N), lambda i, g, k: (i, g))],
               out_shapes=[jax.ShapeDtypeStruct((M, G * N), out_dtype)], acc_shape=(tm, N), cdims=NN, after=after)[0]


def mm_branch_nt(name, a, b, out_dtype=BF16, tm=1024):
    M = a.shape[0]
    G, K, N = b.shape
    tm = _tile(M, tm)
    return _mm(name, a, b, grid=(M // tm, G, 1),
               a_spec=pl.BlockSpec((tm, N), lambda i, g, k: (i, g)),
               b_spec=pl.BlockSpec((None, K, N), lambda i, g, k: (g, 0, 0)),
               o_specs=[pl.BlockSpec((tm, K), lambda i, g, k: (i, g))],
               out_shapes=[jax.ShapeDtypeStruct((M, G * K), out_dtype)], acc_shape=(tm, K), cdims=NT)[0]


def mm_branch_tn(name, a, b, groups, out_dtype=BF16, tk=2048):
    T = a.shape[0]
    K, N = a.shape[1] // groups, b.shape[1] // groups
    tk = _tile(T, tk)
    return _mm(name, a, b, grid=(groups, 1, T // tk),
               a_spec=pl.BlockSpec((tk, K), lambda g, j, k: (k, g)),
               b_spec=pl.BlockSpec((tk, N), lambda g, j, k: (k, g)),
               o_specs=[pl.BlockSpec((None, K, N), lambda g, j, k: (g, 0, 0))],
               out_shapes=[jax.ShapeDtypeStruct((groups, K, N), out_dtype)], acc_shape=(K, N), cdims=TN)[0]


def mm_tn_cb(name, a, b, nb, out_dtype=BF16, tm=1024, tk=2048):
    K, M = a.shape
    N = b.shape[1]
    c = N // nb
    tm, tk = _tile(M, tm), _tile(K, tk)
    return _mm(name, a, b, grid=(M // tm, nb, K // tk),
               a_spec=pl.BlockSpec((tk, tm), lambda i, j, k: (k, i)),
               b_spec=pl.BlockSpec((tk, c), lambda i, j, k: (k, j)),
               o_specs=[pl.BlockSpec((None, tm, c), lambda i, j, k: (j, i, 0))],
               out_shapes=[jax.ShapeDtypeStruct((nb, M, c), out_dtype)], acc_shape=(tm, c), cdims=TN)[0]


def rms_fwd(name, x, w, out_dtype=BF16, tm=512):
    S, D = x.shape
    tm = _tile(S, tm)

    def body(x_ref, w_ref, o_ref):
        xv = x_ref[...]
        r = lax.rsqrt(jnp.mean(xv * xv, axis=-1, keepdims=True) + EPS)
        o_ref[...] = (xv * r * w_ref[...]).astype(o_ref.dtype)

    return pl.pallas_call(
        body, name=name, grid=(S // tm,),
        in_specs=[pl.BlockSpec((tm, D), lambda i: (i, 0)), pl.BlockSpec((1, D), lambda i: (0, 0))],
        out_specs=pl.BlockSpec((tm, D), lambda i: (i, 0)), out_shape=jax.ShapeDtypeStruct((S, D), out_dtype),
        compiler_params=_cp(("parallel",)),
    )(x, w.reshape(1, D))


def rms_bwd(name, x, w, dh, dres=None, tm=512):
    S, D = x.shape
    tm = _tile(S, tm)
    has_res = dres is not None

    def body(*refs):
        if has_res:
            x_ref, w_ref, dh_ref, dres_ref, dx_ref, dw_ref = refs
        else:
            x_ref, w_ref, dh_ref, dx_ref, dw_ref = refs
        xv = x_ref[...]
        dhv = dh_ref[...].astype(F32)
        r = lax.rsqrt(jnp.mean(xv * xv, axis=-1, keepdims=True) + EPS)
        g = dhv * w_ref[...]
        dx = r * g - xv * (r * r * r) * jnp.mean(xv * g, axis=-1, keepdims=True)
        if has_res:
            dx = dx + dres_ref[...]
        dx_ref[...] = dx

        @pl.when(pl.program_id(0) == 0)
        def _():
            dw_ref[...] = jnp.zeros_like(dw_ref)

        dw_ref[...] += jnp.sum(dhv * xv * r, axis=0, keepdims=True)

    row = pl.BlockSpec((tm, D), lambda i: (i, 0))
    vec = pl.BlockSpec((1, D), lambda i: (0, 0))
    args = [x, w.reshape(1, D), dh] + ([dres] if has_res else [])
    dx, dw = pl.pallas_call(
        body, name=name, grid=(S // tm,),
        in_specs=[row, vec, row] + ([row] if has_res else []),
        out_specs=[row, vec], out_shape=[jax.ShapeDtypeStruct((S, D), F32), jax.ShapeDtypeStruct((1, D), F32)],
        compiler_params=_cp(("arbitrary",)),
    )(*args)
    return dx, dw.reshape(D)


def loss_head(x, w, target, tm=512):
    S, D = x.shape
    tm = _tile(S, tm)

    def body(x_ref, w_ref, t_ref, loss_ref, dx_ref, dw_ref):
        xv = x_ref[...]
        wv = w_ref[...]
        r = lax.rsqrt(jnp.mean(xv * xv, axis=-1, keepdims=True) + EPS)
        y = xv * r * wv
        err = y - t_ref[...]
        dy = err * (1.0 / D)
        g = dy * wv
        dx_ref[...] = r * g - xv * (r * r * r) * jnp.mean(xv * g, axis=-1, keepdims=True)

        @pl.when(pl.program_id(0) == 0)
        def _():
            dw_ref[...] = jnp.zeros_like(dw_ref)
            loss_ref[...] = jnp.zeros_like(loss_ref)

        dw_ref[...] += jnp.sum(dy * xv * r, axis=0, keepdims=True)
        part = 0.5 * jnp.sum(jnp.mean(err * err, axis=-1, keepdims=True), axis=0, keepdims=True)
        loss_ref[...] += jnp.broadcast_to(part, loss_ref.shape)

    row = pl.BlockSpec((tm, D), lambda i: (i, 0))
    vec = pl.BlockSpec((1, D), lambda i: (0, 0))
    loss, dx, dw = pl.pallas_call(
        body, name="loss_head", grid=(S // tm,),
        in_specs=[row, vec, row],
        out_specs=[pl.BlockSpec((1, LANE), lambda i: (0, 0)), row, vec],
        out_shape=[jax.ShapeDtypeStruct((1, LANE), F32), jax.ShapeDtypeStruct((S, D), F32), jax.ShapeDtypeStruct((1, D), F32)],
        compiler_params=_cp(("arbitrary",)),
    )(x, w.reshape(1, D), target)
    return loss[0, 0], dx, dw.reshape(D)


SUB = 16
HG_W = HG_HEADS * LANE


def _hg_gates(q, f, lbv):
    sig = _sigmoid(f)
    fg = lbv + (1.0 - lbv) * sig
    sq = _sigmoid(q)
    return sig, fg, 1.0 - fg, sq, q * sq


def _hg_cumsum(logf):
    ri = lax.broadcasted_iota(I32, (CHUNK, CHUNK), 0)
    ci = lax.broadcasted_iota(I32, (CHUNK, CHUNK), 1)
    return _dot((ci <= ri).astype(F32), logf, NN, precision=lax.Precision.HIGHEST)


def _hg_rows():
    return lax.broadcasted_iota(I32, (CHUNK, LANE), 0)


def _hg_below(qf, kk, b, rows):
    blocks, parts = [jnp.zeros((SUB, CHUNK), F32)], []
    for i in range(1, CHUNK // SUB):
        bref = b[SUB * i - 1:SUB * i, :]
        rs = slice(SUB * i, SUB * (i + 1))
        eq = jnp.exp(b[rs] - bref)
        below = rows < SUB * i
        ek = jnp.exp(jnp.where(below, bref - b, NEG))
        qi = (qf[rs] * eq).astype(BF16)
        ki = (kk * ek).astype(BF16)
        blocks.append(_dot(qi, ki, NT))
        parts.append((qi, ki, eq, ek))
    return jnp.concatenate(blocks, axis=0), parts


def hgrn_fwd(proj, lb, nw):
    S = proj.shape[0]
    NC = S // CHUNK
    H = HG_HEADS

    def body(q_ref, f_ref, v_ref, g_ref, lb_ref, nw_ref, out_ref, st_out_ref, o_ref, st, kk_s, b_s):
        c = pl.program_id(0)

        @pl.when(c == 0)
        def _():
            st[...] = jnp.zeros_like(st)

        st_out_ref[...] = st[...]
        sig, fg, kk_all, sq, qf_all = _hg_gates(q_ref[...], f_ref[...], lb_ref[...])
        b_all = _hg_cumsum(jnp.log(fg))
        kk_s[...] = kk_all
        b_s[...] = b_all
        rows = _hg_rows()
        r16 = lax.broadcasted_iota(I32, (SUB, LANE), 0)
        for h in range(H):
            cs = slice(h * LANE, (h + 1) * LANE)
            qf, kk, b, v, g = qf_all[:, cs], kk_all[:, cs], b_all[:, cs], v_ref[:, cs], g_ref[:, cs]
            st_in = st[h]
            diag = []
            for i in range(CHUNK // SUB):
                rs = slice(SUB * i, SUB * (i + 1))
                acc = jnp.zeros((SUB, LANE), F32)
                for j in range(SUB):
                    row = pl.ds(SUB * i + j, 1)
                    e = jnp.exp(jnp.where(r16 >= j, b[rs] - b_s[row, cs], NEG))
                    col = jnp.sum(qf[rs] * (kk_s[row, cs] * e), axis=1, keepdims=True)
                    acc = acc + col * v_ref[row, cs]
                diag.append(acc)
            poff, _ = _hg_below(qf, kk, b, rows)
            vb = v.astype(BF16)
            bl = b[CHUNK - 1:CHUNK, :]
            o = (jnp.concatenate(diag, axis=0) + _dot(poff.astype(BF16), vb, NN)
                 + _dot((qf * jnp.exp(b)).astype(BF16), st_in.astype(BF16), NT))
            st[h] = st_in * jnp.exp(bl) + _dot(vb, (kk * jnp.exp(bl - b)).astype(BF16), TN)
            o_ref[:, cs] = o
            r = lax.rsqrt(jnp.mean(o * o, axis=-1, keepdims=True) + EPS)
            out_ref[:, cs] = (o * r * nw_ref[...] * (g * _sigmoid(g))).astype(out_ref.dtype)

    def seg(off):
        return pl.BlockSpec((CHUNK, HG_W), lambda c: (c, off // HG_W))

    blk = pl.BlockSpec((CHUNK, HG_W), lambda c: (c, 0))
    full = pltpu.VMEM((CHUNK, HG_W), F32)
    return pl.pallas_call(
        body, name="hgrn_fwd", grid=(NC,),
        in_specs=[seg(OFF_Q), seg(OFF_F), seg(OFF_V), seg(OFF_G),
                  pl.BlockSpec((1, HG_W), lambda c: (0, 0)), pl.BlockSpec((1, LANE), lambda c: (0, 0))],
        out_specs=[blk, pl.BlockSpec((None, H, LANE, LANE), lambda c: (c, 0, 0, 0)), blk],
        out_shape=[jax.ShapeDtypeStruct((S, 4 * HG_W), BF16), jax.ShapeDtypeStruct((NC, H, LANE, LANE), F32),
                   jax.ShapeDtypeStruct((S, HG_W), F32)],
        scratch_shapes=[pltpu.VMEM((H, LANE, LANE), F32), full, full],
        compiler_params=_cp(("arbitrary",)),
    )(proj, proj, proj, proj, lb.reshape(1, HG_W), nw.reshape(1, LANE))


def hgrn_bwd(proj, lb, nw, states, o_pre, dbcat, dproj):
    S = proj.shape[0]
    NC = S // CHUNK
    H = HG_HEADS

    def body(q_ref, f_ref, v_ref, g_ref, lb_ref, nw_ref, st_ref, o_ref, do_ref, _,
             dp_ref, dlb_ref, dnw_ref, dst, kk_s, b_s, do_s, db_s, dkk_s, dkk_d, dv_d):
        c = pl.program_id(0)
        dq_ref, df_ref, dv_ref, dg_ref = (dp_ref.at[:, pl.ds(off, HG_W)] for off in (OFF_Q, OFF_F, OFF_V, OFF_G))

        @pl.when(c == 0)
        def _():
            dst[...] = jnp.zeros_like(dst)
            dlb_ref[...] = jnp.zeros_like(dlb_ref)
            dnw_ref[...] = jnp.zeros_like(dnw_ref)

        q_all, g_all = q_ref[...], g_ref[...]
        lbv, nwv = lb_ref[...], nw_ref[...]
        sig, fg, kk_all, sq, qf_all = _hg_gates(q_all, f_ref[...], lbv)
        b_all = _hg_cumsum(jnp.log(fg))
        o_all = o_ref[...]
        dov = do_ref[...].astype(F32)
        sg = _sigmoid(g_all)
        gsg = g_all * sg
        dnw_acc = jnp.zeros((1, LANE), F32)
        for h in range(H):
            cs = slice(h * LANE, (h + 1) * LANE)
            o = o_all[:, cs]
            r = lax.rsqrt(jnp.mean(o * o, axis=-1, keepdims=True) + EPS)
            don = dov[:, cs] * gsg[:, cs]
            dnw_acc = dnw_acc + jnp.sum(don * o * r, axis=0, keepdims=True)
            gno = don * nwv
            do_s[:, cs] = r * gno - o * (r * r * r) * jnp.mean(o * gno, axis=-1, keepdims=True)
            dg_ref[:, cs] = (dov[:, cs] * (o * r * nwv) * _dsilu(g_all[:, cs], sg[:, cs])).astype(dg_ref.dtype)
        dnw_ref[...] += jnp.broadcast_to(dnw_acc, dnw_ref.shape)
        kk_s[...] = kk_all
        b_s[...] = b_all
        rows = _hg_rows()
        r16 = lax.broadcasted_iota(I32, (SUB, LANE), 0)
        for h in range(H):
            cs = slice(h * LANE, (h + 1) * LANE)
            qf, kk, b, v = qf_all[:, cs], kk_all[:, cs], b_all[:, cs], v_ref[:, cs]
            do = do_s[:, cs]
            st_in, dstv = st_ref[h], dst[h]
            bl = b[CHUNK - 1:CHUNK, :]
            eb, ebl, el = jnp.exp(b), jnp.exp(bl - b), jnp.exp(bl)
            qe, ke = qf * eb, kk * ebl
            vb, dob, stb, dstb = v.astype(BF16), do.astype(BF16), st_in.astype(BF16), dstv.astype(BF16)
            w_ = _dot(vb, dstb, NN)
            dqf = eb * _dot(dob, stb, NN)
            dkk = ebl * w_
            dv = _dot(ke.astype(BF16), dstb, NT)
            dbl = el * jnp.sum(st_in * dstv, axis=0, keepdims=True) + jnp.sum(ke * w_, axis=0, keepdims=True)
            dst[h] = dstv * el + _dot(dob, qe.astype(BF16), TN)
            poff, parts = _hg_below(qf, kk, b, rows)
            dpoff = _dot(dob, vb, NT).astype(BF16)
            dv = dv + _dot(poff.astype(BF16), dob, TN)
            dq_blocks = [jnp.zeros((SUB, LANE), F32)]
            for i, (qi, ki, eq, ek) in enumerate(parts, start=1):
                dpi = dpoff[SUB * i:SUB * (i + 1), :]
                dq_blocks.append(_dot(dpi, ki, NN) * eq)
                dkk = dkk + _dot(dpi, qi, TN) * ek
            dqf = dqf + jnp.concatenate(dq_blocks, axis=0)
            dq_diag = []
            for i in range(CHUNK // SUB):
                rs = slice(SUB * i, SUB * (i + 1))
                acc = jnp.zeros((SUB, LANE), F32)
                for j in range(SUB):
                    row = pl.ds(SUB * i + j, 1)
                    ks = kk_s[row, cs]
                    e = jnp.exp(jnp.where(r16 >= j, b[rs] - b_s[row, cs], NEG))
                    x = jnp.sum(do[rs] * v_ref[row, cs], axis=1, keepdims=True) * e
                    acc = acc + x * ks
                    dkk_d[row, cs] = jnp.sum(x * qf[rs], axis=0, keepdims=True)
                    col = jnp.sum(qf[rs] * (ks * e), axis=1, keepdims=True)
                    dv_d[row, cs] = jnp.sum(col * do[rs], axis=0, keepdims=True)
                dq_diag.append(acc)
            dqf = dqf + jnp.concatenate(dq_diag, axis=0)
            dkk = dkk + dkk_d[:, cs]
            dv = dv + dv_d[:, cs]
            dv_ref[:, cs] = dv.astype(dv_ref.dtype)
            db = qf * dqf - kk * dkk
            db_s[:, cs] = db + jnp.where(rows == CHUNK - 1, dbl, 0.0)
            dkk_s[:, cs] = dkk
            dq_ref[:, cs] = (dqf * _dsilu(q_all[:, cs], sq[:, cs])).astype(dq_ref.dtype)
        ri = lax.broadcasted_iota(I32, (CHUNK, CHUNK), 0)
        ci = lax.broadcasted_iota(I32, (CHUNK, CHUNK), 1)
        dlogf = _dot((ci >= ri).astype(F32), db_s[...], NN, precision=lax.Precision.HIGHEST)
        dfg = dlogf / fg - dkk_s[...]
        df_ref[...] = (dfg * (1.0 - lbv) * sig * (1.0 - sig)).astype(df_ref.dtype)
        dlb_ref[...] += jnp.broadcast_to(jnp.sum(dfg * (1.0 - sig), axis=0, keepdims=True), dlb_ref.shape)

    def seg(off):
        return pl.BlockSpec((CHUNK, HG_W), lambda c: (NC - 1 - c, off // HG_W))

    blk = pl.BlockSpec((CHUNK, HG_W), lambda c: (NC - 1 - c, 0))
    full = pltpu.VMEM((CHUNK, HG_W), F32)
    dproj, dlb, dnw = pl.pallas_call(
        body, name="hgrn_bwd", grid=(NC,),
        in_specs=[seg(OFF_Q), seg(OFF_F), seg(OFF_V), seg(OFF_G),
                  pl.BlockSpec((1, HG_W), lambda c: (0, 0)), pl.BlockSpec((1, LANE), lambda c: (0, 0)),
                  pl.BlockSpec((None, H, LANE, LANE), lambda c: (NC - 1 - c, 0, 0, 0)), blk, blk, ANY_SPACE],
        out_specs=[pl.BlockSpec((CHUNK, 4 * HG_W), lambda c: (NC - 1 - c, 0)),
                   pl.BlockSpec((8, HG_W), lambda c: (0, 0)), pl.BlockSpec((8, LANE), lambda c: (0, 0))],
        out_shape=[jax.ShapeDtypeStruct(dproj.shape, dproj.dtype), jax.ShapeDtypeStruct((8, HG_W), F32),
                   jax.ShapeDtypeStruct((8, LANE), F32)],
        input_output_aliases={9: 0},
        scratch_shapes=[pltpu.VMEM((H, LANE, LANE), F32)] + [full] * 7,
        compiler_params=_cp(("arbitrary",)),
    )(proj, proj, proj, proj, lb.reshape(1, HG_W), nw.reshape(1, LANE), states, o_pre, dbcat, dproj)
    return dproj, dlb[0], dnw[0]


CV_PAD = 32
ROWS = 256


def _colblk(S, off):
    return pl.BlockSpec((S, LANE), lambda j: (0, off // LANE + j))


def cv_fwd(proj, w32, bias):
    S = proj.shape[0]
    nchunk = S // ROWS

    def body(a_ref, g_ref, w_ref, b_ref, o_ref, zpad):
        zpad[pl.ds(0, CV_PAD), :] = jnp.zeros((CV_PAD, LANE), F32)

        def glu(c, _):
            r0 = pl.multiple_of(c * ROWS, ROWS)
            zpad[pl.ds(CV_PAD + r0, ROWS), :] = a_ref[pl.ds(r0, ROWS), :] * _sigmoid(g_ref[pl.ds(r0, ROWS), :])
            return 0

        lax.fori_loop(0, nchunk, glu, 0)

        def conv(c, _):
            r0 = pl.multiple_of(c * ROWS, ROWS)
            acc = jnp.broadcast_to(b_ref[...], (ROWS, LANE))
            for j in range(CV_KERNEL):
                acc = acc + w_ref[pl.ds(j, 1), :] * zpad[pl.ds(r0 + (CV_PAD - CV_KERNEL + 1) + j, ROWS), :]
            o_ref[pl.ds(r0, ROWS), :] = acc
            return 0

        lax.fori_loop(0, nchunk, conv, 0)

    return pl.pallas_call(
        body, name="cv_fwd", grid=(BRANCH_W // LANE,),
        in_specs=[_colblk(S, OFF_CV), _colblk(S, OFF_CV + BRANCH_W),
                  pl.BlockSpec((32, LANE), lambda j: (0, j)), pl.BlockSpec((1, LANE), lambda j: (0, j))],
        out_specs=pl.BlockSpec((S, LANE), lambda j: (0, j)), out_shape=jax.ShapeDtypeStruct((S, BRANCH_W), F32),
        scratch_shapes=[pltpu.VMEM((CV_PAD + S, LANE), F32)],
        compiler_params=_cp(("parallel",)),
    )(proj, proj, w32, bias.reshape(1, BRANCH_W))


def cv_bwd(proj, w32, dzc):
    S = proj.shape[0]
    nchunk = S // ROWS

    def body(a_ref, g_ref, w_ref, dz_ref, da_ref, dg_ref, dw_ref, db_ref, zpad, dpad):
        zpad[pl.ds(0, CV_PAD), :] = jnp.zeros((CV_PAD, LANE), F32)
        dpad[pl.ds(S, CV_PAD), :] = jnp.zeros((CV_PAD, LANE), F32)
        dw_ref[...] = jnp.zeros_like(dw_ref)

        def glu(c, dsum):
            r0 = pl.multiple_of(c * ROWS, ROWS)
            zpad[pl.ds(CV_PAD + r0, ROWS), :] = a_ref[pl.ds(r0, ROWS), :] * _sigmoid(g_ref[pl.ds(r0, ROWS), :])
            d = dz_ref[pl.ds(r0, ROWS), :]
            dpad[pl.ds(r0, ROWS), :] = d
            return dsum + jnp.sum(d, axis=0, keepdims=True)

        dsum = lax.fori_loop(0, nchunk, glu, jnp.zeros((1, LANE), F32))
        db_ref[...] = jnp.broadcast_to(dsum, db_ref.shape)

        def conv(c, _):
            r0 = pl.multiple_of(c * ROWS, ROWS)
            d = dpad[pl.ds(r0, ROWS), :]
            acc = jnp.zeros((ROWS, LANE), F32)
            for j in range(CV_KERNEL):
                acc = acc + w_ref[pl.ds(j, 1), :] * dpad[pl.ds(r0 + (CV_KERNEL - 1) - j, ROWS), :]
                zs = zpad[pl.ds(r0 + (CV_PAD - CV_KERNEL + 1) + j, ROWS), :]
                dw_ref[pl.ds(j, 1), :] += jnp.sum(d * zs, axis=0, keepdims=True)
            a = a_ref[pl.ds(r0, ROWS), :]
            sg = _sigmoid(g_ref[pl.ds(r0, ROWS), :])
            da_ref[pl.ds(r0, ROWS), :] = (acc * sg).astype(da_ref.dtype)
            dg_ref[pl.ds(r0, ROWS), :] = (acc * a * sg * (1.0 - sg)).astype(dg_ref.dtype)
            return 0

        lax.fori_loop(0, nchunk, conv, 0)

    blk = pl.BlockSpec((S, LANE), lambda j: (0, j))
    da, dg, dw, db = pl.pallas_call(
        body, name="cv_bwd", grid=(BRANCH_W // LANE,),
        in_specs=[_colblk(S, OFF_CV), _colblk(S, OFF_CV + BRANCH_W), pl.BlockSpec((32, LANE), lambda j: (0, j)), blk],
        out_specs=[blk, blk, pl.BlockSpec((32, LANE), lambda j: (0, j)), pl.BlockSpec((8, LANE), lambda j: (0, j))],
        out_shape=[jax.ShapeDtypeStruct((S, BRANCH_W), BF16), jax.ShapeDtypeStruct((S, BRANCH_W), BF16),
                   jax.ShapeDtypeStruct((32, BRANCH_W), F32), jax.ShapeDtypeStruct((8, BRANCH_W), F32)],
        scratch_shapes=[pltpu.VMEM((CV_PAD + S, LANE), F32), pltpu.VMEM((S + CV_PAD, LANE), F32)],
        compiler_params=_cp(("parallel",)),
    )(proj, proj, w32, dzc)
    return da, dg, dw, db[0]


def ln_silu_fwd(z, w, b, bcat, tm=512):
    S, C = z.shape
    tm = _tile(S, tm)

    def body(z_ref, w_ref, b_ref, _, o_ref):
        zv = z_ref[...]
        mu = jnp.mean(zv, axis=-1, keepdims=True)
        zc = zv - mu
        rstd = lax.rsqrt(jnp.mean(zc * zc, axis=-1, keepdims=True) + EPS)
        y = zc * rstd * w_ref[...] + b_ref[...]
        o_ref[...] = (y * _sigmoid(y)).astype(o_ref.dtype)

    row = pl.BlockSpec((tm, C), lambda i: (i, 0))
    vec = pl.BlockSpec((1, C), lambda i: (0, 0))
    return pl.pallas_call(
        body, name="ln_silu_fwd", grid=(S // tm,), in_specs=[row, vec, vec, ANY_SPACE],
        out_specs=pl.BlockSpec((tm, C), lambda i: (i, 1)), out_shape=jax.ShapeDtypeStruct(bcat.shape, bcat.dtype),
        input_output_aliases={3: 0}, compiler_params=_cp(("parallel",)),
    )(z, w.reshape(1, C), b.reshape(1, C), bcat)


def ln_silu_bwd(z, w, b, dbcat, tm=512):
    S, C = z.shape
    tm = _tile(S, tm)

    def body(z_ref, w_ref, b_ref, do_ref, dz_ref, dw_ref, db_ref):
        zv = z_ref[...]
        wv = w_ref[...]
        mu = jnp.mean(zv, axis=-1, keepdims=True)
        zc = zv - mu
        rstd = lax.rsqrt(jnp.mean(zc * zc, axis=-1, keepdims=True) + EPS)
        xh = zc * rstd
        y = xh * wv + b_ref[...]
        dy = do_ref[...].astype(F32) * _dsilu(y, _sigmoid(y))

        @pl.when(pl.program_id(0) == 0)
        def _():
            dw_ref[...] = jnp.zeros_like(dw_ref)
            db_ref[...] = jnp.zeros_like(db_ref)

        dw_ref[...] += jnp.sum(dy * xh, axis=0, keepdims=True)
        db_ref[...] += jnp.sum(dy, axis=0, keepdims=True)
        dxh = dy * wv
        dz_ref[...] = rstd * (dxh - jnp.mean(dxh, axis=-1, keepdims=True) - xh * jnp.mean(dxh * xh, axis=-1, keepdims=True))

    row = pl.BlockSpec((tm, C), lambda i: (i, 0))
    vec = pl.BlockSpec((1, C), lambda i: (0, 0))
    dz, dw, db = pl.pallas_call(
        body, name="ln_silu_bwd", grid=(S // tm,), in_specs=[row, vec, vec, pl.BlockSpec((tm, C), lambda i: (i, 1))],
        out_specs=[row, vec, vec],
        out_shape=[jax.ShapeDtypeStruct((S, C), F32), jax.ShapeDtypeStruct((1, C), F32), jax.ShapeDtypeStruct((1, C), F32)],
        compiler_params=_cp(("arbitrary",)),
    )(z, w.reshape(1, C), b.reshape(1, C), dbcat)
    return dz, dw.reshape(C), db.reshape(C)


PL_PAD = 16


def _pool_counts(r0, win):
    t = r0 + lax.broadcasted_iota(I32, (ROWS, LANE), 0)
    return jnp.minimum(t + 1, win).astype(F32)


def pool_fwd(proj, wg, scale, bcat):
    S = proj.shape[0]
    nchunk = S // ROWS

    def body(u_ref, w_ref, s_ref, _, o_ref, upad):
        g = pl.program_id(0)
        upad[pl.ds(0, PL_PAD), :] = jnp.zeros((PL_PAD, LANE), F32)

        def fill(c, _):
            r0 = pl.multiple_of(c * ROWS, ROWS)
            upad[pl.ds(PL_PAD + r0, ROWS), :] = u_ref[pl.ds(r0, ROWS), :]
            return 0

        lax.fori_loop(0, nchunk, fill, 0)
        wb = w_ref[...].astype(BF16)
        for gi, win in enumerate(POOL_WINDOWS):
            @pl.when(g == gi)
            def _(win=win):
                def chunk(c, _):
                    r0 = pl.multiple_of(c * ROWS, ROWS)
                    u = upad[pl.ds(PL_PAD + r0, ROWS), :]
                    ws = u
                    for j in range(1, win):
                        ws = ws + upad[pl.ds(PL_PAD + r0 - j, ROWS), :]
                    pooled = ws / _pool_counts(r0, win) - u
                    o_ref[pl.ds(r0, ROWS), :] = (_dot(pooled.astype(BF16), wb, NN) * s_ref[...]).astype(o_ref.dtype)
                    return 0

                lax.fori_loop(0, nchunk, chunk, 0)

    return pl.pallas_call(
        body, name="pool_fwd", grid=(len(POOL_WINDOWS),),
        in_specs=[_colblk(S, OFF_PL), pl.BlockSpec((None, LANE, LANE), lambda j: (j, 0, 0)), pl.BlockSpec((1, LANE), lambda j: (0, j)),
                  ANY_SPACE],
        out_specs=pl.BlockSpec((S, LANE), lambda j: (0, 2 * BRANCH_W // LANE + j)),
        out_shape=jax.ShapeDtypeStruct(bcat.shape, bcat.dtype), input_output_aliases={3: 0},
        scratch_shapes=[pltpu.VMEM((PL_PAD + S, LANE), F32)],
        compiler_params=_cp(("parallel",)),
    )(proj, wg, scale.reshape(1, BRANCH_W), bcat)


def pool_bwd(proj, wg, scale, dbcat, dproj):
    S = proj.shape[0]
    nchunk = S // ROWS

    def body(u_ref, w_ref, s_ref, dy_ref, _, du_ref, dw_ref, ds_ref, upad, dpn, nd):
        g = pl.program_id(0)
        upad[pl.ds(0, PL_PAD), :] = jnp.zeros((PL_PAD, LANE), F32)
        dpn[pl.ds(S, PL_PAD), :] = jnp.zeros((PL_PAD, LANE), F32)

        def fill(c, _):
            r0 = pl.multiple_of(c * ROWS, ROWS)
            upad[pl.ds(PL_PAD + r0, ROWS), :] = u_ref[pl.ds(r0, ROWS), :]
            return 0

        lax.fori_loop(0, nchunk, fill, 0)
        wb = w_ref[...].astype(BF16)
        sv = s_ref[...]
        for gi, win in enumerate(POOL_WINDOWS):
            @pl.when(g == gi)
            def _(win=win):
                def chunk(c, carry):
                    dw, dsc = carry
                    r0 = pl.multiple_of(c * ROWS, ROWS)
                    u = upad[pl.ds(PL_PAD + r0, ROWS), :]
                    ws = u
                    for j in range(1, win):
                        ws = ws + upad[pl.ds(PL_PAD + r0 - j, ROWS), :]
                    cnt = _pool_counts(r0, win)
                    pooled = (ws / cnt - u).astype(BF16)
                    dyv = dy_ref[pl.ds(r0, ROWS), :].astype(F32)
                    dsc = dsc + jnp.sum(dyv * _dot(pooled, wb, NN), axis=0, keepdims=True)
                    dys = (dyv * sv).astype(BF16)
                    dw = dw + _dot(pooled, dys, TN)
                    dp = _dot(dys, wb, NT)
                    dpn[pl.ds(r0, ROWS), :] = dp / cnt
                    nd[pl.ds(r0, ROWS), :] = -dp
                    return dw, dsc

                dw, dsc = lax.fori_loop(0, nchunk, chunk, (jnp.zeros((LANE, LANE), F32), jnp.zeros((1, LANE), F32)))
                dw_ref[...] = dw
                ds_ref[...] = jnp.broadcast_to(dsc, ds_ref.shape)

                def spread(c, _):
                    r0 = pl.multiple_of(c * ROWS, ROWS)
                    acc = nd[pl.ds(r0, ROWS), :]
                    for j in range(win):
                        acc = acc + dpn[pl.ds(r0 + j, ROWS), :]
                    du_ref[pl.ds(r0, ROWS), :] = acc.astype(du_ref.dtype)
                    return 0

                lax.fori_loop(0, nchunk, spread, 0)

    dproj, dw, ds = pl.pallas_call(
        body, name="pool_bwd", grid=(len(POOL_WINDOWS),),
        in_specs=[_colblk(S, OFF_PL), pl.BlockSpec((None, LANE, LANE), lambda j: (j, 0, 0)), pl.BlockSpec((1, LANE), lambda j: (0, j)),
                  pl.BlockSpec((S, LANE), lambda j: (0, 2 * BRANCH_W // LANE + j)), ANY_SPACE],
        out_specs=[_colblk(S, OFF_PL), pl.BlockSpec((None, LANE, LANE), lambda j: (j, 0, 0)), pl.BlockSpec((8, LANE), lambda j: (0, j))],
        out_shape=[jax.ShapeDtypeStruct(dproj.shape, dproj.dtype), jax.ShapeDtypeStruct((len(POOL_WINDOWS), LANE, LANE), F32),
                   jax.ShapeDtypeStruct((8, BRANCH_W), F32)],
        input_output_aliases={4: 0},
        scratch_shapes=[pltpu.VMEM((PL_PAD + S, LANE), F32), pltpu.VMEM((S + PL_PAD, LANE), F32), pltpu.VMEM((S, LANE), F32)],
        compiler_params=_cp(("parallel",)),
    )(proj, wg, scale.reshape(1, BRANCH_W), dbcat, dproj)
    return dproj, dw, ds[0]


LR_PAD = 8
SCAN_TILES = 4
GELU_C = math.sqrt(2.0 / math.pi)
GELU_A = 0.044715


def _gelu(y):
    return 0.5 * y * (1.0 + jnp.tanh(GELU_C * (y + GELU_A * y * y * y)))


def _dgelu(y):
    t = jnp.tanh(GELU_C * (y + GELU_A * y * y * y))
    return 0.5 * (1.0 + t) + 0.5 * y * (1.0 - t * t) * GELU_C * (1.0 + 3.0 * GELU_A * y * y)


def _lru_gates(xpad, r0, cw_ref, cb, wa, ba, wx, bx, sp8):
    xc = jnp.broadcast_to(cb, (ROWS, LANE))
    for j in range(LRU_CONV):
        xc = xc + cw_ref[pl.ds(j, 1), :] * xpad[pl.ds(r0 + (LR_PAD - LRU_CONV + 1) + j, ROWS), :]
    xb = xc.astype(BF16)
    r = _sigmoid(_dot(xb, wa, NN) + ba)
    ig = _sigmoid(_dot(xb, wx, NN) + bx)
    la = -sp8 * r
    a = jnp.exp(la)
    s = jnp.sqrt(-jnp.tanh(la) * (a * a + 1.0))
    return xc, r, ig, a, s


def _tile_scan(a, b, r8, up):
    for s in (1, 2, 4):
        keep = (r8 < 8 - s) if up else (r8 >= s)
        shift = 8 - s if up else s
        a_sh = jnp.where(keep, pltpu.roll(a, shift, 0), 1.0)
        b_sh = jnp.where(keep, pltpu.roll(b, shift, 0), 0.0)
        b = b + a * b_sh
        a = a * a_sh
    return a, b


def lru_fwd(proj, cw8, cb, wa_bd, ba, wx_bd, bx, sp8, bcat):
    S = proj.shape[0]
    nchunk = S // ROWS

    def body(x_ref, y_ref, cw_ref, cb_ref, wa_ref, ba_ref, wx_ref, bx_ref, sp_ref, _, o_ref, h_ref, xpad, a_s):
        xpad[pl.ds(0, LR_PAD), :] = jnp.zeros((LR_PAD, LANE), F32)

        def fill(c, _):
            r0 = pl.multiple_of(c * ROWS, ROWS)
            xpad[pl.ds(LR_PAD + r0, ROWS), :] = x_ref[pl.ds(r0, ROWS), :]
            return 0

        lax.fori_loop(0, nchunk, fill, 0)
        wa = wa_ref[...].astype(BF16)
        wx = wx_ref[...].astype(BF16)

        def gates(c, _):
            r0 = pl.multiple_of(c * ROWS, ROWS)
            xc, r, ig, a, s = _lru_gates(xpad, r0, cw_ref, cb_ref[...], wa, ba_ref[...], wx, bx_ref[...], sp_ref[...])
            a_s[pl.ds(r0, ROWS), :] = a
            h_ref[pl.ds(r0, ROWS), :] = s * (ig * xc)
            return 0

        lax.fori_loop(0, nchunk, gates, 0)

        r8 = lax.broadcasted_iota(I32, (8, LANE), 0)

        def scan(i, h):
            bases = [pl.multiple_of(i * (8 * SCAN_TILES) + 8 * j, 8) for j in range(SCAN_TILES)]
            maps = [_tile_scan(a_s[pl.ds(b, 8), :], h_ref[pl.ds(b, 8), :], r8, False) for b in bases]
            for b, (ca, cb_) in zip(bases, maps):
                out = cb_ + ca * h
                h_ref[pl.ds(b, 8), :] = out
                h = out[7:8, :]
            return h

        lax.fori_loop(0, S // (8 * SCAN_TILES), scan, jnp.zeros((1, LANE), F32))

        def gate_out(c, _):
            r0 = pl.multiple_of(c * ROWS, ROWS)
            o_ref[pl.ds(r0, ROWS), :] = (h_ref[pl.ds(r0, ROWS), :] * _gelu(y_ref[pl.ds(r0, ROWS), :])).astype(o_ref.dtype)
            return 0

        lax.fori_loop(0, nchunk, gate_out, 0)

    vec = pl.BlockSpec((1, LANE), lambda j: (0, j))
    mat = pl.BlockSpec((None, LANE, LANE), lambda j: (j, 0, 0))
    blk = pl.BlockSpec((S, LANE), lambda j: (0, j))
    return pl.pallas_call(
        body, name="lru_fwd", grid=(BRANCH_W // LANE,),
        in_specs=[_colblk(S, OFF_LX), _colblk(S, OFF_LY), pl.BlockSpec((8, LANE), lambda j: (0, j)), vec, mat, vec, mat, vec, vec,
                  ANY_SPACE],
        out_specs=[pl.BlockSpec((S, LANE), lambda j: (0, 3 * BRANCH_W // LANE + j)), blk],
        out_shape=[jax.ShapeDtypeStruct(bcat.shape, bcat.dtype), jax.ShapeDtypeStruct((S, BRANCH_W), F32)],
        input_output_aliases={9: 0},
        scratch_shapes=[pltpu.VMEM((LR_PAD + S, LANE), F32), pltpu.VMEM((S, LANE), F32)],
        compiler_params=_cp(("parallel",)),
    )(proj, proj, cw8, cb.reshape(1, -1), wa_bd, ba.reshape(1, -1), wx_bd, bx.reshape(1, -1), sp8.reshape(1, -1), bcat)


def lru_bwd(proj, cw8, cb, wa_bd, ba, wx_bd, bx, sp8, h, dbcat):
    S = proj.shape[0]
    nchunk = S // ROWS

    def body(x_ref, y_ref, cw_ref, cb_ref, wa_ref, ba_ref, wx_ref, bx_ref, sp_ref, h_ref, do_ref,
             dx_ref, dy_ref, dcw_ref, dcb_ref, dwa_ref, dba_ref, dwx_ref, dbx_ref, dsp_ref,
             xpad, a_s, g_s, hpad, dxc):
        xpad[pl.ds(0, LR_PAD), :] = jnp.zeros((LR_PAD, LANE), F32)
        hpad[pl.ds(0, LR_PAD), :] = jnp.zeros((LR_PAD, LANE), F32)
        dxc[pl.ds(S, LR_PAD), :] = jnp.zeros((LR_PAD, LANE), F32)
        dcw_ref[...] = jnp.zeros_like(dcw_ref)
        wa = wa_ref[...].astype(BF16)
        wx = wx_ref[...].astype(BF16)
        cbv, bav, bxv, spv = cb_ref[...], ba_ref[...], bx_ref[...], sp_ref[...]

        def fill(c, _):
            r0 = pl.multiple_of(c * ROWS, ROWS)
            xpad[pl.ds(LR_PAD + r0, ROWS), :] = x_ref[pl.ds(r0, ROWS), :]
            hv = h_ref[pl.ds(r0, ROWS), :]
            hpad[pl.ds(LR_PAD + r0, ROWS), :] = hv
            yv = y_ref[pl.ds(r0, ROWS), :]
            dov = do_ref[pl.ds(r0, ROWS), :].astype(F32)
            g_s[pl.ds(r0, ROWS), :] = dov * _gelu(yv)
            dy_ref[pl.ds(r0, ROWS), :] = (dov * hv * _dgelu(yv)).astype(dy_ref.dtype)
            return 0

        lax.fori_loop(0, nchunk, fill, 0)

        def gates(c, _):
            r0 = pl.multiple_of(c * ROWS, ROWS)
            _, _, _, a, _ = _lru_gates(xpad, r0, cw_ref, cbv, wa, bav, wx, bxv, spv)
            a_s[pl.ds(r0, ROWS), :] = a
            return 0

        lax.fori_loop(0, nchunk, gates, 0)

        r8 = lax.broadcasted_iota(I32, (8, LANE), 0)

        def rscan(i, carry):
            bases = [pl.multiple_of(S - 8 - i * (8 * SCAN_TILES) - 8 * j, 8) for j in range(SCAN_TILES)]
            firsts, maps = [], []
            for b in bases:
                a8 = a_s[pl.ds(b, 8), :]
                above = jnp.where(r8 < 7, pltpu.roll(a8, 7, 0), 1.0)
                firsts.append(a8[0:1, :])
                maps.append(_tile_scan(above, g_s[pl.ds(b, 8), :], r8, True))
            for b, a0, (ca, cb_) in zip(bases, firsts, maps):
                out = cb_ + ca * carry
                g_s[pl.ds(b, 8), :] = out
                carry = a0 * out[0:1, :]
            return carry

        lax.fori_loop(0, S // (8 * SCAN_TILES), rscan, jnp.zeros((1, LANE), F32))

        def chain(c, carry):
            dwa, dwx, dba, dbx, dsp, dcb = carry
            r0 = pl.multiple_of(c * ROWS, ROWS)
            xc, r, ig, a, s = _lru_gates(xpad, r0, cw_ref, cbv, wa, bav, wx, bxv, spv)
            gt = g_s[pl.ds(r0, ROWS), :]
            hprev = hpad[pl.ds(r0 + LR_PAD - 1, ROWS), :]
            da = gt * hprev - gt * ig * xc * (a / s)
            dig = gt * s * xc
            dla = da * a
            dsp = dsp + jnp.sum(-dla * r, axis=0, keepdims=True)
            dpr = (-dla * spv) * r * (1.0 - r)
            dpi = dig * ig * (1.0 - ig)
            dprb, dpib, xb = dpr.astype(BF16), dpi.astype(BF16), xc.astype(BF16)
            d = gt * s * ig + _dot(dprb, wa, NT) + _dot(dpib, wx, NT)
            dwa = dwa + _dot(xb, dprb, TN)
            dwx = dwx + _dot(xb, dpib, TN)
            dba = dba + jnp.sum(dpr, axis=0, keepdims=True)
            dbx = dbx + jnp.sum(dpi, axis=0, keepdims=True)
            dcb = dcb + jnp.sum(d, axis=0, keepdims=True)
            dxc[pl.ds(r0, ROWS), :] = d
            for j in range(LRU_CONV):
                xs = xpad[pl.ds(r0 + (LR_PAD - LRU_CONV + 1) + j, ROWS), :]
                dcw_ref[pl.ds(j, 1), :] += jnp.sum(d * xs, axis=0, keepdims=True)
            return dwa, dwx, dba, dbx, dsp, dcb

        zm, zv = jnp.zeros((LANE, LANE), F32), jnp.zeros((1, LANE), F32)
        dwa, dwx, dba, dbx, dsp, dcb = lax.fori_loop(0, nchunk, chain, (zm, zm, zv, zv, zv, zv))
        dwa_ref[...] = dwa
        dwx_ref[...] = dwx
        dba_ref[...] = jnp.broadcast_to(dba, dba_ref.shape)
        dbx_ref[...] = jnp.broadcast_to(dbx, dbx_ref.shape)
        dsp_ref[...] = jnp.broadcast_to(dsp, dsp_ref.shape)
        dcb_ref[...] = jnp.broadcast_to(dcb, dcb_ref.shape)

        def convt(c, _):
            r0 = pl.multiple_of(c * ROWS, ROWS)
            acc = jnp.zeros((ROWS, LANE), F32)
            for j in range(LRU_CONV):
                acc = acc + cw_ref[pl.ds(j, 1), :] * dxc[pl.ds(r0 + (LRU_CONV - 1) - j, ROWS), :]
            dx_ref[pl.ds(r0, ROWS), :] = acc.astype(dx_ref.dtype)
            return 0

        lax.fori_loop(0, nchunk, convt, 0)

    vec = pl.BlockSpec((1, LANE), lambda j: (0, j))
    vec8 = pl.BlockSpec((8, LANE), lambda j: (0, j))
    mat = pl.BlockSpec((None, LANE, LANE), lambda j: (j, 0, 0))
    blk = pl.BlockSpec((S, LANE), lambda j: (0, j))
    nblk = BRANCH_W // LANE
    v8 = jax.ShapeDtypeStruct((8, BRANCH_W), F32)
    m4 = jax.ShapeDtypeStruct((nblk, LANE, LANE), F32)
    big = jax.ShapeDtypeStruct((S, BRANCH_W), BF16)
    seq = pltpu.VMEM((S, LANE), F32)
    dx, dy, dcw, dcb, dwa, dba, dwx, dbx, dsp = pl.pallas_call(
        body, name="lru_bwd", grid=(nblk,),
        in_specs=[_colblk(S, OFF_LX), _colblk(S, OFF_LY), vec8, vec, mat, vec, mat, vec, vec, blk,
                  pl.BlockSpec((S, LANE), lambda j: (0, 3 * BRANCH_W // LANE + j))],
        out_specs=[blk, blk, vec8, vec8, mat, vec8, mat, vec8, vec8],
        out_shape=[big, big, v8, v8, m4, v8, m4, v8, v8],
        scratch_shapes=[pltpu.VMEM((LR_PAD + S, LANE), F32), seq, seq, pltpu.VMEM((LR_PAD + S, LANE), F32),
                        pltpu.VMEM((S + LR_PAD, LANE), F32)],
        compiler_params=_cp(("parallel",)),
    )(proj, proj, cw8, cb.reshape(1, -1), wa_bd, ba.reshape(1, -1), wx_bd, bx.reshape(1, -1), sp8.reshape(1, -1), h, dbcat)
    return dx, dy, dcw, dcb[0], dwa, dba[0], dwx, dbx[0], dsp[0]


MG_COLS = 512
N_BRANCH = 4
BCAT_W = N_BRANCH * BRANCH_W


def _gate_spec(tm, k):
    return pl.BlockSpec((tm, MG_COLS), lambda j, i: (i, (OFF_GATE + k * D_MODEL) // MG_COLS + j))


def _up_spec(tm, k):
    return pl.BlockSpec((tm, MG_COLS), lambda j, i: (i, k * D_MODEL // MG_COLS + j))


def merge_fwd(ups, proj, gate_b, tm=512):
    S = proj.shape[0]
    tm = _tile(S, tm)

    def body(u0, u1, u2, u3, g0, g1, g2, g3, gb_ref, o_ref):
        acc = jnp.zeros((tm, MG_COLS), F32)
        for k, (u, g) in enumerate(((u0, g0), (u1, g1), (u2, g2), (u3, g3))):
            acc = acc + _sigmoid(g[...] + gb_ref[pl.ds(k, 1), :]) * u[...]
        o_ref[...] = acc.astype(o_ref.dtype)

    blk = pl.BlockSpec((tm, MG_COLS), lambda j, i: (i, j))
    return pl.pallas_call(
        body, name="merge_fwd", grid=(D_MODEL // MG_COLS, S // tm),
        in_specs=[_up_spec(tm, k) for k in range(N_BRANCH)] + [_gate_spec(tm, k) for k in range(N_BRANCH)]
        + [pl.BlockSpec((N_BRANCH, MG_COLS), lambda j, i: (0, j))],
        out_specs=blk, out_shape=jax.ShapeDtypeStruct((S, D_MODEL), BF16),
        compiler_params=_cp(("parallel", "parallel")),
    )(ups, ups, ups, ups, proj, proj, proj, proj, gate_b)


def merge_bwd(dmerged, ups, proj, gate_b, tm=1024):
    S = proj.shape[0]
    tm = _tile(S, tm)
    halves = D_MODEL // MG_COLS

    def body(dm_ref, u_ref, g_ref, gb_ref, du_ref, dg_ref, dgb_ref):
        k = pl.program_id(0)

        @pl.when(pl.program_id(2) == 0)
        def _():
            dgb_ref[...] = jnp.zeros_like(dgb_ref)

        dm = dm_ref[...].astype(F32)
        sg = _sigmoid(g_ref[...] + gb_ref[pl.ds(k, 1), :])
        du_ref[...] = (dm * sg).astype(du_ref.dtype)
        dgk = dm * u_ref[...] * sg * (1.0 - sg)
        dg_ref[...] = dgk.astype(dg_ref.dtype)
        dgb_ref[...] += jnp.broadcast_to(jnp.sum(dgk, axis=0, keepdims=True), dgb_ref.shape)

    dups, dproj, dgb = pl.pallas_call(
        body, name="merge_bwd", grid=(N_BRANCH, halves, S // tm),
        in_specs=[pl.BlockSpec((tm, MG_COLS), lambda k, j, i: (i, j)),
                  pl.BlockSpec((tm, MG_COLS), lambda k, j, i: (i, k * halves + j)),
                  pl.BlockSpec((tm, MG_COLS), lambda k, j, i: (i, OFF_GATE // MG_COLS + k * halves + j)),
                  pl.BlockSpec((N_BRANCH, MG_COLS), lambda k, j, i: (0, j))],
        out_specs=[pl.BlockSpec((tm, MG_COLS), lambda k, j, i: (i, k * halves + j)),
                   pl.BlockSpec((tm, MG_COLS), lambda k, j, i: (i, OFF_GATE // MG_COLS + k * halves + j)),
                   pl.BlockSpec((8, MG_COLS), lambda k, j, i: (k, j))],
        out_shape=[jax.ShapeDtypeStruct((S, N_BRANCH * D_MODEL), BF16), jax.ShapeDtypeStruct((S, IN_W), BF16),
                   jax.ShapeDtypeStruct((8 * N_BRANCH, D_MODEL), F32)],
        compiler_params=_cp(("parallel", "parallel", "arbitrary")),
    )(dmerged, ups, proj, gate_b)
    return dups, dproj, dgb.reshape(N_BRANCH, 8, D_MODEL)[:, 0]


def attn_fwd(q, kv, tm=512):
    S = q.shape[0]
    M = kv.shape[0]
    tm = _tile(S, tm)
    scale = XA_HD ** -0.5

    def body(q_ref, kv_ref, o_ref):
        for hh in range(XA_HEADS):
            cs = pl.ds(hh * XA_HD, XA_HD)
            qh = q_ref[:, cs]
            kh = kv_ref[:, cs]
            vh = kv_ref[:, pl.ds(D_MODEL + hh * XA_HD, XA_HD)]
            s = _dot(qh, kh, NT) * scale
            p = jnp.exp(s - jnp.max(s, axis=-1, keepdims=True))
            p = p / jnp.sum(p, axis=-1, keepdims=True)
            o_ref[:, cs] = _dot(p.astype(BF16), vh, NN).astype(o_ref.dtype)

    return pl.pallas_call(
        body, name="attn_fwd", grid=(S // tm,),
        in_specs=[pl.BlockSpec((tm, D_MODEL), lambda i: (i, 0)), pl.BlockSpec((M, 2 * D_MODEL), lambda i: (0, 0))],
        out_specs=pl.BlockSpec((tm, D_MODEL), lambda i: (i, 0)), out_shape=jax.ShapeDtypeStruct((S, D_MODEL), BF16),
        compiler_params=_cp(("parallel",)),
    )(q, kv)


def attn_bwd(q, kv, do, tm=512):
    S = q.shape[0]
    M = kv.shape[0]
    tm = _tile(S, tm)
    scale = XA_HD ** -0.5

    def body(q_ref, kv_ref, do_ref, dq_ref, dkv_ref):
        @pl.when(pl.program_id(0) == 0)
        def _():
            dkv_ref[...] = jnp.zeros_like(dkv_ref)

        for hh in range(XA_HEADS):
            cs = pl.ds(hh * XA_HD, XA_HD)
            vs = pl.ds(D_MODEL + hh * XA_HD, XA_HD)
            qh = q_ref[:, cs]
            kh = kv_ref[:, cs]
            vh = kv_ref[:, vs]
            doh = do_ref[:, cs]
            s = _dot(qh, kh, NT) * scale
            p = jnp.exp(s - jnp.max(s, axis=-1, keepdims=True))
            p = p / jnp.sum(p, axis=-1, keepdims=True)
            dp = _dot(doh, vh, NT)
            ds = (p * (dp - jnp.sum(dp * p, axis=-1, keepdims=True)) * scale).astype(BF16)
            dq_ref[:, cs] = _dot(ds, kh, NN).astype(dq_ref.dtype)
            dkv_ref[:, cs] += _dot(ds, qh, TN)
            dkv_ref[:, vs] += _dot(p.astype(BF16), doh, TN)

    row = pl.BlockSpec((tm, D_MODEL), lambda i: (i, 0))
    full = pl.BlockSpec((M, 2 * D_MODEL), lambda i: (0, 0))
    return pl.pallas_call(
        body, name="attn_bwd", grid=(S // tm,), in_specs=[row, full, row], out_specs=[row, full],
        out_shape=[jax.ShapeDtypeStruct((S, D_MODEL), BF16), jax.ShapeDtypeStruct((M, 2 * D_MODEL), F32)],
        compiler_params=_cp(("arbitrary",)),
    )(q, kv, do)


def sum_parts(parts, own=None, tm=256):
    n, R, C = parts.shape
    tm = _tile(R, tm)
    has_own = own is not None

    def body(*refs):
        p_ref, o_ref = refs[0], refs[-1]
        acc = refs[1][...].astype(F32) if has_own else p_ref[0].astype(F32)
        for j in range(0 if has_own else 1, n):
            acc = acc + p_ref[j].astype(F32)
        o_ref[...] = acc

    row = pl.BlockSpec((tm, C), lambda i: (i, 0))
    return pl.pallas_call(
        body, name="sum_parts", grid=(R // tm,),
        in_specs=[pl.BlockSpec((n, tm, C), lambda i: (0, i, 0))] + ([row] if has_own else []), out_specs=row,
        out_shape=jax.ShapeDtypeStruct((R, C), F32), compiler_params=_cp(("parallel",)),
    )(*([parts, own] if has_own else [parts]))


def adamw(w, g, m, v, tm=256):
    R, C = w.shape
    tm = _tile(R, tm)
    c1 = 1.0 / (1.0 - ADAM_B1 ** ADAM_STEP)
    c2 = 1.0 / (1.0 - ADAM_B2 ** ADAM_STEP)

    def body(w_ref, g_ref, m_ref, v_ref, d_ref, nm_ref, nv_ref):
        gv = g_ref[...]
        nm = ADAM_B1 * m_ref[...] + (1.0 - ADAM_B1) * gv
        nv = ADAM_B2 * v_ref[...] + (1.0 - ADAM_B2) * (gv * gv)
        nm_ref[...] = nm
        nv_ref[...] = nv
        d_ref[...] = -ADAM_LR * ((nm * c1) / (jnp.sqrt(nv * c2) + ADAM_EPS) + ADAM_WD * w_ref[...])

    blk = pl.BlockSpec((tm, C), lambda i: (i, 0))
    sd = jax.ShapeDtypeStruct((R, C), F32)
    return pl.pallas_call(
        body, name="adamw", grid=(R // tm,), in_specs=[blk] * 4, out_specs=[blk] * 3, out_shape=[sd] * 3,
        compiler_params=_cp(("parallel",)),
    )(w, g, m, v)


ANY = pl.BlockSpec(memory_space=pl.ANY)


def _place():
    return lax.axis_index("x"), lax.axis_index("y"), lax.axis_index("c")


def _slot(px, py, pc):
    return 4 * px + 2 * py + pc


def all_gather(name, shards, after=()):
    n = len(shards)
    n_in = n + len(after)

    def body(*refs):
        x_refs, out_refs = refs[:n], refs[n_in:n_in + n]
        send_sems, recv_sems, local_sems = refs[n_in + n:]
        x, y, c = _place()
        me, sibling = (x, y, c), (x, y, 1 - c)
        chips = [(1 - x, y), (x, 1 - y), (1 - x, 1 - y)]

        def copy(a, k, block, to, src=None):
            rows = out_refs[a].at[_slot(*block)]
            return pltpu.make_async_remote_copy(
                src_ref=rows if src is None else src, dst_ref=rows,
                send_sem=send_sems.at[7 * a + k], recv_sem=recv_sems.at[7 * a + k],
                device_id=to, device_id_type=MESH)

        mine = [pltpu.make_async_copy(x_refs[a], out_refs[a].at[_slot(*me)], local_sems.at[a]) for a in range(n)]
        for cp in mine:
            cp.start()
        first = []
        for a in range(n):
            first.append(copy(a, 0, me, sibling, src=x_refs[a]))
            first += [copy(a, 1 + j, me, (*chip, c), src=x_refs[a]) for j, chip in enumerate(chips)]
        for cp in first:
            cp.start()
        passed = []
        for a in range(n):
            for j, chip in enumerate(chips):
                copy(a, 1 + j, (*chip, c), me).wait_recv()
                cp = copy(a, 4 + j, (*chip, c), sibling)
                cp.start()
                passed.append(cp)
        for a in range(n):
            copy(a, 0, sibling, me).wait_recv()
            for j, chip in enumerate(chips):
                copy(a, 4 + j, (*chip, 1 - c), me).wait_recv()
        for cp in first + passed:
            cp.wait_send()
        for cp in mine:
            cp.wait()

    return pl.pallas_call(
        body, name=name, in_specs=[ANY] * n_in, out_specs=[ANY] * n,
        out_shape=[jax.ShapeDtypeStruct((N_DEV, *s.shape), s.dtype) for s in shards],
        scratch_shapes=[pltpu.SemaphoreType.DMA((7 * n,)), pltpu.SemaphoreType.DMA((7 * n,)), pltpu.SemaphoreType.DMA((n,))],
    )(*shards, *after)


HBM = pl.BlockSpec(memory_space=pltpu.HBM)
SEM = pl.BlockSpec(memory_space=pltpu.SEMAPHORE)
EFFECT = pltpu.SideEffectType.DATAFLOW_SIDE_EFFECTING
N_PEER = N_DEV - 1
RELATIONS = [(dx, dy, dc) for dx in (0, 1) for dy in (0, 1) for dc in (0, 1)][1:]


def _peer(place, rel):
    return tuple(1 - v if d else v for v, d in zip(place, rel))


def gather_start(name, shards, me, before):
    n = len(shards)

    def body(*refs):
        x_refs, land_refs = refs[:n], refs[n:2 * n]
        send_sems, recv_sems = refs[2 * n + len(before):2 * n + len(before) + 2]
        token = refs[-1]
        place = _place()
        mine = _slot(*place)
        for a in range(n):
            for rel in RELATIONS:
                pltpu.make_async_remote_copy(
                    src_ref=x_refs[a], dst_ref=land_refs[a].at[mine], send_sem=send_sems.at[a], recv_sem=recv_sems.at[a],
                    device_id=_peer(place, rel), device_id_type=MESH).start()
        token[...] = jnp.zeros_like(token)

    lands = [lax.dynamic_update_index_in_dim(lax.empty((N_DEV, *s.shape), s.dtype), s, me, 0) for s in shards]
    outs = pl.pallas_call(
        body, name=name,
        in_specs=[HBM] * (2 * n) + [ANY] * len(before),
        out_specs=[SEM, SEM] + [HBM] * (2 * n) + [pl.BlockSpec(memory_space=pltpu.VMEM)],
        out_shape=[pltpu.SemaphoreType.DMA((n,)), pltpu.SemaphoreType.DMA((n,))]
        + [pltpu.HBM(t.shape, t.dtype) for t in (*shards, *lands)] + [jax.ShapeDtypeStruct((8, LANE), F32)],
        input_output_aliases={i: 2 + i for i in range(2 * n)},
        compiler_params=pltpu.CompilerParams(has_side_effects=EFFECT),
    )(*[pltpu.with_memory_space_constraint(t, pltpu.HBM) for t in (*shards, *lands)], *before)
    return (outs[0], outs[1], outs[2:2 + n], outs[2 + n:2 + 2 * n]), outs[-1]


def gather_wait(name, state, after):
    send_sems, recv_sems, shards, lands = state
    n = len(shards)

    def body(*refs):
        land_refs = refs[n:2 * n]
        s_sems, r_sems = refs[2 * n:2 * n + 2]
        place = _place()
        for a in range(n):
            seven = land_refs[a].at[pl.ds(0, N_PEER)]
            cp = pltpu.make_async_remote_copy(
                src_ref=seven, dst_ref=seven, send_sem=s_sems.at[a], recv_sem=r_sems.at[a], device_id=place, device_id_type=MESH)
            cp.wait_send()
            cp.wait_recv()

    outs = pl.pallas_call(
        body, name=name,
        in_specs=[HBM] * (2 * n) + [SEM, SEM] + [ANY] * len(after), out_specs=[HBM] * (2 * n),
        out_shape=[pltpu.HBM(t.shape, t.dtype) for t in (*shards, *lands)],
        input_output_aliases={i: i for i in range(2 * n)},
        compiler_params=pltpu.CompilerParams(has_side_effects=EFFECT),
    )(*shards, *lands, send_sems, recv_sems, *after)
    return outs[n:]


def exchange_start(name, grads, before):
    n = len(grads)

    def body(*refs):
        g_refs, land_refs = refs[:n], refs[n:2 * n]
        send_sems, recv_sems = refs[2 * n + 1:2 * n + 3]
        token = refs[-1]
        place = _place()
        for a in range(n):
            for r, rel in enumerate(RELATIONS):
                p = _peer(place, rel)
                pltpu.make_async_remote_copy(
                    src_ref=g_refs[a].at[_slot(*p)], dst_ref=land_refs[a].at[r],
                    send_sem=send_sems.at[a], recv_sem=recv_sems.at[a], device_id=p, device_id_type=MESH).start()
        token[...] = jnp.zeros_like(token)

    lands = [lax.empty((N_PEER, *g.shape[1:]), g.dtype) for g in grads]
    outs = pl.pallas_call(
        body, name=name,
        in_specs=[HBM] * (2 * n) + [ANY],
        out_specs=[SEM, SEM] + [HBM] * (2 * n) + [pl.BlockSpec(memory_space=pltpu.VMEM)],
        out_shape=[pltpu.SemaphoreType.DMA((n,)), pltpu.SemaphoreType.DMA((n,))]
        + [pltpu.HBM(g.shape, g.dtype) for g in grads] + [pltpu.HBM(t.shape, t.dtype) for t in lands]
        + [jax.ShapeDtypeStruct((8, LANE), F32)],
        input_output_aliases={i: 2 + i for i in range(2 * n)},
        compiler_params=pltpu.CompilerParams(has_side_effects=EFFECT),
    )(*[pltpu.with_memory_space_constraint(t, pltpu.HBM) for t in (*grads, *lands)], before)
    return (outs[0], outs[1], outs[2:2 + n], outs[2 + n:2 + 2 * n]), outs[-1]


def exchange_wait(name, state, after):
    send_sems, recv_sems, grads, lands = state
    n = len(grads)

    def body(*refs):
        g_refs, land_refs = refs[:n], refs[n:2 * n]
        s_sems, r_sems = refs[2 * n:2 * n + 2]
        place = _place()
        for a in range(n):
            cp = pltpu.make_async_remote_copy(
                src_ref=g_refs[a].at[pl.ds(0, N_PEER)], dst_ref=land_refs[a],
                send_sem=s_sems.at[a], recv_sem=r_sems.at[a], device_id=place, device_id_type=MESH)
            cp.wait_send()
            cp.wait_recv()

    outs = pl.pallas_call(
        body, name=name,
        in_specs=[HBM] * (2 * n) + [SEM, SEM, ANY], out_specs=[HBM] * (2 * n),
        out_shape=[pltpu.HBM(t.shape, t.dtype) for t in (*grads, *lands)],
        input_output_aliases={i: i for i in range(2 * n)},
        compiler_params=pltpu.CompilerParams(has_side_effects=EFFECT),
    )(*grads, *lands, send_sems, recv_sems, after)
    return outs[:n], outs[n:]


WEIGHTS = ['norm_mix_w', 'w_in', 'hg_lb_raw', 'hg_norm_w', 'cv_dw_w', 'cv_dw_b', 'cv_ln_w', 'cv_ln_b', 'pl_w', 'pl_scale',
           'lru_conv_w', 'lru_conv_b', 'lru_wa', 'lru_ba', 'lru_wx', 'lru_bx', 'lru_lambda', 'gate_b', 'w_branch', 'w_out',
           'norm_mem_w', 'mem_norm_w', 'xa_wq', 'xa_wkv', 'xa_wo', 'norm_ffn_w', 'ffn_w1', 'ffn_w2', 'final_norm_w']
BIG = ('w_in', 'w_branch', 'w_out', 'xa_wq', 'xa_wkv', 'xa_wo', 'ffn_w1', 'ffn_w2')
SMALL_SHARDED = ('cv_dw_w', 'lru_conv_w', 'gate_b')
SMALL = tuple(n for n in WEIGHTS if n not in BIG and n not in SMALL_SHARDED)
PACK_ROWS = 256


def _pack(arrs):
    flat = jnp.concatenate([a.reshape(-1).astype(F32) for a in arrs])
    tile = PACK_ROWS * LANE
    padded = -(-flat.shape[0] // tile) * tile
    return jnp.pad(flat, (0, padded - flat.shape[0])).reshape(-1, LANE)


def _unpack(packed, shapes):
    flat = packed.reshape(-1)
    out, off = [], 0
    for s in shapes:
        n = math.prod(s)
        out.append(flat[off:off + n].reshape(s))
        off += n
    return out


def _gather_last(g, shard_shape):
    nd = len(shard_shape)
    full = jnp.moveaxis(g, 0, nd - 1)
    return full.reshape(*shard_shape[:-1], N_DEV * shard_shape[-1])


def _natural(blocks):
    nb, k, c = blocks.shape
    return jnp.transpose(blocks, (1, 0, 2)).reshape(k, nb * c)


def _block_diag(w):
    w2 = w.reshape(4, 2, 64, 64)
    z = jnp.zeros((4, 64, 64), w.dtype)
    return jnp.concatenate([jnp.concatenate([w2[:, 0], z], axis=2), jnp.concatenate([z, w2[:, 1]], axis=2)], axis=1)


def _block_diag_t(d):
    return jnp.stack([d[:, :64, :64], d[:, 64:, 64:]], axis=1).reshape(8, 64, 64)


def _lower_bounds(raw):
    lb = jnp.cumsum(jax.nn.softmax(raw.astype(F32), axis=0), axis=0)
    return lb - lb[0:1]


def _decay_rates(lam):
    return (LRU_C * jax.nn.softplus(-lam.astype(F32))).reshape(DEPTH, BRANCH_W)


def _relu2(acc):
    r = jnp.maximum(acc, 0.0)
    return acc, r * r


def _relu2_grad(acc, u):
    return (acc * 2.0 * jnp.maximum(u, 0.0),)


def _add(acc, e):
    return (acc + e,)


def _layer_fwd(x0, mem, p, g, rest, after=()):
    h1 = rms_fwd("rms_mix", x0, p['norm_mix_w'])
    proj = mm_nt("mm_in", h1, g['w_in'], tn=2176, after=after)[0]
    bcat, states, o_hg = hgrn_fwd(proj, p['lb'], p['hg_norm_w'])
    zc = cv_fwd(proj, p['cv_w32'], p['cv_dw_b'])
    bcat = ln_silu_fwd(zc, p['cv_ln_w'], p['cv_ln_b'], bcat)
    bcat = pool_fwd(proj, p['pl_w'], p['pl_scale'], bcat)
    bcat, hst = lru_fwd(proj, p['lru_cw8'], p['lru_conv_b'], p['wa_bd'], p['lru_ba'], p['wx_bd'], p['lru_bx'], p['sp8'], bcat)
    more, after = rest(bcat)
    g = {**g, **more}
    ups = mm_branch_nn("mm_up", bcat, g['w_branch'], tm=2048, after=after)
    merged = merge_fwd(ups, proj, p['gate_b'])
    x1 = mm_nn("mm_out", merged, g['w_out'], epi=_add, extras=(x0,))[0]
    h2 = rms_fwd("rms_mem", x1, p['norm_mem_w'])
    q = mm_nn("mm_q", h2, g['xa_wq'], out_dtype=BF16)[0]
    memn = rms_fwd("rms_memtok", mem, p['mem_norm_w'])
    kv = mm_nn("mm_kv", memn, g['xa_wkv'], out_dtype=BF16, tn=2048)[0]
    oa = attn_fwd(q, kv)
    x2 = mm_nn("mm_o", oa, g['xa_wo'], epi=_add, extras=(x1,))[0]
    h3 = rms_fwd("rms_ffn", x2, p['norm_ffn_w'])
    u, act = mm_nn("mm_ffn1", h3, g['ffn_w1'], epi=_relu2, out_dtypes=[BF16, BF16])
    x3 = mm_nn("mm_ffn2", act, g['ffn_w2'], epi=_add, extras=(x2,))[0]
    res = dict(x0=x0, h1=h1, proj=proj, states=states, o_hg=o_hg, zc=zc, hst=hst, bcat=bcat, ups=ups, merged=merged,
               x1=x1, h2=h2, q=q, memn=memn, kv=kv, oa=oa, x2=x2, h3=h3, u=u, act=act)
    return x3, res, g


def _layer_bwd(dx3, mem, p, g, r, midway, finish):
    gs, gb = {}, {}
    du = mm_nt("mm_dffn2", dx3, g['ffn_w2'], out_dtype=BF16, epi=_relu2_grad, extras=(r['u'],))[0]
    gb['ffn_w2'] = mm_tn("mm_gw2", r['act'], dx3).reshape(N_DEV, -1, D_MODEL)
    gb['ffn_w1'] = mm_tn_cb("mm_gw1", r['h3'], du, N_DEV)
    dh3 = mm_nt("mm_dffn1", du, g['ffn_w1'], out_dtype=BF16)[0]
    dx2, gs['norm_ffn_w'] = rms_bwd("rmsb_ffn", r['x2'], p['norm_ffn_w'], dh3, dx3)
    doa = mm_nt("mm_do", dx2, g['xa_wo'], out_dtype=BF16)[0]
    gb['xa_wo'] = mm_tn("mm_gwo", r['oa'], dx2).reshape(N_DEV, -1, D_MODEL)
    dq, dkv = attn_bwd(r['q'], r['kv'], doa)
    gb['xa_wq'] = mm_tn("mm_gwq", r['h2'], dq).reshape(N_DEV, -1, D_MODEL)
    dh2 = mm_nt("mm_dq", dq, g['xa_wq'], out_dtype=BF16)[0]
    gb['xa_wkv'] = mm_tn_cb("mm_gwkv", r['memn'], dkv, N_DEV)
    dmemn = mm_nt("mm_dkv", dkv, g['xa_wkv'], out_dtype=BF16)[0]
    _, gs['mem_norm_w'] = rms_bwd("rmsb_memtok", mem, p['mem_norm_w'], dmemn)
    dx1, gs['norm_mem_w'] = rms_bwd("rmsb_mem", r['x1'], p['norm_mem_w'], dh2, dx2)
    after = midway(gb, dx1)
    gb = {}
    dmerged = mm_nt("mm_dout", dx1, g['w_out'], out_dtype=BF16, after=after)[0]
    gb['w_out'] = mm_tn("mm_gwout", r['merged'], dx1).reshape(N_DEV, -1, D_MODEL)
    dups, dproj, gs['gate_b'] = merge_bwd(dmerged, r['ups'], r['proj'], p['gate_b'])
    gwb = mm_branch_tn("mm_gwb", r['bcat'], dups, N_BRANCH)
    gb['w_branch'] = jnp.transpose(gwb.reshape(N_BRANCH, BRANCH_W, N_DEV, -1), (2, 0, 1, 3))
    dbcat = mm_branch_nt("mm_dup", dups, g['w_branch'], tm=2048)
    dproj, gs['lb'], gs['hg_norm_w'] = hgrn_bwd(r['proj'], p['lb'], p['hg_norm_w'], r['states'], r['o_hg'], dbcat, dproj)
    dzc, gs['cv_ln_w'], gs['cv_ln_b'] = ln_silu_bwd(r['zc'], p['cv_ln_w'], p['cv_ln_b'], dbcat)
    dca, dcg, dcw, gs['cv_dw_b'] = cv_bwd(r['proj'], p['cv_w32'], dzc)
    gs['cv_dw_w'] = dcw[:CV_KERNEL]
    dproj, gs['pl_w'], gs['pl_scale'] = pool_bwd(r['proj'], p['pl_w'], p['pl_scale'], dbcat, dproj)
    dlx, dly, dlcw, gs['lru_conv_b'], dwa, gs['lru_ba'], dwx, gs['lru_bx'], gs['sp8'] = lru_bwd(
        r['proj'], p['lru_cw8'], p['lru_conv_b'], p['wa_bd'], p['lru_ba'], p['wx_bd'], p['lru_bx'], p['sp8'], r['hst'], dbcat)
    gs['lru_conv_w'] = dlcw[:LRU_CONV]
    gs['lru_wa'], gs['lru_wx'] = _block_diag_t(dwa), _block_diag_t(dwx)
    gs['lru_ba'], gs['lru_bx'] = gs['lru_ba'].reshape(8, 64), gs['lru_bx'].reshape(8, 64)
    for off, piece in ((OFF_CV, dca), (OFF_CV + BRANCH_W, dcg), (OFF_LX, dlx), (OFF_LY, dly)):
        dproj = lax.dynamic_update_slice(dproj, piece, (0, off))
    gb['w_in'] = mm_tn("mm_gwin", dproj, r['h1'], tm=2176).reshape(N_DEV, -1, D_MODEL)
    dh1 = mm_nn("mm_din", dproj, g['w_in'], out_dtype=BF16, tk=2176, after=finish(gb, dx1))[0]
    dx0, gs['norm_mix_w'] = rms_bwd("rmsb_mix", r['x0'], p['norm_mix_w'], dh1, dx1)
    return dx0, gs


def kernel(x, mem, norm_mix_w, w_in, hg_lb_raw, hg_norm_w, cv_dw_w, cv_dw_b, cv_ln_w, cv_ln_b, pl_w, pl_scale, lru_conv_w, lru_conv_b, lru_wa, lru_ba, lru_wx, lru_bx, lru_lambda, gate_b, w_branch, w_out, norm_mem_w, mem_norm_w, xa_wq, xa_wkv, xa_wo, norm_ffn_w, ffn_w1, ffn_w2, final_norm_w, loss_target, m_norm_mix_w, m_w_in, m_hg_lb_raw, m_hg_norm_w, m_cv_dw_w, m_cv_dw_b, m_cv_ln_w, m_cv_ln_b, m_pl_w, m_pl_scale, m_lru_conv_w, m_lru_conv_b, m_lru_wa, m_lru_ba, m_lru_wx, m_lru_bx, m_lru_lambda, m_gate_b, m_w_branch, m_w_out, m_norm_mem_w, m_mem_norm_w, m_xa_wq, m_xa_wkv, m_xa_wo, m_norm_ffn_w, m_ffn_w1, m_ffn_w2, m_final_norm_w, v_norm_mix_w, v_w_in, v_hg_lb_raw, v_hg_norm_w, v_cv_dw_w, v_cv_dw_b, v_cv_ln_w, v_cv_ln_b, v_pl_w, v_pl_scale, v_lru_conv_w, v_lru_conv_b, v_lru_wa, v_lru_ba, v_lru_wx, v_lru_bx, v_lru_lambda, v_gate_b, v_w_branch, v_w_out, v_norm_mem_w, v_mem_norm_w, v_xa_wq, v_xa_wkv, v_xa_wo, v_norm_ffn_w, v_ffn_w1, v_ffn_w2, v_final_norm_w):
    W = dict(zip(WEIGHTS, (norm_mix_w, w_in, hg_lb_raw, hg_norm_w, cv_dw_w, cv_dw_b, cv_ln_w, cv_ln_b, pl_w, pl_scale, lru_conv_w, lru_conv_b, lru_wa, lru_ba, lru_wx, lru_bx, lru_lambda, gate_b, w_branch, w_out, norm_mem_w, mem_norm_w, xa_wq, xa_wkv, xa_wo, norm_ffn_w, ffn_w1, ffn_w2, final_norm_w)))
    Mo = dict(zip(WEIGHTS, (m_norm_mix_w, m_w_in, m_hg_lb_raw, m_hg_norm_w, m_cv_dw_w, m_cv_dw_b, m_cv_ln_w, m_cv_ln_b, m_pl_w, m_pl_scale, m_lru_conv_w, m_lru_conv_b, m_lru_wa, m_lru_ba, m_lru_wx, m_lru_bx, m_lru_lambda, m_gate_b, m_w_branch, m_w_out, m_norm_mem_w, m_mem_norm_w, m_xa_wq, m_xa_wkv, m_xa_wo, m_norm_ffn_w, m_ffn_w1, m_ffn_w2, m_final_norm_w)))
    Vo = dict(zip(WEIGHTS, (v_norm_mix_w, v_w_in, v_hg_lb_raw, v_hg_norm_w, v_cv_dw_w, v_cv_dw_b, v_cv_ln_w, v_cv_ln_b, v_pl_w, v_pl_scale, v_lru_conv_w, v_lru_conv_b, v_lru_wa, v_lru_ba, v_lru_wx, v_lru_bx, v_lru_lambda, v_gate_b, v_w_branch, v_w_out, v_norm_mem_w, v_mem_norm_w, v_xa_wq, v_xa_wkv, v_xa_wo, v_norm_ffn_w, v_ffn_w1, v_ffn_w2, v_final_norm_w)))
    me = _slot(*_place())
    xs, mems, target = x[0], mem[0], loss_target[0]

    shard_shapes = [W[n].shape for n in SMALL_SHARDED]
    gathered = all_gather("ag_small", [_pack([W[n] for n in SMALL_SHARDED])])[0]
    parts = [jnp.stack(ps) for ps in zip(*[_unpack(gathered[d], shard_shapes) for d in range(N_DEV)])]
    full_small = {n: _gather_last(parts[i], shard_shapes[i]) for i, n in enumerate(SMALL_SHARDED)}
    lb_all, lb_vjp = jax.vjp(_lower_bounds, hg_lb_raw)
    sp8_all, sp8_vjp = jax.vjp(_decay_rates, lru_lambda)

    def layer_params(l):
        p = {n: W[n][l] for n in SMALL if n != 'final_norm_w'}
        p['lb'] = lb_all[l]
        p['sp8'] = sp8_all[l]
        p['cv_w32'] = jnp.pad(full_small['cv_dw_w'][l], ((0, 32 - CV_KERNEL), (0, 0)))
        p['lru_cw8'] = jnp.pad(full_small['lru_conv_w'][l], ((0, 8 - LRU_CONV), (0, 0)))
        p['gate_b'] = full_small['gate_b'][l]
        p['wa_bd'], p['wx_bd'] = _block_diag(lru_wa[l]), _block_diag(lru_wx[l])
        p['lru_ba'], p['lru_bx'] = lru_ba[l].reshape(-1), lru_bx[l].reshape(-1)
        return p

    def shards_of(l):
        first = [jnp.transpose(w_in[l]).astype(BF16)]
        others = [w[l].astype(BF16) for w in (w_branch, w_out, xa_wq, xa_wkv, xa_wo, ffn_w1, ffn_w2)]
        return first, others

    def start_gather(l, before):
        first, others = shards_of(l)
        state_a, tok_a = gather_start(f"ag_start{l}a", first, me, before)
        state_b, tok_b = gather_start(f"ag_start{l}b", others, me, (*before, tok_a))
        return state_a, state_b, (tok_a, tok_b)

    def first_of(o):
        return dict(w_in=o[0].reshape(IN_W, D_MODEL))

    def others_of(o):
        wb = jnp.transpose(o[0], (1, 2, 0, 3)).reshape(N_BRANCH, BRANCH_W, D_MODEL)
        return dict(w_branch=wb, w_out=o[1].reshape(D_MODEL, D_MODEL),
                    xa_wq=o[2].reshape(D_MODEL, D_MODEL), xa_wkv=_natural(o[3]), xa_wo=o[4].reshape(D_MODEL, D_MODEL),
                    ffn_w1=_natural(o[5]), ffn_w2=o[6].reshape(D_FF, D_MODEL))

    params = [layer_params(l) for l in range(DEPTH)]
    mats, residuals = [], []
    xc = xs
    first, others = shards_of(0)
    whole = all_gather("ag_layer0", first)
    state_b, started = gather_start("ag_start0b", others, me, (whole[0],))
    gathers = {}
    for l in range(DEPTH):
        if l == 0:
            g_first = first_of(whole)
        else:
            state_a, state_b, _ = gathers.pop(l)
            g_first = first_of(gather_wait(f"ag_wait{l}a", state_a, (xc,)))

        def rest(mixed, l=l):
            more = others_of(gather_wait(f"ag_wait{l}b", state_b, (mixed,)))
            if l + 1 == DEPTH:
                return more, ()
            gathers[l + 1] = start_gather(l + 1, (more['w_out'],))
            return more, gathers[l + 1][2]

        xc, res, g = _layer_fwd(xc, mems, params[l], g_first, rest, after=(started,) if l == 0 else ())
        mats.append(g)
        residuals.append(res)
    loss_part, dx, g_final = loss_head(xc, final_norm_w, target)
    loss = lax.psum(loss_part, ("x", "y", "c"))

    small_grads = [None] * DEPTH
    big_grads = [{} for _ in range(DEPTH)]
    pending = []

    def send(l, group, blocks, before):
        names = list(blocks)
        state, tok = exchange_start(f"rs_start{l}{group}", [blocks[n] for n in names], before)
        pending.append((l, group, names, state))
        return (tok,)

    def land(after):
        l, group, names, state = pending.pop(0)
        sent, landed = exchange_wait(f"rs_wait{l}{group}", state, after)
        for n, s, t in zip(names, sent, landed):
            own = lax.dynamic_index_in_dim(s, me, 0, keepdims=False).reshape(-1, s.shape[-1])
            big_grads[l][n] = sum_parts(t.reshape(N_PEER, -1, t.shape[-1]), own).reshape(t.shape[1:])

    for l in reversed(range(DEPTH)):
        dx, small_grads[l] = _layer_bwd(dx, mems, params[l], mats[l], residuals[l],
                                        lambda blocks, dx1, l=l: send(l, "a", blocks, dx1),
                                        lambda blocks, dx1, l=l: send(l, "b", blocks, dx1))
        while pending[0][0] > l:
            land(dx)

    def stacked(n):
        return jnp.stack([small_grads[l][n] for l in range(DEPTH)])

    part = {n: stacked(n) for n in SMALL if n not in ('final_norm_w', 'hg_lb_raw', 'lru_lambda')}
    part['final_norm_w'] = g_final
    part['hg_lb_raw'] = lb_vjp(stacked('lb'))[0]
    part['lru_lambda'] = sp8_vjp(stacked('sp8'))[0]
    for n in SMALL_SHARDED:
        part[n] = stacked(n)
    names = list(SMALL) + list(SMALL_SHARDED)
    full_shapes = [part[n].shape for n in names]
    packed = _pack([part[n] for n in names])
    state, _ = exchange_start("rs_small_start", [packed.reshape(N_DEV, -1, LANE)], packed)
    sent, landed = exchange_wait("rs_small_wait", state, packed)
    mine = sum_parts(landed[0], lax.dynamic_index_in_dim(sent[0], me, 0, keepdims=False))
    total = all_gather("ag_grads", [mine])[0].reshape(-1, LANE)
    while pending:
        land(total)

    G = {}
    G['w_in'] = jnp.stack([jnp.transpose(big_grads[l]['w_in']) for l in range(DEPTH)])
    for n in ('w_branch', 'w_out', 'xa_wq', 'xa_wkv', 'xa_wo', 'ffn_w1', 'ffn_w2'):
        G[n] = jnp.stack([big_grads[l][n] for l in range(DEPTH)])
    for n, t in zip(names, _unpack(total, full_shapes)):
        if n in SMALL_SHARDED:
            c = t.shape[-1] // N_DEV
            t = lax.dynamic_slice_in_dim(t, me * c, c, axis=t.ndim - 1)
        G[n] = t

    delta, new_m, new_v = {}, {}, {}
    for n in BIG:
        c = W[n].shape[-1]
        d, nm, nv = adamw(W[n].reshape(-1, c), G[n].reshape(-1, c), Mo[n].reshape(-1, c), Vo[n].reshape(-1, c))
        delta[n], new_m[n], new_v[n] = d.reshape(W[n].shape), nm.reshape(W[n].shape), nv.reshape(W[n].shape)
    shapes = [W[n].shape for n in names]
    d, nm, nv = adamw(_pack([W[n] for n in names]), _pack([G[n] for n in names]), _pack([Mo[n] for n in names]), _pack([Vo[n] for n in names]))
    for n, a, b, c in zip(names, _unpack(d, shapes), _unpack(nm, shapes), _unpack(nv, shapes)):
        delta[n], new_m[n], new_v[n] = a, b, c
    return (loss, dx[None], *[G[n] for n in WEIGHTS], *[delta[n] for n in WEIGHTS],
            *[new_m[n] for n in WEIGHTS], *[new_v[n] for n in WEIGHTS])
```

```python
import functools
import math

import jax
import jax.numpy as jnp
from jax import lax
from jax.experimental import pallas as pl
from jax.experimental.pallas import tpu as pltpu

F32 = jnp.float32
BF16 = jnp.bfloat16
I32 = jnp.int32

N_DEV = 8
D_MODEL = 1024
DEPTH = 4
CHUNK = 64
EPS = 1e-6
HG_HEADS = 4
BRANCH_W = 512
CV_KERNEL = 31
POOL_WINDOWS = (2, 4, 8, 16)
LRU_CONV = 4
LRU_C = 8.0
XA_HEADS = 4
XA_HD = D_MODEL // XA_HEADS
D_FF = 4 * D_MODEL
IN_W = 8704
OFF_Q, OFF_F, OFF_V, OFF_G, OFF_CV, OFF_PL, OFF_LX, OFF_LY, OFF_GATE = 0, 512, 1024, 1536, 2048, 3072, 3584, 4096, 4608
LANE = 128
ADAM_LR, ADAM_B1, ADAM_B2, ADAM_EPS, ADAM_WD, ADAM_STEP = 0.001, 0.9, 0.999, 1e-08, 0.01, 10
VMEM_LIMIT = 56 * 1024 * 1024
MESH = pl.DeviceIdType.MESH
NEG = -1e30
ANY_SPACE = pl.BlockSpec(memory_space=pl.ANY)


def _cp(sem, **kw):
    return pltpu.CompilerParams(dimension_semantics=sem, vmem_limit_bytes=VMEM_LIMIT, **kw)


def _sigmoid(x):
    return 1.0 / (1.0 + jnp.exp(-x))


def _dsilu(x, s):
    return s * (1.0 + x * (1.0 - s))


def _dot(a, b, cdims, precision=None):
    return lax.dot_general(a, b, (cdims, ((), ())), preferred_element_type=F32, precision=precision)


NN = ((1,), (0,))
NT = ((1,), (1,))
TN = ((0,), (0,))


def _mm(name, a, b, *, grid, a_spec, b_spec, o_specs, out_shapes, acc_shape, cdims, epi=None, extras=(), extra_specs=(), after=()):
    nk = grid[2]
    n_e, n_o = len(extras), len(out_shapes)
    extras = (*extras, *after)
    extra_specs = (*extra_specs, *[ANY_SPACE] * len(after))

    def body(*refs):
        a_ref, b_ref = refs[0], refs[1]
        e_refs = refs[2:2 + n_e]
        o_refs = refs[2 + len(extras):2 + len(extras) + n_o]

        def finish(acc):
            vals = epi(acc, *[r[...] for r in e_refs]) if epi is not None else (acc,)
            for r, v in zip(o_refs, vals):
                r[...] = v.astype(r.dtype)

        part = _dot(a_ref[...].astype(BF16), b_ref[...].astype(BF16), cdims)
        if nk == 1:
            finish(part)
        else:
            acc_ref = refs[-1]
            k = pl.program_id(2)

            @pl.when(k == 0)
            def _():
                acc_ref[...] = part

            @pl.when(k > 0)
            def _():
                acc_ref[...] += part

            @pl.when(k == nk - 1)
            def _():
                finish(acc_ref[...])

    return pl.pallas_call(
        body, name=name, grid=grid,
        in_specs=[a_spec, b_spec, *extra_specs], out_specs=list(o_specs), out_shape=list(out_shapes),
        scratch_shapes=[] if nk == 1 else [pltpu.VMEM(acc_shape, F32)],
        compiler_params=_cp(("parallel", "parallel", "arbitrary")),
    )(a, b, *extras)


def _tile(n, pref):
    t = min(n, pref)
    while n % t:
        t //= 2
    return t


def mm_nt(name, a, b, out_dtype=F32, epi=None, extras=(), n_out=1, out_dtypes=None, tm=1024, tn=1024, tk=2048, after=()):
    M, K = a.shape
    N = b.shape[0]
    tm, tn, tk = _tile(M, tm), _tile(N, tn), _tile(K, tk)
    odt = out_dtypes or [out_dtype] * n_out
    o_spec = pl.BlockSpec((tm, tn), lambda i, j, k: (i, j))
    return _mm(name, a, b, grid=(M // tm, N // tn, K // tk),
               a_spec=pl.BlockSpec((tm, tk), lambda i, j, k: (i, k)),
               b_spec=pl.BlockSpec((tn, tk), lambda i, j, k: (j, k)),
               o_specs=[o_spec] * len(odt), out_shapes=[jax.ShapeDtypeStruct((M, N), d) for d in odt],
               acc_shape=(tm, tn), cdims=NT, epi=epi, extras=extras, extra_specs=[o_spec] * len(extras), after=after)


def mm_nn(name, a, b, out_dtype=F32, epi=None, extras=(), n_out=1, out_dtypes=None, tm=1024, tn=1024, tk=2048, after=()):
    M, K = a.shape
    N = b.shape[1]
    tm, tn, tk = _tile(M, tm), _tile(N, tn), _tile(K, tk)
    odt = out_dtypes or [out_dtype] * n_out
    o_spec = pl.BlockSpec((tm, tn), lambda i, j, k: (i, j))
    return _mm(name, a, b, grid=(M // tm, N // tn, K // tk),
               a_spec=pl.BlockSpec((tm, tk), lambda i, j, k: (i, k)),
               b_spec=pl.BlockSpec((tk, tn), lambda i, j, k: (k, j)),
               o_specs=[o_spec] * len(odt), out_shapes=[jax.ShapeDtypeStruct((M, N), d) for d in odt],
               acc_shape=(tm, tn), cdims=NN, epi=epi, extras=extras, extra_specs=[o_spec] * len(extras), after=after)


def mm_tn(name, a, b, out_dtype=BF16, tm=1024, tn=1024, tk=2048):
    K, M = a.shape
    N = b.shape[1]
    tm, tn, tk = _tile(M, tm), _tile(N, tn), _tile(K, tk)
    return _mm(name, a, b, grid=(M // tm, N // tn, K // tk),
               a_spec=pl.BlockSpec((tk, tm), lambda i, j, k: (k, i)),
               b_spec=pl.BlockSpec((tk, tn), lambda i, j, k: (k, j)),
               o_specs=[pl.BlockSpec((tm, tn), lambda i, j, k: (i, j))],
               out_shapes=[jax.ShapeDtypeStruct((M, N), out_dtype)], acc_shape=(tm, tn), cdims=TN)[0]


def mm_branch_nn(name, a, b, out_dtype=BF16, tm=1024, after=()):
    M = a.shape[0]
    G, K, N = b.shape
    tm = _tile(M, tm)
    return _mm(name, a, b, grid=(M // tm, G, 1),
               a_spec=pl.BlockSpec((tm, K), lambda i, g, k: (i, g)),
               b_spec=pl.BlockSpec((None, K, N), lambda i, g, k: (g, 0, 0)),
               o_specs=[pl.BlockSpec((tm, N), lambda i, g, k: (i, g))],
               out_shapes=[jax.ShapeDtypeStruct((M, G * N), out_dtype)], acc_shape=(tm, N), cdims=NN, after=after)[0]


def mm_branch_nt(name, a, b, out_dtype=BF16, tm=1024):
    M = a.shape[0]
    G, K, N = b.shape
    tm = _tile(M, tm)
    return _mm(name, a, b, grid=(M // tm, G, 1),
               a_spec=pl.BlockSpec((tm, N), lambda i, g, k: (i, g)),
               b_spec=pl.BlockSpec((None, K, N), lambda i, g, k: (g, 0, 0)),
               o_specs=[pl.BlockSpec((tm, K), lambda i, g, k: (i, g))],
               out_shapes=[jax.ShapeDtypeStruct((M, G * K), out_dtype)], acc_shape=(tm, K), cdims=NT)[0]


def mm_branch_tn(name, a, b, groups, out_dtype=BF16, tk=2048):
    T = a.shape[0]
    K, N = a.shape[1] // groups, b.shape[1] // groups
    tk = _tile(T, tk)
    return _mm(name, a, b, grid=(groups, 1, T // tk),
               a_spec=pl.BlockSpec((tk, K), lambda g, j, k: (k, g)),
               b_spec=pl.BlockSpec((tk, N), lambda g, j, k: (k, g)),
               o_specs=[pl.BlockSpec((None, K, N), lambda g, j, k: (g, 0, 0))],
               out_shapes=[jax.ShapeDtypeStruct((groups, K, N), out_dtype)], acc_shape=(K, N), cdims=TN)[0]


def mm_tn_cb(name, a, b, nb, out_dtype=BF16, tm=1024, tk=2048):
    K, M = a.shape
    N = b.shape[1]
    c = N // nb
    tm, tk = _tile(M, tm), _tile(K, tk)
    return _mm(name, a, b, grid=(M // tm, nb, K // tk),
               a_spec=pl.BlockSpec((tk, tm), lambda i, j, k: (k, i)),
               b_spec=pl.BlockSpec((tk, c), lambda i, j, k: (k, j)),
               o_specs=[pl.BlockSpec((None, tm, c), lambda i, j, k: (j, i, 0))],
               out_shapes=[jax.ShapeDtypeStruct((nb, M, c), out_dtype)], acc_shape=(tm, c), cdims=TN)[0]


def rms_fwd(name, x, w, out_dtype=BF16, tm=512):
    S, D = x.shape
    tm = _tile(S, tm)

    def body(x_ref, w_ref, o_ref):
        xv = x_ref[...]
        r = lax.rsqrt(jnp.mean(xv * xv, axis=-1, keepdims=True) + EPS)
        o_ref[...] = (xv * r * w_ref[...]).astype(o_ref.dtype)

    return pl.pallas_call(
        body, name=name, grid=(S // tm,),
        in_specs=[pl.BlockSpec((tm, D), lambda i: (i, 0)), pl.BlockSpec((1, D), lambda i: (0, 0))],
        out_specs=pl.BlockSpec((tm, D), lambda i: (i, 0)), out_shape=jax.ShapeDtypeStruct((S, D), out_dtype),
        compiler_params=_cp(("parallel",)),
    )(x, w.reshape(1, D))


def rms_bwd(name, x, w, dh, dres=None, tm=512):
    S, D = x.shape
    tm = _tile(S, tm)
    has_res = dres is not None

    def body(*refs):
        if has_res:
            x_ref, w_ref, dh_ref, dres_ref, dx_ref, dw_ref = refs
        else:
            x_ref, w_ref, dh_ref, dx_ref, dw_ref = refs
        xv = x_ref[...]
        dhv = dh_ref[...].astype(F32)
        r = lax.rsqrt(jnp.mean(xv * xv, axis=-1, keepdims=True) + EPS)
        g = dhv * w_ref[...]
        dx = r * g - xv * (r * r * r) * jnp.mean(xv * g, axis=-1, keepdims=True)
        if has_res:
            dx = dx + dres_ref[...]
        dx_ref[...] = dx

        @pl.when(pl.program_id(0) == 0)
        def _():
            dw_ref[...] = jnp.zeros_like(dw_ref)

        dw_ref[...] += jnp.sum(dhv * xv * r, axis=0, keepdims=True)

    row = pl.BlockSpec((tm, D), lambda i: (i, 0))
    vec = pl.BlockSpec((1, D), lambda i: (0, 0))
    args = [x, w.reshape(1, D), dh] + ([dres] if has_res else [])
    dx, dw = pl.pallas_call(
        body, name=name, grid=(S // tm,),
        in_specs=[row, vec, row] + ([row] if has_res else []),
        out_specs=[row, vec], out_shape=[jax.ShapeDtypeStruct((S, D), F32), jax.ShapeDtypeStruct((1, D), F32)],
        compiler_params=_cp(("arbitrary",)),
    )(*args)
    return dx, dw.reshape(D)


def loss_head(x, w, target, tm=512):
    S, D = x.shape
    tm = _tile(S, tm)

    def body(x_ref, w_ref, t_ref, loss_ref, dx_ref, dw_ref):
        xv = x_ref[...]
        wv = w_ref[...]
        r = lax.rsqrt(jnp.mean(xv * xv, axis=-1, keepdims=True) + EPS)
        y = xv * r * wv
        err = y - t_ref[...]
        dy = err * (1.0 / D)
        g = dy * wv
        dx_ref[...] = r * g - xv * (r * r * r) * jnp.mean(xv * g, axis=-1, keepdims=True)

        @pl.when(pl.program_id(0) == 0)
        def _():
            dw_ref[...] = jnp.zeros_like(dw_ref)
            loss_ref[...] = jnp.zeros_like(loss_ref)

        dw_ref[...] += jnp.sum(dy * xv * r, axis=0, keepdims=True)
        part = 0.5 * jnp.sum(jnp.mean(err * err, axis=-1, keepdims=True), axis=0, keepdims=True)
        loss_ref[...] += jnp.broadcast_to(part, loss_ref.shape)

    row = pl.BlockSpec((tm, D), lambda i: (i, 0))
    vec = pl.BlockSpec((1, D), lambda i: (0, 0))
    loss, dx, dw = pl.pallas_call(
        body, name="loss_head", grid=(S // tm,),
        in_specs=[row, vec, row],
        out_specs=[pl.BlockSpec((1, LANE), lambda i: (0, 0)), row, vec],
        out_shape=[jax.ShapeDtypeStruct((1, LANE), F32), jax.ShapeDtypeStruct((S, D), F32), jax.ShapeDtypeStruct((1, D), F32)],
        compiler_params=_cp(("arbitrary",)),
    )(x, w.reshape(1, D), target)
    return loss[0, 0], dx, dw.reshape(D)


SUB = 16
HG_W = HG_HEADS * LANE


def _hg_gates(q, f, lbv):
    sig = _sigmoid(f)
    fg = lbv + (1.0 - lbv) * sig
    sq = _sigmoid(q)
    return sig, fg, 1.0 - fg, sq, q * sq


def _hg_cumsum(logf):
    ri = lax.broadcasted_iota(I32, (CHUNK, CHUNK), 0)
    ci = lax.broadcasted_iota(I32, (CHUNK, CHUNK), 1)
    return _dot((ci <= ri).astype(F32), logf, NN, precision=lax.Precision.HIGHEST)


def _hg_rows():
    return lax.broadcasted_iota(I32, (CHUNK, LANE), 0)


def _hg_below(qf, kk, b, rows):
    blocks, parts = [jnp.zeros((SUB, CHUNK), F32)], []
    for i in range(1, CHUNK // SUB):
        bref = b[SUB * i - 1:SUB * i, :]
        rs = slice(SUB * i, SUB * (i + 1))
        eq = jnp.exp(b[rs] - bref)
        below = rows < SUB * i
        ek = jnp.exp(jnp.where(below, bref - b, NEG))
        qi = (qf[rs] * eq).astype(BF16)
        ki = (kk * ek).astype(BF16)
        blocks.append(_dot(qi, ki, NT))
        parts.append((qi, ki, eq, ek))
    return jnp.concatenate(blocks, axis=0), parts


def hgrn_fwd(proj, lb, nw):
    S = proj.shape[0]
    NC = S // CHUNK
    H = HG_HEADS

    def body(q_ref, f_ref, v_ref, g_ref, lb_ref, nw_ref, out_ref, st_out_ref, o_ref, st, kk_s, b_s):
        c = pl.program_id(0)

        @pl.when(c == 0)
        def _():
            st[...] = jnp.zeros_like(st)

        st_out_ref[...] = st[...]
        sig, fg, kk_all, sq, qf_all = _hg_gates(q_ref[...], f_ref[...], lb_ref[...])
        b_all = _hg_cumsum(jnp.log(fg))
        kk_s[...] = kk_all
        b_s[...] = b_all
        rows = _hg_rows()
        r16 = lax.broadcasted_iota(I32, (SUB, LANE), 0)
        for h in range(H):
            cs = slice(h * LANE, (h + 1) * LANE)
            qf, kk, b, v, g = qf_all[:, cs], kk_all[:, cs], b_all[:, cs], v_ref[:, cs], g_ref[:, cs]
            st_in = st[h]
            diag = []
            for i in range(CHUNK // SUB):
                rs = slice(SUB * i, SUB * (i + 1))
                acc = jnp.zeros((SUB, LANE), F32)
                for j in range(SUB):
                    row = pl.ds(SUB * i + j, 1)
                    e = jnp.exp(jnp.where(r16 >= j, b[rs] - b_s[row, cs], NEG))
                    col = jnp.sum(qf[rs] * (kk_s[row, cs] * e), axis=1, keepdims=True)
                    acc = acc + col * v_ref[row, cs]
                diag.append(acc)
            poff, _ = _hg_below(qf, kk, b, rows)
            vb = v.astype(BF16)
            bl = b[CHUNK - 1:CHUNK, :]
            o = (jnp.concatenate(diag, axis=0) + _dot(poff.astype(BF16), vb, NN)
                 + _dot((qf * jnp.exp(b)).astype(BF16), st_in.astype(BF16), NT))
            st[h] = st_in * jnp.exp(bl) + _dot(vb, (kk * jnp.exp(bl - b)).astype(BF16), TN)
            o_ref[:, cs] = o
            r = lax.rsqrt(jnp.mean(o * o, axis=-1, keepdims=True) + EPS)
            out_ref[:, cs] = (o * r * nw_ref[...] * (g * _sigmoid(g))).astype(out_ref.dtype)

    def seg(off):
        return pl.BlockSpec((CHUNK, HG_W), lambda c: (c, off // HG_W))

    blk = pl.BlockSpec((CHUNK, HG_W), lambda c: (c, 0))
    full = pltpu.VMEM((CHUNK, HG_W), F32)
    return pl.pallas_call(
        body, name="hgrn_fwd", grid=(NC,),
        in_specs=[seg(OFF_Q), seg(OFF_F), seg(OFF_V), seg(OFF_G),
                  pl.BlockSpec((1, HG_W), lambda c: (0, 0)), pl.BlockSpec((1, LANE), lambda c: (0, 0))],
        out_specs=[blk, pl.BlockSpec((None, H, LANE, LANE), lambda c: (c, 0, 0, 0)), blk],
        out_shape=[jax.ShapeDtypeStruct((S, 4 * HG_W), BF16), jax.ShapeDtypeStruct((NC, H, LANE, LANE), F32),
                   jax.ShapeDtypeStruct((S, HG_W), F32)],
        scratch_shapes=[pltpu.VMEM((H, LANE, LANE), F32), full, full],
        compiler_params=_cp(("arbitrary",)),
    )(proj, proj, proj, proj, lb.reshape(1, HG_W), nw.reshape(1, LANE))


def hgrn_bwd(proj, lb, nw, states, o_pre, dbcat, dproj):
    S = proj.shape[0]
    NC = S // CHUNK
    H = HG_HEADS

    def body(q_ref, f_ref, v_ref, g_ref, lb_ref, nw_ref, st_ref, o_ref, do_ref, _,
             dp_ref, dlb_ref, dnw_ref, dst, kk_s, b_s, do_s, db_s, dkk_s, dkk_d, dv_d):
        c = pl.program_id(0)
        dq_ref, df_ref, dv_ref, dg_ref = (dp_ref.at[:, pl.ds(off, HG_W)] for off in (OFF_Q, OFF_F, OFF_V, OFF_G))

        @pl.when(c == 0)
        def _():
            dst[...] = jnp.zeros_like(dst)
            dlb_ref[...] = jnp.zeros_like(dlb_ref)
            dnw_ref[...] = jnp.zeros_like(dnw_ref)

        q_all, g_all = q_ref[...], g_ref[...]
        lbv, nwv = lb_ref[...], nw_ref[...]
        sig, fg, kk_all, sq, qf_all = _hg_gates(q_all, f_ref[...], lbv)
        b_all = _hg_cumsum(jnp.log(fg))
        o_all = o_ref[...]
        dov = do_ref[...].astype(F32)
        sg = _sigmoid(g_all)
        gsg = g_all * sg
        dnw_acc = jnp.zeros((1, LANE), F32)
        for h in range(H):
            cs = slice(h * LANE, (h + 1) * LANE)
            o = o_all[:, cs]
            r = lax.rsqrt(jnp.mean(o * o, axis=-1, keepdims=True) + EPS)
            don = dov[:, cs] * gsg[:, cs]
            dnw_acc = dnw_acc + jnp.sum(don * o * r, axis=0, keepdims=True)
            gno = don * nwv
            do_s[:, cs] = r * gno - o * (r * r * r) * jnp.mean(o * gno, axis=-1, keepdims=True)
            dg_ref[:, cs] = (dov[:, cs] * (o * r * nwv) * _dsilu(g_all[:, cs], sg[:, cs])).astype(dg_ref.dtype)
        dnw_ref[...] += jnp.broadcast_to(dnw_acc, dnw_ref.shape)
        kk_s[...] = kk_all
        b_s[...] = b_all
        rows = _hg_rows()
        r16 = lax.broadcasted_iota(I32, (SUB, LANE), 0)
        for h in range(H):
            cs = slice(h * LANE, (h + 1) * LANE)
            qf, kk, b, v = qf_all[:, cs], kk_all[:, cs], b_all[:, cs], v_ref[:, cs]
            do = do_s[:, cs]
            st_in, dstv = st_ref[h], dst[h]
            bl = b[CHUNK - 1:CHUNK, :]
            eb, ebl, el = jnp.exp(b), jnp.exp(bl - b), jnp.exp(bl)
            qe, ke = qf * eb, kk * ebl
            vb, dob, stb, dstb = v.astype(BF16), do.astype(BF16), st_in.astype(BF16), dstv.astype(BF16)
            w_ = _dot(vb, dstb, NN)
            dqf = eb * _dot(dob, stb, NN)
            dkk = ebl * w_
            dv = _dot(ke.astype(BF16), dstb, NT)
            dbl = el * jnp.sum(st_in * dstv, axis=0, keepdims=True) + jnp.sum(ke * w_, axis=0, keepdims=True)
            dst[h] = dstv * el + _dot(dob, qe.astype(BF16), TN)
            poff, parts = _hg_below(qf, kk, b, rows)
            dpoff = _dot(dob, vb, NT).astype(BF16)
            dv = dv + _dot(poff.astype(BF16), dob, TN)
            dq_blocks = [jnp.zeros((SUB, LANE), F32)]
            for i, (qi, ki, eq, ek) in enumerate(parts, start=1):
                dpi = dpoff[SUB * i:SUB * (i + 1), :]
                dq_blocks.append(_dot(dpi, ki, NN) * eq)
                dkk = dkk + _dot(dpi, qi, TN) * ek
            dqf = dqf + jnp.concatenate(dq_blocks, axis=0)
            dq_diag = []
            for i in range(CHUNK // SUB):
                rs = slice(SUB * i, SUB * (i + 1))
                acc = jnp.zeros((SUB, LANE), F32)
                for j in range(SUB):
                    row = pl.ds(SUB * i + j, 1)
                    ks = kk_s[row, cs]
                    e = jnp.exp(jnp.where(r16 >= j, b[rs] - b_s[row, cs], NEG))
                    x = jnp.sum(do[rs] * v_ref[row, cs], axis=1, keepdims=True) * e
                    acc = acc + x * ks
                    dkk_d[row, cs] = jnp.sum(x * qf[rs], axis=0, keepdims=True)
                    col = jnp.sum(qf[rs] * (ks * e), axis=1, keepdims=True)
                    dv_d[row, cs] = jnp.sum(col * do[rs], axis=0, keepdims=True)
                dq_diag.append(acc)
            dqf = dqf + jnp.concatenate(dq_diag, axis=0)
            dkk = dkk + dkk_d[:, cs]
            dv = dv + dv_d[:, cs]
            dv_ref[:, cs] = dv.astype(dv_ref.dtype)
            db = qf * dqf - kk * dkk
            db_s[:, cs] = db + jnp.where(rows == CHUNK - 1, dbl, 0.0)
            dkk_s[:, cs] = dkk
            dq_ref[:, cs] = (dqf * _dsilu(q_all[:, cs], sq[:, cs])).astype(dq_ref.dtype)
        ri = lax.broadcasted_iota(I32, (CHUNK, CHUNK), 0)
        ci = lax.broadcasted_iota(I32, (CHUNK, CHUNK), 1)
        dlogf = _dot((ci >= ri).astype(F32), db_s[...], NN, precision=lax.Precision.HIGHEST)
        dfg = dlogf / fg - dkk_s[...]
        df_ref[...] = (dfg * (1.0 - lbv) * sig * (1.0 - sig)).astype(df_ref.dtype)
        dlb_ref[...] += jnp.broadcast_to(jnp.sum(dfg * (1.0 - sig), axis=0, keepdims=True), dlb_ref.shape)

    def seg(off):
        return pl.BlockSpec((CHUNK, HG_W), lambda c: (NC - 1 - c, off // HG_W))

    blk = pl.BlockSpec((CHUNK, HG_W), lambda c: (NC - 1 - c, 0))
    full = pltpu.VMEM((CHUNK, HG_W), F32)
    dproj, dlb, dnw = pl.pallas_call(
        body, name="hgrn_bwd", grid=(NC,),
        in_specs=[seg(OFF_Q), seg(OFF_F), seg(OFF_V), seg(OFF_G),
                  pl.BlockSpec((1, HG_W), lambda c: (0, 0)), pl.BlockSpec((1, LANE), lambda c: (0, 0)),
                  pl.BlockSpec((None, H, LANE, LANE), lambda c: (NC - 1 - c, 0, 0, 0)), blk, blk, ANY_SPACE],
        out_specs=[pl.BlockSpec((CHUNK, 4 * HG_W), lambda c: (NC - 1 - c, 0)),
                   pl.BlockSpec((8, HG_W), lambda c: (0, 0)), pl.BlockSpec((8, LANE), lambda c: (0, 0))],
        out_shape=[jax.ShapeDtypeStruct(dproj.shape, dproj.dtype), jax.ShapeDtypeStruct((8, HG_W), F32),
                   jax.ShapeDtypeStruct((8, LANE), F32)],
        input_output_aliases={9: 0},
        scratch_shapes=[pltpu.VMEM((H, LANE, LANE), F32)] + [full] * 7,
        compiler_params=_cp(("arbitrary",)),
    )(proj, proj, proj, proj, lb.reshape(1, HG_W), nw.reshape(1, LANE), states, o_pre, dbcat, dproj)
    return dproj, dlb[0], dnw[0]


CV_PAD = 32
ROWS = 256


def _colblk(S, off):
    return pl.BlockSpec((S, LANE), lambda j: (0, off // LANE + j))


def cv_fwd(proj, w32, bias):
    S = proj.shape[0]
    nchunk = S // ROWS

    def body(a_ref, g_ref, w_ref, b_ref, o_ref, zpad):
        zpad[pl.ds(0, CV_PAD), :] = jnp.zeros((CV_PAD, LANE), F32)

        def glu(c, _):
            r0 = pl.multiple_of(c * ROWS, ROWS)
            zpad[pl.ds(CV_PAD + r0, ROWS), :] = a_ref[pl.ds(r0, ROWS), :] * _sigmoid(g_ref[pl.ds(r0, ROWS), :])
            return 0

        lax.fori_loop(0, nchunk, glu, 0)

        def conv(c, _):
            r0 = pl.multiple_of(c * ROWS, ROWS)
            acc = jnp.broadcast_to(b_ref[...], (ROWS, LANE))
            for j in range(CV_KERNEL):
                acc = acc + w_ref[pl.ds(j, 1), :] * zpad[pl.ds(r0 + (CV_PAD - CV_KERNEL + 1) + j, ROWS), :]
            o_ref[pl.ds(r0, ROWS), :] = acc
            return 0

        lax.fori_loop(0, nchunk, conv, 0)

    return pl.pallas_call(
        body, name="cv_fwd", grid=(BRANCH_W // LANE,),
        in_specs=[_colblk(S, OFF_CV), _colblk(S, OFF_CV + BRANCH_W),
                  pl.BlockSpec((32, LANE), lambda j: (0, j)), pl.BlockSpec((1, LANE), lambda j: (0, j))],
        out_specs=pl.BlockSpec((S, LANE), lambda j: (0, j)), out_shape=jax.ShapeDtypeStruct((S, BRANCH_W), F32),
        scratch_shapes=[pltpu.VMEM((CV_PAD + S, LANE), F32)],
        compiler_params=_cp(("parallel",)),
    )(proj, proj, w32, bias.reshape(1, BRANCH_W))


def cv_bwd(proj, w32, dzc):
    S = proj.shape[0]
    nchunk = S // ROWS

    def body(a_ref, g_ref, w_ref, dz_ref, da_ref, dg_ref, dw_ref, db_ref, zpad, dpad):
        zpad[pl.ds(0, CV_PAD), :] = jnp.zeros((CV_PAD, LANE), F32)
        dpad[pl.ds(S, CV_PAD), :] = jnp.zeros((CV_PAD, LANE), F32)
        dw_ref[...] = jnp.zeros_like(dw_ref)

        def glu(c, dsum):
            r0 = pl.multiple_of(c * ROWS, ROWS)
            zpad[pl.ds(CV_PAD + r0, ROWS), :] = a_ref[pl.ds(r0, ROWS), :] * _sigmoid(g_ref[pl.ds(r0, ROWS), :])
            d = dz_ref[pl.ds(r0, ROWS), :]
            dpad[pl.ds(r0, ROWS), :] = d
            return dsum + jnp.sum(d, axis=0, keepdims=True)

        dsum = lax.fori_loop(0, nchunk, glu, jnp.zeros((1, LANE), F32))
        db_ref[...] = jnp.broadcast_to(dsum, db_ref.shape)

        def conv(c, _):
            r0 = pl.multiple_of(c * ROWS, ROWS)
            d = dpad[pl.ds(r0, ROWS), :]
            acc = jnp.zeros((ROWS, LANE), F32)
            for j in range(CV_KERNEL):
                acc = acc + w_ref[pl.ds(j, 1), :] * dpad[pl.ds(r0 + (CV_KERNEL - 1) - j, ROWS), :]
                zs = zpad[pl.ds(r0 + (CV_PAD - CV_KERNEL + 1) + j, ROWS), :]
                dw_ref[pl.ds(j, 1), :] += jnp.sum(d * zs, axis=0, keepdims=True)
            a = a_ref[pl.ds(r0, ROWS), :]
            sg = _sigmoid(g_ref[pl.ds(r0, ROWS), :])
            da_ref[pl.ds(r0, ROWS), :] = (acc * sg).astype(da_ref.dtype)
            dg_ref[pl.ds(r0, ROWS), :] = (acc * a * sg * (1.0 - sg)).astype(dg_ref.dtype)
            return 0

        lax.fori_loop(0, nchunk, conv, 0)

    blk = pl.BlockSpec((S, LANE), lambda j: (0, j))
    da, dg, dw, db = pl.pallas_call(
        body, name="cv_bwd", grid=(BRANCH_W // LANE,),
        in_specs=[_colblk(S, OFF_CV), _colblk(S, OFF_CV + BRANCH_W), pl.BlockSpec((32, LANE), lambda j: (0, j)), blk],
        out_specs=[blk, blk, pl.BlockSpec((32, LANE), lambda j: (0, j)), pl.BlockSpec((8, LANE), lambda j: (0, j))],
        out_shape=[jax.ShapeDtypeStruct((S, BRANCH_W), BF16), jax.ShapeDtypeStruct((S, BRANCH_W), BF16),
                   jax.ShapeDtypeStruct((32, BRANCH_W), F32), jax.ShapeDtypeStruct((8, BRANCH_W), F32)],
        scratch_shapes=[pltpu.VMEM((CV_PAD + S, LANE), F32), pltpu.VMEM((S + CV_PAD, LANE), F32)],
        compiler_params=_cp(("parallel",)),
    )(proj, proj, w32, dzc)
    return da, dg, dw, db[0]


def ln_silu_fwd(z, w, b, bcat, tm=512):
    S, C = z.shape
    tm = _tile(S, tm)

    def body(z_ref, w_ref, b_ref, _, o_ref):
        zv = z_ref[...]
        mu = jnp.mean(zv, axis=-1, keepdims=True)
        zc = zv - mu
        rstd = lax.rsqrt(jnp.mean(zc * zc, axis=-1, keepdims=True) + EPS)
        y = zc * rstd * w_ref[...] + b_ref[...]
        o_ref[...] = (y * _sigmoid(y)).astype(o_ref.dtype)

    row = pl.BlockSpec((tm, C), lambda i: (i, 0))
    vec = pl.BlockSpec((1, C), lambda i: (0, 0))
    return pl.pallas_call(
        body, name="ln_silu_fwd", grid=(S // tm,), in_specs=[row, vec, vec, ANY_SPACE],
        out_specs=pl.BlockSpec((tm, C), lambda i: (i, 1)), out_shape=jax.ShapeDtypeStruct(bcat.shape, bcat.dtype),
        input_output_aliases={3: 0}, compiler_params=_cp(("parallel",)),
    )(z, w.reshape(1, C), b.reshape(1, C), bcat)


def ln_silu_bwd(z, w, b, dbcat, tm=512):
    S, C = z.shape
    tm = _tile(S, tm)

    def body(z_ref, w_ref, b_ref, do_ref, dz_ref, dw_ref, db_ref):
        zv = z_ref[...]
        wv = w_ref[...]
        mu = jnp.mean(zv, axis=-1, keepdims=True)
        zc = zv - mu
        rstd = lax.rsqrt(jnp.mean(zc * zc, axis=-1, keepdims=True) + EPS)
        xh = zc * rstd
        y = xh * wv + b_ref[...]
        dy = do_ref[...].astype(F32) * _dsilu(y, _sigmoid(y))

        @pl.when(pl.program_id(0) == 0)
        def _():
            dw_ref[...] = jnp.zeros_like(dw_ref)
            db_ref[...] = jnp.zeros_like(db_ref)

        dw_ref[...] += jnp.sum(dy * xh, axis=0, keepdims=True)
        db_ref[...] += jnp.sum(dy, axis=0, keepdims=True)
        dxh = dy * wv
        dz_ref[...] = rstd * (dxh - jnp.mean(dxh, axis=-1, keepdims=True) - xh * jnp.mean(dxh * xh, axis=-1, keepdims=True))

    row = pl.BlockSpec((tm, C), lambda i: (i, 0))
    vec = pl.BlockSpec((1, C), lambda i: (0, 0))
    dz, dw, db = pl.pallas_call(
        body, name="ln_silu_bwd", grid=(S // tm,), in_specs=[row, vec, vec, pl.BlockSpec((tm, C), lambda i: (i, 1))],
        out_specs=[row, vec, vec],
        out_shape=[jax.ShapeDtypeStruct((S, C), F32), jax.ShapeDtypeStruct((1, C), F32), jax.ShapeDtypeStruct((1, C), F32)],
        compiler_params=_cp(("arbitrary",)),
    )(z, w.reshape(1, C), b.reshape(1, C), dbcat)
    return dz, dw.reshape(C), db.reshape(C)


PL_PAD = 16


def _pool_counts(r0, win):
    t = r0 + lax.broadcasted_iota(I32, (ROWS, LANE), 0)
    return jnp.minimum(t + 1, win).astype(F32)


def pool_fwd(proj, wg, scale, bcat):
    S = proj.shape[0]
    nchunk = S // ROWS

    def body(u_ref, w_ref, s_ref, _, o_ref, upad):
        g = pl.program_id(0)
        upad[pl.ds(0, PL_PAD), :] = jnp.zeros((PL_PAD, LANE), F32)

        def fill(c, _):
            r0 = pl.multiple_of(c * ROWS, ROWS)
            upad[pl.ds(PL_PAD + r0, ROWS), :] = u_ref[pl.ds(r0, ROWS), :]
            return 0

        lax.fori_loop(0, nchunk, fill, 0)
        wb = w_ref[...].astype(BF16)
        for gi, win in enumerate(POOL_WINDOWS):
            @pl.when(g == gi)
            def _(win=win):
                def chunk(c, _):
                    r0 = pl.multiple_of(c * ROWS, ROWS)
                    u = upad[pl.ds(PL_PAD + r0, ROWS), :]
                    ws = u
                    for j in range(1, win):
                        ws = ws + upad[pl.ds(PL_PAD + r0 - j, ROWS), :]
                    pooled = ws / _pool_counts(r0, win) - u
                    o_ref[pl.ds(r0, ROWS), :] = (_dot(pooled.astype(BF16), wb, NN) * s_ref[...]).astype(o_ref.dtype)
                    return 0

                lax.fori_loop(0, nchunk, chunk, 0)

    return pl.pallas_call(
        body, name="pool_fwd", grid=(len(POOL_WINDOWS),),
        in_specs=[_colblk(S, OFF_PL), pl.BlockSpec((None, LANE, LANE), lambda j: (j, 0, 0)), pl.BlockSpec((1, LANE), lambda j: (0, j)),
                  ANY_SPACE],
        out_specs=pl.BlockSpec((S, LANE), lambda j: (0, 2 * BRANCH_W // LANE + j)),
        out_shape=jax.ShapeDtypeStruct(bcat.shape, bcat.dtype), input_output_aliases={3: 0},
        scratch_shapes=[pltpu.VMEM((PL_PAD + S, LANE), F32)],
        compiler_params=_cp(("parallel",)),
    )(proj, wg, scale.reshape(1, BRANCH_W), bcat)


def pool_bwd(proj, wg, scale, dbcat, dproj):
    S = proj.shape[0]
    nchunk = S // ROWS

    def body(u_ref, w_ref, s_ref, dy_ref, _, du_ref, dw_ref, ds_ref, upad, dpn, nd):
        g = pl.program_id(0)
        upad[pl.ds(0, PL_PAD), :] = jnp.zeros((PL_PAD, LANE), F32)
        dpn[pl.ds(S, PL_PAD), :] = jnp.zeros((PL_PAD, LANE), F32)

        def fill(c, _):
            r0 = pl.multiple_of(c * ROWS, ROWS)
            upad[pl.ds(PL_PAD + r0, ROWS), :] = u_ref[pl.ds(r0, ROWS), :]
            return 0

        lax.fori_loop(0, nchunk, fill, 0)
        wb = w_ref[...].astype(BF16)
        sv = s_ref[...]
        for gi, win in enumerate(POOL_WINDOWS):
            @pl.when(g == gi)
            def _(win=win):
                def chunk(c, carry):
                    dw, dsc = carry
                    r0 = pl.multiple_of(c * ROWS, ROWS)
                    u = upad[pl.ds(PL_PAD + r0, ROWS), :]
                    ws = u
                    for j in range(1, win):
                        ws = ws + upad[pl.ds(PL_PAD + r0 - j, ROWS), :]
                    cnt = _pool_counts(r0, win)
                    pooled = (ws / cnt - u).astype(BF16)
                    dyv = dy_ref[pl.ds(r0, ROWS), :].astype(F32)
                    dsc = dsc + jnp.sum(dyv * _dot(pooled, wb, NN), axis=0, keepdims=True)
                    dys = (dyv * sv).astype(BF16)
                    dw = dw + _dot(pooled, dys, TN)
                    dp = _dot(dys, wb, NT)
                    dpn[pl.ds(r0, ROWS), :] = dp / cnt
                    nd[pl.ds(r0, ROWS), :] = -dp
                    return dw, dsc

                dw, dsc = lax.fori_loop(0, nchunk, chunk, (jnp.zeros((LANE, LANE), F32), jnp.zeros((1, LANE), F32)))
                dw_ref[...] = dw
                ds_ref[...] = jnp.broadcast_to(dsc, ds_ref.shape)

                def spread(c, _):
                    r0 = pl.multiple_of(c * ROWS, ROWS)
                    acc = nd[pl.ds(r0, ROWS), :]
                    for j in range(win):
                        acc = acc + dpn[pl.ds(r0 + j, ROWS), :]
                    du_ref[pl.ds(r0, ROWS), :] = acc.astype(du_ref.dtype)
                    return 0

                lax.fori_loop(0, nchunk, spread, 0)

    dproj, dw, ds = pl.pallas_call(
        body, name="pool_bwd", grid=(len(POOL_WINDOWS),),
        in_specs=[_colblk(S, OFF_PL), pl.BlockSpec((None, LANE, LANE), lambda j: (j, 0, 0)), pl.BlockSpec((1, LANE), lambda j: (0, j)),
                  pl.BlockSpec((S, LANE), lambda j: (0, 2 * BRANCH_W // LANE + j)), ANY_SPACE],
        out_specs=[_colblk(S, OFF_PL), pl.BlockSpec((None, LANE, LANE), lambda j: (j, 0, 0)), pl.BlockSpec((8, LANE), lambda j: (0, j))],
        out_shape=[jax.ShapeDtypeStruct(dproj.shape, dproj.dtype), jax.ShapeDtypeStruct((len(POOL_WINDOWS), LANE, LANE), F32),
                   jax.ShapeDtypeStruct((8, BRANCH_W), F32)],
        input_output_aliases={4: 0},
        scratch_shapes=[pltpu.VMEM((PL_PAD + S, LANE), F32), pltpu.VMEM((S + PL_PAD, LANE), F32), pltpu.VMEM((S, LANE), F32)],
        compiler_params=_cp(("parallel",)),
    )(proj, wg, scale.reshape(1, BRANCH_W), dbcat, dproj)
    return dproj, dw, ds[0]


LR_PAD = 8
SCAN_TILES = 4
GELU_C = math.sqrt(2.0 / math.pi)
GELU_A = 0.044715


def _gelu(y):
    return 0.5 * y * (1.0 + jnp.tanh(GELU_C * (y + GELU_A * y * y * y)))


def _dgelu(y):
    t = jnp.tanh(GELU_C * (y + GELU_A * y * y * y))
    return 0.5 * (1.0 + t) + 0.5 * y * (1.0 - t * t) * GELU_C * (1.0 + 3.0 * GELU_A * y * y)


def _lru_gates(xpad, r0, cw_ref, cb, wa, ba, wx, bx, sp8):
    xc = jnp.broadcast_to(cb, (ROWS, LANE))
    for j in range(LRU_CONV):
        xc = xc + cw_ref[pl.ds(j, 1), :] * xpad[pl.ds(r0 + (LR_PAD - LRU_CONV + 1) + j, ROWS), :]
    xb = xc.astype(BF16)
    r = _sigmoid(_dot(xb, wa, NN) + ba)
    ig = _sigmoid(_dot(xb, wx, NN) + bx)
    la = -sp8 * r
    a = jnp.exp(la)
    s = jnp.sqrt(-jnp.tanh(la) * (a * a + 1.0))
    return xc, r, ig, a, s


def _tile_scan(a, b, r8, up):
    for s in (1, 2, 4):
        keep = (r8 < 8 - s) if up else (r8 >= s)
        shift = 8 - s if up else s
        a_sh = jnp.where(keep, pltpu.roll(a, shift, 0), 1.0)
        b_sh = jnp.where(keep, pltpu.roll(b, shift, 0), 0.0)
        b = b + a * b_sh
        a = a * a_sh
    return a, b


def lru_fwd(proj, cw8, cb, wa_bd, ba, wx_bd, bx, sp8, bcat):
    S = proj.shape[0]
    nchunk = S // ROWS

    def body(x_ref, y_ref, cw_ref, cb_ref, wa_ref, ba_ref, wx_ref, bx_ref, sp_ref, _, o_ref, h_ref, xpad, a_s):
        xpad[pl.ds(0, LR_PAD), :] = jnp.zeros((LR_PAD, LANE), F32)

        def fill(c, _):
            r0 = pl.multiple_of(c * ROWS, ROWS)
            xpad[pl.ds(LR_PAD + r0, ROWS), :] = x_ref[pl.ds(r0, ROWS), :]
            return 0

        lax.fori_loop(0, nchunk, fill, 0)
        wa = wa_ref[...].astype(BF16)
        wx = wx_ref[...].astype(BF16)

        def gates(c, _):
            r0 = pl.multiple_of(c * ROWS, ROWS)
            xc, r, ig, a, s = _lru_gates(xpad, r0, cw_ref, cb_ref[...], wa, ba_ref[...], wx, bx_ref[...], sp_ref[...])
            a_s[pl.ds(r0, ROWS), :] = a
            h_ref[pl.ds(r0, ROWS), :] = s * (ig * xc)
            return 0

        lax.fori_loop(0, nchunk, gates, 0)

        r8 = lax.broadcasted_iota(I32, (8, LANE), 0)

        def scan(i, h):
            bases = [pl.multiple_of(i * (8 * SCAN_TILES) + 8 * j, 8) for j in range(SCAN_TILES)]
            maps = [_tile_scan(a_s[pl.ds(b, 8), :], h_ref[pl.ds(b, 8), :], r8, False) for b in bases]
            for b, (ca, cb_) in zip(bases, maps):
                out = cb_ + ca * h
                h_ref[pl.ds(b, 8), :] = out
                h = out[7:8, :]
            return h

        lax.fori_loop(0, S // (8 * SCAN_TILES), scan, jnp.zeros((1, LANE), F32))

        def gate_out(c, _):
            r0 = pl.multiple_of(c * ROWS, ROWS)
            o_ref[pl.ds(r0, ROWS), :] = (h_ref[pl.ds(r0, ROWS), :] * _gelu(y_ref[pl.ds(r0, ROWS), :])).astype(o_ref.dtype)
            return 0

        lax.fori_loop(0, nchunk, gate_out, 0)

    vec = pl.BlockSpec((1, LANE), lambda j: (0, j))
    mat = pl.BlockSpec((None, LANE, LANE), lambda j: (j, 0, 0))
    blk = pl.BlockSpec((S, LANE), lambda j: (0, j))
    return pl.pallas_call(
        body, name="lru_fwd", grid=(BRANCH_W // LANE,),
        in_specs=[_colblk(S, OFF_LX), _colblk(S, OFF_LY), pl.BlockSpec((8, LANE), lambda j: (0, j)), vec, mat, vec, mat, vec, vec,
                  ANY_SPACE],
        out_specs=[pl.BlockSpec((S, LANE), lambda j: (0, 3 * BRANCH_W // LANE + j)), blk],
        out_shape=[jax.ShapeDtypeStruct(bcat.shape, bcat.dtype), jax.ShapeDtypeStruct((S, BRANCH_W), F32)],
        input_output_aliases={9: 0},
        scratch_shapes=[pltpu.VMEM((LR_PAD + S, LANE), F32), pltpu.VMEM((S, LANE), F32)],
        compiler_params=_cp(("parallel",)),
    )(proj, proj, cw8, cb.reshape(1, -1), wa_bd, ba.reshape(1, -1), wx_bd, bx.reshape(1, -1), sp8.reshape(1, -1), bcat)


def lru_bwd(proj, cw8, cb, wa_bd, ba, wx_bd, bx, sp8, h, dbcat):
    S = proj.shape[0]
    nchunk = S // ROWS

    def body(x_ref, y_ref, cw_ref, cb_ref, wa_ref, ba_ref, wx_ref, bx_ref, sp_ref, h_ref, do_ref,
             dx_ref, dy_ref, dcw_ref, dcb_ref, dwa_ref, dba_ref, dwx_ref, dbx_ref, dsp_ref,
             xpad, a_s, g_s, hpad, dxc):
        xpad[pl.ds(0, LR_PAD), :] = jnp.zeros((LR_PAD, LANE), F32)
        hpad[pl.ds(0, LR_PAD), :] = jnp.zeros((LR_PAD, LANE), F32)
        dxc[pl.ds(S, LR_PAD), :] = jnp.zeros((LR_PAD, LANE), F32)
        dcw_ref[...] = jnp.zeros_like(dcw_ref)
        wa = wa_ref[...].astype(BF16)
        wx = wx_ref[...].astype(BF16)
        cbv, bav, bxv, spv = cb_ref[...], ba_ref[...], bx_ref[...], sp_ref[...]

        def fill(c, _):
            r0 = pl.multiple_of(c * ROWS, ROWS)
            xpad[pl.ds(LR_PAD + r0, ROWS), :] = x_ref[pl.ds(r0, ROWS), :]
            hv = h_ref[pl.ds(r0, ROWS), :]
            hpad[pl.ds(LR_PAD + r0, ROWS), :] = hv
            yv = y_ref[pl.ds(r0, ROWS), :]
            dov = do_ref[pl.ds(r0, ROWS), :].astype(F32)
            g_s[pl.ds(r0, ROWS), :] = dov * _gelu(yv)
            dy_ref[pl.ds(r0, ROWS), :] = (dov * hv * _dgelu(yv)).astype(dy_ref.dtype)
            return 0

        lax.fori_loop(0, nchunk, fill, 0)

        def gates(c, _):
            r0 = pl.multiple_of(c * ROWS, ROWS)
            _, _, _, a, _ = _lru_gates(xpad, r0, cw_ref, cbv, wa, bav, wx, bxv, spv)
            a_s[pl.ds(r0, ROWS), :] = a
            return 0

        lax.fori_loop(0, nchunk, gates, 0)

        r8 = lax.broadcasted_iota(I32, (8, LANE), 0)

        def rscan(i, carry):
            bases = [pl.multiple_of(S - 8 - i * (8 * SCAN_TILES) - 8 * j, 8) for j in range(SCAN_TILES)]
            firsts, maps = [], []
            for b in bases:
                a8 = a_s[pl.ds(b, 8), :]
                above = jnp.where(r8 < 7, pltpu.roll(a8, 7, 0), 1.0)
                firsts.append(a8[0:1, :])
                maps.append(_tile_scan(above, g_s[pl.ds(b, 8), :], r8, True))
            for b, a0, (ca, cb_) in zip(bases, firsts, maps):
                out = cb_ + ca * carry
                g_s[pl.ds(b, 8), :] = out
                carry = a0 * out[0:1, :]
            return carry

        lax.fori_loop(0, S // (8 * SCAN_TILES), rscan, jnp.zeros((1, LANE), F32))

        def chain(c, carry):
            dwa, dwx, dba, dbx, dsp, dcb = carry
            r0 = pl.multiple_of(c * ROWS, ROWS)
            xc, r, ig, a, s = _lru_gates(xpad, r0, cw_ref, cbv, wa, bav, wx, bxv, spv)
            gt = g_s[pl.ds(r0, ROWS), :]
            hprev = hpad[pl.ds(r0 + LR_PAD - 1, ROWS), :]
            da = gt * hprev - gt * ig * xc * (a / s)
            dig = gt * s * xc
            dla = da * a
            dsp = dsp + jnp.sum(-dla * r, axis=0, keepdims=True)
            dpr = (-dla * spv) * r * (1.0 - r)
            dpi = dig * ig * (1.0 - ig)
            dprb, dpib, xb = dpr.astype(BF16), dpi.astype(BF16), xc.astype(BF16)
            d = gt * s * ig + _dot(dprb, wa, NT) + _dot(dpib, wx, NT)
            dwa = dwa + _dot(xb, dprb, TN)
            dwx = dwx + _dot(xb, dpib, TN)
            dba = dba + jnp.sum(dpr, axis=0, keepdims=True)
            dbx = dbx + jnp.sum(dpi, axis=0, keepdims=True)
            dcb = dcb + jnp.sum(d, axis=0, keepdims=True)
            dxc[pl.ds(r0, ROWS), :] = d
            for j in range(LRU_CONV):
                xs = xpad[pl.ds(r0 + (LR_PAD - LRU_CONV + 1) + j, ROWS), :]
                dcw_ref[pl.ds(j, 1), :] += jnp.sum(d * xs, axis=0, keepdims=True)
            return dwa, dwx, dba, dbx, dsp, dcb

        zm, zv = jnp.zeros((LANE, LANE), F32), jnp.zeros((1, LANE), F32)
        dwa, dwx, dba, dbx, dsp, dcb = lax.fori_loop(0, nchunk, chain, (zm, zm, zv, zv, zv, zv))
        dwa_ref[...] = dwa
        dwx_ref[...] = dwx
        dba_ref[...] = jnp.broadcast_to(dba, dba_ref.shape)
        dbx_ref[...] = jnp.broadcast_to(dbx, dbx_ref.shape)
        dsp_ref[...] = jnp.broadcast_to(dsp, dsp_ref.shape)
        dcb_ref[...] = jnp.broadcast_to(dcb, dcb_ref.shape)

        def convt(c, _):
            r0 = pl.multiple_of(c * ROWS, ROWS)
            acc = jnp.zeros((ROWS, LANE), F32)
            for j in range(LRU_CONV):
                acc = acc + cw_ref[pl.ds(j, 1), :] * dxc[pl.ds(r0 + (LRU_CONV - 1) - j, ROWS), :]
            dx_ref[pl.ds(r0, ROWS), :] = acc.astype(dx_ref.dtype)
            return 0

        lax.fori_loop(0, nchunk, convt, 0)

    vec = pl.BlockSpec((1, LANE), lambda j: (0, j))
    vec8 = pl.BlockSpec((8, LANE), lambda j: (0, j))
    mat = pl.BlockSpec((None, LANE, LANE), lambda j: (j, 0, 0))
    blk = pl.BlockSpec((S, LANE), lambda j: (0, j))
    nblk = BRANCH_W // LANE
    v8 = jax.ShapeDtypeStruct((8, BRANCH_W), F32)
    m4 = jax.ShapeDtypeStruct((nblk, LANE, LANE), F32)
    big = jax.ShapeDtypeStruct((S, BRANCH_W), BF16)
    seq = pltpu.VMEM((S, LANE), F32)
    dx, dy, dcw, dcb, dwa, dba, dwx, dbx, dsp = pl.pallas_call(
        body, name="lru_bwd", grid=(nblk,),
        in_specs=[_colblk(S, OFF_LX), _colblk(S, OFF_LY), vec8, vec, mat, vec, mat, vec, vec, blk,
                  pl.BlockSpec((S, LANE), lambda j: (0, 3 * BRANCH_W // LANE + j))],
        out_specs=[blk, blk, vec8, vec8, mat, vec8, mat, vec8, vec8],
        out_shape=[big, big, v8, v8, m4, v8, m4, v8, v8],
        scratch_shapes=[pltpu.VMEM((LR_PAD + S, LANE), F32), seq, seq, pltpu.VMEM((LR_PAD + S, LANE), F32),
                        pltpu.VMEM((S + LR_PAD, LANE), F32)],
        compiler_params=_cp(("parallel",)),
    )(proj, proj, cw8, cb.reshape(1, -1), wa_bd, ba.reshape(1, -1), wx_bd, bx.reshape(1, -1), sp8.reshape(1, -1), h, dbcat)
    return dx, dy, dcw, dcb[0], dwa, dba[0], dwx, dbx[0], dsp[0]


MG_COLS = 512
N_BRANCH = 4
BCAT_W = N_BRANCH * BRANCH_W


def _gate_spec(tm, k):
    return pl.BlockSpec((tm, MG_COLS), lambda j, i: (i, (OFF_GATE + k * D_MODEL) // MG_COLS + j))


def _up_spec(tm, k):
    return pl.BlockSpec((tm, MG_COLS), lambda j, i: (i, k * D_MODEL // MG_COLS + j))


def merge_fwd(ups, proj, gate_b, tm=512):
    S = proj.shape[0]
    tm = _tile(S, tm)

    def body(u0, u1, u2, u3, g0, g1, g2, g3, gb_ref, o_ref):
        acc = jnp.zeros((tm, MG_COLS), F32)
        for k, (u, g) in enumerate(((u0, g0), (u1, g1), (u2, g2), (u3, g3))):
            acc = acc + _sigmoid(g[...] + gb_ref[pl.ds(k, 1), :]) * u[...]
        o_ref[...] = acc.astype(o_ref.dtype)

    blk = pl.BlockSpec((tm, MG_COLS), lambda j, i: (i, j))
    return pl.pallas_call(
        body, name="merge_fwd", grid=(D_MODEL // MG_COLS, S // tm),
        in_specs=[_up_spec(tm, k) for k in range(N_BRANCH)] + [_gate_spec(tm, k) for k in range(N_BRANCH)]
        + [pl.BlockSpec((N_BRANCH, MG_COLS), lambda j, i: (0, j))],
        out_specs=blk, out_shape=jax.ShapeDtypeStruct((S, D_MODEL), BF16),
        compiler_params=_cp(("parallel", "parallel")),
    )(ups, ups, ups, ups, proj, proj, proj, proj, gate_b)


def merge_bwd(dmerged, ups, proj, gate_b, tm=1024):
    S = proj.shape[0]
    tm = _tile(S, tm)
    halves = D_MODEL // MG_COLS

    def body(dm_ref, u_ref, g_ref, gb_ref, du_ref, dg_ref, dgb_ref):
        k = pl.program_id(0)

        @pl.when(pl.program_id(2) == 0)
        def _():
            dgb_ref[...] = jnp.zeros_like(dgb_ref)

        dm = dm_ref[...].astype(F32)
        sg = _sigmoid(g_ref[...] + gb_ref[pl.ds(k, 1), :])
        du_ref[...] = (dm * sg).astype(du_ref.dtype)
        dgk = dm * u_ref[...] * sg * (1.0 - sg)
        dg_ref[...] = dgk.astype(dg_ref.dtype)
        dgb_ref[...] += jnp.broadcast_to(jnp.sum(dgk, axis=0, keepdims=True), dgb_ref.shape)

    dups, dproj, dgb = pl.pallas_call(
        body, name="merge_bwd", grid=(N_BRANCH, halves, S // tm),
        in_specs=[pl.BlockSpec((tm, MG_COLS), lambda k, j, i: (i, j)),
                  pl.BlockSpec((tm, MG_COLS), lambda k, j, i: (i, k * halves + j)),
                  pl.BlockSpec((tm, MG_COLS), lambda k, j, i: (i, OFF_GATE // MG_COLS + k * halves + j)),
                  pl.BlockSpec((N_BRANCH, MG_COLS), lambda k, j, i: (0, j))],
        out_specs=[pl.BlockSpec((tm, MG_COLS), lambda k, j, i: (i, k * halves + j)),
                   pl.BlockSpec((tm, MG_COLS), lambda k, j, i: (i, OFF_GATE // MG_COLS + k * halves + j)),
                   pl.BlockSpec((8, MG_COLS), lambda k, j, i: (k, j))],
        out_shape=[jax.ShapeDtypeStruct((S, N_BRANCH * D_MODEL), BF16), jax.ShapeDtypeStruct((S, IN_W), BF16),
                   jax.ShapeDtypeStruct((8 * N_BRANCH, D_MODEL), F32)],
        compiler_params=_cp(("parallel", "parallel", "arbitrary")),
    )(dmerged, ups, proj, gate_b)
    return dups, dproj, dgb.reshape(N_BRANCH, 8, D_MODEL)[:, 0]


def attn_fwd(q, kv, tm=512):
    S = q.shape[0]
    M = kv.shape[0]
    tm = _tile(S, tm)
    scale = XA_HD ** -0.5

    def body(q_ref, kv_ref, o_ref):
        for hh in range(XA_HEADS):
            cs = pl.ds(hh * XA_HD, XA_HD)
            qh = q_ref[:, cs]
            kh = kv_ref[:, cs]
            vh = kv_ref[:, pl.ds(D_MODEL + hh * XA_HD, XA_HD)]
            s = _dot(qh, kh, NT) * scale
            p = jnp.exp(s - jnp.max(s, axis=-1, keepdims=True))
            p = p / jnp.sum(p, axis=-1, keepdims=True)
            o_ref[:, cs] = _dot(p.astype(BF16), vh, NN).astype(o_ref.dtype)

    return pl.pallas_call(
        body, name="attn_fwd", grid=(S // tm,),
        in_specs=[pl.BlockSpec((tm, D_MODEL), lambda i: (i, 0)), pl.BlockSpec((M, 2 * D_MODEL), lambda i: (0, 0))],
        out_specs=pl.BlockSpec((tm, D_MODEL), lambda i: (i, 0)), out_shape=jax.ShapeDtypeStruct((S, D_MODEL), BF16),
        compiler_params=_cp(("parallel",)),
    )(q, kv)


def attn_bwd(q, kv, do, tm=512):
    S = q.shape[0]
    M = kv.shape[0]
    tm = _tile(S, tm)
    scale = XA_HD ** -0.5

    def body(q_ref, kv_ref, do_ref, dq_ref, dkv_ref):
        @pl.when(pl.program_id(0) == 0)
        def _():
            dkv_ref[...] = jnp.zeros_like(dkv_ref)

        for hh in range(XA_HEADS):
            cs = pl.ds(hh * XA_HD, XA_HD)
            vs = pl.ds(D_MODEL + hh * XA_HD, XA_HD)
            qh = q_ref[:, cs]
            kh = kv_ref[:, cs]
            vh = kv_ref[:, vs]
            doh = do_ref[:, cs]
            s = _dot(qh, kh, NT) * scale
            p = jnp.exp(s - jnp.max(s, axis=-1, keepdims=True))
            p = p / jnp.sum(p, axis=-1, keepdims=True)
            dp = _dot(doh, vh, NT)
            ds = (p * (dp - jnp.sum(dp * p, axis=-1, keepdims=True)) * scale).astype(BF16)
            dq_ref[:, cs] = _dot(ds, kh, NN).astype(dq_ref.dtype)
            dkv_ref[:, cs] += _dot(ds, qh, TN)
            dkv_ref[:, vs] += _dot(p.astype(BF16), doh, TN)

    row = pl.BlockSpec((tm, D_MODEL), lambda i: (i, 0))
    full = pl.BlockSpec((M, 2 * D_MODEL), lambda i: (0, 0))
    return pl.pallas_call(
        body, name="attn_bwd", grid=(S // tm,), in_specs=[row, full, row], out_specs=[row, full],
        out_shape=[jax.ShapeDtypeStruct((S, D_MODEL), BF16), jax.ShapeDtypeStruct((M, 2 * D_MODEL), F32)],
        compiler_params=_cp(("arbitrary",)),
    )(q, kv, do)


def sum_parts(parts, own=None, tm=256):
    n, R, C = parts.shape
    tm = _tile(R, tm)
    has_own = own is not None

    def body(*refs):
        p_ref, o_ref = refs[0], refs[-1]
        acc = refs[1][...].astype(F32) if has_own else p_ref[0].astype(F32)
        for j in range(0 if has_own else 1, n):
            acc = acc + p_ref[j].astype(F32)
        o_ref[...] = acc

    row = pl.BlockSpec((tm, C), lambda i: (i, 0))
    return pl.pallas_call(
        body, name="sum_parts", grid=(R // tm,),
        in_specs=[pl.BlockSpec((n, tm, C), lambda i: (0, i, 0))] + ([row] if has_own else []), out_specs=row,
        out_shape=jax.ShapeDtypeStruct((R, C), F32), compiler_params=_cp(("parallel",)),
    )(*([parts, own] if has_own else [parts]))


def adamw(w, g, m, v, tm=256):
    R, C = w.shape
    tm = _tile(R, tm)
    c1 = 1.0 / (1.0 - ADAM_B1 ** ADAM_STEP)
    c2 = 1.0 / (1.0 - ADAM_B2 ** ADAM_STEP)

    def body(w_ref, g_ref, m_ref, v_ref, d_ref, nm_ref, nv_ref):
        gv = g_ref[...]
        nm = ADAM_B1 * m_ref[...] + (1.0 - ADAM_B1) * gv
        nv = ADAM_B2 * v_ref[...] + (1.0 - ADAM_B2) * (gv * gv)
        nm_ref[...] = nm
        nv_ref[...] = nv
        d_ref[...] = -ADAM_LR * ((nm * c1) / (jnp.sqrt(nv * c2) + ADAM_EPS) + ADAM_WD * w_ref[...])

    blk = pl.BlockSpec((tm, C), lambda i: (i, 0))
    sd = jax.ShapeDtypeStruct((R, C), F32)
    return pl.pallas_call(
        body, name="adamw", grid=(R // tm,), in_specs=[blk] * 4, out_specs=[blk] * 3, out_shape=[sd] * 3,
        compiler_params=_cp(("parallel",)),
    )(w, g, m, v)


ANY = pl.BlockSpec(memory_space=pl.ANY)


def _place():
    return lax.axis_index("x"), lax.axis_index("y"), lax.axis_index("c")


def _slot(px, py, pc):
    return 4 * px + 2 * py + pc


def all_gather(name, shards, after=()):
    n = len(shards)
    n_in = n + len(after)

    def body(*refs):
        x_refs, out_refs = refs[:n], refs[n_in:n_in + n]
        send_sems, recv_sems, local_sems = refs[n_in + n:]
        x, y, c = _place()
        me, sibling = (x, y, c), (x, y, 1 - c)
        chips = [(1 - x, y), (x, 1 - y), (1 - x, 1 - y)]

        def copy(a, k, block, to, src=None):
            rows = out_refs[a].at[_slot(*block)]
            return pltpu.make_async_remote_copy(
                src_ref=rows if src is None else src, dst_ref=rows,
                send_sem=send_sems.at[7 * a + k], recv_sem=recv_sems.at[7 * a + k],
                device_id=to, device_id_type=MESH)

        mine = [pltpu.make_async_copy(x_refs[a], out_refs[a].at[_slot(*me)], local_sems.at[a]) for a in range(n)]
        for cp in mine:
            cp.start()
        first = []
        for a in range(n):
            first.append(copy(a, 0, me, sibling, src=x_refs[a]))
            first += [copy(a, 1 + j, me, (*chip, c), src=x_refs[a]) for j, chip in enumerate(chips)]
        for cp in first:
            cp.start()
        passed = []
        for a in range(n):
            for j, chip in enumerate(chips):
                copy(a, 1 + j, (*chip, c), me).wait_recv()
                cp = copy(a, 4 + j, (*chip, c), sibling)
                cp.start()
                passed.append(cp)
        for a in range(n):
            copy(a, 0, sibling, me).wait_recv()
            for j, chip in enumerate(chips):
                copy(a, 4 + j, (*chip, 1 - c), me).wait_recv()
        for cp in first + passed:
            cp.wait_send()
        for cp in mine:
            cp.wait()

    return pl.pallas_call(
        body, name=name, in_specs=[ANY] * n_in, out_specs=[ANY] * n,
        out_shape=[jax.ShapeDtypeStruct((N_DEV, *s.shape), s.dtype) for s in shards],
        scratch_shapes=[pltpu.SemaphoreType.DMA((7 * n,)), pltpu.SemaphoreType.DMA((7 * n,)), pltpu.SemaphoreType.DMA((n,))],
    )(*shards, *after)


HBM = pl.BlockSpec(memory_space=pltpu.HBM)
SEM = pl.BlockSpec(memory_space=pltpu.SEMAPHORE)
EFFECT = pltpu.SideEffectType.DATAFLOW_SIDE_EFFECTING
N_PEER = N_DEV - 1
RELATIONS = [(dx, dy, dc) for dx in (0, 1) for dy in (0, 1) for dc in (0, 1)][1:]


def _peer(place, rel):
    return tuple(1 - v if d else v for v, d in zip(place, rel))


def gather_start(name, shards, me, before):
    n = len(shards)

    def body(*refs):
        x_refs, land_refs = refs[:n], refs[n:2 * n]
        send_sems, recv_sems = refs[2 * n + len(before):2 * n + len(before) + 2]
        token = refs[-1]
        place = _place()
        mine = _slot(*place)
        for a in range(n):
            for rel in RELATIONS:
                pltpu.make_async_remote_copy(
                    src_ref=x_refs[a], dst_ref=land_refs[a].at[mine], send_sem=send_sems.at[a], recv_sem=recv_sems.at[a],
                    device_id=_peer(place, rel), device_id_type=MESH).start()
        token[...] = jnp.zeros_like(token)

    lands = [lax.dynamic_update_index_in_dim(lax.empty((N_DEV, *s.shape), s.dtype), s, me, 0) for s in shards]
    outs = pl.pallas_call(
        body, name=name,
        in_specs=[HBM] * (2 * n) + [ANY] * len(before),
        out_specs=[SEM, SEM] + [HBM] * (2 * n) + [pl.BlockSpec(memory_space=pltpu.VMEM)],
        out_shape=[pltpu.SemaphoreType.DMA((n,)), pltpu.SemaphoreType.DMA((n,))]
        + [pltpu.HBM(t.shape, t.dtype) for t in (*shards, *lands)] + [jax.ShapeDtypeStruct((8, LANE), F32)],
        input_output_aliases={i: 2 + i for i in range(2 * n)},
        compiler_params=pltpu.CompilerParams(has_side_effects=EFFECT),
    )(*[pltpu.with_memory_space_constraint(t, pltpu.HBM) for t in (*shards, *lands)], *before)
    return (outs[0], outs[1], outs[2:2 + n], outs[2 + n:2 + 2 * n]), outs[-1]


def gather_wait(name, state, after):
    send_sems, recv_sems, shards, lands = state
    n = len(shards)

    def body(*refs):
        land_refs = refs[n:2 * n]
        s_sems, r_sems = refs[2 * n:2 * n + 2]
        place = _place()
        for a in range(n):
            seven = land_refs[a].at[pl.ds(0, N_PEER)]
            cp = pltpu.make_async_remote_copy(
                src_ref=seven, dst_ref=seven, send_sem=s_sems.at[a], recv_sem=r_sems.at[a], device_id=place, device_id_type=MESH)
            cp.wait_send()
            cp.wait_recv()

    outs = pl.pallas_call(
        body, name=name,
        in_specs=[HBM] * (2 * n) + [SEM, SEM] + [ANY] * len(after), out_specs=[HBM] * (2 * n),
        out_shape=[pltpu.HBM(t.shape, t.dtype) for t in (*shards, *lands)],
        input_output_aliases={i: i for i in range(2 * n)},
        compiler_params=pltpu.CompilerParams(has_side_effects=EFFECT),
    )(*shards, *lands, send_sems, recv_sems, *after)
    return outs[n:]


def exchange_start(name, grads, before):
    n = len(grads)

    def body(*refs):
        g_refs, land_refs = refs[:n], refs[n:2 * n]
        send_sems, recv_sems = refs[2 * n + 1:2 * n + 3]
        token = refs[-1]
        place = _place()
        for a in range(n):
            for r, rel in enumerate(RELATIONS):
                p = _peer(place, rel)
                pltpu.make_async_remote_copy(
                    src_ref=g_refs[a].at[_slot(*p)], dst_ref=land_refs[a].at[r],
                    send_sem=send_sems.at[a], recv_sem=recv_sems.at[a], device_id=p, device_id_type=MESH).start()
        token[...] = jnp.zeros_like(token)

    lands = [lax.empty((N_PEER, *g.shape[1:]), g.dtype) for g in grads]
    outs = pl.pallas_call(
        body, name=name,
        in_specs=[HBM] * (2 * n) + [ANY],
        out_specs=[SEM, SEM] + [HBM] * (2 * n) + [pl.BlockSpec(memory_space=pltpu.VMEM)],
        out_shape=[pltpu.SemaphoreType.DMA((n,)), pltpu.SemaphoreType.DMA((n,))]
        + [pltpu.HBM(g.shape, g.dtype) for g in grads] + [pltpu.HBM(t.shape, t.dtype) for t in lands]
        + [jax.ShapeDtypeStruct((8, LANE), F32)],
        input_output_aliases={i: 2 + i for i in range(2 * n)},
        compiler_params=pltpu.CompilerParams(has_side_effects=EFFECT),
    )(*[pltpu.with_memory_space_constraint(t, pltpu.HBM) for t in (*grads, *lands)], before)
    return (outs[0], outs[1], outs[2:2 + n], outs[2 + n:2 + 2 * n]), outs[-1]


def exchange_wait(name, state, after):
    send_sems, recv_sems, grads, lands = state
    n = len(grads)

    def body(*refs):
        g_refs, land_refs = refs[:n], refs[n:2 * n]
        s_sems, r_sems = refs[2 * n:2 * n + 2]
        place = _place()
        for a in range(n):
            cp = pltpu.make_async_remote_copy(
                src_ref=g_refs[a].at[pl.ds(0, N_PEER)], dst_ref=land_refs[a],
                send_sem=s_sems.at[a], recv_sem=r_sems.at[a], device_id=place, device_id_type=MESH)
            cp.wait_send()
            cp.wait_recv()

    outs = pl.pallas_call(
        body, name=name,
        in_specs=[HBM] * (2 * n) + [SEM, SEM, ANY], out_specs=[HBM] * (2 * n),
        out_shape=[pltpu.HBM(t.shape, t.dtype) for t in (*grads, *lands)],
        input_output_aliases={i: i for i in range(2 * n)},
        compiler_params=pltpu.CompilerParams(has_side_effects=EFFECT),
    )(*grads, *lands, send_sems, recv_sems, after)
    return outs[:n], outs[n:]


WEIGHTS = ['norm_mix_w', 'w_in', 'hg_lb_raw', 'hg_norm_w', 'cv_dw_w', 'cv_dw_b', 'cv_ln_w', 'cv_ln_b', 'pl_w', 'pl_scale',
           'lru_conv_w', 'lru_conv_b', 'lru_wa', 'lru_ba', 'lru_wx', 'lru_bx', 'lru_lambda', 'gate_b', 'w_branch', 'w_out',
           'norm_mem_w', 'mem_norm_w', 'xa_wq', 'xa_wkv', 'xa_wo', 'norm_ffn_w', 'ffn_w1', 'ffn_w2', 'final_norm_w']
BIG = ('w_in', 'w_branch', 'w_out', 'xa_wq', 'xa_wkv', 'xa_wo', 'ffn_w1', 'ffn_w2')
SMALL_SHARDED = ('cv_dw_w', 'lru_conv_w', 'gate_b')
SMALL = tuple(n for n in WEIGHTS if n not in BIG and n not in SMALL_SHARDED)
PACK_ROWS = 256


def _pack(arrs):
    flat = jnp.concatenate([a.reshape(-1).astype(F32) for a in arrs])
    tile = PACK_ROWS * LANE
    padded = -(-flat.shape[0] // tile) * tile
    return jnp.pad(flat, (0, padded - flat.shape[0])).reshape(-1, LANE)


def _unpack(packed, shapes):
    flat = packed.reshape(-1)
    out, off = [], 0
    for s in shapes:
        n = math.prod(s)
        out.append(flat[off:off + n].reshape(s))
        off += n
    return out


def _gather_last(g, shard_shape):
    nd = len(shard_shape)
    full = jnp.moveaxis(g, 0, nd - 1)
    return full.reshape(*shard_shape[:-1], N_DEV * shard_shape[-1])


def _natural(blocks):
    nb, k, c = blocks.shape
    return jnp.transpose(blocks, (1, 0, 2)).reshape(k, nb * c)


def _block_diag(w):
    w2 = w.reshape(4, 2, 64, 64)
    z = jnp.zeros((4, 64, 64), w.dtype)
    return jnp.concatenate([jnp.concatenate([w2[:, 0], z], axis=2), jnp.concatenate([z, w2[:, 1]], axis=2)], axis=1)


def _block_diag_t(d):
    return jnp.stack([d[:, :64, :64], d[:, 64:, 64:]], axis=1).reshape(8, 64, 64)


def _lower_bounds(raw):
    lb = jnp.cumsum(jax.nn.softmax(raw.astype(F32), axis=0), axis=0)
    return lb - lb[0:1]


def _decay_rates(lam):
    return (LRU_C * jax.nn.softplus(-lam.astype(F32))).reshape(DEPTH, BRANCH_W)


def _relu2(acc):
    r = jnp.maximum(acc, 0.0)
    return acc, r * r


def _relu2_grad(acc, u):
    return (acc * 2.0 * jnp.maximum(u, 0.0),)


def _add(acc, e):
    return (acc + e,)


def _layer_fwd(x0, mem, p, g, rest, after=()):
    h1 = rms_fwd("rms_mix", x0, p['norm_mix_w'])
    proj = mm_nt("mm_in", h1, g['w_in'], tn=2176, after=after)[0]
    bcat, states, o_hg = hgrn_fwd(proj, p['lb'], p['hg_norm_w'])
    zc = cv_fwd(proj, p['cv_w32'], p['cv_dw_b'])
    bcat = ln_silu_fwd(zc, p['cv_ln_w'], p['cv_ln_b'], bcat)
    bcat = pool_fwd(proj, p['pl_w'], p['pl_scale'], bcat)
    bcat, hst = lru_fwd(proj, p['lru_cw8'], p['lru_conv_b'], p['wa_bd'], p['lru_ba'], p['wx_bd'], p['lru_bx'], p['sp8'], bcat)
    more, after = rest(bcat)
    g = {**g, **more}
    ups = mm_branch_nn("mm_up", bcat, g['w_branch'], tm=2048, after=after)
    merged = merge_fwd(ups, proj, p['gate_b'])
    x1 = mm_nn("mm_out", merged, g['w_out'], epi=_add, extras=(x0,))[0]
    h2 = rms_fwd("rms_mem", x1, p['norm_mem_w'])
    q = mm_nn("mm_q", h2, g['xa_wq'], out_dtype=BF16)[0]
    memn = rms_fwd("rms_memtok", mem, p['mem_norm_w'])
    kv = mm_nn("mm_kv", memn, g['xa_wkv'], out_dtype=BF16, tn=2048)[0]
    oa = attn_fwd(q, kv)
    x2 = mm_nn("mm_o", oa, g['xa_wo'], epi=_add, extras=(x1,))[0]
    h3 = rms_fwd("rms_ffn", x2, p['norm_ffn_w'])
    u, act = mm_nn("mm_ffn1", h3, g['ffn_w1'], epi=_relu2, out_dtypes=[BF16, BF16])
    x3 = mm_nn("mm_ffn2", act, g['ffn_w2'], epi=_add, extras=(x2,))[0]
    res = dict(x0=x0, h1=h1, proj=proj, states=states, o_hg=o_hg, zc=zc, hst=hst, bcat=bcat, ups=ups, merged=merged,
               x1=x1, h2=h2, q=q, memn=memn, kv=kv, oa=oa, x2=x2, h3=h3, u=u, act=act)
    return x3, res, g


def _layer_bwd(dx3, mem, p, g, r, midway, finish):
    gs, gb = {}, {}
    du = mm_nt("mm_dffn2", dx3, g['ffn_w2'], out_dtype=BF16, epi=_relu2_grad, extras=(r['u'],))[0]
    gb['ffn_w2'] = mm_tn("mm_gw2", r['act'], dx3).reshape(N_DEV, -1, D_MODEL)
    gb['ffn_w1'] = mm_tn_cb("mm_gw1", r['h3'], du, N_DEV)
    dh3 = mm_nt("mm_dffn1", du, g['ffn_w1'], out_dtype=BF16)[0]
    dx2, gs['norm_ffn_w'] = rms_bwd("rmsb_ffn", r['x2'], p['norm_ffn_w'], dh3, dx3)
    doa = mm_nt("mm_do", dx2, g['xa_wo'], out_dtype=BF16)[0]
    gb['xa_wo'] = mm_tn("mm_gwo", r['oa'], dx2).reshape(N_DEV, -1, D_MODEL)
    dq, dkv = attn_bwd(r['q'], r['kv'], doa)
    gb['xa_wq'] = mm_tn("mm_gwq", r['h2'], dq).reshape(N_DEV, -1, D_MODEL)
    dh2 = mm_nt("mm_dq", dq, g['xa_wq'], out_dtype=BF16)[0]
    gb['xa_wkv'] = mm_tn_cb("mm_gwkv", r['memn'], dkv, N_DEV)
    dmemn = mm_nt("mm_dkv", dkv, g['xa_wkv'], out_dtype=BF16)[0]
    _, gs['mem_norm_w'] = rms_bwd("rmsb_memtok", mem, p['mem_norm_w'], dmemn)
    dx1, gs['norm_mem_w'] = rms_bwd("rmsb_mem", r['x1'], p['norm_mem_w'], dh2, dx2)
    after = midway(gb, dx1)
    gb = {}
    dmerged = mm_nt("mm_dout", dx1, g['w_out'], out_dtype=BF16, after=after)[0]
    gb['w_out'] = mm_tn("mm_gwout", r['merged'], dx1).reshape(N_DEV, -1, D_MODEL)
    dups, dproj, gs['gate_b'] = merge_bwd(dmerged, r['ups'], r['proj'], p['gate_b'])
    gwb = mm_branch_tn("mm_gwb", r['bcat'], dups, N_BRANCH)
    gb['w_branch'] = jnp.transpose(gwb.reshape(N_BRANCH, BRANCH_W, N_DEV, -1), (2, 0, 1, 3))
    dbcat = mm_branch_nt("mm_dup", dups, g['w_branch'], tm=2048)
    dproj, gs['lb'], gs['hg_norm_w'] = hgrn_bwd(r['proj'], p['lb'], p['hg_norm_w'], r['states'], r['o_hg'], dbcat, dproj)
    dzc, gs['cv_ln_w'], gs['cv_ln_b'] = ln_silu_bwd(r['zc'], p['cv_ln_w'], p['cv_ln_b'], dbcat)
    dca, dcg, dcw, gs['cv_dw_b'] = cv_bwd(r['proj'], p['cv_w32'], dzc)
    gs['cv_dw_w'] = dcw[:CV_KERNEL]
    dproj, gs['pl_w'], gs['pl_scale'] = pool_bwd(r['proj'], p['pl_w'], p['pl_scale'], dbcat, dproj)
    dlx, dly, dlcw, gs['lru_conv_b'], dwa, gs['lru_ba'], dwx, gs['lru_bx'], gs['sp8'] = lru_bwd(
        r['proj'], p['lru_cw8'], p['lru_conv_b'], p['wa_bd'], p['lru_ba'], p['wx_bd'], p['lru_bx'], p['sp8'], r['hst'], dbcat)
    gs['lru_conv_w'] = dlcw[:LRU_CONV]
    gs['lru_wa'], gs['lru_wx'] = _block_diag_t(dwa), _block_diag_t(dwx)
    gs['lru_ba'], gs['lru_bx'] = gs['lru_ba'].reshape(8, 64), gs['lru_bx'].reshape(8, 64)
    for off, piece in ((OFF_CV, dca), (OFF_CV + BRANCH_W, dcg), (OFF_LX, dlx), (OFF_LY, dly)):
        dproj = lax.dynamic_update_slice(dproj, piece, (0, off))
    gb['w_in'] = mm_tn("mm_gwin", dproj, r['h1'], tm=2176, tk=1024).reshape(N_DEV, -1, D_MODEL)
    dh1 = mm_nn("mm_din", dproj, g['w_in'], out_dtype=BF16, tk=4352, after=finish(gb, dx1))[0]
    dx0, gs['norm_mix_w'] = rms_bwd("rmsb_mix", r['x0'], p['norm_mix_w'], dh1, dx1)
    return dx0, gs


def kernel(x, mem, norm_mix_w, w_in, hg_lb_raw, hg_norm_w, cv_dw_w, cv_dw_b, cv_ln_w, cv_ln_b, pl_w, pl_scale, lru_conv_w, lru_conv_b, lru_wa, lru_ba, lru_wx, lru_bx, lru_lambda, gate_b, w_branch, w_out, norm_mem_w, mem_norm_w, xa_wq, xa_wkv, xa_wo, norm_ffn_w, ffn_w1, ffn_w2, final_norm_w, loss_target, m_norm_mix_w, m_w_in, m_hg_lb_raw, m_hg_norm_w, m_cv_dw_w, m_cv_dw_b, m_cv_ln_w, m_cv_ln_b, m_pl_w, m_pl_scale, m_lru_conv_w, m_lru_conv_b, m_lru_wa, m_lru_ba, m_lru_wx, m_lru_bx, m_lru_lambda, m_gate_b, m_w_branch, m_w_out, m_norm_mem_w, m_mem_norm_w, m_xa_wq, m_xa_wkv, m_xa_wo, m_norm_ffn_w, m_ffn_w1, m_ffn_w2, m_final_norm_w, v_norm_mix_w, v_w_in, v_hg_lb_raw, v_hg_norm_w, v_cv_dw_w, v_cv_dw_b, v_cv_ln_w, v_cv_ln_b, v_pl_w, v_pl_scale, v_lru_conv_w, v_lru_conv_b, v_lru_wa, v_lru_ba, v_lru_wx, v_lru_bx, v_lru_lambda, v_gate_b, v_w_branch, v_w_out, v_norm_mem_w, v_mem_norm_w, v_xa_wq, v_xa_wkv, v_xa_wo, v_norm_ffn_w, v_ffn_w1, v_ffn_w2, v_final_norm_w):
    W = dict(zip(WEIGHTS, (norm_mix_w, w_in, hg_lb_raw, hg_norm_w, cv_dw_w, cv_dw_b, cv_ln_w, cv_ln_b, pl_w, pl_scale, lru_conv_w, lru_conv_b, lru_wa, lru_ba, lru_wx, lru_bx, lru_lambda, gate_b, w_branch, w_out, norm_mem_w, mem_norm_w, xa_wq, xa_wkv, xa_wo, norm_ffn_w, ffn_w1, ffn_w2, final_norm_w)))
    Mo = dict(zip(WEIGHTS, (m_norm_mix_w, m_w_in, m_hg_lb_raw, m_hg_norm_w, m_cv_dw_w, m_cv_dw_b, m_cv_ln_w, m_cv_ln_b, m_pl_w, m_pl_scale, m_lru_conv_w, m_lru_conv_b, m_lru_wa, m_lru_ba, m_lru_wx, m_lru_bx, m_lru_lambda, m_gate_b, m_w_branch, m_w_out, m_norm_mem_w, m_mem_norm_w, m_xa_wq, m_xa_wkv, m_xa_wo, m_norm_ffn_w, m_ffn_w1, m_ffn_w2, m_final_norm_w)))
    Vo = dict(zip(WEIGHTS, (v_norm_mix_w, v_w_in, v_hg_lb_raw, v_hg_norm_w, v_cv_dw_w, v_cv_dw_b, v_cv_ln_w, v_cv_ln_b, v_pl_w, v_pl_scale, v_lru_conv_w, v_lru_conv_b, v_lru_wa, v_lru_ba, v_lru_wx, v_lru_bx, v_lru_lambda, v_gate_b, v_w_branch, v_w_out, v_norm_mem_w, v_mem_norm_w, v_xa_wq, v_xa_wkv, v_xa_wo, v_norm_ffn_w, v_ffn_w1, v_ffn_w2, v_final_norm_w)))
    me = _slot(*_place())
    xs, mems, target = x[0], mem[0], loss_target[0]

    shard_shapes = [W[n].shape for n in SMALL_SHARDED]
    gathered = all_gather("ag_small", [_pack([W[n] for n in SMALL_SHARDED])])[0]
    parts = [jnp.stack(ps) for ps in zip(*[_unpack(gathered[d], shard_shapes) for d in range(N_DEV)])]
    full_small = {n: _gather_last(parts[i], shard_shapes[i]) for i, n in enumerate(SMALL_SHARDED)}
    lb_all, lb_vjp = jax.vjp(_lower_bounds, hg_lb_raw)
    sp8_all, sp8_vjp = jax.vjp(_decay_rates, lru_lambda)

    def layer_params(l):
        p = {n: W[n][l] for n in SMALL if n != 'final_norm_w'}
        p['lb'] = lb_all[l]
        p['sp8'] = sp8_all[l]
        p['cv_w32'] = jnp.pad(full_small['cv_dw_w'][l], ((0, 32 - CV_KERNEL), (0, 0)))
        p['lru_cw8'] = jnp.pad(full_small['lru_conv_w'][l], ((0, 8 - LRU_CONV), (0, 0)))
        p['gate_b'] = full_small['gate_b'][l]
        p['wa_bd'], p['wx_bd'] = _block_diag(lru_wa[l]), _block_diag(lru_wx[l])
        p['lru_ba'], p['lru_bx'] = lru_ba[l].reshape(-1), lru_bx[l].reshape(-1)
        return p

    def shards_of(l):
        first = [jnp.transpose(w_in[l]).astype(BF16)]
        others = [w[l].astype(BF16) for w in (w_branch, w_out, xa_wq, xa_wkv, xa_wo, ffn_w1, ffn_w2)]
        return first, others

    def start_gather(l, before):
        first, others = shards_of(l)
        state_a, tok_a = gather_start(f"ag_start{l}a", first, me, before)
        state_b, tok_b = gather_start(f"ag_start{l}b", others, me, (*before, tok_a))
        return state_a, state_b, (tok_a, tok_b)

    def first_of(o):
        return dict(w_in=o[0].reshape(IN_W, D_MODEL))

    def others_of(o):
        wb = jnp.transpose(o[0], (1, 2, 0, 3)).reshape(N_BRANCH, BRANCH_W, D_MODEL)
        return dict(w_branch=wb, w_out=o[1].reshape(D_MODEL, D_MODEL),
                    xa_wq=o[2].reshape(D_MODEL, D_MODEL), xa_wkv=_natural(o[3]), xa_wo=o[4].reshape(D_MODEL, D_MODEL),
                    ffn_w1=_natural(o[5]), ffn_w2=o[6].reshape(D_FF, D_MODEL))

    params = [layer_params(l) for l in range(DEPTH)]
    mats, residuals = [], []
    xc = xs
    first, others = shards_of(0)
    whole = all_gather("ag_layer0", first)
    state_b, started = gather_start("ag_start0b", others, me, (whole[0],))
    gathers = {}
    for l in range(DEPTH):
        if l == 0:
            g_first = first_of(whole)
        else:
            state_a, state_b, _ = gathers.pop(l)
            g_first = first_of(gather_wait(f"ag_wait{l}a", state_a, (xc,)))

        def rest(mixed, l=l):
            more = others_of(gather_wait(f"ag_wait{l}b", state_b, (mixed,)))
            if l + 1 == DEPTH:
                return more, ()
            gathers[l + 1] = start_gather(l + 1, (more['w_out'],))
            return more, gathers[l + 1][2]

        xc, res, g = _layer_fwd(xc, mems, params[l], g_first, rest, after=(started,) if l == 0 else ())
        mats.append(g)
        residuals.append(res)
    loss_part, dx, g_final = loss_head(xc, final_norm_w, target)
    loss = lax.psum(loss_part, ("x", "y", "c"))

    small_grads = [None] * DEPTH
    big_grads = [{} for _ in range(DEPTH)]
    pending = []

    def send(l, group, blocks, before):
        names = list(blocks)
        state, tok = exchange_start(f"rs_start{l}{group}", [blocks[n] for n in names], before)
        pending.append((l, group, names, state))
        return (tok,)

    def land(after):
        l, group, names, state = pending.pop(0)
        sent, landed = exchange_wait(f"rs_wait{l}{group}", state, after)
        for n, s, t in zip(names, sent, landed):
            own = lax.dynamic_index_in_dim(s, me, 0, keepdims=False).reshape(-1, s.shape[-1])
            big_grads[l][n] = sum_parts(t.reshape(N_PEER, -1, t.shape[-1]), own).reshape(t.shape[1:])

    for l in reversed(range(DEPTH)):
        dx, small_grads[l] = _layer_bwd(dx, mems, params[l], mats[l], residuals[l],
                                        lambda blocks, dx1, l=l: send(l, "a", blocks, dx1),
                                        lambda blocks, dx1, l=l: send(l, "b", blocks, dx1))
        while pending[0][0] > l:
            land(dx)

    def stacked(n):
        return jnp.stack([small_grads[l][n] for l in range(DEPTH)])

    part = {n: stacked(n) for n in SMALL if n not in ('final_norm_w', 'hg_lb_raw', 'lru_lambda')}
    part['final_norm_w'] = g_final
    part['hg_lb_raw'] = lb_vjp(stacked('lb'))[0]
    part['lru_lambda'] = sp8_vjp(stacked('sp8'))[0]
    for n in SMALL_SHARDED:
        part[n] = stacked(n)
    names = list(SMALL) + list(SMALL_SHARDED)
    full_shapes = [part[n].shape for n in names]
    packed = _pack([part[n] for n in names])
    state, _ = exchange_start("rs_small_start", [packed.reshape(N_DEV, -1, LANE)], packed)
    sent, landed = exchange_wait("rs_small_wait", state, packed)
    mine = sum_parts(landed[0], lax.dynamic_index_in_dim(sent[0], me, 0, keepdims=False))
    total = all_gather("ag_grads", [mine])[0].reshape(-1, LANE)
    while pending:
        land(total)

    G = {}
    G['w_in'] = jnp.stack([jnp.transpose(big_grads[l]['w_in']) for l in range(DEPTH)])
    for n in ('w_branch', 'w_out', 'xa_wq', 'xa_wkv', 'xa_wo', 'ffn_w1', 'ffn_w2'):
        G[n] = jnp.stack([big_grads[l][n] for l in range(DEPTH)])
    for n, t in zip(names, _unpack(total, full_shapes)):
        if n in SMALL_SHARDED:
            c = t.shape[-1] // N_DEV
            t = lax.dynamic_slice_in_dim(t, me * c, c, axis=t.ndim - 1)
        G[n] = t

    delta, new_m, new_v = {}, {}, {}
    for n in BIG:
        c = W[n].shape[-1]
        d, nm, nv = adamw(W[n].reshape(-1, c), G[n].reshape(-1, c), Mo[n].reshape(-1, c), Vo[n].reshape(-1, c))
        delta[n], new_m[n], new_v[n] = d.reshape(W[n].shape), nm.reshape(W[n].shape), nv.reshape(W[n].shape)
    shapes = [W[n].shape for n in names]
    d, nm, nv = adamw(_pack([W[n] for n in names]), _pack([G[n] for n in names]), _pack([Mo[n] for n in names]), _pack([Vo[n] for n in names]))
    for n, a, b, c in zip(names, _unpack(d, shapes), _unpack(nm, shapes), _unpack(nv, shapes)):
        delta[n], new_m[n], new_v[n] = a, b, c
    return (loss, dx[None], *[G[n] for n in WEIGHTS], *[delta[n] for n in WEIGHTS],
            *[new_m[n] for n in WEIGHTS], *[new_v[n] for n in WEIGHTS])
```

```python
import functools
import math

import jax
import jax.numpy as jnp
from jax import lax
from jax.experimental import pallas as pl
from jax.experimental.pallas import tpu as pltpu

F32 = jnp.float32
BF16 = jnp.bfloat16
I32 = jnp.int32

N_DEV = 8
D_MODEL = 1024
DEPTH = 4
CHUNK = 64
EPS = 1e-6
HG_HEADS = 4
BRANCH_W = 512
CV_KERNEL = 31
POOL_WINDOWS = (2, 4, 8, 16)
LRU_CONV = 4
LRU_C = 8.0
XA_HEADS = 4
XA_HD = D_MODEL // XA_HEADS
D_FF = 4 * D_MODEL
IN_W = 8704
OFF_Q, OFF_F, OFF_V, OFF_G, OFF_CV, OFF_PL, OFF_LX, OFF_LY, OFF_GATE = 0, 512, 1024, 1536, 2048, 3072, 3584, 4096, 4608
LANE = 128
ADAM_LR, ADAM_B1, ADAM_B2, ADAM_EPS, ADAM_WD, ADAM_STEP = 0.001, 0.9, 0.999, 1e-08, 0.01, 10
VMEM_LIMIT = 56 * 1024 * 1024
MESH = pl.DeviceIdType.MESH
NEG = -1e30
ANY_SPACE = pl.BlockSpec(memory_space=pl.ANY)


def _cp(sem, **kw):
    return pltpu.CompilerParams(dimension_semantics=sem, vmem_limit_bytes=VMEM_LIMIT, **kw)


def _sigmoid(x):
    return 1.0 / (1.0 + jnp.exp(-x))


def _dsilu(x, s):
    return s * (1.0 + x * (1.0 - s))


def _dot(a, b, cdims, precision=None):
    return lax.dot_general(a, b, (cdims, ((), ())), preferred_element_type=F32, precision=precision)


NN = ((1,), (0,))
NT = ((1,), (1,))
TN = ((0,), (0,))


def _mm(name, a, b, *, grid, a_spec, b_spec, o_specs, out_shapes, acc_shape, cdims, epi=None, extras=(), extra_specs=(), after=()):
    nk = grid[2]
    n_e, n_o = len(extras), len(out_shapes)
    extras = (*extras, *after)
    extra_specs = (*extra_specs, *[ANY_SPACE] * len(after))

    def body(*refs):
        a_ref, b_ref = refs[0], refs[1]
        e_refs = refs[2:2 + n_e]
        o_refs = refs[2 + len(extras):2 + len(extras) + n_o]

        def finish(acc):
            vals = epi(acc, *[r[...] for r in e_refs]) if epi is not None else (acc,)
            for r, v in zip(o_refs, vals):
                r[...] = v.astype(r.dtype)

        part = _dot(a_ref[...].astype(BF16), b_ref[...].astype(BF16), cdims)
        if nk == 1:
            finish(part)
        else:
            acc_ref = refs[-1]
            k = pl.program_id(2)

            @pl.when(k == 0)
            def _():
                acc_ref[...] = part

            @pl.when(k > 0)
            def _():
                acc_ref[...] += part

            @pl.when(k == nk - 1)
            def _():
                finish(acc_ref[...])

    return pl.pallas_call(
        body, name=name, grid=grid,
        in_specs=[a_spec, b_spec, *extra_specs], out_specs=list(o_specs), out_shape=list(out_shapes),
        scratch_shapes=[] if nk == 1 else [pltpu.VMEM(acc_shape, F32)],
        compiler_params=_cp(("parallel", "parallel", "arbitrary")),
    )(a, b, *extras)


def _tile(n, pref):
    t = min(n, pref)
    while n % t:
        t //= 2
    return t


def mm_nt(name, a, b, out_dtype=F32, epi=None, extras=(), n_out=1, out_dtypes=None, tm=1024, tn=1024, tk=2048, after=()):
    M, K = a.shape
    N = b.shape[0]
    tm, tn, tk = _tile(M, tm), _tile(N, tn), _tile(K, tk)
    odt = out_dtypes or [out_dtype] * n_out
    o_spec = pl.BlockSpec((tm, tn), lambda i, j, k: (i, j))
    return _mm(name, a, b, grid=(M // tm, N // tn, K // tk),
               a_spec=pl.BlockSpec((tm, tk), lambda i, j, k: (i, k)),
               b_spec=pl.BlockSpec((tn, tk), lambda i, j, k: (j, k)),
               o_specs=[o_spec] * len(odt), out_shapes=[jax.ShapeDtypeStruct((M, N), d) for d in odt],
               acc_shape=(tm, tn), cdims=NT, epi=epi, extras=extras, extra_specs=[o_spec] * len(extras), after=after)


def mm_nn(name, a, b, out_dtype=F32, epi=None, extras=(), n_out=1, out_dtypes=None, tm=1024, tn=1024, tk=2048, after=()):
    M, K = a.shape
    N = b.shape[1]
    tm, tn, tk = _tile(M, tm), _tile(N, tn), _tile(K, tk)
    odt = out_dtypes or [out_dtype] * n_out
    o_spec = pl.BlockSpec((tm, tn), lambda i, j, k: (i, j))
    return _mm(name, a, b, grid=(M // tm, N // tn, K // tk),
               a_spec=pl.BlockSpec((tm, tk), lambda i, j, k: (i, k)),
               b_spec=pl.BlockSpec((tk, tn), lambda i, j, k: (k, j)),
               o_specs=[o_spec] * len(odt), out_shapes=[jax.ShapeDtypeStruct((M, N), d) for d in odt],
               acc_shape=(tm, tn), cdims=NN, epi=epi, extras=extras, extra_specs=[o_spec] * len(extras), after=after)


def mm_tn(name, a, b, out_dtype=BF16, tm=1024, tn=1024, tk=2048):
    K, M = a.shape
    N = b.shape[1]
    tm, tn, tk = _tile(M, tm), _tile(N, tn), _tile(K, tk)
    return _mm(name, a, b, grid=(M // tm, N // tn, K // tk),
               a_spec=pl.BlockSpec((tk, tm), lambda i, j, k: (k, i)),
               b_spec=pl.BlockSpec((tk, tn), lambda i, j, k: (k, j)),
               o_specs=[pl.BlockSpec((tm, tn), lambda i, j, k: (i, j))],
               out_shapes=[jax.ShapeDtypeStruct((M, N), out_dtype)], acc_shape=(tm, tn), cdims=TN)[0]


def mm_branch_nt(name, a, b, out_dtype=BF16, tm=1024):
    M = a.shape[0]
    G, K, N = b.shape
    tm = _tile(M, tm)
    return _mm(name, a, b, grid=(M // tm, G, 1),
               a_spec=pl.BlockSpec((tm, N), lambda i, g, k: (i, g)),
               b_spec=pl.BlockSpec((None, K, N), lambda i, g, k: (g, 0, 0)),
               o_specs=[pl.BlockSpec((tm, K), lambda i, g, k: (i, g))],
               out_shapes=[jax.ShapeDtypeStruct((M, G * K), out_dtype)], acc_shape=(tm, K), cdims=NT)[0]


def mm_branch_tn(name, a, b, groups, out_dtype=BF16, tk=2048):
    T = a.shape[0]
    K, N = a.shape[1] // groups, b.shape[1] // groups
    tk = _tile(T, tk)
    return _mm(name, a, b, grid=(groups, 1, T // tk),
               a_spec=pl.BlockSpec((tk, K), lambda g, j, k: (k, g)),
               b_spec=pl.BlockSpec((tk, N), lambda g, j, k: (k, g)),
               o_specs=[pl.BlockSpec((None, K, N), lambda g, j, k: (g, 0, 0))],
               out_shapes=[jax.ShapeDtypeStruct((groups, K, N), out_dtype)], acc_shape=(K, N), cdims=TN)[0]


def mm_tn_cb(name, a, b, nb, out_dtype=BF16, tm=1024, tk=2048):
    K, M = a.shape
    N = b.shape[1]
    c = N // nb
    tm, tk = _tile(M, tm), _tile(K, tk)
    return _mm(name, a, b, grid=(M // tm, nb, K // tk),
               a_spec=pl.BlockSpec((tk, tm), lambda i, j, k: (k, i)),
               b_spec=pl.BlockSpec((tk, c), lambda i, j, k: (k, j)),
               o_specs=[pl.BlockSpec((None, tm, c), lambda i, j, k: (j, i, 0))],
               out_shapes=[jax.ShapeDtypeStruct((nb, M, c), out_dtype)], acc_shape=(tm, c), cdims=TN)[0]


def rms_fwd(name, x, w, out_dtype=BF16, tm=512):
    S, D = x.shape
    tm = _tile(S, tm)

    def body(x_ref, w_ref, o_ref):
        xv = x_ref[...]
        r = lax.rsqrt(jnp.mean(xv * xv, axis=-1, keepdims=True) + EPS)
        o_ref[...] = (xv * r * w_ref[...]).astype(o_ref.dtype)

    return pl.pallas_call(
        body, name=name, grid=(S // tm,),
        in_specs=[pl.BlockSpec((tm, D), lambda i: (i, 0)), pl.BlockSpec((1, D), lambda i: (0, 0))],
        out_specs=pl.BlockSpec((tm, D), lambda i: (i, 0)), out_shape=jax.ShapeDtypeStruct((S, D), out_dtype),
        compiler_params=_cp(("parallel",)),
    )(x, w.reshape(1, D))


def rms_bwd(name, x, w, dh, dres=None, tm=512):
    S, D = x.shape
    tm = _tile(S, tm)
    has_res = dres is not None

    def body(*refs):
        if has_res:
            x_ref, w_ref, dh_ref, dres_ref, dx_ref, dw_ref = refs
        else:
            x_ref, w_ref, dh_ref, dx_ref, dw_ref = refs
        xv = x_ref[...]
        dhv = dh_ref[...].astype(F32)
        r = lax.rsqrt(jnp.mean(xv * xv, axis=-1, keepdims=True) + EPS)
        g = dhv * w_ref[...]
        dx = r * g - xv * (r * r * r) * jnp.mean(xv * g, axis=-1, keepdims=True)
        if has_res:
            dx = dx + dres_ref[...]
        dx_ref[...] = dx

        @pl.when(pl.program_id(0) == 0)
        def _():
            dw_ref[...] = jnp.zeros_like(dw_ref)

        dw_ref[...] += jnp.sum(dhv * xv * r, axis=0, keepdims=True)

    row = pl.BlockSpec((tm, D), lambda i: (i, 0))
    vec = pl.BlockSpec((1, D), lambda i: (0, 0))
    args = [x, w.reshape(1, D), dh] + ([dres] if has_res else [])
    dx, dw = pl.pallas_call(
        body, name=name, grid=(S // tm,),
        in_specs=[row, vec, row] + ([row] if has_res else []),
        out_specs=[row, vec], out_shape=[jax.ShapeDtypeStruct((S, D), F32), jax.ShapeDtypeStruct((1, D), F32)],
        compiler_params=_cp(("arbitrary",)),
    )(*args)
    return dx, dw.reshape(D)


def loss_head(x, w, target, tm=512):
    S, D = x.shape
    tm = _tile(S, tm)

    def body(x_ref, w_ref, t_ref, loss_ref, dx_ref, dw_ref):
        xv = x_ref[...]
        wv = w_ref[...]
        r = lax.rsqrt(jnp.mean(xv * xv, axis=-1, keepdims=True) + EPS)
        y = xv * r * wv
        err = y - t_ref[...]
        dy = err * (1.0 / D)
        g = dy * wv
        dx_ref[...] = r * g - xv * (r * r * r) * jnp.mean(xv * g, axis=-1, keepdims=True)

        @pl.when(pl.program_id(0) == 0)
        def _():
            dw_ref[...] = jnp.zeros_like(dw_ref)
            loss_ref[...] = jnp.zeros_like(loss_ref)

        dw_ref[...] += jnp.sum(dy * xv * r, axis=0, keepdims=True)
        part = 0.5 * jnp.sum(jnp.mean(err * err, axis=-1, keepdims=True), axis=0, keepdims=True)
        loss_ref[...] += jnp.broadcast_to(part, loss_ref.shape)

    row = pl.BlockSpec((tm, D), lambda i: (i, 0))
    vec = pl.BlockSpec((1, D), lambda i: (0, 0))
    loss, dx, dw = pl.pallas_call(
        body, name="loss_head", grid=(S // tm,),
        in_specs=[row, vec, row],
        out_specs=[pl.BlockSpec((1, LANE), lambda i: (0, 0)), row, vec],
        out_shape=[jax.ShapeDtypeStruct((1, LANE), F32), jax.ShapeDtypeStruct((S, D), F32), jax.ShapeDtypeStruct((1, D), F32)],
        compiler_params=_cp(("arbitrary",)),
    )(x, w.reshape(1, D), target)
    return loss[0, 0], dx, dw.reshape(D)


SUB = 16
HG_W = HG_HEADS * LANE


def _hg_gates(q, f, lbv):
    sig = _sigmoid(f)
    fg = lbv + (1.0 - lbv) * sig
    sq = _sigmoid(q)
    return sig, fg, 1.0 - fg, sq, q * sq


def _hg_cumsum(logf):
    ri = lax.broadcasted_iota(I32, (CHUNK, CHUNK), 0)
    ci = lax.broadcasted_iota(I32, (CHUNK, CHUNK), 1)
    return _dot((ci <= ri).astype(F32), logf, NN, precision=lax.Precision.HIGHEST)


def _hg_rows():
    return lax.broadcasted_iota(I32, (CHUNK, LANE), 0)


def _hg_below(qf, kk, b, rows):
    blocks, parts = [jnp.zeros((SUB, CHUNK), F32)], []
    for i in range(1, CHUNK // SUB):
        bref = b[SUB * i - 1:SUB * i, :]
        rs = slice(SUB * i, SUB * (i + 1))
        eq = jnp.exp(b[rs] - bref)
        below = rows < SUB * i
        ek = jnp.exp(jnp.where(below, bref - b, NEG))
        qi = (qf[rs] * eq).astype(BF16)
        ki = (kk * ek).astype(BF16)
        blocks.append(_dot(qi, ki, NT))
        parts.append((qi, ki, eq, ek))
    return jnp.concatenate(blocks, axis=0), parts


def hgrn_fwd(proj, lb, nw):
    S = proj.shape[0]
    NC = S // CHUNK
    H = HG_HEADS

    def body(q_ref, f_ref, v_ref, g_ref, lb_ref, nw_ref, out_ref, st_out_ref, o_ref, st, kk_s, b_s):
        c = pl.program_id(0)

        @pl.when(c == 0)
        def _():
            st[...] = jnp.zeros_like(st)

        st_out_ref[...] = st[...]
        sig, fg, kk_all, sq, qf_all = _hg_gates(q_ref[...], f_ref[...], lb_ref[...])
        b_all = _hg_cumsum(jnp.log(fg))
        kk_s[...] = kk_all
        b_s[...] = b_all
        rows = _hg_rows()
        r16 = lax.broadcasted_iota(I32, (SUB, LANE), 0)
        for h in range(H):
            cs = slice(h * LANE, (h + 1) * LANE)
            qf, kk, b, v, g = qf_all[:, cs], kk_all[:, cs], b_all[:, cs], v_ref[:, cs], g_ref[:, cs]
            st_in = st[h]
            diag = []
            for i in range(CHUNK // SUB):
                rs = slice(SUB * i, SUB * (i + 1))
                acc = jnp.zeros((SUB, LANE), F32)
                for j in range(SUB):
                    row = pl.ds(SUB * i + j, 1)
                    e = jnp.exp(jnp.where(r16 >= j, b[rs] - b_s[row, cs], NEG))
                    col = jnp.sum(qf[rs] * (kk_s[row, cs] * e), axis=1, keepdims=True)
                    acc = acc + col * v_ref[row, cs]
                diag.append(acc)
            poff, _ = _hg_below(qf, kk, b, rows)
            vb = v.astype(BF16)
            bl = b[CHUNK - 1:CHUNK, :]
            o = (jnp.concatenate(diag, axis=0) + _dot(poff.astype(BF16), vb, NN)
                 + _dot((qf * jnp.exp(b)).astype(BF16), st_in.astype(BF16), NT))
            st[h] = st_in * jnp.exp(bl) + _dot(vb, (kk * jnp.exp(bl - b)).astype(BF16), TN)
            o_ref[:, cs] = o
            r = lax.rsqrt(jnp.mean(o * o, axis=-1, keepdims=True) + EPS)
            out_ref[:, cs] = (o * r * nw_ref[...] * (g * _sigmoid(g))).astype(out_ref.dtype)

    def seg(off):
        return pl.BlockSpec((CHUNK, HG_W), lambda c: (c, off // HG_W))

    blk = pl.BlockSpec((CHUNK, HG_W), lambda c: (c, 0))
    full = pltpu.VMEM((CHUNK, HG_W), F32)
    return pl.pallas_call(
        body, name="hgrn_fwd", grid=(NC,),
        in_specs=[seg(OFF_Q), seg(OFF_F), seg(OFF_V), seg(OFF_G),
                  pl.BlockSpec((1, HG_W), lambda c: (0, 0)), pl.BlockSpec((1, LANE), lambda c: (0, 0))],
        out_specs=[blk, pl.BlockSpec((None, H, LANE, LANE), lambda c: (c, 0, 0, 0)), blk],
        out_shape=[jax.ShapeDtypeStruct((S, 4 * HG_W), BF16), jax.ShapeDtypeStruct((NC, H, LANE, LANE), F32),
                   jax.ShapeDtypeStruct((S, HG_W), F32)],
        scratch_shapes=[pltpu.VMEM((H, LANE, LANE), F32), full, full],
        compiler_params=_cp(("arbitrary",)),
    )(proj, proj, proj, proj, lb.reshape(1, HG_W), nw.reshape(1, LANE))


def hgrn_bwd(proj, lb, nw, states, o_pre, dbcat, dproj):
    S = proj.shape[0]
    NC = S // CHUNK
    H = HG_HEADS

    def body(q_ref, f_ref, v_ref, g_ref, lb_ref, nw_ref, st_ref, o_ref, do_ref, _,
             dp_ref, dlb_ref, dnw_ref, dst, kk_s, b_s, do_s, db_s, dkk_s, dkk_d, dv_d):
        c = pl.program_id(0)
        dq_ref, df_ref, dv_ref, dg_ref = (dp_ref.at[:, pl.ds(off, HG_W)] for off in (OFF_Q, OFF_F, OFF_V, OFF_G))

        @pl.when(c == 0)
        def _():
            dst[...] = jnp.zeros_like(dst)
            dlb_ref[...] = jnp.zeros_like(dlb_ref)
            dnw_ref[...] = jnp.zeros_like(dnw_ref)

        q_all, g_all = q_ref[...], g_ref[...]
        lbv, nwv = lb_ref[...], nw_ref[...]
        sig, fg, kk_all, sq, qf_all = _hg_gates(q_all, f_ref[...], lbv)
        b_all = _hg_cumsum(jnp.log(fg))
        o_all = o_ref[...]
        dov = do_ref[...].astype(F32)
        sg = _sigmoid(g_all)
        gsg = g_all * sg
        dnw_acc = jnp.zeros((1, LANE), F32)
        for h in range(H):
            cs = slice(h * LANE, (h + 1) * LANE)
            o = o_all[:, cs]
            r = lax.rsqrt(jnp.mean(o * o, axis=-1, keepdims=True) + EPS)
            don = dov[:, cs] * gsg[:, cs]
            dnw_acc = dnw_acc + jnp.sum(don * o * r, axis=0, keepdims=True)
            gno = don * nwv
            do_s[:, cs] = r * gno - o * (r * r * r) * jnp.mean(o * gno, axis=-1, keepdims=True)
            dg_ref[:, cs] = (dov[:, cs] * (o * r * nwv) * _dsilu(g_all[:, cs], sg[:, cs])).astype(dg_ref.dtype)
        dnw_ref[...] += jnp.broadcast_to(dnw_acc, dnw_ref.shape)
        kk_s[...] = kk_all
        b_s[...] = b_all
        rows = _hg_rows()
        r16 = lax.broadcasted_iota(I32, (SUB, LANE), 0)
        for h in range(H):
            cs = slice(h * LANE, (h + 1) * LANE)
            qf, kk, b, v = qf_all[:, cs], kk_all[:, cs], b_all[:, cs], v_ref[:, cs]
            do = do_s[:, cs]
            st_in, dstv = st_ref[h], dst[h]
            bl = b[CHUNK - 1:CHUNK, :]
            eb, ebl, el = jnp.exp(b), jnp.exp(bl - b), jnp.exp(bl)
            qe, ke = qf * eb, kk * ebl
            vb, dob, stb, dstb = v.astype(BF16), do.astype(BF16), st_in.astype(BF16), dstv.astype(BF16)
            w_ = _dot(vb, dstb, NN)
            dqf = eb * _dot(dob, stb, NN)
            dkk = ebl * w_
            dv = _dot(ke.astype(BF16), dstb, NT)
            dbl = el * jnp.sum(st_in * dstv, axis=0, keepdims=True) + jnp.sum(ke * w_, axis=0, keepdims=True)
            dst[h] = dstv * el + _dot(dob, qe.astype(BF16), TN)
            poff, parts = _hg_below(qf, kk, b, rows)
            dpoff = _dot(dob, vb, NT).astype(BF16)
            dv = dv + _dot(poff.astype(BF16), dob, TN)
            dq_blocks = [jnp.zeros((SUB, LANE), F32)]
            for i, (qi, ki, eq, ek) in enumerate(parts, start=1):
                dpi = dpoff[SUB * i:SUB * (i + 1), :]
                dq_blocks.append(_dot(dpi, ki, NN) * eq)
                dkk = dkk + _dot(dpi, qi, TN) * ek
            dqf = dqf + jnp.concatenate(dq_blocks, axis=0)
            dq_diag = []
            for i in range(CHUNK // SUB):
                rs = slice(SUB * i, SUB * (i + 1))
                acc = jnp.zeros((SUB, LANE), F32)
                for j in range(SUB):
                    row = pl.ds(SUB * i + j, 1)
                    ks = kk_s[row, cs]
                    e = jnp.exp(jnp.where(r16 >= j, b[rs] - b_s[row, cs], NEG))
                    x = jnp.sum(do[rs] * v_ref[row, cs], axis=1, keepdims=True) * e
                    acc = acc + x * ks
                    dkk_d[row, cs] = jnp.sum(x * qf[rs], axis=0, keepdims=True)
                    col = jnp.sum(qf[rs] * (ks * e), axis=1, keepdims=True)
                    dv_d[row, cs] = jnp.sum(col * do[rs], axis=0, keepdims=True)
                dq_diag.append(acc)
            dqf = dqf + jnp.concatenate(dq_diag, axis=0)
            dkk = dkk + dkk_d[:, cs]
            dv = dv + dv_d[:, cs]
            dv_ref[:, cs] = dv.astype(dv_ref.dtype)
            db = qf * dqf - kk * dkk
            db_s[:, cs] = db + jnp.where(rows == CHUNK - 1, dbl, 0.0)
            dkk_s[:, cs] = dkk
            dq_ref[:, cs] = (dqf * _dsilu(q_all[:, cs], sq[:, cs])).astype(dq_ref.dtype)
        ri = lax.broadcasted_iota(I32, (CHUNK, CHUNK), 0)
        ci = lax.broadcasted_iota(I32, (CHUNK, CHUNK), 1)
        dlogf = _dot((ci >= ri).astype(F32), db_s[...], NN, precision=lax.Precision.HIGHEST)
        dfg = dlogf / fg - dkk_s[...]
        df_ref[...] = (dfg * (1.0 - lbv) * sig * (1.0 - sig)).astype(df_ref.dtype)
        dlb_ref[...] += jnp.broadcast_to(jnp.sum(dfg * (1.0 - sig), axis=0, keepdims=True), dlb_ref.shape)

    def seg(off):
        return pl.BlockSpec((CHUNK, HG_W), lambda c: (NC - 1 - c, off // HG_W))

    blk = pl.BlockSpec((CHUNK, HG_W), lambda c: (NC - 1 - c, 0))
    full = pltpu.VMEM((CHUNK, HG_W), F32)
    dproj, dlb, dnw = pl.pallas_call(
        body, name="hgrn_bwd", grid=(NC,),
        in_specs=[seg(OFF_Q), seg(OFF_F), seg(OFF_V), seg(OFF_G),
                  pl.BlockSpec((1, HG_W), lambda c: (0, 0)), pl.BlockSpec((1, LANE), lambda c: (0, 0)),
                  pl.BlockSpec((None, H, LANE, LANE), lambda c: (NC - 1 - c, 0, 0, 0)), blk, blk, ANY_SPACE],
        out_specs=[pl.BlockSpec((CHUNK, 4 * HG_W), lambda c: (NC - 1 - c, 0)),
                   pl.BlockSpec((8, HG_W), lambda c: (0, 0)), pl.BlockSpec((8, LANE), lambda c: (0, 0))],
        out_shape=[jax.ShapeDtypeStruct(dproj.shape, dproj.dtype), jax.ShapeDtypeStruct((8, HG_W), F32),
                   jax.ShapeDtypeStruct((8, LANE), F32)],
        input_output_aliases={9: 0},
        scratch_shapes=[pltpu.VMEM((H, LANE, LANE), F32)] + [full] * 7,
        compiler_params=_cp(("arbitrary",)),
    )(proj, proj, proj, proj, lb.reshape(1, HG_W), nw.reshape(1, LANE), states, o_pre, dbcat, dproj)
    return dproj, dlb[0], dnw[0]


CV_PAD = 32
ROWS = 256


def _colblk(S, off):
    return pl.BlockSpec((S, LANE), lambda j: (0, off // LANE + j))


def cv_fwd(proj, w32, bias):
    S = proj.shape[0]
    nchunk = S // ROWS

    def body(a_ref, g_ref, w_ref, b_ref, o_ref, zpad):
        zpad[pl.ds(0, CV_PAD), :] = jnp.zeros((CV_PAD, LANE), F32)

        def glu(c, _):
            r0 = pl.multiple_of(c * ROWS, ROWS)
            zpad[pl.ds(CV_PAD + r0, ROWS), :] = a_ref[pl.ds(r0, ROWS), :] * _sigmoid(g_ref[pl.ds(r0, ROWS), :])
            return 0

        lax.fori_loop(0, nchunk, glu, 0)

        def conv(c, _):
            r0 = pl.multiple_of(c * ROWS, ROWS)
            acc = jnp.broadcast_to(b_ref[...], (ROWS, LANE))
            for j in range(CV_KERNEL):
                acc = acc + w_ref[pl.ds(j, 1), :] * zpad[pl.ds(r0 + (CV_PAD - CV_KERNEL + 1) + j, ROWS), :]
            o_ref[pl.ds(r0, ROWS), :] = acc
            return 0

        lax.fori_loop(0, nchunk, conv, 0)

    return pl.pallas_call(
        body, name="cv_fwd", grid=(BRANCH_W // LANE,),
        in_specs=[_colblk(S, OFF_CV), _colblk(S, OFF_CV + BRANCH_W),
                  pl.BlockSpec((32, LANE), lambda j: (0, j)), pl.BlockSpec((1, LANE), lambda j: (0, j))],
        out_specs=pl.BlockSpec((S, LANE), lambda j: (0, j)), out_shape=jax.ShapeDtypeStruct((S, BRANCH_W), F32),
        scratch_shapes=[pltpu.VMEM((CV_PAD + S, LANE), F32)],
        compiler_params=_cp(("parallel",)),
    )(proj, proj, w32, bias.reshape(1, BRANCH_W))


def cv_bwd(proj, w32, dzc):
    S = proj.shape[0]
    nchunk = S // ROWS

    def body(a_ref, g_ref, w_ref, dz_ref, da_ref, dg_ref, dw_ref, db_ref, zpad, dpad):
        zpad[pl.ds(0, CV_PAD), :] = jnp.zeros((CV_PAD, LANE), F32)
        dpad[pl.ds(S, CV_PAD), :] = jnp.zeros((CV_PAD, LANE), F32)
        dw_ref[...] = jnp.zeros_like(dw_ref)

        def glu(c, dsum):
            r0 = pl.multiple_of(c * ROWS, ROWS)
            zpad[pl.ds(CV_PAD + r0, ROWS), :] = a_ref[pl.ds(r0, ROWS), :] * _sigmoid(g_ref[pl.ds(r0, ROWS), :])
            d = dz_ref[pl.ds(r0, ROWS), :]
            dpad[pl.ds(r0, ROWS), :] = d
            return dsum + jnp.sum(d, axis=0, keepdims=True)

        dsum = lax.fori_loop(0, nchunk, glu, jnp.zeros((1, LANE), F32))
        db_ref[...] = jnp.broadcast_to(dsum, db_ref.shape)

        def conv(c, _):
            r0 = pl.multiple_of(c * ROWS, ROWS)
            d = dpad[pl.ds(r0, ROWS), :]
            acc = jnp.zeros((ROWS, LANE), F32)
            for j in range(CV_KERNEL):
                acc = acc + w_ref[pl.ds(j, 1), :] * dpad[pl.ds(r0 + (CV_KERNEL - 1) - j, ROWS), :]
                zs = zpad[pl.ds(r0 + (CV_PAD - CV_KERNEL + 1) + j, ROWS), :]
                dw_ref[pl.ds(j, 1), :] += jnp.sum(d * zs, axis=0, keepdims=True)
            a = a_ref[pl.ds(r0, ROWS), :]
            sg = _sigmoid(g_ref[pl.ds(r0, ROWS), :])
            da_ref[pl.ds(r0, ROWS), :] = (acc * sg).astype(da_ref.dtype)
            dg_ref[pl.ds(r0, ROWS), :] = (acc * a * sg * (1.0 - sg)).astype(dg_ref.dtype)
            return 0

        lax.fori_loop(0, nchunk, conv, 0)

    blk = pl.BlockSpec((S, LANE), lambda j: (0, j))
    da, dg, dw, db = pl.pallas_call(
        body, name="cv_bwd", grid=(BRANCH_W // LANE,),
        in_specs=[_colblk(S, OFF_CV), _colblk(S, OFF_CV + BRANCH_W), pl.BlockSpec((32, LANE), lambda j: (0, j)), blk],
        out_specs=[blk, blk, pl.BlockSpec((32, LANE), lambda j: (0, j)), pl.BlockSpec((8, LANE), lambda j: (0, j))],
        out_shape=[jax.ShapeDtypeStruct((S, BRANCH_W), BF16), jax.ShapeDtypeStruct((S, BRANCH_W), BF16),
                   jax.ShapeDtypeStruct((32, BRANCH_W), F32), jax.ShapeDtypeStruct((8, BRANCH_W), F32)],
        scratch_shapes=[pltpu.VMEM((CV_PAD + S, LANE), F32), pltpu.VMEM((S + CV_PAD, LANE), F32)],
        compiler_params=_cp(("parallel",)),
    )(proj, proj, w32, dzc)
    return da, dg, dw, db[0]


def ln_silu_fwd(z, w, b, bcat, tm=512):
    S, C = z.shape
    tm = _tile(S, tm)

    def body(z_ref, w_ref, b_ref, _, o_ref):
        zv = z_ref[...]
        mu = jnp.mean(zv, axis=-1, keepdims=True)
        zc = zv - mu
        rstd = lax.rsqrt(jnp.mean(zc * zc, axis=-1, keepdims=True) + EPS)
        y = zc * rstd * w_ref[...] + b_ref[...]
        o_ref[...] = (y * _sigmoid(y)).astype(o_ref.dtype)

    row = pl.BlockSpec((tm, C), lambda i: (i, 0))
    vec = pl.BlockSpec((1, C), lambda i: (0, 0))
    return pl.pallas_call(
        body, name="ln_silu_fwd", grid=(S // tm,), in_specs=[row, vec, vec, ANY_SPACE],
        out_specs=pl.BlockSpec((tm, C), lambda i: (i, 1)), out_shape=jax.ShapeDtypeStruct(bcat.shape, bcat.dtype),
        input_output_aliases={3: 0}, compiler_params=_cp(("parallel",)),
    )(z, w.reshape(1, C), b.reshape(1, C), bcat)


def ln_silu_bwd(z, w, b, dbcat, tm=512):
    S, C = z.shape
    tm = _tile(S, tm)

    def body(z_ref, w_ref, b_ref, do_ref, dz_ref, dw_ref, db_ref):
        zv = z_ref[...]
        wv = w_ref[...]
        mu = jnp.mean(zv, axis=-1, keepdims=True)
        zc = zv - mu
        rstd = lax.rsqrt(jnp.mean(zc * zc, axis=-1, keepdims=True) + EPS)
        xh = zc * rstd
        y = xh * wv + b_ref[...]
        dy = do_ref[...].astype(F32) * _dsilu(y, _sigmoid(y))

        @pl.when(pl.program_id(0) == 0)
        def _():
            dw_ref[...] = jnp.zeros_like(dw_ref)
            db_ref[...] = jnp.zeros_like(db_ref)

        dw_ref[...] += jnp.sum(dy * xh, axis=0, keepdims=True)
        db_ref[...] += jnp.sum(dy, axis=0, keepdims=True)
        dxh = dy * wv
        dz_ref[...] = rstd * (dxh - jnp.mean(dxh, axis=-1, keepdims=True) - xh * jnp.mean(dxh * xh, axis=-1, keepdims=True))

    row = pl.BlockSpec((tm, C), lambda i: (i, 0))
    vec = pl.BlockSpec((1, C), lambda i: (0, 0))
    dz, dw, db = pl.pallas_call(
        body, name="ln_silu_bwd", grid=(S // tm,), in_specs=[row, vec, vec, pl.BlockSpec((tm, C), lambda i: (i, 1))],
        out_specs=[row, vec, vec],
        out_shape=[jax.ShapeDtypeStruct((S, C), F32), jax.ShapeDtypeStruct((1, C), F32), jax.ShapeDtypeStruct((1, C), F32)],
        compiler_params=_cp(("arbitrary",)),
    )(z, w.reshape(1, C), b.reshape(1, C), dbcat)
    return dz, dw.reshape(C), db.reshape(C)


PL_PAD = 16


def _pool_counts(r0, win):
    t = r0 + lax.broadcasted_iota(I32, (ROWS, LANE), 0)
    return jnp.minimum(t + 1, win).astype(F32)


def pool_fwd(proj, wg, scale, bcat):
    S = proj.shape[0]
    nchunk = S // ROWS

    def body(u_ref, w_ref, s_ref, _, o_ref, upad):
        g = pl.program_id(0)
        upad[pl.ds(0, PL_PAD), :] = jnp.zeros((PL_PAD, LANE), F32)

        def fill(c, _):
            r0 = pl.multiple_of(c * ROWS, ROWS)
            upad[pl.ds(PL_PAD + r0, ROWS), :] = u_ref[pl.ds(r0, ROWS), :]
            return 0

        lax.fori_loop(0, nchunk, fill, 0)
        wb = w_ref[...].astype(BF16)
        for gi, win in enumerate(POOL_WINDOWS):
            @pl.when(g == gi)
            def _(win=win):
                def chunk(c, _):
                    r0 = pl.multiple_of(c * ROWS, ROWS)
                    u = upad[pl.ds(PL_PAD + r0, ROWS), :]
                    ws = u
                    for j in range(1, win):
                        ws = ws + upad[pl.ds(PL_PAD + r0 - j, ROWS), :]
                    pooled = ws / _pool_counts(r0, win) - u
                    o_ref[pl.ds(r0, ROWS), :] = (_dot(pooled.astype(BF16), wb, NN) * s_ref[...]).astype(o_ref.dtype)
                    return 0

                lax.fori_loop(0, nchunk, chunk, 0)

    return pl.pallas_call(
        body, name="pool_fwd", grid=(len(POOL_WINDOWS),),
        in_specs=[_colblk(S, OFF_PL), pl.BlockSpec((None, LANE, LANE), lambda j: (j, 0, 0)), pl.BlockSpec((1, LANE), lambda j: (0, j)),
                  ANY_SPACE],
        out_specs=pl.BlockSpec((S, LANE), lambda j: (0, 2 * BRANCH_W // LANE + j)),
        out_shape=jax.ShapeDtypeStruct(bcat.shape, bcat.dtype), input_output_aliases={3: 0},
        scratch_shapes=[pltpu.VMEM((PL_PAD + S, LANE), F32)],
        compiler_params=_cp(("parallel",)),
    )(proj, wg, scale.reshape(1, BRANCH_W), bcat)


def pool_bwd(proj, wg, scale, dbcat, dproj):
    S = proj.shape[0]
    nchunk = S // ROWS

    def body(u_ref, w_ref, s_ref, dy_ref, _, du_ref, dw_ref, ds_ref, upad, dpn, nd):
        g = pl.program_id(0)
        upad[pl.ds(0, PL_PAD), :] = jnp.zeros((PL_PAD, LANE), F32)
        dpn[pl.ds(S, PL_PAD), :] = jnp.zeros((PL_PAD, LANE), F32)

        def fill(c, _):
            r0 = pl.multiple_of(c * ROWS, ROWS)
            upad[pl.ds(PL_PAD + r0, ROWS), :] = u_ref[pl.ds(r0, ROWS), :]
            return 0

        lax.fori_loop(0, nchunk, fill, 0)
        wb = w_ref[...].astype(BF16)
        sv = s_ref[...]
        for gi, win in enumerate(POOL_WINDOWS):
            @pl.when(g == gi)
            def _(win=win):
                def chunk(c, carry):
                    dw, dsc = carry
                    r0 = pl.multiple_of(c * ROWS, ROWS)
                    u = upad[pl.ds(PL_PAD + r0, ROWS), :]
                    ws = u
                    for j in range(1, win):
                        ws = ws + upad[pl.ds(PL_PAD + r0 - j, ROWS), :]
                    cnt = _pool_counts(r0, win)
                    pooled = (ws / cnt - u).astype(BF16)
                    dyv = dy_ref[pl.ds(r0, ROWS), :].astype(F32)
                    dsc = dsc + jnp.sum(dyv * _dot(pooled, wb, NN), axis=0, keepdims=True)
                    dys = (dyv * sv).astype(BF16)
                    dw = dw + _dot(pooled, dys, TN)
                    dp = _dot(dys, wb, NT)
                    dpn[pl.ds(r0, ROWS), :] = dp / cnt
                    nd[pl.ds(r0, ROWS), :] = -dp
                    return dw, dsc

                dw, dsc = lax.fori_loop(0, nchunk, chunk, (jnp.zeros((LANE, LANE), F32), jnp.zeros((1, LANE), F32)))
                dw_ref[...] = dw
                ds_ref[...] = jnp.broadcast_to(dsc, ds_ref.shape)

                def spread(c, _):
                    r0 = pl.multiple_of(c * ROWS, ROWS)
                    acc = nd[pl.ds(r0, ROWS), :]
                    for j in range(win):
                        acc = acc + dpn[pl.ds(r0 + j, ROWS), :]
                    du_ref[pl.ds(r0, ROWS), :] = acc.astype(du_ref.dtype)
                    return 0

                lax.fori_loop(0, nchunk, spread, 0)

    dproj, dw, ds = pl.pallas_call(
        body, name="pool_bwd", grid=(len(POOL_WINDOWS),),
        in_specs=[_colblk(S, OFF_PL), pl.BlockSpec((None, LANE, LANE), lambda j: (j, 0, 0)), pl.BlockSpec((1, LANE), lambda j: (0, j)),
                  pl.BlockSpec((S, LANE), lambda j: (0, 2 * BRANCH_W // LANE + j)), ANY_SPACE],
        out_specs=[_colblk(S, OFF_PL), pl.BlockSpec((None, LANE, LANE), lambda j: (j, 0, 0)), pl.BlockSpec((8, LANE), lambda j: (0, j))],
        out_shape=[jax.ShapeDtypeStruct(dproj.shape, dproj.dtype), jax.ShapeDtypeStruct((len(POOL_WINDOWS), LANE, LANE), F32),
                   jax.ShapeDtypeStruct((8, BRANCH_W), F32)],
        input_output_aliases={4: 0},
        scratch_shapes=[pltpu.VMEM((PL_PAD + S, LANE), F32), pltpu.VMEM((S + PL_PAD, LANE), F32), pltpu.VMEM((S, LANE), F32)],
        compiler_params=_cp(("parallel",)),
    )(proj, wg, scale.reshape(1, BRANCH_W), dbcat, dproj)
    return dproj, dw, ds[0]


LR_PAD = 8
SCAN_TILES = 4
GELU_C = math.sqrt(2.0 / math.pi)
GELU_A = 0.044715


def _gelu(y):
    return 0.5 * y * (1.0 + jnp.tanh(GELU_C * (y + GELU_A * y * y * y)))


def _dgelu(y):
    t = jnp.tanh(GELU_C * (y + GELU_A * y * y * y))
    return 0.5 * (1.0 + t) + 0.5 * y * (1.0 - t * t) * GELU_C * (1.0 + 3.0 * GELU_A * y * y)


def _lru_gates(xpad, r0, cw_ref, cb, wa, ba, wx, bx, sp8):
    xc = jnp.broadcast_to(cb, (ROWS, LANE))
    for j in range(LRU_CONV):
        xc = xc + cw_ref[pl.ds(j, 1), :] * xpad[pl.ds(r0 + (LR_PAD - LRU_CONV + 1) + j, ROWS), :]
    xb = xc.astype(BF16)
    r = _sigmoid(_dot(xb, wa, NN) + ba)
    ig = _sigmoid(_dot(xb, wx, NN) + bx)
    la = -sp8 * r
    a = jnp.exp(la)
    s = jnp.sqrt(-jnp.tanh(la) * (a * a + 1.0))
    return xc, r, ig, a, s


def _tile_scan(a, b, r8, up):
    for s in (1, 2, 4):
        keep = (r8 < 8 - s) if up else (r8 >= s)
        shift = 8 - s if up else s
        a_sh = jnp.where(keep, pltpu.roll(a, shift, 0), 1.0)
        b_sh = jnp.where(keep, pltpu.roll(b, shift, 0), 0.0)
        b = b + a * b_sh
        a = a * a_sh
    return a, b


def lru_fwd(proj, cw8, cb, wa_bd, ba, wx_bd, bx, sp8, bcat):
    S = proj.shape[0]
    nchunk = S // ROWS

    def body(x_ref, y_ref, cw_ref, cb_ref, wa_ref, ba_ref, wx_ref, bx_ref, sp_ref, _, o_ref, h_ref, xpad, a_s):
        xpad[pl.ds(0, LR_PAD), :] = jnp.zeros((LR_PAD, LANE), F32)

        def fill(c, _):
            r0 = pl.multiple_of(c * ROWS, ROWS)
            xpad[pl.ds(LR_PAD + r0, ROWS), :] = x_ref[pl.ds(r0, ROWS), :]
            return 0

        lax.fori_loop(0, nchunk, fill, 0)
        wa = wa_ref[...].astype(BF16)
        wx = wx_ref[...].astype(BF16)

        def gates(c, _):
            r0 = pl.multiple_of(c * ROWS, ROWS)
            xc, r, ig, a, s = _lru_gates(xpad, r0, cw_ref, cb_ref[...], wa, ba_ref[...], wx, bx_ref[...], sp_ref[...])
            a_s[pl.ds(r0, ROWS), :] = a
            h_ref[pl.ds(r0, ROWS), :] = s * (ig * xc)
            return 0

        lax.fori_loop(0, nchunk, gates, 0)

        r8 = lax.broadcasted_iota(I32, (8, LANE), 0)

        def scan(i, h):
            bases = [pl.multiple_of(i * (8 * SCAN_TILES) + 8 * j, 8) for j in range(SCAN_TILES)]
            maps = [_tile_scan(a_s[pl.ds(b, 8), :], h_ref[pl.ds(b, 8), :], r8, False) for b in bases]
            for b, (ca, cb_) in zip(bases, maps):
                out = cb_ + ca * h
                h_ref[pl.ds(b, 8), :] = out
                h = out[7:8, :]
            return h

        lax.fori_loop(0, S // (8 * SCAN_TILES), scan, jnp.zeros((1, LANE), F32))

        def gate_out(c, _):
            r0 = pl.multiple_of(c * ROWS, ROWS)
            o_ref[pl.ds(r0, ROWS), :] = (h_ref[pl.ds(r0, ROWS), :] * _gelu(y_ref[pl.ds(r0, ROWS), :])).astype(o_ref.dtype)
            return 0

        lax.fori_loop(0, nchunk, gate_out, 0)

    vec = pl.BlockSpec((1, LANE), lambda j: (0, j))
    mat = pl.BlockSpec((None, LANE, LANE), lambda j: (j, 0, 0))
    blk = pl.BlockSpec((S, LANE), lambda j: (0, j))
    return pl.pallas_call(
        body, name="lru_fwd", grid=(BRANCH_W // LANE,),
        in_specs=[_colblk(S, OFF_LX), _colblk(S, OFF_LY), pl.BlockSpec((8, LANE), lambda j: (0, j)), vec, mat, vec, mat, vec, vec,
                  ANY_SPACE],
        out_specs=[pl.BlockSpec((S, LANE), lambda j: (0, 3 * BRANCH_W // LANE + j)), blk],
        out_shape=[jax.ShapeDtypeStruct(bcat.shape, bcat.dtype), jax.ShapeDtypeStruct((S, BRANCH_W), F32)],
        input_output_aliases={9: 0},
        scratch_shapes=[pltpu.VMEM((LR_PAD + S, LANE), F32), pltpu.VMEM((S, LANE), F32)],
        compiler_params=_cp(("parallel",)),
    )(proj, proj, cw8, cb.reshape(1, -1), wa_bd, ba.reshape(1, -1), wx_bd, bx.reshape(1, -1), sp8.reshape(1, -1), bcat)


def lru_bwd(proj, cw8, cb, wa_bd, ba, wx_bd, bx, sp8, h, dbcat):
    S = proj.shape[0]
    nchunk = S // ROWS

    def body(x_ref, y_ref, cw_ref, cb_ref, wa_ref, ba_ref, wx_ref, bx_ref, sp_ref, h_ref, do_ref,
             dx_ref, dy_ref, dcw_ref, dcb_ref, dwa_ref, dba_ref, dwx_ref, dbx_ref, dsp_ref,
             xpad, a_s, g_s, hpad, dxc):
        xpad[pl.ds(0, LR_PAD), :] = jnp.zeros((LR_PAD, LANE), F32)
        hpad[pl.ds(0, LR_PAD), :] = jnp.zeros((LR_PAD, LANE), F32)
        dxc[pl.ds(S, LR_PAD), :] = jnp.zeros((LR_PAD, LANE), F32)
        dcw_ref[...] = jnp.zeros_like(dcw_ref)
        wa = wa_ref[...].astype(BF16)
        wx = wx_ref[...].astype(BF16)
        cbv, bav, bxv, spv = cb_ref[...], ba_ref[...], bx_ref[...], sp_ref[...]

        def fill(c, _):
            r0 = pl.multiple_of(c * ROWS, ROWS)
            xpad[pl.ds(LR_PAD + r0, ROWS), :] = x_ref[pl.ds(r0, ROWS), :]
            hv = h_ref[pl.ds(r0, ROWS), :]
            hpad[pl.ds(LR_PAD + r0, ROWS), :] = hv
            yv = y_ref[pl.ds(r0, ROWS), :]
            dov = do_ref[pl.ds(r0, ROWS), :].astype(F32)
            g_s[pl.ds(r0, ROWS), :] = dov * _gelu(yv)
            dy_ref[pl.ds(r0, ROWS), :] = (dov * hv * _dgelu(yv)).astype(dy_ref.dtype)
            return 0

        lax.fori_loop(0, nchunk, fill, 0)

        def gates(c, _):
            r0 = pl.multiple_of(c * ROWS, ROWS)
            _, _, _, a, _ = _lru_gates(xpad, r0, cw_ref, cbv, wa, bav, wx, bxv, spv)
            a_s[pl.ds(r0, ROWS), :] = a
            return 0

        lax.fori_loop(0, nchunk, gates, 0)

        r8 = lax.broadcasted_iota(I32, (8, LANE), 0)

        def rscan(i, carry):
            bases = [pl.multiple_of(S - 8 - i * (8 * SCAN_TILES) - 8 * j, 8) for j in range(SCAN_TILES)]
            firsts, maps = [], []
            for b in bases:
                a8 = a_s[pl.ds(b, 8), :]
                above = jnp.where(r8 < 7, pltpu.roll(a8, 7, 0), 1.0)
                firsts.append(a8[0:1, :])
                maps.append(_tile_scan(above, g_s[pl.ds(b, 8), :], r8, True))
            for b, a0, (ca, cb_) in zip(bases, firsts, maps):
                out = cb_ + ca * carry
                g_s[pl.ds(b, 8), :] = out
                carry = a0 * out[0:1, :]
            return carry

        lax.fori_loop(0, S // (8 * SCAN_TILES), rscan, jnp.zeros((1, LANE), F32))

        def chain(c, carry):
            dwa, dwx, dba, dbx, dsp, dcb = carry
            r0 = pl.multiple_of(c * ROWS, ROWS)
            xc, r, ig, a, s = _lru_gates(xpad, r0, cw_ref, cbv, wa, bav, wx, bxv, spv)
            gt = g_s[pl.ds(r0, ROWS), :]
            hprev = hpad[pl.ds(r0 + LR_PAD - 1, ROWS), :]
            da = gt * hprev - gt * ig * xc * (a / s)
            dig = gt * s * xc
            dla = da * a
            dsp = dsp + jnp.sum(-dla * r, axis=0, keepdims=True)
            dpr = (-dla * spv) * r * (1.0 - r)
            dpi = dig * ig * (1.0 - ig)
            dprb, dpib, xb = dpr.astype(BF16), dpi.astype(BF16), xc.astype(BF16)
            d = gt * s * ig + _dot(dprb, wa, NT) + _dot(dpib, wx, NT)
            dwa = dwa + _dot(xb, dprb, TN)
            dwx = dwx + _dot(xb, dpib, TN)
            dba = dba + jnp.sum(dpr, axis=0, keepdims=True)
            dbx = dbx + jnp.sum(dpi, axis=0, keepdims=True)
            dcb = dcb + jnp.sum(d, axis=0, keepdims=True)
            dxc[pl.ds(r0, ROWS), :] = d
            for j in range(LRU_CONV):
                xs = xpad[pl.ds(r0 + (LR_PAD - LRU_CONV + 1) + j, ROWS), :]
                dcw_ref[pl.ds(j, 1), :] += jnp.sum(d * xs, axis=0, keepdims=True)
            return dwa, dwx, dba, dbx, dsp, dcb

        zm, zv = jnp.zeros((LANE, LANE), F32), jnp.zeros((1, LANE), F32)
        dwa, dwx, dba, dbx, dsp, dcb = lax.fori_loop(0, nchunk, chain, (zm, zm, zv, zv, zv, zv))
        dwa_ref[...] = dwa
        dwx_ref[...] = dwx
        dba_ref[...] = jnp.broadcast_to(dba, dba_ref.shape)
        dbx_ref[...] = jnp.broadcast_to(dbx, dbx_ref.shape)
        dsp_ref[...] = jnp.broadcast_to(dsp, dsp_ref.shape)
        dcb_ref[...] = jnp.broadcast_to(dcb, dcb_ref.shape)

        def convt(c, _):
            r0 = pl.multiple_of(c * ROWS, ROWS)
            acc = jnp.zeros((ROWS, LANE), F32)
            for j in range(LRU_CONV):
                acc = acc + cw_ref[pl.ds(j, 1), :] * dxc[pl.ds(r0 + (LRU_CONV - 1) - j, ROWS), :]
            dx_ref[pl.ds(r0, ROWS), :] = acc.astype(dx_ref.dtype)
            return 0

        lax.fori_loop(0, nchunk, convt, 0)

    vec = pl.BlockSpec((1, LANE), lambda j: (0, j))
    vec8 = pl.BlockSpec((8, LANE), lambda j: (0, j))
    mat = pl.BlockSpec((None, LANE, LANE), lambda j: (j, 0, 0))
    blk = pl.BlockSpec((S, LANE), lambda j: (0, j))
    nblk = BRANCH_W // LANE
    v8 = jax.ShapeDtypeStruct((8, BRANCH_W), F32)
    m4 = jax.ShapeDtypeStruct((nblk, LANE, LANE), F32)
    big = jax.ShapeDtypeStruct((S, BRANCH_W), BF16)
    seq = pltpu.VMEM((S, LANE), F32)
    dx, dy, dcw, dcb, dwa, dba, dwx, dbx, dsp = pl.pallas_call(
        body, name="lru_bwd", grid=(nblk,),
        in_specs=[_colblk(S, OFF_LX), _colblk(S, OFF_LY), vec8, vec, mat, vec, mat, vec, vec, blk,
                  pl.BlockSpec((S, LANE), lambda j: (0, 3 * BRANCH_W // LANE + j))],
        out_specs=[blk, blk, vec8, vec8, mat, vec8, mat, vec8, vec8],
        out_shape=[big, big, v8, v8, m4, v8, m4, v8, v8],
        scratch_shapes=[pltpu.VMEM((LR_PAD + S, LANE), F32), seq, seq, pltpu.VMEM((LR_PAD + S, LANE), F32),
                        pltpu.VMEM((S + LR_PAD, LANE), F32)],
        compiler_params=_cp(("parallel",)),
    )(proj, proj, cw8, cb.reshape(1, -1), wa_bd, ba.reshape(1, -1), wx_bd, bx.reshape(1, -1), sp8.reshape(1, -1), h, dbcat)
    return dx, dy, dcw, dcb[0], dwa, dba[0], dwx, dbx[0], dsp[0]


MG_COLS = 512
N_BRANCH = 4
BCAT_W = N_BRANCH * BRANCH_W


def merge_fwd(bcat, wb, proj, gate_b, tm=1024, after=()):
    S = proj.shape[0]
    tm = _tile(S, tm)
    halves = D_MODEL // MG_COLS

    def body(a_ref, w_ref, g_ref, gb_ref, *rest):
        up_ref, o_ref, acc_ref = rest[len(after):]
        k = pl.program_id(2)
        up = _dot(a_ref[...], w_ref[...], NN)
        up_ref[...] = up.astype(up_ref.dtype)
        term = _sigmoid(g_ref[...] + gb_ref[pl.ds(k, 1), :]) * up

        @pl.when(k == 0)
        def _():
            acc_ref[...] = term

        @pl.when(k > 0)
        def _():
            acc_ref[...] += term

        @pl.when(k == N_BRANCH - 1)
        def _():
            o_ref[...] = acc_ref[...].astype(o_ref.dtype)

    return pl.pallas_call(
        body, name="merge_fwd", grid=(S // tm, halves, N_BRANCH),
        in_specs=[pl.BlockSpec((tm, BRANCH_W), lambda i, j, k: (i, k)),
                  pl.BlockSpec((None, BRANCH_W, MG_COLS), lambda i, j, k: (k, 0, j)),
                  pl.BlockSpec((tm, MG_COLS), lambda i, j, k: (i, OFF_GATE // MG_COLS + k * halves + j)),
                  pl.BlockSpec((N_BRANCH, MG_COLS), lambda i, j, k: (0, j))] + [ANY_SPACE] * len(after),
        out_specs=[pl.BlockSpec((tm, MG_COLS), lambda i, j, k: (i, k * halves + j)),
                   pl.BlockSpec((tm, MG_COLS), lambda i, j, k: (i, j))],
        out_shape=[jax.ShapeDtypeStruct((S, N_BRANCH * D_MODEL), BF16), jax.ShapeDtypeStruct((S, D_MODEL), BF16)],
        scratch_shapes=[pltpu.VMEM((tm, MG_COLS), F32)],
        compiler_params=_cp(("parallel", "parallel", "arbitrary")),
    )(bcat, wb, proj, gate_b, *after)


def merge_bwd(dmerged, ups, proj, gate_b, tm=1024):
    S = proj.shape[0]
    tm = _tile(S, tm)
    halves = D_MODEL // MG_COLS

    def body(dm_ref, u_ref, g_ref, gb_ref, du_ref, dg_ref, dgb_ref):
        k = pl.program_id(0)

        @pl.when(pl.program_id(2) == 0)
        def _():
            dgb_ref[...] = jnp.zeros_like(dgb_ref)

        dm = dm_ref[...].astype(F32)
        sg = _sigmoid(g_ref[...] + gb_ref[pl.ds(k, 1), :])
        du_ref[...] = (dm * sg).astype(du_ref.dtype)
        dgk = dm * u_ref[...] * sg * (1.0 - sg)
        dg_ref[...] = dgk.astype(dg_ref.dtype)
        dgb_ref[...] += jnp.broadcast_to(jnp.sum(dgk, axis=0, keepdims=True), dgb_ref.shape)

    dups, dproj, dgb = pl.pallas_call(
        body, name="merge_bwd", grid=(N_BRANCH, halves, S // tm),
        in_specs=[pl.BlockSpec((tm, MG_COLS), lambda k, j, i: (i, j)),
                  pl.BlockSpec((tm, MG_COLS), lambda k, j, i: (i, k * halves + j)),
                  pl.BlockSpec((tm, MG_COLS), lambda k, j, i: (i, OFF_GATE // MG_COLS + k * halves + j)),
                  pl.BlockSpec((N_BRANCH, MG_COLS), lambda k, j, i: (0, j))],
        out_specs=[pl.BlockSpec((tm, MG_COLS), lambda k, j, i: (i, k * halves + j)),
                   pl.BlockSpec((tm, MG_COLS), lambda k, j, i: (i, OFF_GATE // MG_COLS + k * halves + j)),
                   pl.BlockSpec((8, MG_COLS), lambda k, j, i: (k, j))],
        out_shape=[jax.ShapeDtypeStruct((S, N_BRANCH * D_MODEL), BF16), jax.ShapeDtypeStruct((S, IN_W), BF16),
                   jax.ShapeDtypeStruct((8 * N_BRANCH, D_MODEL), F32)],
        compiler_params=_cp(("parallel", "parallel", "arbitrary")),
    )(dmerged, ups, proj, gate_b)
    return dups, dproj, dgb.reshape(N_BRANCH, 8, D_MODEL)[:, 0]


def attn_fwd(q, kv, tm=512):
    S = q.shape[0]
    M = kv.shape[0]
    tm = _tile(S, tm)
    scale = XA_HD ** -0.5

    def body(q_ref, kv_ref, o_ref):
        for hh in range(XA_HEADS):
            cs = pl.ds(hh * XA_HD, XA_HD)
            qh = q_ref[:, cs]
            kh = kv_ref[:, cs]
            vh = kv_ref[:, pl.ds(D_MODEL + hh * XA_HD, XA_HD)]
            s = _dot(qh, kh, NT) * scale
            p = jnp.exp(s - jnp.max(s, axis=-1, keepdims=True))
            p = p / jnp.sum(p, axis=-1, keepdims=True)
            o_ref[:, cs] = _dot(p.astype(BF16), vh, NN).astype(o_ref.dtype)

    return pl.pallas_call(
        body, name="attn_fwd", grid=(S // tm,),
        in_specs=[pl.BlockSpec((tm, D_MODEL), lambda i: (i, 0)), pl.BlockSpec((M, 2 * D_MODEL), lambda i: (0, 0))],
        out_specs=pl.BlockSpec((tm, D_MODEL), lambda i: (i, 0)), out_shape=jax.ShapeDtypeStruct((S, D_MODEL), BF16),
        compiler_params=_cp(("parallel",)),
    )(q, kv)


def attn_bwd(q, kv, do, tm=512):
    S = q.shape[0]
    M = kv.shape[0]
    tm = _tile(S, tm)
    scale = XA_HD ** -0.5

    def body(q_ref, kv_ref, do_ref, dq_ref, dkv_ref):
        @pl.when(pl.program_id(0) == 0)
        def _():
            dkv_ref[...] = jnp.zeros_like(dkv_ref)

        for hh in range(XA_HEADS):
            cs = pl.ds(hh * XA_HD, XA_HD)
            vs = pl.ds(D_MODEL + hh * XA_HD, XA_HD)
            qh = q_ref[:, cs]
            kh = kv_ref[:, cs]
            vh = kv_ref[:, vs]
            doh = do_ref[:, cs]
            s = _dot(qh, kh, NT) * scale
            p = jnp.exp(s - jnp.max(s, axis=-1, keepdims=True))
            p = p / jnp.sum(p, axis=-1, keepdims=True)
            dp = _dot(doh, vh, NT)
            ds = (p * (dp - jnp.sum(dp * p, axis=-1, keepdims=True)) * scale).astype(BF16)
            dq_ref[:, cs] = _dot(ds, kh, NN).astype(dq_ref.dtype)
            dkv_ref[:, cs] += _dot(ds, qh, TN)
            dkv_ref[:, vs] += _dot(p.astype(BF16), doh, TN)

    row = pl.BlockSpec((tm, D_MODEL), lambda i: (i, 0))
    full = pl.BlockSpec((M, 2 * D_MODEL), lambda i: (0, 0))
    return pl.pallas_call(
        body, name="attn_bwd", grid=(S // tm,), in_specs=[row, full, row], out_specs=[row, full],
        out_shape=[jax.ShapeDtypeStruct((S, D_MODEL), BF16), jax.ShapeDtypeStruct((M, 2 * D_MODEL), F32)],
        compiler_params=_cp(("arbitrary",)),
    )(q, kv, do)


def sum_parts(parts, own=None, tm=256):
    n, R, C = parts.shape
    tm = _tile(R, tm)
    has_own = own is not None

    def body(*refs):
        p_ref, o_ref = refs[0], refs[-1]
        acc = refs[1][...].astype(F32) if has_own else p_ref[0].astype(F32)
        for j in range(0 if has_own else 1, n):
            acc = acc + p_ref[j].astype(F32)
        o_ref[...] = acc

    row = pl.BlockSpec((tm, C), lambda i: (i, 0))
    return pl.pallas_call(
        body, name="sum_parts", grid=(R // tm,),
        in_specs=[pl.BlockSpec((n, tm, C), lambda i: (0, i, 0))] + ([row] if has_own else []), out_specs=row,
        out_shape=jax.ShapeDtypeStruct((R, C), F32), compiler_params=_cp(("parallel",)),
    )(*([parts, own] if has_own else [parts]))


def adamw(w, g, m, v, tm=256):
    R, C = w.shape
    tm = _tile(R, tm)
    c1 = 1.0 / (1.0 - ADAM_B1 ** ADAM_STEP)
    c2 = 1.0 / (1.0 - ADAM_B2 ** ADAM_STEP)

    def body(w_ref, g_ref, m_ref, v_ref, d_ref, nm_ref, nv_ref):
        gv = g_ref[...]
        nm = ADAM_B1 * m_ref[...] + (1.0 - ADAM_B1) * gv
        nv = ADAM_B2 * v_ref[...] + (1.0 - ADAM_B2) * (gv * gv)
        nm_ref[...] = nm
        nv_ref[...] = nv
        d_ref[...] = -ADAM_LR * ((nm * c1) / (jnp.sqrt(nv * c2) + ADAM_EPS) + ADAM_WD * w_ref[...])

    blk = pl.BlockSpec((tm, C), lambda i: (i, 0))
    sd = jax.ShapeDtypeStruct((R, C), F32)
    return pl.pallas_call(
        body, name="adamw", grid=(R // tm,), in_specs=[blk] * 4, out_specs=[blk] * 3, out_shape=[sd] * 3,
        compiler_params=_cp(("parallel",)),
    )(w, g, m, v)


ANY = pl.BlockSpec(memory_space=pl.ANY)


def _place():
    return lax.axis_index("x"), lax.axis_index("y"), lax.axis_index("c")


def _slot(px, py, pc):
    return 4 * px + 2 * py + pc


def all_gather(name, shards, after=()):
    n = len(shards)
    n_in = n + len(after)

    def body(*refs):
        x_refs, out_refs = refs[:n], refs[n_in:n_in + n]
        send_sems, recv_sems, local_sems = refs[n_in + n:]
        x, y, c = _place()
        me, sibling = (x, y, c), (x, y, 1 - c)
        chips = [(1 - x, y), (x, 1 - y), (1 - x, 1 - y)]

        def copy(a, k, block, to, src=None):
            rows = out_refs[a].at[_slot(*block)]
            return pltpu.make_async_remote_copy(
                src_ref=rows if src is None else src, dst_ref=rows,
                send_sem=send_sems.at[7 * a + k], recv_sem=recv_sems.at[7 * a + k],
                device_id=to, device_id_type=MESH)

        mine = [pltpu.make_async_copy(x_refs[a], out_refs[a].at[_slot(*me)], local_sems.at[a]) for a in range(n)]
        for cp in mine:
            cp.start()
        first = []
        for a in range(n):
            first.append(copy(a, 0, me, sibling, src=x_refs[a]))
            first += [copy(a, 1 + j, me, (*chip, c), src=x_refs[a]) for j, chip in enumerate(chips)]
        for cp in first:
            cp.start()
        passed = []
        for a in range(n):
            for j, chip in enumerate(chips):
                copy(a, 1 + j, (*chip, c), me).wait_recv()
                cp = copy(a, 4 + j, (*chip, c), sibling)
                cp.start()
                passed.append(cp)
        for a in range(n):
            copy(a, 0, sibling, me).wait_recv()
            for j, chip in enumerate(chips):
                copy(a, 4 + j, (*chip, 1 - c), me).wait_recv()
        for cp in first + passed:
            cp.wait_send()
        for cp in mine:
            cp.wait()

    return pl.pallas_call(
        body, name=name, in_specs=[ANY] * n_in, out_specs=[ANY] * n,
        out_shape=[jax.ShapeDtypeStruct((N_DEV, *s.shape), s.dtype) for s in shards],
        scratch_shapes=[pltpu.SemaphoreType.DMA((7 * n,)), pltpu.SemaphoreType.DMA((7 * n,)), pltpu.SemaphoreType.DMA((n,))],
    )(*shards, *after)


HBM = pl.BlockSpec(memory_space=pltpu.HBM)
SEM = pl.BlockSpec(memory_space=pltpu.SEMAPHORE)
EFFECT = pltpu.SideEffectType.DATAFLOW_SIDE_EFFECTING
N_PEER = N_DEV - 1
RELATIONS = [(dx, dy, dc) for dx in (0, 1) for dy in (0, 1) for dc in (0, 1)][1:]


def _peer(place, rel):
    return tuple(1 - v if d else v for v, d in zip(place, rel))


def gather_start(name, shards, me, before):
    n = len(shards)

    def body(*refs):
        x_refs, land_refs = refs[:n], refs[n:2 * n]
        send_sems, recv_sems = refs[2 * n + len(before):2 * n + len(before) + 2]
        token = refs[-1]
        place = _place()
        mine = _slot(*place)
        for a in range(n):
            for rel in RELATIONS:
                pltpu.make_async_remote_copy(
                    src_ref=x_refs[a], dst_ref=land_refs[a].at[mine], send_sem=send_sems.at[a], recv_sem=recv_sems.at[a],
                    device_id=_peer(place, rel), device_id_type=MESH).start()
        token[...] = jnp.zeros_like(token)

    lands = [lax.dynamic_update_index_in_dim(lax.empty((N_DEV, *s.shape), s.dtype), s, me, 0) for s in shards]
    outs = pl.pallas_call(
        body, name=name,
        in_specs=[HBM] * (2 * n) + [ANY] * len(before),
        out_specs=[SEM, SEM] + [HBM] * (2 * n) + [pl.BlockSpec(memory_space=pltpu.VMEM)],
        out_shape=[pltpu.SemaphoreType.DMA((n,)), pltpu.SemaphoreType.DMA((n,))]
        + [pltpu.HBM(t.shape, t.dtype) for t in (*shards, *lands)] + [jax.ShapeDtypeStruct((8, LANE), F32)],
        input_output_aliases={i: 2 + i for i in range(2 * n)},
        compiler_params=pltpu.CompilerParams(has_side_effects=EFFECT),
    )(*[pltpu.with_memory_space_constraint(t, pltpu.HBM) for t in (*shards, *lands)], *before)
    return (outs[0], outs[1], outs[2:2 + n], outs[2 + n:2 + 2 * n]), outs[-1]


def gather_wait(name, state, after):
    send_sems, recv_sems, shards, lands = state
    n = len(shards)

    def body(*refs):
        land_refs = refs[n:2 * n]
        s_sems, r_sems = refs[2 * n:2 * n + 2]
        place = _place()
        for a in range(n):
            seven = land_refs[a].at[pl.ds(0, N_PEER)]
            cp = pltpu.make_async_remote_copy(
                src_ref=seven, dst_ref=seven, send_sem=s_sems.at[a], recv_sem=r_sems.at[a], device_id=place, device_id_type=MESH)
            cp.wait_send()
            cp.wait_recv()

    outs = pl.pallas_call(
        body, name=name,
        in_specs=[HBM] * (2 * n) + [SEM, SEM] + [ANY] * len(after), out_specs=[HBM] * (2 * n),
        out_shape=[pltpu.HBM(t.shape, t.dtype) for t in (*shards, *lands)],
        input_output_aliases={i: i for i in range(2 * n)},
        compiler_params=pltpu.CompilerParams(has_side_effects=EFFECT),
    )(*shards, *lands, send_sems, recv_sems, *after)
    return outs[n:]


def exchange_start(name, grads, before):
    n = len(grads)

    def body(*refs):
        g_refs, land_refs = refs[:n], refs[n:2 * n]
        send_sems, recv_sems = refs[2 * n + len(before):2 * n + len(before) + 2]
        token = refs[-1]
        place = _place()
        for a in range(n):
            for r, rel in enumerate(RELATIONS):
                p = _peer(place, rel)
                pltpu.make_async_remote_copy(
                    src_ref=g_refs[a].at[_slot(*p)], dst_ref=land_refs[a].at[r],
                    send_sem=send_sems.at[a], recv_sem=recv_sems.at[a], device_id=p, device_id_type=MESH).start()
        token[...] = jnp.zeros_like(token)

    lands = [lax.empty((N_PEER, *g.shape[1:]), g.dtype) for g in grads]
    outs = pl.pallas_call(
        body, name=name,
        in_specs=[HBM] * (2 * n) + [ANY] * len(before),
        out_specs=[SEM, SEM] + [HBM] * (2 * n) + [pl.BlockSpec(memory_space=pltpu.VMEM)],
        out_shape=[pltpu.SemaphoreType.DMA((n,)), pltpu.SemaphoreType.DMA((n,))]
        + [pltpu.HBM(g.shape, g.dtype) for g in grads] + [pltpu.HBM(t.shape, t.dtype) for t in lands]
        + [jax.ShapeDtypeStruct((8, LANE), F32)],
        input_output_aliases={i: 2 + i for i in range(2 * n)},
        compiler_params=pltpu.CompilerParams(has_side_effects=EFFECT),
    )(*[pltpu.with_memory_space_constraint(t, pltpu.HBM) for t in (*grads, *lands)], *before)
    return (outs[0], outs[1], outs[2:2 + n], outs[2 + n:2 + 2 * n]), outs[-1]


def exchange_wait(name, state, after):
    send_sems, recv_sems, grads, lands = state
    n = len(grads)

    def body(*refs):
        g_refs, land_refs = refs[:n], refs[n:2 * n]
        s_sems, r_sems = refs[2 * n:2 * n + 2]
        place = _place()
        for a in range(n):
            cp = pltpu.make_async_remote_copy(
                src_ref=g_refs[a].at[pl.ds(0, N_PEER)], dst_ref=land_refs[a],
                send_sem=s_sems.at[a], recv_sem=r_sems.at[a], device_id=place, device_id_type=MESH)
            cp.wait_send()
            cp.wait_recv()

    outs = pl.pallas_call(
        body, name=name,
        in_specs=[HBM] * (2 * n) + [SEM, SEM, ANY], out_specs=[HBM] * (2 * n),
        out_shape=[pltpu.HBM(t.shape, t.dtype) for t in (*grads, *lands)],
        input_output_aliases={i: i for i in range(2 * n)},
        compiler_params=pltpu.CompilerParams(has_side_effects=EFFECT),
    )(*grads, *lands, send_sems, recv_sems, after)
    return outs[:n], outs[n:]


WEIGHTS = ['norm_mix_w', 'w_in', 'hg_lb_raw', 'hg_norm_w', 'cv_dw_w', 'cv_dw_b', 'cv_ln_w', 'cv_ln_b', 'pl_w', 'pl_scale',
           'lru_conv_w', 'lru_conv_b', 'lru_wa', 'lru_ba', 'lru_wx', 'lru_bx', 'lru_lambda', 'gate_b', 'w_branch', 'w_out',
           'norm_mem_w', 'mem_norm_w', 'xa_wq', 'xa_wkv', 'xa_wo', 'norm_ffn_w', 'ffn_w1', 'ffn_w2', 'final_norm_w']
BIG = ('w_in', 'w_branch', 'w_out', 'xa_wq', 'xa_wkv', 'xa_wo', 'ffn_w1', 'ffn_w2')
SMALL_SHARDED = ('cv_dw_w', 'lru_conv_w', 'gate_b')
SMALL = tuple(n for n in WEIGHTS if n not in BIG and n not in SMALL_SHARDED)
PACK_ROWS = 256


def _pack(arrs):
    flat = jnp.concatenate([a.reshape(-1).astype(F32) for a in arrs])
    tile = PACK_ROWS * LANE
    padded = -(-flat.shape[0] // tile) * tile
    return jnp.pad(flat, (0, padded - flat.shape[0])).reshape(-1, LANE)


def _unpack(packed, shapes):
    flat = packed.reshape(-1)
    out, off = [], 0
    for s in shapes:
        n = math.prod(s)
        out.append(flat[off:off + n].reshape(s))
        off += n
    return out


def _gather_last(g, shard_shape):
    nd = len(shard_shape)
    full = jnp.moveaxis(g, 0, nd - 1)
    return full.reshape(*shard_shape[:-1], N_DEV * shard_shape[-1])


def _natural(blocks):
    nb, k, c = blocks.shape
    return jnp.transpose(blocks, (1, 0, 2)).reshape(k, nb * c)


def _block_diag(w):
    w2 = w.reshape(4, 2, 64, 64)
    z = jnp.zeros((4, 64, 64), w.dtype)
    return jnp.concatenate([jnp.concatenate([w2[:, 0], z], axis=2), jnp.concatenate([z, w2[:, 1]], axis=2)], axis=1)


def _block_diag_t(d):
    return jnp.stack([d[:, :64, :64], d[:, 64:, 64:]], axis=1).reshape(8, 64, 64)


def _lower_bounds(raw):
    lb = jnp.cumsum(jax.nn.softmax(raw.astype(F32), axis=0), axis=0)
    return lb - lb[0:1]


def _decay_rates(lam):
    return (LRU_C * jax.nn.softplus(-lam.astype(F32))).reshape(DEPTH, BRANCH_W)


def _relu2(acc):
    r = jnp.maximum(acc, 0.0)
    return acc, r * r


def _relu2_grad(acc, u):
    return (acc * 2.0 * jnp.maximum(u, 0.0),)


def _add(acc, e):
    return (acc + e,)


def _layer_fwd(x0, mem, p, g, rest, after=()):
    h1 = rms_fwd("rms_mix", x0, p['norm_mix_w'])
    proj = mm_nt("mm_in", h1, g['w_in'], tn=2176, after=after)[0]
    bcat, states, o_hg = hgrn_fwd(proj, p['lb'], p['hg_norm_w'])
    zc = cv_fwd(proj, p['cv_w32'], p['cv_dw_b'])
    bcat = ln_silu_fwd(zc, p['cv_ln_w'], p['cv_ln_b'], bcat)
    bcat = pool_fwd(proj, p['pl_w'], p['pl_scale'], bcat)
    bcat, hst = lru_fwd(proj, p['lru_cw8'], p['lru_conv_b'], p['wa_bd'], p['lru_ba'], p['wx_bd'], p['lru_bx'], p['sp8'], bcat)
    more, after = rest(bcat)
    g = {**g, **more}
    ups, merged = merge_fwd(bcat, g['w_branch'], proj, p['gate_b'], after=after)
    x1 = mm_nn("mm_out", merged, g['w_out'], epi=_add, extras=(x0,))[0]
    h2 = rms_fwd("rms_mem", x1, p['norm_mem_w'])
    q = mm_nn("mm_q", h2, g['xa_wq'], out_dtype=BF16)[0]
    memn = rms_fwd("rms_memtok", mem, p['mem_norm_w'])
    kv = mm_nn("mm_kv", memn, g['xa_wkv'], out_dtype=BF16, tn=2048)[0]
    oa = attn_fwd(q, kv)
    x2 = mm_nn("mm_o", oa, g['xa_wo'], epi=_add, extras=(x1,))[0]
    h3 = rms_fwd("rms_ffn", x2, p['norm_ffn_w'])
    u, act = mm_nn("mm_ffn1", h3, g['ffn_w1'], epi=_relu2, out_dtypes=[BF16, BF16])
    x3 = mm_nn("mm_ffn2", act, g['ffn_w2'], epi=_add, extras=(x2,))[0]
    res = dict(x0=x0, h1=h1, proj=proj, states=states, o_hg=o_hg, zc=zc, hst=hst, bcat=bcat, ups=ups, merged=merged,
               x1=x1, h2=h2, q=q, memn=memn, kv=kv, oa=oa, x2=x2, h3=h3, u=u, act=act)
    return x3, res, g


def _layer_bwd(dx3, mem, p, g, r, midway, finish):
    gs, gb = {}, {}
    du = mm_nt("mm_dffn2", dx3, g['ffn_w2'], out_dtype=BF16, epi=_relu2_grad, extras=(r['u'],))[0]
    gb['ffn_w2'] = mm_tn("mm_gw2", r['act'], dx3).reshape(N_DEV, -1, D_MODEL)
    gb['ffn_w1'] = mm_tn_cb("mm_gw1", r['h3'], du, N_DEV)
    dh3 = mm_nt("mm_dffn1", du, g['ffn_w1'], out_dtype=BF16)[0]
    dx2, gs['norm_ffn_w'] = rms_bwd("rmsb_ffn", r['x2'], p['norm_ffn_w'], dh3, dx3)
    doa = mm_nt("mm_do", dx2, g['xa_wo'], out_dtype=BF16)[0]
    gb['xa_wo'] = mm_tn("mm_gwo", r['oa'], dx2).reshape(N_DEV, -1, D_MODEL)
    dq, dkv = attn_bwd(r['q'], r['kv'], doa)
    gb['xa_wq'] = mm_tn("mm_gwq", r['h2'], dq).reshape(N_DEV, -1, D_MODEL)
    dh2 = mm_nt("mm_dq", dq, g['xa_wq'], out_dtype=BF16)[0]
    gb['xa_wkv'] = mm_tn_cb("mm_gwkv", r['memn'], dkv, N_DEV)
    dmemn = mm_nt("mm_dkv", dkv, g['xa_wkv'], out_dtype=BF16)[0]
    _, gs['mem_norm_w'] = rms_bwd("rmsb_memtok", mem, p['mem_norm_w'], dmemn)
    dx1, gs['norm_mem_w'] = rms_bwd("rmsb_mem", r['x1'], p['norm_mem_w'], dh2, dx2)
    after = midway(gb, dx1)
    gb = {}
    dmerged = mm_nt("mm_dout", dx1, g['w_out'], out_dtype=BF16, after=after)[0]
    gb['w_out'] = mm_tn("mm_gwout", r['merged'], dx1).reshape(N_DEV, -1, D_MODEL)
    dups, dproj, gs['gate_b'] = merge_bwd(dmerged, r['ups'], r['proj'], p['gate_b'])
    gwb = mm_branch_tn("mm_gwb", r['bcat'], dups, N_BRANCH)
    gb['w_branch'] = jnp.transpose(gwb.reshape(N_BRANCH, BRANCH_W, N_DEV, -1), (2, 0, 1, 3))
    dbcat = mm_branch_nt("mm_dup", dups, g['w_branch'], tm=2048)
    dproj, gs['lb'], gs['hg_norm_w'] = hgrn_bwd(r['proj'], p['lb'], p['hg_norm_w'], r['states'], r['o_hg'], dbcat, dproj)
    dzc, gs['cv_ln_w'], gs['cv_ln_b'] = ln_silu_bwd(r['zc'], p['cv_ln_w'], p['cv_ln_b'], dbcat)
    dca, dcg, dcw, gs['cv_dw_b'] = cv_bwd(r['proj'], p['cv_w32'], dzc)
    gs['cv_dw_w'] = dcw[:CV_KERNEL]
    dproj, gs['pl_w'], gs['pl_scale'] = pool_bwd(r['proj'], p['pl_w'], p['pl_scale'], dbcat, dproj)
    dlx, dly, dlcw, gs['lru_conv_b'], dwa, gs['lru_ba'], dwx, gs['lru_bx'], gs['sp8'] = lru_bwd(
        r['proj'], p['lru_cw8'], p['lru_conv_b'], p['wa_bd'], p['lru_ba'], p['wx_bd'], p['lru_bx'], p['sp8'], r['hst'], dbcat)
    gs['lru_conv_w'] = dlcw[:LRU_CONV]
    gs['lru_wa'], gs['lru_wx'] = _block_diag_t(dwa), _block_diag_t(dwx)
    gs['lru_ba'], gs['lru_bx'] = gs['lru_ba'].reshape(8, 64), gs['lru_bx'].reshape(8, 64)
    for off, piece in ((OFF_CV, dca), (OFF_CV + BRANCH_W, dcg), (OFF_LX, dlx), (OFF_LY, dly)):
        dproj = lax.dynamic_update_slice(dproj, piece, (0, off))
    gb['w_in'] = mm_tn("mm_gwin", dproj, r['h1'], tm=2176, tk=1024).reshape(N_DEV, -1, D_MODEL)
    dh1 = mm_nn("mm_din", dproj, g['w_in'], out_dtype=BF16, tk=4352, after=finish(gb, dx1))[0]
    dx0, gs['norm_mix_w'] = rms_bwd("rmsb_mix", r['x0'], p['norm_mix_w'], dh1, dx1)
    return dx0, gs


def kernel(x, mem, norm_mix_w, w_in, hg_lb_raw, hg_norm_w, cv_dw_w, cv_dw_b, cv_ln_w, cv_ln_b, pl_w, pl_scale, lru_conv_w, lru_conv_b, lru_wa, lru_ba, lru_wx, lru_bx, lru_lambda, gate_b, w_branch, w_out, norm_mem_w, mem_norm_w, xa_wq, xa_wkv, xa_wo, norm_ffn_w, ffn_w1, ffn_w2, final_norm_w, loss_target, m_norm_mix_w, m_w_in, m_hg_lb_raw, m_hg_norm_w, m_cv_dw_w, m_cv_dw_b, m_cv_ln_w, m_cv_ln_b, m_pl_w, m_pl_scale, m_lru_conv_w, m_lru_conv_b, m_lru_wa, m_lru_ba, m_lru_wx, m_lru_bx, m_lru_lambda, m_gate_b, m_w_branch, m_w_out, m_norm_mem_w, m_mem_norm_w, m_xa_wq, m_xa_wkv, m_xa_wo, m_norm_ffn_w, m_ffn_w1, m_ffn_w2, m_final_norm_w, v_norm_mix_w, v_w_in, v_hg_lb_raw, v_hg_norm_w, v_cv_dw_w, v_cv_dw_b, v_cv_ln_w, v_cv_ln_b, v_pl_w, v_pl_scale, v_lru_conv_w, v_lru_conv_b, v_lru_wa, v_lru_ba, v_lru_wx, v_lru_bx, v_lru_lambda, v_gate_b, v_w_branch, v_w_out, v_norm_mem_w, v_mem_norm_w, v_xa_wq, v_xa_wkv, v_xa_wo, v_norm_ffn_w, v_ffn_w1, v_ffn_w2, v_final_norm_w):
    W = dict(zip(WEIGHTS, (norm_mix_w, w_in, hg_lb_raw, hg_norm_w, cv_dw_w, cv_dw_b, cv_ln_w, cv_ln_b, pl_w, pl_scale, lru_conv_w, lru_conv_b, lru_wa, lru_ba, lru_wx, lru_bx, lru_lambda, gate_b, w_branch, w_out, norm_mem_w, mem_norm_w, xa_wq, xa_wkv, xa_wo, norm_ffn_w, ffn_w1, ffn_w2, final_norm_w)))
    Mo = dict(zip(WEIGHTS, (m_norm_mix_w, m_w_in, m_hg_lb_raw, m_hg_norm_w, m_cv_dw_w, m_cv_dw_b, m_cv_ln_w, m_cv_ln_b, m_pl_w, m_pl_scale, m_lru_conv_w, m_lru_conv_b, m_lru_wa, m_lru_ba, m_lru_wx, m_lru_bx, m_lru_lambda, m_gate_b, m_w_branch, m_w_out, m_norm_mem_w, m_mem_norm_w, m_xa_wq, m_xa_wkv, m_xa_wo, m_norm_ffn_w, m_ffn_w1, m_ffn_w2, m_final_norm_w)))
    Vo = dict(zip(WEIGHTS, (v_norm_mix_w, v_w_in, v_hg_lb_raw, v_hg_norm_w, v_cv_dw_w, v_cv_dw_b, v_cv_ln_w, v_cv_ln_b, v_pl_w, v_pl_scale, v_lru_conv_w, v_lru_conv_b, v_lru_wa, v_lru_ba, v_lru_wx, v_lru_bx, v_lru_lambda, v_gate_b, v_w_branch, v_w_out, v_norm_mem_w, v_mem_norm_w, v_xa_wq, v_xa_wkv, v_xa_wo, v_norm_ffn_w, v_ffn_w1, v_ffn_w2, v_final_norm_w)))
    me = _slot(*_place())
    xs, mems, target = x[0], mem[0], loss_target[0]

    shard_shapes = [W[n].shape for n in SMALL_SHARDED]
    gathered = all_gather("ag_small", [_pack([W[n] for n in SMALL_SHARDED])])[0]
    parts = [jnp.stack(ps) for ps in zip(*[_unpack(gathered[d], shard_shapes) for d in range(N_DEV)])]
    full_small = {n: _gather_last(parts[i], shard_shapes[i]) for i, n in enumerate(SMALL_SHARDED)}
    lb_all, lb_vjp = jax.vjp(_lower_bounds, hg_lb_raw)
    sp8_all, sp8_vjp = jax.vjp(_decay_rates, lru_lambda)

    def layer_params(l):
        p = {n: W[n][l] for n in SMALL if n != 'final_norm_w'}
        p['lb'] = lb_all[l]
        p['sp8'] = sp8_all[l]
        p['cv_w32'] = jnp.pad(full_small['cv_dw_w'][l], ((0, 32 - CV_KERNEL), (0, 0)))
        p['lru_cw8'] = jnp.pad(full_small['lru_conv_w'][l], ((0, 8 - LRU_CONV), (0, 0)))
        p['gate_b'] = full_small['gate_b'][l]
        p['wa_bd'], p['wx_bd'] = _block_diag(lru_wa[l]), _block_diag(lru_wx[l])
        p['lru_ba'], p['lru_bx'] = lru_ba[l].reshape(-1), lru_bx[l].reshape(-1)
        return p

    def shards_of(l):
        first = [jnp.transpose(w_in[l]).astype(BF16)]
        others = [w[l].astype(BF16) for w in (w_branch, w_out, xa_wq, xa_wkv, xa_wo, ffn_w1, ffn_w2)]
        return first, others

    def start_gather(l, before):
        first, others = shards_of(l)
        state_a, tok_a = gather_start(f"ag_start{l}a", first, me, before)
        state_b, tok_b = gather_start(f"ag_start{l}b", others, me, (*before, tok_a))
        return state_a, state_b, (tok_a, tok_b)

    def first_of(o):
        return dict(w_in=o[0].reshape(IN_W, D_MODEL))

    def others_of(o):
        wb = jnp.transpose(o[0], (1, 2, 0, 3)).reshape(N_BRANCH, BRANCH_W, D_MODEL)
        return dict(w_branch=wb, w_out=o[1].reshape(D_MODEL, D_MODEL),
                    xa_wq=o[2].reshape(D_MODEL, D_MODEL), xa_wkv=_natural(o[3]), xa_wo=o[4].reshape(D_MODEL, D_MODEL),
                    ffn_w1=_natural(o[5]), ffn_w2=o[6].reshape(D_FF, D_MODEL))

    params = [layer_params(l) for l in range(DEPTH)]
    mats, residuals = [], []
    xc = xs
    first, others = shards_of(0)
    whole = all_gather("ag_layer0", first)
    state_b, started = gather_start("ag_start0b", others, me, (whole[0],))
    gathers = {}
    for l in range(DEPTH):
        if l == 0:
            g_first = first_of(whole)
        else:
            state_a, state_b, _ = gathers.pop(l)
            g_first = first_of(gather_wait(f"ag_wait{l}a", state_a, (xc,)))

        def rest(mixed, l=l):
            more = others_of(gather_wait(f"ag_wait{l}b", state_b, (mixed,)))
            if l + 1 == DEPTH:
                return more, ()
            gathers[l + 1] = start_gather(l + 1, (more['w_out'],))
            return more, gathers[l + 1][2]

        xc, res, g = _layer_fwd(xc, mems, params[l], g_first, rest, after=(started,) if l == 0 else ())
        mats.append(g)
        residuals.append(res)
    loss_part, dx, g_final = loss_head(xc, final_norm_w, target)
    loss = lax.psum(loss_part, ("x", "y", "c"))

    small_grads = [None] * DEPTH
    big_grads = [{} for _ in range(DEPTH)]
    pending = []

    def send(l, group, blocks, before):
        names = list(blocks)
        state, tok = exchange_start(f"rs_start{l}{group}", [blocks[n] for n in names], (before,))
        pending.append((l, group, names, state))
        return (tok,)

    def land(after):
        l, group, names, state = pending.pop(0)
        sent, landed = exchange_wait(f"rs_wait{l}{group}", state, after)
        for n, s, t in zip(names, sent, landed):
            own = lax.dynamic_index_in_dim(s, me, 0, keepdims=False).reshape(-1, s.shape[-1])
            big_grads[l][n] = sum_parts(t.reshape(N_PEER, -1, t.shape[-1]), own).reshape(t.shape[1:])

    for l in reversed(range(DEPTH)):
        dx, small_grads[l] = _layer_bwd(dx, mems, params[l], mats[l], residuals[l],
                                        lambda blocks, dx1, l=l: send(l, "a", blocks, dx1),
                                        lambda blocks, dx1, l=l: send(l, "b", blocks, dx1))
        while pending[0][0] > l:
            land(dx)

    def stacked(n):
        return jnp.stack([small_grads[l][n] for l in range(DEPTH)])

    part = {n: stacked(n) for n in SMALL if n not in ('final_norm_w', 'hg_lb_raw', 'lru_lambda')}
    part['final_norm_w'] = g_final
    part['hg_lb_raw'] = lb_vjp(stacked('lb'))[0]
    part['lru_lambda'] = sp8_vjp(stacked('sp8'))[0]
    for n in SMALL_SHARDED:
        part[n] = stacked(n)
    names = list(SMALL) + list(SMALL_SHARDED)
    full_shapes = [part[n].shape for n in names]
    packed = _pack([part[n] for n in names])
    while len(pending) > 1:
        land(dx)
    summed = [t for layer in big_grads for t in layer.values()]
    state, _ = exchange_start("rs_small_start", [packed.reshape(N_DEV, -1, LANE)], (packed, *summed))
    sent, landed = exchange_wait("rs_small_wait", state, packed)
    mine = sum_parts(landed[0], lax.dynamic_index_in_dim(sent[0], me, 0, keepdims=False))
    total = all_gather("ag_grads", [mine])[0].reshape(-1, LANE)
    while pending:
        land(total)

    G = {}
    G['w_in'] = jnp.stack([jnp.transpose(big_grads[l]['w_in']) for l in range(DEPTH)])
    for n in ('w_branch', 'w_out', 'xa_wq', 'xa_wkv', 'xa_wo', 'ffn_w1', 'ffn_w2'):
        G[n] = jnp.stack([big_grads[l][n] for l in range(DEPTH)])
    for n, t in zip(names, _unpack(total, full_shapes)):
        if n in SMALL_SHARDED:
            c = t.shape[-1] // N_DEV
            t = lax.dynamic_slice_in_dim(t, me * c, c, axis=t.ndim - 1)
        G[n] = t

    delta, new_m, new_v = {}, {}, {}
    for n in BIG:
        c = W[n].shape[-1]
        d, nm, nv = adamw(W[n].reshape(-1, c), G[n].reshape(-1, c), Mo[n].reshape(-1, c), Vo[n].reshape(-1, c))
        delta[n], new_m[n], new_v[n] = d.reshape(W[n].shape), nm.reshape(W[n].shape), nv.reshape(W[n].shape)
    shapes = [W[n].shape for n in names]
    d, nm, nv = adamw(_pack([W[n] for n in names]), _pack([G[n] for n in names]), _pack([Mo[n] for n in names]), _pack([Vo[n] for n in names]))
    for n, a, b, c in zip(names, _unpack(d, shapes), _unpack(nm, shapes), _unpack(nv, shapes)):
        delta[n], new_m[n], new_v[n] = a, b, c
    return (loss, dx[None], *[G[n] for n in WEIGHTS], *[delta[n] for n in WEIGHTS],
            *[new_m[n] for n in WEIGHTS], *[new_v[n] for n in WEIGHTS])
```

```python
import functools
import math

import jax
import jax.numpy as jnp
from jax import lax
from jax.experimental import pallas as pl
from jax.experimental.pallas import tpu as pltpu

F32 = jnp.float32
BF16 = jnp.bfloat16
I32 = jnp.int32

N_DEV = 8
D_MODEL = 1024
DEPTH = 4
CHUNK = 64
EPS = 1e-6
HG_HEADS = 4
BRANCH_W = 512
CV_KERNEL = 31
POOL_WINDOWS = (2, 4, 8, 16)
LRU_CONV = 4
LRU_C = 8.0
XA_HEADS = 4
XA_HD = D_MODEL // XA_HEADS
D_FF = 4 * D_MODEL
IN_W = 8704
OFF_Q, OFF_F, OFF_V, OFF_G, OFF_CV, OFF_PL, OFF_LX, OFF_LY, OFF_GATE = 0, 512, 1024, 1536, 2048, 3072, 3584, 4096, 4608
LANE = 128
ADAM_LR, ADAM_B1, ADAM_B2, ADAM_EPS, ADAM_WD, ADAM_STEP = 0.001, 0.9, 0.999, 1e-08, 0.01, 10
VMEM_LIMIT = 56 * 1024 * 1024
MESH = pl.DeviceIdType.MESH
NEG = -1e30
ANY_SPACE = pl.BlockSpec(memory_space=pl.ANY)


def _cp(sem, **kw):
    return pltpu.CompilerParams(dimension_semantics=sem, vmem_limit_bytes=VMEM_LIMIT, **kw)


def _sigmoid(x):
    return 1.0 / (1.0 + jnp.exp(-x))


def _dsilu(x, s):
    return s * (1.0 + x * (1.0 - s))


def _dot(a, b, cdims, precision=None):
    return lax.dot_general(a, b, (cdims, ((), ())), preferred_element_type=F32, precision=precision)


NN = ((1,), (0,))
NT = ((1,), (1,))
TN = ((0,), (0,))


def _mm(name, a, b, *, grid, a_spec, b_spec, o_specs, out_shapes, acc_shape, cdims, epi=None, extras=(), extra_specs=(), after=()):
    nk = grid[2]
    n_e, n_o = len(extras), len(out_shapes)
    extras = (*extras, *after)
    extra_specs = (*extra_specs, *[ANY_SPACE] * len(after))

    def body(*refs):
        a_ref, b_ref = refs[0], refs[1]
        e_refs = refs[2:2 + n_e]
        o_refs = refs[2 + len(extras):2 + len(extras) + n_o]

        def finish(acc):
            vals = epi(acc, *[r[...] for r in e_refs]) if epi is not None else (acc,)
            for r, v in zip(o_refs, vals):
                r[...] = v.astype(r.dtype)

        part = _dot(a_ref[...].astype(BF16), b_ref[...].astype(BF16), cdims)
        if nk == 1:
            finish(part)
        else:
            acc_ref = refs[-1]
            k = pl.program_id(2)

            @pl.when(k == 0)
            def _():
                acc_ref[...] = part

            @pl.when(k > 0)
            def _():
                acc_ref[...] += part

            @pl.when(k == nk - 1)
            def _():
                finish(acc_ref[...])

    return pl.pallas_call(
        body, name=name, grid=grid,
        in_specs=[a_spec, b_spec, *extra_specs], out_specs=list(o_specs), out_shape=list(out_shapes),
        scratch_shapes=[] if nk == 1 else [pltpu.VMEM(acc_shape, F32)],
        compiler_params=_cp(("parallel", "parallel", "arbitrary")),
    )(a, b, *extras)


def _tile(n, pref):
    t = min(n, pref)
    while n % t:
        t //= 2
    return t


def mm_nt(name, a, b, out_dtype=F32, epi=None, extras=(), n_out=1, out_dtypes=None, tm=1024, tn=1024, tk=2048, after=()):
    M, K = a.shape
    N = b.shape[0]
    tm, tn, tk = _tile(M, tm), _tile(N, tn), _tile(K, tk)
    odt = out_dtypes or [out_dtype] * n_out
    o_spec = pl.BlockSpec((tm, tn), lambda i, j, k: (i, j))
    return _mm(name, a, b, grid=(M // tm, N // tn, K // tk),
               a_spec=pl.BlockSpec((tm, tk), lambda i, j, k: (i, k)),
               b_spec=pl.BlockSpec((tn, tk), lambda i, j, k: (j, k)),
               o_specs=[o_spec] * len(odt), out_shapes=[jax.ShapeDtypeStruct((M, N), d) for d in odt],
               acc_shape=(tm, tn), cdims=NT, epi=epi, extras=extras, extra_specs=[o_spec] * len(extras), after=after)


def mm_nn(name, a, b, out_dtype=F32, epi=None, extras=(), n_out=1, out_dtypes=None, tm=1024, tn=1024, tk=2048, after=()):
    M, K = a.shape
    N = b.shape[1]
    tm, tn, tk = _tile(M, tm), _tile(N, tn), _tile(K, tk)
    odt = out_dtypes or [out_dtype] * n_out
    o_spec = pl.BlockSpec((tm, tn), lambda i, j, k: (i, j))
    return _mm(name, a, b, grid=(M // tm, N // tn, K // tk),
               a_spec=pl.BlockSpec((tm, tk), lambda i, j, k: (i, k)),
               b_spec=pl.BlockSpec((tk, tn), lambda i, j, k: (k, j)),
               o_specs=[o_spec] * len(odt), out_shapes=[jax.ShapeDtypeStruct((M, N), d) for d in odt],
               acc_shape=(tm, tn), cdims=NN, epi=epi, extras=extras, extra_specs=[o_spec] * len(extras), after=after)


def mm_tn(name, a, b, out_dtype=BF16, tm=1024, tn=1024, tk=2048):
    K, M = a.shape
    N = b.shape[1]
    tm, tn, tk = _tile(M, tm), _tile(N, tn), _tile(K, tk)
    return _mm(name, a, b, grid=(M // tm, N // tn, K // tk),
               a_spec=pl.BlockSpec((tk, tm), lambda i, j, k: (k, i)),
               b_spec=pl.BlockSpec((tk, tn), lambda i, j, k: (k, j)),
               o_specs=[pl.BlockSpec((tm, tn), lambda i, j, k: (i, j))],
               out_shapes=[jax.ShapeDtypeStruct((M, N), out_dtype)], acc_shape=(tm, tn), cdims=TN)[0]


def mm_branch_nt(name, a, b, out_dtype=BF16, tm=1024):
    M = a.shape[0]
    G, K, N = b.shape
    tm = _tile(M, tm)
    return _mm(name, a, b, grid=(M // tm, G, 1),
               a_spec=pl.BlockSpec((tm, N), lambda i, g, k: (i, g)),
               b_spec=pl.BlockSpec((None, K, N), lambda i, g, k: (g, 0, 0)),
               o_specs=[pl.BlockSpec((tm, K), lambda i, g, k: (i, g))],
               out_shapes=[jax.ShapeDtypeStruct((M, G * K), out_dtype)], acc_shape=(tm, K), cdims=NT)[0]


def mm_branch_tn(name, a, b, groups, out_dtype=BF16, tk=2048):
    T = a.shape[0]
    K, N = a.shape[1] // groups, b.shape[1] // groups
    tk = _tile(T, tk)
    return _mm(name, a, b, grid=(groups, 1, T // tk),
               a_spec=pl.BlockSpec((tk, K), lambda g, j, k: (k, g)),
               b_spec=pl.BlockSpec((tk, N), lambda g, j, k: (k, g)),
               o_specs=[pl.BlockSpec((None, K, N), lambda g, j, k: (g, 0, 0))],
               out_shapes=[jax.ShapeDtypeStruct((groups, K, N), out_dtype)], acc_shape=(K, N), cdims=TN)[0]


def mm_tn_cb(name, a, b, nb, out_dtype=BF16, tm=1024, tk=2048):
    K, M = a.shape
    N = b.shape[1]
    c = N // nb
    tm, tk = _tile(M, tm), _tile(K, tk)
    return _mm(name, a, b, grid=(M // tm, nb, K // tk),
               a_spec=pl.BlockSpec((tk, tm), lambda i, j, k: (k, i)),
               b_spec=pl.BlockSpec((tk, c), lambda i, j, k: (k, j)),
               o_specs=[pl.BlockSpec((None, tm, c), lambda i, j, k: (j, i, 0))],
               out_shapes=[jax.ShapeDtypeStruct((nb, M, c), out_dtype)], acc_shape=(tm, c), cdims=TN)[0]


def rms_fwd(name, x, w, out_dtype=BF16, tm=512):
    S, D = x.shape
    tm = _tile(S, tm)

    def body(x_ref, w_ref, o_ref):
        xv = x_ref[...]
        r = lax.rsqrt(jnp.mean(xv * xv, axis=-1, keepdims=True) + EPS)
        o_ref[...] = (xv * r * w_ref[...]).astype(o_ref.dtype)

    return pl.pallas_call(
        body, name=name, grid=(S // tm,),
        in_specs=[pl.BlockSpec((tm, D), lambda i: (i, 0)), pl.BlockSpec((1, D), lambda i: (0, 0))],
        out_specs=pl.BlockSpec((tm, D), lambda i: (i, 0)), out_shape=jax.ShapeDtypeStruct((S, D), out_dtype),
        compiler_params=_cp(("parallel",)),
    )(x, w.reshape(1, D))


def rms_bwd(name, x, w, dh, dres=None, tm=512):
    S, D = x.shape
    tm = _tile(S, tm)
    has_res = dres is not None

    def body(*refs):
        if has_res:
            x_ref, w_ref, dh_ref, dres_ref, dx_ref, dw_ref = refs
        else:
            x_ref, w_ref, dh_ref, dx_ref, dw_ref = refs
        xv = x_ref[...]
        dhv = dh_ref[...].astype(F32)
        r = lax.rsqrt(jnp.mean(xv * xv, axis=-1, keepdims=True) + EPS)
        g = dhv * w_ref[...]
        dx = r * g - xv * (r * r * r) * jnp.mean(xv * g, axis=-1, keepdims=True)
        if has_res:
            dx = dx + dres_ref[...]
        dx_ref[...] = dx

        @pl.when(pl.program_id(0) == 0)
        def _():
            dw_ref[...] = jnp.zeros_like(dw_ref)

        dw_ref[...] += jnp.sum(dhv * xv * r, axis=0, keepdims=True)

    row = pl.BlockSpec((tm, D), lambda i: (i, 0))
    vec = pl.BlockSpec((1, D), lambda i: (0, 0))
    args = [x, w.reshape(1, D), dh] + ([dres] if has_res else [])
    dx, dw = pl.pallas_call(
        body, name=name, grid=(S // tm,),
        in_specs=[row, vec, row] + ([row] if has_res else []),
        out_specs=[row, vec], out_shape=[jax.ShapeDtypeStruct((S, D), F32), jax.ShapeDtypeStruct((1, D), F32)],
        compiler_params=_cp(("arbitrary",)),
    )(*args)
    return dx, dw.reshape(D)


def loss_head(x, w, target, tm=512):
    S, D = x.shape
    tm = _tile(S, tm)

    def body(x_ref, w_ref, t_ref, loss_ref, dx_ref, dw_ref):
        xv = x_ref[...]
        wv = w_ref[...]
        r = lax.rsqrt(jnp.mean(xv * xv, axis=-1, keepdims=True) + EPS)
        y = xv * r * wv
        err = y - t_ref[...]
        dy = err * (1.0 / D)
        g = dy * wv
        dx_ref[...] = r * g - xv * (r * r * r) * jnp.mean(xv * g, axis=-1, keepdims=True)

        @pl.when(pl.program_id(0) == 0)
        def _():
            dw_ref[...] = jnp.zeros_like(dw_ref)
            loss_ref[...] = jnp.zeros_like(loss_ref)

        dw_ref[...] += jnp.sum(dy * xv * r, axis=0, keepdims=True)
        part = 0.5 * jnp.sum(jnp.mean(err * err, axis=-1, keepdims=True), axis=0, keepdims=True)
        loss_ref[...] += jnp.broadcast_to(part, loss_ref.shape)

    row = pl.BlockSpec((tm, D), lambda i: (i, 0))
    vec = pl.BlockSpec((1, D), lambda i: (0, 0))
    loss, dx, dw = pl.pallas_call(
        body, name="loss_head", grid=(S // tm,),
        in_specs=[row, vec, row],
        out_specs=[pl.BlockSpec((1, LANE), lambda i: (0, 0)), row, vec],
        out_shape=[jax.ShapeDtypeStruct((1, LANE), F32), jax.ShapeDtypeStruct((S, D), F32), jax.ShapeDtypeStruct((1, D), F32)],
        compiler_params=_cp(("arbitrary",)),
    )(x, w.reshape(1, D), target)
    return loss[0, 0], dx, dw.reshape(D)


SUB = 16
HG_W = HG_HEADS * LANE


def _hg_gates(q, f, lbv):
    sig = _sigmoid(f)
    fg = lbv + (1.0 - lbv) * sig
    sq = _sigmoid(q)
    return sig, fg, 1.0 - fg, sq, q * sq


def _hg_cumsum(logf):
    ri = lax.broadcasted_iota(I32, (CHUNK, CHUNK), 0)
    ci = lax.broadcasted_iota(I32, (CHUNK, CHUNK), 1)
    return _dot((ci <= ri).astype(F32), logf, NN, precision=lax.Precision.HIGHEST)


def _hg_rows():
    return lax.broadcasted_iota(I32, (CHUNK, LANE), 0)


def _hg_below(qf, kk, b, rows):
    blocks, parts = [jnp.zeros((SUB, CHUNK), F32)], []
    for i in range(1, CHUNK // SUB):
        bref = b[SUB * i - 1:SUB * i, :]
        rs = slice(SUB * i, SUB * (i + 1))
        eq = jnp.exp(b[rs] - bref)
        below = rows < SUB * i
        ek = jnp.exp(jnp.where(below, bref - b, NEG))
        qi = (qf[rs] * eq).astype(BF16)
        ki = (kk * ek).astype(BF16)
        blocks.append(_dot(qi, ki, NT))
        parts.append((qi, ki, eq, ek))
    return jnp.concatenate(blocks, axis=0), parts


def hgrn_fwd(proj, lb, nw):
    S = proj.shape[0]
    NC = S // CHUNK
    H = HG_HEADS

    def body(q_ref, f_ref, v_ref, g_ref, lb_ref, nw_ref, out_ref, st_out_ref, o_ref, st, kk_s, b_s):
        c = pl.program_id(0)

        @pl.when(c == 0)
        def _():
            st[...] = jnp.zeros_like(st)

        st_out_ref[...] = st[...]
        sig, fg, kk_all, sq, qf_all = _hg_gates(q_ref[...], f_ref[...], lb_ref[...])
        b_all = _hg_cumsum(jnp.log(fg))
        kk_s[...] = kk_all
        b_s[...] = b_all
        rows = _hg_rows()
        r16 = lax.broadcasted_iota(I32, (SUB, LANE), 0)
        for h in range(H):
            cs = slice(h * LANE, (h + 1) * LANE)
            qf, kk, b, v, g = qf_all[:, cs], kk_all[:, cs], b_all[:, cs], v_ref[:, cs], g_ref[:, cs]
            st_in = st[h]
            diag = []
            for i in range(CHUNK // SUB):
                rs = slice(SUB * i, SUB * (i + 1))
                acc = jnp.zeros((SUB, LANE), F32)
                for j in range(SUB):
                    row = pl.ds(SUB * i + j, 1)
                    e = jnp.exp(jnp.where(r16 >= j, b[rs] - b_s[row, cs], NEG))
                    col = jnp.sum(qf[rs] * (kk_s[row, cs] * e), axis=1, keepdims=True)
                    acc = acc + col * v_ref[row, cs]
                diag.append(acc)
            poff, _ = _hg_below(qf, kk, b, rows)
            vb = v.astype(BF16)
            bl = b[CHUNK - 1:CHUNK, :]
            o = (jnp.concatenate(diag, axis=0) + _dot(poff.astype(BF16), vb, NN)
                 + _dot((qf * jnp.exp(b)).astype(BF16), st_in.astype(BF16), NT))
            st[h] = st_in * jnp.exp(bl) + _dot(vb, (kk * jnp.exp(bl - b)).astype(BF16), TN)
            o_ref[:, cs] = o
            r = lax.rsqrt(jnp.mean(o * o, axis=-1, keepdims=True) + EPS)
            out_ref[:, cs] = (o * r * nw_ref[...] * (g * _sigmoid(g))).astype(out_ref.dtype)

    def seg(off):
        return pl.BlockSpec((CHUNK, HG_W), lambda c: (c, off // HG_W))

    blk = pl.BlockSpec((CHUNK, HG_W), lambda c: (c, 0))
    full = pltpu.VMEM((CHUNK, HG_W), F32)
    return pl.pallas_call(
        body, name="hgrn_fwd", grid=(NC,),
        in_specs=[seg(OFF_Q), seg(OFF_F), seg(OFF_V), seg(OFF_G),
                  pl.BlockSpec((1, HG_W), lambda c: (0, 0)), pl.BlockSpec((1, LANE), lambda c: (0, 0))],
        out_specs=[blk, pl.BlockSpec((None, H, LANE, LANE), lambda c: (c, 0, 0, 0)), blk],
        out_shape=[jax.ShapeDtypeStruct((S, 4 * HG_W), BF16), jax.ShapeDtypeStruct((NC, H, LANE, LANE), F32),
                   jax.ShapeDtypeStruct((S, HG_W), F32)],
        scratch_shapes=[pltpu.VMEM((H, LANE, LANE), F32), full, full],
        compiler_params=_cp(("arbitrary",)),
    )(proj, proj, proj, proj, lb.reshape(1, HG_W), nw.reshape(1, LANE))


def hgrn_bwd(proj, lb, nw, states, o_pre, dbcat, dproj):
    S = proj.shape[0]
    NC = S // CHUNK
    H = HG_HEADS

    def body(q_ref, f_ref, v_ref, g_ref, lb_ref, nw_ref, st_ref, o_ref, do_ref, _,
             dp_ref, dlb_ref, dnw_ref, dst, kk_s, b_s, do_s, db_s, dkk_s, dkk_d, dv_d):
        c = pl.program_id(0)
        dq_ref, df_ref, dv_ref, dg_ref = (dp_ref.at[:, pl.ds(off, HG_W)] for off in (OFF_Q, OFF_F, OFF_V, OFF_G))

        @pl.when(c == 0)
        def _():
            dst[...] = jnp.zeros_like(dst)
            dlb_ref[...] = jnp.zeros_like(dlb_ref)
            dnw_ref[...] = jnp.zeros_like(dnw_ref)

        q_all, g_all = q_ref[...], g_ref[...]
        lbv, nwv = lb_ref[...], nw_ref[...]
        sig, fg, kk_all, sq, qf_all = _hg_gates(q_all, f_ref[...], lbv)
        b_all = _hg_cumsum(jnp.log(fg))
        o_all = o_ref[...]
        dov = do_ref[...].astype(F32)
        sg = _sigmoid(g_all)
        gsg = g_all * sg
        dnw_acc = jnp.zeros((1, LANE), F32)
        for h in range(H):
            cs = slice(h * LANE, (h + 1) * LANE)
            o = o_all[:, cs]
            r = lax.rsqrt(jnp.mean(o * o, axis=-1, keepdims=True) + EPS)
            don = dov[:, cs] * gsg[:, cs]
            dnw_acc = dnw_acc + jnp.sum(don * o * r, axis=0, keepdims=True)
            gno = don * nwv
            do_s[:, cs] = r * gno - o * (r * r * r) * jnp.mean(o * gno, axis=-1, keepdims=True)
            dg_ref[:, cs] = (dov[:, cs] * (o * r * nwv) * _dsilu(g_all[:, cs], sg[:, cs])).astype(dg_ref.dtype)
        dnw_ref[...] += jnp.broadcast_to(dnw_acc, dnw_ref.shape)
        kk_s[...] = kk_all
        b_s[...] = b_all
        rows = _hg_rows()
        r16 = lax.broadcasted_iota(I32, (SUB, LANE), 0)
        for h in range(H):
            cs = slice(h * LANE, (h + 1) * LANE)
            qf, kk, b, v = qf_all[:, cs], kk_all[:, cs], b_all[:, cs], v_ref[:, cs]
            do = do_s[:, cs]
            st_in, dstv = st_ref[h], dst[h]
            bl = b[CHUNK - 1:CHUNK, :]
            eb, ebl, el = jnp.exp(b), jnp.exp(bl - b), jnp.exp(bl)
            qe, ke = qf * eb, kk * ebl
            vb, dob, stb, dstb = v.astype(BF16), do.astype(BF16), st_in.astype(BF16), dstv.astype(BF16)
            w_ = _dot(vb, dstb, NN)
            dqf = eb * _dot(dob, stb, NN)
            dkk = ebl * w_
            dv = _dot(ke.astype(BF16), dstb, NT)
            dbl = el * jnp.sum(st_in * dstv, axis=0, keepdims=True) + jnp.sum(ke * w_, axis=0, keepdims=True)
            dst[h] = dstv * el + _dot(dob, qe.astype(BF16), TN)
            poff, parts = _hg_below(qf, kk, b, rows)
            dpoff = _dot(dob, vb, NT).astype(BF16)
            dv = dv + _dot(poff.astype(BF16), dob, TN)
            dq_blocks = [jnp.zeros((SUB, LANE), F32)]
            for i, (qi, ki, eq, ek) in enumerate(parts, start=1):
                dpi = dpoff[SUB * i:SUB * (i + 1), :]
                dq_blocks.append(_dot(dpi, ki, NN) * eq)
                dkk = dkk + _dot(dpi, qi, TN) * ek
            dqf = dqf + jnp.concatenate(dq_blocks, axis=0)
            dq_diag = []
            for i in range(CHUNK // SUB):
                rs = slice(SUB * i, SUB * (i + 1))
                acc = jnp.zeros((SUB, LANE), F32)
                for j in range(SUB):
                    row = pl.ds(SUB * i + j, 1)
                    ks = kk_s[row, cs]
                    e = jnp.exp(jnp.where(r16 >= j, b[rs] - b_s[row, cs], NEG))
                    x = jnp.sum(do[rs] * v_ref[row, cs], axis=1, keepdims=True) * e
                    acc = acc + x * ks
                    dkk_d[row, cs] = jnp.sum(x * qf[rs], axis=0, keepdims=True)
                    col = jnp.sum(qf[rs] * (ks * e), axis=1, keepdims=True)
                    dv_d[row, cs] = jnp.sum(col * do[rs], axis=0, keepdims=True)
                dq_diag.append(acc)
            dqf = dqf + jnp.concatenate(dq_diag, axis=0)
            dkk = dkk + dkk_d[:, cs]
            dv = dv + dv_d[:, cs]
            dv_ref[:, cs] = dv.astype(dv_ref.dtype)
            db = qf * dqf - kk * dkk
            db_s[:, cs] = db + jnp.where(rows == CHUNK - 1, dbl, 0.0)
            dkk_s[:, cs] = dkk
            dq_ref[:, cs] = (dqf * _dsilu(q_all[:, cs], sq[:, cs])).astype(dq_ref.dtype)
        ri = lax.broadcasted_iota(I32, (CHUNK, CHUNK), 0)
        ci = lax.broadcasted_iota(I32, (CHUNK, CHUNK), 1)
        dlogf = _dot((ci >= ri).astype(F32), db_s[...], NN, precision=lax.Precision.HIGHEST)
        dfg = dlogf / fg - dkk_s[...]
        df_ref[...] = (dfg * (1.0 - lbv) * sig * (1.0 - sig)).astype(df_ref.dtype)
        dlb_ref[...] += jnp.broadcast_to(jnp.sum(dfg * (1.0 - sig), axis=0, keepdims=True), dlb_ref.shape)

    def seg(off):
        return pl.BlockSpec((CHUNK, HG_W), lambda c: (NC - 1 - c, off // HG_W))

    blk = pl.BlockSpec((CHUNK, HG_W), lambda c: (NC - 1 - c, 0))
    full = pltpu.VMEM((CHUNK, HG_W), F32)
    dproj, dlb, dnw = pl.pallas_call(
        body, name="hgrn_bwd", grid=(NC,),
        in_specs=[seg(OFF_Q), seg(OFF_F), seg(OFF_V), seg(OFF_G),
                  pl.BlockSpec((1, HG_W), lambda c: (0, 0)), pl.BlockSpec((1, LANE), lambda c: (0, 0)),
                  pl.BlockSpec((None, H, LANE, LANE), lambda c: (NC - 1 - c, 0, 0, 0)), blk, blk, ANY_SPACE],
        out_specs=[pl.BlockSpec((CHUNK, 4 * HG_W), lambda c: (NC - 1 - c, 0)),
                   pl.BlockSpec((8, HG_W), lambda c: (0, 0)), pl.BlockSpec((8, LANE), lambda c: (0, 0))],
        out_shape=[jax.ShapeDtypeStruct(dproj.shape, dproj.dtype), jax.ShapeDtypeStruct((8, HG_W), F32),
                   jax.ShapeDtypeStruct((8, LANE), F32)],
        input_output_aliases={9: 0},
        scratch_shapes=[pltpu.VMEM((H, LANE, LANE), F32)] + [full] * 7,
        compiler_params=_cp(("arbitrary",)),
    )(proj, proj, proj, proj, lb.reshape(1, HG_W), nw.reshape(1, LANE), states, o_pre, dbcat, dproj)
    return dproj, dlb[0], dnw[0]


CV_PAD = 32
ROWS = 256


def _colblk(S, off):
    return pl.BlockSpec((S, LANE), lambda j: (0, off // LANE + j))


def cv_fwd(proj, w32, bias):
    S = proj.shape[0]
    nchunk = S // ROWS

    def body(a_ref, g_ref, w_ref, b_ref, o_ref, zpad):
        zpad[pl.ds(0, CV_PAD), :] = jnp.zeros((CV_PAD, LANE), F32)

        def glu(c, _):
            r0 = pl.multiple_of(c * ROWS, ROWS)
            zpad[pl.ds(CV_PAD + r0, ROWS), :] = a_ref[pl.ds(r0, ROWS), :] * _sigmoid(g_ref[pl.ds(r0, ROWS), :])
            return 0

        lax.fori_loop(0, nchunk, glu, 0)

        def conv(c, _):
            r0 = pl.multiple_of(c * ROWS, ROWS)
            acc = jnp.broadcast_to(b_ref[...], (ROWS, LANE))
            for j in range(CV_KERNEL):
                acc = acc + w_ref[pl.ds(j, 1), :] * zpad[pl.ds(r0 + (CV_PAD - CV_KERNEL + 1) + j, ROWS), :]
            o_ref[pl.ds(r0, ROWS), :] = acc
            return 0

        lax.fori_loop(0, nchunk, conv, 0)

    return pl.pallas_call(
        body, name="cv_fwd", grid=(BRANCH_W // LANE,),
        in_specs=[_colblk(S, OFF_CV), _colblk(S, OFF_CV + BRANCH_W),
                  pl.BlockSpec((32, LANE), lambda j: (0, j)), pl.BlockSpec((1, LANE), lambda j: (0, j))],
        out_specs=pl.BlockSpec((S, LANE), lambda j: (0, j)), out_shape=jax.ShapeDtypeStruct((S, BRANCH_W), F32),
        scratch_shapes=[pltpu.VMEM((CV_PAD + S, LANE), F32)],
        compiler_params=_cp(("parallel",)),
    )(proj, proj, w32, bias.reshape(1, BRANCH_W))


def cv_bwd(proj, w32, dzc):
    S = proj.shape[0]
    nchunk = S // ROWS

    def body(a_ref, g_ref, w_ref, dz_ref, da_ref, dg_ref, dw_ref, db_ref, zpad, dpad):
        zpad[pl.ds(0, CV_PAD), :] = jnp.zeros((CV_PAD, LANE), F32)
        dpad[pl.ds(S, CV_PAD), :] = jnp.zeros((CV_PAD, LANE), F32)
        dw_ref[...] = jnp.zeros_like(dw_ref)

        def glu(c, dsum):
            r0 = pl.multiple_of(c * ROWS, ROWS)
            zpad[pl.ds(CV_PAD + r0, ROWS), :] = a_ref[pl.ds(r0, ROWS), :] * _sigmoid(g_ref[pl.ds(r0, ROWS), :])
            d = dz_ref[pl.ds(r0, ROWS), :]
            dpad[pl.ds(r0, ROWS), :] = d
            return dsum + jnp.sum(d, axis=0, keepdims=True)

        dsum = lax.fori_loop(0, nchunk, glu, jnp.zeros((1, LANE), F32))
        db_ref[...] = jnp.broadcast_to(dsum, db_ref.shape)

        def conv(c, _):
            r0 = pl.multiple_of(c * ROWS, ROWS)
            d = dpad[pl.ds(r0, ROWS), :]
            acc = jnp.zeros((ROWS, LANE), F32)
            for j in range(CV_KERNEL):
                acc = acc + w_ref[pl.ds(j, 1), :] * dpad[pl.ds(r0 + (CV_KERNEL - 1) - j, ROWS), :]
                zs = zpad[pl.ds(r0 + (CV_PAD - CV_KERNEL + 1) + j, ROWS), :]
                dw_ref[pl.ds(j, 1), :] += jnp.sum(d * zs, axis=0, keepdims=True)
            a = a_ref[pl.ds(r0, ROWS), :]
            sg = _sigmoid(g_ref[pl.ds(r0, ROWS), :])
            da_ref[pl.ds(r0, ROWS), :] = (acc * sg).astype(da_ref.dtype)
            dg_ref[pl.ds(r0, ROWS), :] = (acc * a * sg * (1.0 - sg)).astype(dg_ref.dtype)
            return 0

        lax.fori_loop(0, nchunk, conv, 0)

    blk = pl.BlockSpec((S, LANE), lambda j: (0, j))
    da, dg, dw, db = pl.pallas_call(
        body, name="cv_bwd", grid=(BRANCH_W // LANE,),
        in_specs=[_colblk(S, OFF_CV), _colblk(S, OFF_CV + BRANCH_W), pl.BlockSpec((32, LANE), lambda j: (0, j)), blk],
        out_specs=[blk, blk, pl.BlockSpec((32, LANE), lambda j: (0, j)), pl.BlockSpec((8, LANE), lambda j: (0, j))],
        out_shape=[jax.ShapeDtypeStruct((S, BRANCH_W), BF16), jax.ShapeDtypeStruct((S, BRANCH_W), BF16),
                   jax.ShapeDtypeStruct((32, BRANCH_W), F32), jax.ShapeDtypeStruct((8, BRANCH_W), F32)],
        scratch_shapes=[pltpu.VMEM((CV_PAD + S, LANE), F32), pltpu.VMEM((S + CV_PAD, LANE), F32)],
        compiler_params=_cp(("parallel",)),
    )(proj, proj, w32, dzc)
    return da, dg, dw, db[0]


def ln_silu_fwd(z, w, b, bcat, tm=512):
    S, C = z.shape
    tm = _tile(S, tm)

    def body(z_ref, w_ref, b_ref, _, o_ref):
        zv = z_ref[...]
        mu = jnp.mean(zv, axis=-1, keepdims=True)
        zc = zv - mu
        rstd = lax.rsqrt(jnp.mean(zc * zc, axis=-1, keepdims=True) + EPS)
        y = zc * rstd * w_ref[...] + b_ref[...]
        o_ref[...] = (y * _sigmoid(y)).astype(o_ref.dtype)

    row = pl.BlockSpec((tm, C), lambda i: (i, 0))
    vec = pl.BlockSpec((1, C), lambda i: (0, 0))
    return pl.pallas_call(
        body, name="ln_silu_fwd", grid=(S // tm,), in_specs=[row, vec, vec, ANY_SPACE],
        out_specs=pl.BlockSpec((tm, C), lambda i: (i, 1)), out_shape=jax.ShapeDtypeStruct(bcat.shape, bcat.dtype),
        input_output_aliases={3: 0}, compiler_params=_cp(("parallel",)),
    )(z, w.reshape(1, C), b.reshape(1, C), bcat)


def ln_silu_bwd(z, w, b, dbcat, tm=512):
    S, C = z.shape
    tm = _tile(S, tm)

    def body(z_ref, w_ref, b_ref, do_ref, dz_ref, dw_ref, db_ref):
        zv = z_ref[...]
        wv = w_ref[...]
        mu = jnp.mean(zv, axis=-1, keepdims=True)
        zc = zv - mu
        rstd = lax.rsqrt(jnp.mean(zc * zc, axis=-1, keepdims=True) + EPS)
        xh = zc * rstd
        y = xh * wv + b_ref[...]
        dy = do_ref[...].astype(F32) * _dsilu(y, _sigmoid(y))

        @pl.when(pl.program_id(0) == 0)
        def _():
            dw_ref[...] = jnp.zeros_like(dw_ref)
            db_ref[...] = jnp.zeros_like(db_ref)

        dw_ref[...] += jnp.sum(dy * xh, axis=0, keepdims=True)
        db_ref[...] += jnp.sum(dy, axis=0, keepdims=True)
        dxh = dy * wv
        dz_ref[...] = rstd * (dxh - jnp.mean(dxh, axis=-1, keepdims=True) - xh * jnp.mean(dxh * xh, axis=-1, keepdims=True))

    row = pl.BlockSpec((tm, C), lambda i: (i, 0))
    vec = pl.BlockSpec((1, C), lambda i: (0, 0))
    dz, dw, db = pl.pallas_call(
        body, name="ln_silu_bwd", grid=(S // tm,), in_specs=[row, vec, vec, pl.BlockSpec((tm, C), lambda i: (i, 1))],
        out_specs=[row, vec, vec],
        out_shape=[jax.ShapeDtypeStruct((S, C), F32), jax.ShapeDtypeStruct((1, C), F32), jax.ShapeDtypeStruct((1, C), F32)],
        compiler_params=_cp(("arbitrary",)),
    )(z, w.reshape(1, C), b.reshape(1, C), dbcat)
    return dz, dw.reshape(C), db.reshape(C)


PL_PAD = 16


def _pool_counts(r0, win):
    t = r0 + lax.broadcasted_iota(I32, (ROWS, LANE), 0)
    return jnp.minimum(t + 1, win).astype(F32)


def pool_fwd(proj, wg, scale, bcat):
    S = proj.shape[0]
    nchunk = S // ROWS

    def body(u_ref, w_ref, s_ref, _, o_ref, upad):
        g = pl.program_id(0)
        upad[pl.ds(0, PL_PAD), :] = jnp.zeros((PL_PAD, LANE), F32)

        def fill(c, _):
            r0 = pl.multiple_of(c * ROWS, ROWS)
            upad[pl.ds(PL_PAD + r0, ROWS), :] = u_ref[pl.ds(r0, ROWS), :]
            return 0

        lax.fori_loop(0, nchunk, fill, 0)
        wb = w_ref[...].astype(BF16)
        for gi, win in enumerate(POOL_WINDOWS):
            @pl.when(g == gi)
            def _(win=win):
                def chunk(c, _):
                    r0 = pl.multiple_of(c * ROWS, ROWS)
                    u = upad[pl.ds(PL_PAD + r0, ROWS), :]
                    ws = u
                    for j in range(1, win):
                        ws = ws + upad[pl.ds(PL_PAD + r0 - j, ROWS), :]
                    pooled = ws / _pool_counts(r0, win) - u
                    o_ref[pl.ds(r0, ROWS), :] = (_dot(pooled.astype(BF16), wb, NN) * s_ref[...]).astype(o_ref.dtype)
                    return 0

                lax.fori_loop(0, nchunk, chunk, 0)

    return pl.pallas_call(
        body, name="pool_fwd", grid=(len(POOL_WINDOWS),),
        in_specs=[_colblk(S, OFF_PL), pl.BlockSpec((None, LANE, LANE), lambda j: (j, 0, 0)), pl.BlockSpec((1, LANE), lambda j: (0, j)),
                  ANY_SPACE],
        out_specs=pl.BlockSpec((S, LANE), lambda j: (0, 2 * BRANCH_W // LANE + j)),
        out_shape=jax.ShapeDtypeStruct(bcat.shape, bcat.dtype), input_output_aliases={3: 0},
        scratch_shapes=[pltpu.VMEM((PL_PAD + S, LANE), F32)],
        compiler_params=_cp(("parallel",)),
    )(proj, wg, scale.reshape(1, BRANCH_W), bcat)


def pool_bwd(proj, wg, scale, dbcat, dproj):
    S = proj.shape[0]
    nchunk = S // ROWS

    def body(u_ref, w_ref, s_ref, dy_ref, _, du_ref, dw_ref, ds_ref, upad, dpn, nd):
        g = pl.program_id(0)
        upad[pl.ds(0, PL_PAD), :] = jnp.zeros((PL_PAD, LANE), F32)
        dpn[pl.ds(S, PL_PAD), :] = jnp.zeros((PL_PAD, LANE), F32)

        def fill(c, _):
            r0 = pl.multiple_of(c * ROWS, ROWS)
            upad[pl.ds(PL_PAD + r0, ROWS), :] = u_ref[pl.ds(r0, ROWS), :]
            return 0

        lax.fori_loop(0, nchunk, fill, 0)
        wb = w_ref[...].astype(BF16)
        sv = s_ref[...]
        for gi, win in enumerate(POOL_WINDOWS):
            @pl.when(g == gi)
            def _(win=win):
                def chunk(c, carry):
                    dw, dsc = carry
                    r0 = pl.multiple_of(c * ROWS, ROWS)
                    u = upad[pl.ds(PL_PAD + r0, ROWS), :]
                    ws = u
                    for j in range(1, win):
                        ws = ws + upad[pl.ds(PL_PAD + r0 - j, ROWS), :]
                    cnt = _pool_counts(r0, win)
                    pooled = (ws / cnt - u).astype(BF16)
                    dyv = dy_ref[pl.ds(r0, ROWS), :].astype(F32)
                    dsc = dsc + jnp.sum(dyv * _dot(pooled, wb, NN), axis=0, keepdims=True)
                    dys = (dyv * sv).astype(BF16)
                    dw = dw + _dot(pooled, dys, TN)
                    dp = _dot(dys, wb, NT)
                    dpn[pl.ds(r0, ROWS), :] = dp / cnt
                    nd[pl.ds(r0, ROWS), :] = -dp
                    return dw, dsc

                dw, dsc = lax.fori_loop(0, nchunk, chunk, (jnp.zeros((LANE, LANE), F32), jnp.zeros((1, LANE), F32)))
                dw_ref[...] = dw
                ds_ref[...] = jnp.broadcast_to(dsc, ds_ref.shape)

                def spread(c, _):
                    r0 = pl.multiple_of(c * ROWS, ROWS)
                    acc = nd[pl.ds(r0, ROWS), :]
                    for j in range(win):
                        acc = acc + dpn[pl.ds(r0 + j, ROWS), :]
                    du_ref[pl.ds(r0, ROWS), :] = acc.astype(du_ref.dtype)
                    return 0

                lax.fori_loop(0, nchunk, spread, 0)

    dproj, dw, ds = pl.pallas_call(
        body, name="pool_bwd", grid=(len(POOL_WINDOWS),),
        in_specs=[_colblk(S, OFF_PL), pl.BlockSpec((None, LANE, LANE), lambda j: (j, 0, 0)), pl.BlockSpec((1, LANE), lambda j: (0, j)),
                  pl.BlockSpec((S, LANE), lambda j: (0, 2 * BRANCH_W // LANE + j)), ANY_SPACE],
        out_specs=[_colblk(S, OFF_PL), pl.BlockSpec((None, LANE, LANE), lambda j: (j, 0, 0)), pl.BlockSpec((8, LANE), lambda j: (0, j))],
        out_shape=[jax.ShapeDtypeStruct(dproj.shape, dproj.dtype), jax.ShapeDtypeStruct((len(POOL_WINDOWS), LANE, LANE), F32),
                   jax.ShapeDtypeStruct((8, BRANCH_W), F32)],
        input_output_aliases={4: 0},
        scratch_shapes=[pltpu.VMEM((PL_PAD + S, LANE), F32), pltpu.VMEM((S + PL_PAD, LANE), F32), pltpu.VMEM((S, LANE), F32)],
        compiler_params=_cp(("parallel",)),
    )(proj, wg, scale.reshape(1, BRANCH_W), dbcat, dproj)
    return dproj, dw, ds[0]


LR_PAD = 8
SCAN_TILES = 4
GELU_C = math.sqrt(2.0 / math.pi)
GELU_A = 0.044715


def _gelu(y):
    return 0.5 * y * (1.0 + jnp.tanh(GELU_C * (y + GELU_A * y * y * y)))


def _dgelu(y):
    t = jnp.tanh(GELU_C * (y + GELU_A * y * y * y))
    return 0.5 * (1.0 + t) + 0.5 * y * (1.0 - t * t) * GELU_C * (1.0 + 3.0 * GELU_A * y * y)


def _lru_gates(xpad, r0, cw_ref, cb, wa, ba, wx, bx, sp8):
    xc = jnp.broadcast_to(cb, (ROWS, LANE))
    for j in range(LRU_CONV):
        xc = xc + cw_ref[pl.ds(j, 1), :] * xpad[pl.ds(r0 + (LR_PAD - LRU_CONV + 1) + j, ROWS), :]
    xb = xc.astype(BF16)
    r = _sigmoid(_dot(xb, wa, NN) + ba)
    ig = _sigmoid(_dot(xb, wx, NN) + bx)
    la = -sp8 * r
    a = jnp.exp(la)
    s = jnp.sqrt(-jnp.tanh(la) * (a * a + 1.0))
    return xc, r, ig, a, s


def _tile_scan(a, b, r8, up):
    for s in (1, 2, 4):
        keep = (r8 < 8 - s) if up else (r8 >= s)
        shift = 8 - s if up else s
        a_sh = jnp.where(keep, pltpu.roll(a, shift, 0), 1.0)
        b_sh = jnp.where(keep, pltpu.roll(b, shift, 0), 0.0)
        b = b + a * b_sh
        a = a * a_sh
    return a, b


def lru_fwd(proj, cw8, cb, wa_bd, ba, wx_bd, bx, sp8, bcat):
    S = proj.shape[0]
    nchunk = S // ROWS

    def body(x_ref, y_ref, cw_ref, cb_ref, wa_ref, ba_ref, wx_ref, bx_ref, sp_ref, _, o_ref, h_ref, xpad, a_s):
        xpad[pl.ds(0, LR_PAD), :] = jnp.zeros((LR_PAD, LANE), F32)

        def fill(c, _):
            r0 = pl.multiple_of(c * ROWS, ROWS)
            xpad[pl.ds(LR_PAD + r0, ROWS), :] = x_ref[pl.ds(r0, ROWS), :]
            return 0

        lax.fori_loop(0, nchunk, fill, 0)
        wa = wa_ref[...].astype(BF16)
        wx = wx_ref[...].astype(BF16)

        def gates(c, _):
            r0 = pl.multiple_of(c * ROWS, ROWS)
            xc, r, ig, a, s = _lru_gates(xpad, r0, cw_ref, cb_ref[...], wa, ba_ref[...], wx, bx_ref[...], sp_ref[...])
            a_s[pl.ds(r0, ROWS), :] = a
            h_ref[pl.ds(r0, ROWS), :] = s * (ig * xc)
            return 0

        lax.fori_loop(0, nchunk, gates, 0)

        r8 = lax.broadcasted_iota(I32, (8, LANE), 0)

        def scan(i, h):
            bases = [pl.multiple_of(i * (8 * SCAN_TILES) + 8 * j, 8) for j in range(SCAN_TILES)]
            maps = [_tile_scan(a_s[pl.ds(b, 8), :], h_ref[pl.ds(b, 8), :], r8, False) for b in bases]
            for b, (ca, cb_) in zip(bases, maps):
                out = cb_ + ca * h
                h_ref[pl.ds(b, 8), :] = out
                h = out[7:8, :]
            return h

        lax.fori_loop(0, S // (8 * SCAN_TILES), scan, jnp.zeros((1, LANE), F32))

        def gate_out(c, _):
            r0 = pl.multiple_of(c * ROWS, ROWS)
            o_ref[pl.ds(r0, ROWS), :] = (h_ref[pl.ds(r0, ROWS), :] * _gelu(y_ref[pl.ds(r0, ROWS), :])).astype(o_ref.dtype)
            return 0

        lax.fori_loop(0, nchunk, gate_out, 0)

    vec = pl.BlockSpec((1, LANE), lambda j: (0, j))
    mat = pl.BlockSpec((None, LANE, LANE), lambda j: (j, 0, 0))
    blk = pl.BlockSpec((S, LANE), lambda j: (0, j))
    return pl.pallas_call(
        body, name="lru_fwd", grid=(BRANCH_W // LANE,),
        in_specs=[_colblk(S, OFF_LX), _colblk(S, OFF_LY), pl.BlockSpec((8, LANE), lambda j: (0, j)), vec, mat, vec, mat, vec, vec,
                  ANY_SPACE],
        out_specs=[pl.BlockSpec((S, LANE), lambda j: (0, 3 * BRANCH_W // LANE + j)), blk],
        out_shape=[jax.ShapeDtypeStruct(bcat.shape, bcat.dtype), jax.ShapeDtypeStruct((S, BRANCH_W), F32)],
        input_output_aliases={9: 0},
        scratch_shapes=[pltpu.VMEM((LR_PAD + S, LANE), F32), pltpu.VMEM((S, LANE), F32)],
        compiler_params=_cp(("parallel",)),
    )(proj, proj, cw8, cb.reshape(1, -1), wa_bd, ba.reshape(1, -1), wx_bd, bx.reshape(1, -1), sp8.reshape(1, -1), bcat)


def lru_bwd(proj, cw8, cb, wa_bd, ba, wx_bd, bx, sp8, h, dbcat):
    S = proj.shape[0]
    nchunk = S // ROWS

    def body(x_ref, y_ref, cw_ref, cb_ref, wa_ref, ba_ref, wx_ref, bx_ref, sp_ref, h_ref, do_ref,
             dx_ref, dy_ref, dcw_ref, dcb_ref, dwa_ref, dba_ref, dwx_ref, dbx_ref, dsp_ref,
             xpad, a_s, g_s, hpad, dxc):
        xpad[pl.ds(0, LR_PAD), :] = jnp.zeros((LR_PAD, LANE), F32)
        hpad[pl.ds(0, LR_PAD), :] = jnp.zeros((LR_PAD, LANE), F32)
        dxc[pl.ds(S, LR_PAD), :] = jnp.zeros((LR_PAD, LANE), F32)
        dcw_ref[...] = jnp.zeros_like(dcw_ref)
        wa = wa_ref[...].astype(BF16)
        wx = wx_ref[...].astype(BF16)
        cbv, bav, bxv, spv = cb_ref[...], ba_ref[...], bx_ref[...], sp_ref[...]

        def fill(c, _):
            r0 = pl.multiple_of(c * ROWS, ROWS)
            xpad[pl.ds(LR_PAD + r0, ROWS), :] = x_ref[pl.ds(r0, ROWS), :]
            hv = h_ref[pl.ds(r0, ROWS), :]
            hpad[pl.ds(LR_PAD + r0, ROWS), :] = hv
            yv = y_ref[pl.ds(r0, ROWS), :]
            dov = do_ref[pl.ds(r0, ROWS), :].astype(F32)
            g_s[pl.ds(r0, ROWS), :] = dov * _gelu(yv)
            dy_ref[pl.ds(r0, ROWS), :] = (dov * hv * _dgelu(yv)).astype(dy_ref.dtype)
            return 0

        lax.fori_loop(0, nchunk, fill, 0)

        def gates(c, _):
            r0 = pl.multiple_of(c * ROWS, ROWS)
            _, _, _, a, _ = _lru_gates(xpad, r0, cw_ref, cbv, wa, bav, wx, bxv, spv)
            a_s[pl.ds(r0, ROWS), :] = a
            return 0

        lax.fori_loop(0, nchunk, gates, 0)

        r8 = lax.broadcasted_iota(I32, (8, LANE), 0)

        def rscan(i, carry):
            bases = [pl.multiple_of(S - 8 - i * (8 * SCAN_TILES) - 8 * j, 8) for j in range(SCAN_TILES)]
            firsts, maps = [], []
            for b in bases:
                a8 = a_s[pl.ds(b, 8), :]
                above = jnp.where(r8 < 7, pltpu.roll(a8, 7, 0), 1.0)
                firsts.append(a8[0:1, :])
                maps.append(_tile_scan(above, g_s[pl.ds(b, 8), :], r8, True))
            for b, a0, (ca, cb_) in zip(bases, firsts, maps):
                out = cb_ + ca * carry
                g_s[pl.ds(b, 8), :] = out
                carry = a0 * out[0:1, :]
            return carry

        lax.fori_loop(0, S // (8 * SCAN_TILES), rscan, jnp.zeros((1, LANE), F32))

        def chain(c, carry):
            dwa, dwx, dba, dbx, dsp, dcb = carry
            r0 = pl.multiple_of(c * ROWS, ROWS)
            xc, r, ig, a, s = _lru_gates(xpad, r0, cw_ref, cbv, wa, bav, wx, bxv, spv)
            gt = g_s[pl.ds(r0, ROWS), :]
            hprev = hpad[pl.ds(r0 + LR_PAD - 1, ROWS), :]
            da = gt * hprev - gt * ig * xc * (a / s)
            dig = gt * s * xc
            dla = da * a
            dsp = dsp + jnp.sum(-dla * r, axis=0, keepdims=True)
            dpr = (-dla * spv) * r * (1.0 - r)
            dpi = dig * ig * (1.0 - ig)
            dprb, dpib, xb = dpr.astype(BF16), dpi.astype(BF16), xc.astype(BF16)
            d = gt * s * ig + _dot(dprb, wa, NT) + _dot(dpib, wx, NT)
            dwa = dwa + _dot(xb, dprb, TN)
            dwx = dwx + _dot(xb, dpib, TN)
            dba = dba + jnp.sum(dpr, axis=0, keepdims=True)
            dbx = dbx + jnp.sum(dpi, axis=0, keepdims=True)
            dcb = dcb + jnp.sum(d, axis=0, keepdims=True)
            dxc[pl.ds(r0, ROWS), :] = d
            for j in range(LRU_CONV):
                xs = xpad[pl.ds(r0 + (LR_PAD - LRU_CONV + 1) + j, ROWS), :]
                dcw_ref[pl.ds(j, 1), :] += jnp.sum(d * xs, axis=0, keepdims=True)
            return dwa, dwx, dba, dbx, dsp, dcb

        zm, zv = jnp.zeros((LANE, LANE), F32), jnp.zeros((1, LANE), F32)
        dwa, dwx, dba, dbx, dsp, dcb = lax.fori_loop(0, nchunk, chain, (zm, zm, zv, zv, zv, zv))
        dwa_ref[...] = dwa
        dwx_ref[...] = dwx
        dba_ref[...] = jnp.broadcast_to(dba, dba_ref.shape)
        dbx_ref[...] = jnp.broadcast_to(dbx, dbx_ref.shape)
        dsp_ref[...] = jnp.broadcast_to(dsp, dsp_ref.shape)
        dcb_ref[...] = jnp.broadcast_to(dcb, dcb_ref.shape)

        def convt(c, _):
            r0 = pl.multiple_of(c * ROWS, ROWS)
            acc = jnp.zeros((ROWS, LANE), F32)
            for j in range(LRU_CONV):
                acc = acc + cw_ref[pl.ds(j, 1), :] * dxc[pl.ds(r0 + (LRU_CONV - 1) - j, ROWS), :]
            dx_ref[pl.ds(r0, ROWS), :] = acc.astype(dx_ref.dtype)
            return 0

        lax.fori_loop(0, nchunk, convt, 0)

    vec = pl.BlockSpec((1, LANE), lambda j: (0, j))
    vec8 = pl.BlockSpec((8, LANE), lambda j: (0, j))
    mat = pl.BlockSpec((None, LANE, LANE), lambda j: (j, 0, 0))
    blk = pl.BlockSpec((S, LANE), lambda j: (0, j))
    nblk = BRANCH_W // LANE
    v8 = jax.ShapeDtypeStruct((8, BRANCH_W), F32)
    m4 = jax.ShapeDtypeStruct((nblk, LANE, LANE), F32)
    big = jax.ShapeDtypeStruct((S, BRANCH_W), BF16)
    seq = pltpu.VMEM((S, LANE), F32)
    dx, dy, dcw, dcb, dwa, dba, dwx, dbx, dsp = pl.pallas_call(
        body, name="lru_bwd", grid=(nblk,),
        in_specs=[_colblk(S, OFF_LX), _colblk(S, OFF_LY), vec8, vec, mat, vec, mat, vec, vec, blk,
                  pl.BlockSpec((S, LANE), lambda j: (0, 3 * BRANCH_W // LANE + j))],
        out_specs=[blk, blk, vec8, vec8, mat, vec8, mat, vec8, vec8],
        out_shape=[big, big, v8, v8, m4, v8, m4, v8, v8],
        scratch_shapes=[pltpu.VMEM((LR_PAD + S, LANE), F32), seq, seq, pltpu.VMEM((LR_PAD + S, LANE), F32),
                        pltpu.VMEM((S + LR_PAD, LANE), F32)],
        compiler_params=_cp(("parallel",)),
    )(proj, proj, cw8, cb.reshape(1, -1), wa_bd, ba.reshape(1, -1), wx_bd, bx.reshape(1, -1), sp8.reshape(1, -1), h, dbcat)
    return dx, dy, dcw, dcb[0], dwa, dba[0], dwx, dbx[0], dsp[0]


MG_COLS = 512
N_BRANCH = 4
BCAT_W = N_BRANCH * BRANCH_W


def merge_fwd(bcat, wb, proj, gate_b, tm=2048, after=()):
    S = proj.shape[0]
    tm = _tile(S, tm)
    halves = D_MODEL // MG_COLS

    def body(a_ref, w_ref, g_ref, gb_ref, *rest):
        up_ref, o_ref, acc_ref = rest[len(after):]
        k = pl.program_id(2)
        up = _dot(a_ref[...], w_ref[...], NN)
        up_ref[...] = up.astype(up_ref.dtype)
        term = _sigmoid(g_ref[...] + gb_ref[pl.ds(k, 1), :]) * up

        @pl.when(k == 0)
        def _():
            acc_ref[...] = term

        @pl.when(k > 0)
        def _():
            acc_ref[...] += term

        @pl.when(k == N_BRANCH - 1)
        def _():
            o_ref[...] = acc_ref[...].astype(o_ref.dtype)

    return pl.pallas_call(
        body, name="merge_fwd", grid=(S // tm, halves, N_BRANCH),
        in_specs=[pl.BlockSpec((tm, BRANCH_W), lambda i, j, k: (i, k)),
                  pl.BlockSpec((None, BRANCH_W, MG_COLS), lambda i, j, k: (k, 0, j)),
                  pl.BlockSpec((tm, MG_COLS), lambda i, j, k: (i, OFF_GATE // MG_COLS + k * halves + j)),
                  pl.BlockSpec((N_BRANCH, MG_COLS), lambda i, j, k: (0, j))] + [ANY_SPACE] * len(after),
        out_specs=[pl.BlockSpec((tm, MG_COLS), lambda i, j, k: (i, k * halves + j)),
                   pl.BlockSpec((tm, MG_COLS), lambda i, j, k: (i, j))],
        out_shape=[jax.ShapeDtypeStruct((S, N_BRANCH * D_MODEL), BF16), jax.ShapeDtypeStruct((S, D_MODEL), BF16)],
        scratch_shapes=[pltpu.VMEM((tm, MG_COLS), F32)],
        compiler_params=_cp(("parallel", "parallel", "arbitrary")),
    )(bcat, wb, proj, gate_b, *after)


def merge_bwd(dmerged, ups, proj, gate_b, tm=1024):
    S = proj.shape[0]
    tm = _tile(S, tm)
    halves = D_MODEL // MG_COLS

    def body(dm_ref, u_ref, g_ref, gb_ref, du_ref, dg_ref, dgb_ref):
        k = pl.program_id(0)

        @pl.when(pl.program_id(2) == 0)
        def _():
            dgb_ref[...] = jnp.zeros_like(dgb_ref)

        dm = dm_ref[...].astype(F32)
        sg = _sigmoid(g_ref[...] + gb_ref[pl.ds(k, 1), :])
        du_ref[...] = (dm * sg).astype(du_ref.dtype)
        dgk = dm * u_ref[...] * sg * (1.0 - sg)
        dg_ref[...] = dgk.astype(dg_ref.dtype)
        dgb_ref[...] += jnp.broadcast_to(jnp.sum(dgk, axis=0, keepdims=True), dgb_ref.shape)

    dups, dproj, dgb = pl.pallas_call(
        body, name="merge_bwd", grid=(N_BRANCH, halves, S // tm),
        in_specs=[pl.BlockSpec((tm, MG_COLS), lambda k, j, i: (i, j)),
                  pl.BlockSpec((tm, MG_COLS), lambda k, j, i: (i, k * halves + j)),
                  pl.BlockSpec((tm, MG_COLS), lambda k, j, i: (i, OFF_GATE // MG_COLS + k * halves + j)),
                  pl.BlockSpec((N_BRANCH, MG_COLS), lambda k, j, i: (0, j))],
        out_specs=[pl.BlockSpec((tm, MG_COLS), lambda k, j, i: (i, k * halves + j)),
                   pl.BlockSpec((tm, MG_COLS), lambda k, j, i: (i, OFF_GATE // MG_COLS + k * halves + j)),
                   pl.BlockSpec((8, MG_COLS), lambda k, j, i: (k, j))],
        out_shape=[jax.ShapeDtypeStruct((S, N_BRANCH * D_MODEL), BF16), jax.ShapeDtypeStruct((S, IN_W), BF16),
                   jax.ShapeDtypeStruct((8 * N_BRANCH, D_MODEL), F32)],
        compiler_params=_cp(("parallel", "parallel", "arbitrary")),
    )(dmerged, ups, proj, gate_b)
    return dups, dproj, dgb.reshape(N_BRANCH, 8, D_MODEL)[:, 0]


def attn_fwd(q, kv, tm=512):
    S = q.shape[0]
    M = kv.shape[0]
    tm = _tile(S, tm)
    scale = XA_HD ** -0.5

    def body(q_ref, kv_ref, o_ref):
        for hh in range(XA_HEADS):
            cs = pl.ds(hh * XA_HD, XA_HD)
            qh = q_ref[:, cs]
            kh = kv_ref[:, cs]
            vh = kv_ref[:, pl.ds(D_MODEL + hh * XA_HD, XA_HD)]
            s = _dot(qh, kh, NT) * scale
            p = jnp.exp(s - jnp.max(s, axis=-1, keepdims=True))
            p = p / jnp.sum(p, axis=-1, keepdims=True)
            o_ref[:, cs] = _dot(p.astype(BF16), vh, NN).astype(o_ref.dtype)

    return pl.pallas_call(
        body, name="attn_fwd", grid=(S // tm,),
        in_specs=[pl.BlockSpec((tm, D_MODEL), lambda i: (i, 0)), pl.BlockSpec((M, 2 * D_MODEL), lambda i: (0, 0))],
        out_specs=pl.BlockSpec((tm, D_MODEL), lambda i: (i, 0)), out_shape=jax.ShapeDtypeStruct((S, D_MODEL), BF16),
        compiler_params=_cp(("parallel",)),
    )(q, kv)


def attn_bwd(q, kv, do, tm=512):
    S = q.shape[0]
    M = kv.shape[0]
    tm = _tile(S, tm)
    scale = XA_HD ** -0.5

    def body(q_ref, kv_ref, do_ref, dq_ref, dkv_ref):
        @pl.when(pl.program_id(0) == 0)
        def _():
            dkv_ref[...] = jnp.zeros_like(dkv_ref)

        for hh in range(XA_HEADS):
            cs = pl.ds(hh * XA_HD, XA_HD)
            vs = pl.ds(D_MODEL + hh * XA_HD, XA_HD)
            qh = q_ref[:, cs]
            kh = kv_ref[:, cs]
            vh = kv_ref[:, vs]
            doh = do_ref[:, cs]
            s = _dot(qh, kh, NT) * scale
            p = jnp.exp(s - jnp.max(s, axis=-1, keepdims=True))
            p = p / jnp.sum(p, axis=-1, keepdims=True)
            dp = _dot(doh, vh, NT)
            ds = (p * (dp - jnp.sum(dp * p, axis=-1, keepdims=True)) * scale).astype(BF16)
            dq_ref[:, cs] = _dot(ds, kh, NN).astype(dq_ref.dtype)
            dkv_ref[:, cs] += _dot(ds, qh, TN)
            dkv_ref[:, vs] += _dot(p.astype(BF16), doh, TN)

    row = pl.BlockSpec((tm, D_MODEL), lambda i: (i, 0))
    full = pl.BlockSpec((M, 2 * D_MODEL), lambda i: (0, 0))
    return pl.pallas_call(
        body, name="attn_bwd", grid=(S // tm,), in_specs=[row, full, row], out_specs=[row, full],
        out_shape=[jax.ShapeDtypeStruct((S, D_MODEL), BF16), jax.ShapeDtypeStruct((M, 2 * D_MODEL), F32)],
        compiler_params=_cp(("arbitrary",)),
    )(q, kv, do)


def sum_parts(parts, own=None, tm=256):
    n, R, C = parts.shape
    tm = _tile(R, tm)
    has_own = own is not None

    def body(*refs):
        p_ref, o_ref = refs[0], refs[-1]
        acc = refs[1][...].astype(F32) if has_own else p_ref[0].astype(F32)
        for j in range(0 if has_own else 1, n):
            acc = acc + p_ref[j].astype(F32)
        o_ref[...] = acc

    row = pl.BlockSpec((tm, C), lambda i: (i, 0))
    return pl.pallas_call(
        body, name="sum_parts", grid=(R // tm,),
        in_specs=[pl.BlockSpec((n, tm, C), lambda i: (0, i, 0))] + ([row] if has_own else []), out_specs=row,
        out_shape=jax.ShapeDtypeStruct((R, C), F32), compiler_params=_cp(("parallel",)),
    )(*([parts, own] if has_own else [parts]))


def adamw(w, g, m, v, tm=256):
    L, R, C = w.shape
    tm = _tile(R, tm)
    c1 = 1.0 / (1.0 - ADAM_B1 ** ADAM_STEP)
    c2 = 1.0 / (1.0 - ADAM_B2 ** ADAM_STEP)

    def body(w_ref, g_ref, m_ref, v_ref, d_ref, nm_ref, nv_ref):
        gv = g_ref[...]
        nm = ADAM_B1 * m_ref[...] + (1.0 - ADAM_B1) * gv
        nv = ADAM_B2 * v_ref[...] + (1.0 - ADAM_B2) * (gv * gv)
        nm_ref[...] = nm
        nv_ref[...] = nv
        d_ref[...] = -ADAM_LR * ((nm * c1) / (jnp.sqrt(nv * c2) + ADAM_EPS) + ADAM_WD * w_ref[...])

    blk = pl.BlockSpec((None, tm, C), lambda l, i: (l, i, 0))
    sd = jax.ShapeDtypeStruct((L, R, C), F32)
    return pl.pallas_call(
        body, name="adamw", grid=(L, R // tm), in_specs=[blk] * 4, out_specs=[blk] * 3, out_shape=[sd] * 3,
        compiler_params=_cp(("parallel", "parallel")),
    )(w, g, m, v)


ANY = pl.BlockSpec(memory_space=pl.ANY)


def _place():
    return lax.axis_index("x"), lax.axis_index("y"), lax.axis_index("c")


def _slot(px, py, pc):
    return 4 * px + 2 * py + pc


def all_gather(name, shards, after=()):
    n = len(shards)
    n_in = n + len(after)

    def body(*refs):
        x_refs, out_refs = refs[:n], refs[n_in:n_in + n]
        send_sems, recv_sems, local_sems = refs[n_in + n:]
        x, y, c = _place()
        me, sibling = (x, y, c), (x, y, 1 - c)
        chips = [(1 - x, y), (x, 1 - y), (1 - x, 1 - y)]

        def copy(a, k, block, to, src=None):
            rows = out_refs[a].at[_slot(*block)]
            return pltpu.make_async_remote_copy(
                src_ref=rows if src is None else src, dst_ref=rows,
                send_sem=send_sems.at[7 * a + k], recv_sem=recv_sems.at[7 * a + k],
                device_id=to, device_id_type=MESH)

        mine = [pltpu.make_async_copy(x_refs[a], out_refs[a].at[_slot(*me)], local_sems.at[a]) for a in range(n)]
        for cp in mine:
            cp.start()
        first = []
        for a in range(n):
            first.append(copy(a, 0, me, sibling, src=x_refs[a]))
            first += [copy(a, 1 + j, me, (*chip, c), src=x_refs[a]) for j, chip in enumerate(chips)]
        for cp in first:
            cp.start()
        passed = []
        for a in range(n):
            for j, chip in enumerate(chips):
                copy(a, 1 + j, (*chip, c), me).wait_recv()
                cp = copy(a, 4 + j, (*chip, c), sibling)
                cp.start()
                passed.append(cp)
        for a in range(n):
            copy(a, 0, sibling, me).wait_recv()
            for j, chip in enumerate(chips):
                copy(a, 4 + j, (*chip, 1 - c), me).wait_recv()
        for cp in first + passed:
            cp.wait_send()
        for cp in mine:
            cp.wait()

    return pl.pallas_call(
        body, name=name, in_specs=[ANY] * n_in, out_specs=[ANY] * n,
        out_shape=[jax.ShapeDtypeStruct((N_DEV, *s.shape), s.dtype) for s in shards],
        scratch_shapes=[pltpu.SemaphoreType.DMA((7 * n,)), pltpu.SemaphoreType.DMA((7 * n,)), pltpu.SemaphoreType.DMA((n,))],
    )(*shards, *after)


HBM = pl.BlockSpec(memory_space=pltpu.HBM)
SEM = pl.BlockSpec(memory_space=pltpu.SEMAPHORE)
EFFECT = pltpu.SideEffectType.DATAFLOW_SIDE_EFFECTING
N_PEER = N_DEV - 1
RELATIONS = [(dx, dy, dc) for dx in (0, 1) for dy in (0, 1) for dc in (0, 1)][1:]


def _peer(place, rel):
    return tuple(1 - v if d else v for v, d in zip(place, rel))


def gather_start(name, shards, me, before):
    n = len(shards)

    def body(*refs):
        x_refs, land_refs = refs[:n], refs[n:2 * n]
        send_sems, recv_sems = refs[2 * n + len(before):2 * n + len(before) + 2]
        token = refs[-1]
        place = _place()
        mine = _slot(*place)
        for a in range(n):
            for rel in RELATIONS:
                pltpu.make_async_remote_copy(
                    src_ref=x_refs[a], dst_ref=land_refs[a].at[mine], send_sem=send_sems.at[a], recv_sem=recv_sems.at[a],
                    device_id=_peer(place, rel), device_id_type=MESH).start()
        token[...] = jnp.zeros_like(token)

    lands = [lax.dynamic_update_index_in_dim(lax.empty((N_DEV, *s.shape), s.dtype), s, me, 0) for s in shards]
    outs = pl.pallas_call(
        body, name=name,
        in_specs=[HBM] * (2 * n) + [ANY] * len(before),
        out_specs=[SEM, SEM] + [HBM] * (2 * n) + [pl.BlockSpec(memory_space=pltpu.VMEM)],
        out_shape=[pltpu.SemaphoreType.DMA((n,)), pltpu.SemaphoreType.DMA((n,))]
        + [pltpu.HBM(t.shape, t.dtype) for t in (*shards, *lands)] + [jax.ShapeDtypeStruct((8, LANE), F32)],
        input_output_aliases={i: 2 + i for i in range(2 * n)},
        compiler_params=pltpu.CompilerParams(has_side_effects=EFFECT),
    )(*[pltpu.with_memory_space_constraint(t, pltpu.HBM) for t in (*shards, *lands)], *before)
    return (outs[0], outs[1], outs[2:2 + n], outs[2 + n:2 + 2 * n]), outs[-1]


def gather_wait(name, state, after):
    send_sems, recv_sems, shards, lands = state
    n = len(shards)

    def body(*refs):
        land_refs = refs[n:2 * n]
        s_sems, r_sems = refs[2 * n:2 * n + 2]
        place = _place()
        for a in range(n):
            seven = land_refs[a].at[pl.ds(0, N_PEER)]
            cp = pltpu.make_async_remote_copy(
                src_ref=seven, dst_ref=seven, send_sem=s_sems.at[a], recv_sem=r_sems.at[a], device_id=place, device_id_type=MESH)
            cp.wait_send()
            cp.wait_recv()

    outs = pl.pallas_call(
        body, name=name,
        in_specs=[HBM] * (2 * n) + [SEM, SEM] + [ANY] * len(after), out_specs=[HBM] * (2 * n),
        out_shape=[pltpu.HBM(t.shape, t.dtype) for t in (*shards, *lands)],
        input_output_aliases={i: i for i in range(2 * n)},
        compiler_params=pltpu.CompilerParams(has_side_effects=EFFECT),
    )(*shards, *lands, send_sems, recv_sems, *after)
    return outs[n:]


def exchange_start(name, grads, before):
    n = len(grads)

    def body(*refs):
        g_refs, land_refs = refs[:n], refs[n:2 * n]
        send_sems, recv_sems = refs[2 * n + len(before):2 * n + len(before) + 2]
        token = refs[-1]
        place = _place()
        for a in range(n):
            for r, rel in enumerate(RELATIONS):
                p = _peer(place, rel)
                pltpu.make_async_remote_copy(
                    src_ref=g_refs[a].at[_slot(*p)], dst_ref=land_refs[a].at[r],
                    send_sem=send_sems.at[a], recv_sem=recv_sems.at[a], device_id=p, device_id_type=MESH).start()
        token[...] = jnp.zeros_like(token)

    lands = [lax.empty((N_PEER, *g.shape[1:]), g.dtype) for g in grads]
    outs = pl.pallas_call(
        body, name=name,
        in_specs=[HBM] * (2 * n) + [ANY] * len(before),
        out_specs=[SEM, SEM] + [HBM] * (2 * n) + [pl.BlockSpec(memory_space=pltpu.VMEM)],
        out_shape=[pltpu.SemaphoreType.DMA((n,)), pltpu.SemaphoreType.DMA((n,))]
        + [pltpu.HBM(g.shape, g.dtype) for g in grads] + [pltpu.HBM(t.shape, t.dtype) for t in lands]
        + [jax.ShapeDtypeStruct((8, LANE), F32)],
        input_output_aliases={i: 2 + i for i in range(2 * n)},
        compiler_params=pltpu.CompilerParams(has_side_effects=EFFECT),
    )(*[pltpu.with_memory_space_constraint(t, pltpu.HBM) for t in (*grads, *lands)], *before)
    return (outs[0], outs[1], outs[2:2 + n], outs[2 + n:2 + 2 * n]), outs[-1]


def exchange_wait(name, state, after):
    send_sems, recv_sems, grads, lands = state
    n = len(grads)

    def body(*refs):
        g_refs, land_refs = refs[:n], refs[n:2 * n]
        s_sems, r_sems = refs[2 * n:2 * n + 2]
        place = _place()
        for a in range(n):
            cp = pltpu.make_async_remote_copy(
                src_ref=g_refs[a].at[pl.ds(0, N_PEER)], dst_ref=land_refs[a],
                send_sem=s_sems.at[a], recv_sem=r_sems.at[a], device_id=place, device_id_type=MESH)
            cp.wait_send()
            cp.wait_recv()

    outs = pl.pallas_call(
        body, name=name,
        in_specs=[HBM] * (2 * n) + [SEM, SEM, ANY], out_specs=[HBM] * (2 * n),
        out_shape=[pltpu.HBM(t.shape, t.dtype) for t in (*grads, *lands)],
        input_output_aliases={i: i for i in range(2 * n)},
        compiler_params=pltpu.CompilerParams(has_side_effects=EFFECT),
    )(*grads, *lands, send_sems, recv_sems, after)
    return outs[:n], outs[n:]


WEIGHTS = ['norm_mix_w', 'w_in', 'hg_lb_raw', 'hg_norm_w', 'cv_dw_w', 'cv_dw_b', 'cv_ln_w', 'cv_ln_b', 'pl_w', 'pl_scale',
           'lru_conv_w', 'lru_conv_b', 'lru_wa', 'lru_ba', 'lru_wx', 'lru_bx', 'lru_lambda', 'gate_b', 'w_branch', 'w_out',
           'norm_mem_w', 'mem_norm_w', 'xa_wq', 'xa_wkv', 'xa_wo', 'norm_ffn_w', 'ffn_w1', 'ffn_w2', 'final_norm_w']
BIG = ('w_in', 'w_branch', 'w_out', 'xa_wq', 'xa_wkv', 'xa_wo', 'ffn_w1', 'ffn_w2')
SMALL_SHARDED = ('cv_dw_w', 'lru_conv_w', 'gate_b')
SMALL = tuple(n for n in WEIGHTS if n not in BIG and n not in SMALL_SHARDED)
PACK_ROWS = 256


def _pack(arrs):
    flat = jnp.concatenate([a.reshape(-1).astype(F32) for a in arrs])
    tile = PACK_ROWS * LANE
    padded = -(-flat.shape[0] // tile) * tile
    return jnp.pad(flat, (0, padded - flat.shape[0])).reshape(-1, LANE)


def _unpack(packed, shapes):
    flat = packed.reshape(-1)
    out, off = [], 0
    for s in shapes:
        n = math.prod(s)
        out.append(flat[off:off + n].reshape(s))
        off += n
    return out


def _gather_last(g, shard_shape):
    nd = len(shard_shape)
    full = jnp.moveaxis(g, 0, nd - 1)
    return full.reshape(*shard_shape[:-1], N_DEV * shard_shape[-1])


def _natural(blocks):
    nb, k, c = blocks.shape
    return jnp.transpose(blocks, (1, 0, 2)).reshape(k, nb * c)


def _block_diag(w):
    w2 = w.reshape(4, 2, 64, 64)
    z = jnp.zeros((4, 64, 64), w.dtype)
    return jnp.concatenate([jnp.concatenate([w2[:, 0], z], axis=2), jnp.concatenate([z, w2[:, 1]], axis=2)], axis=1)


def _block_diag_t(d):
    return jnp.stack([d[:, :64, :64], d[:, 64:, 64:]], axis=1).reshape(8, 64, 64)


def _lower_bounds(raw):
    lb = jnp.cumsum(jax.nn.softmax(raw.astype(F32), axis=0), axis=0)
    return lb - lb[0:1]


def _decay_rates(lam):
    return (LRU_C * jax.nn.softplus(-lam.astype(F32))).reshape(DEPTH, BRANCH_W)


def _relu2(acc):
    r = jnp.maximum(acc, 0.0)
    return acc, r * r


def _relu2_grad(acc, u):
    return (acc * 2.0 * jnp.maximum(u, 0.0),)


def _add(acc, e):
    return (acc + e,)


def _layer_fwd(x0, mem, p, g, rest, after=()):
    h1 = rms_fwd("rms_mix", x0, p['norm_mix_w'])
    proj = mm_nt("mm_in", h1, g['w_in'], tn=2176, after=after)[0]
    bcat, states, o_hg = hgrn_fwd(proj, p['lb'], p['hg_norm_w'])
    zc = cv_fwd(proj, p['cv_w32'], p['cv_dw_b'])
    bcat = ln_silu_fwd(zc, p['cv_ln_w'], p['cv_ln_b'], bcat)
    bcat = pool_fwd(proj, p['pl_w'], p['pl_scale'], bcat)
    bcat, hst = lru_fwd(proj, p['lru_cw8'], p['lru_conv_b'], p['wa_bd'], p['lru_ba'], p['wx_bd'], p['lru_bx'], p['sp8'], bcat)
    more, after = rest(bcat)
    g = {**g, **more}
    ups, merged = merge_fwd(bcat, g['w_branch'], proj, p['gate_b'], after=after)
    x1 = mm_nn("mm_out", merged, g['w_out'], epi=_add, extras=(x0,))[0]
    h2 = rms_fwd("rms_mem", x1, p['norm_mem_w'])
    q = mm_nn("mm_q", h2, g['xa_wq'], out_dtype=BF16)[0]
    memn = rms_fwd("rms_memtok", mem, p['mem_norm_w'])
    kv = mm_nn("mm_kv", memn, g['xa_wkv'], out_dtype=BF16, tn=2048)[0]
    oa = attn_fwd(q, kv)
    x2 = mm_nn("mm_o", oa, g['xa_wo'], epi=_add, extras=(x1,))[0]
    h3 = rms_fwd("rms_ffn", x2, p['norm_ffn_w'])
    u, act = mm_nn("mm_ffn1", h3, g['ffn_w1'], epi=_relu2, out_dtypes=[BF16, BF16])
    x3 = mm_nn("mm_ffn2", act, g['ffn_w2'], epi=_add, extras=(x2,))[0]
    res = dict(x0=x0, h1=h1, proj=proj, states=states, o_hg=o_hg, zc=zc, hst=hst, bcat=bcat, ups=ups, merged=merged,
               x1=x1, h2=h2, q=q, memn=memn, kv=kv, oa=oa, x2=x2, h3=h3, u=u, act=act)
    return x3, res, g


def _layer_bwd(dx3, mem, p, g, r, midway, finish):
    gs, gb = {}, {}
    du = mm_nt("mm_dffn2", dx3, g['ffn_w2'], out_dtype=BF16, epi=_relu2_grad, extras=(r['u'],))[0]
    gb['ffn_w2'] = mm_tn("mm_gw2", r['act'], dx3).reshape(N_DEV, -1, D_MODEL)
    gb['ffn_w1'] = mm_tn_cb("mm_gw1", r['h3'], du, N_DEV)
    dh3 = mm_nt("mm_dffn1", du, g['ffn_w1'], out_dtype=BF16)[0]
    dx2, gs['norm_ffn_w'] = rms_bwd("rmsb_ffn", r['x2'], p['norm_ffn_w'], dh3, dx3)
    doa = mm_nt("mm_do", dx2, g['xa_wo'], out_dtype=BF16)[0]
    gb['xa_wo'] = mm_tn("mm_gwo", r['oa'], dx2).reshape(N_DEV, -1, D_MODEL)
    dq, dkv = attn_bwd(r['q'], r['kv'], doa)
    gb['xa_wq'] = mm_tn("mm_gwq", r['h2'], dq).reshape(N_DEV, -1, D_MODEL)
    dh2 = mm_nt("mm_dq", dq, g['xa_wq'], out_dtype=BF16)[0]
    gb['xa_wkv'] = mm_tn_cb("mm_gwkv", r['memn'], dkv, N_DEV)
    dmemn = mm_nt("mm_dkv", dkv, g['xa_wkv'], out_dtype=BF16)[0]
    _, gs['mem_norm_w'] = rms_bwd("rmsb_memtok", mem, p['mem_norm_w'], dmemn)
    dx1, gs['norm_mem_w'] = rms_bwd("rmsb_mem", r['x1'], p['norm_mem_w'], dh2, dx2)
    after = midway(gb, dx1)
    gb = {}
    dmerged = mm_nt("mm_dout", dx1, g['w_out'], out_dtype=BF16, after=after)[0]
    gb['w_out'] = mm_tn("mm_gwout", r['merged'], dx1).reshape(N_DEV, -1, D_MODEL)
    dups, dproj, gs['gate_b'] = merge_bwd(dmerged, r['ups'], r['proj'], p['gate_b'])
    gwb = mm_branch_tn("mm_gwb", r['bcat'], dups, N_BRANCH)
    gb['w_branch'] = jnp.transpose(gwb.reshape(N_BRANCH, BRANCH_W, N_DEV, -1), (2, 0, 1, 3))
    dbcat = mm_branch_nt("mm_dup", dups, g['w_branch'], tm=2048)
    dproj, gs['lb'], gs['hg_norm_w'] = hgrn_bwd(r['proj'], p['lb'], p['hg_norm_w'], r['states'], r['o_hg'], dbcat, dproj)
    dzc, gs['cv_ln_w'], gs['cv_ln_b'] = ln_silu_bwd(r['zc'], p['cv_ln_w'], p['cv_ln_b'], dbcat)
    dca, dcg, dcw, gs['cv_dw_b'] = cv_bwd(r['proj'], p['cv_w32'], dzc)
    gs['cv_dw_w'] = dcw[:CV_KERNEL]
    dproj, gs['pl_w'], gs['pl_scale'] = pool_bwd(r['proj'], p['pl_w'], p['pl_scale'], dbcat, dproj)
    dlx, dly, dlcw, gs['lru_conv_b'], dwa, gs['lru_ba'], dwx, gs['lru_bx'], gs['sp8'] = lru_bwd(
        r['proj'], p['lru_cw8'], p['lru_conv_b'], p['wa_bd'], p['lru_ba'], p['wx_bd'], p['lru_bx'], p['sp8'], r['hst'], dbcat)
    gs['lru_conv_w'] = dlcw[:LRU_CONV]
    gs['lru_wa'], gs['lru_wx'] = _block_diag_t(dwa), _block_diag_t(dwx)
    gs['lru_ba'], gs['lru_bx'] = gs['lru_ba'].reshape(8, 64), gs['lru_bx'].reshape(8, 64)
    for off, piece in ((OFF_CV, dca), (OFF_CV + BRANCH_W, dcg), (OFF_LX, dlx), (OFF_LY, dly)):
        dproj = lax.dynamic_update_slice(dproj, piece, (0, off))
    gb['w_in'] = mm_tn("mm_gwin", dproj, r['h1'], tm=2176, tk=1024).reshape(N_DEV, -1, D_MODEL)
    dh1 = mm_nn("mm_din", dproj, g['w_in'], out_dtype=BF16, tk=4352, after=finish(gb, dx1))[0]
    dx0, gs['norm_mix_w'] = rms_bwd("rmsb_mix", r['x0'], p['norm_mix_w'], dh1, dx1)
    return dx0, gs


def kernel(x, mem, norm_mix_w, w_in, hg_lb_raw, hg_norm_w, cv_dw_w, cv_dw_b, cv_ln_w, cv_ln_b, pl_w, pl_scale, lru_conv_w, lru_conv_b, lru_wa, lru_ba, lru_wx, lru_bx, lru_lambda, gate_b, w_branch, w_out, norm_mem_w, mem_norm_w, xa_wq, xa_wkv, xa_wo, norm_ffn_w, ffn_w1, ffn_w2, final_norm_w, loss_target, m_norm_mix_w, m_w_in, m_hg_lb_raw, m_hg_norm_w, m_cv_dw_w, m_cv_dw_b, m_cv_ln_w, m_cv_ln_b, m_pl_w, m_pl_scale, m_lru_conv_w, m_lru_conv_b, m_lru_wa, m_lru_ba, m_lru_wx, m_lru_bx, m_lru_lambda, m_gate_b, m_w_branch, m_w_out, m_norm_mem_w, m_mem_norm_w, m_xa_wq, m_xa_wkv, m_xa_wo, m_norm_ffn_w, m_ffn_w1, m_ffn_w2, m_final_norm_w, v_norm_mix_w, v_w_in, v_hg_lb_raw, v_hg_norm_w, v_cv_dw_w, v_cv_dw_b, v_cv_ln_w, v_cv_ln_b, v_pl_w, v_pl_scale, v_lru_conv_w, v_lru_conv_b, v_lru_wa, v_lru_ba, v_lru_wx, v_lru_bx, v_lru_lambda, v_gate_b, v_w_branch, v_w_out, v_norm_mem_w, v_mem_norm_w, v_xa_wq, v_xa_wkv, v_xa_wo, v_norm_ffn_w, v_ffn_w1, v_ffn_w2, v_final_norm_w):
    W = dict(zip(WEIGHTS, (norm_mix_w, w_in, hg_lb_raw, hg_norm_w, cv_dw_w, cv_dw_b, cv_ln_w, cv_ln_b, pl_w, pl_scale, lru_conv_w, lru_conv_b, lru_wa, lru_ba, lru_wx, lru_bx, lru_lambda, gate_b, w_branch, w_out, norm_mem_w, mem_norm_w, xa_wq, xa_wkv, xa_wo, norm_ffn_w, ffn_w1, ffn_w2, final_norm_w)))
    Mo = dict(zip(WEIGHTS, (m_norm_mix_w, m_w_in, m_hg_lb_raw, m_hg_norm_w, m_cv_dw_w, m_cv_dw_b, m_cv_ln_w, m_cv_ln_b, m_pl_w, m_pl_scale, m_lru_conv_w, m_lru_conv_b, m_lru_wa, m_lru_ba, m_lru_wx, m_lru_bx, m_lru_lambda, m_gate_b, m_w_branch, m_w_out, m_norm_mem_w, m_mem_norm_w, m_xa_wq, m_xa_wkv, m_xa_wo, m_norm_ffn_w, m_ffn_w1, m_ffn_w2, m_final_norm_w)))
    Vo = dict(zip(WEIGHTS, (v_norm_mix_w, v_w_in, v_hg_lb_raw, v_hg_norm_w, v_cv_dw_w, v_cv_dw_b, v_cv_ln_w, v_cv_ln_b, v_pl_w, v_pl_scale, v_lru_conv_w, v_lru_conv_b, v_lru_wa, v_lru_ba, v_lru_wx, v_lru_bx, v_lru_lambda, v_gate_b, v_w_branch, v_w_out, v_norm_mem_w, v_mem_norm_w, v_xa_wq, v_xa_wkv, v_xa_wo, v_norm_ffn_w, v_ffn_w1, v_ffn_w2, v_final_norm_w)))
    me = _slot(*_place())
    xs, mems, target = x[0], mem[0], loss_target[0]

    shard_shapes = [W[n].shape for n in SMALL_SHARDED]
    gathered = all_gather("ag_small", [_pack([W[n] for n in SMALL_SHARDED])])[0]
    parts = [jnp.stack(ps) for ps in zip(*[_unpack(gathered[d], shard_shapes) for d in range(N_DEV)])]
    full_small = {n: _gather_last(parts[i], shard_shapes[i]) for i, n in enumerate(SMALL_SHARDED)}
    lb_all, lb_vjp = jax.vjp(_lower_bounds, hg_lb_raw)
    sp8_all, sp8_vjp = jax.vjp(_decay_rates, lru_lambda)

    def layer_params(l):
        p = {n: W[n][l] for n in SMALL if n != 'final_norm_w'}
        p['lb'] = lb_all[l]
        p['sp8'] = sp8_all[l]
        p['cv_w32'] = jnp.pad(full_small['cv_dw_w'][l], ((0, 32 - CV_KERNEL), (0, 0)))
        p['lru_cw8'] = jnp.pad(full_small['lru_conv_w'][l], ((0, 8 - LRU_CONV), (0, 0)))
        p['gate_b'] = full_small['gate_b'][l]
        p['wa_bd'], p['wx_bd'] = _block_diag(lru_wa[l]), _block_diag(lru_wx[l])
        p['lru_ba'], p['lru_bx'] = lru_ba[l].reshape(-1), lru_bx[l].reshape(-1)
        return p

    def shards_of(l):
        first = [jnp.transpose(w_in[l]).astype(BF16)]
        others = [w[l].astype(BF16) for w in (w_branch, w_out, xa_wq, xa_wkv, xa_wo, ffn_w1, ffn_w2)]
        return first, others

    def start_gather(l, before):
        first, others = shards_of(l)
        state_a, tok_a = gather_start(f"ag_start{l}a", first, me, before)
        state_b, tok_b = gather_start(f"ag_start{l}b", others, me, (*before, tok_a))
        return state_a, state_b, (tok_a, tok_b)

    def first_of(o):
        return dict(w_in=o[0].reshape(IN_W, D_MODEL))

    def others_of(o):
        wb = jnp.transpose(o[0], (1, 2, 0, 3)).reshape(N_BRANCH, BRANCH_W, D_MODEL)
        return dict(w_branch=wb, w_out=o[1].reshape(D_MODEL, D_MODEL),
                    xa_wq=o[2].reshape(D_MODEL, D_MODEL), xa_wkv=_natural(o[3]), xa_wo=o[4].reshape(D_MODEL, D_MODEL),
                    ffn_w1=_natural(o[5]), ffn_w2=o[6].reshape(D_FF, D_MODEL))

    params = [layer_params(l) for l in range(DEPTH)]
    mats, residuals = [], []
    xc = xs
    first, others = shards_of(0)
    whole = all_gather("ag_layer0", first)
    state_b, started = gather_start("ag_start0b", others, me, (whole[0],))
    gathers = {}
    for l in range(DEPTH):
        if l == 0:
            g_first = first_of(whole)
        else:
            state_a, state_b, _ = gathers.pop(l)
            g_first = first_of(gather_wait(f"ag_wait{l}a", state_a, (xc,)))

        def rest(mixed, l=l):
            more = others_of(gather_wait(f"ag_wait{l}b", state_b, (mixed,)))
            if l + 1 == DEPTH:
                return more, ()
            gathers[l + 1] = start_gather(l + 1, (more['w_out'],))
            return more, gathers[l + 1][2]

        xc, res, g = _layer_fwd(xc, mems, params[l], g_first, rest, after=(started,) if l == 0 else ())
        mats.append(g)
        residuals.append(res)
    loss_part, dx, g_final = loss_head(xc, final_norm_w, target)
    loss = lax.psum(loss_part, ("x", "y", "c"))

    small_grads = [None] * DEPTH
    big_grads = [{} for _ in range(DEPTH)]
    pending = []

    def send(l, group, blocks, before):
        names = list(blocks)
        state, tok = exchange_start(f"rs_start{l}{group}", [blocks[n] for n in names], (before,))
        pending.append((l, group, names, state))
        return (tok,)

    def land(after):
        l, group, names, state = pending.pop(0)
        sent, landed = exchange_wait(f"rs_wait{l}{group}", state, after)
        for n, s, t in zip(names, sent, landed):
            own = lax.dynamic_index_in_dim(s, me, 0, keepdims=False).reshape(-1, s.shape[-1])
            big_grads[l][n] = sum_parts(t.reshape(N_PEER, -1, t.shape[-1]), own).reshape(t.shape[1:])

    for l in reversed(range(DEPTH)):
        dx, small_grads[l] = _layer_bwd(dx, mems, params[l], mats[l], residuals[l],
                                        lambda blocks, dx1, l=l: send(l, "a", blocks, dx1),
                                        lambda blocks, dx1, l=l: send(l, "b", blocks, dx1))
        while pending[0][0] > l:
            land(dx)

    def stacked(n):
        return jnp.stack([small_grads[l][n] for l in range(DEPTH)])

    part = {n: stacked(n) for n in SMALL if n not in ('final_norm_w', 'hg_lb_raw', 'lru_lambda')}
    part['final_norm_w'] = g_final
    part['hg_lb_raw'] = lb_vjp(stacked('lb'))[0]
    part['lru_lambda'] = sp8_vjp(stacked('sp8'))[0]
    for n in SMALL_SHARDED:
        part[n] = stacked(n)
    names = list(SMALL) + list(SMALL_SHARDED)
    full_shapes = [part[n].shape for n in names]
    packed = _pack([part[n] for n in names])
    while len(pending) > 1:
        land(dx)
    summed = [t for layer in big_grads for t in layer.values()]
    state, _ = exchange_start("rs_small_start", [packed.reshape(N_DEV, -1, LANE)], (packed, *summed))
    sent, landed = exchange_wait("rs_small_wait", state, packed)
    mine = sum_parts(landed[0], lax.dynamic_index_in_dim(sent[0], me, 0, keepdims=False))
    total = all_gather("ag_grads", [mine])[0].reshape(-1, LANE)
    while pending:
        land(total)

    G = {}
    G['w_in'] = jnp.stack([jnp.transpose(big_grads[l]['w_in']) for l in range(DEPTH)])
    for n in ('w_branch', 'w_out', 'xa_wq', 'xa_wkv', 'xa_wo', 'ffn_w1', 'ffn_w2'):
        G[n] = jnp.stack([big_grads[l][n] for l in range(DEPTH)])
    for n, t in zip(names, _unpack(total, full_shapes)):
        if n in SMALL_SHARDED:
            c = t.shape[-1] // N_DEV
            t = lax.dynamic_slice_in_dim(t, me * c, c, axis=t.ndim - 1)
        G[n] = t

    delta, new_m, new_v = {}, {}, {}
    for n in BIG:
        rc = W[n].shape[-2:]
        d, nm, nv = adamw(*[t.reshape(-1, *rc) for t in (W[n], G[n], Mo[n], Vo[n])])
        delta[n], new_m[n], new_v[n] = d.reshape(W[n].shape), nm.reshape(W[n].shape), nv.reshape(W[n].shape)
    shapes = [W[n].shape for n in names]
    d, nm, nv = adamw(*[_pack([src[n] for n in names])[None] for src in (W, G, Mo, Vo)])
    for n, a, b, c in zip(names, _unpack(d, shapes), _unpack(nm, shapes), _unpack(nv, shapes)):
        delta[n], new_m[n], new_v[n] = a, b, c
    return (loss, dx[None], *[G[n] for n in WEIGHTS], *[delta[n] for n in WEIGHTS],
            *[new_m[n] for n in WEIGHTS], *[new_v[n] for n in WEIGHTS])
```

```python
import functools
import math

import jax
import jax.numpy as jnp
from jax import lax
from jax.experimental import pallas as pl
from jax.experimental.pallas import tpu as pltpu

F32 = jnp.float32
BF16 = jnp.bfloat16
I32 = jnp.int32

N_DEV = 8
D_MODEL = 1024
DEPTH = 4
CHUNK = 64
EPS = 1e-6
HG_HEADS = 4
BRANCH_W = 512
CV_KERNEL = 31
POOL_WINDOWS = (2, 4, 8, 16)
LRU_CONV = 4
LRU_C = 8.0
XA_HEADS = 4
XA_HD = D_MODEL // XA_HEADS
D_FF = 4 * D_MODEL
IN_W = 8704
OFF_Q, OFF_F, OFF_V, OFF_G, OFF_CV, OFF_PL, OFF_LX, OFF_LY, OFF_GATE = 0, 512, 1024, 1536, 2048, 3072, 3584, 4096, 4608
LANE = 128
ADAM_LR, ADAM_B1, ADAM_B2, ADAM_EPS, ADAM_WD, ADAM_STEP = 0.001, 0.9, 0.999, 1e-08, 0.01, 10
VMEM_LIMIT = 56 * 1024 * 1024
MESH = pl.DeviceIdType.MESH
NEG = -1e30
ANY_SPACE = pl.BlockSpec(memory_space=pl.ANY)


def _cp(sem, **kw):
    return pltpu.CompilerParams(dimension_semantics=sem, vmem_limit_bytes=VMEM_LIMIT, **kw)


def _sigmoid(x):
    return 1.0 / (1.0 + jnp.exp(-x))


def _dsilu(x, s):
    return s * (1.0 + x * (1.0 - s))


def _dot(a, b, cdims, precision=None):
    return lax.dot_general(a, b, (cdims, ((), ())), preferred_element_type=F32, precision=precision)


NN = ((1,), (0,))
NT = ((1,), (1,))
TN = ((0,), (0,))


def _mm(name, a, b, *, grid, a_spec, b_spec, o_specs, out_shapes, acc_shape, cdims, epi=None, extras=(), extra_specs=(), after=()):
    nk = grid[2]
    n_e, n_o = len(extras), len(out_shapes)
    extras = (*extras, *after)
    extra_specs = (*extra_specs, *[ANY_SPACE] * len(after))

    def body(*refs):
        a_ref, b_ref = refs[0], refs[1]
        e_refs = refs[2:2 + n_e]
        o_refs = refs[2 + len(extras):2 + len(extras) + n_o]

        def finish(acc):
            vals = epi(acc, *[r[...] for r in e_refs]) if epi is not None else (acc,)
            for r, v in zip(o_refs, vals):
                r[...] = v.astype(r.dtype)

        part = _dot(a_ref[...].astype(BF16), b_ref[...].astype(BF16), cdims)
        if nk == 1:
            finish(part)
        else:
            acc_ref = refs[-1]
            k = pl.program_id(2)

            @pl.when(k == 0)
            def _():
                acc_ref[...] = part

            @pl.when(k > 0)
            def _():
                acc_ref[...] += part

            @pl.when(k == nk - 1)
            def _():
                finish(acc_ref[...])

    return pl.pallas_call(
        body, name=name, grid=grid,
        in_specs=[a_spec, b_spec, *extra_specs], out_specs=list(o_specs), out_shape=list(out_shapes),
        scratch_shapes=[] if nk == 1 else [pltpu.VMEM(acc_shape, F32)],
        compiler_params=_cp(("parallel", "parallel", "arbitrary")),
    )(a, b, *extras)


def _tile(n, pref):
    t = min(n, pref)
    while n % t:
        t //= 2
    return t


def mm_nt(name, a, b, out_dtype=F32, epi=None, extras=(), n_out=1, out_dtypes=None, tm=1024, tn=1024, tk=2048, after=()):
    M, K = a.shape
    N = b.shape[0]
    tm, tn, tk = _tile(M, tm), _tile(N, tn), _tile(K, tk)
    odt = out_dtypes or [out_dtype] * n_out
    o_spec = pl.BlockSpec((tm, tn), lambda i, j, k: (i, j))
    return _mm(name, a, b, grid=(M // tm, N // tn, K // tk),
               a_spec=pl.BlockSpec((tm, tk), lambda i, j, k: (i, k)),
               b_spec=pl.BlockSpec((tn, tk), lambda i, j, k: (j, k)),
               o_specs=[o_spec] * len(odt), out_shapes=[jax.ShapeDtypeStruct((M, N), d) for d in odt],
               acc_shape=(tm, tn), cdims=NT, epi=epi, extras=extras, extra_specs=[o_spec] * len(extras), after=after)


def mm_nn(name, a, b, out_dtype=F32, epi=None, extras=(), n_out=1, out_dtypes=None, tm=1024, tn=1024, tk=2048, after=()):
    M, K = a.shape
    N = b.shape[1]
    tm, tn, tk = _tile(M, tm), _tile(N, tn), _tile(K, tk)
    odt = out_dtypes or [out_dtype] * n_out
    o_spec = pl.BlockSpec((tm, tn), lambda i, j, k: (i, j))
    return _mm(name, a, b, grid=(M // tm, N // tn, K // tk),
               a_spec=pl.BlockSpec((tm, tk), lambda i, j, k: (i, k)),
               b_spec=pl.BlockSpec((tk, tn), lambda i, j, k: (k, j)),
               o_specs=[o_spec] * len(odt), out_shapes=[jax.ShapeDtypeStruct((M, N), d) for d in odt],
               acc_shape=(tm, tn), cdims=NN, epi=epi, extras=extras, extra_specs=[o_spec] * len(extras), after=after)


def mm_tn(name, a, b, out_dtype=BF16, tm=1024, tn=1024, tk=2048):
    K, M = a.shape
    N = b.shape[1]
    tm, tn, tk = _tile(M, tm), _tile(N, tn), _tile(K, tk)
    return _mm(name, a, b, grid=(M // tm, N // tn, K // tk),
               a_spec=pl.BlockSpec((tk, tm), lambda i, j, k: (k, i)),
               b_spec=pl.BlockSpec((tk, tn), lambda i, j, k: (k, j)),
               o_specs=[pl.BlockSpec((tm, tn), lambda i, j, k: (i, j))],
               out_shapes=[jax.ShapeDtypeStruct((M, N), out_dtype)], acc_shape=(tm, tn), cdims=TN)[0]


def mm_branch_nt(name, a, b, out_dtype=BF16, tm=1024):
    M = a.shape[0]
    G, K, N = b.shape
    tm = _tile(M, tm)
    return _mm(name, a, b, grid=(M // tm, G, 1),
               a_spec=pl.BlockSpec((tm, N), lambda i, g, k: (i, g)),
               b_spec=pl.BlockSpec((None, K, N), lambda i, g, k: (g, 0, 0)),
               o_specs=[pl.BlockSpec((tm, K), lambda i, g, k: (i, g))],
               out_shapes=[jax.ShapeDtypeStruct((M, G * K), out_dtype)], acc_shape=(tm, K), cdims=NT)[0]


def mm_branch_tn(name, a, b, groups, out_dtype=BF16, tk=2048):
    T = a.shape[0]
    K, N = a.shape[1] // groups, b.shape[1] // groups
    tk = _tile(T, tk)
    return _mm(name, a, b, grid=(groups, 1, T // tk),
               a_spec=pl.BlockSpec((tk, K), lambda g, j, k: (k, g)),
               b_spec=pl.BlockSpec((tk, N), lambda g, j, k: (k, g)),
               o_specs=[pl.BlockSpec((None, K, N), lambda g, j, k: (g, 0, 0))],
               out_shapes=[jax.ShapeDtypeStruct((groups, K, N), out_dtype)], acc_shape=(K, N), cdims=TN)[0]


def mm_tn_cb(name, a, b, nb, out_dtype=BF16, tm=1024, tk=2048):
    K, M = a.shape
    N = b.shape[1]
    c = N // nb
    tm, tk = _tile(M, tm), _tile(K, tk)
    return _mm(name, a, b, grid=(M // tm, nb, K // tk),
               a_spec=pl.BlockSpec((tk, tm), lambda i, j, k: (k, i)),
               b_spec=pl.BlockSpec((tk, c), lambda i, j, k: (k, j)),
               o_specs=[pl.BlockSpec((None, tm, c), lambda i, j, k: (j, i, 0))],
               out_shapes=[jax.ShapeDtypeStruct((nb, M, c), out_dtype)], acc_shape=(tm, c), cdims=TN)[0]


def rms_fwd(name, x, w, out_dtype=BF16, tm=512):
    S, D = x.shape
    tm = _tile(S, tm)

    def body(x_ref, w_ref, o_ref):
        xv = x_ref[...]
        r = lax.rsqrt(jnp.mean(xv * xv, axis=-1, keepdims=True) + EPS)
        o_ref[...] = (xv * r * w_ref[...]).astype(o_ref.dtype)

    return pl.pallas_call(
        body, name=name, grid=(S // tm,),
        in_specs=[pl.BlockSpec((tm, D), lambda i: (i, 0)), pl.BlockSpec((1, D), lambda i: (0, 0))],
        out_specs=pl.BlockSpec((tm, D), lambda i: (i, 0)), out_shape=jax.ShapeDtypeStruct((S, D), out_dtype),
        compiler_params=_cp(("parallel",)),
    )(x, w.reshape(1, D))


def rms_bwd(name, x, w, dh, dres=None, tm=512):
    S, D = x.shape
    tm = _tile(S, tm)
    has_res = dres is not None

    def body(*refs):
        if has_res:
            x_ref, w_ref, dh_ref, dres_ref, dx_ref, dw_ref = refs
        else:
            x_ref, w_ref, dh_ref, dx_ref, dw_ref = refs
        xv = x_ref[...]
        dhv = dh_ref[...].astype(F32)
        r = lax.rsqrt(jnp.mean(xv * xv, axis=-1, keepdims=True) + EPS)
        g = dhv * w_ref[...]
        dx = r * g - xv * (r * r * r) * jnp.mean(xv * g, axis=-1, keepdims=True)
        if has_res:
            dx = dx + dres_ref[...]
        dx_ref[...] = dx

        @pl.when(pl.program_id(0) == 0)
        def _():
            dw_ref[...] = jnp.zeros_like(dw_ref)

        dw_ref[...] += jnp.sum(dhv * xv * r, axis=0, keepdims=True)

    row = pl.BlockSpec((tm, D), lambda i: (i, 0))
    vec = pl.BlockSpec((1, D), lambda i: (0, 0))
    args = [x, w.reshape(1, D), dh] + ([dres] if has_res else [])
    dx, dw = pl.pallas_call(
        body, name=name, grid=(S // tm,),
        in_specs=[row, vec, row] + ([row] if has_res else []),
        out_specs=[row, vec], out_shape=[jax.ShapeDtypeStruct((S, D), F32), jax.ShapeDtypeStruct((1, D), F32)],
        compiler_params=_cp(("arbitrary",)),
    )(*args)
    return dx, dw.reshape(D)


def loss_head(x, w, target, tm=512):
    S, D = x.shape
    tm = _tile(S, tm)

    def body(x_ref, w_ref, t_ref, loss_ref, dx_ref, dw_ref):
        xv = x_ref[...]
        wv = w_ref[...]
        r = lax.rsqrt(jnp.mean(xv * xv, axis=-1, keepdims=True) + EPS)
        y = xv * r * wv
        err = y - t_ref[...]
        dy = err * (1.0 / D)
        g = dy * wv
        dx_ref[...] = r * g - xv * (r * r * r) * jnp.mean(xv * g, axis=-1, keepdims=True)

        @pl.when(pl.program_id(0) == 0)
        def _():
            dw_ref[...] = jnp.zeros_like(dw_ref)
            loss_ref[...] = jnp.zeros_like(loss_ref)

        dw_ref[...] += jnp.sum(dy * xv * r, axis=0, keepdims=True)
        part = 0.5 * jnp.sum(jnp.mean(err * err, axis=-1, keepdims=True), axis=0, keepdims=True)
        loss_ref[...] += jnp.broadcast_to(part, loss_ref.shape)

    row = pl.BlockSpec((tm, D), lambda i: (i, 0))
    vec = pl.BlockSpec((1, D), lambda i: (0, 0))
    loss, dx, dw = pl.pallas_call(
        body, name="loss_head", grid=(S // tm,),
        in_specs=[row, vec, row],
        out_specs=[pl.BlockSpec((1, LANE), lambda i: (0, 0)), row, vec],
        out_shape=[jax.ShapeDtypeStruct((1, LANE), F32), jax.ShapeDtypeStruct((S, D), F32), jax.ShapeDtypeStruct((1, D), F32)],
        compiler_params=_cp(("arbitrary",)),
    )(x, w.reshape(1, D), target)
    return loss[0, 0], dx, dw.reshape(D)


SUB = 16
HG_W = HG_HEADS * LANE


def _hg_gates(q, f, lbv):
    sig = _sigmoid(f)
    fg = lbv + (1.0 - lbv) * sig
    sq = _sigmoid(q)
    return sig, fg, 1.0 - fg, sq, q * sq


def _hg_cumsum(logf):
    ri = lax.broadcasted_iota(I32, (CHUNK, CHUNK), 0)
    ci = lax.broadcasted_iota(I32, (CHUNK, CHUNK), 1)
    return _dot((ci <= ri).astype(F32), logf, NN, precision=lax.Precision.HIGHEST)


def _hg_rows():
    return lax.broadcasted_iota(I32, (CHUNK, LANE), 0)


def _hg_below(qf, kk, b, rows):
    blocks, parts = [jnp.zeros((SUB, CHUNK), F32)], []
    for i in range(1, CHUNK // SUB):
        bref = b[SUB * i - 1:SUB * i, :]
        rs = slice(SUB * i, SUB * (i + 1))
        eq = jnp.exp(b[rs] - bref)
        below = rows < SUB * i
        ek = jnp.exp(jnp.where(below, bref - b, NEG))
        qi = (qf[rs] * eq).astype(BF16)
        ki = (kk * ek).astype(BF16)
        blocks.append(_dot(qi, ki, NT))
        parts.append((qi, ki, eq, ek))
    return jnp.concatenate(blocks, axis=0), parts


def hgrn_fwd(proj, lb, nw):
    S = proj.shape[0]
    NC = S // CHUNK
    H = HG_HEADS

    def body(q_ref, f_ref, v_ref, g_ref, lb_ref, nw_ref, out_ref, st_out_ref, o_ref, st, kk_s, b_s):
        c = pl.program_id(0)

        @pl.when(c == 0)
        def _():
            st[...] = jnp.zeros_like(st)

        st_out_ref[...] = st[...]
        sig, fg, kk_all, sq, qf_all = _hg_gates(q_ref[...], f_ref[...], lb_ref[...])
        b_all = _hg_cumsum(jnp.log(fg))
        kk_s[...] = kk_all
        b_s[...] = b_all
        rows = _hg_rows()
        r16 = lax.broadcasted_iota(I32, (SUB, LANE), 0)
        for h in range(H):
            cs = slice(h * LANE, (h + 1) * LANE)
            qf, kk, b, v, g = qf_all[:, cs], kk_all[:, cs], b_all[:, cs], v_ref[:, cs], g_ref[:, cs]
            st_in = st[h]
            diag = []
            for i in range(CHUNK // SUB):
                rs = slice(SUB * i, SUB * (i + 1))
                acc = jnp.zeros((SUB, LANE), F32)
                for j in range(SUB):
                    row = pl.ds(SUB * i + j, 1)
                    e = jnp.exp(jnp.where(r16 >= j, b[rs] - b_s[row, cs], NEG))
                    col = jnp.sum(qf[rs] * (kk_s[row, cs] * e), axis=1, keepdims=True)
                    acc = acc + col * v_ref[row, cs]
                diag.append(acc)
            poff, _ = _hg_below(qf, kk, b, rows)
            vb = v.astype(BF16)
            bl = b[CHUNK - 1:CHUNK, :]
            o = (jnp.concatenate(diag, axis=0) + _dot(poff.astype(BF16), vb, NN)
                 + _dot((qf * jnp.exp(b)).astype(BF16), st_in.astype(BF16), NT))
            st[h] = st_in * jnp.exp(bl) + _dot(vb, (kk * jnp.exp(bl - b)).astype(BF16), TN)
            o_ref[:, cs] = o
            r = lax.rsqrt(jnp.mean(o * o, axis=-1, keepdims=True) + EPS)
            out_ref[:, cs] = (o * r * nw_ref[...] * (g * _sigmoid(g))).astype(out_ref.dtype)

    def seg(off):
        return pl.BlockSpec((CHUNK, HG_W), lambda c: (c, off // HG_W))

    blk = pl.BlockSpec((CHUNK, HG_W), lambda c: (c, 0))
    full = pltpu.VMEM((CHUNK, HG_W), F32)
    return pl.pallas_call(
        body, name="hgrn_fwd", grid=(NC,),
        in_specs=[seg(OFF_Q), seg(OFF_F), seg(OFF_V), seg(OFF_G),
                  pl.BlockSpec((1, HG_W), lambda c: (0, 0)), pl.BlockSpec((1, LANE), lambda c: (0, 0))],
        out_specs=[blk, pl.BlockSpec((None, H, LANE, LANE), lambda c: (c, 0, 0, 0)), blk],
        out_shape=[jax.ShapeDtypeStruct((S, 4 * HG_W), BF16), jax.ShapeDtypeStruct((NC, H, LANE, LANE), F32),
                   jax.ShapeDtypeStruct((S, HG_W), F32)],
        scratch_shapes=[pltpu.VMEM((H, LANE, LANE), F32), full, full],
        compiler_params=_cp(("arbitrary",)),
    )(proj, proj, proj, proj, lb.reshape(1, HG_W), nw.reshape(1, LANE))


def hgrn_bwd(proj, lb, nw, states, o_pre, dbcat, dproj):
    S = proj.shape[0]
    NC = S // CHUNK
    H = HG_HEADS

    def body(q_ref, f_ref, v_ref, g_ref, lb_ref, nw_ref, st_ref, o_ref, do_ref, _,
             dp_ref, dlb_ref, dnw_ref, dst, kk_s, b_s, do_s, db_s, dkk_s, dkk_d, dv_d):
        c = pl.program_id(0)
        dq_ref, df_ref, dv_ref, dg_ref = (dp_ref.at[:, pl.ds(off, HG_W)] for off in (OFF_Q, OFF_F, OFF_V, OFF_G))

        @pl.when(c == 0)
        def _():
            dst[...] = jnp.zeros_like(dst)
            dlb_ref[...] = jnp.zeros_like(dlb_ref)
            dnw_ref[...] = jnp.zeros_like(dnw_ref)

        q_all, g_all = q_ref[...], g_ref[...]
        lbv, nwv = lb_ref[...], nw_ref[...]
        sig, fg, kk_all, sq, qf_all = _hg_gates(q_all, f_ref[...], lbv)
        b_all = _hg_cumsum(jnp.log(fg))
        o_all = o_ref[...]
        dov = do_ref[...].astype(F32)
        sg = _sigmoid(g_all)
        gsg = g_all * sg
        dnw_acc = jnp.zeros((1, LANE), F32)
        for h in range(H):
            cs = slice(h * LANE, (h + 1) * LANE)
            o = o_all[:, cs]
            r = lax.rsqrt(jnp.mean(o * o, axis=-1, keepdims=True) + EPS)
            don = dov[:, cs] * gsg[:, cs]
            dnw_acc = dnw_acc + jnp.sum(don * o * r, axis=0, keepdims=True)
            gno = don * nwv
            do_s[:, cs] = r * gno - o * (r * r * r) * jnp.mean(o * gno, axis=-1, keepdims=True)
            dg_ref[:, cs] = (dov[:, cs] * (o * r * nwv) * _dsilu(g_all[:, cs], sg[:, cs])).astype(dg_ref.dtype)
        dnw_ref[...] += jnp.broadcast_to(dnw_acc, dnw_ref.shape)
        kk_s[...] = kk_all
        b_s[...] = b_all
        rows = _hg_rows()
        r16 = lax.broadcasted_iota(I32, (SUB, LANE), 0)
        for h in range(H):
            cs = slice(h * LANE, (h + 1) * LANE)
            qf, kk, b, v = qf_all[:, cs], kk_all[:, cs], b_all[:, cs], v_ref[:, cs]
            do = do_s[:, cs]
            st_in, dstv = st_ref[h], dst[h]
            bl = b[CHUNK - 1:CHUNK, :]
            eb, ebl, el = jnp.exp(b), jnp.exp(bl - b), jnp.exp(bl)
            qe, ke = qf * eb, kk * ebl
            vb, dob, stb, dstb = v.astype(BF16), do.astype(BF16), st_in.astype(BF16), dstv.astype(BF16)
            w_ = _dot(vb, dstb, NN)
            dqf = eb * _dot(dob, stb, NN)
            dkk = ebl * w_
            dv = _dot(ke.astype(BF16), dstb, NT)
            dbl = el * jnp.sum(st_in * dstv, axis=0, keepdims=True) + jnp.sum(ke * w_, axis=0, keepdims=True)
            dst[h] = dstv * el + _dot(dob, qe.astype(BF16), TN)
            poff, parts = _hg_below(qf, kk, b, rows)
            dpoff = _dot(dob, vb, NT).astype(BF16)
            dv = dv + _dot(poff.astype(BF16), dob, TN)
            dq_blocks = [jnp.zeros((SUB, LANE), F32)]
            for i, (qi, ki, eq, ek) in enumerate(parts, start=1):
                dpi = dpoff[SUB * i:SUB * (i + 1), :]
                dq_blocks.append(_dot(dpi, ki, NN) * eq)
                dkk = dkk + _dot(dpi, qi, TN) * ek
            dqf = dqf + jnp.concatenate(dq_blocks, axis=0)
            dq_diag = []
            for i in range(CHUNK // SUB):
                rs = slice(SUB * i, SUB * (i + 1))
                acc = jnp.zeros((SUB, LANE), F32)
                for j in range(SUB):
                    row = pl.ds(SUB * i + j, 1)
                    ks = kk_s[row, cs]
                    e = jnp.exp(jnp.where(r16 >= j, b[rs] - b_s[row, cs], NEG))
                    x = jnp.sum(do[rs] * v_ref[row, cs], axis=1, keepdims=True) * e
                    acc = acc + x * ks
                    dkk_d[row, cs] = jnp.sum(x * qf[rs], axis=0, keepdims=True)
                    col = jnp.sum(qf[rs] * (ks * e), axis=1, keepdims=True)
                    dv_d[row, cs] = jnp.sum(col * do[rs], axis=0, keepdims=True)
                dq_diag.append(acc)
            dqf = dqf + jnp.concatenate(dq_diag, axis=0)
            dkk = dkk + dkk_d[:, cs]
            dv = dv + dv_d[:, cs]
            dv_ref[:, cs] = dv.astype(dv_ref.dtype)
            db = qf * dqf - kk * dkk
            db_s[:, cs] = db + jnp.where(rows == CHUNK - 1, dbl, 0.0)
            dkk_s[:, cs] = dkk
            dq_ref[:, cs] = (dqf * _dsilu(q_all[:, cs], sq[:, cs])).astype(dq_ref.dtype)
        ri = lax.broadcasted_iota(I32, (CHUNK, CHUNK), 0)
        ci = lax.broadcasted_iota(I32, (CHUNK, CHUNK), 1)
        dlogf = _dot((ci >= ri).astype(F32), db_s[...], NN, precision=lax.Precision.HIGHEST)
        dfg = dlogf / fg - dkk_s[...]
        df_ref[...] = (dfg * (1.0 - lbv) * sig * (1.0 - sig)).astype(df_ref.dtype)
        dlb_ref[...] += jnp.broadcast_to(jnp.sum(dfg * (1.0 - sig), axis=0, keepdims=True), dlb_ref.shape)

    def seg(off):
        return pl.BlockSpec((CHUNK, HG_W), lambda c: (NC - 1 - c, off // HG_W))

    blk = pl.BlockSpec((CHUNK, HG_W), lambda c: (NC - 1 - c, 0))
    full = pltpu.VMEM((CHUNK, HG_W), F32)
    dproj, dlb, dnw = pl.pallas_call(
        body, name="hgrn_bwd", grid=(NC,),
        in_specs=[seg(OFF_Q), seg(OFF_F), seg(OFF_V), seg(OFF_G),
                  pl.BlockSpec((1, HG_W), lambda c: (0, 0)), pl.BlockSpec((1, LANE), lambda c: (0, 0)),
                  pl.BlockSpec((None, H, LANE, LANE), lambda c: (NC - 1 - c, 0, 0, 0)), blk, blk, ANY_SPACE],
        out_specs=[pl.BlockSpec((CHUNK, 4 * HG_W), lambda c: (NC - 1 - c, 0)),
                   pl.BlockSpec((8, HG_W), lambda c: (0, 0)), pl.BlockSpec((8, LANE), lambda c: (0, 0))],
        out_shape=[jax.ShapeDtypeStruct(dproj.shape, dproj.dtype), jax.ShapeDtypeStruct((8, HG_W), F32),
                   jax.ShapeDtypeStruct((8, LANE), F32)],
        input_output_aliases={9: 0},
        scratch_shapes=[pltpu.VMEM((H, LANE, LANE), F32)] + [full] * 7,
        compiler_params=_cp(("arbitrary",)),
    )(proj, proj, proj, proj, lb.reshape(1, HG_W), nw.reshape(1, LANE), states, o_pre, dbcat, dproj)
    return dproj, dlb[0], dnw[0]


CV_PAD = 32
ROWS = 256


def _colblk(S, off):
    return pl.BlockSpec((S, LANE), lambda j: (0, off // LANE + j))


def cv_fwd(proj, w32, bias):
    S = proj.shape[0]
    nchunk = S // ROWS

    def body(a_ref, g_ref, w_ref, b_ref, o_ref, zpad):
        zpad[pl.ds(0, CV_PAD), :] = jnp.zeros((CV_PAD, LANE), F32)

        def glu(c, _):
            r0 = pl.multiple_of(c * ROWS, ROWS)
            zpad[pl.ds(CV_PAD + r0, ROWS), :] = a_ref[pl.ds(r0, ROWS), :] * _sigmoid(g_ref[pl.ds(r0, ROWS), :])
            return 0

        lax.fori_loop(0, nchunk, glu, 0)

        def conv(c, _):
            r0 = pl.multiple_of(c * ROWS, ROWS)
            acc = jnp.broadcast_to(b_ref[...], (ROWS, LANE))
            for j in range(CV_KERNEL):
                acc = acc + w_ref[pl.ds(j, 1), :] * zpad[pl.ds(r0 + (CV_PAD - CV_KERNEL + 1) + j, ROWS), :]
            o_ref[pl.ds(r0, ROWS), :] = acc
            return 0

        lax.fori_loop(0, nchunk, conv, 0)

    return pl.pallas_call(
        body, name="cv_fwd", grid=(BRANCH_W // LANE,),
        in_specs=[_colblk(S, OFF_CV), _colblk(S, OFF_CV + BRANCH_W),
                  pl.BlockSpec((32, LANE), lambda j: (0, j)), pl.BlockSpec((1, LANE), lambda j: (0, j))],
        out_specs=pl.BlockSpec((S, LANE), lambda j: (0, j)), out_shape=jax.ShapeDtypeStruct((S, BRANCH_W), F32),
        scratch_shapes=[pltpu.VMEM((CV_PAD + S, LANE), F32)],
        compiler_params=_cp(("parallel",)),
    )(proj, proj, w32, bias.reshape(1, BRANCH_W))


def cv_bwd(proj, w32, dzc):
    S = proj.shape[0]
    nchunk = S // ROWS

    def body(a_ref, g_ref, w_ref, dz_ref, da_ref, dg_ref, dw_ref, db_ref, zpad, dpad):
        zpad[pl.ds(0, CV_PAD), :] = jnp.zeros((CV_PAD, LANE), F32)
        dpad[pl.ds(S, CV_PAD), :] = jnp.zeros((CV_PAD, LANE), F32)
        dw_ref[...] = jnp.zeros_like(dw_ref)

        def glu(c, dsum):
            r0 = pl.multiple_of(c * ROWS, ROWS)
            zpad[pl.ds(CV_PAD + r0, ROWS), :] = a_ref[pl.ds(r0, ROWS), :] * _sigmoid(g_ref[pl.ds(r0, ROWS), :])
            d = dz_ref[pl.ds(r0, ROWS), :]
            dpad[pl.ds(r0, ROWS), :] = d
            return dsum + jnp.sum(d, axis=0, keepdims=True)

        dsum = lax.fori_loop(0, nchunk, glu, jnp.zeros((1, LANE), F32))
        db_ref[...] = jnp.broadcast_to(dsum, db_ref.shape)

        def conv(c, _):
            r0 = pl.multiple_of(c * ROWS, ROWS)
            d = dpad[pl.ds(r0, ROWS), :]
            acc = jnp.zeros((ROWS, LANE), F32)
            for j in range(CV_KERNEL):
                acc = acc + w_ref[pl.ds(j, 1), :] * dpad[pl.ds(r0 + (CV_KERNEL - 1) - j, ROWS), :]
                zs = zpad[pl.ds(r0 + (CV_PAD - CV_KERNEL + 1) + j, ROWS), :]
                dw_ref[pl.ds(j, 1), :] += jnp.sum(d * zs, axis=0, keepdims=True)
            a = a_ref[pl.ds(r0, ROWS), :]
            sg = _sigmoid(g_ref[pl.ds(r0, ROWS), :])
            da_ref[pl.ds(r0, ROWS), :] = (acc * sg).astype(da_ref.dtype)
            dg_ref[pl.ds(r0, ROWS), :] = (acc * a * sg * (1.0 - sg)).astype(dg_ref.dtype)
            return 0

        lax.fori_loop(0, nchunk, conv, 0)

    blk = pl.BlockSpec((S, LANE), lambda j: (0, j))
    da, dg, dw, db = pl.pallas_call(
        body, name="cv_bwd", grid=(BRANCH_W // LANE,),
        in_specs=[_colblk(S, OFF_CV), _colblk(S, OFF_CV + BRANCH_W), pl.BlockSpec((32, LANE), lambda j: (0, j)), blk],
        out_specs=[blk, blk, pl.BlockSpec((32, LANE), lambda j: (0, j)), pl.BlockSpec((8, LANE), lambda j: (0, j))],
        out_shape=[jax.ShapeDtypeStruct((S, BRANCH_W), BF16), jax.ShapeDtypeStruct((S, BRANCH_W), BF16),
                   jax.ShapeDtypeStruct((32, BRANCH_W), F32), jax.ShapeDtypeStruct((8, BRANCH_W), F32)],
        scratch_shapes=[pltpu.VMEM((CV_PAD + S, LANE), F32), pltpu.VMEM((S + CV_PAD, LANE), F32)],
        compiler_params=_cp(("parallel",)),
    )(proj, proj, w32, dzc)
    return da, dg, dw, db[0]


def ln_silu_fwd(z, w, b, bcat, tm=512):
    S, C = z.shape
    tm = _tile(S, tm)

    def body(z_ref, w_ref, b_ref, _, o_ref):
        zv = z_ref[...]
        mu = jnp.mean(zv, axis=-1, keepdims=True)
        zc = zv - mu
        rstd = lax.rsqrt(jnp.mean(zc * zc, axis=-1, keepdims=True) + EPS)
        y = zc * rstd * w_ref[...] + b_ref[...]
        o_ref[...] = (y * _sigmoid(y)).astype(o_ref.dtype)

    row = pl.BlockSpec((tm, C), lambda i: (i, 0))
    vec = pl.BlockSpec((1, C), lambda i: (0, 0))
    return pl.pallas_call(
        body, name="ln_silu_fwd", grid=(S // tm,), in_specs=[row, vec, vec, ANY_SPACE],
        out_specs=pl.BlockSpec((tm, C), lambda i: (i, 1)), out_shape=jax.ShapeDtypeStruct(bcat.shape, bcat.dtype),
        input_output_aliases={3: 0}, compiler_params=_cp(("parallel",)),
    )(z, w.reshape(1, C), b.reshape(1, C), bcat)


def ln_silu_bwd(z, w, b, dbcat, tm=512):
    S, C = z.shape
    tm = _tile(S, tm)

    def body(z_ref, w_ref, b_ref, do_ref, dz_ref, dw_ref, db_ref):
        zv = z_ref[...]
        wv = w_ref[...]
        mu = jnp.mean(zv, axis=-1, keepdims=True)
        zc = zv - mu
        rstd = lax.rsqrt(jnp.mean(zc * zc, axis=-1, keepdims=True) + EPS)
        xh = zc * rstd
        y = xh * wv + b_ref[...]
        dy = do_ref[...].astype(F32) * _dsilu(y, _sigmoid(y))

        @pl.when(pl.program_id(0) == 0)
        def _():
            dw_ref[...] = jnp.zeros_like(dw_ref)
            db_ref[...] = jnp.zeros_like(db_ref)

        dw_ref[...] += jnp.sum(dy * xh, axis=0, keepdims=True)
        db_ref[...] += jnp.sum(dy, axis=0, keepdims=True)
        dxh = dy * wv
        dz_ref[...] = rstd * (dxh - jnp.mean(dxh, axis=-1, keepdims=True) - xh * jnp.mean(dxh * xh, axis=-1, keepdims=True))

    row = pl.BlockSpec((tm, C), lambda i: (i, 0))
    vec = pl.BlockSpec((1, C), lambda i: (0, 0))
    dz, dw, db = pl.pallas_call(
        body, name="ln_silu_bwd", grid=(S // tm,), in_specs=[row, vec, vec, pl.BlockSpec((tm, C), lambda i: (i, 1))],
        out_specs=[row, vec, vec],
        out_shape=[jax.ShapeDtypeStruct((S, C), F32), jax.ShapeDtypeStruct((1, C), F32), jax.ShapeDtypeStruct((1, C), F32)],
        compiler_params=_cp(("arbitrary",)),
    )(z, w.reshape(1, C), b.reshape(1, C), dbcat)
    return dz, dw.reshape(C), db.reshape(C)


PL_PAD = 16


def _pool_counts(r0, win):
    t = r0 + lax.broadcasted_iota(I32, (ROWS, LANE), 0)
    return jnp.minimum(t + 1, win).astype(F32)


def pool_fwd(proj, wg, scale, bcat):
    S = proj.shape[0]
    nchunk = S // ROWS

    def body(u_ref, w_ref, s_ref, _, o_ref, upad):
        g = pl.program_id(0)
        upad[pl.ds(0, PL_PAD), :] = jnp.zeros((PL_PAD, LANE), F32)

        def fill(c, _):
            r0 = pl.multiple_of(c * ROWS, ROWS)
            upad[pl.ds(PL_PAD + r0, ROWS), :] = u_ref[pl.ds(r0, ROWS), :]
            return 0

        lax.fori_loop(0, nchunk, fill, 0)
        wb = w_ref[...].astype(BF16)
        for gi, win in enumerate(POOL_WINDOWS):
            @pl.when(g == gi)
            def _(win=win):
                def chunk(c, _):
                    r0 = pl.multiple_of(c * ROWS, ROWS)
                    u = upad[pl.ds(PL_PAD + r0, ROWS), :]
                    ws = u
                    for j in range(1, win):
                        ws = ws + upad[pl.ds(PL_PAD + r0 - j, ROWS), :]
                    pooled = ws / _pool_counts(r0, win) - u
                    o_ref[pl.ds(r0, ROWS), :] = (_dot(pooled.astype(BF16), wb, NN) * s_ref[...]).astype(o_ref.dtype)
                    return 0

                lax.fori_loop(0, nchunk, chunk, 0)

    return pl.pallas_call(
        body, name="pool_fwd", grid=(len(POOL_WINDOWS),),
        in_specs=[_colblk(S, OFF_PL), pl.BlockSpec((None, LANE, LANE), lambda j: (j, 0, 0)), pl.BlockSpec((1, LANE), lambda j: (0, j)),
                  ANY_SPACE],
        out_specs=pl.BlockSpec((S, LANE), lambda j: (0, 2 * BRANCH_W // LANE + j)),
        out_shape=jax.ShapeDtypeStruct(bcat.shape, bcat.dtype), input_output_aliases={3: 0},
        scratch_shapes=[pltpu.VMEM((PL_PAD + S, LANE), F32)],
        compiler_params=_cp(("parallel",)),
    )(proj, wg, scale.reshape(1, BRANCH_W), bcat)


def pool_bwd(proj, wg, scale, dbcat, dproj):
    S = proj.shape[0]
    nchunk = S // ROWS

    def body(u_ref, w_ref, s_ref, dy_ref, _, du_ref, dw_ref, ds_ref, upad, dpn, nd):
        g = pl.program_id(0)
        upad[pl.ds(0, PL_PAD), :] = jnp.zeros((PL_PAD, LANE), F32)
        dpn[pl.ds(S, PL_PAD), :] = jnp.zeros((PL_PAD, LANE), F32)

        def fill(c, _):
            r0 = pl.multiple_of(c * ROWS, ROWS)
            upad[pl.ds(PL_PAD + r0, ROWS), :] = u_ref[pl.ds(r0, ROWS), :]
            return 0

        lax.fori_loop(0, nchunk, fill, 0)
        wb = w_ref[...].astype(BF16)
        sv = s_ref[...]
        for gi, win in enumerate(POOL_WINDOWS):
            @pl.when(g == gi)
            def _(win=win):
                def chunk(c, carry):
                    dw, dsc = carry
                    r0 = pl.multiple_of(c * ROWS, ROWS)
                    u = upad[pl.ds(PL_PAD + r0, ROWS), :]
                    ws = u
                    for j in range(1, win):
                        ws = ws + upad[pl.ds(PL_PAD + r0 - j, ROWS), :]
                    cnt = _pool_counts(r0, win)
                    pooled = (ws / cnt - u).astype(BF16)
                    dyv = dy_ref[pl.ds(r0, ROWS), :].astype(F32)
                    dsc = dsc + jnp.sum(dyv * _dot(pooled, wb, NN), axis=0, keepdims=True)
                    dys = (dyv * sv).astype(BF16)
                    dw = dw + _dot(pooled, dys, TN)
                    dp = _dot(dys, wb, NT)
                    dpn[pl.ds(r0, ROWS), :] = dp / cnt
                    nd[pl.ds(r0, ROWS), :] = -dp
                    return dw, dsc

                dw, dsc = lax.fori_loop(0, nchunk, chunk, (jnp.zeros((LANE, LANE), F32), jnp.zeros((1, LANE), F32)))
                dw_ref[...] = dw
                ds_ref[...] = jnp.broadcast_to(dsc, ds_ref.shape)

                def spread(c, _):
                    r0 = pl.multiple_of(c * ROWS, ROWS)
                    acc = nd[pl.ds(r0, ROWS), :]
                    for j in range(win):
                        acc = acc + dpn[pl.ds(r0 + j, ROWS), :]
                    du_ref[pl.ds(r0, ROWS), :] = acc.astype(du_ref.dtype)
                    return 0

                lax.fori_loop(0, nchunk, spread, 0)

    dproj, dw, ds = pl.pallas_call(
        body, name="pool_bwd", grid=(len(POOL_WINDOWS),),
        in_specs=[_colblk(S, OFF_PL), pl.BlockSpec((None, LANE, LANE), lambda j: (j, 0, 0)), pl.BlockSpec((1, LANE), lambda j: (0, j)),
                  pl.BlockSpec((S, LANE), lambda j: (0, 2 * BRANCH_W // LANE + j)), ANY_SPACE],
        out_specs=[_colblk(S, OFF_PL), pl.BlockSpec((None, LANE, LANE), lambda j: (j, 0, 0)), pl.BlockSpec((8, LANE), lambda j: (0, j))],
        out_shape=[jax.ShapeDtypeStruct(dproj.shape, dproj.dtype), jax.ShapeDtypeStruct((len(POOL_WINDOWS), LANE, LANE), F32),
                   jax.ShapeDtypeStruct((8, BRANCH_W), F32)],
        input_output_aliases={4: 0},
        scratch_shapes=[pltpu.VMEM((PL_PAD + S, LANE), F32), pltpu.VMEM((S + PL_PAD, LANE), F32), pltpu.VMEM((S, LANE), F32)],
        compiler_params=_cp(("parallel",)),
    )(proj, wg, scale.reshape(1, BRANCH_W), dbcat, dproj)
    return dproj, dw, ds[0]


LR_PAD = 8
SCAN_TILES = 4
GELU_C = math.sqrt(2.0 / math.pi)
GELU_A = 0.044715


def _gelu(y):
    return 0.5 * y * (1.0 + jnp.tanh(GELU_C * (y + GELU_A * y * y * y)))


def _dgelu(y):
    t = jnp.tanh(GELU_C * (y + GELU_A * y * y * y))
    return 0.5 * (1.0 + t) + 0.5 * y * (1.0 - t * t) * GELU_C * (1.0 + 3.0 * GELU_A * y * y)


def _lru_gates(xpad, r0, cw_ref, cb, wa, ba, wx, bx, sp8):
    xc = jnp.broadcast_to(cb, (ROWS, LANE))
    for j in range(LRU_CONV):
        xc = xc + cw_ref[pl.ds(j, 1), :] * xpad[pl.ds(r0 + (LR_PAD - LRU_CONV + 1) + j, ROWS), :]
    xb = xc.astype(BF16)
    r = _sigmoid(_dot(xb, wa, NN) + ba)
    ig = _sigmoid(_dot(xb, wx, NN) + bx)
    la = -sp8 * r
    a = jnp.exp(la)
    s = jnp.sqrt(-jnp.tanh(la) * (a * a + 1.0))
    return xc, r, ig, a, s


def _tile_scan(a, b, r8, up):
    for s in (1, 2, 4):
        keep = (r8 < 8 - s) if up else (r8 >= s)
        shift = 8 - s if up else s
        a_sh = jnp.where(keep, pltpu.roll(a, shift, 0), 1.0)
        b_sh = jnp.where(keep, pltpu.roll(b, shift, 0), 0.0)
        b = b + a * b_sh
        a = a * a_sh
    return a, b


def lru_fwd(proj, cw8, cb, wa_bd, ba, wx_bd, bx, sp8, bcat):
    S = proj.shape[0]
    nchunk = S // ROWS

    def body(x_ref, y_ref, cw_ref, cb_ref, wa_ref, ba_ref, wx_ref, bx_ref, sp_ref, _, o_ref, h_ref, xpad, a_s):
        xpad[pl.ds(0, LR_PAD), :] = jnp.zeros((LR_PAD, LANE), F32)

        def fill(c, _):
            r0 = pl.multiple_of(c * ROWS, ROWS)
            xpad[pl.ds(LR_PAD + r0, ROWS), :] = x_ref[pl.ds(r0, ROWS), :]
            return 0

        lax.fori_loop(0, nchunk, fill, 0)
        wa = wa_ref[...].astype(BF16)
        wx = wx_ref[...].astype(BF16)

        def gates(c, _):
            r0 = pl.multiple_of(c * ROWS, ROWS)
            xc, r, ig, a, s = _lru_gates(xpad, r0, cw_ref, cb_ref[...], wa, ba_ref[...], wx, bx_ref[...], sp_ref[...])
            a_s[pl.ds(r0, ROWS), :] = a
            h_ref[pl.ds(r0, ROWS), :] = s * (ig * xc)
            return 0

        lax.fori_loop(0, nchunk, gates, 0)

        r8 = lax.broadcasted_iota(I32, (8, LANE), 0)

        def scan(i, h):
            bases = [pl.multiple_of(i * (8 * SCAN_TILES) + 8 * j, 8) for j in range(SCAN_TILES)]
            maps = [_tile_scan(a_s[pl.ds(b, 8), :], h_ref[pl.ds(b, 8), :], r8, False) for b in bases]
            for b, (ca, cb_) in zip(bases, maps):
                out = cb_ + ca * h
                h_ref[pl.ds(b, 8), :] = out
                h = out[7:8, :]
            return h

        lax.fori_loop(0, S // (8 * SCAN_TILES), scan, jnp.zeros((1, LANE), F32))

        def gate_out(c, _):
            r0 = pl.multiple_of(c * ROWS, ROWS)
            o_ref[pl.ds(r0, ROWS), :] = (h_ref[pl.ds(r0, ROWS), :] * _gelu(y_ref[pl.ds(r0, ROWS), :])).astype(o_ref.dtype)
            return 0

        lax.fori_loop(0, nchunk, gate_out, 0)

    vec = pl.BlockSpec((1, LANE), lambda j: (0, j))
    mat = pl.BlockSpec((None, LANE, LANE), lambda j: (j, 0, 0))
    blk = pl.BlockSpec((S, LANE), lambda j: (0, j))
    return pl.pallas_call(
        body, name="lru_fwd", grid=(BRANCH_W // LANE,),
        in_specs=[_colblk(S, OFF_LX), _colblk(S, OFF_LY), pl.BlockSpec((8, LANE), lambda j: (0, j)), vec, mat, vec, mat, vec, vec,
                  ANY_SPACE],
        out_specs=[pl.BlockSpec((S, LANE), lambda j: (0, 3 * BRANCH_W // LANE + j)), blk],
        out_shape=[jax.ShapeDtypeStruct(bcat.shape, bcat.dtype), jax.ShapeDtypeStruct((S, BRANCH_W), F32)],
        input_output_aliases={9: 0},
        scratch_shapes=[pltpu.VMEM((LR_PAD + S, LANE), F32), pltpu.VMEM((S, LANE), F32)],
        compiler_params=_cp(("parallel",)),
    )(proj, proj, cw8, cb.reshape(1, -1), wa_bd, ba.reshape(1, -1), wx_bd, bx.reshape(1, -1), sp8.reshape(1, -1), bcat)


def lru_bwd(proj, cw8, cb, wa_bd, ba, wx_bd, bx, sp8, h, dbcat):
    S = proj.shape[0]
    nchunk = S // ROWS

    def body(x_ref, y_ref, cw_ref, cb_ref, wa_ref, ba_ref, wx_ref, bx_ref, sp_ref, h_ref, do_ref,
             dx_ref, dy_ref, dcw_ref, dcb_ref, dwa_ref, dba_ref, dwx_ref, dbx_ref, dsp_ref,
             xpad, a_s, g_s, hpad, dxc, xc_s, r_s, ig_s, s_s):
        xpad[pl.ds(0, LR_PAD), :] = jnp.zeros((LR_PAD, LANE), F32)
        hpad[pl.ds(0, LR_PAD), :] = jnp.zeros((LR_PAD, LANE), F32)
        dxc[pl.ds(S, LR_PAD), :] = jnp.zeros((LR_PAD, LANE), F32)
        dcw_ref[...] = jnp.zeros_like(dcw_ref)
        wa = wa_ref[...].astype(BF16)
        wx = wx_ref[...].astype(BF16)
        cbv, bav, bxv, spv = cb_ref[...], ba_ref[...], bx_ref[...], sp_ref[...]

        def fill(c, _):
            r0 = pl.multiple_of(c * ROWS, ROWS)
            xpad[pl.ds(LR_PAD + r0, ROWS), :] = x_ref[pl.ds(r0, ROWS), :]
            hv = h_ref[pl.ds(r0, ROWS), :]
            hpad[pl.ds(LR_PAD + r0, ROWS), :] = hv
            yv = y_ref[pl.ds(r0, ROWS), :]
            dov = do_ref[pl.ds(r0, ROWS), :].astype(F32)
            g_s[pl.ds(r0, ROWS), :] = dov * _gelu(yv)
            dy_ref[pl.ds(r0, ROWS), :] = (dov * hv * _dgelu(yv)).astype(dy_ref.dtype)
            return 0

        lax.fori_loop(0, nchunk, fill, 0)

        def gates(c, _):
            r0 = pl.multiple_of(c * ROWS, ROWS)
            rows = pl.ds(r0, ROWS)
            xc_s[rows, :], r_s[rows, :], ig_s[rows, :], a_s[rows, :], s_s[rows, :] = _lru_gates(
                xpad, r0, cw_ref, cbv, wa, bav, wx, bxv, spv)
            return 0

        lax.fori_loop(0, nchunk, gates, 0)

        r8 = lax.broadcasted_iota(I32, (8, LANE), 0)

        def rscan(i, carry):
            bases = [pl.multiple_of(S - 8 - i * (8 * SCAN_TILES) - 8 * j, 8) for j in range(SCAN_TILES)]
            firsts, maps = [], []
            for b in bases:
                a8 = a_s[pl.ds(b, 8), :]
                above = jnp.where(r8 < 7, pltpu.roll(a8, 7, 0), 1.0)
                firsts.append(a8[0:1, :])
                maps.append(_tile_scan(above, g_s[pl.ds(b, 8), :], r8, True))
            for b, a0, (ca, cb_) in zip(bases, firsts, maps):
                out = cb_ + ca * carry
                g_s[pl.ds(b, 8), :] = out
                carry = a0 * out[0:1, :]
            return carry

        lax.fori_loop(0, S // (8 * SCAN_TILES), rscan, jnp.zeros((1, LANE), F32))

        def chain(c, carry):
            dwa, dwx, dba, dbx, dsp, dcb = carry
            r0 = pl.multiple_of(c * ROWS, ROWS)
            rows = pl.ds(r0, ROWS)
            xc, r, ig, a, s = xc_s[rows, :], r_s[rows, :], ig_s[rows, :], a_s[rows, :], s_s[rows, :]
            gt = g_s[rows, :]
            hprev = hpad[pl.ds(r0 + LR_PAD - 1, ROWS), :]
            da = gt * hprev - gt * ig * xc * (a / s)
            dig = gt * s * xc
            dla = da * a
            dsp = dsp + jnp.sum(-dla * r, axis=0, keepdims=True)
            dpr = (-dla * spv) * r * (1.0 - r)
            dpi = dig * ig * (1.0 - ig)
            dprb, dpib, xb = dpr.astype(BF16), dpi.astype(BF16), xc.astype(BF16)
            d = gt * s * ig + _dot(dprb, wa, NT) + _dot(dpib, wx, NT)
            dwa = dwa + _dot(xb, dprb, TN)
            dwx = dwx + _dot(xb, dpib, TN)
            dba = dba + jnp.sum(dpr, axis=0, keepdims=True)
            dbx = dbx + jnp.sum(dpi, axis=0, keepdims=True)
            dcb = dcb + jnp.sum(d, axis=0, keepdims=True)
            dxc[pl.ds(r0, ROWS), :] = d
            for j in range(LRU_CONV):
                xs = xpad[pl.ds(r0 + (LR_PAD - LRU_CONV + 1) + j, ROWS), :]
                dcw_ref[pl.ds(j, 1), :] += jnp.sum(d * xs, axis=0, keepdims=True)
            return dwa, dwx, dba, dbx, dsp, dcb

        zm, zv = jnp.zeros((LANE, LANE), F32), jnp.zeros((1, LANE), F32)
        dwa, dwx, dba, dbx, dsp, dcb = lax.fori_loop(0, nchunk, chain, (zm, zm, zv, zv, zv, zv))
        dwa_ref[...] = dwa
        dwx_ref[...] = dwx
        dba_ref[...] = jnp.broadcast_to(dba, dba_ref.shape)
        dbx_ref[...] = jnp.broadcast_to(dbx, dbx_ref.shape)
        dsp_ref[...] = jnp.broadcast_to(dsp, dsp_ref.shape)
        dcb_ref[...] = jnp.broadcast_to(dcb, dcb_ref.shape)

        def convt(c, _):
            r0 = pl.multiple_of(c * ROWS, ROWS)
            acc = jnp.zeros((ROWS, LANE), F32)
            for j in range(LRU_CONV):
                acc = acc + cw_ref[pl.ds(j, 1), :] * dxc[pl.ds(r0 + (LRU_CONV - 1) - j, ROWS), :]
            dx_ref[pl.ds(r0, ROWS), :] = acc.astype(dx_ref.dtype)
            return 0

        lax.fori_loop(0, nchunk, convt, 0)

    vec = pl.BlockSpec((1, LANE), lambda j: (0, j))
    vec8 = pl.BlockSpec((8, LANE), lambda j: (0, j))
    mat = pl.BlockSpec((None, LANE, LANE), lambda j: (j, 0, 0))
    blk = pl.BlockSpec((S, LANE), lambda j: (0, j))
    nblk = BRANCH_W // LANE
    v8 = jax.ShapeDtypeStruct((8, BRANCH_W), F32)
    m4 = jax.ShapeDtypeStruct((nblk, LANE, LANE), F32)
    big = jax.ShapeDtypeStruct((S, BRANCH_W), BF16)
    seq = pltpu.VMEM((S, LANE), F32)
    dx, dy, dcw, dcb, dwa, dba, dwx, dbx, dsp = pl.pallas_call(
        body, name="lru_bwd", grid=(nblk,),
        in_specs=[_colblk(S, OFF_LX), _colblk(S, OFF_LY), vec8, vec, mat, vec, mat, vec, vec, blk,
                  pl.BlockSpec((S, LANE), lambda j: (0, 3 * BRANCH_W // LANE + j))],
        out_specs=[blk, blk, vec8, vec8, mat, vec8, mat, vec8, vec8],
        out_shape=[big, big, v8, v8, m4, v8, m4, v8, v8],
        scratch_shapes=[pltpu.VMEM((LR_PAD + S, LANE), F32), seq, seq, pltpu.VMEM((LR_PAD + S, LANE), F32),
                        pltpu.VMEM((S + LR_PAD, LANE), F32), seq, seq, seq, seq],
        compiler_params=_cp(("parallel",)),
    )(proj, proj, cw8, cb.reshape(1, -1), wa_bd, ba.reshape(1, -1), wx_bd, bx.reshape(1, -1), sp8.reshape(1, -1), h, dbcat)
    return dx, dy, dcw, dcb[0], dwa, dba[0], dwx, dbx[0], dsp[0]


MG_COLS = 512
N_BRANCH = 4
BCAT_W = N_BRANCH * BRANCH_W


def merge_fwd(bcat, wb, proj, gate_b, tm=2048, after=()):
    S = proj.shape[0]
    tm = _tile(S, tm)
    halves = D_MODEL // MG_COLS

    def body(a_ref, w_ref, g_ref, gb_ref, *rest):
        up_ref, o_ref, acc_ref = rest[len(after):]
        k = pl.program_id(2)
        up = _dot(a_ref[...], w_ref[...], NN)
        up_ref[...] = up.astype(up_ref.dtype)
        term = _sigmoid(g_ref[...] + gb_ref[pl.ds(k, 1), :]) * up

        @pl.when(k == 0)
        def _():
            acc_ref[...] = term

        @pl.when(k > 0)
        def _():
            acc_ref[...] += term

        @pl.when(k == N_BRANCH - 1)
        def _():
            o_ref[...] = acc_ref[...].astype(o_ref.dtype)

    return pl.pallas_call(
        body, name="merge_fwd", grid=(S // tm, halves, N_BRANCH),
        in_specs=[pl.BlockSpec((tm, BRANCH_W), lambda i, j, k: (i, k)),
                  pl.BlockSpec((None, BRANCH_W, MG_COLS), lambda i, j, k: (k, 0, j)),
                  pl.BlockSpec((tm, MG_COLS), lambda i, j, k: (i, OFF_GATE // MG_COLS + k * halves + j)),
                  pl.BlockSpec((N_BRANCH, MG_COLS), lambda i, j, k: (0, j))] + [ANY_SPACE] * len(after),
        out_specs=[pl.BlockSpec((tm, MG_COLS), lambda i, j, k: (i, k * halves + j)),
                   pl.BlockSpec((tm, MG_COLS), lambda i, j, k: (i, j))],
        out_shape=[jax.ShapeDtypeStruct((S, N_BRANCH * D_MODEL), BF16), jax.ShapeDtypeStruct((S, D_MODEL), BF16)],
        scratch_shapes=[pltpu.VMEM((tm, MG_COLS), F32)],
        compiler_params=_cp(("parallel", "parallel", "arbitrary")),
    )(bcat, wb, proj, gate_b, *after)


def merge_bwd(dmerged, ups, proj, gate_b, tm=1024):
    S = proj.shape[0]
    tm = _tile(S, tm)
    halves = D_MODEL // MG_COLS

    def body(dm_ref, u_ref, g_ref, gb_ref, du_ref, dg_ref, dgb_ref):
        k = pl.program_id(0)

        @pl.when(pl.program_id(2) == 0)
        def _():
            dgb_ref[...] = jnp.zeros_like(dgb_ref)

        dm = dm_ref[...].astype(F32)
        sg = _sigmoid(g_ref[...] + gb_ref[pl.ds(k, 1), :])
        du_ref[...] = (dm * sg).astype(du_ref.dtype)
        dgk = dm * u_ref[...] * sg * (1.0 - sg)
        dg_ref[...] = dgk.astype(dg_ref.dtype)
        dgb_ref[...] += jnp.broadcast_to(jnp.sum(dgk, axis=0, keepdims=True), dgb_ref.shape)

    dups, dproj, dgb = pl.pallas_call(
        body, name="merge_bwd", grid=(N_BRANCH, halves, S // tm),
        in_specs=[pl.BlockSpec((tm, MG_COLS), lambda k, j, i: (i, j)),
                  pl.BlockSpec((tm, MG_COLS), lambda k, j, i: (i, k * halves + j)),
                  pl.BlockSpec((tm, MG_COLS), lambda k, j, i: (i, OFF_GATE // MG_COLS + k * halves + j)),
                  pl.BlockSpec((N_BRANCH, MG_COLS), lambda k, j, i: (0, j))],
        out_specs=[pl.BlockSpec((tm, MG_COLS), lambda k, j, i: (i, k * halves + j)),
                   pl.BlockSpec((tm, MG_COLS), lambda k, j, i: (i, OFF_GATE // MG_COLS + k * halves + j)),
                   pl.BlockSpec((8, MG_COLS), lambda k, j, i: (k, j))],
        out_shape=[jax.ShapeDtypeStruct((S, N_BRANCH * D_MODEL), BF16), jax.ShapeDtypeStruct((S, IN_W), BF16),
                   jax.ShapeDtypeStruct((8 * N_BRANCH, D_MODEL), F32)],
        compiler_params=_cp(("parallel", "parallel", "arbitrary")),
    )(dmerged, ups, proj, gate_b)
    return dups, dproj, dgb.reshape(N_BRANCH, 8, D_MODEL)[:, 0]


def attn_fwd(q, kv, tm=512):
    S = q.shape[0]
    M = kv.shape[0]
    tm = _tile(S, tm)
    scale = XA_HD ** -0.5

    def body(q_ref, kv_ref, o_ref):
        for hh in range(XA_HEADS):
            cs = pl.ds(hh * XA_HD, XA_HD)
            qh = q_ref[:, cs]
            kh = kv_ref[:, cs]
            vh = kv_ref[:, pl.ds(D_MODEL + hh * XA_HD, XA_HD)]
            s = _dot(qh, kh, NT) * scale
            p = jnp.exp(s - jnp.max(s, axis=-1, keepdims=True))
            p = p / jnp.sum(p, axis=-1, keepdims=True)
            o_ref[:, cs] = _dot(p.astype(BF16), vh, NN).astype(o_ref.dtype)

    return pl.pallas_call(
        body, name="attn_fwd", grid=(S // tm,),
        in_specs=[pl.BlockSpec((tm, D_MODEL), lambda i: (i, 0)), pl.BlockSpec((M, 2 * D_MODEL), lambda i: (0, 0))],
        out_specs=pl.BlockSpec((tm, D_MODEL), lambda i: (i, 0)), out_shape=jax.ShapeDtypeStruct((S, D_MODEL), BF16),
        compiler_params=_cp(("parallel",)),
    )(q, kv)


def attn_bwd(q, kv, do, tm=512):
    S = q.shape[0]
    M = kv.shape[0]
    tm = _tile(S, tm)
    scale = XA_HD ** -0.5

    def body(q_ref, kv_ref, do_ref, dq_ref, dkv_ref):
        @pl.when(pl.program_id(0) == 0)
        def _():
            dkv_ref[...] = jnp.zeros_like(dkv_ref)

        for hh in range(XA_HEADS):
            cs = pl.ds(hh * XA_HD, XA_HD)
            vs = pl.ds(D_MODEL + hh * XA_HD, XA_HD)
            qh = q_ref[:, cs]
            kh = kv_ref[:, cs]
            vh = kv_ref[:, vs]
            doh = do_ref[:, cs]
            s = _dot(qh, kh, NT) * scale
            p = jnp.exp(s - jnp.max(s, axis=-1, keepdims=True))
            p = p / jnp.sum(p, axis=-1, keepdims=True)
            dp = _dot(doh, vh, NT)
            ds = (p * (dp - jnp.sum(dp * p, axis=-1, keepdims=True)) * scale).astype(BF16)
            dq_ref[:, cs] = _dot(ds, kh, NN).astype(dq_ref.dtype)
            dkv_ref[:, cs] += _dot(ds, qh, TN)
            dkv_ref[:, vs] += _dot(p.astype(BF16), doh, TN)

    row = pl.BlockSpec((tm, D_MODEL), lambda i: (i, 0))
    full = pl.BlockSpec((M, 2 * D_MODEL), lambda i: (0, 0))
    return pl.pallas_call(
        body, name="attn_bwd", grid=(S // tm,), in_specs=[row, full, row], out_specs=[row, full],
        out_shape=[jax.ShapeDtypeStruct((S, D_MODEL), BF16), jax.ShapeDtypeStruct((M, 2 * D_MODEL), F32)],
        compiler_params=_cp(("arbitrary",)),
    )(q, kv, do)


def sum_parts(parts, own=None, tm=256):
    n, R, C = parts.shape
    tm = _tile(R, tm)
    has_own = own is not None

    def body(*refs):
        p_ref, o_ref = refs[0], refs[-1]
        acc = refs[1][...].astype(F32) if has_own else p_ref[0].astype(F32)
        for j in range(0 if has_own else 1, n):
            acc = acc + p_ref[j].astype(F32)
        o_ref[...] = acc

    row = pl.BlockSpec((tm, C), lambda i: (i, 0))
    return pl.pallas_call(
        body, name="sum_parts", grid=(R // tm,),
        in_specs=[pl.BlockSpec((n, tm, C), lambda i: (0, i, 0))] + ([row] if has_own else []), out_specs=row,
        out_shape=jax.ShapeDtypeStruct((R, C), F32), compiler_params=_cp(("parallel",)),
    )(*([parts, own] if has_own else [parts]))


def adamw(w, g, m, v, tm=256):
    R, C = w.shape
    tm = _tile(R, tm)
    c1 = 1.0 / (1.0 - ADAM_B1 ** ADAM_STEP)
    c2 = 1.0 / (1.0 - ADAM_B2 ** ADAM_STEP)

    def body(w_ref, g_ref, m_ref, v_ref, d_ref, nm_ref, nv_ref):
        gv = g_ref[...]
        nm = ADAM_B1 * m_ref[...] + (1.0 - ADAM_B1) * gv
        nv = ADAM_B2 * v_ref[...] + (1.0 - ADAM_B2) * (gv * gv)
        nm_ref[...] = nm
        nv_ref[...] = nv
        d_ref[...] = -ADAM_LR * ((nm * c1) / (jnp.sqrt(nv * c2) + ADAM_EPS) + ADAM_WD * w_ref[...])

    blk = pl.BlockSpec((tm, C), lambda i: (i, 0))
    sd = jax.ShapeDtypeStruct((R, C), F32)
    return pl.pallas_call(
        body, name="adamw", grid=(R // tm,), in_specs=[blk] * 4, out_specs=[blk] * 3, out_shape=[sd] * 3,
        compiler_params=_cp(("parallel",)),
    )(w, g, m, v)


ANY = pl.BlockSpec(memory_space=pl.ANY)


def _place():
    return lax.axis_index("x"), lax.axis_index("y"), lax.axis_index("c")


def _slot(px, py, pc):
    return 4 * px + 2 * py + pc


def all_gather(name, shards, after=()):
    n = len(shards)
    n_in = n + len(after)

    def body(*refs):
        x_refs, out_refs = refs[:n], refs[n_in:n_in + n]
        send_sems, recv_sems, local_sems = refs[n_in + n:]
        x, y, c = _place()
        me, sibling = (x, y, c), (x, y, 1 - c)
        chips = [(1 - x, y), (x, 1 - y), (1 - x, 1 - y)]

        def copy(a, k, block, to, src=None):
            rows = out_refs[a].at[_slot(*block)]
            return pltpu.make_async_remote_copy(
                src_ref=rows if src is None else src, dst_ref=rows,
                send_sem=send_sems.at[7 * a + k], recv_sem=recv_sems.at[7 * a + k],
                device_id=to, device_id_type=MESH)

        mine = [pltpu.make_async_copy(x_refs[a], out_refs[a].at[_slot(*me)], local_sems.at[a]) for a in range(n)]
        for cp in mine:
            cp.start()
        first = []
        for a in range(n):
            first.append(copy(a, 0, me, sibling, src=x_refs[a]))
            first += [copy(a, 1 + j, me, (*chip, c), src=x_refs[a]) for j, chip in enumerate(chips)]
        for cp in first:
            cp.start()
        passed = []
        for a in range(n):
            for j, chip in enumerate(chips):
                copy(a, 1 + j, (*chip, c), me).wait_recv()
                cp = copy(a, 4 + j, (*chip, c), sibling)
                cp.start()
                passed.append(cp)
        for a in range(n):
            copy(a, 0, sibling, me).wait_recv()
            for j, chip in enumerate(chips):
                copy(a, 4 + j, (*chip, 1 - c), me).wait_recv()
        for cp in first + passed:
            cp.wait_send()
        for cp in mine:
            cp.wait()

    return pl.pallas_call(
        body, name=name, in_specs=[ANY] * n_in, out_specs=[ANY] * n,
        out_shape=[jax.ShapeDtypeStruct((N_DEV, *s.shape), s.dtype) for s in shards],
        scratch_shapes=[pltpu.SemaphoreType.DMA((7 * n,)), pltpu.SemaphoreType.DMA((7 * n,)), pltpu.SemaphoreType.DMA((n,))],
    )(*shards, *after)


HBM = pl.BlockSpec(memory_space=pltpu.HBM)
SEM = pl.BlockSpec(memory_space=pltpu.SEMAPHORE)
EFFECT = pltpu.SideEffectType.DATAFLOW_SIDE_EFFECTING
N_PEER = N_DEV - 1
RELATIONS = [(dx, dy, dc) for dx in (0, 1) for dy in (0, 1) for dc in (0, 1)][1:]


def _peer(place, rel):
    return tuple(1 - v if d else v for v, d in zip(place, rel))


def gather_start(name, shards, me, before):
    n = len(shards)

    def body(*refs):
        x_refs, land_refs = refs[:n], refs[n:2 * n]
        send_sems, recv_sems = refs[2 * n + len(before):2 * n + len(before) + 2]
        token = refs[-1]
        place = _place()
        mine = _slot(*place)
        for a in range(n):
            for rel in RELATIONS:
                pltpu.make_async_remote_copy(
                    src_ref=x_refs[a], dst_ref=land_refs[a].at[mine], send_sem=send_sems.at[a], recv_sem=recv_sems.at[a],
                    device_id=_peer(place, rel), device_id_type=MESH).start()
        token[...] = jnp.zeros_like(token)

    lands = [lax.dynamic_update_index_in_dim(lax.empty((N_DEV, *s.shape), s.dtype), s, me, 0) for s in shards]
    outs = pl.pallas_call(
        body, name=name,
        in_specs=[HBM] * (2 * n) + [ANY] * len(before),
        out_specs=[SEM, SEM] + [HBM] * (2 * n) + [pl.BlockSpec(memory_space=pltpu.VMEM)],
        out_shape=[pltpu.SemaphoreType.DMA((n,)), pltpu.SemaphoreType.DMA((n,))]
        + [pltpu.HBM(t.shape, t.dtype) for t in (*shards, *lands)] + [jax.ShapeDtypeStruct((8, LANE), F32)],
        input_output_aliases={i: 2 + i for i in range(2 * n)},
        compiler_params=pltpu.CompilerParams(has_side_effects=EFFECT),
    )(*[pltpu.with_memory_space_constraint(t, pltpu.HBM) for t in (*shards, *lands)], *before)
    return (outs[0], outs[1], outs[2:2 + n], outs[2 + n:2 + 2 * n]), outs[-1]


def gather_wait(name, state, after):
    send_sems, recv_sems, shards, lands = state
    n = len(shards)

    def body(*refs):
        land_refs = refs[n:2 * n]
        s_sems, r_sems = refs[2 * n:2 * n + 2]
        place = _place()
        for a in range(n):
            seven = land_refs[a].at[pl.ds(0, N_PEER)]
            cp = pltpu.make_async_remote_copy(
                src_ref=seven, dst_ref=seven, send_sem=s_sems.at[a], recv_sem=r_sems.at[a], device_id=place, device_id_type=MESH)
            cp.wait_send()
            cp.wait_recv()

    outs = pl.pallas_call(
        body, name=name,
        in_specs=[HBM] * (2 * n) + [SEM, SEM] + [ANY] * len(after), out_specs=[HBM] * (2 * n),
        out_shape=[pltpu.HBM(t.shape, t.dtype) for t in (*shards, *lands)],
        input_output_aliases={i: i for i in range(2 * n)},
        compiler_params=pltpu.CompilerParams(has_side_effects=EFFECT),
    )(*shards, *lands, send_sems, recv_sems, *after)
    return outs[n:]


def exchange_start(name, grads, before):
    n = len(grads)

    def body(*refs):
        g_refs, land_refs = refs[:n], refs[n:2 * n]
        send_sems, recv_sems = refs[2 * n + len(before):2 * n + len(before) + 2]
        token = refs[-1]
        place = _place()
        for a in range(n):
            for r, rel in enumerate(RELATIONS):
                p = _peer(place, rel)
                pltpu.make_async_remote_copy(
                    src_ref=g_refs[a].at[_slot(*p)], dst_ref=land_refs[a].at[r],
                    send_sem=send_sems.at[a], recv_sem=recv_sems.at[a], device_id=p, device_id_type=MESH).start()
        token[...] = jnp.zeros_like(token)

    lands = [lax.empty((N_PEER, *g.shape[1:]), g.dtype) for g in grads]
    outs = pl.pallas_call(
        body, name=name,
        in_specs=[HBM] * (2 * n) + [ANY] * len(before),
        out_specs=[SEM, SEM] + [HBM] * (2 * n) + [pl.BlockSpec(memory_space=pltpu.VMEM)],
        out_shape=[pltpu.SemaphoreType.DMA((n,)), pltpu.SemaphoreType.DMA((n,))]
        + [pltpu.HBM(g.shape, g.dtype) for g in grads] + [pltpu.HBM(t.shape, t.dtype) for t in lands]
        + [jax.ShapeDtypeStruct((8, LANE), F32)],
        input_output_aliases={i: 2 + i for i in range(2 * n)},
        compiler_params=pltpu.CompilerParams(has_side_effects=EFFECT),
    )(*[pltpu.with_memory_space_constraint(t, pltpu.HBM) for t in (*grads, *lands)], *before)
    return (outs[0], outs[1], outs[2:2 + n], outs[2 + n:2 + 2 * n]), outs[-1]


def exchange_wait(name, state, after):
    send_sems, recv_sems, grads, lands = state
    n = len(grads)

    def body(*refs):
        g_refs, land_refs = refs[:n], refs[n:2 * n]
        s_sems, r_sems = refs[2 * n:2 * n + 2]
        place = _place()
        for a in range(n):
            cp = pltpu.make_async_remote_copy(
                src_ref=g_refs[a].at[pl.ds(0, N_PEER)], dst_ref=land_refs[a],
                send_sem=s_sems.at[a], recv_sem=r_sems.at[a], device_id=place, device_id_type=MESH)
            cp.wait_send()
            cp.wait_recv()

    outs = pl.pallas_call(
        body, name=name,
        in_specs=[HBM] * (2 * n) + [SEM, SEM, ANY], out_specs=[HBM] * (2 * n),
        out_shape=[pltpu.HBM(t.shape, t.dtype) for t in (*grads, *lands)],
        input_output_aliases={i: i for i in range(2 * n)},
        compiler_params=pltpu.CompilerParams(has_side_effects=EFFECT),
    )(*grads, *lands, send_sems, recv_sems, after)
    return outs[:n], outs[n:]


WEIGHTS = ['norm_mix_w', 'w_in', 'hg_lb_raw', 'hg_norm_w', 'cv_dw_w', 'cv_dw_b', 'cv_ln_w', 'cv_ln_b', 'pl_w', 'pl_scale',
           'lru_conv_w', 'lru_conv_b', 'lru_wa', 'lru_ba', 'lru_wx', 'lru_bx', 'lru_lambda', 'gate_b', 'w_branch', 'w_out',
           'norm_mem_w', 'mem_norm_w', 'xa_wq', 'xa_wkv', 'xa_wo', 'norm_ffn_w', 'ffn_w1', 'ffn_w2', 'final_norm_w']
BIG = ('w_in', 'w_branch', 'w_out', 'xa_wq', 'xa_wkv', 'xa_wo', 'ffn_w1', 'ffn_w2')
SMALL_SHARDED = ('cv_dw_w', 'lru_conv_w', 'gate_b')
SMALL = tuple(n for n in WEIGHTS if n not in BIG and n not in SMALL_SHARDED)
PACK_ROWS = 256


def _pack(arrs):
    flat = jnp.concatenate([a.reshape(-1).astype(F32) for a in arrs])
    tile = PACK_ROWS * LANE
    padded = -(-flat.shape[0] // tile) * tile
    return jnp.pad(flat, (0, padded - flat.shape[0])).reshape(-1, LANE)


def _unpack(packed, shapes):
    flat = packed.reshape(-1)
    out, off = [], 0
    for s in shapes:
        n = math.prod(s)
        out.append(flat[off:off + n].reshape(s))
        off += n
    return out


def _gather_last(g, shard_shape):
    nd = len(shard_shape)
    full = jnp.moveaxis(g, 0, nd - 1)
    return full.reshape(*shard_shape[:-1], N_DEV * shard_shape[-1])


def _natural(blocks):
    nb, k, c = blocks.shape
    return jnp.transpose(blocks, (1, 0, 2)).reshape(k, nb * c)


def _block_diag(w):
    w2 = w.reshape(4, 2, 64, 64)
    z = jnp.zeros((4, 64, 64), w.dtype)
    return jnp.concatenate([jnp.concatenate([w2[:, 0], z], axis=2), jnp.concatenate([z, w2[:, 1]], axis=2)], axis=1)


def _block_diag_t(d):
    return jnp.stack([d[:, :64, :64], d[:, 64:, 64:]], axis=1).reshape(8, 64, 64)


def _lower_bounds(raw):
    lb = jnp.cumsum(jax.nn.softmax(raw.astype(F32), axis=0), axis=0)
    return lb - lb[0:1]


def _decay_rates(lam):
    return (LRU_C * jax.nn.softplus(-lam.astype(F32))).reshape(DEPTH, BRANCH_W)


def _relu2(acc):
    r = jnp.maximum(acc, 0.0)
    return acc, r * r


def _relu2_grad(acc, u):
    return (acc * 2.0 * jnp.maximum(u, 0.0),)


def _add(acc, e):
    return (acc + e,)


def _layer_fwd(x0, mem, p, g, rest, after=()):
    h1 = rms_fwd("rms_mix", x0, p['norm_mix_w'])
    proj = mm_nt("mm_in", h1, g['w_in'], tn=2176, after=after)[0]
    bcat, states, o_hg = hgrn_fwd(proj, p['lb'], p['hg_norm_w'])
    zc = cv_fwd(proj, p['cv_w32'], p['cv_dw_b'])
    bcat = ln_silu_fwd(zc, p['cv_ln_w'], p['cv_ln_b'], bcat)
    bcat = pool_fwd(proj, p['pl_w'], p['pl_scale'], bcat)
    bcat, hst = lru_fwd(proj, p['lru_cw8'], p['lru_conv_b'], p['wa_bd'], p['lru_ba'], p['wx_bd'], p['lru_bx'], p['sp8'], bcat)
    more, after = rest(bcat)
    g = {**g, **more}
    ups, merged = merge_fwd(bcat, g['w_branch'], proj, p['gate_b'], after=after)
    x1 = mm_nn("mm_out", merged, g['w_out'], epi=_add, extras=(x0,))[0]
    h2 = rms_fwd("rms_mem", x1, p['norm_mem_w'])
    q = mm_nn("mm_q", h2, g['xa_wq'], out_dtype=BF16)[0]
    memn = rms_fwd("rms_memtok", mem, p['mem_norm_w'])
    kv = mm_nn("mm_kv", memn, g['xa_wkv'], out_dtype=BF16, tn=2048)[0]
    oa = attn_fwd(q, kv)
    x2 = mm_nn("mm_o", oa, g['xa_wo'], epi=_add, extras=(x1,))[0]
    h3 = rms_fwd("rms_ffn", x2, p['norm_ffn_w'])
    u, act = mm_nn("mm_ffn1", h3, g['ffn_w1'], epi=_relu2, out_dtypes=[BF16, BF16])
    x3 = mm_nn("mm_ffn2", act, g['ffn_w2'], epi=_add, extras=(x2,))[0]
    res = dict(x0=x0, h1=h1, proj=proj, states=states, o_hg=o_hg, zc=zc, hst=hst, bcat=bcat, ups=ups, merged=merged,
               x1=x1, h2=h2, q=q, memn=memn, kv=kv, oa=oa, x2=x2, h3=h3, u=u, act=act)
    return x3, res, g


def _layer_bwd(dx3, mem, p, g, r, midway, finish):
    gs, gb = {}, {}
    du = mm_nt("mm_dffn2", dx3, g['ffn_w2'], out_dtype=BF16, epi=_relu2_grad, extras=(r['u'],))[0]
    gb['ffn_w2'] = mm_tn("mm_gw2", r['act'], dx3).reshape(N_DEV, -1, D_MODEL)
    gb['ffn_w1'] = mm_tn_cb("mm_gw1", r['h3'], du, N_DEV)
    dh3 = mm_nt("mm_dffn1", du, g['ffn_w1'], out_dtype=BF16)[0]
    dx2, gs['norm_ffn_w'] = rms_bwd("rmsb_ffn", r['x2'], p['norm_ffn_w'], dh3, dx3)
    doa = mm_nt("mm_do", dx2, g['xa_wo'], out_dtype=BF16)[0]
    gb['xa_wo'] = mm_tn("mm_gwo", r['oa'], dx2).reshape(N_DEV, -1, D_MODEL)
    dq, dkv = attn_bwd(r['q'], r['kv'], doa)
    gb['xa_wq'] = mm_tn("mm_gwq", r['h2'], dq).reshape(N_DEV, -1, D_MODEL)
    dh2 = mm_nt("mm_dq", dq, g['xa_wq'], out_dtype=BF16)[0]
    gb['xa_wkv'] = mm_tn_cb("mm_gwkv", r['memn'], dkv, N_DEV)
    dmemn = mm_nt("mm_dkv", dkv, g['xa_wkv'], out_dtype=BF16)[0]
    _, gs['mem_norm_w'] = rms_bwd("rmsb_memtok", mem, p['mem_norm_w'], dmemn)
    dx1, gs['norm_mem_w'] = rms_bwd("rmsb_mem", r['x1'], p['norm_mem_w'], dh2, dx2)
    after = midway(gb, dx1)
    gb = {}
    dmerged = mm_nt("mm_dout", dx1, g['w_out'], out_dtype=BF16, after=after)[0]
    gb['w_out'] = mm_tn("mm_gwout", r['merged'], dx1).reshape(N_DEV, -1, D_MODEL)
    dups, dproj, gs['gate_b'] = merge_bwd(dmerged, r['ups'], r['proj'], p['gate_b'])
    gwb = mm_branch_tn("mm_gwb", r['bcat'], dups, N_BRANCH)
    gb['w_branch'] = jnp.transpose(gwb.reshape(N_BRANCH, BRANCH_W, N_DEV, -1), (2, 0, 1, 3))
    dbcat = mm_branch_nt("mm_dup", dups, g['w_branch'], tm=2048)
    dproj, gs['lb'], gs['hg_norm_w'] = hgrn_bwd(r['proj'], p['lb'], p['hg_norm_w'], r['states'], r['o_hg'], dbcat, dproj)
    dzc, gs['cv_ln_w'], gs['cv_ln_b'] = ln_silu_bwd(r['zc'], p['cv_ln_w'], p['cv_ln_b'], dbcat)
    dca, dcg, dcw, gs['cv_dw_b'] = cv_bwd(r['proj'], p['cv_w32'], dzc)
    gs['cv_dw_w'] = dcw[:CV_KERNEL]
    dproj, gs['pl_w'], gs['pl_scale'] = pool_bwd(r['proj'], p['pl_w'], p['pl_scale'], dbcat, dproj)
    dlx, dly, dlcw, gs['lru_conv_b'], dwa, gs['lru_ba'], dwx, gs['lru_bx'], gs['sp8'] = lru_bwd(
        r['proj'], p['lru_cw8'], p['lru_conv_b'], p['wa_bd'], p['lru_ba'], p['wx_bd'], p['lru_bx'], p['sp8'], r['hst'], dbcat)
    gs['lru_conv_w'] = dlcw[:LRU_CONV]
    gs['lru_wa'], gs['lru_wx'] = _block_diag_t(dwa), _block_diag_t(dwx)
    gs['lru_ba'], gs['lru_bx'] = gs['lru_ba'].reshape(8, 64), gs['lru_bx'].reshape(8, 64)
    for off, piece in ((OFF_CV, dca), (OFF_CV + BRANCH_W, dcg), (OFF_LX, dlx), (OFF_LY, dly)):
        dproj = lax.dynamic_update_slice(dproj, piece, (0, off))
    gb['w_in'] = mm_tn("mm_gwin", dproj, r['h1'], tm=2176, tk=1024).reshape(N_DEV, -1, D_MODEL)
    dh1 = mm_nn("mm_din", dproj, g['w_in'], out_dtype=BF16, tk=4352, after=finish(gb, dx1))[0]
    dx0, gs['norm_mix_w'] = rms_bwd("rmsb_mix", r['x0'], p['norm_mix_w'], dh1, dx1)
    return dx0, gs


def kernel(x, mem, norm_mix_w, w_in, hg_lb_raw, hg_norm_w, cv_dw_w, cv_dw_b, cv_ln_w, cv_ln_b, pl_w, pl_scale, lru_conv_w, lru_conv_b, lru_wa, lru_ba, lru_wx, lru_bx, lru_lambda, gate_b, w_branch, w_out, norm_mem_w, mem_norm_w, xa_wq, xa_wkv, xa_wo, norm_ffn_w, ffn_w1, ffn_w2, final_norm_w, loss_target, m_norm_mix_w, m_w_in, m_hg_lb_raw, m_hg_norm_w, m_cv_dw_w, m_cv_dw_b, m_cv_ln_w, m_cv_ln_b, m_pl_w, m_pl_scale, m_lru_conv_w, m_lru_conv_b, m_lru_wa, m_lru_ba, m_lru_wx, m_lru_bx, m_lru_lambda, m_gate_b, m_w_branch, m_w_out, m_norm_mem_w, m_mem_norm_w, m_xa_wq, m_xa_wkv, m_xa_wo, m_norm_ffn_w, m_ffn_w1, m_ffn_w2, m_final_norm_w, v_norm_mix_w, v_w_in, v_hg_lb_raw, v_hg_norm_w, v_cv_dw_w, v_cv_dw_b, v_cv_ln_w, v_cv_ln_b, v_pl_w, v_pl_scale, v_lru_conv_w, v_lru_conv_b, v_lru_wa, v_lru_ba, v_lru_wx, v_lru_bx, v_lru_lambda, v_gate_b, v_w_branch, v_w_out, v_norm_mem_w, v_mem_norm_w, v_xa_wq, v_xa_wkv, v_xa_wo, v_norm_ffn_w, v_ffn_w1, v_ffn_w2, v_final_norm_w):
    W = dict(zip(WEIGHTS, (norm_mix_w, w_in, hg_lb_raw, hg_norm_w, cv_dw_w, cv_dw_b, cv_ln_w, cv_ln_b, pl_w, pl_scale, lru_conv_w, lru_conv_b, lru_wa, lru_ba, lru_wx, lru_bx, lru_lambda, gate_b, w_branch, w_out, norm_mem_w, mem_norm_w, xa_wq, xa_wkv, xa_wo, norm_ffn_w, ffn_w1, ffn_w2, final_norm_w)))
    Mo = dict(zip(WEIGHTS, (m_norm_mix_w, m_w_in, m_hg_lb_raw, m_hg_norm_w, m_cv_dw_w, m_cv_dw_b, m_cv_ln_w, m_cv_ln_b, m_pl_w, m_pl_scale, m_lru_conv_w, m_lru_conv_b, m_lru_wa, m_lru_ba, m_lru_wx, m_lru_bx, m_lru_lambda, m_gate_b, m_w_branch, m_w_out, m_norm_mem_w, m_mem_norm_w, m_xa_wq, m_xa_wkv, m_xa_wo, m_norm_ffn_w, m_ffn_w1, m_ffn_w2, m_final_norm_w)))
    Vo = dict(zip(WEIGHTS, (v_norm_mix_w, v_w_in, v_hg_lb_raw, v_hg_norm_w, v_cv_dw_w, v_cv_dw_b, v_cv_ln_w, v_cv_ln_b, v_pl_w, v_pl_scale, v_lru_conv_w, v_lru_conv_b, v_lru_wa, v_lru_ba, v_lru_wx, v_lru_bx, v_lru_lambda, v_gate_b, v_w_branch, v_w_out, v_norm_mem_w, v_mem_norm_w, v_xa_wq, v_xa_wkv, v_xa_wo, v_norm_ffn_w, v_ffn_w1, v_ffn_w2, v_final_norm_w)))
    me = _slot(*_place())
    xs, mems, target = x[0], mem[0], loss_target[0]

    shard_shapes = [W[n].shape for n in SMALL_SHARDED]
    gathered = all_gather("ag_small", [_pack([W[n] for n in SMALL_SHARDED])])[0]
    parts = [jnp.stack(ps) for ps in zip(*[_unpack(gathered[d], shard_shapes) for d in range(N_DEV)])]
    full_small = {n: _gather_last(parts[i], shard_shapes[i]) for i, n in enumerate(SMALL_SHARDED)}
    lb_all, lb_vjp = jax.vjp(_lower_bounds, hg_lb_raw)
    sp8_all, sp8_vjp = jax.vjp(_decay_rates, lru_lambda)

    def layer_params(l):
        p = {n: W[n][l] for n in SMALL if n != 'final_norm_w'}
        p['lb'] = lb_all[l]
        p['sp8'] = sp8_all[l]
        p['cv_w32'] = jnp.pad(full_small['cv_dw_w'][l], ((0, 32 - CV_KERNEL), (0, 0)))
        p['lru_cw8'] = jnp.pad(full_small['lru_conv_w'][l], ((0, 8 - LRU_CONV), (0, 0)))
        p['gate_b'] = full_small['gate_b'][l]
        p['wa_bd'], p['wx_bd'] = _block_diag(lru_wa[l]), _block_diag(lru_wx[l])
        p['lru_ba'], p['lru_bx'] = lru_ba[l].reshape(-1), lru_bx[l].reshape(-1)
        return p

    def shards_of(l):
        first = [jnp.transpose(w_in[l]).astype(BF16)]
        others = [w[l].astype(BF16) for w in (w_branch, w_out, xa_wq, xa_wkv, xa_wo, ffn_w1, ffn_w2)]
        return first, others

    def start_gather(l, before):
        first, others = shards_of(l)
        state_a, tok_a = gather_start(f"ag_start{l}a", first, me, before)
        state_b, tok_b = gather_start(f"ag_start{l}b", others, me, (*before, tok_a))
        return state_a, state_b, (tok_a, tok_b)

    def first_of(o):
        return dict(w_in=o[0].reshape(IN_W, D_MODEL))

    def others_of(o):
        wb = jnp.transpose(o[0], (1, 2, 0, 3)).reshape(N_BRANCH, BRANCH_W, D_MODEL)
        return dict(w_branch=wb, w_out=o[1].reshape(D_MODEL, D_MODEL),
                    xa_wq=o[2].reshape(D_MODEL, D_MODEL), xa_wkv=_natural(o[3]), xa_wo=o[4].reshape(D_MODEL, D_MODEL),
                    ffn_w1=_natural(o[5]), ffn_w2=o[6].reshape(D_FF, D_MODEL))

    params = [layer_params(l) for l in range(DEPTH)]
    mats, residuals = [], []
    xc = xs
    first, others = shards_of(0)
    whole = all_gather("ag_layer0", first)
    state_b, started = gather_start("ag_start0b", others, me, (whole[0],))
    gathers = {}
    for l in range(DEPTH):
        if l == 0:
            g_first = first_of(whole)
        else:
            state_a, state_b, _ = gathers.pop(l)
            g_first = first_of(gather_wait(f"ag_wait{l}a", state_a, (xc,)))

        def rest(mixed, l=l):
            more = others_of(gather_wait(f"ag_wait{l}b", state_b, (mixed,)))
            if l + 1 == DEPTH:
                return more, ()
            gathers[l + 1] = start_gather(l + 1, (more['w_out'],))
            return more, gathers[l + 1][2]

        xc, res, g = _layer_fwd(xc, mems, params[l], g_first, rest, after=(started,) if l == 0 else ())
        mats.append(g)
        residuals.append(res)
    loss_part, dx, g_final = loss_head(xc, final_norm_w, target)
    loss = lax.psum(loss_part, ("x", "y", "c"))

    small_grads = [None] * DEPTH
    big_grads = [{} for _ in range(DEPTH)]
    pending = []

    def send(l, group, blocks, before):
        names = list(blocks)
        state, tok = exchange_start(f"rs_start{l}{group}", [blocks[n] for n in names], (before,))
        pending.append((l, group, names, state))
        return (tok,)

    def land(after):
        l, group, names, state = pending.pop(0)
        sent, landed = exchange_wait(f"rs_wait{l}{group}", state, after)
        for n, s, t in zip(names, sent, landed):
            own = lax.dynamic_index_in_dim(s, me, 0, keepdims=False).reshape(-1, s.shape[-1])
            big_grads[l][n] = sum_parts(t.reshape(N_PEER, -1, t.shape[-1]), own).reshape(t.shape[1:])

    for l in reversed(range(DEPTH)):
        dx, small_grads[l] = _layer_bwd(dx, mems, params[l], mats[l], residuals[l],
                                        lambda blocks, dx1, l=l: send(l, "a", blocks, dx1),
                                        lambda blocks, dx1, l=l: send(l, "b", blocks, dx1))
        while pending[0][0] > l:
            land(dx)

    def stacked(n):
        return jnp.stack([small_grads[l][n] for l in range(DEPTH)])

    part = {n: stacked(n) for n in SMALL if n not in ('final_norm_w', 'hg_lb_raw', 'lru_lambda')}
    part['final_norm_w'] = g_final
    part['hg_lb_raw'] = lb_vjp(stacked('lb'))[0]
    part['lru_lambda'] = sp8_vjp(stacked('sp8'))[0]
    for n in SMALL_SHARDED:
        part[n] = stacked(n)
    names = list(SMALL) + list(SMALL_SHARDED)
    full_shapes = [part[n].shape for n in names]
    packed = _pack([part[n] for n in names])
    while len(pending) > 1:
        land(dx)
    summed = [t for layer in big_grads for t in layer.values()]
    state, _ = exchange_start("rs_small_start", [packed.reshape(N_DEV, -1, LANE)], (packed, *summed))
    sent, landed = exchange_wait("rs_small_wait", state, packed)
    mine = sum_parts(landed[0], lax.dynamic_index_in_dim(sent[0], me, 0, keepdims=False))
    total = all_gather("ag_grads", [mine])[0].reshape(-1, LANE)
    while pending:
        land(total)

    G = {}
    G['w_in'] = jnp.stack([jnp.transpose(big_grads[l]['w_in']) for l in range(DEPTH)])
    for n in ('w_branch', 'w_out', 'xa_wq', 'xa_wkv', 'xa_wo', 'ffn_w1', 'ffn_w2'):
        G[n] = jnp.stack([big_grads[l][n] for l in range(DEPTH)])
    for n, t in zip(names, _unpack(total, full_shapes)):
        if n in SMALL_SHARDED:
            c = t.shape[-1] // N_DEV
            t = lax.dynamic_slice_in_dim(t, me * c, c, axis=t.ndim - 1)
        G[n] = t

    delta, new_m, new_v = {}, {}, {}
    for n in BIG:
        c = W[n].shape[-1]
        d, nm, nv = adamw(W[n].reshape(-1, c), G[n].reshape(-1, c), Mo[n].reshape(-1, c), Vo[n].reshape(-1, c))
        delta[n], new_m[n], new_v[n] = d.reshape(W[n].shape), nm.reshape(W[n].shape), nv.reshape(W[n].shape)
    shapes = [W[n].shape for n in names]
    d, nm, nv = adamw(_pack([W[n] for n in names]), _pack([G[n] for n in names]), _pack([Mo[n] for n in names]), _pack([Vo[n] for n in names]))
    for n, a, b, c in zip(names, _unpack(d, shapes), _unpack(nm, shapes), _unpack(nv, shapes)):
        delta[n], new_m[n], new_v[n] = a, b, c
    return (loss, dx[None], *[G[n] for n in WEIGHTS], *[delta[n] for n in WEIGHTS],
            *[new_m[n] for n in WEIGHTS], *[new_v[n] for n in WEIGHTS])
```

```python
import functools
import math

import jax
import jax.numpy as jnp
from jax import lax
from jax.experimental import pallas as pl
from jax.experimental.pallas import tpu as pltpu

F32 = jnp.float32
BF16 = jnp.bfloat16
I32 = jnp.int32

N_DEV = 8
D_MODEL = 1024
DEPTH = 4
CHUNK = 64
EPS = 1e-6
HG_HEADS = 4
BRANCH_W = 512
CV_KERNEL = 31
POOL_WINDOWS = (2, 4, 8, 16)
LRU_CONV = 4
LRU_C = 8.0
XA_HEADS = 4
XA_HD = D_MODEL // XA_HEADS
D_FF = 4 * D_MODEL
IN_W = 8704
OFF_Q, OFF_F, OFF_V, OFF_G, OFF_CV, OFF_PL, OFF_LX, OFF_LY, OFF_GATE = 0, 512, 1024, 1536, 2048, 3072, 3584, 4096, 4608
LANE = 128
ADAM_LR, ADAM_B1, ADAM_B2, ADAM_EPS, ADAM_WD, ADAM_STEP = 0.001, 0.9, 0.999, 1e-08, 0.01, 10
VMEM_LIMIT = 56 * 1024 * 1024
MESH = pl.DeviceIdType.MESH
NEG = -1e30
ANY_SPACE = pl.BlockSpec(memory_space=pl.ANY)


def _cp(sem, **kw):
    return pltpu.CompilerParams(dimension_semantics=sem, vmem_limit_bytes=VMEM_LIMIT, **kw)


def _sigmoid(x):
    return 1.0 / (1.0 + jnp.exp(-x))


def _dsilu(x, s):
    return s * (1.0 + x * (1.0 - s))


def _dot(a, b, cdims, precision=None):
    return lax.dot_general(a, b, (cdims, ((), ())), preferred_element_type=F32, precision=precision)


NN = ((1,), (0,))
NT = ((1,), (1,))
TN = ((0,), (0,))


def _mm(name, a, b, *, grid, a_spec, b_spec, o_specs, out_shapes, acc_shape, cdims, epi=None, extras=(), extra_specs=(), after=()):
    nk = grid[2]
    n_e, n_o = len(extras), len(out_shapes)
    extras = (*extras, *after)
    extra_specs = (*extra_specs, *[ANY_SPACE] * len(after))

    def body(*refs):
        a_ref, b_ref = refs[0], refs[1]
        e_refs = refs[2:2 + n_e]
        o_refs = refs[2 + len(extras):2 + len(extras) + n_o]

        def finish(acc):
            vals = epi(acc, *[r[...] for r in e_refs]) if epi is not None else (acc,)
            for r, v in zip(o_refs, vals):
                r[...] = v.astype(r.dtype)

        part = _dot(a_ref[...].astype(BF16), b_ref[...].astype(BF16), cdims)
        if nk == 1:
            finish(part)
        else:
            acc_ref = refs[-1]
            k = pl.program_id(2)

            @pl.when(k == 0)
            def _():
                acc_ref[...] = part

            @pl.when(k > 0)
            def _():
                acc_ref[...] += part

            @pl.when(k == nk - 1)
            def _():
                finish(acc_ref[...])

    return pl.pallas_call(
        body, name=name, grid=grid,
        in_specs=[a_spec, b_spec, *extra_specs], out_specs=list(o_specs), out_shape=list(out_shapes),
        scratch_shapes=[] if nk == 1 else [pltpu.VMEM(acc_shape, F32)],
        compiler_params=_cp(("parallel", "parallel", "arbitrary")),
    )(a, b, *extras)


def _tile(n, pref):
    t = min(n, pref)
    while n % t:
        t //= 2
    return t


def mm_nt(name, a, b, out_dtype=F32, epi=None, extras=(), n_out=1, out_dtypes=None, tm=1024, tn=1024, tk=2048, after=()):
    M, K = a.shape
    N = b.shape[0]
    tm, tn, tk = _tile(M, tm), _tile(N, tn), _tile(K, tk)
    odt = out_dtypes or [out_dtype] * n_out
    o_spec = pl.BlockSpec((tm, tn), lambda i, j, k: (i, j))
    return _mm(name, a, b, grid=(M // tm, N // tn, K // tk),
               a_spec=pl.BlockSpec((tm, tk), lambda i, j, k: (i, k)),
               b_spec=pl.BlockSpec((tn, tk), lambda i, j, k: (j, k)),
               o_specs=[o_spec] * len(odt), out_shapes=[jax.ShapeDtypeStruct((M, N), d) for d in odt],
               acc_shape=(tm, tn), cdims=NT, epi=epi, extras=extras, extra_specs=[o_spec] * len(extras), after=after)


def mm_nn(name, a, b, out_dtype=F32, epi=None, extras=(), n_out=1, out_dtypes=None, tm=1024, tn=1024, tk=2048, after=()):
    M, K = a.shape
    N = b.shape[1]
    tm, tn, tk = _tile(M, tm), _tile(N, tn), _tile(K, tk)
    odt = out_dtypes or [out_dtype] * n_out
    o_spec = pl.BlockSpec((tm, tn), lambda i, j, k: (i, j))
    return _mm(name, a, b, grid=(M // tm, N // tn, K // tk),
               a_spec=pl.BlockSpec((tm, tk), lambda i, j, k: (i, k)),
               b_spec=pl.BlockSpec((tk, tn), lambda i, j, k: (k, j)),
               o_specs=[o_spec] * len(odt), out_shapes=[jax.ShapeDtypeStruct((M, N), d) for d in odt],
               acc_shape=(tm, tn), cdims=NN, epi=epi, extras=extras, extra_specs=[o_spec] * len(extras), after=after)


def mm_tn(name, a, b, out_dtype=BF16, tm=1024, tn=1024, tk=2048):
    K, M = a.shape
    N = b.shape[1]
    tm, tn, tk = _tile(M, tm), _tile(N, tn), _tile(K, tk)
    return _mm(name, a, b, grid=(M // tm, N // tn, K // tk),
               a_spec=pl.BlockSpec((tk, tm), lambda i, j, k: (k, i)),
               b_spec=pl.BlockSpec((tk, tn), lambda i, j, k: (k, j)),
               o_specs=[pl.BlockSpec((tm, tn), lambda i, j, k: (i, j))],
               out_shapes=[jax.ShapeDtypeStruct((M, N), out_dtype)], acc_shape=(tm, tn), cdims=TN)[0]


def mm_branch_nt(name, a, b, out_dtype=BF16, tm=1024):
    M = a.shape[0]
    G, K, N = b.shape
    tm = _tile(M, tm)
    return _mm(name, a, b, grid=(M // tm, G, 1),
               a_spec=pl.BlockSpec((tm, N), lambda i, g, k: (i, g)),
               b_spec=pl.BlockSpec((None, K, N), lambda i, g, k: (g, 0, 0)),
               o_specs=[pl.BlockSpec((tm, K), lambda i, g, k: (i, g))],
               out_shapes=[jax.ShapeDtypeStruct((M, G * K), out_dtype)], acc_shape=(tm, K), cdims=NT)[0]


def mm_branch_tn(name, a, b, groups, out_dtype=BF16, tk=2048):
    T = a.shape[0]
    K, N = a.shape[1] // groups, b.shape[1] // groups
    tk = _tile(T, tk)
    return _mm(name, a, b, grid=(groups, 1, T // tk),
               a_spec=pl.BlockSpec((tk, K), lambda g, j, k: (k, g)),
               b_spec=pl.BlockSpec((tk, N), lambda g, j, k: (k, g)),
               o_specs=[pl.BlockSpec((None, K, N), lambda g, j, k: (g, 0, 0))],
               out_shapes=[jax.ShapeDtypeStruct((groups, K, N), out_dtype)], acc_shape=(K, N), cdims=TN)[0]


def mm_tn_cb(name, a, b, nb, out_dtype=BF16, tm=1024, tk=2048):
    K, M = a.shape
    N = b.shape[1]
    c = N // nb
    tm, tk = _tile(M, tm), _tile(K, tk)
    return _mm(name, a, b, grid=(M // tm, nb, K // tk),
               a_spec=pl.BlockSpec((tk, tm), lambda i, j, k: (k, i)),
               b_spec=pl.BlockSpec((tk, c), lambda i, j, k: (k, j)),
               o_specs=[pl.BlockSpec((None, tm, c), lambda i, j, k: (j, i, 0))],
               out_shapes=[jax.ShapeDtypeStruct((nb, M, c), out_dtype)], acc_shape=(tm, c), cdims=TN)[0]


def rms_fwd(name, x, w, out_dtype=BF16, tm=512):
    S, D = x.shape
    tm = _tile(S, tm)

    def body(x_ref, w_ref, o_ref):
        xv = x_ref[...]
        r = lax.rsqrt(jnp.mean(xv * xv, axis=-1, keepdims=True) + EPS)
        o_ref[...] = (xv * r * w_ref[...]).astype(o_ref.dtype)

    return pl.pallas_call(
        body, name=name, grid=(S // tm,),
        in_specs=[pl.BlockSpec((tm, D), lambda i: (i, 0)), pl.BlockSpec((1, D), lambda i: (0, 0))],
        out_specs=pl.BlockSpec((tm, D), lambda i: (i, 0)), out_shape=jax.ShapeDtypeStruct((S, D), out_dtype),
        compiler_params=_cp(("parallel",)),
    )(x, w.reshape(1, D))


def rms_bwd(name, x, w, dh, dres=None, tm=512):
    S, D = x.shape
    tm = _tile(S, tm)
    has_res = dres is not None

    def body(*refs):
        if has_res:
            x_ref, w_ref, dh_ref, dres_ref, dx_ref, dw_ref = refs
        else:
            x_ref, w_ref, dh_ref, dx_ref, dw_ref = refs
        xv = x_ref[...]
        dhv = dh_ref[...].astype(F32)
        r = lax.rsqrt(jnp.mean(xv * xv, axis=-1, keepdims=True) + EPS)
        g = dhv * w_ref[...]
        dx = r * g - xv * (r * r * r) * jnp.mean(xv * g, axis=-1, keepdims=True)
        if has_res:
            dx = dx + dres_ref[...]
        dx_ref[...] = dx

        @pl.when(pl.program_id(0) == 0)
        def _():
            dw_ref[...] = jnp.zeros_like(dw_ref)

        dw_ref[...] += jnp.sum(dhv * xv * r, axis=0, keepdims=True)

    row = pl.BlockSpec((tm, D), lambda i: (i, 0))
    vec = pl.BlockSpec((1, D), lambda i: (0, 0))
    args = [x, w.reshape(1, D), dh] + ([dres] if has_res else [])
    dx, dw = pl.pallas_call(
        body, name=name, grid=(S // tm,),
        in_specs=[row, vec, row] + ([row] if has_res else []),
        out_specs=[row, vec], out_shape=[jax.ShapeDtypeStruct((S, D), F32), jax.ShapeDtypeStruct((1, D), F32)],
        compiler_params=_cp(("arbitrary",)),
    )(*args)
    return dx, dw.reshape(D)


def loss_head(x, w, target, tm=512):
    S, D = x.shape
    tm = _tile(S, tm)

    def body(x_ref, w_ref, t_ref, loss_ref, dx_ref, dw_ref):
        xv = x_ref[...]
        wv = w_ref[...]
        r = lax.rsqrt(jnp.mean(xv * xv, axis=-1, keepdims=True) + EPS)
        y = xv * r * wv
        err = y - t_ref[...]
        dy = err * (1.0 / D)
        g = dy * wv
        dx_ref[...] = r * g - xv * (r * r * r) * jnp.mean(xv * g, axis=-1, keepdims=True)

        @pl.when(pl.program_id(0) == 0)
        def _():
            dw_ref[...] = jnp.zeros_like(dw_ref)
            loss_ref[...] = jnp.zeros_like(loss_ref)

        dw_ref[...] += jnp.sum(dy * xv * r, axis=0, keepdims=True)
        part = 0.5 * jnp.sum(jnp.mean(err * err, axis=-1, keepdims=True), axis=0, keepdims=True)
        loss_ref[...] += jnp.broadcast_to(part, loss_ref.shape)

    row = pl.BlockSpec((tm, D), lambda i: (i, 0))
    vec = pl.BlockSpec((1, D), lambda i: (0, 0))
    loss, dx, dw = pl.pallas_call(
        body, name="loss_head", grid=(S // tm,),
        in_specs=[row, vec, row],
        out_specs=[pl.BlockSpec((1, LANE), lambda i: (0, 0)), row, vec],
        out_shape=[jax.ShapeDtypeStruct((1, LANE), F32), jax.ShapeDtypeStruct((S, D), F32), jax.ShapeDtypeStruct((1, D), F32)],
        compiler_params=_cp(("arbitrary",)),
    )(x, w.reshape(1, D), target)
    return loss[0, 0], dx, dw.reshape(D)


SUB = 16
HG_W = HG_HEADS * LANE


def _hg_gates(q, f, lbv):
    sig = _sigmoid(f)
    fg = lbv + (1.0 - lbv) * sig
    sq = _sigmoid(q)
    return sig, fg, 1.0 - fg, sq, q * sq


def _hg_cumsum(logf):
    ri = lax.broadcasted_iota(I32, (CHUNK, CHUNK), 0)
    ci = lax.broadcasted_iota(I32, (CHUNK, CHUNK), 1)
    return _dot((ci <= ri).astype(F32), logf, NN, precision=lax.Precision.HIGHEST)


def _hg_rows():
    return lax.broadcasted_iota(I32, (CHUNK, LANE), 0)


def _hg_below(qf, kk, b, rows):
    blocks, parts = [jnp.zeros((SUB, CHUNK), F32)], []
    for i in range(1, CHUNK // SUB):
        bref = b[SUB * i - 1:SUB * i, :]
        rs = slice(SUB * i, SUB * (i + 1))
        eq = jnp.exp(b[rs] - bref)
        below = rows < SUB * i
        ek = jnp.exp(jnp.where(below, bref - b, NEG))
        qi = (qf[rs] * eq).astype(BF16)
        ki = (kk * ek).astype(BF16)
        blocks.append(_dot(qi, ki, NT))
        parts.append((qi, ki, eq, ek))
    return jnp.concatenate(blocks, axis=0), parts


def hgrn_fwd(proj, lb, nw):
    S = proj.shape[0]
    NC = S // CHUNK
    H = HG_HEADS

    def body(q_ref, f_ref, v_ref, g_ref, lb_ref, nw_ref, out_ref, st_out_ref, o_ref, st, kk_s, b_s):
        c = pl.program_id(0)

        @pl.when(c == 0)
        def _():
            st[...] = jnp.zeros_like(st)

        st_out_ref[...] = st[...]
        sig, fg, kk_all, sq, qf_all = _hg_gates(q_ref[...], f_ref[...], lb_ref[...])
        b_all = _hg_cumsum(jnp.log(fg))
        kk_s[...] = kk_all
        b_s[...] = b_all
        rows = _hg_rows()
        r16 = lax.broadcasted_iota(I32, (SUB, LANE), 0)
        for h in range(H):
            cs = slice(h * LANE, (h + 1) * LANE)
            qf, kk, b, v, g = qf_all[:, cs], kk_all[:, cs], b_all[:, cs], v_ref[:, cs], g_ref[:, cs]
            st_in = st[h]
            diag = []
            for i in range(CHUNK // SUB):
                rs = slice(SUB * i, SUB * (i + 1))
                acc = jnp.zeros((SUB, LANE), F32)
                for j in range(SUB):
                    row = pl.ds(SUB * i + j, 1)
                    e = jnp.exp(jnp.where(r16 >= j, b[rs] - b_s[row, cs], NEG))
                    col = jnp.sum(qf[rs] * (kk_s[row, cs] * e), axis=1, keepdims=True)
                    acc = acc + col * v_ref[row, cs]
                diag.append(acc)
            poff, _ = _hg_below(qf, kk, b, rows)
            vb = v.astype(BF16)
            bl = b[CHUNK - 1:CHUNK, :]
            o = (jnp.concatenate(diag, axis=0) + _dot(poff.astype(BF16), vb, NN)
                 + _dot((qf * jnp.exp(b)).astype(BF16), st_in.astype(BF16), NT))
            st[h] = st_in * jnp.exp(bl) + _dot(vb, (kk * jnp.exp(bl - b)).astype(BF16), TN)
            o_ref[:, cs] = o
            r = lax.rsqrt(jnp.mean(o * o, axis=-1, keepdims=True) + EPS)
            out_ref[:, cs] = (o * r * nw_ref[...] * (g * _sigmoid(g))).astype(out_ref.dtype)

    def seg(off):
        return pl.BlockSpec((CHUNK, HG_W), lambda c: (c, off // HG_W))

    blk = pl.BlockSpec((CHUNK, HG_W), lambda c: (c, 0))
    full = pltpu.VMEM((CHUNK, HG_W), F32)
    return pl.pallas_call(
        body, name="hgrn_fwd", grid=(NC,),
        in_specs=[seg(OFF_Q), seg(OFF_F), seg(OFF_V), seg(OFF_G),
                  pl.BlockSpec((1, HG_W), lambda c: (0, 0)), pl.BlockSpec((1, LANE), lambda c: (0, 0))],
        out_specs=[blk, pl.BlockSpec((None, H, LANE, LANE), lambda c: (c, 0, 0, 0)), blk],
        out_shape=[jax.ShapeDtypeStruct((S, 4 * HG_W), BF16), jax.ShapeDtypeStruct((NC, H, LANE, LANE), F32),
                   jax.ShapeDtypeStruct((S, HG_W), F32)],
        scratch_shapes=[pltpu.VMEM((H, LANE, LANE), F32), full, full],
        compiler_params=_cp(("arbitrary",)),
    )(proj, proj, proj, proj, lb.reshape(1, HG_W), nw.reshape(1, LANE))


def hgrn_bwd(proj, lb, nw, states, o_pre, dbcat, dproj):
    S = proj.shape[0]
    NC = S // CHUNK
    H = HG_HEADS

    def body(q_ref, f_ref, v_ref, g_ref, lb_ref, nw_ref, st_ref, o_ref, do_ref, _,
             dp_ref, dlb_ref, dnw_ref, dst, kk_s, b_s, do_s, db_s, dkk_s, dkk_d, dv_d):
        c = pl.program_id(0)
        dq_ref, df_ref, dv_ref, dg_ref = (dp_ref.at[:, pl.ds(off, HG_W)] for off in (OFF_Q, OFF_F, OFF_V, OFF_G))

        @pl.when(c == 0)
        def _():
            dst[...] = jnp.zeros_like(dst)
            dlb_ref[...] = jnp.zeros_like(dlb_ref)
            dnw_ref[...] = jnp.zeros_like(dnw_ref)

        q_all, g_all = q_ref[...], g_ref[...]
        lbv, nwv = lb_ref[...], nw_ref[...]
        sig, fg, kk_all, sq, qf_all = _hg_gates(q_all, f_ref[...], lbv)
        b_all = _hg_cumsum(jnp.log(fg))
        o_all = o_ref[...]
        dov = do_ref[...].astype(F32)
        sg = _sigmoid(g_all)
        gsg = g_all * sg
        dnw_acc = jnp.zeros((1, LANE), F32)
        for h in range(H):
            cs = slice(h * LANE, (h + 1) * LANE)
            o = o_all[:, cs]
            r = lax.rsqrt(jnp.mean(o * o, axis=-1, keepdims=True) + EPS)
            don = dov[:, cs] * gsg[:, cs]
            dnw_acc = dnw_acc + jnp.sum(don * o * r, axis=0, keepdims=True)
            gno = don * nwv
            do_s[:, cs] = r * gno - o * (r * r * r) * jnp.mean(o * gno, axis=-1, keepdims=True)
            dg_ref[:, cs] = (dov[:, cs] * (o * r * nwv) * _dsilu(g_all[:, cs], sg[:, cs])).astype(dg_ref.dtype)
        dnw_ref[...] += jnp.broadcast_to(dnw_acc, dnw_ref.shape)
        kk_s[...] = kk_all
        b_s[...] = b_all
        rows = _hg_rows()
        r16 = lax.broadcasted_iota(I32, (SUB, LANE), 0)
        for h in range(H):
            cs = slice(h * LANE, (h + 1) * LANE)
            qf, kk, b, v = qf_all[:, cs], kk_all[:, cs], b_all[:, cs], v_ref[:, cs]
            do = do_s[:, cs]
            st_in, dstv = st_ref[h], dst[h]
            bl = b[CHUNK - 1:CHUNK, :]
            eb, ebl, el = jnp.exp(b), jnp.exp(bl - b), jnp.exp(bl)
            qe, ke = qf * eb, kk * ebl
            vb, dob, stb, dstb = v.astype(BF16), do.astype(BF16), st_in.astype(BF16), dstv.astype(BF16)
            w_ = _dot(vb, dstb, NN)
            dqf = eb * _dot(dob, stb, NN)
            dkk = ebl * w_
            dv = _dot(ke.astype(BF16), dstb, NT)
            dbl = el * jnp.sum(st_in * dstv, axis=0, keepdims=True) + jnp.sum(ke * w_, axis=0, keepdims=True)
            dst[h] = dstv * el + _dot(dob, qe.astype(BF16), TN)
            poff, parts = _hg_below(qf, kk, b, rows)
            dpoff = _dot(dob, vb, NT).astype(BF16)
            dv = dv + _dot(poff.astype(BF16), dob, TN)
            dq_blocks = [jnp.zeros((SUB, LANE), F32)]
            for i, (qi, ki, eq, ek) in enumerate(parts, start=1):
                dpi = dpoff[SUB * i:SUB * (i + 1), :]
                dq_blocks.append(_dot(dpi, ki, NN) * eq)
                dkk = dkk + _dot(dpi, qi, TN) * ek
            dqf = dqf + jnp.concatenate(dq_blocks, axis=0)
            dq_diag = []
            for i in range(CHUNK // SUB):
                rs = slice(SUB * i, SUB * (i + 1))
                acc = jnp.zeros((SUB, LANE), F32)
                for j in range(SUB):
                    row = pl.ds(SUB * i + j, 1)
                    ks = kk_s[row, cs]
                    e = jnp.exp(jnp.where(r16 >= j, b[rs] - b_s[row, cs], NEG))
                    x = jnp.sum(do[rs] * v_ref[row, cs], axis=1, keepdims=True) * e
                    acc = acc + x * ks
                    dkk_d[row, cs] = jnp.sum(x * qf[rs], axis=0, keepdims=True)
                    col = jnp.sum(qf[rs] * (ks * e), axis=1, keepdims=True)
                    dv_d[row, cs] = jnp.sum(col * do[rs], axis=0, keepdims=True)
                dq_diag.append(acc)
            dqf = dqf + jnp.concatenate(dq_diag, axis=0)
            dkk = dkk + dkk_d[:, cs]
            dv = dv + dv_d[:, cs]
            dv_ref[:, cs] = dv.astype(dv_ref.dtype)
            db = qf * dqf - kk * dkk
            db_s[:, cs] = db + jnp.where(rows == CHUNK - 1, dbl, 0.0)
            dkk_s[:, cs] = dkk
            dq_ref[:, cs] = (dqf * _dsilu(q_all[:, cs], sq[:, cs])).astype(dq_ref.dtype)
        ri = lax.broadcasted_iota(I32, (CHUNK, CHUNK), 0)
        ci = lax.broadcasted_iota(I32, (CHUNK, CHUNK), 1)
        dlogf = _dot((ci >= ri).astype(F32), db_s[...], NN, precision=lax.Precision.HIGHEST)
        dfg = dlogf / fg - dkk_s[...]
        df_ref[...] = (dfg * (1.0 - lbv) * sig * (1.0 - sig)).astype(df_ref.dtype)
        dlb_ref[...] += jnp.broadcast_to(jnp.sum(dfg * (1.0 - sig), axis=0, keepdims=True), dlb_ref.shape)

    def seg(off):
        return pl.BlockSpec((CHUNK, HG_W), lambda c: (NC - 1 - c, off // HG_W))

    blk = pl.BlockSpec((CHUNK, HG_W), lambda c: (NC - 1 - c, 0))
    full = pltpu.VMEM((CHUNK, HG_W), F32)
    dproj, dlb, dnw = pl.pallas_call(
        body, name="hgrn_bwd", grid=(NC,),
        in_specs=[seg(OFF_Q), seg(OFF_F), seg(OFF_V), seg(OFF_G),
                  pl.BlockSpec((1, HG_W), lambda c: (0, 0)), pl.BlockSpec((1, LANE), lambda c: (0, 0)),
                  pl.BlockSpec((None, H, LANE, LANE), lambda c: (NC - 1 - c, 0, 0, 0)), blk, blk, ANY_SPACE],
        out_specs=[pl.BlockSpec((CHUNK, 4 * HG_W), lambda c: (NC - 1 - c, 0)),
                   pl.BlockSpec((8, HG_W), lambda c: (0, 0)), pl.BlockSpec((8, LANE), lambda c: (0, 0))],
        out_shape=[jax.ShapeDtypeStruct(dproj.shape, dproj.dtype), jax.ShapeDtypeStruct((8, HG_W), F32),
                   jax.ShapeDtypeStruct((8, LANE), F32)],
        input_output_aliases={9: 0},
        scratch_shapes=[pltpu.VMEM((H, LANE, LANE), F32)] + [full] * 7,
        compiler_params=_cp(("arbitrary",)),
    )(proj, proj, proj, proj, lb.reshape(1, HG_W), nw.reshape(1, LANE), states, o_pre, dbcat, dproj)
    return dproj, dlb[0], dnw[0]


CV_PAD = 32
ROWS = 256


def _colblk(S, off):
    return pl.BlockSpec((S, LANE), lambda j: (0, off // LANE + j))


def cv_fwd(proj, w32, bias):
    S = proj.shape[0]
    nchunk = S // ROWS

    def body(a_ref, g_ref, w_ref, b_ref, o_ref, zpad):
        zpad[pl.ds(0, CV_PAD), :] = jnp.zeros((CV_PAD, LANE), F32)

        def glu(c, _):
            r0 = pl.multiple_of(c * ROWS, ROWS)
            zpad[pl.ds(CV_PAD + r0, ROWS), :] = a_ref[pl.ds(r0, ROWS), :] * _sigmoid(g_ref[pl.ds(r0, ROWS), :])
            return 0

        lax.fori_loop(0, nchunk, glu, 0)

        def conv(c, _):
            r0 = pl.multiple_of(c * ROWS, ROWS)
            acc = jnp.broadcast_to(b_ref[...], (ROWS, LANE))
            for j in range(CV_KERNEL):
                acc = acc + w_ref[pl.ds(j, 1), :] * zpad[pl.ds(r0 + (CV_PAD - CV_KERNEL + 1) + j, ROWS), :]
            o_ref[pl.ds(r0, ROWS), :] = acc
            return 0

        lax.fori_loop(0, nchunk, conv, 0)

    return pl.pallas_call(
        body, name="cv_fwd", grid=(BRANCH_W // LANE,),
        in_specs=[_colblk(S, OFF_CV), _colblk(S, OFF_CV + BRANCH_W),
                  pl.BlockSpec((32, LANE), lambda j: (0, j)), pl.BlockSpec((1, LANE), lambda j: (0, j))],
        out_specs=pl.BlockSpec((S, LANE), lambda j: (0, j)), out_shape=jax.ShapeDtypeStruct((S, BRANCH_W), F32),
        scratch_shapes=[pltpu.VMEM((CV_PAD + S, LANE), F32)],
        compiler_params=_cp(("parallel",)),
    )(proj, proj, w32, bias.reshape(1, BRANCH_W))


def cv_bwd(proj, w32, dzc):
    S = proj.shape[0]
    nchunk = S // ROWS

    def body(a_ref, g_ref, w_ref, dz_ref, da_ref, dg_ref, dw_ref, db_ref, zpad, dpad):
        zpad[pl.ds(0, CV_PAD), :] = jnp.zeros((CV_PAD, LANE), F32)
        dpad[pl.ds(S, CV_PAD), :] = jnp.zeros((CV_PAD, LANE), F32)
        dw_ref[...] = jnp.zeros_like(dw_ref)

        def glu(c, dsum):
            r0 = pl.multiple_of(c * ROWS, ROWS)
            zpad[pl.ds(CV_PAD + r0, ROWS), :] = a_ref[pl.ds(r0, ROWS), :] * _sigmoid(g_ref[pl.ds(r0, ROWS), :])
            d = dz_ref[pl.ds(r0, ROWS), :]
            dpad[pl.ds(r0, ROWS), :] = d
            return dsum + jnp.sum(d, axis=0, keepdims=True)

        dsum = lax.fori_loop(0, nchunk, glu, jnp.zeros((1, LANE), F32))
        db_ref[...] = jnp.broadcast_to(dsum, db_ref.shape)

        def conv(c, _):
            r0 = pl.multiple_of(c * ROWS, ROWS)
            d = dpad[pl.ds(r0, ROWS), :]
            acc = jnp.zeros((ROWS, LANE), F32)
            for j in range(CV_KERNEL):
                acc = acc + w_ref[pl.ds(j, 1), :] * dpad[pl.ds(r0 + (CV_KERNEL - 1) - j, ROWS), :]
                zs = zpad[pl.ds(r0 + (CV_PAD - CV_KERNEL + 1) + j, ROWS), :]
                dw_ref[pl.ds(j, 1), :] += jnp.sum(d * zs, axis=0, keepdims=True)
            a = a_ref[pl.ds(r0, ROWS), :]
            sg = _sigmoid(g_ref[pl.ds(r0, ROWS), :])
            da_ref[pl.ds(r0, ROWS), :] = (acc * sg).astype(da_ref.dtype)
            dg_ref[pl.ds(r0, ROWS), :] = (acc * a * sg * (1.0 - sg)).astype(dg_ref.dtype)
            return 0

        lax.fori_loop(0, nchunk, conv, 0)

    blk = pl.BlockSpec((S, LANE), lambda j: (0, j))
    da, dg, dw, db = pl.pallas_call(
        body, name="cv_bwd", grid=(BRANCH_W // LANE,),
        in_specs=[_colblk(S, OFF_CV), _colblk(S, OFF_CV + BRANCH_W), pl.BlockSpec((32, LANE), lambda j: (0, j)), blk],
        out_specs=[blk, blk, pl.BlockSpec((32, LANE), lambda j: (0, j)), pl.BlockSpec((8, LANE), lambda j: (0, j))],
        out_shape=[jax.ShapeDtypeStruct((S, BRANCH_W), BF16), jax.ShapeDtypeStruct((S, BRANCH_W), BF16),
                   jax.ShapeDtypeStruct((32, BRANCH_W), F32), jax.ShapeDtypeStruct((8, BRANCH_W), F32)],
        scratch_shapes=[pltpu.VMEM((CV_PAD + S, LANE), F32), pltpu.VMEM((S + CV_PAD, LANE), F32)],
        compiler_params=_cp(("parallel",)),
    )(proj, proj, w32, dzc)
    return da, dg, dw, db[0]


def ln_silu_fwd(z, w, b, bcat, tm=512):
    S, C = z.shape
    tm = _tile(S, tm)

    def body(z_ref, w_ref, b_ref, _, o_ref):
        zv = z_ref[...]
        mu = jnp.mean(zv, axis=-1, keepdims=True)
        zc = zv - mu
        rstd = lax.rsqrt(jnp.mean(zc * zc, axis=-1, keepdims=True) + EPS)
        y = zc * rstd * w_ref[...] + b_ref[...]
        o_ref[...] = (y * _sigmoid(y)).astype(o_ref.dtype)

    row = pl.BlockSpec((tm, C), lambda i: (i, 0))
    vec = pl.BlockSpec((1, C), lambda i: (0, 0))
    return pl.pallas_call(
        body, name="ln_silu_fwd", grid=(S // tm,), in_specs=[row, vec, vec, ANY_SPACE],
        out_specs=pl.BlockSpec((tm, C), lambda i: (i, 1)), out_shape=jax.ShapeDtypeStruct(bcat.shape, bcat.dtype),
        input_output_aliases={3: 0}, compiler_params=_cp(("parallel",)),
    )(z, w.reshape(1, C), b.reshape(1, C), bcat)


def ln_silu_bwd(z, w, b, dbcat, tm=512):
    S, C = z.shape
    tm = _tile(S, tm)

    def body(z_ref, w_ref, b_ref, do_ref, dz_ref, dw_ref, db_ref):
        zv = z_ref[...]
        wv = w_ref[...]
        mu = jnp.mean(zv, axis=-1, keepdims=True)
        zc = zv - mu
        rstd = lax.rsqrt(jnp.mean(zc * zc, axis=-1, keepdims=True) + EPS)
        xh = zc * rstd
        y = xh * wv + b_ref[...]
        dy = do_ref[...].astype(F32) * _dsilu(y, _sigmoid(y))

        @pl.when(pl.program_id(0) == 0)
        def _():
            dw_ref[...] = jnp.zeros_like(dw_ref)
            db_ref[...] = jnp.zeros_like(db_ref)

        dw_ref[...] += jnp.sum(dy * xh, axis=0, keepdims=True)
        db_ref[...] += jnp.sum(dy, axis=0, keepdims=True)
        dxh = dy * wv
        dz_ref[...] = rstd * (dxh - jnp.mean(dxh, axis=-1, keepdims=True) - xh * jnp.mean(dxh * xh, axis=-1, keepdims=True))

    row = pl.BlockSpec((tm, C), lambda i: (i, 0))
    vec = pl.BlockSpec((1, C), lambda i: (0, 0))
    dz, dw, db = pl.pallas_call(
        body, name="ln_silu_bwd", grid=(S // tm,), in_specs=[row, vec, vec, pl.BlockSpec((tm, C), lambda i: (i, 1))],
        out_specs=[row, vec, vec],
        out_shape=[jax.ShapeDtypeStruct((S, C), F32), jax.ShapeDtypeStruct((1, C), F32), jax.ShapeDtypeStruct((1, C), F32)],
        compiler_params=_cp(("arbitrary",)),
    )(z, w.reshape(1, C), b.reshape(1, C), dbcat)
    return dz, dw.reshape(C), db.reshape(C)


PL_PAD = 16


def _pool_counts(r0, win):
    t = r0 + lax.broadcasted_iota(I32, (ROWS, LANE), 0)
    return jnp.minimum(t + 1, win).astype(F32)


def pool_fwd(proj, wg, scale, bcat):
    S = proj.shape[0]
    nchunk = S // ROWS

    def body(u_ref, w_ref, s_ref, _, o_ref, upad):
        g = pl.program_id(0)
        upad[pl.ds(0, PL_PAD), :] = jnp.zeros((PL_PAD, LANE), F32)

        def fill(c, _):
            r0 = pl.multiple_of(c * ROWS, ROWS)
            upad[pl.ds(PL_PAD + r0, ROWS), :] = u_ref[pl.ds(r0, ROWS), :]
            return 0

        lax.fori_loop(0, nchunk, fill, 0)
        wb = w_ref[...].astype(BF16)
        for gi, win in enumerate(POOL_WINDOWS):
            @pl.when(g == gi)
            def _(win=win):
                def chunk(c, _):
                    r0 = pl.multiple_of(c * ROWS, ROWS)
                    u = upad[pl.ds(PL_PAD + r0, ROWS), :]
                    ws = u
                    for j in range(1, win):
                        ws = ws + upad[pl.ds(PL_PAD + r0 - j, ROWS), :]
                    pooled = ws / _pool_counts(r0, win) - u
                    o_ref[pl.ds(r0, ROWS), :] = (_dot(pooled.astype(BF16), wb, NN) * s_ref[...]).astype(o_ref.dtype)
                    return 0

                lax.fori_loop(0, nchunk, chunk, 0)

    return pl.pallas_call(
        body, name="pool_fwd", grid=(len(POOL_WINDOWS),),
        in_specs=[_colblk(S, OFF_PL), pl.BlockSpec((None, LANE, LANE), lambda j: (j, 0, 0)), pl.BlockSpec((1, LANE), lambda j: (0, j)),
                  ANY_SPACE],
        out_specs=pl.BlockSpec((S, LANE), lambda j: (0, 2 * BRANCH_W // LANE + j)),
        out_shape=jax.ShapeDtypeStruct(bcat.shape, bcat.dtype), input_output_aliases={3: 0},
        scratch_shapes=[pltpu.VMEM((PL_PAD + S, LANE), F32)],
        compiler_params=_cp(("parallel",)),
    )(proj, wg, scale.reshape(1, BRANCH_W), bcat)


def pool_bwd(proj, wg, scale, dbcat, dproj):
    S = proj.shape[0]
    nchunk = S // ROWS

    def body(u_ref, w_ref, s_ref, dy_ref, _, du_ref, dw_ref, ds_ref, upad, dpn, nd):
        g = pl.program_id(0)
        upad[pl.ds(0, PL_PAD), :] = jnp.zeros((PL_PAD, LANE), F32)
        dpn[pl.ds(S, PL_PAD), :] = jnp.zeros((PL_PAD, LANE), F32)

        def fill(c, _):
            r0 = pl.multiple_of(c * ROWS, ROWS)
            upad[pl.ds(PL_PAD + r0, ROWS), :] = u_ref[pl.ds(r0, ROWS), :]
            return 0

        lax.fori_loop(0, nchunk, fill, 0)
        wb = w_ref[...].astype(BF16)
        sv = s_ref[...]
        for gi, win in enumerate(POOL_WINDOWS):
            @pl.when(g == gi)
            def _(win=win):
                def chunk(c, carry):
                    dw, dsc = carry
                    r0 = pl.multiple_of(c * ROWS, ROWS)
                    u = upad[pl.ds(PL_PAD + r0, ROWS), :]
                    ws = u
                    for j in range(1, win):
                        ws = ws + upad[pl.ds(PL_PAD + r0 - j, ROWS), :]
                    cnt = _pool_counts(r0, win)
                    pooled = (ws / cnt - u).astype(BF16)
                    dyv = dy_ref[pl.ds(r0, ROWS), :].astype(F32)
                    dsc = dsc + jnp.sum(dyv * _dot(pooled, wb, NN), axis=0, keepdims=True)
                    dys = (dyv * sv).astype(BF16)
                    dw = dw + _dot(pooled, dys, TN)
                    dp = _dot(dys, wb, NT)
                    dpn[pl.ds(r0, ROWS), :] = dp / cnt
                    nd[pl.ds(r0, ROWS), :] = -dp
                    return dw, dsc

                dw, dsc = lax.fori_loop(0, nchunk, chunk, (jnp.zeros((LANE, LANE), F32), jnp.zeros((1, LANE), F32)))
                dw_ref[...] = dw
                ds_ref[...] = jnp.broadcast_to(dsc, ds_ref.shape)

                def spread(c, _):
                    r0 = pl.multiple_of(c * ROWS, ROWS)
                    acc = nd[pl.ds(r0, ROWS), :]
                    for j in range(win):
                        acc = acc + dpn[pl.ds(r0 + j, ROWS), :]
                    du_ref[pl.ds(r0, ROWS), :] = acc.astype(du_ref.dtype)
                    return 0

                lax.fori_loop(0, nchunk, spread, 0)

    dproj, dw, ds = pl.pallas_call(
        body, name="pool_bwd", grid=(len(POOL_WINDOWS),),
        in_specs=[_colblk(S, OFF_PL), pl.BlockSpec((None, LANE, LANE), lambda j: (j, 0, 0)), pl.BlockSpec((1, LANE), lambda j: (0, j)),
                  pl.BlockSpec((S, LANE), lambda j: (0, 2 * BRANCH_W // LANE + j)), ANY_SPACE],
        out_specs=[_colblk(S, OFF_PL), pl.BlockSpec((None, LANE, LANE), lambda j: (j, 0, 0)), pl.BlockSpec((8, LANE), lambda j: (0, j))],
        out_shape=[jax.ShapeDtypeStruct(dproj.shape, dproj.dtype), jax.ShapeDtypeStruct((len(POOL_WINDOWS), LANE, LANE), F32),
                   jax.ShapeDtypeStruct((8, BRANCH_W), F32)],
        input_output_aliases={4: 0},
        scratch_shapes=[pltpu.VMEM((PL_PAD + S, LANE), F32), pltpu.VMEM((S + PL_PAD, LANE), F32), pltpu.VMEM((S, LANE), F32)],
        compiler_params=_cp(("parallel",)),
    )(proj, wg, scale.reshape(1, BRANCH_W), dbcat, dproj)
    return dproj, dw, ds[0]


LR_PAD = 8
SCAN_TILES = 4
GELU_C = math.sqrt(2.0 / math.pi)
GELU_A = 0.044715


def _gelu(y):
    return 0.5 * y * (1.0 + jnp.tanh(GELU_C * (y + GELU_A * y * y * y)))


def _dgelu(y):
    t = jnp.tanh(GELU_C * (y + GELU_A * y * y * y))
    return 0.5 * (1.0 + t) + 0.5 * y * (1.0 - t * t) * GELU_C * (1.0 + 3.0 * GELU_A * y * y)


def _lru_gates(xpad, r0, cw_ref, cb, wa, ba, wx, bx, sp8):
    xc = jnp.broadcast_to(cb, (ROWS, LANE))
    for j in range(LRU_CONV):
        xc = xc + cw_ref[pl.ds(j, 1), :] * xpad[pl.ds(r0 + (LR_PAD - LRU_CONV + 1) + j, ROWS), :]
    xb = xc.astype(BF16)
    r = _sigmoid(_dot(xb, wa, NN) + ba)
    ig = _sigmoid(_dot(xb, wx, NN) + bx)
    la = -sp8 * r
    a = jnp.exp(la)
    s = jnp.sqrt(-jnp.tanh(la) * (a * a + 1.0))
    return xc, r, ig, a, s


def _tile_scan(a, b, r8, up):
    for s in (1, 2, 4):
        keep = (r8 < 8 - s) if up else (r8 >= s)
        shift = 8 - s if up else s
        a_sh = jnp.where(keep, pltpu.roll(a, shift, 0), 1.0)
        b_sh = jnp.where(keep, pltpu.roll(b, shift, 0), 0.0)
        b = b + a * b_sh
        a = a * a_sh
    return a, b


def lru_fwd(proj, cw8, cb, wa_bd, ba, wx_bd, bx, sp8, bcat):
    S = proj.shape[0]
    nchunk = S // ROWS

    def body(x_ref, y_ref, cw_ref, cb_ref, wa_ref, ba_ref, wx_ref, bx_ref, sp_ref, _, o_ref, h_ref, xpad, a_s):
        xpad[pl.ds(0, LR_PAD), :] = jnp.zeros((LR_PAD, LANE), F32)

        def fill(c, _):
            r0 = pl.multiple_of(c * ROWS, ROWS)
            xpad[pl.ds(LR_PAD + r0, ROWS), :] = x_ref[pl.ds(r0, ROWS), :]
            return 0

        lax.fori_loop(0, nchunk, fill, 0)
        wa = wa_ref[...].astype(BF16)
        wx = wx_ref[...].astype(BF16)

        def gates(c, _):
            r0 = pl.multiple_of(c * ROWS, ROWS)
            xc, r, ig, a, s = _lru_gates(xpad, r0, cw_ref, cb_ref[...], wa, ba_ref[...], wx, bx_ref[...], sp_ref[...])
            a_s[pl.ds(r0, ROWS), :] = a
            h_ref[pl.ds(r0, ROWS), :] = s * (ig * xc)
            return 0

        lax.fori_loop(0, nchunk, gates, 0)

        r8 = lax.broadcasted_iota(I32, (8, LANE), 0)

        def scan(i, h):
            bases = [pl.multiple_of(i * (8 * SCAN_TILES) + 8 * j, 8) for j in range(SCAN_TILES)]
            maps = [_tile_scan(a_s[pl.ds(b, 8), :], h_ref[pl.ds(b, 8), :], r8, False) for b in bases]
            for b, (ca, cb_) in zip(bases, maps):
                out = cb_ + ca * h
                h_ref[pl.ds(b, 8), :] = out
                h = out[7:8, :]
            return h

        lax.fori_loop(0, S // (8 * SCAN_TILES), scan, jnp.zeros((1, LANE), F32))

        def gate_out(c, _):
            r0 = pl.multiple_of(c * ROWS, ROWS)
            o_ref[pl.ds(r0, ROWS), :] = (h_ref[pl.ds(r0, ROWS), :] * _gelu(y_ref[pl.ds(r0, ROWS), :])).astype(o_ref.dtype)
            return 0

        lax.fori_loop(0, nchunk, gate_out, 0)

    vec = pl.BlockSpec((1, LANE), lambda j: (0, j))
    mat = pl.BlockSpec((None, LANE, LANE), lambda j: (j, 0, 0))
    blk = pl.BlockSpec((S, LANE), lambda j: (0, j))
    return pl.pallas_call(
        body, name="lru_fwd", grid=(BRANCH_W // LANE,),
        in_specs=[_colblk(S, OFF_LX), _colblk(S, OFF_LY), pl.BlockSpec((8, LANE), lambda j: (0, j)), vec, mat, vec, mat, vec, vec,
                  ANY_SPACE],
        out_specs=[pl.BlockSpec((S, LANE), lambda j: (0, 3 * BRANCH_W // LANE + j)), blk],
        out_shape=[jax.ShapeDtypeStruct(bcat.shape, bcat.dtype), jax.ShapeDtypeStruct((S, BRANCH_W), F32)],
        input_output_aliases={9: 0},
        scratch_shapes=[pltpu.VMEM((LR_PAD + S, LANE), F32), pltpu.VMEM((S, LANE), F32)],
        compiler_params=_cp(("parallel",)),
    )(proj, proj, cw8, cb.reshape(1, -1), wa_bd, ba.reshape(1, -1), wx_bd, bx.reshape(1, -1), sp8.reshape(1, -1), bcat)


def lru_bwd(proj, cw8, cb, wa_bd, ba, wx_bd, bx, sp8, h, dbcat):
    S = proj.shape[0]
    nchunk = S // ROWS

    def body(x_ref, y_ref, cw_ref, cb_ref, wa_ref, ba_ref, wx_ref, bx_ref, sp_ref, h_ref, do_ref,
             dx_ref, dy_ref, dcw_ref, dcb_ref, dwa_ref, dba_ref, dwx_ref, dbx_ref, dsp_ref,
             xpad, a_s, g_s, hpad, dxc, xc_s, r_s, ig_s, s_s):
        xpad[pl.ds(0, LR_PAD), :] = jnp.zeros((LR_PAD, LANE), F32)
        hpad[pl.ds(0, LR_PAD), :] = jnp.zeros((LR_PAD, LANE), F32)
        dxc[pl.ds(S, LR_PAD), :] = jnp.zeros((LR_PAD, LANE), F32)
        dcw_ref[...] = jnp.zeros_like(dcw_ref)
        wa = wa_ref[...].astype(BF16)
        wx = wx_ref[...].astype(BF16)
        cbv, bav, bxv, spv = cb_ref[...], ba_ref[...], bx_ref[...], sp_ref[...]

        def fill(c, _):
            r0 = pl.multiple_of(c * ROWS, ROWS)
            xpad[pl.ds(LR_PAD + r0, ROWS), :] = x_ref[pl.ds(r0, ROWS), :]
            hv = h_ref[pl.ds(r0, ROWS), :]
            hpad[pl.ds(LR_PAD + r0, ROWS), :] = hv
            yv = y_ref[pl.ds(r0, ROWS), :]
            dov = do_ref[pl.ds(r0, ROWS), :].astype(F32)
            g_s[pl.ds(r0, ROWS), :] = dov * _gelu(yv)
            dy_ref[pl.ds(r0, ROWS), :] = (dov * hv * _dgelu(yv)).astype(dy_ref.dtype)
            return 0

        lax.fori_loop(0, nchunk, fill, 0)

        def gates(c, _):
            r0 = pl.multiple_of(c * ROWS, ROWS)
            rows = pl.ds(r0, ROWS)
            xc_s[rows, :], r_s[rows, :], ig_s[rows, :], a_s[rows, :], s_s[rows, :] = _lru_gates(
                xpad, r0, cw_ref, cbv, wa, bav, wx, bxv, spv)
            return 0

        lax.fori_loop(0, nchunk, gates, 0)

        r8 = lax.broadcasted_iota(I32, (8, LANE), 0)

        def rscan(i, carry):
            bases = [pl.multiple_of(S - 8 - i * (8 * SCAN_TILES) - 8 * j, 8) for j in range(SCAN_TILES)]
            firsts, maps = [], []
            for b in bases:
                a8 = a_s[pl.ds(b, 8), :]
                above = jnp.where(r8 < 7, pltpu.roll(a8, 7, 0), 1.0)
                firsts.append(a8[0:1, :])
                maps.append(_tile_scan(above, g_s[pl.ds(b, 8), :], r8, True))
            for b, a0, (ca, cb_) in zip(bases, firsts, maps):
                out = cb_ + ca * carry
                g_s[pl.ds(b, 8), :] = out
                carry = a0 * out[0:1, :]
            return carry

        lax.fori_loop(0, S // (8 * SCAN_TILES), rscan, jnp.zeros((1, LANE), F32))

        def chain(c, carry):
            dwa, dwx, dba, dbx, dsp, dcb = carry
            r0 = pl.multiple_of(c * ROWS, ROWS)
            rows = pl.ds(r0, ROWS)
            xc, r, ig, a, s = xc_s[rows, :], r_s[rows, :], ig_s[rows, :], a_s[rows, :], s_s[rows, :]
            gt = g_s[rows, :]
            hprev = hpad[pl.ds(r0 + LR_PAD - 1, ROWS), :]
            da = gt * hprev - gt * ig * xc * (a / s)
            dig = gt * s * xc
            dla = da * a
            dsp = dsp + jnp.sum(-dla * r, axis=0, keepdims=True)
            dpr = (-dla * spv) * r * (1.0 - r)
            dpi = dig * ig * (1.0 - ig)
            dprb, dpib, xb = dpr.astype(BF16), dpi.astype(BF16), xc.astype(BF16)
            d = gt * s * ig + _dot(dprb, wa, NT) + _dot(dpib, wx, NT)
            dwa = dwa + _dot(xb, dprb, TN)
            dwx = dwx + _dot(xb, dpib, TN)
            dba = dba + jnp.sum(dpr, axis=0, keepdims=True)
            dbx = dbx + jnp.sum(dpi, axis=0, keepdims=True)
            dcb = dcb + jnp.sum(d, axis=0, keepdims=True)
            dxc[pl.ds(r0, ROWS), :] = d
            for j in range(LRU_CONV):
                xs = xpad[pl.ds(r0 + (LR_PAD - LRU_CONV + 1) + j, ROWS), :]
                dcw_ref[pl.ds(j, 1), :] += jnp.sum(d * xs, axis=0, keepdims=True)
            return dwa, dwx, dba, dbx, dsp, dcb

        zm, zv = jnp.zeros((LANE, LANE), F32), jnp.zeros((1, LANE), F32)
        dwa, dwx, dba, dbx, dsp, dcb = lax.fori_loop(0, nchunk, chain, (zm, zm, zv, zv, zv, zv))
        dwa_ref[...] = dwa
        dwx_ref[...] = dwx
        dba_ref[...] = jnp.broadcast_to(dba, dba_ref.shape)
        dbx_ref[...] = jnp.broadcast_to(dbx, dbx_ref.shape)
        dsp_ref[...] = jnp.broadcast_to(dsp, dsp_ref.shape)
        dcb_ref[...] = jnp.broadcast_to(dcb, dcb_ref.shape)

        def convt(c, _):
            r0 = pl.multiple_of(c * ROWS, ROWS)
            acc = jnp.zeros((ROWS, LANE), F32)
            for j in range(LRU_CONV):
                acc = acc + cw_ref[pl.ds(j, 1), :] * dxc[pl.ds(r0 + (LRU_CONV - 1) - j, ROWS), :]
            dx_ref[pl.ds(r0, ROWS), :] = acc.astype(dx_ref.dtype)
            return 0

        lax.fori_loop(0, nchunk, convt, 0)

    vec = pl.BlockSpec((1, LANE), lambda j: (0, j))
    vec8 = pl.BlockSpec((8, LANE), lambda j: (0, j))
    mat = pl.BlockSpec((None, LANE, LANE), lambda j: (j, 0, 0))
    blk = pl.BlockSpec((S, LANE), lambda j: (0, j))
    nblk = BRANCH_W // LANE
    v8 = jax.ShapeDtypeStruct((8, BRANCH_W), F32)
    m4 = jax.ShapeDtypeStruct((nblk, LANE, LANE), F32)
    big = jax.ShapeDtypeStruct((S, BRANCH_W), BF16)
    seq = pltpu.VMEM((S, LANE), F32)
    dx, dy, dcw, dcb, dwa, dba, dwx, dbx, dsp = pl.pallas_call(
        body, name="lru_bwd", grid=(nblk,),
        in_specs=[_colblk(S, OFF_LX), _colblk(S, OFF_LY), vec8, vec, mat, vec, mat, vec, vec, blk,
                  pl.BlockSpec((S, LANE), lambda j: (0, 3 * BRANCH_W // LANE + j))],
        out_specs=[blk, blk, vec8, vec8, mat, vec8, mat, vec8, vec8],
        out_shape=[big, big, v8, v8, m4, v8, m4, v8, v8],
        scratch_shapes=[pltpu.VMEM((LR_PAD + S, LANE), F32), seq, seq, pltpu.VMEM((LR_PAD + S, LANE), F32),
                        pltpu.VMEM((S + LR_PAD, LANE), F32), seq, seq, seq, seq],
        compiler_params=_cp(("parallel",)),
    )(proj, proj, cw8, cb.reshape(1, -1), wa_bd, ba.reshape(1, -1), wx_bd, bx.reshape(1, -1), sp8.reshape(1, -1), h, dbcat)
    return dx, dy, dcw, dcb[0], dwa, dba[0], dwx, dbx[0], dsp[0]


MG_COLS = 512
N_BRANCH = 4
BCAT_W = N_BRANCH * BRANCH_W


def merge_fwd(bcat, wb, proj, gate_b, tm=2048, after=()):
    S = proj.shape[0]
    tm = _tile(S, tm)
    halves = D_MODEL // MG_COLS

    def body(a_ref, w_ref, g_ref, gb_ref, *rest):
        up_ref, sg_ref, o_ref, acc_ref = rest[len(after):]
        k = pl.program_id(2)
        up = _dot(a_ref[...], w_ref[...], NN)
        up_ref[...] = up.astype(up_ref.dtype)
        sg = _sigmoid(g_ref[...] + gb_ref[pl.ds(k, 1), :])
        sg_ref[...] = sg.astype(sg_ref.dtype)
        term = sg * up

        @pl.when(k == 0)
        def _():
            acc_ref[...] = term

        @pl.when(k > 0)
        def _():
            acc_ref[...] += term

        @pl.when(k == N_BRANCH - 1)
        def _():
            o_ref[...] = acc_ref[...].astype(o_ref.dtype)

    return pl.pallas_call(
        body, name="merge_fwd", grid=(S // tm, halves, N_BRANCH),
        in_specs=[pl.BlockSpec((tm, BRANCH_W), lambda i, j, k: (i, k)),
                  pl.BlockSpec((None, BRANCH_W, MG_COLS), lambda i, j, k: (k, 0, j)),
                  pl.BlockSpec((tm, MG_COLS), lambda i, j, k: (i, OFF_GATE // MG_COLS + k * halves + j)),
                  pl.BlockSpec((N_BRANCH, MG_COLS), lambda i, j, k: (0, j))] + [ANY_SPACE] * len(after),
        out_specs=[pl.BlockSpec((tm, MG_COLS), lambda i, j, k: (i, k * halves + j)),
                   pl.BlockSpec((tm, MG_COLS), lambda i, j, k: (i, k * halves + j)),
                   pl.BlockSpec((tm, MG_COLS), lambda i, j, k: (i, j))],
        out_shape=[jax.ShapeDtypeStruct((S, N_BRANCH * D_MODEL), BF16), jax.ShapeDtypeStruct((S, N_BRANCH * D_MODEL), BF16),
                   jax.ShapeDtypeStruct((S, D_MODEL), BF16)],
        scratch_shapes=[pltpu.VMEM((tm, MG_COLS), F32)],
        compiler_params=_cp(("parallel", "parallel", "arbitrary")),
    )(bcat, wb, proj, gate_b, *after)


def merge_bwd(dmerged, ups, gates, tm=1024):
    S = dmerged.shape[0]
    tm = _tile(S, tm)
    halves = D_MODEL // MG_COLS

    def body(dm_ref, u_ref, sg_ref, du_ref, dg_ref, dgb_ref):
        @pl.when(pl.program_id(2) == 0)
        def _():
            dgb_ref[...] = jnp.zeros_like(dgb_ref)

        dm = dm_ref[...].astype(F32)
        sg = sg_ref[...].astype(F32)
        du_ref[...] = (dm * sg).astype(du_ref.dtype)
        dgk = dm * u_ref[...] * sg * (1.0 - sg)
        dg_ref[...] = dgk.astype(dg_ref.dtype)
        dgb_ref[...] += jnp.broadcast_to(jnp.sum(dgk, axis=0, keepdims=True), dgb_ref.shape)

    dups, dproj, dgb = pl.pallas_call(
        body, name="merge_bwd", grid=(N_BRANCH, halves, S // tm),
        in_specs=[pl.BlockSpec((tm, MG_COLS), lambda k, j, i: (i, j)),
                  pl.BlockSpec((tm, MG_COLS), lambda k, j, i: (i, k * halves + j)),
                  pl.BlockSpec((tm, MG_COLS), lambda k, j, i: (i, k * halves + j))],
        out_specs=[pl.BlockSpec((tm, MG_COLS), lambda k, j, i: (i, k * halves + j)),
                   pl.BlockSpec((tm, MG_COLS), lambda k, j, i: (i, OFF_GATE // MG_COLS + k * halves + j)),
                   pl.BlockSpec((8, MG_COLS), lambda k, j, i: (k, j))],
        out_shape=[jax.ShapeDtypeStruct((S, N_BRANCH * D_MODEL), BF16), jax.ShapeDtypeStruct((S, IN_W), BF16),
                   jax.ShapeDtypeStruct((8 * N_BRANCH, D_MODEL), F32)],
        compiler_params=_cp(("parallel", "parallel", "arbitrary")),
    )(dmerged, ups, gates)
    return dups, dproj, dgb.reshape(N_BRANCH, 8, D_MODEL)[:, 0]


def attn_fwd(q, kv, tm=512):
    S = q.shape[0]
    M = kv.shape[0]
    tm = _tile(S, tm)
    scale = XA_HD ** -0.5

    def body(q_ref, kv_ref, o_ref):
        for hh in range(XA_HEADS):
            cs = pl.ds(hh * XA_HD, XA_HD)
            qh = q_ref[:, cs]
            kh = kv_ref[:, cs]
            vh = kv_ref[:, pl.ds(D_MODEL + hh * XA_HD, XA_HD)]
            s = _dot(qh, kh, NT) * scale
            p = jnp.exp(s - jnp.max(s, axis=-1, keepdims=True))
            p = p / jnp.sum(p, axis=-1, keepdims=True)
            o_ref[:, cs] = _dot(p.astype(BF16), vh, NN).astype(o_ref.dtype)

    return pl.pallas_call(
        body, name="attn_fwd", grid=(S // tm,),
        in_specs=[pl.BlockSpec((tm, D_MODEL), lambda i: (i, 0)), pl.BlockSpec((M, 2 * D_MODEL), lambda i: (0, 0))],
        out_specs=pl.BlockSpec((tm, D_MODEL), lambda i: (i, 0)), out_shape=jax.ShapeDtypeStruct((S, D_MODEL), BF16),
        compiler_params=_cp(("parallel",)),
    )(q, kv)


def attn_bwd(q, kv, do, tm=512):
    S = q.shape[0]
    M = kv.shape[0]
    tm = _tile(S, tm)
    scale = XA_HD ** -0.5

    def body(q_ref, kv_ref, do_ref, dq_ref, dkv_ref):
        @pl.when(pl.program_id(0) == 0)
        def _():
            dkv_ref[...] = jnp.zeros_like(dkv_ref)

        for hh in range(XA_HEADS):
            cs = pl.ds(hh * XA_HD, XA_HD)
            vs = pl.ds(D_MODEL + hh * XA_HD, XA_HD)
            qh = q_ref[:, cs]
            kh = kv_ref[:, cs]
            vh = kv_ref[:, vs]
            doh = do_ref[:, cs]
            s = _dot(qh, kh, NT) * scale
            p = jnp.exp(s - jnp.max(s, axis=-1, keepdims=True))
            p = p / jnp.sum(p, axis=-1, keepdims=True)
            dp = _dot(doh, vh, NT)
            ds = (p * (dp - jnp.sum(dp * p, axis=-1, keepdims=True)) * scale).astype(BF16)
            dq_ref[:, cs] = _dot(ds, kh, NN).astype(dq_ref.dtype)
            dkv_ref[:, cs] += _dot(ds, qh, TN)
            dkv_ref[:, vs] += _dot(p.astype(BF16), doh, TN)

    row = pl.BlockSpec((tm, D_MODEL), lambda i: (i, 0))
    full = pl.BlockSpec((M, 2 * D_MODEL), lambda i: (0, 0))
    return pl.pallas_call(
        body, name="attn_bwd", grid=(S // tm,), in_specs=[row, full, row], out_specs=[row, full],
        out_shape=[jax.ShapeDtypeStruct((S, D_MODEL), BF16), jax.ShapeDtypeStruct((M, 2 * D_MODEL), F32)],
        compiler_params=_cp(("arbitrary",)),
    )(q, kv, do)


def sum_parts(parts, own=None, tm=256):
    n, R, C = parts.shape
    tm = _tile(R, tm)
    has_own = own is not None

    def body(*refs):
        p_ref, o_ref = refs[0], refs[-1]
        acc = refs[1][...].astype(F32) if has_own else p_ref[0].astype(F32)
        for j in range(0 if has_own else 1, n):
            acc = acc + p_ref[j].astype(F32)
        o_ref[...] = acc

    row = pl.BlockSpec((tm, C), lambda i: (i, 0))
    return pl.pallas_call(
        body, name="sum_parts", grid=(R // tm,),
        in_specs=[pl.BlockSpec((n, tm, C), lambda i: (0, i, 0))] + ([row] if has_own else []), out_specs=row,
        out_shape=jax.ShapeDtypeStruct((R, C), F32), compiler_params=_cp(("parallel",)),
    )(*([parts, own] if has_own else [parts]))


def adamw(w, g, m, v, tm=256):
    R, C = w.shape
    tm = _tile(R, tm)
    c1 = 1.0 / (1.0 - ADAM_B1 ** ADAM_STEP)
    c2 = 1.0 / (1.0 - ADAM_B2 ** ADAM_STEP)

    def body(w_ref, g_ref, m_ref, v_ref, d_ref, nm_ref, nv_ref):
        gv = g_ref[...]
        nm = ADAM_B1 * m_ref[...] + (1.0 - ADAM_B1) * gv
        nv = ADAM_B2 * v_ref[...] + (1.0 - ADAM_B2) * (gv * gv)
        nm_ref[...] = nm
        nv_ref[...] = nv
        d_ref[...] = -ADAM_LR * ((nm * c1) / (jnp.sqrt(nv * c2) + ADAM_EPS) + ADAM_WD * w_ref[...])

    blk = pl.BlockSpec((tm, C), lambda i: (i, 0))
    sd = jax.ShapeDtypeStruct((R, C), F32)
    return pl.pallas_call(
        body, name="adamw", grid=(R // tm,), in_specs=[blk] * 4, out_specs=[blk] * 3, out_shape=[sd] * 3,
        compiler_params=_cp(("parallel",)),
    )(w, g, m, v)


ANY = pl.BlockSpec(memory_space=pl.ANY)


def _place():
    return lax.axis_index("x"), lax.axis_index("y"), lax.axis_index("c")


def _slot(px, py, pc):
    return 4 * px + 2 * py + pc


def all_gather(name, shards, after=()):
    n = len(shards)
    n_in = n + len(after)

    def body(*refs):
        x_refs, out_refs = refs[:n], refs[n_in:n_in + n]
        send_sems, recv_sems, local_sems = refs[n_in + n:]
        x, y, c = _place()
        me, sibling = (x, y, c), (x, y, 1 - c)
        chips = [(1 - x, y), (x, 1 - y), (1 - x, 1 - y)]

        def copy(a, k, block, to, src=None):
            rows = out_refs[a].at[_slot(*block)]
            return pltpu.make_async_remote_copy(
                src_ref=rows if src is None else src, dst_ref=rows,
                send_sem=send_sems.at[7 * a + k], recv_sem=recv_sems.at[7 * a + k],
                device_id=to, device_id_type=MESH)

        mine = [pltpu.make_async_copy(x_refs[a], out_refs[a].at[_slot(*me)], local_sems.at[a]) for a in range(n)]
        for cp in mine:
            cp.start()
        first = []
        for a in range(n):
            first.append(copy(a, 0, me, sibling, src=x_refs[a]))
            first += [copy(a, 1 + j, me, (*chip, c), src=x_refs[a]) for j, chip in enumerate(chips)]
        for cp in first:
            cp.start()
        passed = []
        for a in range(n):
            for j, chip in enumerate(chips):
                copy(a, 1 + j, (*chip, c), me).wait_recv()
                cp = copy(a, 4 + j, (*chip, c), sibling)
                cp.start()
                passed.append(cp)
        for a in range(n):
            copy(a, 0, sibling, me).wait_recv()
            for j, chip in enumerate(chips):
                copy(a, 4 + j, (*chip, 1 - c), me).wait_recv()
        for cp in first + passed:
            cp.wait_send()
        for cp in mine:
            cp.wait()

    return pl.pallas_call(
        body, name=name, in_specs=[ANY] * n_in, out_specs=[ANY] * n,
        out_shape=[jax.ShapeDtypeStruct((N_DEV, *s.shape), s.dtype) for s in shards],
        scratch_shapes=[pltpu.SemaphoreType.DMA((7 * n,)), pltpu.SemaphoreType.DMA((7 * n,)), pltpu.SemaphoreType.DMA((n,))],
    )(*shards, *after)


HBM = pl.BlockSpec(memory_space=pltpu.HBM)
SEM = pl.BlockSpec(memory_space=pltpu.SEMAPHORE)
EFFECT = pltpu.SideEffectType.DATAFLOW_SIDE_EFFECTING
N_PEER = N_DEV - 1
RELATIONS = [(dx, dy, dc) for dx in (0, 1) for dy in (0, 1) for dc in (0, 1)][1:]


def _peer(place, rel):
    return tuple(1 - v if d else v for v, d in zip(place, rel))


def gather_start(name, shards, me, before):
    n = len(shards)

    def body(*refs):
        x_refs, land_refs = refs[:n], refs[n:2 * n]
        send_sems, recv_sems = refs[2 * n + len(before):2 * n + len(before) + 2]
        token = refs[-1]
        place = _place()
        mine = _slot(*place)
        for a in range(n):
            for rel in RELATIONS:
                pltpu.make_async_remote_copy(
                    src_ref=x_refs[a], dst_ref=land_refs[a].at[mine], send_sem=send_sems.at[a], recv_sem=recv_sems.at[a],
                    device_id=_peer(place, rel), device_id_type=MESH).start()
        token[...] = jnp.zeros_like(token)

    lands = [lax.dynamic_update_index_in_dim(lax.empty((N_DEV, *s.shape), s.dtype), s, me, 0) for s in shards]
    outs = pl.pallas_call(
        body, name=name,
        in_specs=[HBM] * (2 * n) + [ANY] * len(before),
        out_specs=[SEM, SEM] + [HBM] * (2 * n) + [pl.BlockSpec(memory_space=pltpu.VMEM)],
        out_shape=[pltpu.SemaphoreType.DMA((n,)), pltpu.SemaphoreType.DMA((n,))]
        + [pltpu.HBM(t.shape, t.dtype) for t in (*shards, *lands)] + [jax.ShapeDtypeStruct((8, LANE), F32)],
        input_output_aliases={i: 2 + i for i in range(2 * n)},
        compiler_params=pltpu.CompilerParams(has_side_effects=EFFECT),
    )(*[pltpu.with_memory_space_constraint(t, pltpu.HBM) for t in (*shards, *lands)], *before)
    return (outs[0], outs[1], outs[2:2 + n], outs[2 + n:2 + 2 * n]), outs[-1]


def gather_wait(name, state, after):
    send_sems, recv_sems, shards, lands = state
    n = len(shards)

    def body(*refs):
        land_refs = refs[n:2 * n]
        s_sems, r_sems = refs[2 * n:2 * n + 2]
        place = _place()
        for a in range(n):
            seven = land_refs[a].at[pl.ds(0, N_PEER)]
            cp = pltpu.make_async_remote_copy(
                src_ref=seven, dst_ref=seven, send_sem=s_sems.at[a], recv_sem=r_sems.at[a], device_id=place, device_id_type=MESH)
            cp.wait_send()
            cp.wait_recv()

    outs = pl.pallas_call(
        body, name=name,
        in_specs=[HBM] * (2 * n) + [SEM, SEM] + [ANY] * len(after), out_specs=[HBM] * (2 * n),
        out_shape=[pltpu.HBM(t.shape, t.dtype) for t in (*shards, *lands)],
        input_output_aliases={i: i for i in range(2 * n)},
        compiler_params=pltpu.CompilerParams(has_side_effects=EFFECT),
    )(*shards, *lands, send_sems, recv_sems, *after)
    return outs[n:]


def exchange_start(name, grads, before):
    n = len(grads)

    def body(*refs):
        g_refs, land_refs = refs[:n], refs[n:2 * n]
        send_sems, recv_sems = refs[2 * n + len(before):2 * n + len(before) + 2]
        token = refs[-1]
        place = _place()
        for a in range(n):
            for r, rel in enumerate(RELATIONS):
                p = _peer(place, rel)
                pltpu.make_async_remote_copy(
                    src_ref=g_refs[a].at[_slot(*p)], dst_ref=land_refs[a].at[r],
                    send_sem=send_sems.at[a], recv_sem=recv_sems.at[a], device_id=p, device_id_type=MESH).start()
        token[...] = jnp.zeros_like(token)

    lands = [lax.empty((N_PEER, *g.shape[1:]), g.dtype) for g in grads]
    outs = pl.pallas_call(
        body, name=name,
        in_specs=[HBM] * (2 * n) + [ANY] * len(before),
        out_specs=[SEM, SEM] + [HBM] * (2 * n) + [pl.BlockSpec(memory_space=pltpu.VMEM)],
        out_shape=[pltpu.SemaphoreType.DMA((n,)), pltpu.SemaphoreType.DMA((n,))]
        + [pltpu.HBM(g.shape, g.dtype) for g in grads] + [pltpu.HBM(t.shape, t.dtype) for t in lands]
        + [jax.ShapeDtypeStruct((8, LANE), F32)],
        input_output_aliases={i: 2 + i for i in range(2 * n)},
        compiler_params=pltpu.CompilerParams(has_side_effects=EFFECT),
    )(*[pltpu.with_memory_space_constraint(t, pltpu.HBM) for t in (*grads, *lands)], *before)
    return (outs[0], outs[1], outs[2:2 + n], outs[2 + n:2 + 2 * n]), outs[-1]


def exchange_wait(name, state, after):
    send_sems, recv_sems, grads, lands = state
    n = len(grads)

    def body(*refs):
        g_refs, land_refs = refs[:n], refs[n:2 * n]
        s_sems, r_sems = refs[2 * n:2 * n + 2]
        place = _place()
        for a in range(n):
            cp = pltpu.make_async_remote_copy(
                src_ref=g_refs[a].at[pl.ds(0, N_PEER)], dst_ref=land_refs[a],
                send_sem=s_sems.at[a], recv_sem=r_sems.at[a], device_id=place, device_id_type=MESH)
            cp.wait_send()
            cp.wait_recv()

    outs = pl.pallas_call(
        body, name=name,
        in_specs=[HBM] * (2 * n) + [SEM, SEM, ANY], out_specs=[HBM] * (2 * n),
        out_shape=[pltpu.HBM(t.shape, t.dtype) for t in (*grads, *lands)],
        input_output_aliases={i: i for i in range(2 * n)},
        compiler_params=pltpu.CompilerParams(has_side_effects=EFFECT),
    )(*grads, *lands, send_sems, recv_sems, after)
    return outs[:n], outs[n:]


WEIGHTS = ['norm_mix_w', 'w_in', 'hg_lb_raw', 'hg_norm_w', 'cv_dw_w', 'cv_dw_b', 'cv_ln_w', 'cv_ln_b', 'pl_w', 'pl_scale',
           'lru_conv_w', 'lru_conv_b', 'lru_wa', 'lru_ba', 'lru_wx', 'lru_bx', 'lru_lambda', 'gate_b', 'w_branch', 'w_out',
           'norm_mem_w', 'mem_norm_w', 'xa_wq', 'xa_wkv', 'xa_wo', 'norm_ffn_w', 'ffn_w1', 'ffn_w2', 'final_norm_w']
BIG = ('w_in', 'w_branch', 'w_out', 'xa_wq', 'xa_wkv', 'xa_wo', 'ffn_w1', 'ffn_w2')
SMALL_SHARDED = ('cv_dw_w', 'lru_conv_w', 'gate_b')
SMALL = tuple(n for n in WEIGHTS if n not in BIG and n not in SMALL_SHARDED)
PACK_ROWS = 256


def _pack(arrs):
    flat = jnp.concatenate([a.reshape(-1).astype(F32) for a in arrs])
    tile = PACK_ROWS * LANE
    padded = -(-flat.shape[0] // tile) * tile
    return jnp.pad(flat, (0, padded - flat.shape[0])).reshape(-1, LANE)


def _unpack(packed, shapes):
    flat = packed.reshape(-1)
    out, off = [], 0
    for s in shapes:
        n = math.prod(s)
        out.append(flat[off:off + n].reshape(s))
        off += n
    return out


def _gather_last(g, shard_shape):
    nd = len(shard_shape)
    full = jnp.moveaxis(g, 0, nd - 1)
    return full.reshape(*shard_shape[:-1], N_DEV * shard_shape[-1])


def _natural(blocks):
    nb, k, c = blocks.shape
    return jnp.transpose(blocks, (1, 0, 2)).reshape(k, nb * c)


def _block_diag(w):
    w2 = w.reshape(4, 2, 64, 64)
    z = jnp.zeros((4, 64, 64), w.dtype)
    return jnp.concatenate([jnp.concatenate([w2[:, 0], z], axis=2), jnp.concatenate([z, w2[:, 1]], axis=2)], axis=1)


def _block_diag_t(d):
    return jnp.stack([d[:, :64, :64], d[:, 64:, 64:]], axis=1).reshape(8, 64, 64)


def _lower_bounds(raw):
    lb = jnp.cumsum(jax.nn.softmax(raw.astype(F32), axis=0), axis=0)
    return lb - lb[0:1]


def _decay_rates(lam):
    return (LRU_C * jax.nn.softplus(-lam.astype(F32))).reshape(DEPTH, BRANCH_W)


def _relu2(acc):
    r = jnp.maximum(acc, 0.0)
    return acc, r * r


def _relu2_grad(acc, u):
    return (acc * 2.0 * jnp.maximum(u, 0.0),)


def _add(acc, e):
    return (acc + e,)


def _layer_fwd(x0, mem, p, g, rest, after=()):
    h1 = rms_fwd("rms_mix", x0, p['norm_mix_w'])
    proj = mm_nt("mm_in", h1, g['w_in'], tn=2176, after=after)[0]
    bcat, states, o_hg = hgrn_fwd(proj, p['lb'], p['hg_norm_w'])
    zc = cv_fwd(proj, p['cv_w32'], p['cv_dw_b'])
    bcat = ln_silu_fwd(zc, p['cv_ln_w'], p['cv_ln_b'], bcat)
    bcat = pool_fwd(proj, p['pl_w'], p['pl_scale'], bcat)
    bcat, hst = lru_fwd(proj, p['lru_cw8'], p['lru_conv_b'], p['wa_bd'], p['lru_ba'], p['wx_bd'], p['lru_bx'], p['sp8'], bcat)
    more, after = rest(bcat)
    g = {**g, **more}
    ups, gates, merged = merge_fwd(bcat, g['w_branch'], proj, p['gate_b'], after=after)
    x1 = mm_nn("mm_out", merged, g['w_out'], epi=_add, extras=(x0,))[0]
    h2 = rms_fwd("rms_mem", x1, p['norm_mem_w'])
    q = mm_nn("mm_q", h2, g['xa_wq'], out_dtype=BF16)[0]
    memn = rms_fwd("rms_memtok", mem, p['mem_norm_w'])
    kv = mm_nn("mm_kv", memn, g['xa_wkv'], out_dtype=BF16, tn=2048)[0]
    oa = attn_fwd(q, kv)
    x2 = mm_nn("mm_o", oa, g['xa_wo'], epi=_add, extras=(x1,))[0]
    h3 = rms_fwd("rms_ffn", x2, p['norm_ffn_w'])
    u, act = mm_nn("mm_ffn1", h3, g['ffn_w1'], epi=_relu2, out_dtypes=[BF16, BF16])
    x3 = mm_nn("mm_ffn2", act, g['ffn_w2'], epi=_add, extras=(x2,))[0]
    res = dict(x0=x0, h1=h1, proj=proj, states=states, o_hg=o_hg, zc=zc, hst=hst, bcat=bcat, ups=ups, gates=gates, merged=merged,
               x1=x1, h2=h2, q=q, memn=memn, kv=kv, oa=oa, x2=x2, h3=h3, u=u, act=act)
    return x3, res, g


def _layer_bwd(dx3, mem, p, g, r, midway, finish):
    gs, gb = {}, {}
    du = mm_nt("mm_dffn2", dx3, g['ffn_w2'], out_dtype=BF16, epi=_relu2_grad, extras=(r['u'],))[0]
    gb['ffn_w2'] = mm_tn("mm_gw2", r['act'], dx3).reshape(N_DEV, -1, D_MODEL)
    gb['ffn_w1'] = mm_tn_cb("mm_gw1", r['h3'], du, N_DEV)
    dh3 = mm_nt("mm_dffn1", du, g['ffn_w1'], out_dtype=BF16)[0]
    dx2, gs['norm_ffn_w'] = rms_bwd("rmsb_ffn", r['x2'], p['norm_ffn_w'], dh3, dx3)
    doa = mm_nt("mm_do", dx2, g['xa_wo'], out_dtype=BF16)[0]
    gb['xa_wo'] = mm_tn("mm_gwo", r['oa'], dx2).reshape(N_DEV, -1, D_MODEL)
    dq, dkv = attn_bwd(r['q'], r['kv'], doa)
    gb['xa_wq'] = mm_tn("mm_gwq", r['h2'], dq).reshape(N_DEV, -1, D_MODEL)
    dh2 = mm_nt("mm_dq", dq, g['xa_wq'], out_dtype=BF16)[0]
    gb['xa_wkv'] = mm_tn_cb("mm_gwkv", r['memn'], dkv, N_DEV)
    dmemn = mm_nt("mm_dkv", dkv, g['xa_wkv'], out_dtype=BF16)[0]
    _, gs['mem_norm_w'] = rms_bwd("rmsb_memtok", mem, p['mem_norm_w'], dmemn)
    dx1, gs['norm_mem_w'] = rms_bwd("rmsb_mem", r['x1'], p['norm_mem_w'], dh2, dx2)
    after = midway(gb, dx1)
    gb = {}
    dmerged = mm_nt("mm_dout", dx1, g['w_out'], out_dtype=BF16, after=after)[0]
    gb['w_out'] = mm_tn("mm_gwout", r['merged'], dx1).reshape(N_DEV, -1, D_MODEL)
    dups, dproj, gs['gate_b'] = merge_bwd(dmerged, r['ups'], r['gates'])
    gwb = mm_branch_tn("mm_gwb", r['bcat'], dups, N_BRANCH)
    gb['w_branch'] = jnp.transpose(gwb.reshape(N_BRANCH, BRANCH_W, N_DEV, -1), (2, 0, 1, 3))
    dbcat = mm_branch_nt("mm_dup", dups, g['w_branch'], tm=2048)
    dproj, gs['lb'], gs['hg_norm_w'] = hgrn_bwd(r['proj'], p['lb'], p['hg_norm_w'], r['states'], r['o_hg'], dbcat, dproj)
    dzc, gs['cv_ln_w'], gs['cv_ln_b'] = ln_silu_bwd(r['zc'], p['cv_ln_w'], p['cv_ln_b'], dbcat)
    dca, dcg, dcw, gs['cv_dw_b'] = cv_bwd(r['proj'], p['cv_w32'], dzc)
    gs['cv_dw_w'] = dcw[:CV_KERNEL]
    dproj, gs['pl_w'], gs['pl_scale'] = pool_bwd(r['proj'], p['pl_w'], p['pl_scale'], dbcat, dproj)
    dlx, dly, dlcw, gs['lru_conv_b'], dwa, gs['lru_ba'], dwx, gs['lru_bx'], gs['sp8'] = lru_bwd(
        r['proj'], p['lru_cw8'], p['lru_conv_b'], p['wa_bd'], p['lru_ba'], p['wx_bd'], p['lru_bx'], p['sp8'], r['hst'], dbcat)
    gs['lru_conv_w'] = dlcw[:LRU_CONV]
    gs['lru_wa'], gs['lru_wx'] = _block_diag_t(dwa), _block_diag_t(dwx)
    gs['lru_ba'], gs['lru_bx'] = gs['lru_ba'].reshape(8, 64), gs['lru_bx'].reshape(8, 64)
    for off, piece in ((OFF_CV, dca), (OFF_CV + BRANCH_W, dcg), (OFF_LX, dlx), (OFF_LY, dly)):
        dproj = lax.dynamic_update_slice(dproj, piece, (0, off))
    gb['w_in'] = mm_tn("mm_gwin", dproj, r['h1'], tm=2176, tk=1024).reshape(N_DEV, -1, D_MODEL)
    dh1 = mm_nn("mm_din", dproj, g['w_in'], out_dtype=BF16, tk=4352, after=finish(gb, dx1))[0]
    dx0, gs['norm_mix_w'] = rms_bwd("rmsb_mix", r['x0'], p['norm_mix_w'], dh1, dx1)
    return dx0, gs


def kernel(x, mem, norm_mix_w, w_in, hg_lb_raw, hg_norm_w, cv_dw_w, cv_dw_b, cv_ln_w, cv_ln_b, pl_w, pl_scale, lru_conv_w, lru_conv_b, lru_wa, lru_ba, lru_wx, lru_bx, lru_lambda, gate_b, w_branch, w_out, norm_mem_w, mem_norm_w, xa_wq, xa_wkv, xa_wo, norm_ffn_w, ffn_w1, ffn_w2, final_norm_w, loss_target, m_norm_mix_w, m_w_in, m_hg_lb_raw, m_hg_norm_w, m_cv_dw_w, m_cv_dw_b, m_cv_ln_w, m_cv_ln_b, m_pl_w, m_pl_scale, m_lru_conv_w, m_lru_conv_b, m_lru_wa, m_lru_ba, m_lru_wx, m_lru_bx, m_lru_lambda, m_gate_b, m_w_branch, m_w_out, m_norm_mem_w, m_mem_norm_w, m_xa_wq, m_xa_wkv, m_xa_wo, m_norm_ffn_w, m_ffn_w1, m_ffn_w2, m_final_norm_w, v_norm_mix_w, v_w_in, v_hg_lb_raw, v_hg_norm_w, v_cv_dw_w, v_cv_dw_b, v_cv_ln_w, v_cv_ln_b, v_pl_w, v_pl_scale, v_lru_conv_w, v_lru_conv_b, v_lru_wa, v_lru_ba, v_lru_wx, v_lru_bx, v_lru_lambda, v_gate_b, v_w_branch, v_w_out, v_norm_mem_w, v_mem_norm_w, v_xa_wq, v_xa_wkv, v_xa_wo, v_norm_ffn_w, v_ffn_w1, v_ffn_w2, v_final_norm_w):
    W = dict(zip(WEIGHTS, (norm_mix_w, w_in, hg_lb_raw, hg_norm_w, cv_dw_w, cv_dw_b, cv_ln_w, cv_ln_b, pl_w, pl_scale, lru_conv_w, lru_conv_b, lru_wa, lru_ba, lru_wx, lru_bx, lru_lambda, gate_b, w_branch, w_out, norm_mem_w, mem_norm_w, xa_wq, xa_wkv, xa_wo, norm_ffn_w, ffn_w1, ffn_w2, final_norm_w)))
    Mo = dict(zip(WEIGHTS, (m_norm_mix_w, m_w_in, m_hg_lb_raw, m_hg_norm_w, m_cv_dw_w, m_cv_dw_b, m_cv_ln_w, m_cv_ln_b, m_pl_w, m_pl_scale, m_lru_conv_w, m_lru_conv_b, m_lru_wa, m_lru_ba, m_lru_wx, m_lru_bx, m_lru_lambda, m_gate_b, m_w_branch, m_w_out, m_norm_mem_w, m_mem_norm_w, m_xa_wq, m_xa_wkv, m_xa_wo, m_norm_ffn_w, m_ffn_w1, m_ffn_w2, m_final_norm_w)))
    Vo = dict(zip(WEIGHTS, (v_norm_mix_w, v_w_in, v_hg_lb_raw, v_hg_norm_w, v_cv_dw_w, v_cv_dw_b, v_cv_ln_w, v_cv_ln_b, v_pl_w, v_pl_scale, v_lru_conv_w, v_lru_conv_b, v_lru_wa, v_lru_ba, v_lru_wx, v_lru_bx, v_lru_lambda, v_gate_b, v_w_branch, v_w_out, v_norm_mem_w, v_mem_norm_w, v_xa_wq, v_xa_wkv, v_xa_wo, v_norm_ffn_w, v_ffn_w1, v_ffn_w2, v_final_norm_w)))
    me = _slot(*_place())
    xs, mems, target = x[0], mem[0], loss_target[0]

    shard_shapes = [W[n].shape for n in SMALL_SHARDED]
    gathered = all_gather("ag_small", [_pack([W[n] for n in SMALL_SHARDED])])[0]
    parts = [jnp.stack(ps) for ps in zip(*[_unpack(gathered[d], shard_shapes) for d in range(N_DEV)])]
    full_small = {n: _gather_last(parts[i], shard_shapes[i]) for i, n in enumerate(SMALL_SHARDED)}
    lb_all, lb_vjp = jax.vjp(_lower_bounds, hg_lb_raw)
    sp8_all, sp8_vjp = jax.vjp(_decay_rates, lru_lambda)

    def layer_params(l):
        p = {n: W[n][l] for n in SMALL if n != 'final_norm_w'}
        p['lb'] = lb_all[l]
        p['sp8'] = sp8_all[l]
        p['cv_w32'] = jnp.pad(full_small['cv_dw_w'][l], ((0, 32 - CV_KERNEL), (0, 0)))
        p['lru_cw8'] = jnp.pad(full_small['lru_conv_w'][l], ((0, 8 - LRU_CONV), (0, 0)))
        p['gate_b'] = full_small['gate_b'][l]
        p['wa_bd'], p['wx_bd'] = _block_diag(lru_wa[l]), _block_diag(lru_wx[l])
        p['lru_ba'], p['lru_bx'] = lru_ba[l].reshape(-1), lru_bx[l].reshape(-1)
        return p

    def shards_of(l):
        first = [jnp.transpose(w_in[l]).astype(BF16)]
        others = [w[l].astype(BF16) for w in (w_branch, w_out, xa_wq, xa_wkv, xa_wo, ffn_w1, ffn_w2)]
        return first, others

    def start_gather(l, before):
        first, others = shards_of(l)
        state_a, tok_a = gather_start(f"ag_start{l}a", first, me, before)
        state_b, tok_b = gather_start(f"ag_start{l}b", others, me, (*before, tok_a))
        return state_a, state_b, (tok_a, tok_b)

    def first_of(o):
        return dict(w_in=o[0].reshape(IN_W, D_MODEL))

    def others_of(o):
        wb = jnp.transpose(o[0], (1, 2, 0, 3)).reshape(N_BRANCH, BRANCH_W, D_MODEL)
        return dict(w_branch=wb, w_out=o[1].reshape(D_MODEL, D_MODEL),
                    xa_wq=o[2].reshape(D_MODEL, D_MODEL), xa_wkv=_natural(o[3]), xa_wo=o[4].reshape(D_MODEL, D_MODEL),
                    ffn_w1=_natural(o[5]), ffn_w2=o[6].reshape(D_FF, D_MODEL))

    params = [layer_params(l) for l in range(DEPTH)]
    mats, residuals = [], []
    xc = xs
    first, others = shards_of(0)
    whole = all_gather("ag_layer0", first)
    state_b, started = gather_start("ag_start0b", others, me, (whole[0],))
    gathers = {}
    for l in range(DEPTH):
        if l == 0:
            g_first = first_of(whole)
        else:
            state_a, state_b, _ = gathers.pop(l)
            g_first = first_of(gather_wait(f"ag_wait{l}a", state_a, (xc,)))

        def rest(mixed, l=l):
            more = others_of(gather_wait(f"ag_wait{l}b", state_b, (mixed,)))
            if l + 1 == DEPTH:
                return more, ()
            gathers[l + 1] = start_gather(l + 1, (more['w_out'],))
            return more, gathers[l + 1][2]

        xc, res, g = _layer_fwd(xc, mems, params[l], g_first, rest, after=(started,) if l == 0 else ())
        mats.append(g)
        residuals.append(res)
    loss_part, dx, g_final = loss_head(xc, final_norm_w, target)
    loss = lax.psum(loss_part, ("x", "y", "c"))

    small_grads = [None] * DEPTH
    big_grads = [{} for _ in range(DEPTH)]
    pending = []

    def send(l, group, blocks, before):
        names = list(blocks)
        state, tok = exchange_start(f"rs_start{l}{group}", [blocks[n] for n in names], (before,))
        pending.append((l, group, names, state))
        return (tok,)

    def land(after):
        l, group, names, state = pending.pop(0)
        sent, landed = exchange_wait(f"rs_wait{l}{group}", state, after)
        for n, s, t in zip(names, sent, landed):
            own = lax.dynamic_index_in_dim(s, me, 0, keepdims=False).reshape(-1, s.shape[-1])
            big_grads[l][n] = sum_parts(t.reshape(N_PEER, -1, t.shape[-1]), own).reshape(t.shape[1:])

    for l in reversed(range(DEPTH)):
        dx, small_grads[l] = _layer_bwd(dx, mems, params[l], mats[l], residuals[l],
                                        lambda blocks, dx1, l=l: send(l, "a", blocks, dx1),
                                        lambda blocks, dx1, l=l: send(l, "b", blocks, dx1))
        while pending[0][0] > l:
            land(dx)

    def stacked(n):
        return jnp.stack([small_grads[l][n] for l in range(DEPTH)])

    part = {n: stacked(n) for n in SMALL if n not in ('final_norm_w', 'hg_lb_raw', 'lru_lambda')}
    part['final_norm_w'] = g_final
    part['hg_lb_raw'] = lb_vjp(stacked('lb'))[0]
    part['lru_lambda'] = sp8_vjp(stacked('sp8'))[0]
    for n in SMALL_SHARDED:
        part[n] = stacked(n)
    names = list(SMALL) + list(SMALL_SHARDED)
    full_shapes = [part[n].shape for n in names]
    packed = _pack([part[n] for n in names])
    while len(pending) > 1:
        land(dx)
    summed = [t for layer in big_grads for t in layer.values()]
    state, _ = exchange_start("rs_small_start", [packed.reshape(N_DEV, -1, LANE)], (packed, *summed))
    sent, landed = exchange_wait("rs_small_wait", state, packed)
    mine = sum_parts(landed[0], lax.dynamic_index_in_dim(sent[0], me, 0, keepdims=False))
    total = all_gather("ag_grads", [mine])[0].reshape(-1, LANE)
    while pending:
        land(total)

    G = {}
    G['w_in'] = jnp.stack([jnp.transpose(big_grads[l]['w_in']) for l in range(DEPTH)])
    for n in ('w_branch', 'w_out', 'xa_wq', 'xa_wkv', 'xa_wo', 'ffn_w1', 'ffn_w2'):
        G[n] = jnp.stack([big_grads[l][n] for l in range(DEPTH)])
    for n, t in zip(names, _unpack(total, full_shapes)):
        if n in SMALL_SHARDED:
            c = t.shape[-1] // N_DEV
            t = lax.dynamic_slice_in_dim(t, me * c, c, axis=t.ndim - 1)
        G[n] = t

    delta, new_m, new_v = {}, {}, {}
    for n in BIG:
        c = W[n].shape[-1]
        d, nm, nv = adamw(W[n].reshape(-1, c), G[n].reshape(-1, c), Mo[n].reshape(-1, c), Vo[n].reshape(-1, c))
        delta[n], new_m[n], new_v[n] = d.reshape(W[n].shape), nm.reshape(W[n].shape), nv.reshape(W[n].shape)
    shapes = [W[n].shape for n in names]
    d, nm, nv = adamw(_pack([W[n] for n in names]), _pack([G[n] for n in names]), _pack([Mo[n] for n in names]), _pack([Vo[n] for n in names]))
    for n, a, b, c in zip(names, _unpack(d, shapes), _unpack(nm, shapes), _unpack(nv, shapes)):
        delta[n], new_m[n], new_v[n] = a, b, c
    return (loss, dx[None], *[G[n] for n in WEIGHTS], *[delta[n] for n in WEIGHTS],
            *[new_m[n] for n in WEIGHTS], *[new_v[n] for n in WEIGHTS])
```

```python
import functools
import math

import jax
import jax.numpy as jnp
from jax import lax
from jax.experimental import pallas as pl
from jax.experimental.pallas import tpu as pltpu

F32 = jnp.float32
BF16 = jnp.bfloat16
I32 = jnp.int32

N_DEV = 8
D_MODEL = 1024
DEPTH = 4
CHUNK = 64
EPS = 1e-6
HG_HEADS = 4
BRANCH_W = 512
CV_KERNEL = 31
POOL_WINDOWS = (2, 4, 8, 16)
LRU_CONV = 4
LRU_C = 8.0
XA_HEADS = 4
XA_HD = D_MODEL // XA_HEADS
D_FF = 4 * D_MODEL
IN_W = 8704
OFF_Q, OFF_F, OFF_V, OFF_G, OFF_CV, OFF_PL, OFF_LX, OFF_LY, OFF_GATE = 0, 512, 1024, 1536, 2048, 3072, 3584, 4096, 4608
LANE = 128
ADAM_LR, ADAM_B1, ADAM_B2, ADAM_EPS, ADAM_WD, ADAM_STEP = 0.001, 0.9, 0.999, 1e-08, 0.01, 10
VMEM_LIMIT = 56 * 1024 * 1024
MESH = pl.DeviceIdType.MESH
NEG = -1e30
ANY_SPACE = pl.BlockSpec(memory_space=pl.ANY)


def _cp(sem, **kw):
    return pltpu.CompilerParams(dimension_semantics=sem, vmem_limit_bytes=VMEM_LIMIT, **kw)


def _sigmoid(x):
    return 1.0 / (1.0 + jnp.exp(-x))


def _dsilu(x, s):
    return s * (1.0 + x * (1.0 - s))


def _dot(a, b, cdims, precision=None):
    return lax.dot_general(a, b, (cdims, ((), ())), preferred_element_type=F32, precision=precision)


NN = ((1,), (0,))
NT = ((1,), (1,))
TN = ((0,), (0,))


def _mm(name, a, b, *, grid, a_spec, b_spec, o_specs, out_shapes, acc_shape, cdims, epi=None, extras=(), extra_specs=(), after=()):
    nk = grid[2]
    n_e, n_o = len(extras), len(out_shapes)
    extras = (*extras, *after)
    extra_specs = (*extra_specs, *[ANY_SPACE] * len(after))

    def body(*refs):
        a_ref, b_ref = refs[0], refs[1]
        e_refs = refs[2:2 + n_e]
        o_refs = refs[2 + len(extras):2 + len(extras) + n_o]

        def finish(acc):
            vals = epi(acc, *[r[...] for r in e_refs]) if epi is not None else (acc,)
            for r, v in zip(o_refs, vals):
                r[...] = v.astype(r.dtype)

        part = _dot(a_ref[...].astype(BF16), b_ref[...].astype(BF16), cdims)
        if nk == 1:
            finish(part)
        else:
            acc_ref = refs[-1]
            k = pl.program_id(2)

            @pl.when(k == 0)
            def _():
                acc_ref[...] = part

            @pl.when(k > 0)
            def _():
                acc_ref[...] += part

            @pl.when(k == nk - 1)
            def _():
                finish(acc_ref[...])

    return pl.pallas_call(
        body, name=name, grid=grid,
        in_specs=[a_spec, b_spec, *extra_specs], out_specs=list(o_specs), out_shape=list(out_shapes),
        scratch_shapes=[] if nk == 1 else [pltpu.VMEM(acc_shape, F32)],
        compiler_params=_cp(("parallel", "parallel", "arbitrary")),
    )(a, b, *extras)


def _tile(n, pref):
    t = min(n, pref)
    while n % t:
        t //= 2
    return t


def mm_nt(name, a, b, out_dtype=F32, epi=None, extras=(), n_out=1, out_dtypes=None, tm=1024, tn=1024, tk=2048, after=()):
    M, K = a.shape
    N = b.shape[0]
    tm, tn, tk = _tile(M, tm), _tile(N, tn), _tile(K, tk)
    odt = out_dtypes or [out_dtype] * n_out
    o_spec = pl.BlockSpec((tm, tn), lambda i, j, k: (i, j))
    return _mm(name, a, b, grid=(M // tm, N // tn, K // tk),
               a_spec=pl.BlockSpec((tm, tk), lambda i, j, k: (i, k)),
               b_spec=pl.BlockSpec((tn, tk), lambda i, j, k: (j, k)),
               o_specs=[o_spec] * len(odt), out_shapes=[jax.ShapeDtypeStruct((M, N), d) for d in odt],
               acc_shape=(tm, tn), cdims=NT, epi=epi, extras=extras, extra_specs=[o_spec] * len(extras), after=after)


def mm_nn(name, a, b, out_dtype=F32, epi=None, extras=(), n_out=1, out_dtypes=None, tm=1024, tn=1024, tk=2048, after=()):
    M, K = a.shape
    N = b.shape[1]
    tm, tn, tk = _tile(M, tm), _tile(N, tn), _tile(K, tk)
    odt = out_dtypes or [out_dtype] * n_out
    o_spec = pl.BlockSpec((tm, tn), lambda i, j, k: (i, j))
    return _mm(name, a, b, grid=(M // tm, N // tn, K // tk),
               a_spec=pl.BlockSpec((tm, tk), lambda i, j, k: (i, k)),
               b_spec=pl.BlockSpec((tk, tn), lambda i, j, k: (k, j)),
               o_specs=[o_spec] * len(odt), out_shapes=[jax.ShapeDtypeStruct((M, N), d) for d in odt],
               acc_shape=(tm, tn), cdims=NN, epi=epi, extras=extras, extra_specs=[o_spec] * len(extras), after=after)


def mm_tn(name, a, b, out_dtype=BF16, tm=1024, tn=1024, tk=2048):
    K, M = a.shape
    N = b.shape[1]
    tm, tn, tk = _tile(M, tm), _tile(N, tn), _tile(K, tk)
    return _mm(name, a, b, grid=(M // tm, N // tn, K // tk),
               a_spec=pl.BlockSpec((tk, tm), lambda i, j, k: (k, i)),
               b_spec=pl.BlockSpec((tk, tn), lambda i, j, k: (k, j)),
               o_specs=[pl.BlockSpec((tm, tn), lambda i, j, k: (i, j))],
               out_shapes=[jax.ShapeDtypeStruct((M, N), out_dtype)], acc_shape=(tm, tn), cdims=TN)[0]


def mm_branch_nt(name, a, b, out_dtype=BF16, tm=1024):
    M = a.shape[0]
    G, K, N = b.shape
    tm = _tile(M, tm)
    return _mm(name, a, b, grid=(M // tm, G, 1),
               a_spec=pl.BlockSpec((tm, N), lambda i, g, k: (i, g)),
               b_spec=pl.BlockSpec((None, K, N), lambda i, g, k: (g, 0, 0)),
               o_specs=[pl.BlockSpec((tm, K), lambda i, g, k: (i, g))],
               out_shapes=[jax.ShapeDtypeStruct((M, G * K), out_dtype)], acc_shape=(tm, K), cdims=NT)[0]


def mm_branch_tn(name, a, b, groups, out_dtype=BF16, tk=2048):
    T = a.shape[0]
    K, N = a.shape[1] // groups, b.shape[1] // groups
    tk = _tile(T, tk)
    return _mm(name, a, b, grid=(groups, 1, T // tk),
               a_spec=pl.BlockSpec((tk, K), lambda g, j, k: (k, g)),
               b_spec=pl.BlockSpec((tk, N), lambda g, j, k: (k, g)),
               o_specs=[pl.BlockSpec((None, K, N), lambda g, j, k: (g, 0, 0))],
               out_shapes=[jax.ShapeDtypeStruct((groups, K, N), out_dtype)], acc_shape=(K, N), cdims=TN)[0]


def mm_tn_cb(name, a, b, nb, out_dtype=BF16, tm=1024, tk=2048):
    K, M = a.shape
    N = b.shape[1]
    c = N // nb
    tm, tk = _tile(M, tm), _tile(K, tk)
    return _mm(name, a, b, grid=(M // tm, nb, K // tk),
               a_spec=pl.BlockSpec((tk, tm), lambda i, j, k: (k, i)),
               b_spec=pl.BlockSpec((tk, c), lambda i, j, k: (k, j)),
               o_specs=[pl.BlockSpec((None, tm, c), lambda i, j, k: (j, i, 0))],
               out_shapes=[jax.ShapeDtypeStruct((nb, M, c), out_dtype)], acc_shape=(tm, c), cdims=TN)[0]


def rms_fwd(name, x, w, out_dtype=BF16, tm=512):
    S, D = x.shape
    tm = _tile(S, tm)

    def body(x_ref, w_ref, o_ref):
        xv = x_ref[...]
        r = lax.rsqrt(jnp.mean(xv * xv, axis=-1, keepdims=True) + EPS)
        o_ref[...] = (xv * r * w_ref[...]).astype(o_ref.dtype)

    return pl.pallas_call(
        body, name=name, grid=(S // tm,),
        in_specs=[pl.BlockSpec((tm, D), lambda i: (i, 0)), pl.BlockSpec((1, D), lambda i: (0, 0))],
        out_specs=pl.BlockSpec((tm, D), lambda i: (i, 0)), out_shape=jax.ShapeDtypeStruct((S, D), out_dtype),
        compiler_params=_cp(("parallel",)),
    )(x, w.reshape(1, D))


def rms_bwd(name, x, w, dh, dres=None, tm=512):
    S, D = x.shape
    tm = _tile(S, tm)
    has_res = dres is not None

    def body(*refs):
        if has_res:
            x_ref, w_ref, dh_ref, dres_ref, dx_ref, dw_ref = refs
        else:
            x_ref, w_ref, dh_ref, dx_ref, dw_ref = refs
        xv = x_ref[...]
        dhv = dh_ref[...].astype(F32)
        r = lax.rsqrt(jnp.mean(xv * xv, axis=-1, keepdims=True) + EPS)
        g = dhv * w_ref[...]
        dx = r * g - xv * (r * r * r) * jnp.mean(xv * g, axis=-1, keepdims=True)
        if has_res:
            dx = dx + dres_ref[...]
        dx_ref[...] = dx

        @pl.when(pl.program_id(0) == 0)
        def _():
            dw_ref[...] = jnp.zeros_like(dw_ref)

        dw_ref[...] += jnp.sum(dhv * xv * r, axis=0, keepdims=True)

    row = pl.BlockSpec((tm, D), lambda i: (i, 0))
    vec = pl.BlockSpec((1, D), lambda i: (0, 0))
    args = [x, w.reshape(1, D), dh] + ([dres] if has_res else [])
    dx, dw = pl.pallas_call(
        body, name=name, grid=(S // tm,),
        in_specs=[row, vec, row] + ([row] if has_res else []),
        out_specs=[row, vec], out_shape=[jax.ShapeDtypeStruct((S, D), F32), jax.ShapeDtypeStruct((1, D), F32)],
        compiler_params=_cp(("arbitrary",)),
    )(*args)
    return dx, dw.reshape(D)


def loss_head(x, w, target, tm=512):
    S, D = x.shape
    tm = _tile(S, tm)

    def body(x_ref, w_ref, t_ref, loss_ref, dx_ref, dw_ref):
        xv = x_ref[...]
        wv = w_ref[...]
        r = lax.rsqrt(jnp.mean(xv * xv, axis=-1, keepdims=True) + EPS)
        y = xv * r * wv
        err = y - t_ref[...]
        dy = err * (1.0 / D)
        g = dy * wv
        dx_ref[...] = r * g - xv * (r * r * r) * jnp.mean(xv * g, axis=-1, keepdims=True)

        @pl.when(pl.program_id(0) == 0)
        def _():
            dw_ref[...] = jnp.zeros_like(dw_ref)
            loss_ref[...] = jnp.zeros_like(loss_ref)

        dw_ref[...] += jnp.sum(dy * xv * r, axis=0, keepdims=True)
        part = 0.5 * jnp.sum(jnp.mean(err * err, axis=-1, keepdims=True), axis=0, keepdims=True)
        loss_ref[...] += jnp.broadcast_to(part, loss_ref.shape)

    row = pl.BlockSpec((tm, D), lambda i: (i, 0))
    vec = pl.BlockSpec((1, D), lambda i: (0, 0))
    loss, dx, dw = pl.pallas_call(
        body, name="loss_head", grid=(S // tm,),
        in_specs=[row, vec, row],
        out_specs=[pl.BlockSpec((1, LANE), lambda i: (0, 0)), row, vec],
        out_shape=[jax.ShapeDtypeStruct((1, LANE), F32), jax.ShapeDtypeStruct((S, D), F32), jax.ShapeDtypeStruct((1, D), F32)],
        compiler_params=_cp(("arbitrary",)),
    )(x, w.reshape(1, D), target)
    return loss[0, 0], dx, dw.reshape(D)


SUB = 16
HG_W = HG_HEADS * LANE


def _hg_gates(q, f, lbv):
    sig = _sigmoid(f)
    fg = lbv + (1.0 - lbv) * sig
    sq = _sigmoid(q)
    return sig, fg, 1.0 - fg, sq, q * sq


def _hg_cumsum(logf):
    ri = lax.broadcasted_iota(I32, (CHUNK, CHUNK), 0)
    ci = lax.broadcasted_iota(I32, (CHUNK, CHUNK), 1)
    return _dot((ci <= ri).astype(F32), logf, NN, precision=lax.Precision.HIGHEST)


def _hg_rows():
    return lax.broadcasted_iota(I32, (CHUNK, LANE), 0)


def _hg_below(qf, kk, b, rows):
    blocks, parts = [jnp.zeros((SUB, CHUNK), F32)], []
    for i in range(1, CHUNK // SUB):
        bref = b[SUB * i - 1:SUB * i, :]
        rs = slice(SUB * i, SUB * (i + 1))
        eq = jnp.exp(b[rs] - bref)
        below = rows < SUB * i
        ek = jnp.exp(jnp.where(below, bref - b, NEG))
        qi = (qf[rs] * eq).astype(BF16)
        ki = (kk * ek).astype(BF16)
        blocks.append(_dot(qi, ki, NT))
        parts.append((qi, ki, eq, ek))
    return jnp.concatenate(blocks, axis=0), parts


def hgrn_fwd(proj, lb, nw):
    S = proj.shape[0]
    NC = S // CHUNK
    H = HG_HEADS

    def body(q_ref, f_ref, v_ref, g_ref, lb_ref, nw_ref, out_ref, st_out_ref, o_ref, st, kk_s, b_s):
        c = pl.program_id(0)

        @pl.when(c == 0)
        def _():
            st[...] = jnp.zeros_like(st)

        st_out_ref[...] = st[...]
        sig, fg, kk_all, sq, qf_all = _hg_gates(q_ref[...], f_ref[...], lb_ref[...])
        b_all = _hg_cumsum(jnp.log(fg))
        kk_s[...] = kk_all
        b_s[...] = b_all
        rows = _hg_rows()
        r16 = lax.broadcasted_iota(I32, (SUB, LANE), 0)
        for h in range(H):
            cs = slice(h * LANE, (h + 1) * LANE)
            qf, kk, b, v, g = qf_all[:, cs], kk_all[:, cs], b_all[:, cs], v_ref[:, cs], g_ref[:, cs]
            st_in = st[h]
            diag = []
            for i in range(CHUNK // SUB):
                rs = slice(SUB * i, SUB * (i + 1))
                acc = jnp.zeros((SUB, LANE), F32)
                for j in range(SUB):
                    row = pl.ds(SUB * i + j, 1)
                    e = jnp.exp(jnp.where(r16 >= j, b[rs] - b_s[row, cs], NEG))
                    col = jnp.sum(qf[rs] * (kk_s[row, cs] * e), axis=1, keepdims=True)
                    acc = acc + col * v_ref[row, cs]
                diag.append(acc)
            poff, _ = _hg_below(qf, kk, b, rows)
            vb = v.astype(BF16)
            bl = b[CHUNK - 1:CHUNK, :]
            o = (jnp.concatenate(diag, axis=0) + _dot(poff.astype(BF16), vb, NN)
                 + _dot((qf * jnp.exp(b)).astype(BF16), st_in.astype(BF16), NT))
            st[h] = st_in * jnp.exp(bl) + _dot(vb, (kk * jnp.exp(bl - b)).astype(BF16), TN)
            o_ref[:, cs] = o
            r = lax.rsqrt(jnp.mean(o * o, axis=-1, keepdims=True) + EPS)
            out_ref[:, cs] = (o * r * nw_ref[...] * (g * _sigmoid(g))).astype(out_ref.dtype)

    def seg(off):
        return pl.BlockSpec((CHUNK, HG_W), lambda c: (c, off // HG_W))

    blk = pl.BlockSpec((CHUNK, HG_W), lambda c: (c, 0))
    full = pltpu.VMEM((CHUNK, HG_W), F32)
    return pl.pallas_call(
        body, name="hgrn_fwd", grid=(NC,),
        in_specs=[seg(OFF_Q), seg(OFF_F), seg(OFF_V), seg(OFF_G),
                  pl.BlockSpec((1, HG_W), lambda c: (0, 0)), pl.BlockSpec((1, LANE), lambda c: (0, 0))],
        out_specs=[blk, pl.BlockSpec((None, H, LANE, LANE), lambda c: (c, 0, 0, 0)), blk],
        out_shape=[jax.ShapeDtypeStruct((S, 4 * HG_W), BF16), jax.ShapeDtypeStruct((NC, H, LANE, LANE), F32),
                   jax.ShapeDtypeStruct((S, HG_W), F32)],
        scratch_shapes=[pltpu.VMEM((H, LANE, LANE), F32), full, full],
        compiler_params=_cp(("arbitrary",)),
    )(proj, proj, proj, proj, lb.reshape(1, HG_W), nw.reshape(1, LANE))


def hgrn_bwd(proj, lb, nw, states, o_pre, dbcat, dproj):
    S = proj.shape[0]
    NC = S // CHUNK
    H = HG_HEADS

    def body(q_ref, f_ref, v_ref, g_ref, lb_ref, nw_ref, st_ref, o_ref, do_ref, _,
             dp_ref, dlb_ref, dnw_ref, dst, kk_s, b_s, do_s, db_s, dkk_s, dkk_d, dv_d):
        c = pl.program_id(0)
        dq_ref, df_ref, dv_ref, dg_ref = (dp_ref.at[:, pl.ds(off, HG_W)] for off in (OFF_Q, OFF_F, OFF_V, OFF_G))

        @pl.when(c == 0)
        def _():
            dst[...] = jnp.zeros_like(dst)
            dlb_ref[...] = jnp.zeros_like(dlb_ref)
            dnw_ref[...] = jnp.zeros_like(dnw_ref)

        q_all, g_all = q_ref[...], g_ref[...]
        lbv, nwv = lb_ref[...], nw_ref[...]
        sig, fg, kk_all, sq, qf_all = _hg_gates(q_all, f_ref[...], lbv)
        b_all = _hg_cumsum(jnp.log(fg))
        o_all = o_ref[...]
        dov = do_ref[...].astype(F32)
        sg = _sigmoid(g_all)
        gsg = g_all * sg
        dnw_acc = jnp.zeros((1, LANE), F32)
        for h in range(H):
            cs = slice(h * LANE, (h + 1) * LANE)
            o = o_all[:, cs]
            r = lax.rsqrt(jnp.mean(o * o, axis=-1, keepdims=True) + EPS)
            don = dov[:, cs] * gsg[:, cs]
            dnw_acc = dnw_acc + jnp.sum(don * o * r, axis=0, keepdims=True)
            gno = don * nwv
            do_s[:, cs] = r * gno - o * (r * r * r) * jnp.mean(o * gno, axis=-1, keepdims=True)
            dg_ref[:, cs] = (dov[:, cs] * (o * r * nwv) * _dsilu(g_all[:, cs], sg[:, cs])).astype(dg_ref.dtype)
        dnw_ref[...] += jnp.broadcast_to(dnw_acc, dnw_ref.shape)
        kk_s[...] = kk_all
        b_s[...] = b_all
        rows = _hg_rows()
        r16 = lax.broadcasted_iota(I32, (SUB, LANE), 0)
        for h in range(H):
            cs = slice(h * LANE, (h + 1) * LANE)
            qf, kk, b, v = qf_all[:, cs], kk_all[:, cs], b_all[:, cs], v_ref[:, cs]
            do = do_s[:, cs]
            st_in, dstv = st_ref[h], dst[h]
            bl = b[CHUNK - 1:CHUNK, :]
            eb, ebl, el = jnp.exp(b), jnp.exp(bl - b), jnp.exp(bl)
            qe, ke = qf * eb, kk * ebl
            vb, dob, stb, dstb = v.astype(BF16), do.astype(BF16), st_in.astype(BF16), dstv.astype(BF16)
            w_ = _dot(vb, dstb, NN)
            dqf = eb * _dot(dob, stb, NN)
            dkk = ebl * w_
            dv = _dot(ke.astype(BF16), dstb, NT)
            dbl = el * jnp.sum(st_in * dstv, axis=0, keepdims=True) + jnp.sum(ke * w_, axis=0, keepdims=True)
            dst[h] = dstv * el + _dot(dob, qe.astype(BF16), TN)
            poff, parts = _hg_below(qf, kk, b, rows)
            dpoff = _dot(dob, vb, NT).astype(BF16)
            dv = dv + _dot(poff.astype(BF16), dob, TN)
            dq_blocks = [jnp.zeros((SUB, LANE), F32)]
            for i, (qi, ki, eq, ek) in enumerate(parts, start=1):
                dpi = dpoff[SUB * i:SUB * (i + 1), :]
                dq_blocks.append(_dot(dpi, ki, NN) * eq)
                dkk = dkk + _dot(dpi, qi, TN) * ek
            dqf = dqf + jnp.concatenate(dq_blocks, axis=0)
            dq_diag = []
            for i in range(CHUNK // SUB):
                rs = slice(SUB * i, SUB * (i + 1))
                acc = jnp.zeros((SUB, LANE), F32)
                for j in range(SUB):
                    row = pl.ds(SUB * i + j, 1)
                    ks = kk_s[row, cs]
                    e = jnp.exp(jnp.where(r16 >= j, b[rs] - b_s[row, cs], NEG))
                    x = jnp.sum(do[rs] * v_ref[row, cs], axis=1, keepdims=True) * e
                    acc = acc + x * ks
                    dkk_d[row, cs] = jnp.sum(x * qf[rs], axis=0, keepdims=True)
                    col = jnp.sum(qf[rs] * (ks * e), axis=1, keepdims=True)
                    dv_d[row, cs] = jnp.sum(col * do[rs], axis=0, keepdims=True)
                dq_diag.append(acc)
            dqf = dqf + jnp.concatenate(dq_diag, axis=0)
            dkk = dkk + dkk_d[:, cs]
            dv = dv + dv_d[:, cs]
            dv_ref[:, cs] = dv.astype(dv_ref.dtype)
            db = qf * dqf - kk * dkk
            db_s[:, cs] = db + jnp.where(rows == CHUNK - 1, dbl, 0.0)
            dkk_s[:, cs] = dkk
            dq_ref[:, cs] = (dqf * _dsilu(q_all[:, cs], sq[:, cs])).astype(dq_ref.dtype)
        ri = lax.broadcasted_iota(I32, (CHUNK, CHUNK), 0)
        ci = lax.broadcasted_iota(I32, (CHUNK, CHUNK), 1)
        dlogf = _dot((ci >= ri).astype(F32), db_s[...], NN, precision=lax.Precision.HIGHEST)
        dfg = dlogf / fg - dkk_s[...]
        df_ref[...] = (dfg * (1.0 - lbv) * sig * (1.0 - sig)).astype(df_ref.dtype)
        dlb_ref[...] += jnp.broadcast_to(jnp.sum(dfg * (1.0 - sig), axis=0, keepdims=True), dlb_ref.shape)

    def seg(off):
        return pl.BlockSpec((CHUNK, HG_W), lambda c: (NC - 1 - c, off // HG_W))

    blk = pl.BlockSpec((CHUNK, HG_W), lambda c: (NC - 1 - c, 0))
    full = pltpu.VMEM((CHUNK, HG_W), F32)
    dproj, dlb, dnw = pl.pallas_call(
        body, name="hgrn_bwd", grid=(NC,),
        in_specs=[seg(OFF_Q), seg(OFF_F), seg(OFF_V), seg(OFF_G),
                  pl.BlockSpec((1, HG_W), lambda c: (0, 0)), pl.BlockSpec((1, LANE), lambda c: (0, 0)),
                  pl.BlockSpec((None, H, LANE, LANE), lambda c: (NC - 1 - c, 0, 0, 0)), blk, blk, ANY_SPACE],
        out_specs=[pl.BlockSpec((CHUNK, 4 * HG_W), lambda c: (NC - 1 - c, 0)),
                   pl.BlockSpec((8, HG_W), lambda c: (0, 0)), pl.BlockSpec((8, LANE), lambda c: (0, 0))],
        out_shape=[jax.ShapeDtypeStruct(dproj.shape, dproj.dtype), jax.ShapeDtypeStruct((8, HG_W), F32),
                   jax.ShapeDtypeStruct((8, LANE), F32)],
        input_output_aliases={9: 0},
        scratch_shapes=[pltpu.VMEM((H, LANE, LANE), F32)] + [full] * 7,
        compiler_params=_cp(("arbitrary",)),
    )(proj, proj, proj, proj, lb.reshape(1, HG_W), nw.reshape(1, LANE), states, o_pre, dbcat, dproj)
    return dproj, dlb[0], dnw[0]


CV_PAD = 32
ROWS = 256


def _colblk(S, off):
    return pl.BlockSpec((S, LANE), lambda j: (0, off // LANE + j))


def cv_fwd(proj, w32, bias):
    S = proj.shape[0]
    nchunk = S // ROWS

    def body(a_ref, g_ref, w_ref, b_ref, o_ref, zpad):
        zpad[pl.ds(0, CV_PAD), :] = jnp.zeros((CV_PAD, LANE), F32)

        def glu(c, _):
            r0 = pl.multiple_of(c * ROWS, ROWS)
            zpad[pl.ds(CV_PAD + r0, ROWS), :] = a_ref[pl.ds(r0, ROWS), :] * _sigmoid(g_ref[pl.ds(r0, ROWS), :])
            return 0

        lax.fori_loop(0, nchunk, glu, 0)

        def conv(c, _):
            r0 = pl.multiple_of(c * ROWS, ROWS)
            acc = jnp.broadcast_to(b_ref[...], (ROWS, LANE))
            for j in range(CV_KERNEL):
                acc = acc + w_ref[pl.ds(j, 1), :] * zpad[pl.ds(r0 + (CV_PAD - CV_KERNEL + 1) + j, ROWS), :]
            o_ref[pl.ds(r0, ROWS), :] = acc
            return 0

        lax.fori_loop(0, nchunk, conv, 0)

    return pl.pallas_call(
        body, name="cv_fwd", grid=(BRANCH_W // LANE,),
        in_specs=[_colblk(S, OFF_CV), _colblk(S, OFF_CV + BRANCH_W),
                  pl.BlockSpec((32, LANE), lambda j: (0, j)), pl.BlockSpec((1, LANE), lambda j: (0, j))],
        out_specs=pl.BlockSpec((S, LANE), lambda j: (0, j)), out_shape=jax.ShapeDtypeStruct((S, BRANCH_W), F32),
        scratch_shapes=[pltpu.VMEM((CV_PAD + S, LANE), F32)],
        compiler_params=_cp(("parallel",)),
    )(proj, proj, w32, bias.reshape(1, BRANCH_W))


def cv_bwd(proj, w32, dzc):
    S = proj.shape[0]
    nchunk = S // ROWS

    def body(a_ref, g_ref, w_ref, dz_ref, da_ref, dg_ref, dw_ref, db_ref, zpad, dpad):
        zpad[pl.ds(0, CV_PAD), :] = jnp.zeros((CV_PAD, LANE), F32)
        dpad[pl.ds(S, CV_PAD), :] = jnp.zeros((CV_PAD, LANE), F32)
        dw_ref[...] = jnp.zeros_like(dw_ref)

        def glu(c, dsum):
            r0 = pl.multiple_of(c * ROWS, ROWS)
            zpad[pl.ds(CV_PAD + r0, ROWS), :] = a_ref[pl.ds(r0, ROWS), :] * _sigmoid(g_ref[pl.ds(r0, ROWS), :])
            d = dz_ref[pl.ds(r0, ROWS), :]
            dpad[pl.ds(r0, ROWS), :] = d
            return dsum + jnp.sum(d, axis=0, keepdims=True)

        dsum = lax.fori_loop(0, nchunk, glu, jnp.zeros((1, LANE), F32))
        db_ref[...] = jnp.broadcast_to(dsum, db_ref.shape)

        def conv(c, _):
            r0 = pl.multiple_of(c * ROWS, ROWS)
            d = dpad[pl.ds(r0, ROWS), :]
            acc = jnp.zeros((ROWS, LANE), F32)
            for j in range(CV_KERNEL):
                acc = acc + w_ref[pl.ds(j, 1), :] * dpad[pl.ds(r0 + (CV_KERNEL - 1) - j, ROWS), :]
                zs = zpad[pl.ds(r0 + (CV_PAD - CV_KERNEL + 1) + j, ROWS), :]
                dw_ref[pl.ds(j, 1), :] += jnp.sum(d * zs, axis=0, keepdims=True)
            a = a_ref[pl.ds(r0, ROWS), :]
            sg = _sigmoid(g_ref[pl.ds(r0, ROWS), :])
            da_ref[pl.ds(r0, ROWS), :] = (acc * sg).astype(da_ref.dtype)
            dg_ref[pl.ds(r0, ROWS), :] = (acc * a * sg * (1.0 - sg)).astype(dg_ref.dtype)
            return 0

        lax.fori_loop(0, nchunk, conv, 0)

    blk = pl.BlockSpec((S, LANE), lambda j: (0, j))
    da, dg, dw, db = pl.pallas_call(
        body, name="cv_bwd", grid=(BRANCH_W // LANE,),
        in_specs=[_colblk(S, OFF_CV), _colblk(S, OFF_CV + BRANCH_W), pl.BlockSpec((32, LANE), lambda j: (0, j)), blk],
        out_specs=[blk, blk, pl.BlockSpec((32, LANE), lambda j: (0, j)), pl.BlockSpec((8, LANE), lambda j: (0, j))],
        out_shape=[jax.ShapeDtypeStruct((S, BRANCH_W), BF16), jax.ShapeDtypeStruct((S, BRANCH_W), BF16),
                   jax.ShapeDtypeStruct((32, BRANCH_W), F32), jax.ShapeDtypeStruct((8, BRANCH_W), F32)],
        scratch_shapes=[pltpu.VMEM((CV_PAD + S, LANE), F32), pltpu.VMEM((S + CV_PAD, LANE), F32)],
        compiler_params=_cp(("parallel",)),
    )(proj, proj, w32, dzc)
    return da, dg, dw, db[0]


def ln_silu_fwd(z, w, b, bcat, tm=512):
    S, C = z.shape
    tm = _tile(S, tm)

    def body(z_ref, w_ref, b_ref, _, o_ref):
        zv = z_ref[...]
        mu = jnp.mean(zv, axis=-1, keepdims=True)
        zc = zv - mu
        rstd = lax.rsqrt(jnp.mean(zc * zc, axis=-1, keepdims=True) + EPS)
        y = zc * rstd * w_ref[...] + b_ref[...]
        o_ref[...] = (y * _sigmoid(y)).astype(o_ref.dtype)

    row = pl.BlockSpec((tm, C), lambda i: (i, 0))
    vec = pl.BlockSpec((1, C), lambda i: (0, 0))
    return pl.pallas_call(
        body, name="ln_silu_fwd", grid=(S // tm,), in_specs=[row, vec, vec, ANY_SPACE],
        out_specs=pl.BlockSpec((tm, C), lambda i: (i, 1)), out_shape=jax.ShapeDtypeStruct(bcat.shape, bcat.dtype),
        input_output_aliases={3: 0}, compiler_params=_cp(("parallel",)),
    )(z, w.reshape(1, C), b.reshape(1, C), bcat)


def ln_silu_bwd(z, w, b, dbcat, tm=512):
    S, C = z.shape
    tm = _tile(S, tm)

    def body(z_ref, w_ref, b_ref, do_ref, dz_ref, dw_ref, db_ref):
        zv = z_ref[...]
        wv = w_ref[...]
        mu = jnp.mean(zv, axis=-1, keepdims=True)
        zc = zv - mu
        rstd = lax.rsqrt(jnp.mean(zc * zc, axis=-1, keepdims=True) + EPS)
        xh = zc * rstd
        y = xh * wv + b_ref[...]
        dy = do_ref[...].astype(F32) * _dsilu(y, _sigmoid(y))

        @pl.when(pl.program_id(0) == 0)
        def _():
            dw_ref[...] = jnp.zeros_like(dw_ref)
            db_ref[...] = jnp.zeros_like(db_ref)

        dw_ref[...] += jnp.sum(dy * xh, axis=0, keepdims=True)
        db_ref[...] += jnp.sum(dy, axis=0, keepdims=True)
        dxh = dy * wv
        dz_ref[...] = rstd * (dxh - jnp.mean(dxh, axis=-1, keepdims=True) - xh * jnp.mean(dxh * xh, axis=-1, keepdims=True))

    row = pl.BlockSpec((tm, C), lambda i: (i, 0))
    vec = pl.BlockSpec((1, C), lambda i: (0, 0))
    dz, dw, db = pl.pallas_call(
        body, name="ln_silu_bwd", grid=(S // tm,), in_specs=[row, vec, vec, pl.BlockSpec((tm, C), lambda i: (i, 1))],
        out_specs=[row, vec, vec],
        out_shape=[jax.ShapeDtypeStruct((S, C), F32), jax.ShapeDtypeStruct((1, C), F32), jax.ShapeDtypeStruct((1, C), F32)],
        compiler_params=_cp(("arbitrary",)),
    )(z, w.reshape(1, C), b.reshape(1, C), dbcat)
    return dz, dw.reshape(C), db.reshape(C)


PL_PAD = 16


def _pool_counts(r0, win):
    t = r0 + lax.broadcasted_iota(I32, (ROWS, LANE), 0)
    return jnp.minimum(t + 1, win).astype(F32)


def pool_fwd(proj, wg, scale, bcat):
    S = proj.shape[0]
    nchunk = S // ROWS

    def body(u_ref, w_ref, s_ref, _, o_ref, upad):
        g = pl.program_id(0)
        upad[pl.ds(0, PL_PAD), :] = jnp.zeros((PL_PAD, LANE), F32)

        def fill(c, _):
            r0 = pl.multiple_of(c * ROWS, ROWS)
            upad[pl.ds(PL_PAD + r0, ROWS), :] = u_ref[pl.ds(r0, ROWS), :]
            return 0

        lax.fori_loop(0, nchunk, fill, 0)
        wb = w_ref[...].astype(BF16)
        for gi, win in enumerate(POOL_WINDOWS):
            @pl.when(g == gi)
            def _(win=win):
                def chunk(c, _):
                    r0 = pl.multiple_of(c * ROWS, ROWS)
                    u = upad[pl.ds(PL_PAD + r0, ROWS), :]
                    ws = u
                    for j in range(1, win):
                        ws = ws + upad[pl.ds(PL_PAD + r0 - j, ROWS), :]
                    pooled = ws / _pool_counts(r0, win) - u
                    o_ref[pl.ds(r0, ROWS), :] = (_dot(pooled.astype(BF16), wb, NN) * s_ref[...]).astype(o_ref.dtype)
                    return 0

                lax.fori_loop(0, nchunk, chunk, 0)

    return pl.pallas_call(
        body, name="pool_fwd", grid=(len(POOL_WINDOWS),),
        in_specs=[_colblk(S, OFF_PL), pl.BlockSpec((None, LANE, LANE), lambda j: (j, 0, 0)), pl.BlockSpec((1, LANE), lambda j: (0, j)),
                  ANY_SPACE],
        out_specs=pl.BlockSpec((S, LANE), lambda j: (0, 2 * BRANCH_W // LANE + j)),
        out_shape=jax.ShapeDtypeStruct(bcat.shape, bcat.dtype), input_output_aliases={3: 0},
        scratch_shapes=[pltpu.VMEM((PL_PAD + S, LANE), F32)],
        compiler_params=_cp(("parallel",)),
    )(proj, wg, scale.reshape(1, BRANCH_W), bcat)


def pool_bwd(proj, wg, scale, dbcat, dproj):
    S = proj.shape[0]
    nchunk = S // ROWS

    def body(u_ref, w_ref, s_ref, dy_ref, _, du_ref, dw_ref, ds_ref, upad, dpn, nd):
        g = pl.program_id(0)
        upad[pl.ds(0, PL_PAD), :] = jnp.zeros((PL_PAD, LANE), F32)
        dpn[pl.ds(S, PL_PAD), :] = jnp.zeros((PL_PAD, LANE), F32)

        def fill(c, _):
            r0 = pl.multiple_of(c * ROWS, ROWS)
            upad[pl.ds(PL_PAD + r0, ROWS), :] = u_ref[pl.ds(r0, ROWS), :]
            return 0

        lax.fori_loop(0, nchunk, fill, 0)
        wb = w_ref[...].astype(BF16)
        sv = s_ref[...]
        for gi, win in enumerate(POOL_WINDOWS):
            @pl.when(g == gi)
            def _(win=win):
                def chunk(c, carry):
                    dw, dsc = carry
                    r0 = pl.multiple_of(c * ROWS, ROWS)
                    u = upad[pl.ds(PL_PAD + r0, ROWS), :]
                    ws = u
                    for j in range(1, win):
                        ws = ws + upad[pl.ds(PL_PAD + r0 - j, ROWS), :]
                    cnt = _pool_counts(r0, win)
                    pooled = (ws / cnt - u).astype(BF16)
                    dyv = dy_ref[pl.ds(r0, ROWS), :].astype(F32)
                    dsc = dsc + jnp.sum(dyv * _dot(pooled, wb, NN), axis=0, keepdims=True)
                    dys = (dyv * sv).astype(BF16)
                    dw = dw + _dot(pooled, dys, TN)
                    dp = _dot(dys, wb, NT)
                    dpn[pl.ds(r0, ROWS), :] = dp / cnt
                    nd[pl.ds(r0, ROWS), :] = -dp
                    return dw, dsc

                dw, dsc = lax.fori_loop(0, nchunk, chunk, (jnp.zeros((LANE, LANE), F32), jnp.zeros((1, LANE), F32)))
                dw_ref[...] = dw
                ds_ref[...] = jnp.broadcast_to(dsc, ds_ref.shape)

                def spread(c, _):
                    r0 = pl.multiple_of(c * ROWS, ROWS)
                    acc = nd[pl.ds(r0, ROWS), :]
                    for j in range(win):
                        acc = acc + dpn[pl.ds(r0 + j, ROWS), :]
                    du_ref[pl.ds(r0, ROWS), :] = acc.astype(du_ref.dtype)
                    return 0

                lax.fori_loop(0, nchunk, spread, 0)

    dproj, dw, ds = pl.pallas_call(
        body, name="pool_bwd", grid=(len(POOL_WINDOWS),),
        in_specs=[_colblk(S, OFF_PL), pl.BlockSpec((None, LANE, LANE), lambda j: (j, 0, 0)), pl.BlockSpec((1, LANE), lambda j: (0, j)),
                  pl.BlockSpec((S, LANE), lambda j: (0, 2 * BRANCH_W // LANE + j)), ANY_SPACE],
        out_specs=[_colblk(S, OFF_PL), pl.BlockSpec((None, LANE, LANE), lambda j: (j, 0, 0)), pl.BlockSpec((8, LANE), lambda j: (0, j))],
        out_shape=[jax.ShapeDtypeStruct(dproj.shape, dproj.dtype), jax.ShapeDtypeStruct((len(POOL_WINDOWS), LANE, LANE), F32),
                   jax.ShapeDtypeStruct((8, BRANCH_W), F32)],
        input_output_aliases={4: 0},
        scratch_shapes=[pltpu.VMEM((PL_PAD + S, LANE), F32), pltpu.VMEM((S + PL_PAD, LANE), F32), pltpu.VMEM((S, LANE), F32)],
        compiler_params=_cp(("parallel",)),
    )(proj, wg, scale.reshape(1, BRANCH_W), dbcat, dproj)
    return dproj, dw, ds[0]


LR_PAD = 8
SCAN_TILES = 4
GELU_C = math.sqrt(2.0 / math.pi)
GELU_A = 0.044715


def _gelu(y):
    return 0.5 * y * (1.0 + jnp.tanh(GELU_C * (y + GELU_A * y * y * y)))


def _dgelu(y):
    t = jnp.tanh(GELU_C * (y + GELU_A * y * y * y))
    return 0.5 * (1.0 + t) + 0.5 * y * (1.0 - t * t) * GELU_C * (1.0 + 3.0 * GELU_A * y * y)


def _lru_gates(xpad, r0, cw_ref, cb, wa, ba, wx, bx, sp8):
    xc = jnp.broadcast_to(cb, (ROWS, LANE))
    for j in range(LRU_CONV):
        xc = xc + cw_ref[pl.ds(j, 1), :] * xpad[pl.ds(r0 + (LR_PAD - LRU_CONV + 1) + j, ROWS), :]
    xb = xc.astype(BF16)
    r = _sigmoid(_dot(xb, wa, NN) + ba)
    ig = _sigmoid(_dot(xb, wx, NN) + bx)
    la = -sp8 * r
    a = jnp.exp(la)
    s = jnp.sqrt(-jnp.tanh(la) * (a * a + 1.0))
    return xc, r, ig, a, s


def _tile_scan(a, b, r8, up):
    for s in (1, 2, 4):
        keep = (r8 < 8 - s) if up else (r8 >= s)
        shift = 8 - s if up else s
        a_sh = jnp.where(keep, pltpu.roll(a, shift, 0), 1.0)
        b_sh = jnp.where(keep, pltpu.roll(b, shift, 0), 0.0)
        b = b + a * b_sh
        a = a * a_sh
    return a, b


def lru_fwd(proj, cw8, cb, wa_bd, ba, wx_bd, bx, sp8, bcat):
    S = proj.shape[0]
    nchunk = S // ROWS

    def body(x_ref, y_ref, cw_ref, cb_ref, wa_ref, ba_ref, wx_ref, bx_ref, sp_ref, _, o_ref, h_ref, xpad, a_s):
        xpad[pl.ds(0, LR_PAD), :] = jnp.zeros((LR_PAD, LANE), F32)

        def fill(c, _):
            r0 = pl.multiple_of(c * ROWS, ROWS)
            xpad[pl.ds(LR_PAD + r0, ROWS), :] = x_ref[pl.ds(r0, ROWS), :]
            return 0

        lax.fori_loop(0, nchunk, fill, 0)
        wa = wa_ref[...].astype(BF16)
        wx = wx_ref[...].astype(BF16)

        def gates(c, _):
            r0 = pl.multiple_of(c * ROWS, ROWS)
            xc, r, ig, a, s = _lru_gates(xpad, r0, cw_ref, cb_ref[...], wa, ba_ref[...], wx, bx_ref[...], sp_ref[...])
            a_s[pl.ds(r0, ROWS), :] = a
            h_ref[pl.ds(r0, ROWS), :] = s * (ig * xc)
            return 0

        lax.fori_loop(0, nchunk, gates, 0)

        r8 = lax.broadcasted_iota(I32, (8, LANE), 0)

        def scan(i, h):
            bases = [pl.multiple_of(i * (8 * SCAN_TILES) + 8 * j, 8) for j in range(SCAN_TILES)]
            maps = [_tile_scan(a_s[pl.ds(b, 8), :], h_ref[pl.ds(b, 8), :], r8, False) for b in bases]
            for b, (ca, cb_) in zip(bases, maps):
                out = cb_ + ca * h
                h_ref[pl.ds(b, 8), :] = out
                h = out[7:8, :]
            return h

        lax.fori_loop(0, S // (8 * SCAN_TILES), scan, jnp.zeros((1, LANE), F32))

        def gate_out(c, _):
            r0 = pl.multiple_of(c * ROWS, ROWS)
            o_ref[pl.ds(r0, ROWS), :] = (h_ref[pl.ds(r0, ROWS), :] * _gelu(y_ref[pl.ds(r0, ROWS), :])).astype(o_ref.dtype)
            return 0

        lax.fori_loop(0, nchunk, gate_out, 0)

    vec = pl.BlockSpec((1, LANE), lambda j: (0, j))
    mat = pl.BlockSpec((None, LANE, LANE), lambda j: (j, 0, 0))
    blk = pl.BlockSpec((S, LANE), lambda j: (0, j))
    return pl.pallas_call(
        body, name="lru_fwd", grid=(BRANCH_W // LANE,),
        in_specs=[_colblk(S, OFF_LX), _colblk(S, OFF_LY), pl.BlockSpec((8, LANE), lambda j: (0, j)), vec, mat, vec, mat, vec, vec,
                  ANY_SPACE],
        out_specs=[pl.BlockSpec((S, LANE), lambda j: (0, 3 * BRANCH_W // LANE + j)), blk],
        out_shape=[jax.ShapeDtypeStruct(bcat.shape, bcat.dtype), jax.ShapeDtypeStruct((S, BRANCH_W), F32)],
        input_output_aliases={9: 0},
        scratch_shapes=[pltpu.VMEM((LR_PAD + S, LANE), F32), pltpu.VMEM((S, LANE), F32)],
        compiler_params=_cp(("parallel",)),
    )(proj, proj, cw8, cb.reshape(1, -1), wa_bd, ba.reshape(1, -1), wx_bd, bx.reshape(1, -1), sp8.reshape(1, -1), bcat)


def lru_bwd(proj, cw8, cb, wa_bd, ba, wx_bd, bx, sp8, h, dbcat):
    S = proj.shape[0]
    nchunk = S // ROWS

    def body(x_ref, y_ref, cw_ref, cb_ref, wa_ref, ba_ref, wx_ref, bx_ref, sp_ref, h_ref, do_ref,
             dx_ref, dy_ref, dcw_ref, dcb_ref, dwa_ref, dba_ref, dwx_ref, dbx_ref, dsp_ref,
             xpad, a_s, g_s, hpad, dxc, xc_s, r_s, ig_s, s_s):
        xpad[pl.ds(0, LR_PAD), :] = jnp.zeros((LR_PAD, LANE), F32)
        hpad[pl.ds(0, LR_PAD), :] = jnp.zeros((LR_PAD, LANE), F32)
        dxc[pl.ds(S, LR_PAD), :] = jnp.zeros((LR_PAD, LANE), F32)
        dcw_ref[...] = jnp.zeros_like(dcw_ref)
        wa = wa_ref[...].astype(BF16)
        wx = wx_ref[...].astype(BF16)
        cbv, bav, bxv, spv = cb_ref[...], ba_ref[...], bx_ref[...], sp_ref[...]

        def fill(c, _):
            r0 = pl.multiple_of(c * ROWS, ROWS)
            xpad[pl.ds(LR_PAD + r0, ROWS), :] = x_ref[pl.ds(r0, ROWS), :]
            hv = h_ref[pl.ds(r0, ROWS), :]
            hpad[pl.ds(LR_PAD + r0, ROWS), :] = hv
            yv = y_ref[pl.ds(r0, ROWS), :]
            dov = do_ref[pl.ds(r0, ROWS), :].astype(F32)
            g_s[pl.ds(r0, ROWS), :] = dov * _gelu(yv)
            dy_ref[pl.ds(r0, ROWS), :] = (dov * hv * _dgelu(yv)).astype(dy_ref.dtype)
            return 0

        lax.fori_loop(0, nchunk, fill, 0)

        def gates(c, _):
            r0 = pl.multiple_of(c * ROWS, ROWS)
            rows = pl.ds(r0, ROWS)
            xc_s[rows, :], r_s[rows, :], ig_s[rows, :], a_s[rows, :], s_s[rows, :] = _lru_gates(
                xpad, r0, cw_ref, cbv, wa, bav, wx, bxv, spv)
            return 0

        lax.fori_loop(0, nchunk, gates, 0)

        r8 = lax.broadcasted_iota(I32, (8, LANE), 0)

        def rscan(i, carry):
            bases = [pl.multiple_of(S - 8 - i * (8 * SCAN_TILES) - 8 * j, 8) for j in range(SCAN_TILES)]
            firsts, maps = [], []
            for b in bases:
                a8 = a_s[pl.ds(b, 8), :]
                above = jnp.where(r8 < 7, pltpu.roll(a8, 7, 0), 1.0)
                firsts.append(a8[0:1, :])
                maps.append(_tile_scan(above, g_s[pl.ds(b, 8), :], r8, True))
            for b, a0, (ca, cb_) in zip(bases, firsts, maps):
                out = cb_ + ca * carry
                g_s[pl.ds(b, 8), :] = out
                carry = a0 * out[0:1, :]
            return carry

        lax.fori_loop(0, S // (8 * SCAN_TILES), rscan, jnp.zeros((1, LANE), F32))

        def chain(c, carry):
            dwa, dwx, dba, dbx, dsp, dcb = carry
            r0 = pl.multiple_of(c * ROWS, ROWS)
            rows = pl.ds(r0, ROWS)
            xc, r, ig, a, s = xc_s[rows, :], r_s[rows, :], ig_s[rows, :], a_s[rows, :], s_s[rows, :]
            gt = g_s[rows, :]
            hprev = hpad[pl.ds(r0 + LR_PAD - 1, ROWS), :]
            da = gt * hprev - gt * ig * xc * (a / s)
            dig = gt * s * xc
            dla = da * a
            dsp = dsp + jnp.sum(-dla * r, axis=0, keepdims=True)
            dpr = (-dla * spv) * r * (1.0 - r)
            dpi = dig * ig * (1.0 - ig)
            dprb, dpib, xb = dpr.astype(BF16), dpi.astype(BF16), xc.astype(BF16)
            d = gt * s * ig + _dot(dprb, wa, NT) + _dot(dpib, wx, NT)
            dwa = dwa + _dot(xb, dprb, TN)
            dwx = dwx + _dot(xb, dpib, TN)
            dba = dba + jnp.sum(dpr, axis=0, keepdims=True)
            dbx = dbx + jnp.sum(dpi, axis=0, keepdims=True)
            dcb = dcb + jnp.sum(d, axis=0, keepdims=True)
            dxc[pl.ds(r0, ROWS), :] = d
            for j in range(LRU_CONV):
                xs = xpad[pl.ds(r0 + (LR_PAD - LRU_CONV + 1) + j, ROWS), :]
                dcw_ref[pl.ds(j, 1), :] += jnp.sum(d * xs, axis=0, keepdims=True)
            return dwa, dwx, dba, dbx, dsp, dcb

        zm, zv = jnp.zeros((LANE, LANE), F32), jnp.zeros((1, LANE), F32)
        dwa, dwx, dba, dbx, dsp, dcb = lax.fori_loop(0, nchunk, chain, (zm, zm, zv, zv, zv, zv))
        dwa_ref[...] = dwa
        dwx_ref[...] = dwx
        dba_ref[...] = jnp.broadcast_to(dba, dba_ref.shape)
        dbx_ref[...] = jnp.broadcast_to(dbx, dbx_ref.shape)
        dsp_ref[...] = jnp.broadcast_to(dsp, dsp_ref.shape)
        dcb_ref[...] = jnp.broadcast_to(dcb, dcb_ref.shape)

        def convt(c, _):
            r0 = pl.multiple_of(c * ROWS, ROWS)
            acc = jnp.zeros((ROWS, LANE), F32)
            for j in range(LRU_CONV):
                acc = acc + cw_ref[pl.ds(j, 1), :] * dxc[pl.ds(r0 + (LRU_CONV - 1) - j, ROWS), :]
            dx_ref[pl.ds(r0, ROWS), :] = acc.astype(dx_ref.dtype)
            return 0

        lax.fori_loop(0, nchunk, convt, 0)

    vec = pl.BlockSpec((1, LANE), lambda j: (0, j))
    vec8 = pl.BlockSpec((8, LANE), lambda j: (0, j))
    mat = pl.BlockSpec((None, LANE, LANE), lambda j: (j, 0, 0))
    blk = pl.BlockSpec((S, LANE), lambda j: (0, j))
    nblk = BRANCH_W // LANE
    v8 = jax.ShapeDtypeStruct((8, BRANCH_W), F32)
    m4 = jax.ShapeDtypeStruct((nblk, LANE, LANE), F32)
    big = jax.ShapeDtypeStruct((S, BRANCH_W), BF16)
    seq = pltpu.VMEM((S, LANE), F32)
    dx, dy, dcw, dcb, dwa, dba, dwx, dbx, dsp = pl.pallas_call(
        body, name="lru_bwd", grid=(nblk,),
        in_specs=[_colblk(S, OFF_LX), _colblk(S, OFF_LY), vec8, vec, mat, vec, mat, vec, vec, blk,
                  pl.BlockSpec((S, LANE), lambda j: (0, 3 * BRANCH_W // LANE + j))],
        out_specs=[blk, blk, vec8, vec8, mat, vec8, mat, vec8, vec8],
        out_shape=[big, big, v8, v8, m4, v8, m4, v8, v8],
        scratch_shapes=[pltpu.VMEM((LR_PAD + S, LANE), F32), seq, seq, pltpu.VMEM((LR_PAD + S, LANE), F32),
                        pltpu.VMEM((S + LR_PAD, LANE), F32), seq, seq, seq, seq],
        compiler_params=_cp(("parallel",)),
    )(proj, proj, cw8, cb.reshape(1, -1), wa_bd, ba.reshape(1, -1), wx_bd, bx.reshape(1, -1), sp8.reshape(1, -1), h, dbcat)
    return dx, dy, dcw, dcb[0], dwa, dba[0], dwx, dbx[0], dsp[0]


MG_COLS = 512
N_BRANCH = 4
BCAT_W = N_BRANCH * BRANCH_W


def merge_fwd(bcat, wb, proj, gate_b, tm=2048, after=()):
    S = proj.shape[0]
    tm = _tile(S, tm)
    halves = D_MODEL // MG_COLS

    def body(a_ref, w_ref, g_ref, gb_ref, *rest):
        up_ref, sg_ref, o_ref, acc_ref = rest[len(after):]
        k = pl.program_id(2)
        up = _dot(a_ref[...], w_ref[...], NN)
        up_ref[...] = up.astype(up_ref.dtype)
        sg = _sigmoid(g_ref[...] + gb_ref[pl.ds(k, 1), :])
        sg_ref[...] = sg.astype(sg_ref.dtype)
        term = sg * up

        @pl.when(k == 0)
        def _():
            acc_ref[...] = term

        @pl.when(k > 0)
        def _():
            acc_ref[...] += term

        @pl.when(k == N_BRANCH - 1)
        def _():
            o_ref[...] = acc_ref[...].astype(o_ref.dtype)

    return pl.pallas_call(
        body, name="merge_fwd", grid=(S // tm, halves, N_BRANCH),
        in_specs=[pl.BlockSpec((tm, BRANCH_W), lambda i, j, k: (i, k)),
                  pl.BlockSpec((None, BRANCH_W, MG_COLS), lambda i, j, k: (k, 0, j)),
                  pl.BlockSpec((tm, MG_COLS), lambda i, j, k: (i, OFF_GATE // MG_COLS + k * halves + j)),
                  pl.BlockSpec((N_BRANCH, MG_COLS), lambda i, j, k: (0, j))] + [ANY_SPACE] * len(after),
        out_specs=[pl.BlockSpec((tm, MG_COLS), lambda i, j, k: (i, k * halves + j)),
                   pl.BlockSpec((tm, MG_COLS), lambda i, j, k: (i, k * halves + j)),
                   pl.BlockSpec((tm, MG_COLS), lambda i, j, k: (i, j))],
        out_shape=[jax.ShapeDtypeStruct((S, N_BRANCH * D_MODEL), BF16), jax.ShapeDtypeStruct((S, N_BRANCH * D_MODEL), BF16),
                   jax.ShapeDtypeStruct((S, D_MODEL), BF16)],
        scratch_shapes=[pltpu.VMEM((tm, MG_COLS), F32)],
        compiler_params=_cp(("parallel", "parallel", "arbitrary")),
    )(bcat, wb, proj, gate_b, *after)


def merge_bwd(dmerged, ups, gates, tm=1024):
    S = dmerged.shape[0]
    tm = _tile(S, tm)
    halves = D_MODEL // MG_COLS

    def body(dm_ref, u_ref, sg_ref, du_ref, dg_ref, dgb_ref):
        @pl.when(pl.program_id(2) == 0)
        def _():
            dgb_ref[...] = jnp.zeros_like(dgb_ref)

        dm = dm_ref[...].astype(F32)
        sg = sg_ref[...].astype(F32)
        du_ref[...] = (dm * sg).astype(du_ref.dtype)
        dgk = dm * u_ref[...] * sg * (1.0 - sg)
        dg_ref[...] = dgk.astype(dg_ref.dtype)
        dgb_ref[...] += jnp.broadcast_to(jnp.sum(dgk, axis=0, keepdims=True), dgb_ref.shape)

    dups, dproj, dgb = pl.pallas_call(
        body, name="merge_bwd", grid=(N_BRANCH, halves, S // tm),
        in_specs=[pl.BlockSpec((tm, MG_COLS), lambda k, j, i: (i, j)),
                  pl.BlockSpec((tm, MG_COLS), lambda k, j, i: (i, k * halves + j)),
                  pl.BlockSpec((tm, MG_COLS), lambda k, j, i: (i, k * halves + j))],
        out_specs=[pl.BlockSpec((tm, MG_COLS), lambda k, j, i: (i, k * halves + j)),
                   pl.BlockSpec((tm, MG_COLS), lambda k, j, i: (i, OFF_GATE // MG_COLS + k * halves + j)),
                   pl.BlockSpec((8, MG_COLS), lambda k, j, i: (k, j))],
        out_shape=[jax.ShapeDtypeStruct((S, N_BRANCH * D_MODEL), BF16), jax.ShapeDtypeStruct((S, IN_W), BF16),
                   jax.ShapeDtypeStruct((8 * N_BRANCH, D_MODEL), F32)],
        compiler_params=_cp(("parallel", "parallel", "arbitrary")),
    )(dmerged, ups, gates)
    return dups, dproj, dgb.reshape(N_BRANCH, 8, D_MODEL)[:, 0]


def attn_fwd(q, kv, tm=512):
    S = q.shape[0]
    M = kv.shape[0]
    tm = _tile(S, tm)
    scale = XA_HD ** -0.5

    def body(q_ref, kv_ref, o_ref):
        for hh in range(XA_HEADS):
            cs = pl.ds(hh * XA_HD, XA_HD)
            qh = q_ref[:, cs]
            kh = kv_ref[:, cs]
            vh = kv_ref[:, pl.ds(D_MODEL + hh * XA_HD, XA_HD)]
            s = _dot(qh, kh, NT) * scale
            p = jnp.exp(s - jnp.max(s, axis=-1, keepdims=True))
            p = p / jnp.sum(p, axis=-1, keepdims=True)
            o_ref[:, cs] = _dot(p.astype(BF16), vh, NN).astype(o_ref.dtype)

    return pl.pallas_call(
        body, name="attn_fwd", grid=(S // tm,),
        in_specs=[pl.BlockSpec((tm, D_MODEL), lambda i: (i, 0)), pl.BlockSpec((M, 2 * D_MODEL), lambda i: (0, 0))],
        out_specs=pl.BlockSpec((tm, D_MODEL), lambda i: (i, 0)), out_shape=jax.ShapeDtypeStruct((S, D_MODEL), BF16),
        compiler_params=_cp(("parallel",)),
    )(q, kv)


def attn_bwd(q, kv, do, tm=512):
    S = q.shape[0]
    M = kv.shape[0]
    tm = _tile(S, tm)
    scale = XA_HD ** -0.5

    def body(q_ref, kv_ref, do_ref, dq_ref, dkv_ref):
        @pl.when(pl.program_id(0) == 0)
        def _():
            dkv_ref[...] = jnp.zeros_like(dkv_ref)

        for hh in range(XA_HEADS):
            cs = pl.ds(hh * XA_HD, XA_HD)
            vs = pl.ds(D_MODEL + hh * XA_HD, XA_HD)
            qh = q_ref[:, cs]
            kh = kv_ref[:, cs]
            vh = kv_ref[:, vs]
            doh = do_ref[:, cs]
            s = _dot(qh, kh, NT) * scale
            p = jnp.exp(s - jnp.max(s, axis=-1, keepdims=True))
            p = p / jnp.sum(p, axis=-1, keepdims=True)
            dp = _dot(doh, vh, NT)
            ds = (p * (dp - jnp.sum(dp * p, axis=-1, keepdims=True)) * scale).astype(BF16)
            dq_ref[:, cs] = _dot(ds, kh, NN).astype(dq_ref.dtype)
            dkv_ref[:, cs] += _dot(ds, qh, TN)
            dkv_ref[:, vs] += _dot(p.astype(BF16), doh, TN)

    row = pl.BlockSpec((tm, D_MODEL), lambda i: (i, 0))
    full = pl.BlockSpec((M, 2 * D_MODEL), lambda i: (0, 0))
    return pl.pallas_call(
        body, name="attn_bwd", grid=(S // tm,), in_specs=[row, full, row], out_specs=[row, full],
        out_shape=[jax.ShapeDtypeStruct((S, D_MODEL), BF16), jax.ShapeDtypeStruct((M, 2 * D_MODEL), F32)],
        compiler_params=_cp(("arbitrary",)),
    )(q, kv, do)


def sum_parts(parts, own=None, tm=256):
    n, R, C = parts.shape
    tm = _tile(R, tm)
    has_own = own is not None

    def body(*refs):
        p_ref, o_ref = refs[0], refs[-1]
        acc = refs[1][...].astype(F32) if has_own else p_ref[0].astype(F32)
        for j in range(0 if has_own else 1, n):
            acc = acc + p_ref[j].astype(F32)
        o_ref[...] = acc

    row = pl.BlockSpec((tm, C), lambda i: (i, 0))
    return pl.pallas_call(
        body, name="sum_parts", grid=(R // tm,),
        in_specs=[pl.BlockSpec((n, tm, C), lambda i: (0, i, 0))] + ([row] if has_own else []), out_specs=row,
        out_shape=jax.ShapeDtypeStruct((R, C), F32), compiler_params=_cp(("parallel",)),
    )(*([parts, own] if has_own else [parts]))


def adamw(w, g, m, v, tm=256):
    R, C = w.shape
    tm = _tile(R, tm)
    c1 = 1.0 / (1.0 - ADAM_B1 ** ADAM_STEP)
    c2 = 1.0 / (1.0 - ADAM_B2 ** ADAM_STEP)

    def body(w_ref, g_ref, m_ref, v_ref, d_ref, nm_ref, nv_ref):
        gv = g_ref[...]
        nm = ADAM_B1 * m_ref[...] + (1.0 - ADAM_B1) * gv
        nv = ADAM_B2 * v_ref[...] + (1.0 - ADAM_B2) * (gv * gv)
        nm_ref[...] = nm
        nv_ref[...] = nv
        d_ref[...] = -ADAM_LR * ((nm * c1) / (jnp.sqrt(nv * c2) + ADAM_EPS) + ADAM_WD * w_ref[...])

    blk = pl.BlockSpec((tm, C), lambda i: (i, 0))
    sd = jax.ShapeDtypeStruct((R, C), F32)
    return pl.pallas_call(
        body, name="adamw", grid=(R // tm,), in_specs=[blk] * 4, out_specs=[blk] * 3, out_shape=[sd] * 3,
        compiler_params=_cp(("parallel",)),
    )(w, g, m, v)


ANY = pl.BlockSpec(memory_space=pl.ANY)


def _place():
    return lax.axis_index("x"), lax.axis_index("y"), lax.axis_index("c")


def _slot(px, py, pc):
    return 4 * px + 2 * py + pc


def all_gather(name, shards, after=()):
    n = len(shards)
    n_in = n + len(after)

    def body(*refs):
        x_refs, out_refs = refs[:n], refs[n_in:n_in + n]
        send_sems, recv_sems, local_sems = refs[n_in + n:]
        x, y, c = _place()
        me, sibling = (x, y, c), (x, y, 1 - c)
        chips = [(1 - x, y), (x, 1 - y), (1 - x, 1 - y)]

        def copy(a, k, block, to, src=None):
            rows = out_refs[a].at[_slot(*block)]
            return pltpu.make_async_remote_copy(
                src_ref=rows if src is None else src, dst_ref=rows,
                send_sem=send_sems.at[7 * a + k], recv_sem=recv_sems.at[7 * a + k],
                device_id=to, device_id_type=MESH)

        mine = [pltpu.make_async_copy(x_refs[a], out_refs[a].at[_slot(*me)], local_sems.at[a]) for a in range(n)]
        for cp in mine:
            cp.start()
        first = []
        for a in range(n):
            first.append(copy(a, 0, me, sibling, src=x_refs[a]))
            first += [copy(a, 1 + j, me, (*chip, c), src=x_refs[a]) for j, chip in enumerate(chips)]
        for cp in first:
            cp.start()
        passed = []
        for a in range(n):
            for j, chip in enumerate(chips):
                copy(a, 1 + j, (*chip, c), me).wait_recv()
                cp = copy(a, 4 + j, (*chip, c), sibling)
                cp.start()
                passed.append(cp)
        for a in range(n):
            copy(a, 0, sibling, me).wait_recv()
            for j, chip in enumerate(chips):
                copy(a, 4 + j, (*chip, 1 - c), me).wait_recv()
        for cp in first + passed:
            cp.wait_send()
        for cp in mine:
            cp.wait()

    return pl.pallas_call(
        body, name=name, in_specs=[ANY] * n_in, out_specs=[ANY] * n,
        out_shape=[jax.ShapeDtypeStruct((N_DEV, *s.shape), s.dtype) for s in shards],
        scratch_shapes=[pltpu.SemaphoreType.DMA((7 * n,)), pltpu.SemaphoreType.DMA((7 * n,)), pltpu.SemaphoreType.DMA((n,))],
    )(*shards, *after)


HBM = pl.BlockSpec(memory_space=pltpu.HBM)
SEM = pl.BlockSpec(memory_space=pltpu.SEMAPHORE)
EFFECT = pltpu.SideEffectType.DATAFLOW_SIDE_EFFECTING
N_PEER = N_DEV - 1
RELATIONS = [(dx, dy, dc) for dx in (0, 1) for dy in (0, 1) for dc in (0, 1)][1:]


def _peer(place, rel):
    return tuple(1 - v if d else v for v, d in zip(place, rel))


def gather_start(name, shards, me, before):
    n = len(shards)

    def body(*refs):
        x_refs, land_refs = refs[:n], refs[n:2 * n]
        send_sems, recv_sems = refs[2 * n + len(before):2 * n + len(before) + 2]
        token = refs[-1]
        place = _place()
        mine = _slot(*place)
        for a in range(n):
            for rel in RELATIONS:
                pltpu.make_async_remote_copy(
                    src_ref=x_refs[a], dst_ref=land_refs[a].at[mine], send_sem=send_sems.at[a], recv_sem=recv_sems.at[a],
                    device_id=_peer(place, rel), device_id_type=MESH).start()
        token[...] = jnp.zeros_like(token)

    lands = [lax.dynamic_update_index_in_dim(lax.empty((N_DEV, *s.shape), s.dtype), s, me, 0) for s in shards]
    outs = pl.pallas_call(
        body, name=name,
        in_specs=[HBM] * (2 * n) + [ANY] * len(before),
        out_specs=[SEM, SEM] + [HBM] * (2 * n) + [pl.BlockSpec(memory_space=pltpu.VMEM)],
        out_shape=[pltpu.SemaphoreType.DMA((n,)), pltpu.SemaphoreType.DMA((n,))]
        + [pltpu.HBM(t.shape, t.dtype) for t in (*shards, *lands)] + [jax.ShapeDtypeStruct((8, LANE), F32)],
        input_output_aliases={i: 2 + i for i in range(2 * n)},
        compiler_params=pltpu.CompilerParams(has_side_effects=EFFECT),
    )(*[pltpu.with_memory_space_constraint(t, pltpu.HBM) for t in (*shards, *lands)], *before)
    return (outs[0], outs[1], outs[2:2 + n], outs[2 + n:2 + 2 * n]), outs[-1]


def gather_wait(name, state, after):
    send_sems, recv_sems, shards, lands = state
    n = len(shards)

    def body(*refs):
        land_refs = refs[n:2 * n]
        s_sems, r_sems = refs[2 * n:2 * n + 2]
        place = _place()
        for a in range(n):
            seven = land_refs[a].at[pl.ds(0, N_PEER)]
            cp = pltpu.make_async_remote_copy(
                src_ref=seven, dst_ref=seven, send_sem=s_sems.at[a], recv_sem=r_sems.at[a], device_id=place, device_id_type=MESH)
            cp.wait_send()
            cp.wait_recv()

    outs = pl.pallas_call(
        body, name=name,
        in_specs=[HBM] * (2 * n) + [SEM, SEM] + [ANY] * len(after), out_specs=[HBM] * (2 * n),
        out_shape=[pltpu.HBM(t.shape, t.dtype) for t in (*shards, *lands)],
        input_output_aliases={i: i for i in range(2 * n)},
        compiler_params=pltpu.CompilerParams(has_side_effects=EFFECT),
    )(*shards, *lands, send_sems, recv_sems, *after)
    return outs[n:]


def exchange_start(name, grads, before):
    n = len(grads)

    def body(*refs):
        g_refs, land_refs = refs[:n], refs[n:2 * n]
        send_sems, recv_sems = refs[2 * n + len(before):2 * n + len(before) + 2]
        token = refs[-1]
        place = _place()
        for a in range(n):
            for r, rel in enumerate(RELATIONS):
                p = _peer(place, rel)
                pltpu.make_async_remote_copy(
                    src_ref=g_refs[a].at[_slot(*p)], dst_ref=land_refs[a].at[r],
                    send_sem=send_sems.at[a], recv_sem=recv_sems.at[a], device_id=p, device_id_type=MESH).start()
        token[...] = jnp.zeros_like(token)

    lands = [lax.empty((N_PEER, *g.shape[1:]), g.dtype) for g in grads]
    outs = pl.pallas_call(
        body, name=name,
        in_specs=[HBM] * (2 * n) + [ANY] * len(before),
        out_specs=[SEM, SEM] + [HBM] * (2 * n) + [pl.BlockSpec(memory_space=pltpu.VMEM)],
        out_shape=[pltpu.SemaphoreType.DMA((n,)), pltpu.SemaphoreType.DMA((n,))]
        + [pltpu.HBM(g.shape, g.dtype) for g in grads] + [pltpu.HBM(t.shape, t.dtype) for t in lands]
        + [jax.ShapeDtypeStruct((8, LANE), F32)],
        input_output_aliases={i: 2 + i for i in range(2 * n)},
        compiler_params=pltpu.CompilerParams(has_side_effects=EFFECT),
    )(*[pltpu.with_memory_space_constraint(t, pltpu.HBM) for t in (*grads, *lands)], *before)
    return (outs[0], outs[1], outs[2:2 + n], outs[2 + n:2 + 2 * n]), outs[-1]


def exchange_wait(name, state, after):
    send_sems, recv_sems, grads, lands = state
    n = len(grads)

    def body(*refs):
        g_refs, land_refs = refs[:n], refs[n:2 * n]
        s_sems, r_sems = refs[2 * n:2 * n + 2]
        place = _place()
        for a in range(n):
            cp = pltpu.make_async_remote_copy(
                src_ref=g_refs[a].at[pl.ds(0, N_PEER)], dst_ref=land_refs[a],
                send_sem=s_sems.at[a], recv_sem=r_sems.at[a], device_id=place, device_id_type=MESH)
            cp.wait_send()
            cp.wait_recv()

    outs = pl.pallas_call(
        body, name=name,
        in_specs=[HBM] * (2 * n) + [SEM, SEM, ANY], out_specs=[HBM] * (2 * n),
        out_shape=[pltpu.HBM(t.shape, t.dtype) for t in (*grads, *lands)],
        input_output_aliases={i: i for i in range(2 * n)},
        compiler_params=pltpu.CompilerParams(has_side_effects=EFFECT),
    )(*grads, *lands, send_sems, recv_sems, after)
    return outs[:n], outs[n:]


WEIGHTS = ['norm_mix_w', 'w_in', 'hg_lb_raw', 'hg_norm_w', 'cv_dw_w', 'cv_dw_b', 'cv_ln_w', 'cv_ln_b', 'pl_w', 'pl_scale',
           'lru_conv_w', 'lru_conv_b', 'lru_wa', 'lru_ba', 'lru_wx', 'lru_bx', 'lru_lambda', 'gate_b', 'w_branch', 'w_out',
           'norm_mem_w', 'mem_norm_w', 'xa_wq', 'xa_wkv', 'xa_wo', 'norm_ffn_w', 'ffn_w1', 'ffn_w2', 'final_norm_w']
BIG = ('w_in', 'w_branch', 'w_out', 'xa_wq', 'xa_wkv', 'xa_wo', 'ffn_w1', 'ffn_w2')
SMALL_SHARDED = ('cv_dw_w', 'lru_conv_w', 'gate_b')
SMALL = tuple(n for n in WEIGHTS if n not in BIG and n not in SMALL_SHARDED)
PACK_ROWS = 256


def _pack(arrs):
    flat = jnp.concatenate([a.reshape(-1).astype(F32) for a in arrs])
    tile = PACK_ROWS * LANE
    padded = -(-flat.shape[0] // tile) * tile
    return jnp.pad(flat, (0, padded - flat.shape[0])).reshape(-1, LANE)


def _unpack(packed, shapes):
    flat = packed.reshape(-1)
    out, off = [], 0
    for s in shapes:
        n = math.prod(s)
        out.append(flat[off:off + n].reshape(s))
        off += n
    return out


def _gather_last(g, shard_shape):
    nd = len(shard_shape)
    full = jnp.moveaxis(g, 0, nd - 1)
    return full.reshape(*shard_shape[:-1], N_DEV * shard_shape[-1])


def _natural(blocks):
    nb, k, c = blocks.shape
    return jnp.transpose(blocks, (1, 0, 2)).reshape(k, nb * c)


def _block_diag(w):
    w2 = w.reshape(4, 2, 64, 64)
    z = jnp.zeros((4, 64, 64), w.dtype)
    return jnp.concatenate([jnp.concatenate([w2[:, 0], z], axis=2), jnp.concatenate([z, w2[:, 1]], axis=2)], axis=1)


def _block_diag_t(d):
    return jnp.stack([d[:, :64, :64], d[:, 64:, 64:]], axis=1).reshape(8, 64, 64)


def _lower_bounds(raw):
    lb = jnp.cumsum(jax.nn.softmax(raw.astype(F32), axis=0), axis=0)
    return lb - lb[0:1]


def _decay_rates(lam):
    return (LRU_C * jax.nn.softplus(-lam.astype(F32))).reshape(DEPTH, BRANCH_W)


def _relu2(acc):
    r = jnp.maximum(acc, 0.0)
    return acc, r * r


def _relu2_grad(acc, u):
    return (acc * 2.0 * jnp.maximum(u, 0.0),)


def _add(acc, e):
    return (acc + e,)


def _layer_fwd(x0, mem, p, g, rest, after=(), late=None):
    h1 = rms_fwd("rms_mix", x0, p['norm_mix_w'])
    proj = mm_nt("mm_in", h1, g['w_in'], tn=2176, after=after)[0]
    bcat, states, o_hg = hgrn_fwd(proj, p['lb'], p['hg_norm_w'])
    zc = cv_fwd(proj, p['cv_w32'], p['cv_dw_b'])
    bcat = ln_silu_fwd(zc, p['cv_ln_w'], p['cv_ln_b'], bcat)
    bcat = pool_fwd(proj, p['pl_w'], p['pl_scale'], bcat)
    bcat, hst = lru_fwd(proj, p['lru_cw8'], p['lru_conv_b'], p['wa_bd'], p['lru_ba'], p['wx_bd'], p['lru_bx'], p['sp8'], bcat)
    more, after = rest(bcat)
    g = {**g, **more}
    ups, gates, merged = merge_fwd(bcat, g['w_branch'], proj, p['gate_b'], after=after)
    x1 = mm_nn("mm_out", merged, g['w_out'], epi=_add, extras=(x0,))[0]
    if late is not None:
        g = {**g, **late(x1)}
    h2 = rms_fwd("rms_mem", x1, p['norm_mem_w'])
    q = mm_nn("mm_q", h2, g['xa_wq'], out_dtype=BF16)[0]
    memn = rms_fwd("rms_memtok", mem, p['mem_norm_w'])
    kv = mm_nn("mm_kv", memn, g['xa_wkv'], out_dtype=BF16, tn=2048)[0]
    oa = attn_fwd(q, kv)
    x2 = mm_nn("mm_o", oa, g['xa_wo'], epi=_add, extras=(x1,))[0]
    h3 = rms_fwd("rms_ffn", x2, p['norm_ffn_w'])
    u, act = mm_nn("mm_ffn1", h3, g['ffn_w1'], epi=_relu2, out_dtypes=[BF16, BF16])
    x3 = mm_nn("mm_ffn2", act, g['ffn_w2'], epi=_add, extras=(x2,))[0]
    res = dict(x0=x0, h1=h1, proj=proj, states=states, o_hg=o_hg, zc=zc, hst=hst, bcat=bcat, ups=ups, gates=gates, merged=merged,
               x1=x1, h2=h2, q=q, memn=memn, kv=kv, oa=oa, x2=x2, h3=h3, u=u, act=act)
    return x3, res, g


def _layer_bwd(dx3, mem, p, g, r, midway, finish):
    gs, gb = {}, {}
    du = mm_nt("mm_dffn2", dx3, g['ffn_w2'], out_dtype=BF16, epi=_relu2_grad, extras=(r['u'],))[0]
    gb['ffn_w2'] = mm_tn("mm_gw2", r['act'], dx3).reshape(N_DEV, -1, D_MODEL)
    gb['ffn_w1'] = mm_tn_cb("mm_gw1", r['h3'], du, N_DEV)
    dh3 = mm_nt("mm_dffn1", du, g['ffn_w1'], out_dtype=BF16)[0]
    dx2, gs['norm_ffn_w'] = rms_bwd("rmsb_ffn", r['x2'], p['norm_ffn_w'], dh3, dx3)
    doa = mm_nt("mm_do", dx2, g['xa_wo'], out_dtype=BF16)[0]
    gb['xa_wo'] = mm_tn("mm_gwo", r['oa'], dx2).reshape(N_DEV, -1, D_MODEL)
    dq, dkv = attn_bwd(r['q'], r['kv'], doa)
    gb['xa_wq'] = mm_tn("mm_gwq", r['h2'], dq).reshape(N_DEV, -1, D_MODEL)
    dh2 = mm_nt("mm_dq", dq, g['xa_wq'], out_dtype=BF16)[0]
    gb['xa_wkv'] = mm_tn_cb("mm_gwkv", r['memn'], dkv, N_DEV)
    dmemn = mm_nt("mm_dkv", dkv, g['xa_wkv'], out_dtype=BF16)[0]
    _, gs['mem_norm_w'] = rms_bwd("rmsb_memtok", mem, p['mem_norm_w'], dmemn)
    dx1, gs['norm_mem_w'] = rms_bwd("rmsb_mem", r['x1'], p['norm_mem_w'], dh2, dx2)
    after = midway(gb, dx1)
    gb = {}
    dmerged = mm_nt("mm_dout", dx1, g['w_out'], out_dtype=BF16, after=after)[0]
    gb['w_out'] = mm_tn("mm_gwout", r['merged'], dx1).reshape(N_DEV, -1, D_MODEL)
    dups, dproj, gs['gate_b'] = merge_bwd(dmerged, r['ups'], r['gates'])
    gwb = mm_branch_tn("mm_gwb", r['bcat'], dups, N_BRANCH)
    gb['w_branch'] = jnp.transpose(gwb.reshape(N_BRANCH, BRANCH_W, N_DEV, -1), (2, 0, 1, 3))
    dbcat = mm_branch_nt("mm_dup", dups, g['w_branch'], tm=2048)
    dproj, gs['lb'], gs['hg_norm_w'] = hgrn_bwd(r['proj'], p['lb'], p['hg_norm_w'], r['states'], r['o_hg'], dbcat, dproj)
    dzc, gs['cv_ln_w'], gs['cv_ln_b'] = ln_silu_bwd(r['zc'], p['cv_ln_w'], p['cv_ln_b'], dbcat)
    dca, dcg, dcw, gs['cv_dw_b'] = cv_bwd(r['proj'], p['cv_w32'], dzc)
    gs['cv_dw_w'] = dcw[:CV_KERNEL]
    dproj, gs['pl_w'], gs['pl_scale'] = pool_bwd(r['proj'], p['pl_w'], p['pl_scale'], dbcat, dproj)
    dlx, dly, dlcw, gs['lru_conv_b'], dwa, gs['lru_ba'], dwx, gs['lru_bx'], gs['sp8'] = lru_bwd(
        r['proj'], p['lru_cw8'], p['lru_conv_b'], p['wa_bd'], p['lru_ba'], p['wx_bd'], p['lru_bx'], p['sp8'], r['hst'], dbcat)
    gs['lru_conv_w'] = dlcw[:LRU_CONV]
    gs['lru_wa'], gs['lru_wx'] = _block_diag_t(dwa), _block_diag_t(dwx)
    gs['lru_ba'], gs['lru_bx'] = gs['lru_ba'].reshape(8, 64), gs['lru_bx'].reshape(8, 64)
    for off, piece in ((OFF_CV, dca), (OFF_CV + BRANCH_W, dcg), (OFF_LX, dlx), (OFF_LY, dly)):
        dproj = lax.dynamic_update_slice(dproj, piece, (0, off))
    gb['w_in'] = mm_tn("mm_gwin", dproj, r['h1'], tm=2176, tk=1024).reshape(N_DEV, -1, D_MODEL)
    dh1 = mm_nn("mm_din", dproj, g['w_in'], out_dtype=BF16, tk=4352, after=finish(gb, dx1))[0]
    dx0, gs['norm_mix_w'] = rms_bwd("rmsb_mix", r['x0'], p['norm_mix_w'], dh1, dx1)
    return dx0, gs


def kernel(x, mem, norm_mix_w, w_in, hg_lb_raw, hg_norm_w, cv_dw_w, cv_dw_b, cv_ln_w, cv_ln_b, pl_w, pl_scale, lru_conv_w, lru_conv_b, lru_wa, lru_ba, lru_wx, lru_bx, lru_lambda, gate_b, w_branch, w_out, norm_mem_w, mem_norm_w, xa_wq, xa_wkv, xa_wo, norm_ffn_w, ffn_w1, ffn_w2, final_norm_w, loss_target, m_norm_mix_w, m_w_in, m_hg_lb_raw, m_hg_norm_w, m_cv_dw_w, m_cv_dw_b, m_cv_ln_w, m_cv_ln_b, m_pl_w, m_pl_scale, m_lru_conv_w, m_lru_conv_b, m_lru_wa, m_lru_ba, m_lru_wx, m_lru_bx, m_lru_lambda, m_gate_b, m_w_branch, m_w_out, m_norm_mem_w, m_mem_norm_w, m_xa_wq, m_xa_wkv, m_xa_wo, m_norm_ffn_w, m_ffn_w1, m_ffn_w2, m_final_norm_w, v_norm_mix_w, v_w_in, v_hg_lb_raw, v_hg_norm_w, v_cv_dw_w, v_cv_dw_b, v_cv_ln_w, v_cv_ln_b, v_pl_w, v_pl_scale, v_lru_conv_w, v_lru_conv_b, v_lru_wa, v_lru_ba, v_lru_wx, v_lru_bx, v_lru_lambda, v_gate_b, v_w_branch, v_w_out, v_norm_mem_w, v_mem_norm_w, v_xa_wq, v_xa_wkv, v_xa_wo, v_norm_ffn_w, v_ffn_w1, v_ffn_w2, v_final_norm_w):
    W = dict(zip(WEIGHTS, (norm_mix_w, w_in, hg_lb_raw, hg_norm_w, cv_dw_w, cv_dw_b, cv_ln_w, cv_ln_b, pl_w, pl_scale, lru_conv_w, lru_conv_b, lru_wa, lru_ba, lru_wx, lru_bx, lru_lambda, gate_b, w_branch, w_out, norm_mem_w, mem_norm_w, xa_wq, xa_wkv, xa_wo, norm_ffn_w, ffn_w1, ffn_w2, final_norm_w)))
    Mo = dict(zip(WEIGHTS, (m_norm_mix_w, m_w_in, m_hg_lb_raw, m_hg_norm_w, m_cv_dw_w, m_cv_dw_b, m_cv_ln_w, m_cv_ln_b, m_pl_w, m_pl_scale, m_lru_conv_w, m_lru_conv_b, m_lru_wa, m_lru_ba, m_lru_wx, m_lru_bx, m_lru_lambda, m_gate_b, m_w_branch, m_w_out, m_norm_mem_w, m_mem_norm_w, m_xa_wq, m_xa_wkv, m_xa_wo, m_norm_ffn_w, m_ffn_w1, m_ffn_w2, m_final_norm_w)))
    Vo = dict(zip(WEIGHTS, (v_norm_mix_w, v_w_in, v_hg_lb_raw, v_hg_norm_w, v_cv_dw_w, v_cv_dw_b, v_cv_ln_w, v_cv_ln_b, v_pl_w, v_pl_scale, v_lru_conv_w, v_lru_conv_b, v_lru_wa, v_lru_ba, v_lru_wx, v_lru_bx, v_lru_lambda, v_gate_b, v_w_branch, v_w_out, v_norm_mem_w, v_mem_norm_w, v_xa_wq, v_xa_wkv, v_xa_wo, v_norm_ffn_w, v_ffn_w1, v_ffn_w2, v_final_norm_w)))
    me = _slot(*_place())
    xs, mems, target = x[0], mem[0], loss_target[0]

    shard_shapes = [W[n].shape for n in SMALL_SHARDED]
    gathered = all_gather("ag_small", [_pack([W[n] for n in SMALL_SHARDED])])[0]
    parts = [jnp.stack(ps) for ps in zip(*[_unpack(gathered[d], shard_shapes) for d in range(N_DEV)])]
    full_small = {n: _gather_last(parts[i], shard_shapes[i]) for i, n in enumerate(SMALL_SHARDED)}
    lb_all, lb_vjp = jax.vjp(_lower_bounds, hg_lb_raw)
    sp8_all, sp8_vjp = jax.vjp(_decay_rates, lru_lambda)

    def layer_params(l):
        p = {n: W[n][l] for n in SMALL if n != 'final_norm_w'}
        p['lb'] = lb_all[l]
        p['sp8'] = sp8_all[l]
        p['cv_w32'] = jnp.pad(full_small['cv_dw_w'][l], ((0, 32 - CV_KERNEL), (0, 0)))
        p['lru_cw8'] = jnp.pad(full_small['lru_conv_w'][l], ((0, 8 - LRU_CONV), (0, 0)))
        p['gate_b'] = full_small['gate_b'][l]
        p['wa_bd'], p['wx_bd'] = _block_diag(lru_wa[l]), _block_diag(lru_wx[l])
        p['lru_ba'], p['lru_bx'] = lru_ba[l].reshape(-1), lru_bx[l].reshape(-1)
        return p

    def shards_of(l):
        first = [jnp.transpose(w_in[l]).astype(BF16)]
        others = [w[l].astype(BF16) for w in (w_branch, w_out, xa_wq, xa_wkv, xa_wo, ffn_w1, ffn_w2)]
        return first, others

    def start_gather(l, before):
        first, others = shards_of(l)
        state_a, tok_a = gather_start(f"ag_start{l}a", first, me, before)
        state_b, tok_b = gather_start(f"ag_start{l}b", others, me, (*before, tok_a))
        return state_a, state_b, (tok_a, tok_b)

    def first_of(o):
        return dict(w_in=o[0].reshape(IN_W, D_MODEL))

    def merge_mats(o):
        wb = jnp.transpose(o[0], (1, 2, 0, 3)).reshape(N_BRANCH, BRANCH_W, D_MODEL)
        return dict(w_branch=wb, w_out=o[1].reshape(D_MODEL, D_MODEL))

    def late_mats(o):
        return dict(xa_wq=o[0].reshape(D_MODEL, D_MODEL), xa_wkv=_natural(o[1]), xa_wo=o[2].reshape(D_MODEL, D_MODEL),
                    ffn_w1=_natural(o[3]), ffn_w2=o[4].reshape(D_FF, D_MODEL))

    def others_of(o):
        return {**merge_mats(o[:2]), **late_mats(o[2:])}

    params = [layer_params(l) for l in range(DEPTH)]
    mats, residuals = [], []
    xc = xs
    first, others = shards_of(0)
    whole = all_gather("ag_layer0", first)
    state_b, started_b = gather_start("ag_start0b", others[:2], me, (whole[0],))
    state_c, started_c = gather_start("ag_start0c", others[2:], me, (whole[0], started_b))
    gathers = {}
    for l in range(DEPTH):
        if l == 0:
            g_first = first_of(whole)
        else:
            state_a, state_b, _ = gathers.pop(l)
            g_first = first_of(gather_wait(f"ag_wait{l}a", state_a, (xc,)))

        def rest(mixed, l=l):
            more = (merge_mats if l == 0 else others_of)(gather_wait(f"ag_wait{l}b", state_b, (mixed,)))
            if l + 1 == DEPTH:
                return more, ()
            gathers[l + 1] = start_gather(l + 1, (more['w_out'],))
            return more, gathers[l + 1][2]

        if l == 0:
            xc, res, g = _layer_fwd(xc, mems, params[l], g_first, rest, after=(started_b, started_c),
                                    late=lambda x1: late_mats(gather_wait("ag_wait0c", state_c, (x1,))))
        else:
            xc, res, g = _layer_fwd(xc, mems, params[l], g_first, rest)
        mats.append(g)
        residuals.append(res)
    loss_part, dx, g_final = loss_head(xc, final_norm_w, target)
    loss = lax.psum(loss_part, ("x", "y", "c"))

    small_grads = [None] * DEPTH
    big_grads = [{} for _ in range(DEPTH)]
    pending = []

    def send(l, group, blocks, before):
        names = list(blocks)
        state, tok = exchange_start(f"rs_start{l}{group}", [blocks[n] for n in names], (before,))
        pending.append((l, group, names, state))
        return (tok,)

    def land(after):
        l, group, names, state = pending.pop(0)
        sent, landed = exchange_wait(f"rs_wait{l}{group}", state, after)
        for n, s, t in zip(names, sent, landed):
            own = lax.dynamic_index_in_dim(s, me, 0, keepdims=False).reshape(-1, s.shape[-1])
            big_grads[l][n] = sum_parts(t.reshape(N_PEER, -1, t.shape[-1]), own).reshape(t.shape[1:])

    for l in reversed(range(DEPTH)):
        dx, small_grads[l] = _layer_bwd(dx, mems, params[l], mats[l], residuals[l],
                                        lambda blocks, dx1, l=l: send(l, "a", blocks, dx1),
                                        lambda blocks, dx1, l=l: send(l, "b", blocks, dx1))
        while pending[0][0] > l:
            land(dx)

    def stacked(n):
        return jnp.stack([small_grads[l][n] for l in range(DEPTH)])

    part = {n: stacked(n) for n in SMALL if n not in ('final_norm_w', 'hg_lb_raw', 'lru_lambda')}
    part['final_norm_w'] = g_final
    part['hg_lb_raw'] = lb_vjp(stacked('lb'))[0]
    part['lru_lambda'] = sp8_vjp(stacked('sp8'))[0]
    for n in SMALL_SHARDED:
        part[n] = stacked(n)
    names = list(SMALL) + list(SMALL_SHARDED)
    full_shapes = [part[n].shape for n in names]
    packed = _pack([part[n] for n in names])
    while len(pending) > 1:
        land(dx)
    summed = [t for layer in big_grads for t in layer.values()]
    state, _ = exchange_start("rs_small_start", [packed.reshape(N_DEV, -1, LANE)], (packed, *summed))
    sent, landed = exchange_wait("rs_small_wait", state, packed)
    mine = sum_parts(landed[0], lax.dynamic_index_in_dim(sent[0], me, 0, keepdims=False))
    total = all_gather("ag_grads", [mine])[0].reshape(-1, LANE)
    while pending:
        land(total)

    G = {}
    G['w_in'] = jnp.stack([jnp.transpose(big_grads[l]['w_in']) for l in range(DEPTH)])
    for n in ('w_branch', 'w_out', 'xa_wq', 'xa_wkv', 'xa_wo', 'ffn_w1', 'ffn_w2'):
        G[n] = jnp.stack([big_grads[l][n] for l in range(DEPTH)])
    for n, t in zip(names, _unpack(total, full_shapes)):
        if n in SMALL_SHARDED:
            c = t.shape[-1] // N_DEV
            t = lax.dynamic_slice_in_dim(t, me * c, c, axis=t.ndim - 1)
        G[n] = t

    delta, new_m, new_v = {}, {}, {}
    for n in BIG:
        c = W[n].shape[-1]
        d, nm, nv = adamw(W[n].reshape(-1, c), G[n].reshape(-1, c), Mo[n].reshape(-1, c), Vo[n].reshape(-1, c))
        delta[n], new_m[n], new_v[n] = d.reshape(W[n].shape), nm.reshape(W[n].shape), nv.reshape(W[n].shape)
    shapes = [W[n].shape for n in names]
    d, nm, nv = adamw(_pack([W[n] for n in names]), _pack([G[n] for n in names]), _pack([Mo[n] for n in names]), _pack([Vo[n] for n in names]))
    for n, a, b, c in zip(names, _unpack(d, shapes), _unpack(nm, shapes), _unpack(nv, shapes)):
        delta[n], new_m[n], new_v[n] = a, b, c
    return (loss, dx[None], *[G[n] for n in WEIGHTS], *[delta[n] for n in WEIGHTS],
            *[new_m[n] for n in WEIGHTS], *[new_v[n] for n in WEIGHTS])
```

```python
import functools
import math

import jax
import jax.numpy as jnp
from jax import lax
from jax.experimental import pallas as pl
from jax.experimental.pallas import tpu as pltpu

F32 = jnp.float32
BF16 = jnp.bfloat16
I32 = jnp.int32

N_DEV = 8
D_MODEL = 1024
DEPTH = 4
CHUNK = 64
EPS = 1e-6
HG_HEADS = 4
BRANCH_W = 512
CV_KERNEL = 31
POOL_WINDOWS = (2, 4, 8, 16)
LRU_CONV = 4
LRU_C = 8.0
XA_HEADS = 4
XA_HD = D_MODEL // XA_HEADS
D_FF = 4 * D_MODEL
IN_W = 8704
OFF_Q, OFF_F, OFF_V, OFF_G, OFF_CV, OFF_PL, OFF_LX, OFF_LY, OFF_GATE = 0, 512, 1024, 1536, 2048, 3072, 3584, 4096, 4608
LANE = 128
ADAM_LR, ADAM_B1, ADAM_B2, ADAM_EPS, ADAM_WD, ADAM_STEP = 0.001, 0.9, 0.999, 1e-08, 0.01, 10
VMEM_LIMIT = 56 * 1024 * 1024
MESH = pl.DeviceIdType.MESH
NEG = -1e30
ANY_SPACE = pl.BlockSpec(memory_space=pl.ANY)


def _cp(sem, **kw):
    return pltpu.CompilerParams(dimension_semantics=sem, vmem_limit_bytes=VMEM_LIMIT, **kw)


def _sigmoid(x):
    return 1.0 / (1.0 + jnp.exp(-x))


def _dsilu(x, s):
    return s * (1.0 + x * (1.0 - s))


def _dot(a, b, cdims, precision=None):
    return lax.dot_general(a, b, (cdims, ((), ())), preferred_element_type=F32, precision=precision)


NN = ((1,), (0,))
NT = ((1,), (1,))
TN = ((0,), (0,))


def _mm(name, a, b, *, grid, a_spec, b_spec, o_specs, out_shapes, acc_shape, cdims, epi=None, extras=(), extra_specs=(), after=()):
    nk = grid[2]
    n_e, n_o = len(extras), len(out_shapes)
    extras = (*extras, *after)
    extra_specs = (*extra_specs, *[ANY_SPACE] * len(after))

    def body(*refs):
        a_ref, b_ref = refs[0], refs[1]
        e_refs = refs[2:2 + n_e]
        o_refs = refs[2 + len(extras):2 + len(extras) + n_o]

        def finish(acc):
            vals = epi(acc, *[r[...] for r in e_refs]) if epi is not None else (acc,)
            for r, v in zip(o_refs, vals):
                r[...] = v.astype(r.dtype)

        part = _dot(a_ref[...].astype(BF16), b_ref[...].astype(BF16), cdims)
        if nk == 1:
            finish(part)
        else:
            acc_ref = refs[-1]
            k = pl.program_id(2)

            @pl.when(k == 0)
            def _():
                acc_ref[...] = part

            @pl.when(k > 0)
            def _():
                acc_ref[...] += part

            @pl.when(k == nk - 1)
            def _():
                finish(acc_ref[...])

    return pl.pallas_call(
        body, name=name, grid=grid,
        in_specs=[a_spec, b_spec, *extra_specs], out_specs=list(o_specs), out_shape=list(out_shapes),
        scratch_shapes=[] if nk == 1 else [pltpu.VMEM(acc_shape, F32)],
        compiler_params=_cp(("parallel", "parallel", "arbitrary")),
    )(a, b, *extras)


def _tile(n, pref):
    t = min(n, pref)
    while n % t:
        t //= 2
    return t


def mm_nt(name, a, b, out_dtype=F32, epi=None, extras=(), n_out=1, out_dtypes=None, tm=1024, tn=1024, tk=2048, after=()):
    M, K = a.shape
    N = b.shape[0]
    tm, tn, tk = _tile(M, tm), _tile(N, tn), _tile(K, tk)
    odt = out_dtypes or [out_dtype] * n_out
    o_spec = pl.BlockSpec((tm, tn), lambda i, j, k: (i, j))
    return _mm(name, a, b, grid=(M // tm, N // tn, K // tk),
               a_spec=pl.BlockSpec((tm, tk), lambda i, j, k: (i, k)),
               b_spec=pl.BlockSpec((tn, tk), lambda i, j, k: (j, k)),
               o_specs=[o_spec] * len(odt), out_shapes=[jax.ShapeDtypeStruct((M, N), d) for d in odt],
               acc_shape=(tm, tn), cdims=NT, epi=epi, extras=extras, extra_specs=[o_spec] * len(extras), after=after)


def mm_nn(name, a, b, out_dtype=F32, epi=None, extras=(), n_out=1, out_dtypes=None, tm=1024, tn=1024, tk=2048, after=()):
    M, K = a.shape
    N = b.shape[1]
    tm, tn, tk = _tile(M, tm), _tile(N, tn), _tile(K, tk)
    odt = out_dtypes or [out_dtype] * n_out
    o_spec = pl.BlockSpec((tm, tn), lambda i, j, k: (i, j))
    return _mm(name, a, b, grid=(M // tm, N // tn, K // tk),
               a_spec=pl.BlockSpec((tm, tk), lambda i, j, k: (i, k)),
               b_spec=pl.BlockSpec((tk, tn), lambda i, j, k: (k, j)),
               o_specs=[o_spec] * len(odt), out_shapes=[jax.ShapeDtypeStruct((M, N), d) for d in odt],
               acc_shape=(tm, tn), cdims=NN, epi=epi, extras=extras, extra_specs=[o_spec] * len(extras), after=after)


def mm_tn(name, a, b, out_dtype=BF16, tm=1024, tn=1024, tk=2048):
    K, M = a.shape
    N = b.shape[1]
    tm, tn, tk = _tile(M, tm), _tile(N, tn), _tile(K, tk)
    return _mm(name, a, b, grid=(M // tm, N // tn, K // tk),
               a_spec=pl.BlockSpec((tk, tm), lambda i, j, k: (k, i)),
               b_spec=pl.BlockSpec((tk, tn), lambda i, j, k: (k, j)),
               o_specs=[pl.BlockSpec((tm, tn), lambda i, j, k: (i, j))],
               out_shapes=[jax.ShapeDtypeStruct((M, N), out_dtype)], acc_shape=(tm, tn), cdims=TN)[0]


def mm_branch_nt(name, a, b, out_dtype=BF16, tm=1024):
    M = a.shape[0]
    G, K, N = b.shape
    tm = _tile(M, tm)
    return _mm(name, a, b, grid=(M // tm, G, 1),
               a_spec=pl.BlockSpec((tm, N), lambda i, g, k: (i, g)),
               b_spec=pl.BlockSpec((None, K, N), lambda i, g, k: (g, 0, 0)),
               o_specs=[pl.BlockSpec((tm, K), lambda i, g, k: (i, g))],
               out_shapes=[jax.ShapeDtypeStruct((M, G * K), out_dtype)], acc_shape=(tm, K), cdims=NT)[0]


def mm_branch_tn(name, a, b, groups, out_dtype=BF16, tk=2048):
    T = a.shape[0]
    K, N = a.shape[1] // groups, b.shape[1] // groups
    tk = _tile(T, tk)
    return _mm(name, a, b, grid=(groups, 1, T // tk),
               a_spec=pl.BlockSpec((tk, K), lambda g, j, k: (k, g)),
               b_spec=pl.BlockSpec((tk, N), lambda g, j, k: (k, g)),
               o_specs=[pl.BlockSpec((None, K, N), lambda g, j, k: (g, 0, 0))],
               out_shapes=[jax.ShapeDtypeStruct((groups, K, N), out_dtype)], acc_shape=(K, N), cdims=TN)[0]


def mm_tn_cb(name, a, b, nb, out_dtype=BF16, tm=1024, tk=2048):
    K, M = a.shape
    N = b.shape[1]
    c = N // nb
    tm, tk = _tile(M, tm), _tile(K, tk)
    return _mm(name, a, b, grid=(M // tm, nb, K // tk),
               a_spec=pl.BlockSpec((tk, tm), lambda i, j, k: (k, i)),
               b_spec=pl.BlockSpec((tk, c), lambda i, j, k: (k, j)),
               o_specs=[pl.BlockSpec((None, tm, c), lambda i, j, k: (j, i, 0))],
               out_shapes=[jax.ShapeDtypeStruct((nb, M, c), out_dtype)], acc_shape=(tm, c), cdims=TN)[0]


def rms_fwd(name, x, w, out_dtype=BF16, tm=512):
    S, D = x.shape
    tm = _tile(S, tm)

    def body(x_ref, w_ref, o_ref):
        xv = x_ref[...]
        r = lax.rsqrt(jnp.mean(xv * xv, axis=-1, keepdims=True) + EPS)
        o_ref[...] = (xv * r * w_ref[...]).astype(o_ref.dtype)

    return pl.pallas_call(
        body, name=name, grid=(S // tm,),
        in_specs=[pl.BlockSpec((tm, D), lambda i: (i, 0)), pl.BlockSpec((1, D), lambda i: (0, 0))],
        out_specs=pl.BlockSpec((tm, D), lambda i: (i, 0)), out_shape=jax.ShapeDtypeStruct((S, D), out_dtype),
        compiler_params=_cp(("parallel",)),
    )(x, w.reshape(1, D))


def rms_bwd(name, x, w, dh, dres=None, tm=512):
    S, D = x.shape
    tm = _tile(S, tm)
    has_res = dres is not None

    def body(*refs):
        if has_res:
            x_ref, w_ref, dh_ref, dres_ref, dx_ref, dw_ref = refs
        else:
            x_ref, w_ref, dh_ref, dx_ref, dw_ref = refs
        xv = x_ref[...]
        dhv = dh_ref[...].astype(F32)
        r = lax.rsqrt(jnp.mean(xv * xv, axis=-1, keepdims=True) + EPS)
        g = dhv * w_ref[...]
        dx = r * g - xv * (r * r * r) * jnp.mean(xv * g, axis=-1, keepdims=True)
        if has_res:
            dx = dx + dres_ref[...]
        dx_ref[...] = dx

        @pl.when(pl.program_id(0) == 0)
        def _():
            dw_ref[...] = jnp.zeros_like(dw_ref)

        dw_ref[...] += jnp.sum(dhv * xv * r, axis=0, keepdims=True)

    row = pl.BlockSpec((tm, D), lambda i: (i, 0))
    vec = pl.BlockSpec((1, D), lambda i: (0, 0))
    args = [x, w.reshape(1, D), dh] + ([dres] if has_res else [])
    dx, dw = pl.pallas_call(
        body, name=name, grid=(S // tm,),
        in_specs=[row, vec, row] + ([row] if has_res else []),
        out_specs=[row, vec], out_shape=[jax.ShapeDtypeStruct((S, D), F32), jax.ShapeDtypeStruct((1, D), F32)],
        compiler_params=_cp(("arbitrary",)),
    )(*args)
    return dx, dw.reshape(D)


def loss_head(x, w, target, tm=512):
    S, D = x.shape
    tm = _tile(S, tm)

    def body(x_ref, w_ref, t_ref, loss_ref, dx_ref, dw_ref):
        xv = x_ref[...]
        wv = w_ref[...]
        r = lax.rsqrt(jnp.mean(xv * xv, axis=-1, keepdims=True) + EPS)
        y = xv * r * wv
        err = y - t_ref[...]
        dy = err * (1.0 / D)
        g = dy * wv
        dx_ref[...] = r * g - xv * (r * r * r) * jnp.mean(xv * g, axis=-1, keepdims=True)

        @pl.when(pl.program_id(0) == 0)
        def _():
            dw_ref[...] = jnp.zeros_like(dw_ref)
            loss_ref[...] = jnp.zeros_like(loss_ref)

        dw_ref[...] += jnp.sum(dy * xv * r, axis=0, keepdims=True)
        part = 0.5 * jnp.sum(jnp.mean(err * err, axis=-1, keepdims=True), axis=0, keepdims=True)
        loss_ref[...] += jnp.broadcast_to(part, loss_ref.shape)

    row = pl.BlockSpec((tm, D), lambda i: (i, 0))
    vec = pl.BlockSpec((1, D), lambda i: (0, 0))
    loss, dx, dw = pl.pallas_call(
        body, name="loss_head", grid=(S // tm,),
        in_specs=[row, vec, row],
        out_specs=[pl.BlockSpec((1, LANE), lambda i: (0, 0)), row, vec],
        out_shape=[jax.ShapeDtypeStruct((1, LANE), F32), jax.ShapeDtypeStruct((S, D), F32), jax.ShapeDtypeStruct((1, D), F32)],
        compiler_params=_cp(("arbitrary",)),
    )(x, w.reshape(1, D), target)
    return loss[0, 0], dx, dw.reshape(D)


SUB = 16
HG_W = HG_HEADS * LANE


def _hg_gates(q, f, lbv):
    sig = _sigmoid(f)
    fg = lbv + (1.0 - lbv) * sig
    sq = _sigmoid(q)
    return sig, fg, 1.0 - fg, sq, q * sq


def _hg_cumsum(logf):
    ri = lax.broadcasted_iota(I32, (CHUNK, CHUNK), 0)
    ci = lax.broadcasted_iota(I32, (CHUNK, CHUNK), 1)
    return _dot((ci <= ri).astype(F32), logf, NN, precision=lax.Precision.HIGHEST)


def _hg_rows():
    return lax.broadcasted_iota(I32, (CHUNK, LANE), 0)


def _hg_below(qf, kk, b, rows):
    blocks, parts = [jnp.zeros((SUB, CHUNK), F32)], []
    for i in range(1, CHUNK // SUB):
        bref = b[SUB * i - 1:SUB * i, :]
        rs = slice(SUB * i, SUB * (i + 1))
        eq = jnp.exp(b[rs] - bref)
        below = rows < SUB * i
        ek = jnp.exp(jnp.where(below, bref - b, NEG))
        qi = (qf[rs] * eq).astype(BF16)
        ki = (kk * ek).astype(BF16)
        blocks.append(_dot(qi, ki, NT))
        parts.append((qi, ki, eq, ek))
    return jnp.concatenate(blocks, axis=0), parts


def hgrn_fwd(proj, lb, nw):
    S = proj.shape[0]
    NC = S // CHUNK
    H = HG_HEADS

    def body(q_ref, f_ref, v_ref, g_ref, lb_ref, nw_ref, out_ref, st_out_ref, o_ref, st, kk_s, b_s):
        c = pl.program_id(0)

        @pl.when(c == 0)
        def _():
            st[...] = jnp.zeros_like(st)

        st_out_ref[...] = st[...]
        sig, fg, kk_all, sq, qf_all = _hg_gates(q_ref[...], f_ref[...], lb_ref[...])
        b_all = _hg_cumsum(jnp.log(fg))
        kk_s[...] = kk_all
        b_s[...] = b_all
        rows = _hg_rows()
        r16 = lax.broadcasted_iota(I32, (SUB, LANE), 0)
        for h in range(H):
            cs = slice(h * LANE, (h + 1) * LANE)
            qf, kk, b, v, g = qf_all[:, cs], kk_all[:, cs], b_all[:, cs], v_ref[:, cs], g_ref[:, cs]
            st_in = st[h]
            diag = []
            for i in range(CHUNK // SUB):
                rs = slice(SUB * i, SUB * (i + 1))
                acc = jnp.zeros((SUB, LANE), F32)
                for j in range(SUB):
                    row = pl.ds(SUB * i + j, 1)
                    e = jnp.exp(jnp.where(r16 >= j, b[rs] - b_s[row, cs], NEG))
                    col = jnp.sum(qf[rs] * (kk_s[row, cs] * e), axis=1, keepdims=True)
                    acc = acc + col * v_ref[row, cs]
                diag.append(acc)
            poff, _ = _hg_below(qf, kk, b, rows)
            vb = v.astype(BF16)
            bl = b[CHUNK - 1:CHUNK, :]
            o = (jnp.concatenate(diag, axis=0) + _dot(poff.astype(BF16), vb, NN)
                 + _dot((qf * jnp.exp(b)).astype(BF16), st_in.astype(BF16), NT))
            st[h] = st_in * jnp.exp(bl) + _dot(vb, (kk * jnp.exp(bl - b)).astype(BF16), TN)
            o_ref[:, cs] = o
            r = lax.rsqrt(jnp.mean(o * o, axis=-1, keepdims=True) + EPS)
            out_ref[:, cs] = (o * r * nw_ref[...] * (g * _sigmoid(g))).astype(out_ref.dtype)

    def seg(off):
        return pl.BlockSpec((CHUNK, HG_W), lambda c: (c, off // HG_W))

    blk = pl.BlockSpec((CHUNK, HG_W), lambda c: (c, 0))
    full = pltpu.VMEM((CHUNK, HG_W), F32)
    return pl.pallas_call(
        body, name="hgrn_fwd", grid=(NC,),
        in_specs=[seg(OFF_Q), seg(OFF_F), seg(OFF_V), seg(OFF_G),
                  pl.BlockSpec((1, HG_W), lambda c: (0, 0)), pl.BlockSpec((1, LANE), lambda c: (0, 0))],
        out_specs=[blk, pl.BlockSpec((None, H, LANE, LANE), lambda c: (c, 0, 0, 0)), blk],
        out_shape=[jax.ShapeDtypeStruct((S, 4 * HG_W), BF16), jax.ShapeDtypeStruct((NC, H, LANE, LANE), F32),
                   jax.ShapeDtypeStruct((S, HG_W), F32)],
        scratch_shapes=[pltpu.VMEM((H, LANE, LANE), F32), full, full],
        compiler_params=_cp(("arbitrary",)),
    )(proj, proj, proj, proj, lb.reshape(1, HG_W), nw.reshape(1, LANE))


def hgrn_bwd(proj, lb, nw, states, o_pre, dbcat, dproj):
    S = proj.shape[0]
    NC = S // CHUNK
    H = HG_HEADS

    def body(q_ref, f_ref, v_ref, g_ref, lb_ref, nw_ref, st_ref, o_ref, do_ref, _,
             dp_ref, dlb_ref, dnw_ref, dst, kk_s, b_s, do_s, db_s, dkk_s, dkk_d, dv_d):
        c = pl.program_id(0)
        dq_ref, df_ref, dv_ref, dg_ref = (dp_ref.at[:, pl.ds(off, HG_W)] for off in (OFF_Q, OFF_F, OFF_V, OFF_G))

        @pl.when(c == 0)
        def _():
            dst[...] = jnp.zeros_like(dst)
            dlb_ref[...] = jnp.zeros_like(dlb_ref)
            dnw_ref[...] = jnp.zeros_like(dnw_ref)

        q_all, g_all = q_ref[...], g_ref[...]
        lbv, nwv = lb_ref[...], nw_ref[...]
        sig, fg, kk_all, sq, qf_all = _hg_gates(q_all, f_ref[...], lbv)
        b_all = _hg_cumsum(jnp.log(fg))
        o_all = o_ref[...]
        dov = do_ref[...].astype(F32)
        sg = _sigmoid(g_all)
        gsg = g_all * sg
        dnw_acc = jnp.zeros((1, LANE), F32)
        for h in range(H):
            cs = slice(h * LANE, (h + 1) * LANE)
            o = o_all[:, cs]
            r = lax.rsqrt(jnp.mean(o * o, axis=-1, keepdims=True) + EPS)
            don = dov[:, cs] * gsg[:, cs]
            dnw_acc = dnw_acc + jnp.sum(don * o * r, axis=0, keepdims=True)
            gno = don * nwv
            do_s[:, cs] = r * gno - o * (r * r * r) * jnp.mean(o * gno, axis=-1, keepdims=True)
            dg_ref[:, cs] = (dov[:, cs] * (o * r * nwv) * _dsilu(g_all[:, cs], sg[:, cs])).astype(dg_ref.dtype)
        dnw_ref[...] += jnp.broadcast_to(dnw_acc, dnw_ref.shape)
        kk_s[...] = kk_all
        b_s[...] = b_all
        rows = _hg_rows()
        r16 = lax.broadcasted_iota(I32, (SUB, LANE), 0)
        for h in range(H):
            cs = slice(h * LANE, (h + 1) * LANE)
            qf, kk, b, v = qf_all[:, cs], kk_all[:, cs], b_all[:, cs], v_ref[:, cs]
            do = do_s[:, cs]
            st_in, dstv = st_ref[h], dst[h]
            bl = b[CHUNK - 1:CHUNK, :]
            eb, ebl, el = jnp.exp(b), jnp.exp(bl - b), jnp.exp(bl)
            qe, ke = qf * eb, kk * ebl
            vb, dob, stb, dstb = v.astype(BF16), do.astype(BF16), st_in.astype(BF16), dstv.astype(BF16)
            w_ = _dot(vb, dstb, NN)
            dqf = eb * _dot(dob, stb, NN)
            dkk = ebl * w_
            dv = _dot(ke.astype(BF16), dstb, NT)
            dbl = el * jnp.sum(st_in * dstv, axis=0, keepdims=True) + jnp.sum(ke * w_, axis=0, keepdims=True)
            dst[h] = dstv * el + _dot(dob, qe.astype(BF16), TN)
            poff, parts = _hg_below(qf, kk, b, rows)
            dpoff = _dot(dob, vb, NT).astype(BF16)
            dv = dv + _dot(poff.astype(BF16), dob, TN)
            dq_blocks = [jnp.zeros((SUB, LANE), F32)]
            for i, (qi, ki, eq, ek) in enumerate(parts, start=1):
                dpi = dpoff[SUB * i:SUB * (i + 1), :]
                dq_blocks.append(_dot(dpi, ki, NN) * eq)
                dkk = dkk + _dot(dpi, qi, TN) * ek
            dqf = dqf + jnp.concatenate(dq_blocks, axis=0)
            dq_diag = []
            for i in range(CHUNK // SUB):
                rs = slice(SUB * i, SUB * (i + 1))
                acc = jnp.zeros((SUB, LANE), F32)
                for j in range(SUB):
                    row = pl.ds(SUB * i + j, 1)
                    ks = kk_s[row, cs]
                    e = jnp.exp(jnp.where(r16 >= j, b[rs] - b_s[row, cs], NEG))
                    x = jnp.sum(do[rs] * v_ref[row, cs], axis=1, keepdims=True) * e
                    acc = acc + x * ks
                    dkk_d[row, cs] = jnp.sum(x * qf[rs], axis=0, keepdims=True)
                    col = jnp.sum(qf[rs] * (ks * e), axis=1, keepdims=True)
                    dv_d[row, cs] = jnp.sum(col * do[rs], axis=0, keepdims=True)
                dq_diag.append(acc)
            dqf = dqf + jnp.concatenate(dq_diag, axis=0)
            dkk = dkk + dkk_d[:, cs]
            dv = dv + dv_d[:, cs]
            dv_ref[:, cs] = dv.astype(dv_ref.dtype)
            db = qf * dqf - kk * dkk
            db_s[:, cs] = db + jnp.where(rows == CHUNK - 1, dbl, 0.0)
            dkk_s[:, cs] = dkk
            dq_ref[:, cs] = (dqf * _dsilu(q_all[:, cs], sq[:, cs])).astype(dq_ref.dtype)
        ri = lax.broadcasted_iota(I32, (CHUNK, CHUNK), 0)
        ci = lax.broadcasted_iota(I32, (CHUNK, CHUNK), 1)
        dlogf = _dot((ci >= ri).astype(F32), db_s[...], NN, precision=lax.Precision.HIGHEST)
        dfg = dlogf / fg - dkk_s[...]
        df_ref[...] = (dfg * (1.0 - lbv) * sig * (1.0 - sig)).astype(df_ref.dtype)
        dlb_ref[...] += jnp.broadcast_to(jnp.sum(dfg * (1.0 - sig), axis=0, keepdims=True), dlb_ref.shape)

    def seg(off):
        return pl.BlockSpec((CHUNK, HG_W), lambda c: (NC - 1 - c, off // HG_W))

    blk = pl.BlockSpec((CHUNK, HG_W), lambda c: (NC - 1 - c, 0))
    full = pltpu.VMEM((CHUNK, HG_W), F32)
    dproj, dlb, dnw = pl.pallas_call(
        body, name="hgrn_bwd", grid=(NC,),
        in_specs=[seg(OFF_Q), seg(OFF_F), seg(OFF_V), seg(OFF_G),
                  pl.BlockSpec((1, HG_W), lambda c: (0, 0)), pl.BlockSpec((1, LANE), lambda c: (0, 0)),
                  pl.BlockSpec((None, H, LANE, LANE), lambda c: (NC - 1 - c, 0, 0, 0)), blk, blk, ANY_SPACE],
        out_specs=[pl.BlockSpec((CHUNK, 4 * HG_W), lambda c: (NC - 1 - c, 0)),
                   pl.BlockSpec((8, HG_W), lambda c: (0, 0)), pl.BlockSpec((8, LANE), lambda c: (0, 0))],
        out_shape=[jax.ShapeDtypeStruct(dproj.shape, dproj.dtype), jax.ShapeDtypeStruct((8, HG_W), F32),
                   jax.ShapeDtypeStruct((8, LANE), F32)],
        input_output_aliases={9: 0},
        scratch_shapes=[pltpu.VMEM((H, LANE, LANE), F32)] + [full] * 7,
        compiler_params=_cp(("arbitrary",)),
    )(proj, proj, proj, proj, lb.reshape(1, HG_W), nw.reshape(1, LANE), states, o_pre, dbcat, dproj)
    return dproj, dlb[0], dnw[0]


CV_PAD = 32
ROWS = 256


def _colblk(S, off):
    return pl.BlockSpec((S, LANE), lambda j: (0, off // LANE + j))


def cv_fwd(proj, w32, bias):
    S = proj.shape[0]
    nchunk = S // ROWS

    def body(a_ref, g_ref, w_ref, b_ref, o_ref, zpad):
        zpad[pl.ds(0, CV_PAD), :] = jnp.zeros((CV_PAD, LANE), F32)

        def glu(c, _):
            r0 = pl.multiple_of(c * ROWS, ROWS)
            zpad[pl.ds(CV_PAD + r0, ROWS), :] = a_ref[pl.ds(r0, ROWS), :] * _sigmoid(g_ref[pl.ds(r0, ROWS), :])
            return 0

        lax.fori_loop(0, nchunk, glu, 0)

        def conv(c, _):
            r0 = pl.multiple_of(c * ROWS, ROWS)
            acc = jnp.broadcast_to(b_ref[...], (ROWS, LANE))
            for j in range(CV_KERNEL):
                acc = acc + w_ref[pl.ds(j, 1), :] * zpad[pl.ds(r0 + (CV_PAD - CV_KERNEL + 1) + j, ROWS), :]
            o_ref[pl.ds(r0, ROWS), :] = acc
            return 0

        lax.fori_loop(0, nchunk, conv, 0)

    return pl.pallas_call(
        body, name="cv_fwd", grid=(BRANCH_W // LANE,),
        in_specs=[_colblk(S, OFF_CV), _colblk(S, OFF_CV + BRANCH_W),
                  pl.BlockSpec((32, LANE), lambda j: (0, j)), pl.BlockSpec((1, LANE), lambda j: (0, j))],
        out_specs=pl.BlockSpec((S, LANE), lambda j: (0, j)), out_shape=jax.ShapeDtypeStruct((S, BRANCH_W), F32),
        scratch_shapes=[pltpu.VMEM((CV_PAD + S, LANE), F32)],
        compiler_params=_cp(("parallel",)),
    )(proj, proj, w32, bias.reshape(1, BRANCH_W))


def cv_bwd(proj, w32, dzc):
    S = proj.shape[0]
    nchunk = S // ROWS

    def body(a_ref, g_ref, w_ref, dz_ref, da_ref, dg_ref, dw_ref, db_ref, zpad, dpad):
        zpad[pl.ds(0, CV_PAD), :] = jnp.zeros((CV_PAD, LANE), F32)
        dpad[pl.ds(S, CV_PAD), :] = jnp.zeros((CV_PAD, LANE), F32)
        dw_ref[...] = jnp.zeros_like(dw_ref)

        def glu(c, dsum):
            r0 = pl.multiple_of(c * ROWS, ROWS)
            zpad[pl.ds(CV_PAD + r0, ROWS), :] = a_ref[pl.ds(r0, ROWS), :] * _sigmoid(g_ref[pl.ds(r0, ROWS), :])
            d = dz_ref[pl.ds(r0, ROWS), :]
            dpad[pl.ds(r0, ROWS), :] = d
            return dsum + jnp.sum(d, axis=0, keepdims=True)

        dsum = lax.fori_loop(0, nchunk, glu, jnp.zeros((1, LANE), F32))
        db_ref[...] = jnp.broadcast_to(dsum, db_ref.shape)

        def conv(c, _):
            r0 = pl.multiple_of(c * ROWS, ROWS)
            d = dpad[pl.ds(r0, ROWS), :]
            acc = jnp.zeros((ROWS, LANE), F32)
            for j in range(CV_KERNEL):
                acc = acc + w_ref[pl.ds(j, 1), :] * dpad[pl.ds(r0 + (CV_KERNEL - 1) - j, ROWS), :]
                zs = zpad[pl.ds(r0 + (CV_PAD - CV_KERNEL + 1) + j, ROWS), :]
                dw_ref[pl.ds(j, 1), :] += jnp.sum(d * zs, axis=0, keepdims=True)
            a = a_ref[pl.ds(r0, ROWS), :]
            sg = _sigmoid(g_ref[pl.ds(r0, ROWS), :])
            da_ref[pl.ds(r0, ROWS), :] = (acc * sg).astype(da_ref.dtype)
            dg_ref[pl.ds(r0, ROWS), :] = (acc * a * sg * (1.0 - sg)).astype(dg_ref.dtype)
            return 0

        lax.fori_loop(0, nchunk, conv, 0)

    blk = pl.BlockSpec((S, LANE), lambda j: (0, j))
    da, dg, dw, db = pl.pallas_call(
        body, name="cv_bwd", grid=(BRANCH_W // LANE,),
        in_specs=[_colblk(S, OFF_CV), _colblk(S, OFF_CV + BRANCH_W), pl.BlockSpec((32, LANE), lambda j: (0, j)), blk],
        out_specs=[blk, blk, pl.BlockSpec((32, LANE), lambda j: (0, j)), pl.BlockSpec((8, LANE), lambda j: (0, j))],
        out_shape=[jax.ShapeDtypeStruct((S, BRANCH_W), BF16), jax.ShapeDtypeStruct((S, BRANCH_W), BF16),
                   jax.ShapeDtypeStruct((32, BRANCH_W), F32), jax.ShapeDtypeStruct((8, BRANCH_W), F32)],
        scratch_shapes=[pltpu.VMEM((CV_PAD + S, LANE), F32), pltpu.VMEM((S + CV_PAD, LANE), F32)],
        compiler_params=_cp(("parallel",)),
    )(proj, proj, w32, dzc)
    return da, dg, dw, db[0]


def ln_silu_fwd(z, w, b, bcat, tm=512):
    S, C = z.shape
    tm = _tile(S, tm)

    def body(z_ref, w_ref, b_ref, _, o_ref):
        zv = z_ref[...]
        mu = jnp.mean(zv, axis=-1, keepdims=True)
        zc = zv - mu
        rstd = lax.rsqrt(jnp.mean(zc * zc, axis=-1, keepdims=True) + EPS)
        y = zc * rstd * w_ref[...] + b_ref[...]
        o_ref[...] = (y * _sigmoid(y)).astype(o_ref.dtype)

    row = pl.BlockSpec((tm, C), lambda i: (i, 0))
    vec = pl.BlockSpec((1, C), lambda i: (0, 0))
    return pl.pallas_call(
        body, name="ln_silu_fwd", grid=(S // tm,), in_specs=[row, vec, vec, ANY_SPACE],
        out_specs=pl.BlockSpec((tm, C), lambda i: (i, 1)), out_shape=jax.ShapeDtypeStruct(bcat.shape, bcat.dtype),
        input_output_aliases={3: 0}, compiler_params=_cp(("parallel",)),
    )(z, w.reshape(1, C), b.reshape(1, C), bcat)


def ln_silu_bwd(z, w, b, dbcat, tm=512):
    S, C = z.shape
    tm = _tile(S, tm)

    def body(z_ref, w_ref, b_ref, do_ref, dz_ref, dw_ref, db_ref):
        zv = z_ref[...]
        wv = w_ref[...]
        mu = jnp.mean(zv, axis=-1, keepdims=True)
        zc = zv - mu
        rstd = lax.rsqrt(jnp.mean(zc * zc, axis=-1, keepdims=True) + EPS)
        xh = zc * rstd
        y = xh * wv + b_ref[...]
        dy = do_ref[...].astype(F32) * _dsilu(y, _sigmoid(y))

        @pl.when(pl.program_id(0) == 0)
        def _():
            dw_ref[...] = jnp.zeros_like(dw_ref)
            db_ref[...] = jnp.zeros_like(db_ref)

        dw_ref[...] += jnp.sum(dy * xh, axis=0, keepdims=True)
        db_ref[...] += jnp.sum(dy, axis=0, keepdims=True)
        dxh = dy * wv
        dz_ref[...] = rstd * (dxh - jnp.mean(dxh, axis=-1, keepdims=True) - xh * jnp.mean(dxh * xh, axis=-1, keepdims=True))

    row = pl.BlockSpec((tm, C), lambda i: (i, 0))
    vec = pl.BlockSpec((1, C), lambda i: (0, 0))
    dz, dw, db = pl.pallas_call(
        body, name="ln_silu_bwd", grid=(S // tm,), in_specs=[row, vec, vec, pl.BlockSpec((tm, C), lambda i: (i, 1))],
        out_specs=[row, vec, vec],
        out_shape=[jax.ShapeDtypeStruct((S, C), F32), jax.ShapeDtypeStruct((1, C), F32), jax.ShapeDtypeStruct((1, C), F32)],
        compiler_params=_cp(("arbitrary",)),
    )(z, w.reshape(1, C), b.reshape(1, C), dbcat)
    return dz, dw.reshape(C), db.reshape(C)


PL_PAD = 16


def _pool_counts(r0, win):
    t = r0 + lax.broadcasted_iota(I32, (ROWS, LANE), 0)
    return jnp.minimum(t + 1, win).astype(F32)


def pool_fwd(proj, wg, scale, bcat):
    S = proj.shape[0]
    nchunk = S // ROWS

    def body(u_ref, w_ref, s_ref, _, o_ref, upad):
        g = pl.program_id(0)
        upad[pl.ds(0, PL_PAD), :] = jnp.zeros((PL_PAD, LANE), F32)

        def fill(c, _):
            r0 = pl.multiple_of(c * ROWS, ROWS)
            upad[pl.ds(PL_PAD + r0, ROWS), :] = u_ref[pl.ds(r0, ROWS), :]
            return 0

        lax.fori_loop(0, nchunk, fill, 0)
        wb = w_ref[...].astype(BF16)
        for gi, win in enumerate(POOL_WINDOWS):
            @pl.when(g == gi)
            def _(win=win):
                def chunk(c, _):
                    r0 = pl.multiple_of(c * ROWS, ROWS)
                    u = upad[pl.ds(PL_PAD + r0, ROWS), :]
                    ws = u
                    for j in range(1, win):
                        ws = ws + upad[pl.ds(PL_PAD + r0 - j, ROWS), :]
                    pooled = ws / _pool_counts(r0, win) - u
                    o_ref[pl.ds(r0, ROWS), :] = (_dot(pooled.astype(BF16), wb, NN) * s_ref[...]).astype(o_ref.dtype)
                    return 0

                lax.fori_loop(0, nchunk, chunk, 0)

    return pl.pallas_call(
        body, name="pool_fwd", grid=(len(POOL_WINDOWS),),
        in_specs=[_colblk(S, OFF_PL), pl.BlockSpec((None, LANE, LANE), lambda j: (j, 0, 0)), pl.BlockSpec((1, LANE), lambda j: (0, j)),
                  ANY_SPACE],
        out_specs=pl.BlockSpec((S, LANE), lambda j: (0, 2 * BRANCH_W // LANE + j)),
        out_shape=jax.ShapeDtypeStruct(bcat.shape, bcat.dtype), input_output_aliases={3: 0},
        scratch_shapes=[pltpu.VMEM((PL_PAD + S, LANE), F32)],
        compiler_params=_cp(("parallel",)),
    )(proj, wg, scale.reshape(1, BRANCH_W), bcat)


def pool_bwd(proj, wg, scale, dbcat, dproj):
    S = proj.shape[0]
    nchunk = S // ROWS

    def body(u_ref, w_ref, s_ref, dy_ref, _, du_ref, dw_ref, ds_ref, upad, dpn, nd):
        g = pl.program_id(0)
        upad[pl.ds(0, PL_PAD), :] = jnp.zeros((PL_PAD, LANE), F32)
        dpn[pl.ds(S, PL_PAD), :] = jnp.zeros((PL_PAD, LANE), F32)

        def fill(c, _):
            r0 = pl.multiple_of(c * ROWS, ROWS)
            upad[pl.ds(PL_PAD + r0, ROWS), :] = u_ref[pl.ds(r0, ROWS), :]
            return 0

        lax.fori_loop(0, nchunk, fill, 0)
        wb = w_ref[...].astype(BF16)
        sv = s_ref[...]
        for gi, win in enumerate(POOL_WINDOWS):
            @pl.when(g == gi)
            def _(win=win):
                def chunk(c, carry):
                    dw, dsc = carry
                    r0 = pl.multiple_of(c * ROWS, ROWS)
                    u = upad[pl.ds(PL_PAD + r0, ROWS), :]
                    ws = u
                    for j in range(1, win):
                        ws = ws + upad[pl.ds(PL_PAD + r0 - j, ROWS), :]
                    cnt = _pool_counts(r0, win)
                    pooled = (ws / cnt - u).astype(BF16)
                    dyv = dy_ref[pl.ds(r0, ROWS), :].astype(F32)
                    dsc = dsc + jnp.sum(dyv * _dot(pooled, wb, NN), axis=0, keepdims=True)
                    dys = (dyv * sv).astype(BF16)
                    dw = dw + _dot(pooled, dys, TN)
                    dp = _dot(dys, wb, NT)
                    dpn[pl.ds(r0, ROWS), :] = dp / cnt
                    nd[pl.ds(r0, ROWS), :] = -dp
                    return dw, dsc

                dw, dsc = lax.fori_loop(0, nchunk, chunk, (jnp.zeros((LANE, LANE), F32), jnp.zeros((1, LANE), F32)))
                dw_ref[...] = dw
                ds_ref[...] = jnp.broadcast_to(dsc, ds_ref.shape)

                def spread(c, _):
                    r0 = pl.multiple_of(c * ROWS, ROWS)
                    acc = nd[pl.ds(r0, ROWS), :]
                    for j in range(win):
                        acc = acc + dpn[pl.ds(r0 + j, ROWS), :]
                    du_ref[pl.ds(r0, ROWS), :] = acc.astype(du_ref.dtype)
                    return 0

                lax.fori_loop(0, nchunk, spread, 0)

    dproj, dw, ds = pl.pallas_call(
        body, name="pool_bwd", grid=(len(POOL_WINDOWS),),
        in_specs=[_colblk(S, OFF_PL), pl.BlockSpec((None, LANE, LANE), lambda j: (j, 0, 0)), pl.BlockSpec((1, LANE), lambda j: (0, j)),
                  pl.BlockSpec((S, LANE), lambda j: (0, 2 * BRANCH_W // LANE + j)), ANY_SPACE],
        out_specs=[_colblk(S, OFF_PL), pl.BlockSpec((None, LANE, LANE), lambda j: (j, 0, 0)), pl.BlockSpec((8, LANE), lambda j: (0, j))],
        out_shape=[jax.ShapeDtypeStruct(dproj.shape, dproj.dtype), jax.ShapeDtypeStruct((len(POOL_WINDOWS), LANE, LANE), F32),
                   jax.ShapeDtypeStruct((8, BRANCH_W), F32)],
        input_output_aliases={4: 0},
        scratch_shapes=[pltpu.VMEM((PL_PAD + S, LANE), F32), pltpu.VMEM((S + PL_PAD, LANE), F32), pltpu.VMEM((S, LANE), F32)],
        compiler_params=_cp(("parallel",)),
    )(proj, wg, scale.reshape(1, BRANCH_W), dbcat, dproj)
    return dproj, dw, ds[0]


LR_PAD = 8
SCAN_TILES = 4
GELU_C = math.sqrt(2.0 / math.pi)
GELU_A = 0.044715


def _gelu(y):
    return 0.5 * y * (1.0 + jnp.tanh(GELU_C * (y + GELU_A * y * y * y)))


def _dgelu(y):
    t = jnp.tanh(GELU_C * (y + GELU_A * y * y * y))
    return 0.5 * (1.0 + t) + 0.5 * y * (1.0 - t * t) * GELU_C * (1.0 + 3.0 * GELU_A * y * y)


def _lru_gates(xpad, r0, cw_ref, cb, wa, ba, wx, bx, sp8):
    xc = jnp.broadcast_to(cb, (ROWS, LANE))
    for j in range(LRU_CONV):
        xc = xc + cw_ref[pl.ds(j, 1), :] * xpad[pl.ds(r0 + (LR_PAD - LRU_CONV + 1) + j, ROWS), :]
    xb = xc.astype(BF16)
    r = _sigmoid(_dot(xb, wa, NN) + ba)
    ig = _sigmoid(_dot(xb, wx, NN) + bx)
    la = -sp8 * r
    a = jnp.exp(la)
    s = jnp.sqrt(-jnp.tanh(la) * (a * a + 1.0))
    return xc, r, ig, a, s


def _tile_scan(a, b, r8, up):
    for s in (1, 2, 4):
        keep = (r8 < 8 - s) if up else (r8 >= s)
        shift = 8 - s if up else s
        a_sh = jnp.where(keep, pltpu.roll(a, shift, 0), 1.0)
        b_sh = jnp.where(keep, pltpu.roll(b, shift, 0), 0.0)
        b = b + a * b_sh
        a = a * a_sh
    return a, b


def lru_fwd(proj, cw8, cb, wa_bd, ba, wx_bd, bx, sp8, bcat):
    S = proj.shape[0]
    nchunk = S // ROWS

    def body(x_ref, y_ref, cw_ref, cb_ref, wa_ref, ba_ref, wx_ref, bx_ref, sp_ref, _, o_ref, h_ref, xpad, a_s):
        xpad[pl.ds(0, LR_PAD), :] = jnp.zeros((LR_PAD, LANE), F32)

        def fill(c, _):
            r0 = pl.multiple_of(c * ROWS, ROWS)
            xpad[pl.ds(LR_PAD + r0, ROWS), :] = x_ref[pl.ds(r0, ROWS), :]
            return 0

        lax.fori_loop(0, nchunk, fill, 0)
        wa = wa_ref[...].astype(BF16)
        wx = wx_ref[...].astype(BF16)

        def gates(c, _):
            r0 = pl.multiple_of(c * ROWS, ROWS)
            xc, r, ig, a, s = _lru_gates(xpad, r0, cw_ref, cb_ref[...], wa, ba_ref[...], wx, bx_ref[...], sp_ref[...])
            a_s[pl.ds(r0, ROWS), :] = a
            h_ref[pl.ds(r0, ROWS), :] = s * (ig * xc)
            return 0

        lax.fori_loop(0, nchunk, gates, 0)

        r8 = lax.broadcasted_iota(I32, (8, LANE), 0)

        def scan(i, h):
            bases = [pl.multiple_of(i * (8 * SCAN_TILES) + 8 * j, 8) for j in range(SCAN_TILES)]
            maps = [_tile_scan(a_s[pl.ds(b, 8), :], h_ref[pl.ds(b, 8), :], r8, False) for b in bases]
            for b, (ca, cb_) in zip(bases, maps):
                out = cb_ + ca * h
                h_ref[pl.ds(b, 8), :] = out
                h = out[7:8, :]
            return h

        lax.fori_loop(0, S // (8 * SCAN_TILES), scan, jnp.zeros((1, LANE), F32))

        def gate_out(c, _):
            r0 = pl.multiple_of(c * ROWS, ROWS)
            o_ref[pl.ds(r0, ROWS), :] = (h_ref[pl.ds(r0, ROWS), :] * _gelu(y_ref[pl.ds(r0, ROWS), :])).astype(o_ref.dtype)
            return 0

        lax.fori_loop(0, nchunk, gate_out, 0)

    vec = pl.BlockSpec((1, LANE), lambda j: (0, j))
    mat = pl.BlockSpec((None, LANE, LANE), lambda j: (j, 0, 0))
    blk = pl.BlockSpec((S, LANE), lambda j: (0, j))
    return pl.pallas_call(
        body, name="lru_fwd", grid=(BRANCH_W // LANE,),
        in_specs=[_colblk(S, OFF_LX), _colblk(S, OFF_LY), pl.BlockSpec((8, LANE), lambda j: (0, j)), vec, mat, vec, mat, vec, vec,
                  ANY_SPACE],
        out_specs=[pl.BlockSpec((S, LANE), lambda j: (0, 3 * BRANCH_W // LANE + j)), blk],
        out_shape=[jax.ShapeDtypeStruct(bcat.shape, bcat.dtype), jax.ShapeDtypeStruct((S, BRANCH_W), F32)],
        input_output_aliases={9: 0},
        scratch_shapes=[pltpu.VMEM((LR_PAD + S, LANE), F32), pltpu.VMEM((S, LANE), F32)],
        compiler_params=_cp(("parallel",)),
    )(proj, proj, cw8, cb.reshape(1, -1), wa_bd, ba.reshape(1, -1), wx_bd, bx.reshape(1, -1), sp8.reshape(1, -1), bcat)


def lru_bwd(proj, cw8, cb, wa_bd, ba, wx_bd, bx, sp8, h, dbcat):
    S = proj.shape[0]
    nchunk = S // ROWS

    def body(x_ref, y_ref, cw_ref, cb_ref, wa_ref, ba_ref, wx_ref, bx_ref, sp_ref, h_ref, do_ref,
             dx_ref, dy_ref, dcw_ref, dcb_ref, dwa_ref, dba_ref, dwx_ref, dbx_ref, dsp_ref,
             xpad, a_s, g_s, hpad, dxc, xc_s, r_s, ig_s, s_s):
        xpad[pl.ds(0, LR_PAD), :] = jnp.zeros((LR_PAD, LANE), F32)
        hpad[pl.ds(0, LR_PAD), :] = jnp.zeros((LR_PAD, LANE), F32)
        dxc[pl.ds(S, LR_PAD), :] = jnp.zeros((LR_PAD, LANE), F32)
        dcw_ref[...] = jnp.zeros_like(dcw_ref)
        wa = wa_ref[...].astype(BF16)
        wx = wx_ref[...].astype(BF16)
        cbv, bav, bxv, spv = cb_ref[...], ba_ref[...], bx_ref[...], sp_ref[...]

        def fill(c, _):
            r0 = pl.multiple_of(c * ROWS, ROWS)
            xpad[pl.ds(LR_PAD + r0, ROWS), :] = x_ref[pl.ds(r0, ROWS), :]
            hv = h_ref[pl.ds(r0, ROWS), :]
            hpad[pl.ds(LR_PAD + r0, ROWS), :] = hv
            yv = y_ref[pl.ds(r0, ROWS), :]
            dov = do_ref[pl.ds(r0, ROWS), :].astype(F32)
            g_s[pl.ds(r0, ROWS), :] = dov * _gelu(yv)
            dy_ref[pl.ds(r0, ROWS), :] = (dov * hv * _dgelu(yv)).astype(dy_ref.dtype)
            return 0

        lax.fori_loop(0, nchunk, fill, 0)

        def gates(c, _):
            r0 = pl.multiple_of(c * ROWS, ROWS)
            rows = pl.ds(r0, ROWS)
            xc_s[rows, :], r_s[rows, :], ig_s[rows, :], a_s[rows, :], s_s[rows, :] = _lru_gates(
                xpad, r0, cw_ref, cbv, wa, bav, wx, bxv, spv)
            return 0

        lax.fori_loop(0, nchunk, gates, 0)

        r8 = lax.broadcasted_iota(I32, (8, LANE), 0)

        def rscan(i, carry):
            bases = [pl.multiple_of(S - 8 - i * (8 * SCAN_TILES) - 8 * j, 8) for j in range(SCAN_TILES)]
            firsts, maps = [], []
            for b in bases:
                a8 = a_s[pl.ds(b, 8), :]
                above = jnp.where(r8 < 7, pltpu.roll(a8, 7, 0), 1.0)
                firsts.append(a8[0:1, :])
                maps.append(_tile_scan(above, g_s[pl.ds(b, 8), :], r8, True))
            for b, a0, (ca, cb_) in zip(bases, firsts, maps):
                out = cb_ + ca * carry
                g_s[pl.ds(b, 8), :] = out
                carry = a0 * out[0:1, :]
            return carry

        lax.fori_loop(0, S // (8 * SCAN_TILES), rscan, jnp.zeros((1, LANE), F32))

        def chain(c, carry):
            dwa, dwx, dba, dbx, dsp, dcb = carry
            r0 = pl.multiple_of(c * ROWS, ROWS)
            rows = pl.ds(r0, ROWS)
            xc, r, ig, a, s = xc_s[rows, :], r_s[rows, :], ig_s[rows, :], a_s[rows, :], s_s[rows, :]
            gt = g_s[rows, :]
            hprev = hpad[pl.ds(r0 + LR_PAD - 1, ROWS), :]
            da = gt * hprev - gt * ig * xc * (a / s)
            dig = gt * s * xc
            dla = da * a
            dsp = dsp + jnp.sum(-dla * r, axis=0, keepdims=True)
            dpr = (-dla * spv) * r * (1.0 - r)
            dpi = dig * ig * (1.0 - ig)
            dprb, dpib, xb = dpr.astype(BF16), dpi.astype(BF16), xc.astype(BF16)
            d = gt * s * ig + _dot(dprb, wa, NT) + _dot(dpib, wx, NT)
            dwa = dwa + _dot(xb, dprb, TN)
            dwx = dwx + _dot(xb, dpib, TN)
            dba = dba + jnp.sum(dpr, axis=0, keepdims=True)
            dbx = dbx + jnp.sum(dpi, axis=0, keepdims=True)
            dcb = dcb + jnp.sum(d, axis=0, keepdims=True)
            dxc[pl.ds(r0, ROWS), :] = d
            for j in range(LRU_CONV):
                xs = xpad[pl.ds(r0 + (LR_PAD - LRU_CONV + 1) + j, ROWS), :]
                dcw_ref[pl.ds(j, 1), :] += jnp.sum(d * xs, axis=0, keepdims=True)
            return dwa, dwx, dba, dbx, dsp, dcb

        zm, zv = jnp.zeros((LANE, LANE), F32), jnp.zeros((1, LANE), F32)
        dwa, dwx, dba, dbx, dsp, dcb = lax.fori_loop(0, nchunk, chain, (zm, zm, zv, zv, zv, zv))
        dwa_ref[...] = dwa
        dwx_ref[...] = dwx
        dba_ref[...] = jnp.broadcast_to(dba, dba_ref.shape)
        dbx_ref[...] = jnp.broadcast_to(dbx, dbx_ref.shape)
        dsp_ref[...] = jnp.broadcast_to(dsp, dsp_ref.shape)
        dcb_ref[...] = jnp.broadcast_to(dcb, dcb_ref.shape)

        def convt(c, _):
            r0 = pl.multiple_of(c * ROWS, ROWS)
            acc = jnp.zeros((ROWS, LANE), F32)
            for j in range(LRU_CONV):
                acc = acc + cw_ref[pl.ds(j, 1), :] * dxc[pl.ds(r0 + (LRU_CONV - 1) - j, ROWS), :]
            dx_ref[pl.ds(r0, ROWS), :] = acc.astype(dx_ref.dtype)
            return 0

        lax.fori_loop(0, nchunk, convt, 0)

    vec = pl.BlockSpec((1, LANE), lambda j: (0, j))
    vec8 = pl.BlockSpec((8, LANE), lambda j: (0, j))
    mat = pl.BlockSpec((None, LANE, LANE), lambda j: (j, 0, 0))
    blk = pl.BlockSpec((S, LANE), lambda j: (0, j))
    nblk = BRANCH_W // LANE
    v8 = jax.ShapeDtypeStruct((8, BRANCH_W), F32)
    m4 = jax.ShapeDtypeStruct((nblk, LANE, LANE), F32)
    big = jax.ShapeDtypeStruct((S, BRANCH_W), BF16)
    seq = pltpu.VMEM((S, LANE), F32)
    dx, dy, dcw, dcb, dwa, dba, dwx, dbx, dsp = pl.pallas_call(
        body, name="lru_bwd", grid=(nblk,),
        in_specs=[_colblk(S, OFF_LX), _colblk(S, OFF_LY), vec8, vec, mat, vec, mat, vec, vec, blk,
                  pl.BlockSpec((S, LANE), lambda j: (0, 3 * BRANCH_W // LANE + j))],
        out_specs=[blk, blk, vec8, vec8, mat, vec8, mat, vec8, vec8],
        out_shape=[big, big, v8, v8, m4, v8, m4, v8, v8],
        scratch_shapes=[pltpu.VMEM((LR_PAD + S, LANE), F32), seq, seq, pltpu.VMEM((LR_PAD + S, LANE), F32),
                        pltpu.VMEM((S + LR_PAD, LANE), F32), seq, seq, seq, seq],
        compiler_params=_cp(("parallel",)),
    )(proj, proj, cw8, cb.reshape(1, -1), wa_bd, ba.reshape(1, -1), wx_bd, bx.reshape(1, -1), sp8.reshape(1, -1), h, dbcat)
    return dx, dy, dcw, dcb[0], dwa, dba[0], dwx, dbx[0], dsp[0]


MG_COLS = 512
N_BRANCH = 4
BCAT_W = N_BRANCH * BRANCH_W


def merge_fwd(bcat, wb, proj, gate_b, tm=2048, after=()):
    S = proj.shape[0]
    tm = _tile(S, tm)
    halves = D_MODEL // MG_COLS

    def body(a_ref, w_ref, g_ref, gb_ref, *rest):
        up_ref, sg_ref, o_ref, acc_ref = rest[len(after):]
        k = pl.program_id(2)
        up = _dot(a_ref[...], w_ref[...], NN)
        up_ref[...] = up.astype(up_ref.dtype)
        sg = _sigmoid(g_ref[...] + gb_ref[pl.ds(k, 1), :])
        sg_ref[...] = sg.astype(sg_ref.dtype)
        term = sg * up

        @pl.when(k == 0)
        def _():
            acc_ref[...] = term

        @pl.when(k > 0)
        def _():
            acc_ref[...] += term

        @pl.when(k == N_BRANCH - 1)
        def _():
            o_ref[...] = acc_ref[...].astype(o_ref.dtype)

    return pl.pallas_call(
        body, name="merge_fwd", grid=(S // tm, halves, N_BRANCH),
        in_specs=[pl.BlockSpec((tm, BRANCH_W), lambda i, j, k: (i, k)),
                  pl.BlockSpec((None, BRANCH_W, MG_COLS), lambda i, j, k: (k, 0, j)),
                  pl.BlockSpec((tm, MG_COLS), lambda i, j, k: (i, OFF_GATE // MG_COLS + k * halves + j)),
                  pl.BlockSpec((N_BRANCH, MG_COLS), lambda i, j, k: (0, j))] + [ANY_SPACE] * len(after),
        out_specs=[pl.BlockSpec((tm, MG_COLS), lambda i, j, k: (i, k * halves + j)),
                   pl.BlockSpec((tm, MG_COLS), lambda i, j, k: (i, k * halves + j)),
                   pl.BlockSpec((tm, MG_COLS), lambda i, j, k: (i, j))],
        out_shape=[jax.ShapeDtypeStruct((S, N_BRANCH * D_MODEL), BF16), jax.ShapeDtypeStruct((S, N_BRANCH * D_MODEL), BF16),
                   jax.ShapeDtypeStruct((S, D_MODEL), BF16)],
        scratch_shapes=[pltpu.VMEM((tm, MG_COLS), F32)],
        compiler_params=_cp(("parallel", "parallel", "arbitrary")),
    )(bcat, wb, proj, gate_b, *after)


def merge_bwd(dmerged, ups, gates, tm=1024):
    S = dmerged.shape[0]
    tm = _tile(S, tm)
    halves = D_MODEL // MG_COLS

    def body(dm_ref, u_ref, sg_ref, du_ref, dg_ref, dgb_ref):
        @pl.when(pl.program_id(2) == 0)
        def _():
            dgb_ref[...] = jnp.zeros_like(dgb_ref)

        dm = dm_ref[...].astype(F32)
        sg = sg_ref[...].astype(F32)
        du_ref[...] = (dm * sg).astype(du_ref.dtype)
        dgk = dm * u_ref[...] * sg * (1.0 - sg)
        dg_ref[...] = dgk.astype(dg_ref.dtype)
        dgb_ref[...] += jnp.broadcast_to(jnp.sum(dgk, axis=0, keepdims=True), dgb_ref.shape)

    dups, dproj, dgb = pl.pallas_call(
        body, name="merge_bwd", grid=(N_BRANCH, halves, S // tm),
        in_specs=[pl.BlockSpec((tm, MG_COLS), lambda k, j, i: (i, j)),
                  pl.BlockSpec((tm, MG_COLS), lambda k, j, i: (i, k * halves + j)),
                  pl.BlockSpec((tm, MG_COLS), lambda k, j, i: (i, k * halves + j))],
        out_specs=[pl.BlockSpec((tm, MG_COLS), lambda k, j, i: (i, k * halves + j)),
                   pl.BlockSpec((tm, MG_COLS), lambda k, j, i: (i, OFF_GATE // MG_COLS + k * halves + j)),
                   pl.BlockSpec((8, MG_COLS), lambda k, j, i: (k, j))],
        out_shape=[jax.ShapeDtypeStruct((S, N_BRANCH * D_MODEL), BF16), jax.ShapeDtypeStruct((S, IN_W), BF16),
                   jax.ShapeDtypeStruct((8 * N_BRANCH, D_MODEL), F32)],
        compiler_params=_cp(("parallel", "parallel", "arbitrary")),
    )(dmerged, ups, gates)
    return dups, dproj, dgb.reshape(N_BRANCH, 8, D_MODEL)[:, 0]


def attn_fwd(q, kv, tm=512):
    S = q.shape[0]
    M = kv.shape[0]
    tm = _tile(S, tm)
    scale = XA_HD ** -0.5

    def body(q_ref, kv_ref, o_ref):
        for hh in range(XA_HEADS):
            cs = pl.ds(hh * XA_HD, XA_HD)
            qh = q_ref[:, cs]
            kh = kv_ref[:, cs]
            vh = kv_ref[:, pl.ds(D_MODEL + hh * XA_HD, XA_HD)]
            s = _dot(qh, kh, NT) * scale
            p = jnp.exp(s - jnp.max(s, axis=-1, keepdims=True))
            p = p / jnp.sum(p, axis=-1, keepdims=True)
            o_ref[:, cs] = _dot(p.astype(BF16), vh, NN).astype(o_ref.dtype)

    return pl.pallas_call(
        body, name="attn_fwd", grid=(S // tm,),
        in_specs=[pl.BlockSpec((tm, D_MODEL), lambda i: (i, 0)), pl.BlockSpec((M, 2 * D_MODEL), lambda i: (0, 0))],
        out_specs=pl.BlockSpec((tm, D_MODEL), lambda i: (i, 0)), out_shape=jax.ShapeDtypeStruct((S, D_MODEL), BF16),
        compiler_params=_cp(("parallel",)),
    )(q, kv)


def attn_bwd(q, kv, do, tm=512):
    S = q.shape[0]
    M = kv.shape[0]
    tm = _tile(S, tm)
    scale = XA_HD ** -0.5

    def body(q_ref, kv_ref, do_ref, dq_ref, dkv_ref):
        @pl.when(pl.program_id(0) == 0)
        def _():
            dkv_ref[...] = jnp.zeros_like(dkv_ref)

        for hh in range(XA_HEADS):
            cs = pl.ds(hh * XA_HD, XA_HD)
            vs = pl.ds(D_MODEL + hh * XA_HD, XA_HD)
            qh = q_ref[:, cs]
            kh = kv_ref[:, cs]
            vh = kv_ref[:, vs]
            doh = do_ref[:, cs]
            s = _dot(qh, kh, NT) * scale
            p = jnp.exp(s - jnp.max(s, axis=-1, keepdims=True))
            p = p / jnp.sum(p, axis=-1, keepdims=True)
            dp = _dot(doh, vh, NT)
            ds = (p * (dp - jnp.sum(dp * p, axis=-1, keepdims=True)) * scale).astype(BF16)
            dq_ref[:, cs] = _dot(ds, kh, NN).astype(dq_ref.dtype)
            dkv_ref[:, cs] += _dot(ds, qh, TN)
            dkv_ref[:, vs] += _dot(p.astype(BF16), doh, TN)

    row = pl.BlockSpec((tm, D_MODEL), lambda i: (i, 0))
    full = pl.BlockSpec((M, 2 * D_MODEL), lambda i: (0, 0))
    return pl.pallas_call(
        body, name="attn_bwd", grid=(S // tm,), in_specs=[row, full, row], out_specs=[row, full],
        out_shape=[jax.ShapeDtypeStruct((S, D_MODEL), BF16), jax.ShapeDtypeStruct((M, 2 * D_MODEL), F32)],
        compiler_params=_cp(("arbitrary",)),
    )(q, kv, do)


def sum_parts(parts, own=None, tm=512):
    n, R, C = parts.shape
    tm = next(t for t in range(min(R, tm) // 8 * 8, 0, -8) if R % t == 0)
    has_own = own is not None

    def body(*refs):
        p_ref, o_ref = refs[0], refs[-1]
        acc = refs[1][...].astype(F32) if has_own else p_ref[0].astype(F32)
        for j in range(0 if has_own else 1, n):
            acc = acc + p_ref[j].astype(F32)
        o_ref[...] = acc

    row = pl.BlockSpec((tm, C), lambda i: (i, 0))
    return pl.pallas_call(
        body, name="sum_parts", grid=(R // tm,),
        in_specs=[pl.BlockSpec((n, tm, C), lambda i: (0, i, 0))] + ([row] if has_own else []), out_specs=row,
        out_shape=jax.ShapeDtypeStruct((R, C), F32), compiler_params=_cp(("parallel",)),
    )(*([parts, own] if has_own else [parts]))


def adamw(w, g, m, v, tm=256):
    R, C = w.shape
    tm = _tile(R, tm)
    c1 = 1.0 / (1.0 - ADAM_B1 ** ADAM_STEP)
    c2 = 1.0 / (1.0 - ADAM_B2 ** ADAM_STEP)

    def body(w_ref, g_ref, m_ref, v_ref, d_ref, nm_ref, nv_ref):
        gv = g_ref[...]
        nm = ADAM_B1 * m_ref[...] + (1.0 - ADAM_B1) * gv
        nv = ADAM_B2 * v_ref[...] + (1.0 - ADAM_B2) * (gv * gv)
        nm_ref[...] = nm
        nv_ref[...] = nv
        d_ref[...] = -ADAM_LR * ((nm * c1) / (jnp.sqrt(nv * c2) + ADAM_EPS) + ADAM_WD * w_ref[...])

    blk = pl.BlockSpec((tm, C), lambda i: (i, 0))
    sd = jax.ShapeDtypeStruct((R, C), F32)
    return pl.pallas_call(
        body, name="adamw", grid=(R // tm,), in_specs=[blk] * 4, out_specs=[blk] * 3, out_shape=[sd] * 3,
        compiler_params=_cp(("parallel",)),
    )(w, g, m, v)


ANY = pl.BlockSpec(memory_space=pl.ANY)


def _place():
    return lax.axis_index("x"), lax.axis_index("y"), lax.axis_index("c")


def _slot(px, py, pc):
    return 4 * px + 2 * py + pc


def all_gather(name, shards, after=()):
    n = len(shards)
    n_in = n + len(after)

    def body(*refs):
        x_refs, out_refs = refs[:n], refs[n_in:n_in + n]
        send_sems, recv_sems, local_sems = refs[n_in + n:]
        x, y, c = _place()
        me, sibling = (x, y, c), (x, y, 1 - c)
        chips = [(1 - x, y), (x, 1 - y), (1 - x, 1 - y)]

        def copy(a, k, block, to, src=None):
            rows = out_refs[a].at[_slot(*block)]
            return pltpu.make_async_remote_copy(
                src_ref=rows if src is None else src, dst_ref=rows,
                send_sem=send_sems.at[7 * a + k], recv_sem=recv_sems.at[7 * a + k],
                device_id=to, device_id_type=MESH)

        mine = [pltpu.make_async_copy(x_refs[a], out_refs[a].at[_slot(*me)], local_sems.at[a]) for a in range(n)]
        for cp in mine:
            cp.start()
        first = []
        for a in range(n):
            first.append(copy(a, 0, me, sibling, src=x_refs[a]))
            first += [copy(a, 1 + j, me, (*chip, c), src=x_refs[a]) for j, chip in enumerate(chips)]
        for cp in first:
            cp.start()
        passed = []
        for a in range(n):
            for j, chip in enumerate(chips):
                copy(a, 1 + j, (*chip, c), me).wait_recv()
                cp = copy(a, 4 + j, (*chip, c), sibling)
                cp.start()
                passed.append(cp)
        for a in range(n):
            copy(a, 0, sibling, me).wait_recv()
            for j, chip in enumerate(chips):
                copy(a, 4 + j, (*chip, 1 - c), me).wait_recv()
        for cp in first + passed:
            cp.wait_send()
        for cp in mine:
            cp.wait()

    return pl.pallas_call(
        body, name=name, in_specs=[ANY] * n_in, out_specs=[ANY] * n,
        out_shape=[jax.ShapeDtypeStruct((N_DEV, *s.shape), s.dtype) for s in shards],
        scratch_shapes=[pltpu.SemaphoreType.DMA((7 * n,)), pltpu.SemaphoreType.DMA((7 * n,)), pltpu.SemaphoreType.DMA((n,))],
    )(*shards, *after)


HBM = pl.BlockSpec(memory_space=pltpu.HBM)
SEM = pl.BlockSpec(memory_space=pltpu.SEMAPHORE)
EFFECT = pltpu.SideEffectType.DATAFLOW_SIDE_EFFECTING
N_PEER = N_DEV - 1
RELATIONS = [(dx, dy, dc) for dx in (0, 1) for dy in (0, 1) for dc in (0, 1)][1:]


def _peer(place, rel):
    return tuple(1 - v if d else v for v, d in zip(place, rel))


def gather_start(name, shards, me, before):
    n = len(shards)

    def body(*refs):
        x_refs, land_refs = refs[:n], refs[n:2 * n]
        send_sems, recv_sems = refs[2 * n + len(before):2 * n + len(before) + 2]
        token = refs[-1]
        place = _place()
        mine = _slot(*place)
        for a in range(n):
            for rel in RELATIONS:
                pltpu.make_async_remote_copy(
                    src_ref=x_refs[a], dst_ref=land_refs[a].at[mine], send_sem=send_sems.at[a], recv_sem=recv_sems.at[a],
                    device_id=_peer(place, rel), device_id_type=MESH).start()
        token[...] = jnp.zeros_like(token)

    lands = [lax.dynamic_update_index_in_dim(lax.empty((N_DEV, *s.shape), s.dtype), s, me, 0) for s in shards]
    outs = pl.pallas_call(
        body, name=name,
        in_specs=[HBM] * (2 * n) + [ANY] * len(before),
        out_specs=[SEM, SEM] + [HBM] * (2 * n) + [pl.BlockSpec(memory_space=pltpu.VMEM)],
        out_shape=[pltpu.SemaphoreType.DMA((n,)), pltpu.SemaphoreType.DMA((n,))]
        + [pltpu.HBM(t.shape, t.dtype) for t in (*shards, *lands)] + [jax.ShapeDtypeStruct((8, LANE), F32)],
        input_output_aliases={i: 2 + i for i in range(2 * n)},
        compiler_params=pltpu.CompilerParams(has_side_effects=EFFECT),
    )(*[pltpu.with_memory_space_constraint(t, pltpu.HBM) for t in (*shards, *lands)], *before)
    return (outs[0], outs[1], outs[2:2 + n], outs[2 + n:2 + 2 * n]), outs[-1]


def gather_wait(name, state, after):
    send_sems, recv_sems, shards, lands = state
    n = len(shards)

    def body(*refs):
        land_refs = refs[n:2 * n]
        s_sems, r_sems = refs[2 * n:2 * n + 2]
        place = _place()
        for a in range(n):
            seven = land_refs[a].at[pl.ds(0, N_PEER)]
            cp = pltpu.make_async_remote_copy(
                src_ref=seven, dst_ref=seven, send_sem=s_sems.at[a], recv_sem=r_sems.at[a], device_id=place, device_id_type=MESH)
            cp.wait_send()
            cp.wait_recv()

    outs = pl.pallas_call(
        body, name=name,
        in_specs=[HBM] * (2 * n) + [SEM, SEM] + [ANY] * len(after), out_specs=[HBM] * (2 * n),
        out_shape=[pltpu.HBM(t.shape, t.dtype) for t in (*shards, *lands)],
        input_output_aliases={i: i for i in range(2 * n)},
        compiler_params=pltpu.CompilerParams(has_side_effects=EFFECT),
    )(*shards, *lands, send_sems, recv_sems, *after)
    return outs[n:]


def exchange_start(name, grads, before):
    n = len(grads)

    def body(*refs):
        g_refs, land_refs = refs[:n], refs[n:2 * n]
        send_sems, recv_sems = refs[2 * n + len(before):2 * n + len(before) + 2]
        token = refs[-1]
        place = _place()
        for a in range(n):
            for r, rel in enumerate(RELATIONS):
                p = _peer(place, rel)
                pltpu.make_async_remote_copy(
                    src_ref=g_refs[a].at[_slot(*p)], dst_ref=land_refs[a].at[r],
                    send_sem=send_sems.at[a], recv_sem=recv_sems.at[a], device_id=p, device_id_type=MESH).start()
        token[...] = jnp.zeros_like(token)

    lands = [lax.empty((N_PEER, *g.shape[1:]), g.dtype) for g in grads]
    outs = pl.pallas_call(
        body, name=name,
        in_specs=[HBM] * (2 * n) + [ANY] * len(before),
        out_specs=[SEM, SEM] + [HBM] * (2 * n) + [pl.BlockSpec(memory_space=pltpu.VMEM)],
        out_shape=[pltpu.SemaphoreType.DMA((n,)), pltpu.SemaphoreType.DMA((n,))]
        + [pltpu.HBM(g.shape, g.dtype) for g in grads] + [pltpu.HBM(t.shape, t.dtype) for t in lands]
        + [jax.ShapeDtypeStruct((8, LANE), F32)],
        input_output_aliases={i: 2 + i for i in range(2 * n)},
        compiler_params=pltpu.CompilerParams(has_side_effects=EFFECT),
    )(*[pltpu.with_memory_space_constraint(t, pltpu.HBM) for t in (*grads, *lands)], *before)
    return (outs[0], outs[1], outs[2:2 + n], outs[2 + n:2 + 2 * n]), outs[-1]


def exchange_wait(name, state, after):
    send_sems, recv_sems, grads, lands = state
    n = len(grads)

    def body(*refs):
        g_refs, land_refs = refs[:n], refs[n:2 * n]
        s_sems, r_sems = refs[2 * n:2 * n + 2]
        place = _place()
        for a in range(n):
            cp = pltpu.make_async_remote_copy(
                src_ref=g_refs[a].at[pl.ds(0, N_PEER)], dst_ref=land_refs[a],
                send_sem=s_sems.at[a], recv_sem=r_sems.at[a], device_id=place, device_id_type=MESH)
            cp.wait_send()
            cp.wait_recv()

    outs = pl.pallas_call(
        body, name=name,
        in_specs=[HBM] * (2 * n) + [SEM, SEM, ANY], out_specs=[HBM] * (2 * n),
        out_shape=[pltpu.HBM(t.shape, t.dtype) for t in (*grads, *lands)],
        input_output_aliases={i: i for i in range(2 * n)},
        compiler_params=pltpu.CompilerParams(has_side_effects=EFFECT),
    )(*grads, *lands, send_sems, recv_sems, after)
    return outs[:n], outs[n:]


WEIGHTS = ['norm_mix_w', 'w_in', 'hg_lb_raw', 'hg_norm_w', 'cv_dw_w', 'cv_dw_b', 'cv_ln_w', 'cv_ln_b', 'pl_w', 'pl_scale',
           'lru_conv_w', 'lru_conv_b', 'lru_wa', 'lru_ba', 'lru_wx', 'lru_bx', 'lru_lambda', 'gate_b', 'w_branch', 'w_out',
           'norm_mem_w', 'mem_norm_w', 'xa_wq', 'xa_wkv', 'xa_wo', 'norm_ffn_w', 'ffn_w1', 'ffn_w2', 'final_norm_w']
BIG = ('w_in', 'w_branch', 'w_out', 'xa_wq', 'xa_wkv', 'xa_wo', 'ffn_w1', 'ffn_w2')
SMALL_SHARDED = ('cv_dw_w', 'lru_conv_w', 'gate_b')
SMALL = tuple(n for n in WEIGHTS if n not in BIG and n not in SMALL_SHARDED)
PACK_ROWS = 256


def _pack(arrs):
    flat = jnp.concatenate([a.reshape(-1).astype(F32) for a in arrs])
    tile = PACK_ROWS * LANE
    padded = -(-flat.shape[0] // tile) * tile
    return jnp.pad(flat, (0, padded - flat.shape[0])).reshape(-1, LANE)


def _unpack(packed, shapes):
    flat = packed.reshape(-1)
    out, off = [], 0
    for s in shapes:
        n = math.prod(s)
        out.append(flat[off:off + n].reshape(s))
        off += n
    return out


def _gather_last(g, shard_shape):
    nd = len(shard_shape)
    full = jnp.moveaxis(g, 0, nd - 1)
    return full.reshape(*shard_shape[:-1], N_DEV * shard_shape[-1])


def _natural(blocks):
    nb, k, c = blocks.shape
    return jnp.transpose(blocks, (1, 0, 2)).reshape(k, nb * c)


def _block_diag(w):
    w2 = w.reshape(4, 2, 64, 64)
    z = jnp.zeros((4, 64, 64), w.dtype)
    return jnp.concatenate([jnp.concatenate([w2[:, 0], z], axis=2), jnp.concatenate([z, w2[:, 1]], axis=2)], axis=1)


def _block_diag_t(d):
    return jnp.stack([d[:, :64, :64], d[:, 64:, 64:]], axis=1).reshape(8, 64, 64)


def _lower_bounds(raw):
    lb = jnp.cumsum(jax.nn.softmax(raw.astype(F32), axis=0), axis=0)
    return lb - lb[0:1]


def _decay_rates(lam):
    return (LRU_C * jax.nn.softplus(-lam.astype(F32))).reshape(DEPTH, BRANCH_W)


def _relu2(acc):
    r = jnp.maximum(acc, 0.0)
    return acc, r * r


def _relu2_grad(acc, u):
    return (acc * 2.0 * jnp.maximum(u, 0.0),)


def _add(acc, e):
    return (acc + e,)


def _layer_fwd(x0, mem, p, g, rest, after=(), late=None):
    h1 = rms_fwd("rms_mix", x0, p['norm_mix_w'])
    proj = mm_nt("mm_in", h1, g['w_in'], tn=2176, after=after)[0]
    bcat, states, o_hg = hgrn_fwd(proj, p['lb'], p['hg_norm_w'])
    zc = cv_fwd(proj, p['cv_w32'], p['cv_dw_b'])
    bcat = ln_silu_fwd(zc, p['cv_ln_w'], p['cv_ln_b'], bcat)
    bcat = pool_fwd(proj, p['pl_w'], p['pl_scale'], bcat)
    bcat, hst = lru_fwd(proj, p['lru_cw8'], p['lru_conv_b'], p['wa_bd'], p['lru_ba'], p['wx_bd'], p['lru_bx'], p['sp8'], bcat)
    more, after = rest(bcat)
    g = {**g, **more}
    ups, gates, merged = merge_fwd(bcat, g['w_branch'], proj, p['gate_b'], after=after)
    x1 = mm_nn("mm_out", merged, g['w_out'], epi=_add, extras=(x0,))[0]
    if late is not None:
        g = {**g, **late(x1)}
    h2 = rms_fwd("rms_mem", x1, p['norm_mem_w'])
    q = mm_nn("mm_q", h2, g['xa_wq'], out_dtype=BF16)[0]
    memn = rms_fwd("rms_memtok", mem, p['mem_norm_w'])
    kv = mm_nn("mm_kv", memn, g['xa_wkv'], out_dtype=BF16, tn=2048)[0]
    oa = attn_fwd(q, kv)
    x2 = mm_nn("mm_o", oa, g['xa_wo'], epi=_add, extras=(x1,))[0]
    h3 = rms_fwd("rms_ffn", x2, p['norm_ffn_w'])
    u, act = mm_nn("mm_ffn1", h3, g['ffn_w1'], epi=_relu2, out_dtypes=[BF16, BF16])
    x3 = mm_nn("mm_ffn2", act, g['ffn_w2'], epi=_add, extras=(x2,))[0]
    res = dict(x0=x0, h1=h1, proj=proj, states=states, o_hg=o_hg, zc=zc, hst=hst, bcat=bcat, ups=ups, gates=gates, merged=merged,
               x1=x1, h2=h2, q=q, memn=memn, kv=kv, oa=oa, x2=x2, h3=h3, u=u, act=act)
    return x3, res, g


def _layer_bwd(dx3, mem, p, g, r, midway, finish):
    gs, gb = {}, {}
    du = mm_nt("mm_dffn2", dx3, g['ffn_w2'], out_dtype=BF16, epi=_relu2_grad, extras=(r['u'],))[0]
    gb['ffn_w2'] = mm_tn("mm_gw2", r['act'], dx3).reshape(N_DEV, -1, D_MODEL)
    gb['ffn_w1'] = mm_tn_cb("mm_gw1", r['h3'], du, N_DEV)
    dh3 = mm_nt("mm_dffn1", du, g['ffn_w1'], out_dtype=BF16)[0]
    dx2, gs['norm_ffn_w'] = rms_bwd("rmsb_ffn", r['x2'], p['norm_ffn_w'], dh3, dx3)
    doa = mm_nt("mm_do", dx2, g['xa_wo'], out_dtype=BF16)[0]
    gb['xa_wo'] = mm_tn("mm_gwo", r['oa'], dx2).reshape(N_DEV, -1, D_MODEL)
    dq, dkv = attn_bwd(r['q'], r['kv'], doa)
    gb['xa_wq'] = mm_tn("mm_gwq", r['h2'], dq).reshape(N_DEV, -1, D_MODEL)
    dh2 = mm_nt("mm_dq", dq, g['xa_wq'], out_dtype=BF16)[0]
    gb['xa_wkv'] = mm_tn_cb("mm_gwkv", r['memn'], dkv, N_DEV)
    dmemn = mm_nt("mm_dkv", dkv, g['xa_wkv'], out_dtype=BF16)[0]
    _, gs['mem_norm_w'] = rms_bwd("rmsb_memtok", mem, p['mem_norm_w'], dmemn)
    dx1, gs['norm_mem_w'] = rms_bwd("rmsb_mem", r['x1'], p['norm_mem_w'], dh2, dx2)
    after = midway(gb, dx1)
    gb = {}
    dmerged = mm_nt("mm_dout", dx1, g['w_out'], out_dtype=BF16, after=after)[0]
    gb['w_out'] = mm_tn("mm_gwout", r['merged'], dx1).reshape(N_DEV, -1, D_MODEL)
    dups, dproj, gs['gate_b'] = merge_bwd(dmerged, r['ups'], r['gates'])
    gwb = mm_branch_tn("mm_gwb", r['bcat'], dups, N_BRANCH)
    gb['w_branch'] = jnp.transpose(gwb.reshape(N_BRANCH, BRANCH_W, N_DEV, -1), (2, 0, 1, 3))
    dbcat = mm_branch_nt("mm_dup", dups, g['w_branch'], tm=2048)
    dproj, gs['lb'], gs['hg_norm_w'] = hgrn_bwd(r['proj'], p['lb'], p['hg_norm_w'], r['states'], r['o_hg'], dbcat, dproj)
    dzc, gs['cv_ln_w'], gs['cv_ln_b'] = ln_silu_bwd(r['zc'], p['cv_ln_w'], p['cv_ln_b'], dbcat)
    dca, dcg, dcw, gs['cv_dw_b'] = cv_bwd(r['proj'], p['cv_w32'], dzc)
    gs['cv_dw_w'] = dcw[:CV_KERNEL]
    dproj, gs['pl_w'], gs['pl_scale'] = pool_bwd(r['proj'], p['pl_w'], p['pl_scale'], dbcat, dproj)
    dlx, dly, dlcw, gs['lru_conv_b'], dwa, gs['lru_ba'], dwx, gs['lru_bx'], gs['sp8'] = lru_bwd(
        r['proj'], p['lru_cw8'], p['lru_conv_b'], p['wa_bd'], p['lru_ba'], p['wx_bd'], p['lru_bx'], p['sp8'], r['hst'], dbcat)
    gs['lru_conv_w'] = dlcw[:LRU_CONV]
    gs['lru_wa'], gs['lru_wx'] = _block_diag_t(dwa), _block_diag_t(dwx)
    gs['lru_ba'], gs['lru_bx'] = gs['lru_ba'].reshape(8, 64), gs['lru_bx'].reshape(8, 64)
    for off, piece in ((OFF_CV, dca), (OFF_CV + BRANCH_W, dcg), (OFF_LX, dlx), (OFF_LY, dly)):
        dproj = lax.dynamic_update_slice(dproj, piece, (0, off))
    gb['w_in'] = mm_tn("mm_gwin", dproj, r['h1'], tm=2176, tk=1024).reshape(N_DEV, -1, D_MODEL)
    dh1 = mm_nn("mm_din", dproj, g['w_in'], out_dtype=BF16, tk=4352, after=finish(gb, dx1))[0]
    dx0, gs['norm_mix_w'] = rms_bwd("rmsb_mix", r['x0'], p['norm_mix_w'], dh1, dx1)
    return dx0, gs


def kernel(x, mem, norm_mix_w, w_in, hg_lb_raw, hg_norm_w, cv_dw_w, cv_dw_b, cv_ln_w, cv_ln_b, pl_w, pl_scale, lru_conv_w, lru_conv_b, lru_wa, lru_ba, lru_wx, lru_bx, lru_lambda, gate_b, w_branch, w_out, norm_mem_w, mem_norm_w, xa_wq, xa_wkv, xa_wo, norm_ffn_w, ffn_w1, ffn_w2, final_norm_w, loss_target, m_norm_mix_w, m_w_in, m_hg_lb_raw, m_hg_norm_w, m_cv_dw_w, m_cv_dw_b, m_cv_ln_w, m_cv_ln_b, m_pl_w, m_pl_scale, m_lru_conv_w, m_lru_conv_b, m_lru_wa, m_lru_ba, m_lru_wx, m_lru_bx, m_lru_lambda, m_gate_b, m_w_branch, m_w_out, m_norm_mem_w, m_mem_norm_w, m_xa_wq, m_xa_wkv, m_xa_wo, m_norm_ffn_w, m_ffn_w1, m_ffn_w2, m_final_norm_w, v_norm_mix_w, v_w_in, v_hg_lb_raw, v_hg_norm_w, v_cv_dw_w, v_cv_dw_b, v_cv_ln_w, v_cv_ln_b, v_pl_w, v_pl_scale, v_lru_conv_w, v_lru_conv_b, v_lru_wa, v_lru_ba, v_lru_wx, v_lru_bx, v_lru_lambda, v_gate_b, v_w_branch, v_w_out, v_norm_mem_w, v_mem_norm_w, v_xa_wq, v_xa_wkv, v_xa_wo, v_norm_ffn_w, v_ffn_w1, v_ffn_w2, v_final_norm_w):
    W = dict(zip(WEIGHTS, (norm_mix_w, w_in, hg_lb_raw, hg_norm_w, cv_dw_w, cv_dw_b, cv_ln_w, cv_ln_b, pl_w, pl_scale, lru_conv_w, lru_conv_b, lru_wa, lru_ba, lru_wx, lru_bx, lru_lambda, gate_b, w_branch, w_out, norm_mem_w, mem_norm_w, xa_wq, xa_wkv, xa_wo, norm_ffn_w, ffn_w1, ffn_w2, final_norm_w)))
    Mo = dict(zip(WEIGHTS, (m_norm_mix_w, m_w_in, m_hg_lb_raw, m_hg_norm_w, m_cv_dw_w, m_cv_dw_b, m_cv_ln_w, m_cv_ln_b, m_pl_w, m_pl_scale, m_lru_conv_w, m_lru_conv_b, m_lru_wa, m_lru_ba, m_lru_wx, m_lru_bx, m_lru_lambda, m_gate_b, m_w_branch, m_w_out, m_norm_mem_w, m_mem_norm_w, m_xa_wq, m_xa_wkv, m_xa_wo, m_norm_ffn_w, m_ffn_w1, m_ffn_w2, m_final_norm_w)))
    Vo = dict(zip(WEIGHTS, (v_norm_mix_w, v_w_in, v_hg_lb_raw, v_hg_norm_w, v_cv_dw_w, v_cv_dw_b, v_cv_ln_w, v_cv_ln_b, v_pl_w, v_pl_scale, v_lru_conv_w, v_lru_conv_b, v_lru_wa, v_lru_ba, v_lru_wx, v_lru_bx, v_lru_lambda, v_gate_b, v_w_branch, v_w_out, v_norm_mem_w, v_mem_norm_w, v_xa_wq, v_xa_wkv, v_xa_wo, v_norm_ffn_w, v_ffn_w1, v_ffn_w2, v_final_norm_w)))
    me = _slot(*_place())
    xs, mems, target = x[0], mem[0], loss_target[0]

    shard_shapes = [W[n].shape for n in SMALL_SHARDED]
    gathered = all_gather("ag_small", [_pack([W[n] for n in SMALL_SHARDED])])[0]
    parts = [jnp.stack(ps) for ps in zip(*[_unpack(gathered[d], shard_shapes) for d in range(N_DEV)])]
    full_small = {n: _gather_last(parts[i], shard_shapes[i]) for i, n in enumerate(SMALL_SHARDED)}
    lb_all, lb_vjp = jax.vjp(_lower_bounds, hg_lb_raw)
    sp8_all, sp8_vjp = jax.vjp(_decay_rates, lru_lambda)

    def layer_params(l):
        p = {n: W[n][l] for n in SMALL if n != 'final_norm_w'}
        p['lb'] = lb_all[l]
        p['sp8'] = sp8_all[l]
        p['cv_w32'] = jnp.pad(full_small['cv_dw_w'][l], ((0, 32 - CV_KERNEL), (0, 0)))
        p['lru_cw8'] = jnp.pad(full_small['lru_conv_w'][l], ((0, 8 - LRU_CONV), (0, 0)))
        p['gate_b'] = full_small['gate_b'][l]
        p['wa_bd'], p['wx_bd'] = _block_diag(lru_wa[l]), _block_diag(lru_wx[l])
        p['lru_ba'], p['lru_bx'] = lru_ba[l].reshape(-1), lru_bx[l].reshape(-1)
        return p

    def shards_of(l):
        first = [jnp.transpose(w_in[l]).astype(BF16)]
        others = [w[l].astype(BF16) for w in (w_branch, w_out, xa_wq, xa_wkv, xa_wo, ffn_w1, ffn_w2)]
        return first, others

    def start_gather(l, before):
        first, others = shards_of(l)
        state_a, tok_a = gather_start(f"ag_start{l}a", first, me, before)
        state_b, tok_b = gather_start(f"ag_start{l}b", others, me, (*before, tok_a))
        return state_a, state_b, (tok_a, tok_b)

    def first_of(o):
        return dict(w_in=o[0].reshape(IN_W, D_MODEL))

    def merge_mats(o):
        wb = jnp.transpose(o[0], (1, 2, 0, 3)).reshape(N_BRANCH, BRANCH_W, D_MODEL)
        return dict(w_branch=wb, w_out=o[1].reshape(D_MODEL, D_MODEL))

    def late_mats(o):
        return dict(xa_wq=o[0].reshape(D_MODEL, D_MODEL), xa_wkv=_natural(o[1]), xa_wo=o[2].reshape(D_MODEL, D_MODEL),
                    ffn_w1=_natural(o[3]), ffn_w2=o[4].reshape(D_FF, D_MODEL))

    def others_of(o):
        return {**merge_mats(o[:2]), **late_mats(o[2:])}

    params = [layer_params(l) for l in range(DEPTH)]
    mats, residuals = [], []
    xc = xs
    first, others = shards_of(0)
    whole = all_gather("ag_layer0", first)
    state_b, started_b = gather_start("ag_start0b", others[:2], me, (whole[0],))
    state_c, started_c = gather_start("ag_start0c", others[2:], me, (whole[0], started_b))
    gathers = {}
    for l in range(DEPTH):
        if l == 0:
            g_first = first_of(whole)
        else:
            state_a, state_b, _ = gathers.pop(l)
            g_first = first_of(gather_wait(f"ag_wait{l}a", state_a, (xc,)))

        def rest(mixed, l=l):
            more = (merge_mats if l == 0 else others_of)(gather_wait(f"ag_wait{l}b", state_b, (mixed,)))
            if l + 1 == DEPTH:
                return more, ()
            gathers[l + 1] = start_gather(l + 1, (more['w_out'],))
            return more, gathers[l + 1][2]

        if l == 0:
            xc, res, g = _layer_fwd(xc, mems, params[l], g_first, rest, after=(started_b, started_c),
                                    late=lambda x1: late_mats(gather_wait("ag_wait0c", state_c, (x1,))))
        else:
            xc, res, g = _layer_fwd(xc, mems, params[l], g_first, rest)
        mats.append(g)
        residuals.append(res)
    loss_part, dx, g_final = loss_head(xc, final_norm_w, target)
    loss = lax.psum(loss_part, ("x", "y", "c"))

    small_grads = [None] * DEPTH
    big_grads = [{} for _ in range(DEPTH)]
    pending = []

    def send(l, group, blocks, before):
        names = list(blocks)
        state, tok = exchange_start(f"rs_start{l}{group}", [blocks[n] for n in names], (before,))
        pending.append((l, group, names, state))
        return (tok,)

    def land(after):
        l, group, names, state = pending.pop(0)
        sent, landed = exchange_wait(f"rs_wait{l}{group}", state, after)
        for n, s, t in zip(names, sent, landed):
            own = lax.dynamic_index_in_dim(s, me, 0, keepdims=False).reshape(-1, s.shape[-1])
            big_grads[l][n] = sum_parts(t.reshape(N_PEER, -1, t.shape[-1]), own).reshape(t.shape[1:])

    for l in reversed(range(DEPTH)):
        dx, small_grads[l] = _layer_bwd(dx, mems, params[l], mats[l], residuals[l],
                                        lambda blocks, dx1, l=l: send(l, "a", blocks, dx1),
                                        lambda blocks, dx1, l=l: send(l, "b", blocks, dx1))
        while pending[0][0] > l:
            land(dx)

    def stacked(n):
        return jnp.stack([small_grads[l][n] for l in range(DEPTH)])

    part = {n: stacked(n) for n in SMALL if n not in ('final_norm_w', 'hg_lb_raw', 'lru_lambda')}
    part['final_norm_w'] = g_final
    part['hg_lb_raw'] = lb_vjp(stacked('lb'))[0]
    part['lru_lambda'] = sp8_vjp(stacked('sp8'))[0]
    for n in SMALL_SHARDED:
        part[n] = stacked(n)
    names = list(SMALL) + list(SMALL_SHARDED)
    full_shapes = [part[n].shape for n in names]
    packed = _pack([part[n] for n in names])
    while len(pending) > 1:
        land(dx)
    summed = [t for layer in big_grads for t in layer.values()]
    state, _ = exchange_start("rs_small_start", [packed.reshape(N_DEV, -1, LANE)], (packed, *summed))
    sent, landed = exchange_wait("rs_small_wait", state, packed)
    mine = sum_parts(landed[0], lax.dynamic_index_in_dim(sent[0], me, 0, keepdims=False))
    total = all_gather("ag_grads", [mine])[0].reshape(-1, LANE)
    while pending:
        land(total)

    G = {}
    G['w_in'] = jnp.stack([jnp.transpose(big_grads[l]['w_in']) for l in range(DEPTH)])
    for n in ('w_branch', 'w_out', 'xa_wq', 'xa_wkv', 'xa_wo', 'ffn_w1', 'ffn_w2'):
        G[n] = jnp.stack([big_grads[l][n] for l in range(DEPTH)])
    for n, t in zip(names, _unpack(total, full_shapes)):
        if n in SMALL_SHARDED:
            c = t.shape[-1] // N_DEV
            t = lax.dynamic_slice_in_dim(t, me * c, c, axis=t.ndim - 1)
        G[n] = t

    delta, new_m, new_v = {}, {}, {}
    for n in BIG:
        c = W[n].shape[-1]
        d, nm, nv = adamw(W[n].reshape(-1, c), G[n].reshape(-1, c), Mo[n].reshape(-1, c), Vo[n].reshape(-1, c))
        delta[n], new_m[n], new_v[n] = d.reshape(W[n].shape), nm.reshape(W[n].shape), nv.reshape(W[n].shape)
    shapes = [W[n].shape for n in names]
    d, nm, nv = adamw(_pack([W[n] for n in names]), _pack([G[n] for n in names]), _pack([Mo[n] for n in names]), _pack([Vo[n] for n in names]))
    for n, a, b, c in zip(names, _unpack(d, shapes), _unpack(nm, shapes), _unpack(nv, shapes)):
        delta[n], new_m[n], new_v[n] = a, b, c
    return (loss, dx[None], *[G[n] for n in WEIGHTS], *[delta[n] for n in WEIGHTS],
            *[new_m[n] for n in WEIGHTS], *[new_v[n] for n in WEIGHTS])
```

```python
import functools
import math

import jax
import jax.numpy as jnp
from jax import lax
from jax.experimental import pallas as pl
from jax.experimental.pallas import tpu as pltpu

F32 = jnp.float32
BF16 = jnp.bfloat16
I32 = jnp.int32

N_DEV = 8
D_MODEL = 1024
DEPTH = 4
CHUNK = 64
EPS = 1e-6
HG_HEADS = 4
BRANCH_W = 512
CV_KERNEL = 31
POOL_WINDOWS = (2, 4, 8, 16)
LRU_CONV = 4
LRU_C = 8.0
XA_HEADS = 4
XA_HD = D_MODEL // XA_HEADS
D_FF = 4 * D_MODEL
IN_W = 8704
OFF_Q, OFF_F, OFF_V, OFF_G, OFF_CV, OFF_PL, OFF_LX, OFF_LY, OFF_GATE = 0, 512, 1024, 1536, 2048, 3072, 3584, 4096, 4608
LANE = 128
ADAM_LR, ADAM_B1, ADAM_B2, ADAM_EPS, ADAM_WD, ADAM_STEP = 0.001, 0.9, 0.999, 1e-08, 0.01, 10
VMEM_LIMIT = 56 * 1024 * 1024
MESH = pl.DeviceIdType.MESH
NEG = -1e30
ANY_SPACE = pl.BlockSpec(memory_space=pl.ANY)


def _cp(sem, **kw):
    return pltpu.CompilerParams(dimension_semantics=sem, vmem_limit_bytes=VMEM_LIMIT, **kw)


def _sigmoid(x):
    return 1.0 / (1.0 + jnp.exp(-x))


def _dsilu(x, s):
    return s * (1.0 + x * (1.0 - s))


def _dot(a, b, cdims, precision=None):
    return lax.dot_general(a, b, (cdims, ((), ())), preferred_element_type=F32, precision=precision)


NN = ((1,), (0,))
NT = ((1,), (1,))
TN = ((0,), (0,))


def _mm(name, a, b, *, grid, a_spec, b_spec, o_specs, out_shapes, acc_shape, cdims, epi=None, extras=(), extra_specs=(), after=()):
    nk = grid[2]
    n_e, n_o = len(extras), len(out_shapes)
    extras = (*extras, *after)
    extra_specs = (*extra_specs, *[ANY_SPACE] * len(after))

    def body(*refs):
        a_ref, b_ref = refs[0], refs[1]
        e_refs = refs[2:2 + n_e]
        o_refs = refs[2 + len(extras):2 + len(extras) + n_o]

        def finish(acc):
            vals = epi(acc, *[r[...] for r in e_refs]) if epi is not None else (acc,)
            for r, v in zip(o_refs, vals):
                r[...] = v.astype(r.dtype)

        part = _dot(a_ref[...].astype(BF16), b_ref[...].astype(BF16), cdims)
        if nk == 1:
            finish(part)
        else:
            acc_ref = refs[-1]
            k = pl.program_id(2)

            @pl.when(k == 0)
            def _():
                acc_ref[...] = part

            @pl.when(k > 0)
            def _():
                acc_ref[...] += part

            @pl.when(k == nk - 1)
            def _():
                finish(acc_ref[...])

    return pl.pallas_call(
        body, name=name, grid=grid,
        in_specs=[a_spec, b_spec, *extra_specs], out_specs=list(o_specs), out_shape=list(out_shapes),
        scratch_shapes=[] if nk == 1 else [pltpu.VMEM(acc_shape, F32)],
        compiler_params=_cp(("parallel", "parallel", "arbitrary")),
    )(a, b, *extras)


def _tile(n, pref):
    t = min(n, pref)
    while n % t:
        t //= 2
    return t


def mm_nt(name, a, b, out_dtype=F32, epi=None, extras=(), n_out=1, out_dtypes=None, tm=1024, tn=1024, tk=2048, after=()):
    M, K = a.shape
    N = b.shape[0]
    tm, tn, tk = _tile(M, tm), _tile(N, tn), _tile(K, tk)
    odt = out_dtypes or [out_dtype] * n_out
    o_spec = pl.BlockSpec((tm, tn), lambda i, j, k: (i, j))
    return _mm(name, a, b, grid=(M // tm, N // tn, K // tk),
               a_spec=pl.BlockSpec((tm, tk), lambda i, j, k: (i, k)),
               b_spec=pl.BlockSpec((tn, tk), lambda i, j, k: (j, k)),
               o_specs=[o_spec] * len(odt), out_shapes=[jax.ShapeDtypeStruct((M, N), d) for d in odt],
               acc_shape=(tm, tn), cdims=NT, epi=epi, extras=extras, extra_specs=[o_spec] * len(extras), after=after)


def mm_nn(name, a, b, out_dtype=F32, epi=None, extras=(), n_out=1, out_dtypes=None, tm=1024, tn=1024, tk=2048, after=()):
    M, K = a.shape
    N = b.shape[1]
    tm, tn, tk = _tile(M, tm), _tile(N, tn), _tile(K, tk)
    odt = out_dtypes or [out_dtype] * n_out
    o_spec = pl.BlockSpec((tm, tn), lambda i, j, k: (i, j))
    return _mm(name, a, b, grid=(M // tm, N // tn, K // tk),
               a_spec=pl.BlockSpec((tm, tk), lambda i, j, k: (i, k)),
               b_spec=pl.BlockSpec((tk, tn), lambda i, j, k: (k, j)),
               o_specs=[o_spec] * len(odt), out_shapes=[jax.ShapeDtypeStruct((M, N), d) for d in odt],
               acc_shape=(tm, tn), cdims=NN, epi=epi, extras=extras, extra_specs=[o_spec] * len(extras), after=after)


def mm_tn(name, a, b, out_dtype=BF16, tm=1024, tn=1024, tk=2048):
    K, M = a.shape
    N = b.shape[1]
    tm, tn, tk = _tile(M, tm), _tile(N, tn), _tile(K, tk)
    return _mm(name, a, b, grid=(M // tm, N // tn, K // tk),
               a_spec=pl.BlockSpec((tk, tm), lambda i, j, k: (k, i)),
               b_spec=pl.BlockSpec((tk, tn), lambda i, j, k: (k, j)),
               o_specs=[pl.BlockSpec((tm, tn), lambda i, j, k: (i, j))],
               out_shapes=[jax.ShapeDtypeStruct((M, N), out_dtype)], acc_shape=(tm, tn), cdims=TN)[0]


def mm_branch_nt(name, a, b, out_dtype=BF16, tm=1024):
    M = a.shape[0]
    G, K, N = b.shape
    tm = _tile(M, tm)
    return _mm(name, a, b, grid=(M // tm, G, 1),
               a_spec=pl.BlockSpec((tm, N), lambda i, g, k: (i, g)),
               b_spec=pl.BlockSpec((None, K, N), lambda i, g, k: (g, 0, 0)),
               o_specs=[pl.BlockSpec((tm, K), lambda i, g, k: (i, g))],
               out_shapes=[jax.ShapeDtypeStruct((M, G * K), out_dtype)], acc_shape=(tm, K), cdims=NT)[0]


def mm_branch_tn(name, a, b, groups, out_dtype=BF16, tk=2048):
    T = a.shape[0]
    K, N = a.shape[1] // groups, b.shape[1] // groups
    tk = _tile(T, tk)
    return _mm(name, a, b, grid=(groups, 1, T // tk),
               a_spec=pl.BlockSpec((tk, K), lambda g, j, k: (k, g)),
               b_spec=pl.BlockSpec((tk, N), lambda g, j, k: (k, g)),
               o_specs=[pl.BlockSpec((None, K, N), lambda g, j, k: (g, 0, 0))],
               out_shapes=[jax.ShapeDtypeStruct((groups, K, N), out_dtype)], acc_shape=(K, N), cdims=TN)[0]


def mm_tn_cb(name, a, b, nb, out_dtype=BF16, tm=1024, tk=2048):
    K, M = a.shape
    N = b.shape[1]
    c = N // nb
    tm, tk = _tile(M, tm), _tile(K, tk)
    return _mm(name, a, b, grid=(M // tm, nb, K // tk),
               a_spec=pl.BlockSpec((tk, tm), lambda i, j, k: (k, i)),
               b_spec=pl.BlockSpec((tk, c), lambda i, j, k: (k, j)),
               o_specs=[pl.BlockSpec((None, tm, c), lambda i, j, k: (j, i, 0))],
               out_shapes=[jax.ShapeDtypeStruct((nb, M, c), out_dtype)], acc_shape=(tm, c), cdims=TN)[0]


def rms_fwd(name, x, w, out_dtype=BF16, tm=512):
    S, D = x.shape
    tm = _tile(S, tm)

    def body(x_ref, w_ref, o_ref):
        xv = x_ref[...]
        r = lax.rsqrt(jnp.mean(xv * xv, axis=-1, keepdims=True) + EPS)
        o_ref[...] = (xv * r * w_ref[...]).astype(o_ref.dtype)

    return pl.pallas_call(
        body, name=name, grid=(S // tm,),
        in_specs=[pl.BlockSpec((tm, D), lambda i: (i, 0)), pl.BlockSpec((1, D), lambda i: (0, 0))],
        out_specs=pl.BlockSpec((tm, D), lambda i: (i, 0)), out_shape=jax.ShapeDtypeStruct((S, D), out_dtype),
        compiler_params=_cp(("parallel",)),
    )(x, w.reshape(1, D))


def rms_bwd(name, x, w, dh, dres=None, tm=512):
    S, D = x.shape
    tm = _tile(S, tm)
    has_res = dres is not None

    def body(*refs):
        if has_res:
            x_ref, w_ref, dh_ref, dres_ref, dx_ref, dxb_ref, dw_ref = refs
        else:
            x_ref, w_ref, dh_ref, dx_ref, dxb_ref, dw_ref = refs
        xv = x_ref[...]
        dhv = dh_ref[...].astype(F32)
        r = lax.rsqrt(jnp.mean(xv * xv, axis=-1, keepdims=True) + EPS)
        g = dhv * w_ref[...]
        dx = r * g - xv * (r * r * r) * jnp.mean(xv * g, axis=-1, keepdims=True)
        if has_res:
            dx = dx + dres_ref[...]
        dx_ref[...] = dx
        dxb_ref[...] = dx.astype(dxb_ref.dtype)

        @pl.when(pl.program_id(0) == 0)
        def _():
            dw_ref[...] = jnp.zeros_like(dw_ref)

        dw_ref[...] += jnp.sum(dhv * xv * r, axis=0, keepdims=True)

    row = pl.BlockSpec((tm, D), lambda i: (i, 0))
    vec = pl.BlockSpec((1, D), lambda i: (0, 0))
    args = [x, w.reshape(1, D), dh] + ([dres] if has_res else [])
    dx, dxb, dw = pl.pallas_call(
        body, name=name, grid=(S // tm,),
        in_specs=[row, vec, row] + ([row] if has_res else []),
        out_specs=[row, row, vec],
        out_shape=[jax.ShapeDtypeStruct((S, D), F32), jax.ShapeDtypeStruct((S, D), BF16), jax.ShapeDtypeStruct((1, D), F32)],
        compiler_params=_cp(("arbitrary",)),
    )(*args)
    return dx, dxb, dw.reshape(D)


def loss_head(x, w, target, tm=512):
    S, D = x.shape
    tm = _tile(S, tm)

    def body(x_ref, w_ref, t_ref, loss_ref, dx_ref, dxb_ref, dw_ref):
        xv = x_ref[...]
        wv = w_ref[...]
        r = lax.rsqrt(jnp.mean(xv * xv, axis=-1, keepdims=True) + EPS)
        y = xv * r * wv
        err = y - t_ref[...]
        dy = err * (1.0 / D)
        g = dy * wv
        dx = r * g - xv * (r * r * r) * jnp.mean(xv * g, axis=-1, keepdims=True)
        dx_ref[...] = dx
        dxb_ref[...] = dx.astype(dxb_ref.dtype)

        @pl.when(pl.program_id(0) == 0)
        def _():
            dw_ref[...] = jnp.zeros_like(dw_ref)
            loss_ref[...] = jnp.zeros_like(loss_ref)

        dw_ref[...] += jnp.sum(dy * xv * r, axis=0, keepdims=True)
        part = 0.5 * jnp.sum(jnp.mean(err * err, axis=-1, keepdims=True), axis=0, keepdims=True)
        loss_ref[...] += jnp.broadcast_to(part, loss_ref.shape)

    row = pl.BlockSpec((tm, D), lambda i: (i, 0))
    vec = pl.BlockSpec((1, D), lambda i: (0, 0))
    loss, dx, dxb, dw = pl.pallas_call(
        body, name="loss_head", grid=(S // tm,),
        in_specs=[row, vec, row],
        out_specs=[pl.BlockSpec((1, LANE), lambda i: (0, 0)), row, row, vec],
        out_shape=[jax.ShapeDtypeStruct((1, LANE), F32), jax.ShapeDtypeStruct((S, D), F32), jax.ShapeDtypeStruct((S, D), BF16),
                   jax.ShapeDtypeStruct((1, D), F32)],
        compiler_params=_cp(("arbitrary",)),
    )(x, w.reshape(1, D), target)
    return loss[0, 0], dx, dxb, dw.reshape(D)


SUB = 16
HG_W = HG_HEADS * LANE


def _hg_gates(q, f, lbv):
    sig = _sigmoid(f)
    fg = lbv + (1.0 - lbv) * sig
    sq = _sigmoid(q)
    return sig, fg, 1.0 - fg, sq, q * sq


def _hg_cumsum(logf):
    ri = lax.broadcasted_iota(I32, (CHUNK, CHUNK), 0)
    ci = lax.broadcasted_iota(I32, (CHUNK, CHUNK), 1)
    return _dot((ci <= ri).astype(F32), logf, NN, precision=lax.Precision.HIGHEST)


def _hg_rows():
    return lax.broadcasted_iota(I32, (CHUNK, LANE), 0)


def _hg_below(qf, kk, b, rows):
    blocks, parts = [jnp.zeros((SUB, CHUNK), F32)], []
    for i in range(1, CHUNK // SUB):
        bref = b[SUB * i - 1:SUB * i, :]
        rs = slice(SUB * i, SUB * (i + 1))
        eq = jnp.exp(b[rs] - bref)
        below = rows < SUB * i
        ek = jnp.exp(jnp.where(below, bref - b, NEG))
        qi = (qf[rs] * eq).astype(BF16)
        ki = (kk * ek).astype(BF16)
        blocks.append(_dot(qi, ki, NT))
        parts.append((qi, ki, eq, ek))
    return jnp.concatenate(blocks, axis=0), parts


def hgrn_fwd(proj, lb, nw):
    S = proj.shape[0]
    NC = S // CHUNK
    H = HG_HEADS

    def body(q_ref, f_ref, v_ref, g_ref, lb_ref, nw_ref, out_ref, st_out_ref, o_ref, st, kk_s, b_s):
        c = pl.program_id(0)

        @pl.when(c == 0)
        def _():
            st[...] = jnp.zeros_like(st)

        st_out_ref[...] = st[...]
        sig, fg, kk_all, sq, qf_all = _hg_gates(q_ref[...], f_ref[...], lb_ref[...])
        b_all = _hg_cumsum(jnp.log(fg))
        kk_s[...] = kk_all
        b_s[...] = b_all
        rows = _hg_rows()
        r16 = lax.broadcasted_iota(I32, (SUB, LANE), 0)
        for h in range(H):
            cs = slice(h * LANE, (h + 1) * LANE)
            qf, kk, b, v, g = qf_all[:, cs], kk_all[:, cs], b_all[:, cs], v_ref[:, cs], g_ref[:, cs]
            st_in = st[h]
            diag = []
            for i in range(CHUNK // SUB):
                rs = slice(SUB * i, SUB * (i + 1))
                acc = jnp.zeros((SUB, LANE), F32)
                for j in range(SUB):
                    row = pl.ds(SUB * i + j, 1)
                    e = jnp.exp(jnp.where(r16 >= j, b[rs] - b_s[row, cs], NEG))
                    col = jnp.sum(qf[rs] * (kk_s[row, cs] * e), axis=1, keepdims=True)
                    acc = acc + col * v_ref[row, cs]
                diag.append(acc)
            poff, _ = _hg_below(qf, kk, b, rows)
            vb = v.astype(BF16)
            bl = b[CHUNK - 1:CHUNK, :]
            o = (jnp.concatenate(diag, axis=0) + _dot(poff.astype(BF16), vb, NN)
                 + _dot((qf * jnp.exp(b)).astype(BF16), st_in.astype(BF16), NT))
            st[h] = st_in * jnp.exp(bl) + _dot(vb, (kk * jnp.exp(bl - b)).astype(BF16), TN)
            o_ref[:, cs] = o
            r = lax.rsqrt(jnp.mean(o * o, axis=-1, keepdims=True) + EPS)
            out_ref[:, cs] = (o * r * nw_ref[...] * (g * _sigmoid(g))).astype(out_ref.dtype)

    def seg(off):
        return pl.BlockSpec((CHUNK, HG_W), lambda c: (c, off // HG_W))

    blk = pl.BlockSpec((CHUNK, HG_W), lambda c: (c, 0))
    full = pltpu.VMEM((CHUNK, HG_W), F32)
    return pl.pallas_call(
        body, name="hgrn_fwd", grid=(NC,),
        in_specs=[seg(OFF_Q), seg(OFF_F), seg(OFF_V), seg(OFF_G),
                  pl.BlockSpec((1, HG_W), lambda c: (0, 0)), pl.BlockSpec((1, LANE), lambda c: (0, 0))],
        out_specs=[blk, pl.BlockSpec((None, H, LANE, LANE), lambda c: (c, 0, 0, 0)), blk],
        out_shape=[jax.ShapeDtypeStruct((S, 4 * HG_W), BF16), jax.ShapeDtypeStruct((NC, H, LANE, LANE), F32),
                   jax.ShapeDtypeStruct((S, HG_W), F32)],
        scratch_shapes=[pltpu.VMEM((H, LANE, LANE), F32), full, full],
        compiler_params=_cp(("arbitrary",)),
    )(proj, proj, proj, proj, lb.reshape(1, HG_W), nw.reshape(1, LANE))


def hgrn_bwd(proj, lb, nw, states, o_pre, dbcat, dproj):
    S = proj.shape[0]
    NC = S // CHUNK
    H = HG_HEADS

    def body(q_ref, f_ref, v_ref, g_ref, lb_ref, nw_ref, st_ref, o_ref, do_ref, _,
             dp_ref, dlb_ref, dnw_ref, dst, kk_s, b_s, do_s, db_s, dkk_s, dkk_d, dv_d):
        c = pl.program_id(0)
        dq_ref, df_ref, dv_ref, dg_ref = (dp_ref.at[:, pl.ds(off, HG_W)] for off in (OFF_Q, OFF_F, OFF_V, OFF_G))

        @pl.when(c == 0)
        def _():
            dst[...] = jnp.zeros_like(dst)
            dlb_ref[...] = jnp.zeros_like(dlb_ref)
            dnw_ref[...] = jnp.zeros_like(dnw_ref)

        q_all, g_all = q_ref[...], g_ref[...]
        lbv, nwv = lb_ref[...], nw_ref[...]
        sig, fg, kk_all, sq, qf_all = _hg_gates(q_all, f_ref[...], lbv)
        b_all = _hg_cumsum(jnp.log(fg))
        o_all = o_ref[...]
        dov = do_ref[...].astype(F32)
        sg = _sigmoid(g_all)
        gsg = g_all * sg
        dnw_acc = jnp.zeros((1, LANE), F32)
        for h in range(H):
            cs = slice(h * LANE, (h + 1) * LANE)
            o = o_all[:, cs]
            r = lax.rsqrt(jnp.mean(o * o, axis=-1, keepdims=True) + EPS)
            don = dov[:, cs] * gsg[:, cs]
            dnw_acc = dnw_acc + jnp.sum(don * o * r, axis=0, keepdims=True)
            gno = don * nwv
            do_s[:, cs] = r * gno - o * (r * r * r) * jnp.mean(o * gno, axis=-1, keepdims=True)
            dg_ref[:, cs] = (dov[:, cs] * (o * r * nwv) * _dsilu(g_all[:, cs], sg[:, cs])).astype(dg_ref.dtype)
        dnw_ref[...] += jnp.broadcast_to(dnw_acc, dnw_ref.shape)
        kk_s[...] = kk_all
        b_s[...] = b_all
        rows = _hg_rows()
        r16 = lax.broadcasted_iota(I32, (SUB, LANE), 0)
        for h in range(H):
            cs = slice(h * LANE, (h + 1) * LANE)
            qf, kk, b, v = qf_all[:, cs], kk_all[:, cs], b_all[:, cs], v_ref[:, cs]
            do = do_s[:, cs]
            st_in, dstv = st_ref[h], dst[h]
            bl = b[CHUNK - 1:CHUNK, :]
            eb, ebl, el = jnp.exp(b), jnp.exp(bl - b), jnp.exp(bl)
            qe, ke = qf * eb, kk * ebl
            vb, dob, stb, dstb = v.astype(BF16), do.astype(BF16), st_in.astype(BF16), dstv.astype(BF16)
            w_ = _dot(vb, dstb, NN)
            dqf = eb * _dot(dob, stb, NN)
            dkk = ebl * w_
            dv = _dot(ke.astype(BF16), dstb, NT)
            dbl = el * jnp.sum(st_in * dstv, axis=0, keepdims=True) + jnp.sum(ke * w_, axis=0, keepdims=True)
            dst[h] = dstv * el + _dot(dob, qe.astype(BF16), TN)
            poff, parts = _hg_below(qf, kk, b, rows)
            dpoff = _dot(dob, vb, NT).astype(BF16)
            dv = dv + _dot(poff.astype(BF16), dob, TN)
            dq_blocks = [jnp.zeros((SUB, LANE), F32)]
            for i, (qi, ki, eq, ek) in enumerate(parts, start=1):
                dpi = dpoff[SUB * i:SUB * (i + 1), :]
                dq_blocks.append(_dot(dpi, ki, NN) * eq)
                dkk = dkk + _dot(dpi, qi, TN) * ek
            dqf = dqf + jnp.concatenate(dq_blocks, axis=0)
            dq_diag = []
            for i in range(CHUNK // SUB):
                rs = slice(SUB * i, SUB * (i + 1))
                acc = jnp.zeros((SUB, LANE), F32)
                for j in range(SUB):
                    row = pl.ds(SUB * i + j, 1)
                    ks = kk_s[row, cs]
                    e = jnp.exp(jnp.where(r16 >= j, b[rs] - b_s[row, cs], NEG))
                    x = jnp.sum(do[rs] * v_ref[row, cs], axis=1, keepdims=True) * e
                    acc = acc + x * ks
                    dkk_d[row, cs] = jnp.sum(x * qf[rs], axis=0, keepdims=True)
                    col = jnp.sum(qf[rs] * (ks * e), axis=1, keepdims=True)
                    dv_d[row, cs] = jnp.sum(col * do[rs], axis=0, keepdims=True)
                dq_diag.append(acc)
            dqf = dqf + jnp.concatenate(dq_diag, axis=0)
            dkk = dkk + dkk_d[:, cs]
            dv = dv + dv_d[:, cs]
            dv_ref[:, cs] = dv.astype(dv_ref.dtype)
            db = qf * dqf - kk * dkk
            db_s[:, cs] = db + jnp.where(rows == CHUNK - 1, dbl, 0.0)
            dkk_s[:, cs] = dkk
            dq_ref[:, cs] = (dqf * _dsilu(q_all[:, cs], sq[:, cs])).astype(dq_ref.dtype)
        ri = lax.broadcasted_iota(I32, (CHUNK, CHUNK), 0)
        ci = lax.broadcasted_iota(I32, (CHUNK, CHUNK), 1)
        dlogf = _dot((ci >= ri).astype(F32), db_s[...], NN, precision=lax.Precision.HIGHEST)
        dfg = dlogf / fg - dkk_s[...]
        df_ref[...] = (dfg * (1.0 - lbv) * sig * (1.0 - sig)).astype(df_ref.dtype)
        dlb_ref[...] += jnp.broadcast_to(jnp.sum(dfg * (1.0 - sig), axis=0, keepdims=True), dlb_ref.shape)

    def seg(off):
        return pl.BlockSpec((CHUNK, HG_W), lambda c: (NC - 1 - c, off // HG_W))

    blk = pl.BlockSpec((CHUNK, HG_W), lambda c: (NC - 1 - c, 0))
    full = pltpu.VMEM((CHUNK, HG_W), F32)
    dproj, dlb, dnw = pl.pallas_call(
        body, name="hgrn_bwd", grid=(NC,),
        in_specs=[seg(OFF_Q), seg(OFF_F), seg(OFF_V), seg(OFF_G),
                  pl.BlockSpec((1, HG_W), lambda c: (0, 0)), pl.BlockSpec((1, LANE), lambda c: (0, 0)),
                  pl.BlockSpec((None, H, LANE, LANE), lambda c: (NC - 1 - c, 0, 0, 0)), blk, blk, ANY_SPACE],
        out_specs=[pl.BlockSpec((CHUNK, 4 * HG_W), lambda c: (NC - 1 - c, 0)),
                   pl.BlockSpec((8, HG_W), lambda c: (0, 0)), pl.BlockSpec((8, LANE), lambda c: (0, 0))],
        out_shape=[jax.ShapeDtypeStruct(dproj.shape, dproj.dtype), jax.ShapeDtypeStruct((8, HG_W), F32),
                   jax.ShapeDtypeStruct((8, LANE), F32)],
        input_output_aliases={9: 0},
        scratch_shapes=[pltpu.VMEM((H, LANE, LANE), F32)] + [full] * 7,
        compiler_params=_cp(("arbitrary",)),
    )(proj, proj, proj, proj, lb.reshape(1, HG_W), nw.reshape(1, LANE), states, o_pre, dbcat, dproj)
    return dproj, dlb[0], dnw[0]


CV_PAD = 32
ROWS = 256


def _colblk(S, off):
    return pl.BlockSpec((S, LANE), lambda j: (0, off // LANE + j))


def cv_fwd(proj, w32, bias):
    S = proj.shape[0]
    nchunk = S // ROWS

    def body(a_ref, g_ref, w_ref, b_ref, o_ref, zpad):
        zpad[pl.ds(0, CV_PAD), :] = jnp.zeros((CV_PAD, LANE), F32)

        def glu(c, _):
            r0 = pl.multiple_of(c * ROWS, ROWS)
            zpad[pl.ds(CV_PAD + r0, ROWS), :] = a_ref[pl.ds(r0, ROWS), :] * _sigmoid(g_ref[pl.ds(r0, ROWS), :])
            return 0

        lax.fori_loop(0, nchunk, glu, 0)

        def conv(c, _):
            r0 = pl.multiple_of(c * ROWS, ROWS)
            acc = jnp.broadcast_to(b_ref[...], (ROWS, LANE))
            for j in range(CV_KERNEL):
                acc = acc + w_ref[pl.ds(j, 1), :] * zpad[pl.ds(r0 + (CV_PAD - CV_KERNEL + 1) + j, ROWS), :]
            o_ref[pl.ds(r0, ROWS), :] = acc
            return 0

        lax.fori_loop(0, nchunk, conv, 0)

    return pl.pallas_call(
        body, name="cv_fwd", grid=(BRANCH_W // LANE,),
        in_specs=[_colblk(S, OFF_CV), _colblk(S, OFF_CV + BRANCH_W),
                  pl.BlockSpec((32, LANE), lambda j: (0, j)), pl.BlockSpec((1, LANE), lambda j: (0, j))],
        out_specs=pl.BlockSpec((S, LANE), lambda j: (0, j)), out_shape=jax.ShapeDtypeStruct((S, BRANCH_W), F32),
        scratch_shapes=[pltpu.VMEM((CV_PAD + S, LANE), F32)],
        compiler_params=_cp(("parallel",)),
    )(proj, proj, w32, bias.reshape(1, BRANCH_W))


def cv_bwd(proj, w32, dzc):
    S = proj.shape[0]
    nchunk = S // ROWS

    def body(a_ref, g_ref, w_ref, dz_ref, da_ref, dg_ref, dw_ref, db_ref, zpad, dpad):
        zpad[pl.ds(0, CV_PAD), :] = jnp.zeros((CV_PAD, LANE), F32)
        dpad[pl.ds(S, CV_PAD), :] = jnp.zeros((CV_PAD, LANE), F32)
        dw_ref[...] = jnp.zeros_like(dw_ref)

        def glu(c, dsum):
            r0 = pl.multiple_of(c * ROWS, ROWS)
            zpad[pl.ds(CV_PAD + r0, ROWS), :] = a_ref[pl.ds(r0, ROWS), :] * _sigmoid(g_ref[pl.ds(r0, ROWS), :])
            d = dz_ref[pl.ds(r0, ROWS), :]
            dpad[pl.ds(r0, ROWS), :] = d
            return dsum + jnp.sum(d, axis=0, keepdims=True)

        dsum = lax.fori_loop(0, nchunk, glu, jnp.zeros((1, LANE), F32))
        db_ref[...] = jnp.broadcast_to(dsum, db_ref.shape)

        def conv(c, _):
            r0 = pl.multiple_of(c * ROWS, ROWS)
            d = dpad[pl.ds(r0, ROWS), :]
            acc = jnp.zeros((ROWS, LANE), F32)
            for j in range(CV_KERNEL):
                acc = acc + w_ref[pl.ds(j, 1), :] * dpad[pl.ds(r0 + (CV_KERNEL - 1) - j, ROWS), :]
                zs = zpad[pl.ds(r0 + (CV_PAD - CV_KERNEL + 1) + j, ROWS), :]
                dw_ref[pl.ds(j, 1), :] += jnp.sum(d * zs, axis=0, keepdims=True)
            a = a_ref[pl.ds(r0, ROWS), :]
            sg = _sigmoid(g_ref[pl.ds(r0, ROWS), :])
            da_ref[pl.ds(r0, ROWS), :] = (acc * sg).astype(da_ref.dtype)
            dg_ref[pl.ds(r0, ROWS), :] = (acc * a * sg * (1.0 - sg)).astype(dg_ref.dtype)
            return 0

        lax.fori_loop(0, nchunk, conv, 0)

    blk = pl.BlockSpec((S, LANE), lambda j: (0, j))
    da, dg, dw, db = pl.pallas_call(
        body, name="cv_bwd", grid=(BRANCH_W // LANE,),
        in_specs=[_colblk(S, OFF_CV), _colblk(S, OFF_CV + BRANCH_W), pl.BlockSpec((32, LANE), lambda j: (0, j)), blk],
        out_specs=[blk, blk, pl.BlockSpec((32, LANE), lambda j: (0, j)), pl.BlockSpec((8, LANE), lambda j: (0, j))],
        out_shape=[jax.ShapeDtypeStruct((S, BRANCH_W), BF16), jax.ShapeDtypeStruct((S, BRANCH_W), BF16),
                   jax.ShapeDtypeStruct((32, BRANCH_W), F32), jax.ShapeDtypeStruct((8, BRANCH_W), F32)],
        scratch_shapes=[pltpu.VMEM((CV_PAD + S, LANE), F32), pltpu.VMEM((S + CV_PAD, LANE), F32)],
        compiler_params=_cp(("parallel",)),
    )(proj, proj, w32, dzc)
    return da, dg, dw, db[0]


def ln_silu_fwd(z, w, b, bcat, tm=512):
    S, C = z.shape
    tm = _tile(S, tm)

    def body(z_ref, w_ref, b_ref, _, o_ref):
        zv = z_ref[...]
        mu = jnp.mean(zv, axis=-1, keepdims=True)
        zc = zv - mu
        rstd = lax.rsqrt(jnp.mean(zc * zc, axis=-1, keepdims=True) + EPS)
        y = zc * rstd * w_ref[...] + b_ref[...]
        o_ref[...] = (y * _sigmoid(y)).astype(o_ref.dtype)

    row = pl.BlockSpec((tm, C), lambda i: (i, 0))
    vec = pl.BlockSpec((1, C), lambda i: (0, 0))
    return pl.pallas_call(
        body, name="ln_silu_fwd", grid=(S // tm,), in_specs=[row, vec, vec, ANY_SPACE],
        out_specs=pl.BlockSpec((tm, C), lambda i: (i, 1)), out_shape=jax.ShapeDtypeStruct(bcat.shape, bcat.dtype),
        input_output_aliases={3: 0}, compiler_params=_cp(("parallel",)),
    )(z, w.reshape(1, C), b.reshape(1, C), bcat)


def ln_silu_bwd(z, w, b, dbcat, tm=512):
    S, C = z.shape
    tm = _tile(S, tm)

    def body(z_ref, w_ref, b_ref, do_ref, dz_ref, dw_ref, db_ref):
        zv = z_ref[...]
        wv = w_ref[...]
        mu = jnp.mean(zv, axis=-1, keepdims=True)
        zc = zv - mu
        rstd = lax.rsqrt(jnp.mean(zc * zc, axis=-1, keepdims=True) + EPS)
        xh = zc * rstd
        y = xh * wv + b_ref[...]
        dy = do_ref[...].astype(F32) * _dsilu(y, _sigmoid(y))

        @pl.when(pl.program_id(0) == 0)
        def _():
            dw_ref[...] = jnp.zeros_like(dw_ref)
            db_ref[...] = jnp.zeros_like(db_ref)

        dw_ref[...] += jnp.sum(dy * xh, axis=0, keepdims=True)
        db_ref[...] += jnp.sum(dy, axis=0, keepdims=True)
        dxh = dy * wv
        dz_ref[...] = rstd * (dxh - jnp.mean(dxh, axis=-1, keepdims=True) - xh * jnp.mean(dxh * xh, axis=-1, keepdims=True))

    row = pl.BlockSpec((tm, C), lambda i: (i, 0))
    vec = pl.BlockSpec((1, C), lambda i: (0, 0))
    dz, dw, db = pl.pallas_call(
        body, name="ln_silu_bwd", grid=(S // tm,), in_specs=[row, vec, vec, pl.BlockSpec((tm, C), lambda i: (i, 1))],
        out_specs=[row, vec, vec],
        out_shape=[jax.ShapeDtypeStruct((S, C), F32), jax.ShapeDtypeStruct((1, C), F32), jax.ShapeDtypeStruct((1, C), F32)],
        compiler_params=_cp(("arbitrary",)),
    )(z, w.reshape(1, C), b.reshape(1, C), dbcat)
    return dz, dw.reshape(C), db.reshape(C)


PL_PAD = 16


def _pool_counts(r0, win):
    t = r0 + lax.broadcasted_iota(I32, (ROWS, LANE), 0)
    return jnp.minimum(t + 1, win).astype(F32)


def pool_fwd(proj, wg, scale, bcat):
    S = proj.shape[0]
    nchunk = S // ROWS

    def body(u_ref, w_ref, s_ref, _, o_ref, upad):
        g = pl.program_id(0)
        upad[pl.ds(0, PL_PAD), :] = jnp.zeros((PL_PAD, LANE), F32)

        def fill(c, _):
            r0 = pl.multiple_of(c * ROWS, ROWS)
            upad[pl.ds(PL_PAD + r0, ROWS), :] = u_ref[pl.ds(r0, ROWS), :]
            return 0

        lax.fori_loop(0, nchunk, fill, 0)
        wb = w_ref[...].astype(BF16)
        for gi, win in enumerate(POOL_WINDOWS):
            @pl.when(g == gi)
            def _(win=win):
                def chunk(c, _):
                    r0 = pl.multiple_of(c * ROWS, ROWS)
                    u = upad[pl.ds(PL_PAD + r0, ROWS), :]
                    ws = u
                    for j in range(1, win):
                        ws = ws + upad[pl.ds(PL_PAD + r0 - j, ROWS), :]
                    pooled = ws / _pool_counts(r0, win) - u
                    o_ref[pl.ds(r0, ROWS), :] = (_dot(pooled.astype(BF16), wb, NN) * s_ref[...]).astype(o_ref.dtype)
                    return 0

                lax.fori_loop(0, nchunk, chunk, 0)

    return pl.pallas_call(
        body, name="pool_fwd", grid=(len(POOL_WINDOWS),),
        in_specs=[_colblk(S, OFF_PL), pl.BlockSpec((None, LANE, LANE), lambda j: (j, 0, 0)), pl.BlockSpec((1, LANE), lambda j: (0, j)),
                  ANY_SPACE],
        out_specs=pl.BlockSpec((S, LANE), lambda j: (0, 2 * BRANCH_W // LANE + j)),
        out_shape=jax.ShapeDtypeStruct(bcat.shape, bcat.dtype), input_output_aliases={3: 0},
        scratch_shapes=[pltpu.VMEM((PL_PAD + S, LANE), F32)],
        compiler_params=_cp(("parallel",)),
    )(proj, wg, scale.reshape(1, BRANCH_W), bcat)


def pool_bwd(proj, wg, scale, dbcat, dproj):
    S = proj.shape[0]
    nchunk = S // ROWS

    def body(u_ref, w_ref, s_ref, dy_ref, _, du_ref, dw_ref, ds_ref, upad, dpn, nd):
        g = pl.program_id(0)
        upad[pl.ds(0, PL_PAD), :] = jnp.zeros((PL_PAD, LANE), F32)
        dpn[pl.ds(S, PL_PAD), :] = jnp.zeros((PL_PAD, LANE), F32)

        def fill(c, _):
            r0 = pl.multiple_of(c * ROWS, ROWS)
            upad[pl.ds(PL_PAD + r0, ROWS), :] = u_ref[pl.ds(r0, ROWS), :]
            return 0

        lax.fori_loop(0, nchunk, fill, 0)
        wb = w_ref[...].astype(BF16)
        sv = s_ref[...]
        for gi, win in enumerate(POOL_WINDOWS):
            @pl.when(g == gi)
            def _(win=win):
                def chunk(c, carry):
                    dw, dsc = carry
                    r0 = pl.multiple_of(c * ROWS, ROWS)
                    u = upad[pl.ds(PL_PAD + r0, ROWS), :]
                    ws = u
                    for j in range(1, win):
                        ws = ws + upad[pl.ds(PL_PAD + r0 - j, ROWS), :]
                    cnt = _pool_counts(r0, win)
                    pooled = (ws / cnt - u).astype(BF16)
                    dyv = dy_ref[pl.ds(r0, ROWS), :].astype(F32)
                    dsc = dsc + jnp.sum(dyv * _dot(pooled, wb, NN), axis=0, keepdims=True)
                    dys = (dyv * sv).astype(BF16)
                    dw = dw + _dot(pooled, dys, TN)
                    dp = _dot(dys, wb, NT)
                    dpn[pl.ds(r0, ROWS), :] = dp / cnt
                    nd[pl.ds(r0, ROWS), :] = -dp
                    return dw, dsc

                dw, dsc = lax.fori_loop(0, nchunk, chunk, (jnp.zeros((LANE, LANE), F32), jnp.zeros((1, LANE), F32)))
                dw_ref[...] = dw
                ds_ref[...] = jnp.broadcast_to(dsc, ds_ref.shape)

                def spread(c, _):
                    r0 = pl.multiple_of(c * ROWS, ROWS)
                    acc = nd[pl.ds(r0, ROWS), :]
                    for j in range(win):
                        acc = acc + dpn[pl.ds(r0 + j, ROWS), :]
                    du_ref[pl.ds(r0, ROWS), :] = acc.astype(du_ref.dtype)
                    return 0

                lax.fori_loop(0, nchunk, spread, 0)

    dproj, dw, ds = pl.pallas_call(
        body, name="pool_bwd", grid=(len(POOL_WINDOWS),),
        in_specs=[_colblk(S, OFF_PL), pl.BlockSpec((None, LANE, LANE), lambda j: (j, 0, 0)), pl.BlockSpec((1, LANE), lambda j: (0, j)),
                  pl.BlockSpec((S, LANE), lambda j: (0, 2 * BRANCH_W // LANE + j)), ANY_SPACE],
        out_specs=[_colblk(S, OFF_PL), pl.BlockSpec((None, LANE, LANE), lambda j: (j, 0, 0)), pl.BlockSpec((8, LANE), lambda j: (0, j))],
        out_shape=[jax.ShapeDtypeStruct(dproj.shape, dproj.dtype), jax.ShapeDtypeStruct((len(POOL_WINDOWS), LANE, LANE), F32),
                   jax.ShapeDtypeStruct((8, BRANCH_W), F32)],
        input_output_aliases={4: 0},
        scratch_shapes=[pltpu.VMEM((PL_PAD + S, LANE), F32), pltpu.VMEM((S + PL_PAD, LANE), F32), pltpu.VMEM((S, LANE), F32)],
        compiler_params=_cp(("parallel",)),
    )(proj, wg, scale.reshape(1, BRANCH_W), dbcat, dproj)
    return dproj, dw, ds[0]


LR_PAD = 8
SCAN_TILES = 4
GELU_C = math.sqrt(2.0 / math.pi)
GELU_A = 0.044715


def _gelu(y):
    return 0.5 * y * (1.0 + jnp.tanh(GELU_C * (y + GELU_A * y * y * y)))


def _dgelu(y):
    t = jnp.tanh(GELU_C * (y + GELU_A * y * y * y))
    return 0.5 * (1.0 + t) + 0.5 * y * (1.0 - t * t) * GELU_C * (1.0 + 3.0 * GELU_A * y * y)


def _lru_gates(xpad, r0, cw_ref, cb, wa, ba, wx, bx, sp8):
    xc = jnp.broadcast_to(cb, (ROWS, LANE))
    for j in range(LRU_CONV):
        xc = xc + cw_ref[pl.ds(j, 1), :] * xpad[pl.ds(r0 + (LR_PAD - LRU_CONV + 1) + j, ROWS), :]
    xb = xc.astype(BF16)
    r = _sigmoid(_dot(xb, wa, NN) + ba)
    ig = _sigmoid(_dot(xb, wx, NN) + bx)
    la = -sp8 * r
    a = jnp.exp(la)
    s = jnp.sqrt(-jnp.tanh(la) * (a * a + 1.0))
    return xc, r, ig, a, s


def _tile_scan(a, b, r8, up):
    for s in (1, 2, 4):
        keep = (r8 < 8 - s) if up else (r8 >= s)
        shift = 8 - s if up else s
        a_sh = jnp.where(keep, pltpu.roll(a, shift, 0), 1.0)
        b_sh = jnp.where(keep, pltpu.roll(b, shift, 0), 0.0)
        b = b + a * b_sh
        a = a * a_sh
    return a, b


def lru_fwd(proj, cw8, cb, wa_bd, ba, wx_bd, bx, sp8, bcat):
    S = proj.shape[0]
    nchunk = S // ROWS

    def body(x_ref, y_ref, cw_ref, cb_ref, wa_ref, ba_ref, wx_ref, bx_ref, sp_ref, _, o_ref, h_ref, xpad, a_s):
        xpad[pl.ds(0, LR_PAD), :] = jnp.zeros((LR_PAD, LANE), F32)

        def fill(c, _):
            r0 = pl.multiple_of(c * ROWS, ROWS)
            xpad[pl.ds(LR_PAD + r0, ROWS), :] = x_ref[pl.ds(r0, ROWS), :]
            return 0

        lax.fori_loop(0, nchunk, fill, 0)
        wa = wa_ref[...].astype(BF16)
        wx = wx_ref[...].astype(BF16)

        def gates(c, _):
            r0 = pl.multiple_of(c * ROWS, ROWS)
            xc, r, ig, a, s = _lru_gates(xpad, r0, cw_ref, cb_ref[...], wa, ba_ref[...], wx, bx_ref[...], sp_ref[...])
            a_s[pl.ds(r0, ROWS), :] = a
            h_ref[pl.ds(r0, ROWS), :] = s * (ig * xc)
            return 0

        lax.fori_loop(0, nchunk, gates, 0)

        r8 = lax.broadcasted_iota(I32, (8, LANE), 0)

        def scan(i, h):
            bases = [pl.multiple_of(i * (8 * SCAN_TILES) + 8 * j, 8) for j in range(SCAN_TILES)]
            maps = [_tile_scan(a_s[pl.ds(b, 8), :], h_ref[pl.ds(b, 8), :], r8, False) for b in bases]
            for b, (ca, cb_) in zip(bases, maps):
                out = cb_ + ca * h
                h_ref[pl.ds(b, 8), :] = out
                h = out[7:8, :]
            return h

        lax.fori_loop(0, S // (8 * SCAN_TILES), scan, jnp.zeros((1, LANE), F32))

        def gate_out(c, _):
            r0 = pl.multiple_of(c * ROWS, ROWS)
            o_ref[pl.ds(r0, ROWS), :] = (h_ref[pl.ds(r0, ROWS), :] * _gelu(y_ref[pl.ds(r0, ROWS), :])).astype(o_ref.dtype)
            return 0

        lax.fori_loop(0, nchunk, gate_out, 0)

    vec = pl.BlockSpec((1, LANE), lambda j: (0, j))
    mat = pl.BlockSpec((None, LANE, LANE), lambda j: (j, 0, 0))
    blk = pl.BlockSpec((S, LANE), lambda j: (0, j))
    return pl.pallas_call(
        body, name="lru_fwd", grid=(BRANCH_W // LANE,),
        in_specs=[_colblk(S, OFF_LX), _colblk(S, OFF_LY), pl.BlockSpec((8, LANE), lambda j: (0, j)), vec, mat, vec, mat, vec, vec,
                  ANY_SPACE],
        out_specs=[pl.BlockSpec((S, LANE), lambda j: (0, 3 * BRANCH_W // LANE + j)), blk],
        out_shape=[jax.ShapeDtypeStruct(bcat.shape, bcat.dtype), jax.ShapeDtypeStruct((S, BRANCH_W), F32)],
        input_output_aliases={9: 0},
        scratch_shapes=[pltpu.VMEM((LR_PAD + S, LANE), F32), pltpu.VMEM((S, LANE), F32)],
        compiler_params=_cp(("parallel",)),
    )(proj, proj, cw8, cb.reshape(1, -1), wa_bd, ba.reshape(1, -1), wx_bd, bx.reshape(1, -1), sp8.reshape(1, -1), bcat)


def lru_bwd(proj, cw8, cb, wa_bd, ba, wx_bd, bx, sp8, h, dbcat):
    S = proj.shape[0]
    nchunk = S // ROWS

    def body(x_ref, y_ref, cw_ref, cb_ref, wa_ref, ba_ref, wx_ref, bx_ref, sp_ref, h_ref, do_ref,
             dx_ref, dy_ref, dcw_ref, dcb_ref, dwa_ref, dba_ref, dwx_ref, dbx_ref, dsp_ref,
             xpad, a_s, g_s, hpad, dxc, xc_s, r_s, ig_s, s_s):
        xpad[pl.ds(0, LR_PAD), :] = jnp.zeros((LR_PAD, LANE), F32)
        hpad[pl.ds(0, LR_PAD), :] = jnp.zeros((LR_PAD, LANE), F32)
        dxc[pl.ds(S, LR_PAD), :] = jnp.zeros((LR_PAD, LANE), F32)
        dcw_ref[...] = jnp.zeros_like(dcw_ref)
        wa = wa_ref[...].astype(BF16)
        wx = wx_ref[...].astype(BF16)
        cbv, bav, bxv, spv = cb_ref[...], ba_ref[...], bx_ref[...], sp_ref[...]

        def fill(c, _):
            r0 = pl.multiple_of(c * ROWS, ROWS)
            xpad[pl.ds(LR_PAD + r0, ROWS), :] = x_ref[pl.ds(r0, ROWS), :]
            hv = h_ref[pl.ds(r0, ROWS), :]
            hpad[pl.ds(LR_PAD + r0, ROWS), :] = hv
            yv = y_ref[pl.ds(r0, ROWS), :]
            dov = do_ref[pl.ds(r0, ROWS), :].astype(F32)
            g_s[pl.ds(r0, ROWS), :] = dov * _gelu(yv)
            dy_ref[pl.ds(r0, ROWS), :] = (dov * hv * _dgelu(yv)).astype(dy_ref.dtype)
            return 0

        lax.fori_loop(0, nchunk, fill, 0)

        def gates(c, _):
            r0 = pl.multiple_of(c * ROWS, ROWS)
            rows = pl.ds(r0, ROWS)
            xc_s[rows, :], r_s[rows, :], ig_s[rows, :], a_s[rows, :], s_s[rows, :] = _lru_gates(
                xpad, r0, cw_ref, cbv, wa, bav, wx, bxv, spv)
            return 0

        lax.fori_loop(0, nchunk, gates, 0)

        r8 = lax.broadcasted_iota(I32, (8, LANE), 0)

        def rscan(i, carry):
            bases = [pl.multiple_of(S - 8 - i * (8 * SCAN_TILES) - 8 * j, 8) for j in range(SCAN_TILES)]
            firsts, maps = [], []
            for b in bases:
                a8 = a_s[pl.ds(b, 8), :]
                above = jnp.where(r8 < 7, pltpu.roll(a8, 7, 0), 1.0)
                firsts.append(a8[0:1, :])
                maps.append(_tile_scan(above, g_s[pl.ds(b, 8), :], r8, True))
            for b, a0, (ca, cb_) in zip(bases, firsts, maps):
                out = cb_ + ca * carry
                g_s[pl.ds(b, 8), :] = out
                carry = a0 * out[0:1, :]
            return carry

        lax.fori_loop(0, S // (8 * SCAN_TILES), rscan, jnp.zeros((1, LANE), F32))

        def chain(c, carry):
            dwa, dwx, dba, dbx, dsp, dcb = carry
            r0 = pl.multiple_of(c * ROWS, ROWS)
            rows = pl.ds(r0, ROWS)
            xc, r, ig, a, s = xc_s[rows, :], r_s[rows, :], ig_s[rows, :], a_s[rows, :], s_s[rows, :]
            gt = g_s[rows, :]
            hprev = hpad[pl.ds(r0 + LR_PAD - 1, ROWS), :]
            da = gt * hprev - gt * ig * xc * (a / s)
            dig = gt * s * xc
            dla = da * a
            dsp = dsp + jnp.sum(-dla * r, axis=0, keepdims=True)
            dpr = (-dla * spv) * r * (1.0 - r)
            dpi = dig * ig * (1.0 - ig)
            dprb, dpib, xb = dpr.astype(BF16), dpi.astype(BF16), xc.astype(BF16)
            d = gt * s * ig + _dot(dprb, wa, NT) + _dot(dpib, wx, NT)
            dwa = dwa + _dot(xb, dprb, TN)
            dwx = dwx + _dot(xb, dpib, TN)
            dba = dba + jnp.sum(dpr, axis=0, keepdims=True)
            dbx = dbx + jnp.sum(dpi, axis=0, keepdims=True)
            dcb = dcb + jnp.sum(d, axis=0, keepdims=True)
            dxc[pl.ds(r0, ROWS), :] = d
            for j in range(LRU_CONV):
                xs = xpad[pl.ds(r0 + (LR_PAD - LRU_CONV + 1) + j, ROWS), :]
                dcw_ref[pl.ds(j, 1), :] += jnp.sum(d * xs, axis=0, keepdims=True)
            return dwa, dwx, dba, dbx, dsp, dcb

        zm, zv = jnp.zeros((LANE, LANE), F32), jnp.zeros((1, LANE), F32)
        dwa, dwx, dba, dbx, dsp, dcb = lax.fori_loop(0, nchunk, chain, (zm, zm, zv, zv, zv, zv))
        dwa_ref[...] = dwa
        dwx_ref[...] = dwx
        dba_ref[...] = jnp.broadcast_to(dba, dba_ref.shape)
        dbx_ref[...] = jnp.broadcast_to(dbx, dbx_ref.shape)
        dsp_ref[...] = jnp.broadcast_to(dsp, dsp_ref.shape)
        dcb_ref[...] = jnp.broadcast_to(dcb, dcb_ref.shape)

        def convt(c, _):
            r0 = pl.multiple_of(c * ROWS, ROWS)
            acc = jnp.zeros((ROWS, LANE), F32)
            for j in range(LRU_CONV):
                acc = acc + cw_ref[pl.ds(j, 1), :] * dxc[pl.ds(r0 + (LRU_CONV - 1) - j, ROWS), :]
            dx_ref[pl.ds(r0, ROWS), :] = acc.astype(dx_ref.dtype)
            return 0

        lax.fori_loop(0, nchunk, convt, 0)

    vec = pl.BlockSpec((1, LANE), lambda j: (0, j))
    vec8 = pl.BlockSpec((8, LANE), lambda j: (0, j))
    mat = pl.BlockSpec((None, LANE, LANE), lambda j: (j, 0, 0))
    blk = pl.BlockSpec((S, LANE), lambda j: (0, j))
    nblk = BRANCH_W // LANE
    v8 = jax.ShapeDtypeStruct((8, BRANCH_W), F32)
    m4 = jax.ShapeDtypeStruct((nblk, LANE, LANE), F32)
    big = jax.ShapeDtypeStruct((S, BRANCH_W), BF16)
    seq = pltpu.VMEM((S, LANE), F32)
    dx, dy, dcw, dcb, dwa, dba, dwx, dbx, dsp = pl.pallas_call(
        body, name="lru_bwd", grid=(nblk,),
        in_specs=[_colblk(S, OFF_LX), _colblk(S, OFF_LY), vec8, vec, mat, vec, mat, vec, vec, blk,
                  pl.BlockSpec((S, LANE), lambda j: (0, 3 * BRANCH_W // LANE + j))],
        out_specs=[blk, blk, vec8, vec8, mat, vec8, mat, vec8, vec8],
        out_shape=[big, big, v8, v8, m4, v8, m4, v8, v8],
        scratch_shapes=[pltpu.VMEM((LR_PAD + S, LANE), F32), seq, seq, pltpu.VMEM((LR_PAD + S, LANE), F32),
                        pltpu.VMEM((S + LR_PAD, LANE), F32), seq, seq, seq, seq],
        compiler_params=_cp(("parallel",)),
    )(proj, proj, cw8, cb.reshape(1, -1), wa_bd, ba.reshape(1, -1), wx_bd, bx.reshape(1, -1), sp8.reshape(1, -1), h, dbcat)
    return dx, dy, dcw, dcb[0], dwa, dba[0], dwx, dbx[0], dsp[0]


MG_COLS = 512
N_BRANCH = 4
BCAT_W = N_BRANCH * BRANCH_W


def merge_fwd(bcat, wb, proj, gate_b, tm=2048, after=()):
    S = proj.shape[0]
    tm = _tile(S, tm)
    halves = D_MODEL // MG_COLS

    def body(a_ref, w_ref, g_ref, gb_ref, *rest):
        up_ref, sg_ref, o_ref, acc_ref = rest[len(after):]
        k = pl.program_id(2)
        up = _dot(a_ref[...], w_ref[...], NN)
        up_ref[...] = up.astype(up_ref.dtype)
        sg = _sigmoid(g_ref[...] + gb_ref[pl.ds(k, 1), :])
        sg_ref[...] = sg.astype(sg_ref.dtype)
        term = sg * up

        @pl.when(k == 0)
        def _():
            acc_ref[...] = term

        @pl.when(k > 0)
        def _():
            acc_ref[...] += term

        @pl.when(k == N_BRANCH - 1)
        def _():
            o_ref[...] = acc_ref[...].astype(o_ref.dtype)

    return pl.pallas_call(
        body, name="merge_fwd", grid=(S // tm, halves, N_BRANCH),
        in_specs=[pl.BlockSpec((tm, BRANCH_W), lambda i, j, k: (i, k)),
                  pl.BlockSpec((None, BRANCH_W, MG_COLS), lambda i, j, k: (k, 0, j)),
                  pl.BlockSpec((tm, MG_COLS), lambda i, j, k: (i, OFF_GATE // MG_COLS + k * halves + j)),
                  pl.BlockSpec((N_BRANCH, MG_COLS), lambda i, j, k: (0, j))] + [ANY_SPACE] * len(after),
        out_specs=[pl.BlockSpec((tm, MG_COLS), lambda i, j, k: (i, k * halves + j)),
                   pl.BlockSpec((tm, MG_COLS), lambda i, j, k: (i, k * halves + j)),
                   pl.BlockSpec((tm, MG_COLS), lambda i, j, k: (i, j))],
        out_shape=[jax.ShapeDtypeStruct((S, N_BRANCH * D_MODEL), BF16), jax.ShapeDtypeStruct((S, N_BRANCH * D_MODEL), BF16),
                   jax.ShapeDtypeStruct((S, D_MODEL), BF16)],
        scratch_shapes=[pltpu.VMEM((tm, MG_COLS), F32)],
        compiler_params=_cp(("parallel", "parallel", "arbitrary")),
    )(bcat, wb, proj, gate_b, *after)


def merge_bwd(dmerged, ups, gates, tm=1024):
    S = dmerged.shape[0]
    tm = _tile(S, tm)
    halves = D_MODEL // MG_COLS

    def body(dm_ref, u_ref, sg_ref, du_ref, dg_ref, dgb_ref):
        @pl.when(pl.program_id(2) == 0)
        def _():
            dgb_ref[...] = jnp.zeros_like(dgb_ref)

        dm = dm_ref[...].astype(F32)
        sg = sg_ref[...].astype(F32)
        du_ref[...] = (dm * sg).astype(du_ref.dtype)
        dgk = dm * u_ref[...] * sg * (1.0 - sg)
        dg_ref[...] = dgk.astype(dg_ref.dtype)
        dgb_ref[...] += jnp.broadcast_to(jnp.sum(dgk, axis=0, keepdims=True), dgb_ref.shape)

    dups, dproj, dgb = pl.pallas_call(
        body, name="merge_bwd", grid=(N_BRANCH, halves, S // tm),
        in_specs=[pl.BlockSpec((tm, MG_COLS), lambda k, j, i: (i, j)),
                  pl.BlockSpec((tm, MG_COLS), lambda k, j, i: (i, k * halves + j)),
                  pl.BlockSpec((tm, MG_COLS), lambda k, j, i: (i, k * halves + j))],
        out_specs=[pl.BlockSpec((tm, MG_COLS), lambda k, j, i: (i, k * halves + j)),
                   pl.BlockSpec((tm, MG_COLS), lambda k, j, i: (i, OFF_GATE // MG_COLS + k * halves + j)),
                   pl.BlockSpec((8, MG_COLS), lambda k, j, i: (k, j))],
        out_shape=[jax.ShapeDtypeStruct((S, N_BRANCH * D_MODEL), BF16), jax.ShapeDtypeStruct((S, IN_W), BF16),
                   jax.ShapeDtypeStruct((8 * N_BRANCH, D_MODEL), F32)],
        compiler_params=_cp(("parallel", "parallel", "arbitrary")),
    )(dmerged, ups, gates)
    return dups, dproj, dgb.reshape(N_BRANCH, 8, D_MODEL)[:, 0]


def attn_fwd(q, kv, tm=512):
    S = q.shape[0]
    M = kv.shape[0]
    tm = _tile(S, tm)
    scale = XA_HD ** -0.5

    def body(q_ref, kv_ref, o_ref):
        for hh in range(XA_HEADS):
            cs = pl.ds(hh * XA_HD, XA_HD)
            qh = q_ref[:, cs]
            kh = kv_ref[:, cs]
            vh = kv_ref[:, pl.ds(D_MODEL + hh * XA_HD, XA_HD)]
            s = _dot(qh, kh, NT) * scale
            p = jnp.exp(s - jnp.max(s, axis=-1, keepdims=True))
            p = p / jnp.sum(p, axis=-1, keepdims=True)
            o_ref[:, cs] = _dot(p.astype(BF16), vh, NN).astype(o_ref.dtype)

    return pl.pallas_call(
        body, name="attn_fwd", grid=(S // tm,),
        in_specs=[pl.BlockSpec((tm, D_MODEL), lambda i: (i, 0)), pl.BlockSpec((M, 2 * D_MODEL), lambda i: (0, 0))],
        out_specs=pl.BlockSpec((tm, D_MODEL), lambda i: (i, 0)), out_shape=jax.ShapeDtypeStruct((S, D_MODEL), BF16),
        compiler_params=_cp(("parallel",)),
    )(q, kv)


def attn_bwd(q, kv, do, tm=512):
    S = q.shape[0]
    M = kv.shape[0]
    tm = _tile(S, tm)
    scale = XA_HD ** -0.5

    def body(q_ref, kv_ref, do_ref, dq_ref, dkv_ref):
        @pl.when(pl.program_id(0) == 0)
        def _():
            dkv_ref[...] = jnp.zeros_like(dkv_ref)

        for hh in range(XA_HEADS):
            cs = pl.ds(hh * XA_HD, XA_HD)
            vs = pl.ds(D_MODEL + hh * XA_HD, XA_HD)
            qh = q_ref[:, cs]
            kh = kv_ref[:, cs]
            vh = kv_ref[:, vs]
            doh = do_ref[:, cs]
            s = _dot(qh, kh, NT) * scale
            p = jnp.exp(s - jnp.max(s, axis=-1, keepdims=True))
            p = p / jnp.sum(p, axis=-1, keepdims=True)
            dp = _dot(doh, vh, NT)
            ds = (p * (dp - jnp.sum(dp * p, axis=-1, keepdims=True)) * scale).astype(BF16)
            dq_ref[:, cs] = _dot(ds, kh, NN).astype(dq_ref.dtype)
            dkv_ref[:, cs] += _dot(ds, qh, TN)
            dkv_ref[:, vs] += _dot(p.astype(BF16), doh, TN)

    row = pl.BlockSpec((tm, D_MODEL), lambda i: (i, 0))
    full = pl.BlockSpec((M, 2 * D_MODEL), lambda i: (0, 0))
    return pl.pallas_call(
        body, name="attn_bwd", grid=(S // tm,), in_specs=[row, full, row], out_specs=[row, full],
        out_shape=[jax.ShapeDtypeStruct((S, D_MODEL), BF16), jax.ShapeDtypeStruct((M, 2 * D_MODEL), F32)],
        compiler_params=_cp(("arbitrary",)),
    )(q, kv, do)


def sum_parts(parts, own=None, tm=512):
    n, R, C = parts.shape
    tm = next(t for t in range(min(R, tm) // 8 * 8, 0, -8) if R % t == 0)
    has_own = own is not None

    def body(*refs):
        p_ref, o_ref = refs[0], refs[-1]
        acc = refs[1][...].astype(F32) if has_own else p_ref[0].astype(F32)
        for j in range(0 if has_own else 1, n):
            acc = acc + p_ref[j].astype(F32)
        o_ref[...] = acc

    row = pl.BlockSpec((tm, C), lambda i: (i, 0))
    return pl.pallas_call(
        body, name="sum_parts", grid=(R // tm,),
        in_specs=[pl.BlockSpec((n, tm, C), lambda i: (0, i, 0))] + ([row] if has_own else []), out_specs=row,
        out_shape=jax.ShapeDtypeStruct((R, C), F32), compiler_params=_cp(("parallel",)),
    )(*([parts, own] if has_own else [parts]))


def adamw(w, g, m, v, tm=256):
    R, C = w.shape
    tm = _tile(R, tm)
    c1 = 1.0 / (1.0 - ADAM_B1 ** ADAM_STEP)
    c2 = 1.0 / (1.0 - ADAM_B2 ** ADAM_STEP)

    def body(w_ref, g_ref, m_ref, v_ref, d_ref, nm_ref, nv_ref):
        gv = g_ref[...]
        nm = ADAM_B1 * m_ref[...] + (1.0 - ADAM_B1) * gv
        nv = ADAM_B2 * v_ref[...] + (1.0 - ADAM_B2) * (gv * gv)
        nm_ref[...] = nm
        nv_ref[...] = nv
        d_ref[...] = -ADAM_LR * ((nm * c1) / (jnp.sqrt(nv * c2) + ADAM_EPS) + ADAM_WD * w_ref[...])

    blk = pl.BlockSpec((tm, C), lambda i: (i, 0))
    sd = jax.ShapeDtypeStruct((R, C), F32)
    return pl.pallas_call(
        body, name="adamw", grid=(R // tm,), in_specs=[blk] * 4, out_specs=[blk] * 3, out_shape=[sd] * 3,
        compiler_params=_cp(("parallel",)),
    )(w, g, m, v)


ANY = pl.BlockSpec(memory_space=pl.ANY)


def _place():
    return lax.axis_index("x"), lax.axis_index("y"), lax.axis_index("c")


def _slot(px, py, pc):
    return 4 * px + 2 * py + pc


def all_gather(name, shards, after=()):
    n = len(shards)
    n_in = n + len(after)

    def body(*refs):
        x_refs, out_refs = refs[:n], refs[n_in:n_in + n]
        send_sems, recv_sems, local_sems = refs[n_in + n:]
        x, y, c = _place()
        me, sibling = (x, y, c), (x, y, 1 - c)
        chips = [(1 - x, y), (x, 1 - y), (1 - x, 1 - y)]

        def copy(a, k, block, to, src=None):
            rows = out_refs[a].at[_slot(*block)]
            return pltpu.make_async_remote_copy(
                src_ref=rows if src is None else src, dst_ref=rows,
                send_sem=send_sems.at[7 * a + k], recv_sem=recv_sems.at[7 * a + k],
                device_id=to, device_id_type=MESH)

        mine = [pltpu.make_async_copy(x_refs[a], out_refs[a].at[_slot(*me)], local_sems.at[a]) for a in range(n)]
        for cp in mine:
            cp.start()
        first = []
        for a in range(n):
            first.append(copy(a, 0, me, sibling, src=x_refs[a]))
            first += [copy(a, 1 + j, me, (*chip, c), src=x_refs[a]) for j, chip in enumerate(chips)]
        for cp in first:
            cp.start()
        passed = []
        for a in range(n):
            for j, chip in enumerate(chips):
                copy(a, 1 + j, (*chip, c), me).wait_recv()
                cp = copy(a, 4 + j, (*chip, c), sibling)
                cp.start()
                passed.append(cp)
        for a in range(n):
            copy(a, 0, sibling, me).wait_recv()
            for j, chip in enumerate(chips):
                copy(a, 4 + j, (*chip, 1 - c), me).wait_recv()
        for cp in first + passed:
            cp.wait_send()
        for cp in mine:
            cp.wait()

    return pl.pallas_call(
        body, name=name, in_specs=[ANY] * n_in, out_specs=[ANY] * n,
        out_shape=[jax.ShapeDtypeStruct((N_DEV, *s.shape), s.dtype) for s in shards],
        scratch_shapes=[pltpu.SemaphoreType.DMA((7 * n,)), pltpu.SemaphoreType.DMA((7 * n,)), pltpu.SemaphoreType.DMA((n,))],
    )(*shards, *after)


HBM = pl.BlockSpec(memory_space=pltpu.HBM)
SEM = pl.BlockSpec(memory_space=pltpu.SEMAPHORE)
EFFECT = pltpu.SideEffectType.DATAFLOW_SIDE_EFFECTING
N_PEER = N_DEV - 1
RELATIONS = [(dx, dy, dc) for dx in (0, 1) for dy in (0, 1) for dc in (0, 1)][1:]


def _peer(place, rel):
    return tuple(1 - v if d else v for v, d in zip(place, rel))


def gather_start(name, shards, me, before):
    n = len(shards)

    def body(*refs):
        x_refs, land_refs = refs[:n], refs[n:2 * n]
        send_sems, recv_sems = refs[2 * n + len(before):2 * n + len(before) + 2]
        token = refs[-1]
        place = _place()
        mine = _slot(*place)
        for a in range(n):
            for rel in RELATIONS:
                pltpu.make_async_remote_copy(
                    src_ref=x_refs[a], dst_ref=land_refs[a].at[mine], send_sem=send_sems.at[a], recv_sem=recv_sems.at[a],
                    device_id=_peer(place, rel), device_id_type=MESH).start()
        token[...] = jnp.zeros_like(token)

    lands = [lax.dynamic_update_index_in_dim(lax.empty((N_DEV, *s.shape), s.dtype), s, me, 0) for s in shards]
    outs = pl.pallas_call(
        body, name=name,
        in_specs=[HBM] * (2 * n) + [ANY] * len(before),
        out_specs=[SEM, SEM] + [HBM] * (2 * n) + [pl.BlockSpec(memory_space=pltpu.VMEM)],
        out_shape=[pltpu.SemaphoreType.DMA((n,)), pltpu.SemaphoreType.DMA((n,))]
        + [pltpu.HBM(t.shape, t.dtype) for t in (*shards, *lands)] + [jax.ShapeDtypeStruct((8, LANE), F32)],
        input_output_aliases={i: 2 + i for i in range(2 * n)},
        compiler_params=pltpu.CompilerParams(has_side_effects=EFFECT),
    )(*[pltpu.with_memory_space_constraint(t, pltpu.HBM) for t in (*shards, *lands)], *before)
    return (outs[0], outs[1], outs[2:2 + n], outs[2 + n:2 + 2 * n]), outs[-1]


def gather_wait(name, state, after):
    send_sems, recv_sems, shards, lands = state
    n = len(shards)

    def body(*refs):
        land_refs = refs[n:2 * n]
        s_sems, r_sems = refs[2 * n:2 * n + 2]
        place = _place()
        for a in range(n):
            seven = land_refs[a].at[pl.ds(0, N_PEER)]
            cp = pltpu.make_async_remote_copy(
                src_ref=seven, dst_ref=seven, send_sem=s_sems.at[a], recv_sem=r_sems.at[a], device_id=place, device_id_type=MESH)
            cp.wait_send()
            cp.wait_recv()

    outs = pl.pallas_call(
        body, name=name,
        in_specs=[HBM] * (2 * n) + [SEM, SEM] + [ANY] * len(after), out_specs=[HBM] * (2 * n),
        out_shape=[pltpu.HBM(t.shape, t.dtype) for t in (*shards, *lands)],
        input_output_aliases={i: i for i in range(2 * n)},
        compiler_params=pltpu.CompilerParams(has_side_effects=EFFECT),
    )(*shards, *lands, send_sems, recv_sems, *after)
    return outs[n:]


def exchange_start(name, grads, before):
    n = len(grads)

    def body(*refs):
        g_refs, land_refs = refs[:n], refs[n:2 * n]
        send_sems, recv_sems = refs[2 * n + len(before):2 * n + len(before) + 2]
        token = refs[-1]
        place = _place()
        for a in range(n):
            for r, rel in enumerate(RELATIONS):
                p = _peer(place, rel)
                pltpu.make_async_remote_copy(
                    src_ref=g_refs[a].at[_slot(*p)], dst_ref=land_refs[a].at[r],
                    send_sem=send_sems.at[a], recv_sem=recv_sems.at[a], device_id=p, device_id_type=MESH).start()
        token[...] = jnp.zeros_like(token)

    lands = [lax.empty((N_PEER, *g.shape[1:]), g.dtype) for g in grads]
    outs = pl.pallas_call(
        body, name=name,
        in_specs=[HBM] * (2 * n) + [ANY] * len(before),
        out_specs=[SEM, SEM] + [HBM] * (2 * n) + [pl.BlockSpec(memory_space=pltpu.VMEM)],
        out_shape=[pltpu.SemaphoreType.DMA((n,)), pltpu.SemaphoreType.DMA((n,))]
        + [pltpu.HBM(g.shape, g.dtype) for g in grads] + [pltpu.HBM(t.shape, t.dtype) for t in lands]
        + [jax.ShapeDtypeStruct((8, LANE), F32)],
        input_output_aliases={i: 2 + i for i in range(2 * n)},
        compiler_params=pltpu.CompilerParams(has_side_effects=EFFECT),
    )(*[pltpu.with_memory_space_constraint(t, pltpu.HBM) for t in (*grads, *lands)], *before)
    return (outs[0], outs[1], outs[2:2 + n], outs[2 + n:2 + 2 * n]), outs[-1]


def exchange_wait(name, state, after):
    send_sems, recv_sems, grads, lands = state
    n = len(grads)

    def body(*refs):
        g_refs, land_refs = refs[:n], refs[n:2 * n]
        s_sems, r_sems = refs[2 * n:2 * n + 2]
        place = _place()
        for a in range(n):
            cp = pltpu.make_async_remote_copy(
                src_ref=g_refs[a].at[pl.ds(0, N_PEER)], dst_ref=land_refs[a],
                send_sem=s_sems.at[a], recv_sem=r_sems.at[a], device_id=place, device_id_type=MESH)
            cp.wait_send()
            cp.wait_recv()

    outs = pl.pallas_call(
        body, name=name,
        in_specs=[HBM] * (2 * n) + [SEM, SEM, ANY], out_specs=[HBM] * (2 * n),
        out_shape=[pltpu.HBM(t.shape, t.dtype) for t in (*grads, *lands)],
        input_output_aliases={i: i for i in range(2 * n)},
        compiler_params=pltpu.CompilerParams(has_side_effects=EFFECT),
    )(*grads, *lands, send_sems, recv_sems, after)
    return outs[:n], outs[n:]


WEIGHTS = ['norm_mix_w', 'w_in', 'hg_lb_raw', 'hg_norm_w', 'cv_dw_w', 'cv_dw_b', 'cv_ln_w', 'cv_ln_b', 'pl_w', 'pl_scale',
           'lru_conv_w', 'lru_conv_b', 'lru_wa', 'lru_ba', 'lru_wx', 'lru_bx', 'lru_lambda', 'gate_b', 'w_branch', 'w_out',
           'norm_mem_w', 'mem_norm_w', 'xa_wq', 'xa_wkv', 'xa_wo', 'norm_ffn_w', 'ffn_w1', 'ffn_w2', 'final_norm_w']
BIG = ('w_in', 'w_branch', 'w_out', 'xa_wq', 'xa_wkv', 'xa_wo', 'ffn_w1', 'ffn_w2')
SMALL_SHARDED = ('cv_dw_w', 'lru_conv_w', 'gate_b')
SMALL = tuple(n for n in WEIGHTS if n not in BIG and n not in SMALL_SHARDED)
PACK_ROWS = 256


def _pack(arrs):
    flat = jnp.concatenate([a.reshape(-1).astype(F32) for a in arrs])
    tile = PACK_ROWS * LANE
    padded = -(-flat.shape[0] // tile) * tile
    return jnp.pad(flat, (0, padded - flat.shape[0])).reshape(-1, LANE)


def _unpack(packed, shapes):
    flat = packed.reshape(-1)
    out, off = [], 0
    for s in shapes:
        n = math.prod(s)
        out.append(flat[off:off + n].reshape(s))
        off += n
    return out


def _gather_last(g, shard_shape):
    nd = len(shard_shape)
    full = jnp.moveaxis(g, 0, nd - 1)
    return full.reshape(*shard_shape[:-1], N_DEV * shard_shape[-1])


def _natural(blocks):
    nb, k, c = blocks.shape
    return jnp.transpose(blocks, (1, 0, 2)).reshape(k, nb * c)


def _block_diag(w):
    w2 = w.reshape(4, 2, 64, 64)
    z = jnp.zeros((4, 64, 64), w.dtype)
    return jnp.concatenate([jnp.concatenate([w2[:, 0], z], axis=2), jnp.concatenate([z, w2[:, 1]], axis=2)], axis=1)


def _block_diag_t(d):
    return jnp.stack([d[:, :64, :64], d[:, 64:, 64:]], axis=1).reshape(8, 64, 64)


def _lower_bounds(raw):
    lb = jnp.cumsum(jax.nn.softmax(raw.astype(F32), axis=0), axis=0)
    return lb - lb[0:1]


def _decay_rates(lam):
    return (LRU_C * jax.nn.softplus(-lam.astype(F32))).reshape(DEPTH, BRANCH_W)


def _relu2(acc):
    r = jnp.maximum(acc, 0.0)
    return acc, r * r


def _relu2_grad(acc, u):
    return (acc * 2.0 * jnp.maximum(u, 0.0),)


def _add(acc, e):
    return (acc + e,)


def _layer_fwd(x0, mem, p, g, rest, after=(), late=None):
    h1 = rms_fwd("rms_mix", x0, p['norm_mix_w'])
    proj = mm_nt("mm_in", h1, g['w_in'], tn=2176, after=after)[0]
    bcat, states, o_hg = hgrn_fwd(proj, p['lb'], p['hg_norm_w'])
    zc = cv_fwd(proj, p['cv_w32'], p['cv_dw_b'])
    bcat = ln_silu_fwd(zc, p['cv_ln_w'], p['cv_ln_b'], bcat)
    bcat = pool_fwd(proj, p['pl_w'], p['pl_scale'], bcat)
    bcat, hst = lru_fwd(proj, p['lru_cw8'], p['lru_conv_b'], p['wa_bd'], p['lru_ba'], p['wx_bd'], p['lru_bx'], p['sp8'], bcat)
    more, after = rest(bcat)
    g = {**g, **more}
    ups, gates, merged = merge_fwd(bcat, g['w_branch'], proj, p['gate_b'], after=after)
    x1 = mm_nn("mm_out", merged, g['w_out'], epi=_add, extras=(x0,))[0]
    if late is not None:
        g = {**g, **late(x1)}
    h2 = rms_fwd("rms_mem", x1, p['norm_mem_w'])
    q = mm_nn("mm_q", h2, g['xa_wq'], out_dtype=BF16)[0]
    memn = rms_fwd("rms_memtok", mem, p['mem_norm_w'])
    kv = mm_nn("mm_kv", memn, g['xa_wkv'], out_dtype=BF16, tn=2048)[0]
    oa = attn_fwd(q, kv)
    x2 = mm_nn("mm_o", oa, g['xa_wo'], epi=_add, extras=(x1,))[0]
    h3 = rms_fwd("rms_ffn", x2, p['norm_ffn_w'])
    u, act = mm_nn("mm_ffn1", h3, g['ffn_w1'], epi=_relu2, out_dtypes=[BF16, BF16])
    x3 = mm_nn("mm_ffn2", act, g['ffn_w2'], epi=_add, extras=(x2,))[0]
    res = dict(x0=x0, h1=h1, proj=proj, states=states, o_hg=o_hg, zc=zc, hst=hst, bcat=bcat, ups=ups, gates=gates, merged=merged,
               x1=x1, h2=h2, q=q, memn=memn, kv=kv, oa=oa, x2=x2, h3=h3, u=u, act=act)
    return x3, res, g


def _layer_bwd(dx3, dx3b, mem, p, g, r, midway, finish):
    gs, gb = {}, {}
    du = mm_nt("mm_dffn2", dx3b, g['ffn_w2'], out_dtype=BF16, epi=_relu2_grad, extras=(r['u'],))[0]
    gb['ffn_w2'] = mm_tn("mm_gw2", r['act'], dx3b).reshape(N_DEV, -1, D_MODEL)
    gb['ffn_w1'] = mm_tn_cb("mm_gw1", r['h3'], du, N_DEV)
    dh3 = mm_nt("mm_dffn1", du, g['ffn_w1'], out_dtype=BF16)[0]
    dx2, dx2b, gs['norm_ffn_w'] = rms_bwd("rmsb_ffn", r['x2'], p['norm_ffn_w'], dh3, dx3)
    doa = mm_nt("mm_do", dx2b, g['xa_wo'], out_dtype=BF16)[0]
    gb['xa_wo'] = mm_tn("mm_gwo", r['oa'], dx2b).reshape(N_DEV, -1, D_MODEL)
    dq, dkv = attn_bwd(r['q'], r['kv'], doa)
    gb['xa_wq'] = mm_tn("mm_gwq", r['h2'], dq).reshape(N_DEV, -1, D_MODEL)
    dh2 = mm_nt("mm_dq", dq, g['xa_wq'], out_dtype=BF16)[0]
    gb['xa_wkv'] = mm_tn_cb("mm_gwkv", r['memn'], dkv, N_DEV)
    dmemn = mm_nt("mm_dkv", dkv, g['xa_wkv'], out_dtype=BF16)[0]
    _, _, gs['mem_norm_w'] = rms_bwd("rmsb_memtok", mem, p['mem_norm_w'], dmemn)
    dx1, dx1b, gs['norm_mem_w'] = rms_bwd("rmsb_mem", r['x1'], p['norm_mem_w'], dh2, dx2)
    after = midway(gb, dx1)
    gb = {}
    dmerged = mm_nt("mm_dout", dx1b, g['w_out'], out_dtype=BF16, after=after)[0]
    gb['w_out'] = mm_tn("mm_gwout", r['merged'], dx1b).reshape(N_DEV, -1, D_MODEL)
    dups, dproj, gs['gate_b'] = merge_bwd(dmerged, r['ups'], r['gates'])
    gwb = mm_branch_tn("mm_gwb", r['bcat'], dups, N_BRANCH)
    gb['w_branch'] = jnp.transpose(gwb.reshape(N_BRANCH, BRANCH_W, N_DEV, -1), (2, 0, 1, 3))
    dbcat = mm_branch_nt("mm_dup", dups, g['w_branch'], tm=2048)
    dproj, gs['lb'], gs['hg_norm_w'] = hgrn_bwd(r['proj'], p['lb'], p['hg_norm_w'], r['states'], r['o_hg'], dbcat, dproj)
    dzc, gs['cv_ln_w'], gs['cv_ln_b'] = ln_silu_bwd(r['zc'], p['cv_ln_w'], p['cv_ln_b'], dbcat)
    dca, dcg, dcw, gs['cv_dw_b'] = cv_bwd(r['proj'], p['cv_w32'], dzc)
    gs['cv_dw_w'] = dcw[:CV_KERNEL]
    dproj, gs['pl_w'], gs['pl_scale'] = pool_bwd(r['proj'], p['pl_w'], p['pl_scale'], dbcat, dproj)
    dlx, dly, dlcw, gs['lru_conv_b'], dwa, gs['lru_ba'], dwx, gs['lru_bx'], gs['sp8'] = lru_bwd(
        r['proj'], p['lru_cw8'], p['lru_conv_b'], p['wa_bd'], p['lru_ba'], p['wx_bd'], p['lru_bx'], p['sp8'], r['hst'], dbcat)
    gs['lru_conv_w'] = dlcw[:LRU_CONV]
    gs['lru_wa'], gs['lru_wx'] = _block_diag_t(dwa), _block_diag_t(dwx)
    gs['lru_ba'], gs['lru_bx'] = gs['lru_ba'].reshape(8, 64), gs['lru_bx'].reshape(8, 64)
    for off, piece in ((OFF_CV, dca), (OFF_CV + BRANCH_W, dcg), (OFF_LX, dlx), (OFF_LY, dly)):
        dproj = lax.dynamic_update_slice(dproj, piece, (0, off))
    gb['w_in'] = mm_tn("mm_gwin", dproj, r['h1'], tm=2176, tk=1024).reshape(N_DEV, -1, D_MODEL)
    dh1 = mm_nn("mm_din", dproj, g['w_in'], out_dtype=BF16, tk=4352, after=finish(gb, dx1))[0]
    dx0, dx0b, gs['norm_mix_w'] = rms_bwd("rmsb_mix", r['x0'], p['norm_mix_w'], dh1, dx1)
    return dx0, dx0b, gs


def kernel(x, mem, norm_mix_w, w_in, hg_lb_raw, hg_norm_w, cv_dw_w, cv_dw_b, cv_ln_w, cv_ln_b, pl_w, pl_scale, lru_conv_w, lru_conv_b, lru_wa, lru_ba, lru_wx, lru_bx, lru_lambda, gate_b, w_branch, w_out, norm_mem_w, mem_norm_w, xa_wq, xa_wkv, xa_wo, norm_ffn_w, ffn_w1, ffn_w2, final_norm_w, loss_target, m_norm_mix_w, m_w_in, m_hg_lb_raw, m_hg_norm_w, m_cv_dw_w, m_cv_dw_b, m_cv_ln_w, m_cv_ln_b, m_pl_w, m_pl_scale, m_lru_conv_w, m_lru_conv_b, m_lru_wa, m_lru_ba, m_lru_wx, m_lru_bx, m_lru_lambda, m_gate_b, m_w_branch, m_w_out, m_norm_mem_w, m_mem_norm_w, m_xa_wq, m_xa_wkv, m_xa_wo, m_norm_ffn_w, m_ffn_w1, m_ffn_w2, m_final_norm_w, v_norm_mix_w, v_w_in, v_hg_lb_raw, v_hg_norm_w, v_cv_dw_w, v_cv_dw_b, v_cv_ln_w, v_cv_ln_b, v_pl_w, v_pl_scale, v_lru_conv_w, v_lru_conv_b, v_lru_wa, v_lru_ba, v_lru_wx, v_lru_bx, v_lru_lambda, v_gate_b, v_w_branch, v_w_out, v_norm_mem_w, v_mem_norm_w, v_xa_wq, v_xa_wkv, v_xa_wo, v_norm_ffn_w, v_ffn_w1, v_ffn_w2, v_final_norm_w):
    W = dict(zip(WEIGHTS, (norm_mix_w, w_in, hg_lb_raw, hg_norm_w, cv_dw_w, cv_dw_b, cv_ln_w, cv_ln_b, pl_w, pl_scale, lru_conv_w, lru_conv_b, lru_wa, lru_ba, lru_wx, lru_bx, lru_lambda, gate_b, w_branch, w_out, norm_mem_w, mem_norm_w, xa_wq, xa_wkv, xa_wo, norm_ffn_w, ffn_w1, ffn_w2, final_norm_w)))
    Mo = dict(zip(WEIGHTS, (m_norm_mix_w, m_w_in, m_hg_lb_raw, m_hg_norm_w, m_cv_dw_w, m_cv_dw_b, m_cv_ln_w, m_cv_ln_b, m_pl_w, m_pl_scale, m_lru_conv_w, m_lru_conv_b, m_lru_wa, m_lru_ba, m_lru_wx, m_lru_bx, m_lru_lambda, m_gate_b, m_w_branch, m_w_out, m_norm_mem_w, m_mem_norm_w, m_xa_wq, m_xa_wkv, m_xa_wo, m_norm_ffn_w, m_ffn_w1, m_ffn_w2, m_final_norm_w)))
    Vo = dict(zip(WEIGHTS, (v_norm_mix_w, v_w_in, v_hg_lb_raw, v_hg_norm_w, v_cv_dw_w, v_cv_dw_b, v_cv_ln_w, v_cv_ln_b, v_pl_w, v_pl_scale, v_lru_conv_w, v_lru_conv_b, v_lru_wa, v_lru_ba, v_lru_wx, v_lru_bx, v_lru_lambda, v_gate_b, v_w_branch, v_w_out, v_norm_mem_w, v_mem_norm_w, v_xa_wq, v_xa_wkv, v_xa_wo, v_norm_ffn_w, v_ffn_w1, v_ffn_w2, v_final_norm_w)))
    me = _slot(*_place())
    xs, mems, target = x[0], mem[0], loss_target[0]

    shard_shapes = [W[n].shape for n in SMALL_SHARDED]
    gathered = all_gather("ag_small", [_pack([W[n] for n in SMALL_SHARDED])])[0]
    parts = [jnp.stack(ps) for ps in zip(*[_unpack(gathered[d], shard_shapes) for d in range(N_DEV)])]
    full_small = {n: _gather_last(parts[i], shard_shapes[i]) for i, n in enumerate(SMALL_SHARDED)}
    lb_all, lb_vjp = jax.vjp(_lower_bounds, hg_lb_raw)
    sp8_all, sp8_vjp = jax.vjp(_decay_rates, lru_lambda)

    def layer_params(l):
        p = {n: W[n][l] for n in SMALL if n != 'final_norm_w'}
        p['lb'] = lb_all[l]
        p['sp8'] = sp8_all[l]
        p['cv_w32'] = jnp.pad(full_small['cv_dw_w'][l], ((0, 32 - CV_KERNEL), (0, 0)))
        p['lru_cw8'] = jnp.pad(full_small['lru_conv_w'][l], ((0, 8 - LRU_CONV), (0, 0)))
        p['gate_b'] = full_small['gate_b'][l]
        p['wa_bd'], p['wx_bd'] = _block_diag(lru_wa[l]), _block_diag(lru_wx[l])
        p['lru_ba'], p['lru_bx'] = lru_ba[l].reshape(-1), lru_bx[l].reshape(-1)
        return p

    def shards_of(l):
        first = [jnp.transpose(w_in[l]).astype(BF16)]
        others = [w[l].astype(BF16) for w in (w_branch, w_out, xa_wq, xa_wkv, xa_wo, ffn_w1, ffn_w2)]
        return first, others

    def start_gather(l, before):
        first, others = shards_of(l)
        state_a, tok_a = gather_start(f"ag_start{l}a", first, me, before)
        state_b, tok_b = gather_start(f"ag_start{l}b", others, me, (*before, tok_a))
        return state_a, state_b, (tok_a, tok_b)

    def first_of(o):
        return dict(w_in=o[0].reshape(IN_W, D_MODEL))

    def merge_mats(o):
        wb = jnp.transpose(o[0], (1, 2, 0, 3)).reshape(N_BRANCH, BRANCH_W, D_MODEL)
        return dict(w_branch=wb, w_out=o[1].reshape(D_MODEL, D_MODEL))

    def late_mats(o):
        return dict(xa_wq=o[0].reshape(D_MODEL, D_MODEL), xa_wkv=_natural(o[1]), xa_wo=o[2].reshape(D_MODEL, D_MODEL),
                    ffn_w1=_natural(o[3]), ffn_w2=o[4].reshape(D_FF, D_MODEL))

    def others_of(o):
        return {**merge_mats(o[:2]), **late_mats(o[2:])}

    params = [layer_params(l) for l in range(DEPTH)]
    mats, residuals = [], []
    xc = xs
    first, others = shards_of(0)
    whole = all_gather("ag_layer0", first)
    state_b, started_b = gather_start("ag_start0b", others[:2], me, (whole[0],))
    state_c, started_c = gather_start("ag_start0c", others[2:], me, (whole[0], started_b))
    gathers = {}
    for l in range(DEPTH):
        if l == 0:
            g_first = first_of(whole)
        else:
            state_a, state_b, _ = gathers.pop(l)
            g_first = first_of(gather_wait(f"ag_wait{l}a", state_a, (xc,)))

        def rest(mixed, l=l):
            more = (merge_mats if l == 0 else others_of)(gather_wait(f"ag_wait{l}b", state_b, (mixed,)))
            if l + 1 == DEPTH:
                return more, ()
            gathers[l + 1] = start_gather(l + 1, (more['w_out'],))
            return more, gathers[l + 1][2]

        if l == 0:
            xc, res, g = _layer_fwd(xc, mems, params[l], g_first, rest, after=(started_b, started_c),
                                    late=lambda x1: late_mats(gather_wait("ag_wait0c", state_c, (x1,))))
        else:
            xc, res, g = _layer_fwd(xc, mems, params[l], g_first, rest)
        mats.append(g)
        residuals.append(res)
    loss_part, dx, dxb, g_final = loss_head(xc, final_norm_w, target)
    loss = lax.psum(loss_part, ("x", "y", "c"))

    small_grads = [None] * DEPTH
    big_grads = [{} for _ in range(DEPTH)]
    pending = []

    def send(l, group, blocks, before):
        names = list(blocks)
        state, tok = exchange_start(f"rs_start{l}{group}", [blocks[n] for n in names], (before,))
        pending.append((l, group, names, state))
        return (tok,)

    def land(after):
        l, group, names, state = pending.pop(0)
        sent, landed = exchange_wait(f"rs_wait{l}{group}", state, after)
        for n, s, t in zip(names, sent, landed):
            own = lax.dynamic_index_in_dim(s, me, 0, keepdims=False).reshape(-1, s.shape[-1])
            big_grads[l][n] = sum_parts(t.reshape(N_PEER, -1, t.shape[-1]), own).reshape(t.shape[1:])

    for l in reversed(range(DEPTH)):
        dx, dxb, small_grads[l] = _layer_bwd(dx, dxb, mems, params[l], mats[l], residuals[l],
                                        lambda blocks, dx1, l=l: send(l, "a", blocks, dx1),
                                        lambda blocks, dx1, l=l: send(l, "b", blocks, dx1))
        while pending[0][0] > l:
            land(dx)

    def stacked(n):
        return jnp.stack([small_grads[l][n] for l in range(DEPTH)])

    part = {n: stacked(n) for n in SMALL if n not in ('final_norm_w', 'hg_lb_raw', 'lru_lambda')}
    part['final_norm_w'] = g_final
    part['hg_lb_raw'] = lb_vjp(stacked('lb'))[0]
    part['lru_lambda'] = sp8_vjp(stacked('sp8'))[0]
    for n in SMALL_SHARDED:
        part[n] = stacked(n)
    names = list(SMALL) + list(SMALL_SHARDED)
    full_shapes = [part[n].shape for n in names]
    packed = _pack([part[n] for n in names])
    while len(pending) > 1:
        land(dx)
    summed = [t for layer in big_grads for t in layer.values()]
    state, _ = exchange_start("rs_small_start", [packed.reshape(N_DEV, -1, LANE)], (packed, *summed))
    sent, landed = exchange_wait("rs_small_wait", state, packed)
    mine = sum_parts(landed[0], lax.dynamic_index_in_dim(sent[0], me, 0, keepdims=False))
    total = all_gather("ag_grads", [mine])[0].reshape(-1, LANE)
    while pending:
        land(total)

    G = {}
    G['w_in'] = jnp.stack([jnp.transpose(big_grads[l]['w_in']) for l in range(DEPTH)])
    for n in ('w_branch', 'w_out', 'xa_wq', 'xa_wkv', 'xa_wo', 'ffn_w1', 'ffn_w2'):
        G[n] = jnp.stack([big_grads[l][n] for l in range(DEPTH)])
    for n, t in zip(names, _unpack(total, full_shapes)):
        if n in SMALL_SHARDED:
            c = t.shape[-1] // N_DEV
            t = lax.dynamic_slice_in_dim(t, me * c, c, axis=t.ndim - 1)
        G[n] = t

    delta, new_m, new_v = {}, {}, {}
    for n in BIG:
        c = W[n].shape[-1]
        d, nm, nv = adamw(W[n].reshape(-1, c), G[n].reshape(-1, c), Mo[n].reshape(-1, c), Vo[n].reshape(-1, c))
        delta[n], new_m[n], new_v[n] = d.reshape(W[n].shape), nm.reshape(W[n].shape), nv.reshape(W[n].shape)
    shapes = [W[n].shape for n in names]
    d, nm, nv = adamw(_pack([W[n] for n in names]), _pack([G[n] for n in names]), _pack([Mo[n] for n in names]), _pack([Vo[n] for n in names]))
    for n, a, b, c in zip(names, _unpack(d, shapes), _unpack(nm, shapes), _unpack(nv, shapes)):
        delta[n], new_m[n], new_v[n] = a, b, c
    return (loss, dx[None], *[G[n] for n in WEIGHTS], *[delta[n] for n in WEIGHTS],
            *[new_m[n] for n in WEIGHTS], *[new_v[n] for n in WEIGHTS])
```
